```python
import math
import jax, jax.numpy as jnp
from jax import lax
import numpy as np

D_MODEL = 1024
BATCH = 8
SEQ = 4096
DEPTH = 1

S5_GROUP = 16
S5_WIDTH = D_MODEL // 2
S5_GROUPS = S5_WIDTH // S5_GROUP
S5_STATE = 64
S5_DT_MIN = 0.001
S5_DT_MAX = 0.1
GLA_HEADS = 4
GLA_VAL_WIDTH = D_MODEL // 2
GLA_DV = GLA_VAL_WIDTH // GLA_HEADS
GLA_DK = GLA_DV // 2
GLA_KEY_WIDTH = GLA_HEADS * GLA_DK
GLA_GATE_RANK = 16
GLA_TAU = 16.0
GLA_CHUNK = 64
D_FF = ((8 * D_MODEL // 3 + 255) // 256) * 256
EPS = 1e-6
IN_SIZES = (S5_WIDTH, GLA_KEY_WIDTH, GLA_KEY_WIDTH, GLA_VAL_WIDTH, GLA_VAL_WIDTH,
            GLA_GATE_RANK, D_MODEL, D_MODEL)
IN_COLS = sum(IN_SIZES)

kernel_name = "hybrid_s5_gla_macaron_block"


def rms_norm(x, g):
    xf = x.astype(jnp.float32)
    y = xf * lax.rsqrt(jnp.mean(xf * xf, axis=-1, keepdims=True) + EPS)
    return (y * g.astype(jnp.float32)).astype(x.dtype)


def swiglu(x, w1, w3, w2):
    return (jax.nn.silu(x @ w1) * (x @ w3)) @ w2


def _ssm_combine(e1, e2):
    ar1, ai1, br1, bi1 = e1
    ar2, ai2, br2, bi2 = e2
    return (ar1 * ar2 - ai1 * ai2,
            ar1 * ai2 + ai1 * ar2,
            ar2 * br1 - ai2 * bi1 + br2,
            ar2 * bi1 + ai2 * br1 + bi2)


def s5_mixer(u, lam_re, lam_im, log_dt, b_re, b_im, c_re, c_im, d_skip, w_glu, b_glu):
    bsz, seq, _ = u.shape
    f32 = jnp.float32
    lam_re = lam_re.astype(f32)
    lam_im = lam_im.astype(f32)
    dt = jnp.exp(log_dt.astype(f32))[:, None]
    mag = jnp.exp(lam_re * dt)
    ar = mag * jnp.cos(lam_im * dt)
    ai = mag * jnp.sin(lam_im * dt)
    den = lam_re * lam_re + lam_im * lam_im
    nr = ar - 1.0
    fr = (nr * lam_re + ai * lam_im) / den
    fi = (ai * lam_re - nr * lam_im) / den
    b_re = b_re.astype(f32)
    b_im = b_im.astype(f32)
    bbar_re = fr[:, :, None] * b_re - fi[:, :, None] * b_im
    bbar_im = fr[:, :, None] * b_im + fi[:, :, None] * b_re
    ug = u.astype(f32).reshape(bsz, seq, S5_GROUPS, S5_GROUP)
    bu_re = jnp.einsum('blgh,gph->lbgp', ug, bbar_re)
    bu_im = jnp.einsum('blgh,gph->lbgp', ug, bbar_im)
    a_re = jnp.broadcast_to(ar, (seq, 1, S5_GROUPS, S5_STATE))
    a_im = jnp.broadcast_to(ai, (seq, 1, S5_GROUPS, S5_STATE))
    _, _, xr, xi = lax.associative_scan(_ssm_combine, (a_re, a_im, bu_re, bu_im), axis=0)
    y = (jnp.einsum('ghp,lbgp->blgh', c_re.astype(f32), xr)
         - jnp.einsum('ghp,lbgp->blgh', c_im.astype(f32), xi)
         + d_skip.astype(f32) * ug)
    y = y.reshape(bsz, seq, S5_WIDTH).astype(u.dtype)
    z = jax.nn.gelu(y)
    return z * jax.nn.sigmoid(z @ w_glu + b_glu)


def gla_mixer(q, k, v, r, a_low, w_a_up, b_a_up, g_norm):
    bsz, seq, _ = q.shape
    n_chunks = seq // GLA_CHUNK
    f32 = jnp.float32
    shp_k = (bsz, n_chunks, GLA_CHUNK, GLA_HEADS, GLA_DK)
    shp_v = (bsz, n_chunks, GLA_CHUNK, GLA_HEADS, GLA_DV)
    qc = q.astype(f32).reshape(shp_k) * (GLA_DK ** -0.5)
    kc = k.astype(f32).reshape(shp_k)
    vc = v.astype(f32).reshape(shp_v)
    log_a = jax.nn.log_sigmoid((a_low @ w_a_up + b_a_up).astype(f32)) / GLA_TAU
    bcum = jnp.cumsum(log_a.reshape(shp_k), axis=2)
    b_last = bcum[:, :, -1]
    q_t = qc * jnp.exp(bcum)
    k_t = kc * jnp.exp(-bcum)
    scores = jnp.einsum('bnthd,bnshd->bnhts', q_t, k_t)
    causal = jnp.tril(jnp.ones((GLA_CHUNK, GLA_CHUNK), dtype=bool))
    scores = jnp.where(causal, scores, 0.0)
    o_intra = jnp.einsum('bnhts,bnshv->bnthv', scores, vc)
    k_end = kc * jnp.exp(b_last[:, :, None] - bcum)
    d_state = jnp.einsum('bnshd,bnshv->bnhdv', k_end, vc)
    decay = jnp.exp(b_last)

    def step(state, inp):
        dec, ds = inp
        return dec[..., None] * state + ds, state

    s0 = jnp.zeros((bsz, GLA_HEADS, GLA_DK, GLA_DV), f32)
    _, s_prev = lax.scan(step, s0, (jnp.moveaxis(decay, 1, 0), jnp.moveaxis(d_state, 1, 0)))
    s_prev = jnp.moveaxis(s_prev, 0, 1)
    o_inter = jnp.einsum('bnthd,bnhdv->bnthv', q_t, s_prev)
    o = (o_intra + o_inter).reshape(bsz, seq, GLA_HEADS, GLA_DV)
    o = o * lax.rsqrt(jnp.mean(o * o, axis=-1, keepdims=True) + EPS)
    o = o.reshape(bsz, seq, GLA_VAL_WIDTH) * g_norm.astype(f32)
    return (o * jax.nn.silu(r.astype(f32))).astype(q.dtype)


def _fwd_setup_inputs(seed: int = 0) -> dict:
    key = jax.random.key(seed)
    ks = jax.random.split(key, 32)
    f32 = jnp.float32
    L, G, P, H = DEPTH, S5_GROUPS, S5_STATE, S5_GROUP

    def nrm(k, shape, scale):
        return jax.random.normal(k, shape, f32) * scale

    def gain(k, shape):
        return 1.0 + 0.01 * jax.random.normal(k, shape, f32)

    n_idx = jnp.arange(P, dtype=f32)
    return {
        "x": nrm(ks[0], (BATCH, SEQ, D_MODEL), 1.0),
        "ffn1_norm": gain(ks[1], (L, D_MODEL)),
        "ffn1_w1": nrm(ks[2], (L, D_MODEL, D_FF), D_MODEL ** -0.5),
        "ffn1_w3": nrm(ks[3], (L, D_MODEL, D_FF), D_MODEL ** -0.5),
        "ffn1_w2": nrm(ks[4], (L, D_FF, D_MODEL), D_FF ** -0.5),
        "mix_norm": gain(ks[5], (L, D_MODEL)),
        "w_in": nrm(ks[6], (L, D_MODEL, IN_COLS), D_MODEL ** -0.5),
        "s5_lambda_re": -0.5 + 0.01 * jax.random.normal(ks[7], (L, G, P), f32),
        "s5_lambda_im": math.pi * n_idx + 0.01 * jax.random.normal(ks[8], (L, G, P), f32),
        "s5_log_dt": jax.random.uniform(ks[9], (L, G), f32, math.log(S5_DT_MIN), math.log(S5_DT_MAX)),
        "s5_b_re": nrm(ks[10], (L, G, P, H), (2.0 * H) ** -0.5),
        "s5_b_im": nrm(ks[11], (L, G, P, H), (2.0 * H) ** -0.5),
        "s5_c_re": nrm(ks[12], (L, G, H, P), (2.0 * P) ** -0.5),
        "s5_c_im": nrm(ks[13], (L, G, H, P), (2.0 * P) ** -0.5),
        "s5_d": nrm(ks[14], (L, G, H), 1.0),
        "s5_glu_w": nrm(ks[15], (L, S5_WIDTH, S5_WIDTH), S5_WIDTH ** -0.5),
        "s5_glu_b": nrm(ks[16], (L, S5_WIDTH), 0.01),
        "gla_a_up_w": nrm(ks[17], (L, GLA_GATE_RANK, GLA_KEY_WIDTH), GLA_GATE_RANK ** -0.5),
        "gla_a_up_b": nrm(ks[18], (L, GLA_KEY_WIDTH), 0.1),
        "gla_out_norm": gain(ks[19], (L, GLA_VAL_WIDTH)),
        "proj_s5": nrm(ks[20], (L, S5_WIDTH, D_MODEL), S5_WIDTH ** -0.5),
        "proj_gla": nrm(ks[21], (L, GLA_VAL_WIDTH, D_MODEL), GLA_VAL_WIDTH ** -0.5),
        "w_out": nrm(ks[22], (L, D_MODEL, D_MODEL), D_MODEL ** -0.5),
        "ffn2_norm": gain(ks[23], (L, D_MODEL)),
        "ffn2_w1": nrm(ks[24], (L, D_MODEL, D_FF), D_MODEL ** -0.5),
        "ffn2_w3": nrm(ks[25], (L, D_MODEL, D_FF), D_MODEL ** -0.5),
        "ffn2_w2": nrm(ks[26], (L, D_FF, D_MODEL), D_FF ** -0.5),
        "final_norm": gain(ks[27], (D_MODEL,)),
    }


def _fwd_reference(x, ffn1_norm, ffn1_w1, ffn1_w3, ffn1_w2, mix_norm, w_in,
              s5_lambda_re, s5_lambda_im, s5_log_dt, s5_b_re, s5_b_im, s5_c_re, s5_c_im,
              s5_d, s5_glu_w, s5_glu_b, gla_a_up_w, gla_a_up_b, gla_out_norm,
              proj_s5, proj_gla, w_out, ffn2_norm, ffn2_w1, ffn2_w3, ffn2_w2, final_norm):
    split_idx = list(np.cumsum(IN_SIZES)[:-1])
    h = x
    for l in range(DEPTH):
        h = h + 0.5 * swiglu(rms_norm(h, ffn1_norm[l]), ffn1_w1[l], ffn1_w3[l], ffn1_w2[l])
        u = rms_norm(h, mix_norm[l])
        s5_in, q, k, v, r, a_low, g_s5, g_gla = jnp.split(u @ w_in[l], split_idx, axis=-1)
        y_s5 = s5_mixer(s5_in, s5_lambda_re[l], s5_lambda_im[l], s5_log_dt[l],
                        s5_b_re[l], s5_b_im[l], s5_c_re[l], s5_c_im[l], s5_d[l],
                        s5_glu_w[l], s5_glu_b[l])
        y_gla = gla_mixer(q, k, v, r, a_low, gla_a_up_w[l], gla_a_up_b[l], gla_out_norm[l])
        merged = (jax.nn.sigmoid(g_s5) * (y_s5 @ proj_s5[l])
                  + jax.nn.sigmoid(g_gla) * (y_gla @ proj_gla[l]))
        h = h + merged @ w_out[l]
        h = h + 0.5 * swiglu(rms_norm(h, ffn2_norm[l]), ffn2_w1[l], ffn2_w3[l], ffn2_w2[l])
    return rms_norm(h, final_norm).astype(x.dtype)


import jax as _jax
import jax.numpy as _jnp

TWIN_FORMAT = 'train_step'
FWD_PARAMS = ['x', 'ffn1_norm', 'ffn1_w1', 'ffn1_w3', 'ffn1_w2', 'mix_norm', 'w_in', 's5_lambda_re', 's5_lambda_im', 's5_log_dt', 's5_b_re', 's5_b_im', 's5_c_re', 's5_c_im', 's5_d', 's5_glu_w', 's5_glu_b', 'gla_a_up_w', 'gla_a_up_b', 'gla_out_norm', 'proj_s5', 'proj_gla', 'w_out', 'ffn2_norm', 'ffn2_w1', 'ffn2_w3', 'ffn2_w2', 'final_norm']
TWIN_WEIGHTS = ['ffn1_norm', 'ffn1_w1', 'ffn1_w3', 'ffn1_w2', 'mix_norm', 'w_in', 's5_lambda_re', 's5_lambda_im', 's5_log_dt', 's5_b_re', 's5_b_im', 's5_c_re', 's5_c_im', 's5_d', 's5_glu_w', 's5_glu_b', 'gla_a_up_w', 'gla_a_up_b', 'gla_out_norm', 'proj_s5', 'proj_gla', 'w_out', 'ffn2_norm', 'ffn2_w1', 'ffn2_w3', 'ffn2_w2', 'final_norm']
TWIN_DIFF_INPUT = 'x'
TWIN_INPUTS = ['x', 'ffn1_norm', 'ffn1_w1', 'ffn1_w3', 'ffn1_w2', 'mix_norm', 'w_in', 's5_lambda_re', 's5_lambda_im', 's5_log_dt', 's5_b_re', 's5_b_im', 's5_c_re', 's5_c_im', 's5_d', 's5_glu_w', 's5_glu_b', 'gla_a_up_w', 'gla_a_up_b', 'gla_out_norm', 'proj_s5', 'proj_gla', 'w_out', 'ffn2_norm', 'ffn2_w1', 'ffn2_w3', 'ffn2_w2', 'final_norm', 'loss_target', 'm_ffn1_norm', 'm_ffn1_w1', 'm_ffn1_w3', 'm_ffn1_w2', 'm_mix_norm', 'm_w_in', 'm_s5_lambda_re', 'm_s5_lambda_im', 'm_s5_log_dt', 'm_s5_b_re', 'm_s5_b_im', 'm_s5_c_re', 'm_s5_c_im', 'm_s5_d', 'm_s5_glu_w', 'm_s5_glu_b', 'm_gla_a_up_w', 'm_gla_a_up_b', 'm_gla_out_norm', 'm_proj_s5', 'm_proj_gla', 'm_w_out', 'm_ffn2_norm', 'm_ffn2_w1', 'm_ffn2_w3', 'm_ffn2_w2', 'm_final_norm', 'v_ffn1_norm', 'v_ffn1_w1', 'v_ffn1_w3', 'v_ffn1_w2', 'v_mix_norm', 'v_w_in', 'v_s5_lambda_re', 'v_s5_lambda_im', 'v_s5_log_dt', 'v_s5_b_re', 'v_s5_b_im', 'v_s5_c_re', 'v_s5_c_im', 'v_s5_d', 'v_s5_glu_w', 'v_s5_glu_b', 'v_gla_a_up_w', 'v_gla_a_up_b', 'v_gla_out_norm', 'v_proj_s5', 'v_proj_gla', 'v_w_out', 'v_ffn2_norm', 'v_ffn2_w1', 'v_ffn2_w3', 'v_ffn2_w2', 'v_final_norm']
TWIN_OUTPUTS = ['loss', 'grad_x', 'grad_ffn1_norm', 'grad_ffn1_w1', 'grad_ffn1_w3', 'grad_ffn1_w2', 'grad_mix_norm', 'grad_w_in', 'grad_s5_lambda_re', 'grad_s5_lambda_im', 'grad_s5_log_dt', 'grad_s5_b_re', 'grad_s5_b_im', 'grad_s5_c_re', 'grad_s5_c_im', 'grad_s5_d', 'grad_s5_glu_w', 'grad_s5_glu_b', 'grad_gla_a_up_w', 'grad_gla_a_up_b', 'grad_gla_out_norm', 'grad_proj_s5', 'grad_proj_gla', 'grad_w_out', 'grad_ffn2_norm', 'grad_ffn2_w1', 'grad_ffn2_w3', 'grad_ffn2_w2', 'grad_final_norm', 'delta_ffn1_norm', 'delta_ffn1_w1', 'delta_ffn1_w3', 'delta_ffn1_w2', 'delta_mix_norm', 'delta_w_in', 'delta_s5_lambda_re', 'delta_s5_lambda_im', 'delta_s5_log_dt', 'delta_s5_b_re', 'delta_s5_b_im', 'delta_s5_c_re', 'delta_s5_c_im', 'delta_s5_d', 'delta_s5_glu_w', 'delta_s5_glu_b', 'delta_gla_a_up_w', 'delta_gla_a_up_b', 'delta_gla_out_norm', 'delta_proj_s5', 'delta_proj_gla', 'delta_w_out', 'delta_ffn2_norm', 'delta_ffn2_w1', 'delta_ffn2_w3', 'delta_ffn2_w2', 'delta_final_norm', 'new_m_ffn1_norm', 'new_m_ffn1_w1', 'new_m_ffn1_w3', 'new_m_ffn1_w2', 'new_m_mix_norm', 'new_m_w_in', 'new_m_s5_lambda_re', 'new_m_s5_lambda_im', 'new_m_s5_log_dt', 'new_m_s5_b_re', 'new_m_s5_b_im', 'new_m_s5_c_re', 'new_m_s5_c_im', 'new_m_s5_d', 'new_m_s5_glu_w', 'new_m_s5_glu_b', 'new_m_gla_a_up_w', 'new_m_gla_a_up_b', 'new_m_gla_out_norm', 'new_m_proj_s5', 'new_m_proj_gla', 'new_m_w_out', 'new_m_ffn2_norm', 'new_m_ffn2_w1', 'new_m_ffn2_w3', 'new_m_ffn2_w2', 'new_m_final_norm', 'new_v_ffn1_norm', 'new_v_ffn1_w1', 'new_v_ffn1_w3', 'new_v_ffn1_w2', 'new_v_mix_norm', 'new_v_w_in', 'new_v_s5_lambda_re', 'new_v_s5_lambda_im', 'new_v_s5_log_dt', 'new_v_s5_b_re', 'new_v_s5_b_im', 'new_v_s5_c_re', 'new_v_s5_c_im', 'new_v_s5_d', 'new_v_s5_glu_w', 'new_v_s5_glu_b', 'new_v_gla_a_up_w', 'new_v_gla_a_up_b', 'new_v_gla_out_norm', 'new_v_proj_s5', 'new_v_proj_gla', 'new_v_w_out', 'new_v_ffn2_norm', 'new_v_ffn2_w1', 'new_v_ffn2_w3', 'new_v_ffn2_w2', 'new_v_final_norm']
TWIN_LEAF_KINDS = {'loss': 'loss', 'grad_x': 'grad_x', 'grad_ffn1_norm': 'grad_w', 'grad_ffn1_w1': 'grad_w', 'grad_ffn1_w3': 'grad_w', 'grad_ffn1_w2': 'grad_w', 'grad_mix_norm': 'grad_w', 'grad_w_in': 'grad_w', 'grad_s5_lambda_re': 'grad_w', 'grad_s5_lambda_im': 'grad_w', 'grad_s5_log_dt': 'grad_w', 'grad_s5_b_re': 'grad_w', 'grad_s5_b_im': 'grad_w', 'grad_s5_c_re': 'grad_w', 'grad_s5_c_im': 'grad_w', 'grad_s5_d': 'grad_w', 'grad_s5_glu_w': 'grad_w', 'grad_s5_glu_b': 'grad_w', 'grad_gla_a_up_w': 'grad_w', 'grad_gla_a_up_b': 'grad_w', 'grad_gla_out_norm': 'grad_w', 'grad_proj_s5': 'grad_w', 'grad_proj_gla': 'grad_w', 'grad_w_out': 'grad_w', 'grad_ffn2_norm': 'grad_w', 'grad_ffn2_w1': 'grad_w', 'grad_ffn2_w3': 'grad_w', 'grad_ffn2_w2': 'grad_w', 'grad_final_norm': 'grad_w', 'delta_ffn1_norm': 'delta_w', 'delta_ffn1_w1': 'delta_w', 'delta_ffn1_w3': 'delta_w', 'delta_ffn1_w2': 'delta_w', 'delta_mix_norm': 'delta_w', 'delta_w_in': 'delta_w', 'delta_s5_lambda_re': 'delta_w', 'delta_s5_lambda_im': 'delta_w', 'delta_s5_log_dt': 'delta_w', 'delta_s5_b_re': 'delta_w', 'delta_s5_b_im': 'delta_w', 'delta_s5_c_re': 'delta_w', 'delta_s5_c_im': 'delta_w', 'delta_s5_d': 'delta_w', 'delta_s5_glu_w': 'delta_w', 'delta_s5_glu_b': 'delta_w', 'delta_gla_a_up_w': 'delta_w', 'delta_gla_a_up_b': 'delta_w', 'delta_gla_out_norm': 'delta_w', 'delta_proj_s5': 'delta_w', 'delta_proj_gla': 'delta_w', 'delta_w_out': 'delta_w', 'delta_ffn2_norm': 'delta_w', 'delta_ffn2_w1': 'delta_w', 'delta_ffn2_w3': 'delta_w', 'delta_ffn2_w2': 'delta_w', 'delta_final_norm': 'delta_w', 'new_m_ffn1_norm': 'new_m', 'new_m_ffn1_w1': 'new_m', 'new_m_ffn1_w3': 'new_m', 'new_m_ffn1_w2': 'new_m', 'new_m_mix_norm': 'new_m', 'new_m_w_in': 'new_m', 'new_m_s5_lambda_re': 'new_m', 'new_m_s5_lambda_im': 'new_m', 'new_m_s5_log_dt': 'new_m', 'new_m_s5_b_re': 'new_m', 'new_m_s5_b_im': 'new_m', 'new_m_s5_c_re': 'new_m', 'new_m_s5_c_im': 'new_m', 'new_m_s5_d': 'new_m', 'new_m_s5_glu_w': 'new_m', 'new_m_s5_glu_b': 'new_m', 'new_m_gla_a_up_w': 'new_m', 'new_m_gla_a_up_b': 'new_m', 'new_m_gla_out_norm': 'new_m', 'new_m_proj_s5': 'new_m', 'new_m_proj_gla': 'new_m', 'new_m_w_out': 'new_m', 'new_m_ffn2_norm': 'new_m', 'new_m_ffn2_w1': 'new_m', 'new_m_ffn2_w3': 'new_m', 'new_m_ffn2_w2': 'new_m', 'new_m_final_norm': 'new_m', 'new_v_ffn1_norm': 'new_v', 'new_v_ffn1_w1': 'new_v', 'new_v_ffn1_w3': 'new_v', 'new_v_ffn1_w2': 'new_v', 'new_v_mix_norm': 'new_v', 'new_v_w_in': 'new_v', 'new_v_s5_lambda_re': 'new_v', 'new_v_s5_lambda_im': 'new_v', 'new_v_s5_log_dt': 'new_v', 'new_v_s5_b_re': 'new_v', 'new_v_s5_b_im': 'new_v', 'new_v_s5_c_re': 'new_v', 'new_v_s5_c_im': 'new_v', 'new_v_s5_d': 'new_v', 'new_v_s5_glu_w': 'new_v', 'new_v_s5_glu_b': 'new_v', 'new_v_gla_a_up_w': 'new_v', 'new_v_gla_a_up_b': 'new_v', 'new_v_gla_out_norm': 'new_v', 'new_v_proj_s5': 'new_v', 'new_v_proj_gla': 'new_v', 'new_v_w_out': 'new_v', 'new_v_ffn2_norm': 'new_v', 'new_v_ffn2_w1': 'new_v', 'new_v_ffn2_w3': 'new_v', 'new_v_ffn2_w2': 'new_v', 'new_v_final_norm': 'new_v'}


def _forward(args):
    return _fwd_reference(*[args[k] for k in FWD_PARAMS])


def _output_shape():
    out = _jax.eval_shape(lambda: _forward(_fwd_setup_inputs(0)))
    return out.shape, out.dtype

N_MICROBATCH = 1
ADAM_LR = 0.001
ADAM_B1 = 0.9
ADAM_B2 = 0.999
ADAM_EPS = 1e-08
ADAM_WD = 0.01
ADAM_STEP = 10
PER_EXAMPLE_BATCH_AXIS = {'x': 0, 'loss_target': 0}
SHARED_INPUTS = []
_WEIGHT_DTYPES = {'ffn1_norm': _jnp.float32, 'ffn1_w1': _jnp.float32, 'ffn1_w3': _jnp.float32, 'ffn1_w2': _jnp.float32, 'mix_norm': _jnp.float32, 'w_in': _jnp.float32, 's5_lambda_re': _jnp.float32, 's5_lambda_im': _jnp.float32, 's5_log_dt': _jnp.float32, 's5_b_re': _jnp.float32, 's5_b_im': _jnp.float32, 's5_c_re': _jnp.float32, 's5_c_im': _jnp.float32, 's5_d': _jnp.float32, 's5_glu_w': _jnp.float32, 's5_glu_b': _jnp.float32, 'gla_a_up_w': _jnp.float32, 'gla_a_up_b': _jnp.float32, 'gla_out_norm': _jnp.float32, 'proj_s5': _jnp.float32, 'proj_gla': _jnp.float32, 'w_out': _jnp.float32, 'ffn2_norm': _jnp.float32, 'ffn2_w1': _jnp.float32, 'ffn2_w3': _jnp.float32, 'ffn2_w2': _jnp.float32, 'final_norm': _jnp.float32}
MOMENT_SCALE = {'ffn1_norm': 9.232127e-02, 'ffn1_w1': 3.704842e-02, 'ffn1_w3': 3.581759e-02, 'ffn1_w2': 5.937287e-02, 'mix_norm': 1.167838e-01, 'w_in': 5.845855e-02, 's5_lambda_re': 2.641174e-03, 's5_lambda_im': 2.290390e-03, 's5_log_dt': 1.272509e+00, 's5_b_re': 1.546994e-03, 's5_b_im': 1.546664e-03, 's5_c_re': 2.998197e-03, 's5_c_im': 3.119140e-03, 's5_d': 4.759337e-02, 's5_glu_w': 1.324131e-02, 's5_glu_b': 2.049186e-02, 'gla_a_up_w': 1.218798e-02, 'gla_a_up_b': 4.861657e-02, 'gla_out_norm': 7.622912e-02, 'proj_s5': 3.147598e-02, 'proj_gla': 5.327613e-02, 'w_out': 6.141284e-02, 'ffn2_norm': 6.639623e-02, 'ffn2_w1': 2.865660e-02, 'ffn2_w3': 2.774896e-02, 'ffn2_w2': 4.594341e-02, 'final_norm': 3.194043e+01}


def _to_microbatches(a, axis):
    t = _jnp.moveaxis(a, axis, 0)
    t = t.reshape((N_MICROBATCH, t.shape[0] // N_MICROBATCH) + t.shape[1:])
    return _jnp.moveaxis(t, 1, axis + 1)


def setup_inputs(seed: int = 0) -> dict:
    inp = _fwd_setup_inputs(seed)
    key = _jax.random.fold_in(_jax.random.key(seed), 7919)
    shape, _ = _output_shape()
    out = dict(inp)
    out["loss_target"] = _jax.random.normal(_jax.random.fold_in(key, 0), shape, _jnp.float32)
    for i, name in enumerate(TWIN_WEIGHTS):
        w = inp[name].astype(_jnp.float32)
        if MOMENT_SCALE is None:
            s = _jnp.sqrt(_jnp.mean(_jnp.square(w)) + 1e-30)
        else:
            s = MOMENT_SCALE[name]
        km, kv = _jax.random.split(_jax.random.fold_in(key, i + 1))
        out[name] = w
        out["m_" + name] = s * _jax.random.normal(km, w.shape, _jnp.float32)
        out["v_" + name] = (s * s) * _jax.random.uniform(kv, w.shape, _jnp.float32, 0.5, 1.5)
    if N_MICROBATCH > 1:
        for name, axis in PER_EXAMPLE_BATCH_AXIS.items():
            out[name] = _to_microbatches(out[name], axis)
    return {'x': out['x'], 'ffn1_norm': out['ffn1_norm'], 'ffn1_w1': out['ffn1_w1'], 'ffn1_w3': out['ffn1_w3'], 'ffn1_w2': out['ffn1_w2'], 'mix_norm': out['mix_norm'], 'w_in': out['w_in'], 's5_lambda_re': out['s5_lambda_re'], 's5_lambda_im': out['s5_lambda_im'], 's5_log_dt': out['s5_log_dt'], 's5_b_re': out['s5_b_re'], 's5_b_im': out['s5_b_im'], 's5_c_re': out['s5_c_re'], 's5_c_im': out['s5_c_im'], 's5_d': out['s5_d'], 's5_glu_w': out['s5_glu_w'], 's5_glu_b': out['s5_glu_b'], 'gla_a_up_w': out['gla_a_up_w'], 'gla_a_up_b': out['gla_a_up_b'], 'gla_out_norm': out['gla_out_norm'], 'proj_s5': out['proj_s5'], 'proj_gla': out['proj_gla'], 'w_out': out['w_out'], 'ffn2_norm': out['ffn2_norm'], 'ffn2_w1': out['ffn2_w1'], 'ffn2_w3': out['ffn2_w3'], 'ffn2_w2': out['ffn2_w2'], 'final_norm': out['final_norm'], 'loss_target': out['loss_target'], 'm_ffn1_norm': out['m_ffn1_norm'], 'm_ffn1_w1': out['m_ffn1_w1'], 'm_ffn1_w3': out['m_ffn1_w3'], 'm_ffn1_w2': out['m_ffn1_w2'], 'm_mix_norm': out['m_mix_norm'], 'm_w_in': out['m_w_in'], 'm_s5_lambda_re': out['m_s5_lambda_re'], 'm_s5_lambda_im': out['m_s5_lambda_im'], 'm_s5_log_dt': out['m_s5_log_dt'], 'm_s5_b_re': out['m_s5_b_re'], 'm_s5_b_im': out['m_s5_b_im'], 'm_s5_c_re': out['m_s5_c_re'], 'm_s5_c_im': out['m_s5_c_im'], 'm_s5_d': out['m_s5_d'], 'm_s5_glu_w': out['m_s5_glu_w'], 'm_s5_glu_b': out['m_s5_glu_b'], 'm_gla_a_up_w': out['m_gla_a_up_w'], 'm_gla_a_up_b': out['m_gla_a_up_b'], 'm_gla_out_norm': out['m_gla_out_norm'], 'm_proj_s5': out['m_proj_s5'], 'm_proj_gla': out['m_proj_gla'], 'm_w_out': out['m_w_out'], 'm_ffn2_norm': out['m_ffn2_norm'], 'm_ffn2_w1': out['m_ffn2_w1'], 'm_ffn2_w3': out['m_ffn2_w3'], 'm_ffn2_w2': out['m_ffn2_w2'], 'm_final_norm': out['m_final_norm'], 'v_ffn1_norm': out['v_ffn1_norm'], 'v_ffn1_w1': out['v_ffn1_w1'], 'v_ffn1_w3': out['v_ffn1_w3'], 'v_ffn1_w2': out['v_ffn1_w2'], 'v_mix_norm': out['v_mix_norm'], 'v_w_in': out['v_w_in'], 'v_s5_lambda_re': out['v_s5_lambda_re'], 'v_s5_lambda_im': out['v_s5_lambda_im'], 'v_s5_log_dt': out['v_s5_log_dt'], 'v_s5_b_re': out['v_s5_b_re'], 'v_s5_b_im': out['v_s5_b_im'], 'v_s5_c_re': out['v_s5_c_re'], 'v_s5_c_im': out['v_s5_c_im'], 'v_s5_d': out['v_s5_d'], 'v_s5_glu_w': out['v_s5_glu_w'], 'v_s5_glu_b': out['v_s5_glu_b'], 'v_gla_a_up_w': out['v_gla_a_up_w'], 'v_gla_a_up_b': out['v_gla_a_up_b'], 'v_gla_out_norm': out['v_gla_out_norm'], 'v_proj_s5': out['v_proj_s5'], 'v_proj_gla': out['v_proj_gla'], 'v_w_out': out['v_w_out'], 'v_ffn2_norm': out['v_ffn2_norm'], 'v_ffn2_w1': out['v_ffn2_w1'], 'v_ffn2_w3': out['v_ffn2_w3'], 'v_ffn2_w2': out['v_ffn2_w2'], 'v_final_norm': out['v_final_norm']}


def _loss(weights, diff, rest, loss_target):
    with _jax.named_scope("forward"):
        args = {**rest, TWIN_DIFF_INPUT: diff, **{k: w.astype(_WEIGHT_DTYPES[k]) for k, w in weights.items()}}
        y = _forward(args)
    with _jax.named_scope("loss_head"):
        err = _jnp.square(y.astype(_jnp.float32) - loss_target)
        return 0.5 * _jnp.sum(_jnp.mean(err, axis=-1)) if err.ndim else 0.5 * err


def _adamw(w, g, m, v):
    m = ADAM_B1 * m + (1.0 - ADAM_B1) * g
    v = ADAM_B2 * v + (1.0 - ADAM_B2) * _jnp.square(g)
    m_hat = m / (1.0 - ADAM_B1 ** ADAM_STEP)
    v_hat = v / (1.0 - ADAM_B2 ** ADAM_STEP)
    delta = -ADAM_LR * (m_hat / (_jnp.sqrt(v_hat) + ADAM_EPS) + ADAM_WD * w)
    return delta, m, v


def reference(x, ffn1_norm, ffn1_w1, ffn1_w3, ffn1_w2, mix_norm, w_in, s5_lambda_re, s5_lambda_im, s5_log_dt, s5_b_re, s5_b_im, s5_c_re, s5_c_im, s5_d, s5_glu_w, s5_glu_b, gla_a_up_w, gla_a_up_b, gla_out_norm, proj_s5, proj_gla, w_out, ffn2_norm, ffn2_w1, ffn2_w3, ffn2_w2, final_norm, loss_target, m_ffn1_norm, m_ffn1_w1, m_ffn1_w3, m_ffn1_w2, m_mix_norm, m_w_in, m_s5_lambda_re, m_s5_lambda_im, m_s5_log_dt, m_s5_b_re, m_s5_b_im, m_s5_c_re, m_s5_c_im, m_s5_d, m_s5_glu_w, m_s5_glu_b, m_gla_a_up_w, m_gla_a_up_b, m_gla_out_norm, m_proj_s5, m_proj_gla, m_w_out, m_ffn2_norm, m_ffn2_w1, m_ffn2_w3, m_ffn2_w2, m_final_norm, v_ffn1_norm, v_ffn1_w1, v_ffn1_w3, v_ffn1_w2, v_mix_norm, v_w_in, v_s5_lambda_re, v_s5_lambda_im, v_s5_log_dt, v_s5_b_re, v_s5_b_im, v_s5_c_re, v_s5_c_im, v_s5_d, v_s5_glu_w, v_s5_glu_b, v_gla_a_up_w, v_gla_a_up_b, v_gla_out_norm, v_proj_s5, v_proj_gla, v_w_out, v_ffn2_norm, v_ffn2_w1, v_ffn2_w3, v_ffn2_w2, v_final_norm):
    given = dict(x=x, ffn1_norm=ffn1_norm, ffn1_w1=ffn1_w1, ffn1_w3=ffn1_w3, ffn1_w2=ffn1_w2, mix_norm=mix_norm, w_in=w_in, s5_lambda_re=s5_lambda_re, s5_lambda_im=s5_lambda_im, s5_log_dt=s5_log_dt, s5_b_re=s5_b_re, s5_b_im=s5_b_im, s5_c_re=s5_c_re, s5_c_im=s5_c_im, s5_d=s5_d, s5_glu_w=s5_glu_w, s5_glu_b=s5_glu_b, gla_a_up_w=gla_a_up_w, gla_a_up_b=gla_a_up_b, gla_out_norm=gla_out_norm, proj_s5=proj_s5, proj_gla=proj_gla, w_out=w_out, ffn2_norm=ffn2_norm, ffn2_w1=ffn2_w1, ffn2_w3=ffn2_w3, ffn2_w2=ffn2_w2, final_norm=final_norm, loss_target=loss_target, m_ffn1_norm=m_ffn1_norm, m_ffn1_w1=m_ffn1_w1, m_ffn1_w3=m_ffn1_w3, m_ffn1_w2=m_ffn1_w2, m_mix_norm=m_mix_norm, m_w_in=m_w_in, m_s5_lambda_re=m_s5_lambda_re, m_s5_lambda_im=m_s5_lambda_im, m_s5_log_dt=m_s5_log_dt, m_s5_b_re=m_s5_b_re, m_s5_b_im=m_s5_b_im, m_s5_c_re=m_s5_c_re, m_s5_c_im=m_s5_c_im, m_s5_d=m_s5_d, m_s5_glu_w=m_s5_glu_w, m_s5_glu_b=m_s5_glu_b, m_gla_a_up_w=m_gla_a_up_w, m_gla_a_up_b=m_gla_a_up_b, m_gla_out_norm=m_gla_out_norm, m_proj_s5=m_proj_s5, m_proj_gla=m_proj_gla, m_w_out=m_w_out, m_ffn2_norm=m_ffn2_norm, m_ffn2_w1=m_ffn2_w1, m_ffn2_w3=m_ffn2_w3, m_ffn2_w2=m_ffn2_w2, m_final_norm=m_final_norm, v_ffn1_norm=v_ffn1_norm, v_ffn1_w1=v_ffn1_w1, v_ffn1_w3=v_ffn1_w3, v_ffn1_w2=v_ffn1_w2, v_mix_norm=v_mix_norm, v_w_in=v_w_in, v_s5_lambda_re=v_s5_lambda_re, v_s5_lambda_im=v_s5_lambda_im, v_s5_log_dt=v_s5_log_dt, v_s5_b_re=v_s5_b_re, v_s5_b_im=v_s5_b_im, v_s5_c_re=v_s5_c_re, v_s5_c_im=v_s5_c_im, v_s5_d=v_s5_d, v_s5_glu_w=v_s5_glu_w, v_s5_glu_b=v_s5_glu_b, v_gla_a_up_w=v_gla_a_up_w, v_gla_a_up_b=v_gla_a_up_b, v_gla_out_norm=v_gla_out_norm, v_proj_s5=v_proj_s5, v_proj_gla=v_proj_gla, v_w_out=v_w_out, v_ffn2_norm=v_ffn2_norm, v_ffn2_w1=v_ffn2_w1, v_ffn2_w3=v_ffn2_w3, v_ffn2_w2=v_ffn2_w2, v_final_norm=v_final_norm)
    weights = {n: given[n] for n in TWIN_WEIGHTS}
    shared = {n: given[n] for n in SHARED_INPUTS}
    per_example = {n: given[n] for n in ['x']}
    grad_fn = _jax.value_and_grad(_loss, argnums=(0, 1))

    def one_microbatch(ex, loss_target):
        ex = dict(ex)
        diff = ex.pop(TWIN_DIFF_INPUT)
        return grad_fn(weights, diff, {**shared, **ex}, loss_target)

    if N_MICROBATCH == 1:
        loss, (grad_w, grad_x) = one_microbatch(per_example, given["loss_target"])
    else:
        def body(carry, xs):
            loss_sum, grad_sum = carry
            l_k, (gw_k, gx_k) = one_microbatch(xs[0], xs[1])
            with _jax.named_scope("update"):
                return (loss_sum + l_k, _jax.tree.map(_jnp.add, grad_sum, gw_k)), gx_k

        init = (_jnp.zeros((), _jnp.float32), _jax.tree.map(_jnp.zeros_like, weights))
        (loss, grad_w), grad_x = _jax.lax.scan(body, init, (per_example, given["loss_target"]))
    with _jax.named_scope("update"):
        delta_w, new_m, new_v = {}, {}, {}
        for n in TWIN_WEIGHTS:
            delta_w[n], new_m[n], new_v[n] = _adamw(weights[n], grad_w[n], given["m_" + n], given["v_" + n])
    return (loss, grad_x, *[grad_w[n] for n in TWIN_WEIGHTS], *[delta_w[n] for n in TWIN_WEIGHTS],
            *[new_m[n] for n in TWIN_WEIGHTS], *[new_v[n] for n in TWIN_WEIGHTS])
```

```python
import functools
import math

import jax
import jax.numpy as jnp
from jax import lax
from jax.experimental import pallas as pl
from jax.experimental.pallas import tpu as pltpu

F32, BF16 = jnp.float32, jnp.bfloat16
HIGHEST = lax.Precision.HIGHEST

D_MODEL = 1024
D_FF = 2816
N_DEV = 8
S5_WIDTH, S5_GROUPS, S5_GROUP, S5_STATE = 512, 32, 16, 64
S5_BLOCKS = 4
S5_BSTATE = 512
S5_SEGS = 8
GLA_HEADS, GLA_DK, GLA_DV = 4, 64, 128
GLA_KEY, GLA_VAL, GLA_RANK, GLA_CHUNK = 256, 512, 16, 64
GLA_TAU = 16.0
EPS = 1e-6
IN_SIZES = (512, 256, 256, 512, 512, 16, 1024, 1024)
IN_OFFS = tuple(sum(IN_SIZES[:i]) for i in range(len(IN_SIZES)))
IN_COLS = sum(IN_SIZES)
ADAM_LR, ADAM_B1, ADAM_B2, ADAM_EPS, ADAM_WD, ADAM_STEP = 0.001, 0.9, 0.999, 1e-08, 0.01, 10
GELU_C0 = math.sqrt(2.0 / math.pi)
GELU_C1 = 0.044715

FFN_FT = 256
VMEM_LIMIT_BYTES = 56 * 1024 * 1024

VMEM_FULL = pl.BlockSpec(memory_space=pltpu.VMEM)
ANY = pl.BlockSpec(memory_space=pl.ANY)


def _cparams(n_grid):
    return pltpu.CompilerParams(dimension_semantics=("arbitrary",) * n_grid, vmem_limit_bytes=VMEM_LIMIT_BYTES)


def _tile(t):
    return 512 if t >= 1024 else t // 2


def _nn(a, b):
    return jnp.dot(a, b, preferred_element_type=F32)


def _nt(a, b):
    return lax.dot_general(a, b, (((1,), (1,)), ((), ())), preferred_element_type=F32)


def _tn(a, b):
    return lax.dot_general(a, b, (((0,), (0,)), ((), ())), preferred_element_type=F32)


def _rms_parts(x):
    r = lax.rsqrt(jnp.mean(x * x, axis=-1, keepdims=True) + EPS)
    return x * r, r


def _rms_bwd(dn, g, xhat, r):
    dxh = dn * g
    dx = r * (dxh - xhat * jnp.mean(dxh * xhat, axis=-1, keepdims=True))
    return dx, jnp.sum(dn * xhat, axis=0, keepdims=True)


def _row_tile(tm, d):
    return pl.BlockSpec((tm, d), lambda i: (i, 0))


def _acc_row(d):
    return pl.BlockSpec((1, d), lambda i: (0, 0))


def _ffn_fwd(x, g, w1t, w3t, w2, name):
    t = x.shape[0]
    tm = _tile(t)
    nf = D_FF // FFN_FT

    def body(x_ref, g_ref, w1_ref, w3_ref, w2_ref, o_ref):
        xv = x_ref[...]
        xhat, _ = _rms_parts(xv)
        n = (xhat * g_ref[...]).astype(BF16)
        o_ref[...] = xv

        def fstep(f, c):
            rows = pl.ds(pl.multiple_of(f * FFN_FT, FFN_FT), FFN_FT)
            a = _nt(n, w1_ref[rows, :])
            b = _nt(n, w3_ref[rows, :])
            s = (a * jax.nn.sigmoid(a) * b).astype(BF16)
            o_ref[...] += 0.5 * _nn(s, w2_ref[rows, :])
            return c

        lax.fori_loop(0, nf, fstep, 0)

    return pl.pallas_call(
        body, name=name, grid=(t // tm,),
        in_specs=[_row_tile(tm, D_MODEL), _acc_row(D_MODEL), VMEM_FULL, VMEM_FULL, VMEM_FULL],
        out_specs=_row_tile(tm, D_MODEL),
        out_shape=jax.ShapeDtypeStruct((t, D_MODEL), F32),
        compiler_params=_cparams(1),
    )(x, g, w1t, w3t, w2)


def _ffn_bwd(x, dh, g, w1t, w3t, w2, name):
    t = x.shape[0]
    tm = _tile(t) // 2
    nf = D_FF // FFN_FT

    def body(x_ref, dh_ref, g_ref, w1_ref, w3_ref, w2_ref,
             dx_ref, dg_ref, da_ref, db_ref, s_ref, n_ref, dhh_ref, dn_acc):
        i = pl.program_id(0)
        xv = x_ref[...]
        gv = g_ref[...]
        xhat, r = _rms_parts(xv)
        n = (xhat * gv).astype(BF16)
        n_ref[...] = n
        dhv = dh_ref[...]
        dhh = (0.5 * dhv).astype(BF16)
        dhh_ref[...] = dhh
        dn_acc[...] = jnp.zeros_like(dn_acc)

        def fstep(f, c):
            rows = pl.ds(pl.multiple_of(f * FFN_FT, FFN_FT), FFN_FT)
            w1c, w3c, w2c = w1_ref[rows, :], w3_ref[rows, :], w2_ref[rows, :]
            a = _nt(n, w1c)
            b = _nt(n, w3c)
            sg = jax.nn.sigmoid(a)
            sl = a * sg
            ds = _nt(dhh, w2c)
            da = (ds * b * sg * (1.0 + a * (1.0 - sg))).astype(BF16)
            db = (ds * sl).astype(BF16)
            s_ref[f] = (sl * b).astype(BF16)
            da_ref[f] = da
            db_ref[f] = db
            dn_acc[...] += _nn(da, w1c) + _nn(db, w3c)
            return c

        lax.fori_loop(0, nf, fstep, 0)
        dx, dg = _rms_bwd(dn_acc[...], gv, xhat, r)
        dx_ref[...] = dhv + dx

        @pl.when(i == 0)
        def _():
            dg_ref[...] = jnp.zeros_like(dg_ref)

        dg_ref[...] += dg

    blk3 = pl.BlockSpec((nf, tm, FFN_FT), lambda i: (0, i, 0))
    sh3 = jax.ShapeDtypeStruct((nf, t, FFN_FT), BF16)
    return pl.pallas_call(
        body, name=name, grid=(t // tm,),
        in_specs=[_row_tile(tm, D_MODEL), _row_tile(tm, D_MODEL), _acc_row(D_MODEL), VMEM_FULL, VMEM_FULL, VMEM_FULL],
        out_specs=[_row_tile(tm, D_MODEL), _acc_row(D_MODEL), blk3, blk3, blk3,
                   _row_tile(tm, D_MODEL), _row_tile(tm, D_MODEL)],
        out_shape=[jax.ShapeDtypeStruct((t, D_MODEL), F32), jax.ShapeDtypeStruct((1, D_MODEL), F32), sh3, sh3, sh3,
                   jax.ShapeDtypeStruct((t, D_MODEL), BF16), jax.ShapeDtypeStruct((t, D_MODEL), BF16)],
        scratch_shapes=[pltpu.VMEM((tm, D_MODEL), F32)],
        compiler_params=_cparams(1),
    )(x, dh, g, w1t, w3t, w2)


def _mm_tn(a, b, name):
    t, n = b.shape
    kc = min(512, t)
    if a.ndim == 3:
        nb, _, tb = a.shape
        a_spec = pl.BlockSpec((1, t, tb), lambda i: (i, 0, 0))
    else:
        m = a.shape[1]
        tb = min(m, 256)
        nb = m // tb
        a_spec = pl.BlockSpec((t, tb), lambda i: (0, i))
    three_d = a.ndim == 3

    def body(a_ref, b_ref, o_ref):
        o_ref[...] = jnp.zeros_like(o_ref)

        def kstep(k, c):
            rows = pl.ds(pl.multiple_of(k * kc, kc), kc)
            av = a_ref[0, rows, :] if three_d else a_ref[rows, :]
            o_ref[...] += _tn(av.astype(BF16), b_ref[rows, :])
            return c

        lax.fori_loop(0, t // kc, kstep, 0)

    return pl.pallas_call(
        body, name=name, grid=(nb,),
        in_specs=[a_spec, VMEM_FULL],
        out_specs=pl.BlockSpec((tb, n), lambda i: (i, 0)),
        out_shape=jax.ShapeDtypeStruct((nb * tb, n), F32),
        compiler_params=_cparams(1),
    )(a, b)


def _mix_pre_fwd(h, g, wint):
    t = h.shape[0]
    tm = _tile(t)

    def body(h_ref, g_ref, w_ref, u_ref, *outs):
        xhat, _ = _rms_parts(h_ref[...])
        u = (xhat * g_ref[...]).astype(BF16)
        u_ref[...] = u
        for o_ref, off, size in zip(outs, IN_OFFS, IN_SIZES):
            o_ref[...] = _nt(u, w_ref[off:off + size, :])

    return pl.pallas_call(
        body, name="mix_pre_fwd", grid=(t // tm,),
        in_specs=[_row_tile(tm, D_MODEL), _acc_row(D_MODEL), VMEM_FULL],
        out_specs=[_row_tile(tm, D_MODEL)] + [_row_tile(tm, s) for s in IN_SIZES],
        out_shape=[jax.ShapeDtypeStruct((t, D_MODEL), BF16)] + [jax.ShapeDtypeStruct((t, s), F32) for s in IN_SIZES],
        compiler_params=_cparams(1),
    )(h, g, wint)


def _mix_pre_bwd(h, g, wint, dh2, dz):
    t = h.shape[0]
    tm = _tile(t)

    def body(h_ref, g_ref, w_ref, dh2_ref, *rest):
        dz_refs, (dh1_ref, dg_ref) = rest[:len(IN_SIZES)], rest[len(IN_SIZES):]
        i = pl.program_id(0)
        gv = g_ref[...]
        xhat, r = _rms_parts(h_ref[...])
        du = jnp.zeros((tm, D_MODEL), F32)
        for dz_ref, off, size in zip(dz_refs, IN_OFFS, IN_SIZES):
            du = du + _nn(dz_ref[...].astype(BF16), w_ref[off:off + size, :])
        dx, dg = _rms_bwd(du, gv, xhat, r)
        dh1_ref[...] = dh2_ref[...] + dx

        @pl.when(i == 0)
        def _():
            dg_ref[...] = jnp.zeros_like(dg_ref)

        dg_ref[...] += dg

    return pl.pallas_call(
        body, name="mix_pre_bwd", grid=(t // tm,),
        in_specs=[_row_tile(tm, D_MODEL), _acc_row(D_MODEL), VMEM_FULL, _row_tile(tm, D_MODEL)]
        + [_row_tile(tm, s) for s in IN_SIZES],
        out_specs=[_row_tile(tm, D_MODEL), _acc_row(D_MODEL)],
        out_shape=[jax.ShapeDtypeStruct((t, D_MODEL), F32), jax.ShapeDtypeStruct((1, D_MODEL), F32)],
        compiler_params=_cparams(1),
    )(h, g, wint, dh2, *dz)


def _disc_math(lre, lim, ldt, bre, bim):
    dt = jnp.exp(ldt)
    mag = jnp.exp(lre * dt)
    ar = mag * jnp.cos(lim * dt)
    ai = mag * jnp.sin(lim * dt)
    den = lre * lre + lim * lim
    nr = ar - 1.0
    fr = (nr * lre + ai * lim) / den
    fi = (ai * lre - nr * lim) / den
    return ar, ai, fr[None] * bre - fi[None] * bim, fr[None] * bim + fi[None] * bre


def _s5_disc(lre, lim, ldt, bre, bim):
    def body(lre_ref, lim_ref, ldt_ref, bre_ref, bim_ref, ar_ref, ai_ref, bbr_ref, bbi_ref):
        ar, ai, bbr, bbi = _disc_math(lre_ref[...], lim_ref[...], ldt_ref[...], bre_ref[...], bim_ref[...])
        ar_ref[...] = ar
        ai_ref[...] = ai
        bbr_ref[...] = bbr
        bbi_ref[...] = bbi

    small = jax.ShapeDtypeStruct(lre.shape, F32)
    big = jax.ShapeDtypeStruct(bre.shape, F32)
    return pl.pallas_call(body, name="s5_disc", out_shape=[small, small, big, big],
                          in_specs=[VMEM_FULL] * 5, out_specs=[VMEM_FULL] * 4)(lre, lim, ldt, bre, bim)


def _s5_disc_bwd(lre, lim, ldt, bre, bim, dar, dai, dbbr, dbbi):
    def body(lre_ref, lim_ref, ldt_ref, bre_ref, bim_ref, dar_ref, dai_ref, dbbr_ref, dbbi_ref,
             glre_ref, glim_ref, gldt_ref, gbre_ref, gbim_ref):
        _, vjp = jax.vjp(_disc_math, lre_ref[...], lim_ref[...], ldt_ref[...], bre_ref[...], bim_ref[...])
        glre, glim, gldt, gbre, gbim = vjp((dar_ref[...], dai_ref[...], dbbr_ref[...], dbbi_ref[...]))
        glre_ref[...] = glre
        glim_ref[...] = glim
        gldt_ref[...] = gldt
        gbre_ref[...] = gbre
        gbim_ref[...] = gbim

    small = jax.ShapeDtypeStruct(lre.shape, F32)
    big = jax.ShapeDtypeStruct(bre.shape, F32)
    return pl.pallas_call(body, name="s5_disc_bwd",
                          out_shape=[small, small, jax.ShapeDtypeStruct(ldt.shape, F32), big, big],
                          in_specs=[VMEM_FULL] * 9, out_specs=[VMEM_FULL] * 5,
                          )(lre, lim, ldt, bre, bim, dar, dai, dbbr, dbbi)


def _cmul(ar, ai, br, bi):
    return ar * br - ai * bi, ar * bi + ai * br


def _cpow(ar, ai, n):
    rr, ri = None, None
    pr, pi = ar, ai
    while n:
        if n & 1:
            rr, ri = (pr, pi) if rr is None else _cmul(rr, ri, pr, pi)
        n >>= 1
        if n:
            pr, pi = _cmul(pr, pi, pr, pi)
    return rr, ri


def _shift_rows(v, down):
    row = lax.broadcasted_iota(jnp.int32, v.shape, 0)
    if down:
        return jnp.where(row == 0, 0.0, pltpu.roll(v, 1, 0))
    return jnp.where(row == S5_SEGS - 1, 0.0, pltpu.roll(v, S5_SEGS - 1, 0))


def _chain_segments(er, ei, pr, pi, down):
    fr, fi = er, ei
    for _ in range(S5_SEGS - 1):
        sr, si = _shift_rows(fr, down), _shift_rows(fi, down)
        mr, mi = _cmul(pr, pi, sr, si)
        fr, fi = er + mr, ei + mi
    return _shift_rows(fr, down), _shift_rows(fi, down)


def _s5_fwd(ugp, bd, ctd, ar4, ai4, dskip):
    t = ugp.shape[0]
    ls = t // S5_SEGS
    rc = min(512, t)
    ns = S5_BSTATE

    def body(ug_ref, bd_ref, ct_ref, ar_ref, ai_ref, d_ref, xs_hbm, y_ref, buf, sem):
        cb = pl.program_id(0)
        bdv = bd_ref[0]

        def mm(i, c):
            rows = pl.ds(pl.multiple_of(i * rc, rc), rc)
            buf[rows, :] = _nn(ug_ref[rows, :].astype(BF16), bdv)
            return c

        lax.fori_loop(0, t // rc, mm, 0)
        arb = jnp.broadcast_to(ar_ref[0], (S5_SEGS, ns))
        aib = jnp.broadcast_to(ai_ref[0], (S5_SEGS, ns))

        def step(j, c, store):
            sr, si = c
            rows = pl.ds(pl.multiple_of(j * S5_SEGS, S5_SEGS), S5_SEGS)
            nr = arb * sr - aib * si + buf[rows, 0:ns]
            ni = arb * si + aib * sr + buf[rows, ns:2 * ns]
            if store:
                buf[rows, 0:ns] = nr
                buf[rows, ns:2 * ns] = ni
            return nr, ni

        zero = jnp.zeros((S5_SEGS, ns), F32)
        er, ei = lax.fori_loop(0, ls, functools.partial(step, store=False), (zero, zero))
        pr, pi = _cpow(arb, aib, ls)
        init = _chain_segments(er, ei, pr, pi, down=True)
        lax.fori_loop(0, ls, functools.partial(step, store=True), init)

        out = pltpu.make_async_copy(buf, xs_hbm.at[cb], sem)
        out.start()
        ctv = ct_ref[0]
        dv = d_ref[...]

        def ymm(i, c):
            rows = pl.ds(pl.multiple_of(i * rc, rc), rc)
            y_ref[rows, :] = _nn(buf[rows, :].astype(BF16), ctv) + dv * ug_ref[rows, :]
            return c

        lax.fori_loop(0, t // rc, ymm, 0)
        out.wait()

    return pl.pallas_call(
        body, name="s5_fwd", grid=(S5_BLOCKS,),
        in_specs=[pl.BlockSpec((t, 128), lambda i: (0, i)),
                  pl.BlockSpec((1, 128, 2 * ns), lambda i: (i, 0, 0)),
                  pl.BlockSpec((1, 2 * ns, 128), lambda i: (i, 0, 0)),
                  pl.BlockSpec((1, 1, ns), lambda i: (i, 0, 0)),
                  pl.BlockSpec((1, 1, ns), lambda i: (i, 0, 0)),
                  pl.BlockSpec((1, 128), lambda i: (0, i))],
        out_specs=[ANY, pl.BlockSpec((t, 128), lambda i: (0, i))],
        out_shape=[jax.ShapeDtypeStruct((S5_BLOCKS, t, 2 * ns), F32), jax.ShapeDtypeStruct((t, S5_WIDTH), F32)],
        scratch_shapes=[pltpu.VMEM((t, 2 * ns), F32), pltpu.SemaphoreType.DMA(())],
        compiler_params=_cparams(1),
    )(ugp, bd, ctd, ar4, ai4, dskip)


def _s5_bwd(dyp, ugp, xs, cd, bdt, ar4, ai4, dskip):
    t = ugp.shape[0]
    ls = t // S5_SEGS
    rc = min(512, t)
    ns = S5_BSTATE

    def body(dy_ref, ug_ref, xs_hbm, cd_ref, bdt_ref, ar_ref, ai_ref, d_ref,
             dug_ref, dbd_ref, dcd_ref, dd_ref, dar_ref, dai_ref, xbuf, lam, sem):
        cb = pl.program_id(0)
        load = pltpu.make_async_copy(xs_hbm.at[cb], xbuf, sem)
        load.start()
        cdv = cd_ref[0]

        def mm(i, c):
            rows = pl.ds(pl.multiple_of(i * rc, rc), rc)
            lam[rows, :] = _nn(dy_ref[rows, :].astype(BF16), cdv)
            return c

        lax.fori_loop(0, t // rc, mm, 0)
        arb = jnp.broadcast_to(ar_ref[0], (S5_SEGS, ns))
        aib = jnp.broadcast_to(ai_ref[0], (S5_SEGS, ns))

        def lam_step(j, lr, li):
            rows = pl.ds(pl.multiple_of(j * S5_SEGS, S5_SEGS), S5_SEGS)
            nr = arb * lr + aib * li + lam[rows, 0:ns]
            ni = arb * li - aib * lr + lam[rows, ns:2 * ns]
            return rows, nr, ni

        def pass1(jj, c):
            _, nr, ni = lam_step(ls - 1 - jj, *c)
            return nr, ni

        zero = jnp.zeros((S5_SEGS, ns), F32)
        er, ei = lax.fori_loop(0, ls, pass1, (zero, zero))
        pr, pi = _cpow(arb, aib, ls)
        init = _chain_segments(er, ei, pr, -pi, down=False)
        load.wait()

        def accumulate(acc, nr, ni, xpr, xpi):
            return acc[0] + nr * xpr + ni * xpi, acc[1] + ni * xpr - nr * xpi

        def pass2(jj, c):
            lr, li, accr, acci = c
            j = ls - 1 - jj
            rows, nr, ni = lam_step(j, lr, li)
            lam[rows, 0:ns] = nr
            lam[rows, ns:2 * ns] = ni
            prev = pl.ds(pl.multiple_of((j - 1) * S5_SEGS, S5_SEGS), S5_SEGS)
            accr, acci = accumulate((accr, acci), nr, ni, xbuf[prev, 0:ns], xbuf[prev, ns:2 * ns])
            return nr, ni, accr, acci

        lr, li, accr, acci = lax.fori_loop(0, ls - 1, pass2, (init[0], init[1], zero, zero))
        rows, nr, ni = lam_step(0, lr, li)
        lam[rows, 0:ns] = nr
        lam[rows, ns:2 * ns] = ni
        last = pl.ds((ls - 1) * S5_SEGS, S5_SEGS)
        accr, acci = accumulate((accr, acci), nr, ni,
                                _shift_rows(xbuf[last, 0:ns], True), _shift_rows(xbuf[last, ns:2 * ns], True))
        dar_ref[0] = jnp.sum(accr, axis=0, keepdims=True)
        dai_ref[0] = jnp.sum(acci, axis=0, keepdims=True)

        bdtv = bdt_ref[0]
        dv = d_ref[...]
        dbd_ref[...] = jnp.zeros_like(dbd_ref)
        dcd_ref[...] = jnp.zeros_like(dcd_ref)
        dd_ref[...] = jnp.zeros_like(dd_ref)

        def tail(i, c):
            rows = pl.ds(pl.multiple_of(i * rc, rc), rc)
            dy = dy_ref[rows, :]
            ug = ug_ref[rows, :]
            lb = lam[rows, :].astype(BF16)
            dug_ref[rows, :] = _nn(lb, bdtv) + dv * dy
            dbd_ref[0] += _tn(ug.astype(BF16), lb)
            dcd_ref[0] += _tn(dy.astype(BF16), xbuf[rows, :].astype(BF16))
            dd_ref[...] += jnp.sum(dy * ug, axis=0, keepdims=True)
            return c

        lax.fori_loop(0, t // rc, tail, 0)

    chan = pl.BlockSpec((t, 128), lambda i: (0, i))
    dense = pl.BlockSpec((1, 128, 2 * ns), lambda i: (i, 0, 0))
    vec = pl.BlockSpec((1, 1, ns), lambda i: (i, 0, 0))
    return pl.pallas_call(
        body, name="s5_bwd", grid=(S5_BLOCKS,),
        in_specs=[chan, chan, ANY, dense, pl.BlockSpec((1, 2 * ns, 128), lambda i: (i, 0, 0)), vec, vec,
                  pl.BlockSpec((1, 128), lambda i: (0, i))],
        out_specs=[chan, dense, dense, pl.BlockSpec((1, 128), lambda i: (0, i)), vec, vec],
        out_shape=[jax.ShapeDtypeStruct((t, S5_WIDTH), F32),
                   jax.ShapeDtypeStruct((S5_BLOCKS, 128, 2 * ns), F32),
                   jax.ShapeDtypeStruct((S5_BLOCKS, 128, 2 * ns), F32),
                   jax.ShapeDtypeStruct((1, S5_WIDTH), F32),
                   jax.ShapeDtypeStruct((S5_BLOCKS, 1, ns), F32),
                   jax.ShapeDtypeStruct((S5_BLOCKS, 1, ns), F32)],
        scratch_shapes=[pltpu.VMEM((t, 2 * ns), F32), pltpu.VMEM((t, 2 * ns), F32), pltpu.SemaphoreType.DMA(())],
        compiler_params=_cparams(1),
    )(dyp, ugp, xs, cd, bdt, ar4, ai4, dskip)


def _gla_common(q, k, alow, wup, bup):
    c = GLA_CHUNK
    pre = _nn(alow.astype(BF16), wup.astype(BF16)) + bup
    la = (jnp.minimum(pre, 0.0) - jnp.log(1.0 + jnp.exp(-jnp.abs(pre)))) * (1.0 / GLA_TAU)
    rr = lax.broadcasted_iota(jnp.int32, (c, c), 0)
    cc = lax.broadcasted_iota(jnp.int32, (c, c), 1)
    tril = (rr >= cc).astype(F32)
    bc = jnp.dot(tril, la, precision=HIGHEST, preferred_element_type=F32)
    bl = bc[c - 1:c, :]
    e_pos = jnp.exp(bc)
    e_neg = jnp.exp(-bc)
    e_end = jnp.exp(bl - bc)
    qt = q * (GLA_DK ** -0.5) * e_pos
    kt = k * e_neg
    ke = k * e_end
    decb = jnp.exp(lax.dot_general(la, jnp.ones((c, GLA_DV), F32), (((0,), (0,)), ((), ())),
                                   precision=HIGHEST, preferred_element_type=F32))
    lane = lax.broadcasted_iota(jnp.int32, (1, GLA_KEY), 1)
    masks = [((lane >= h * GLA_DK) & (lane < (h + 1) * GLA_DK)).astype(F32) for h in range(GLA_HEADS)]
    return dict(pre=pre, tril=tril, bc=bc, bl=bl, e_pos=e_pos, e_neg=e_neg, e_end=e_end,
                qt=qt, kt=kt, ke=ke, decb=decb, masks=masks)


def _gla_fwd(q, k, v, alow, wup, bup):
    t = q.shape[0]
    c = GLA_CHUNK
    n = t // c

    def body(q_ref, k_ref, v_ref, al_ref, wup_ref, bup_ref, o_ref, ss_ref, s_ref):
        i = pl.program_id(0)

        @pl.when(i == 0)
        def _():
            s_ref[...] = jnp.zeros_like(s_ref)

        m = _gla_common(q_ref[...], k_ref[...], al_ref[...], wup_ref[...], bup_ref[...])
        s = s_ref[...]
        ss_ref[0] = s
        sb = s.astype(BF16)
        ktb = m["kt"].astype(BF16)
        keb = m["ke"].astype(BF16)
        for h in range(GLA_HEADS):
            qm = (m["qt"] * m["masks"][h]).astype(BF16)
            vh = v_ref[:, h * GLA_DV:(h + 1) * GLA_DV].astype(BF16)
            p = (m["tril"] * _nt(qm, ktb)).astype(BF16)
            o_ref[:, h * GLA_DV:(h + 1) * GLA_DV] = _nn(p, vh) + _nn(qm, sb)
            rows = slice(h * GLA_DK, (h + 1) * GLA_DK)
            s_ref[rows, :] = m["decb"][rows, :] * s[rows, :] + _tn(keb, vh)[rows, :]

    return pl.pallas_call(
        body, name="gla_fwd", grid=(n,),
        in_specs=[_row_tile(c, GLA_KEY), _row_tile(c, GLA_KEY), _row_tile(c, GLA_VAL), _row_tile(c, GLA_RANK),
                  VMEM_FULL, VMEM_FULL],
        out_specs=[_row_tile(c, GLA_VAL), pl.BlockSpec((1, GLA_KEY, GLA_DV), lambda i: (i, 0, 0))],
        out_shape=[jax.ShapeDtypeStruct((t, GLA_VAL), F32), jax.ShapeDtypeStruct((n, GLA_KEY, GLA_DV), F32)],
        scratch_shapes=[pltpu.VMEM((GLA_KEY, GLA_DV), F32)],
        compiler_params=_cparams(1),
    )(q, k, v, alow, wup, bup)


def _gla_bwd(q, k, v, alow, wup, bup, ssave, do):
    t = q.shape[0]
    c = GLA_CHUNK
    n = t // c

    def body(q_ref, k_ref, v_ref, al_ref, wup_ref, bup_ref, ss_ref, do_ref,
             dq_ref, dk_ref, dv_ref, dal_ref, dwup_ref, dbup_ref, ds_ref):
        i = pl.program_id(0)

        @pl.when(i == 0)
        def _():
            ds_ref[...] = jnp.zeros_like(ds_ref)
            dwup_ref[...] = jnp.zeros_like(dwup_ref)
            dbup_ref[...] = jnp.zeros_like(dbup_ref)

        alow_v = al_ref[...]
        wup_v = wup_ref[...]
        m = _gla_common(q_ref[...], k_ref[...], alow_v, wup_v, bup_ref[...])
        s = ss_ref[0]
        ds_in = ds_ref[...]
        sb = s.astype(BF16)
        dsb = ds_in.astype(BF16)
        qt, kt, ke = m["qt"], m["kt"], m["ke"]
        ktb = kt.astype(BF16)
        dqt = jnp.zeros((c, GLA_KEY), F32)
        dkt = jnp.zeros((c, GLA_KEY), F32)
        dke = jnp.zeros((c, GLA_KEY), F32)
        for h in range(GLA_HEADS):
            mask = m["masks"][h]
            qm = (qt * mask).astype(BF16)
            km = (kt * mask).astype(BF16)
            kem = (ke * mask).astype(BF16)
            cols = slice(h * GLA_DV, (h + 1) * GLA_DV)
            vh = v_ref[:, cols].astype(BF16)
            doh = do_ref[:, cols].astype(BF16)
            p = (m["tril"] * _nt(qm, ktb)).astype(BF16)
            dp = (m["tril"] * _nt(doh, vh)).astype(BF16)
            dv_ref[:, cols] = _tn(p, doh) + _nn(kem, dsb)
            dqt = dqt + _nn(dp, km) + _nt(doh, sb) * mask
            dkt = dkt + _tn(dp, qm)
            dke = dke + _nt(vh, dsb) * mask
            rows = slice(h * GLA_DK, (h + 1) * GLA_DK)
            ds_ref[rows, :] = m["decb"][rows, :] * ds_in[rows, :] + _tn(qm, doh)[rows, :]
        ddec = lax.dot_general(jnp.ones((8, GLA_DV), F32), ds_in * s, (((1,), (1,)), ((), ())),
                               precision=HIGHEST, preferred_element_type=F32)[0:1, :]
        dq_ref[...] = dqt * m["e_pos"] * (GLA_DK ** -0.5)
        dk_ref[...] = dkt * m["e_neg"] + dke * m["e_end"]
        dkeke = dke * ke
        dbl = jnp.sum(dkeke, axis=0, keepdims=True) + ddec * jnp.exp(m["bl"])
        last = (lax.broadcasted_iota(jnp.int32, (c, 1), 0) == c - 1).astype(F32)
        db_tot = dqt * qt - dkt * kt - dkeke + last * dbl
        dla = lax.dot_general(m["tril"], db_tot, (((0,), (0,)), ((), ())),
                              precision=HIGHEST, preferred_element_type=F32)
        dpre = dla * (1.0 / GLA_TAU) * jax.nn.sigmoid(-m["pre"])
        dpb = dpre.astype(BF16)
        dal_ref[...] = _nt(dpb, wup_v.astype(BF16))
        dwup_ref[...] += _tn(alow_v.astype(BF16), dpb)
        dbup_ref[...] += jnp.sum(dpre, axis=0, keepdims=True)

    def rev(d):
        return pl.BlockSpec((c, d), lambda i: (n - 1 - i, 0))

    return pl.pallas_call(
        body, name="gla_bwd", grid=(n,),
        in_specs=[rev(GLA_KEY), rev(GLA_KEY), rev(GLA_VAL), rev(GLA_RANK), VMEM_FULL, VMEM_FULL,
                  pl.BlockSpec((1, GLA_KEY, GLA_DV), lambda i: (n - 1 - i, 0, 0)), rev(GLA_VAL)],
        out_specs=[rev(GLA_KEY), rev(GLA_KEY), rev(GLA_VAL), rev(GLA_RANK),
                   pl.BlockSpec((GLA_RANK, GLA_KEY), lambda i: (0, 0)), _acc_row(GLA_KEY)],
        out_shape=[jax.ShapeDtypeStruct((t, GLA_KEY), F32), jax.ShapeDtypeStruct((t, GLA_KEY), F32),
                   jax.ShapeDtypeStruct((t, GLA_VAL), F32), jax.ShapeDtypeStruct((t, GLA_RANK), F32),
                   jax.ShapeDtypeStruct((GLA_RANK, GLA_KEY), F32), jax.ShapeDtypeStruct((1, GLA_KEY), F32)],
        scratch_shapes=[pltpu.VMEM((GLA_KEY, GLA_DV), F32)],
        compiler_params=_cparams(1),
    )(q, k, v, alow, wup, bup, ssave, do)


def _post_math(y, o, r, gs5, ggla, wg, bg, gn, ps5t, pglat):
    y2 = y * y
    th = jnp.tanh(GELU_C0 * (y + GELU_C1 * y * y2))
    z5 = 0.5 * y * (1.0 + th)
    z5b = z5.astype(BF16)
    gate = jax.nn.sigmoid(_nn(z5b, wg) + bg)
    ys5 = z5 * gate
    rs, on = [], []
    for h in range(GLA_HEADS):
        oh = o[:, h * GLA_DV:(h + 1) * GLA_DV]
        rh = lax.rsqrt(jnp.mean(oh * oh, axis=-1, keepdims=True) + EPS)
        rs.append(rh)
        on.append(oh * rh)
    on = jnp.concatenate(on, axis=-1)
    sr = jax.nn.sigmoid(r)
    silu_r = r * sr
    ygla = on * gn * silu_r
    ys5b, yglab = ys5.astype(BF16), ygla.astype(BF16)
    m5 = _nt(ys5b, ps5t)
    mg = _nt(yglab, pglat)
    s5g, glag = jax.nn.sigmoid(gs5), jax.nn.sigmoid(ggla)
    merged = s5g * m5 + glag * mg
    return dict(y2=y2, th=th, z5=z5, z5b=z5b, gate=gate, ys5b=ys5b, yglab=yglab, rs=rs, on=on, sr=sr,
                silu_r=silu_r, m5=m5, mg=mg, s5g=s5g, glag=glag, mergedb=merged.astype(BF16))


def _mix_post_fwd(y, o, r, gs5, ggla, h1, wg, bg, gn, ps5t, pglat, wout):
    t = y.shape[0]
    tm = _tile(t)

    def body(y_ref, o_ref, r_ref, gs5_ref, ggla_ref, h1_ref, wg_ref, bg_ref, gn_ref, ps_ref, pg_ref, wo_ref, h2_ref):
        m = _post_math(y_ref[...], o_ref[...], r_ref[...], gs5_ref[...], ggla_ref[...],
                       wg_ref[...], bg_ref[...], gn_ref[...], ps_ref[...], pg_ref[...])
        h2_ref[...] = h1_ref[...] + _nn(m["mergedb"], wo_ref[...])

    return pl.pallas_call(
        body, name="mix_post_fwd", grid=(t // tm,),
        in_specs=[_row_tile(tm, 512)] * 3 + [_row_tile(tm, D_MODEL)] * 3
        + [VMEM_FULL, _acc_row(512), _acc_row(512), VMEM_FULL, VMEM_FULL, VMEM_FULL],
        out_specs=_row_tile(tm, D_MODEL),
        out_shape=jax.ShapeDtypeStruct((t, D_MODEL), F32),
        compiler_params=_cparams(1),
    )(y, o, r, gs5, ggla, h1, wg, bg, gn, ps5t, pglat, wout)


def _mix_post_bwd(y, o, r, gs5, ggla, dh2, wg, bg, gn, ps5t, pglat, wout):
    t = y.shape[0]
    tm = _tile(t) // 2

    def body(y_ref, o_ref, r_ref, gs5_ref, ggla_ref, dh2_ref, wg_ref, bg_ref, gn_ref, ps_ref, pg_ref, wo_ref,
             dy_ref, do_ref, dr_ref, dgs5_ref, dggla_ref, dbg_ref, dgn_ref,
             z5b_ref, dgp_ref, ys5b_ref, dm5b_ref, yglab_ref, dmgb_ref, mergedb_ref, dh2b_ref):
        i = pl.program_id(0)
        yv, ov, rv = y_ref[...], o_ref[...], r_ref[...]
        wg, gn, ps5t, pglat = wg_ref[...], gn_ref[...], ps_ref[...], pg_ref[...]
        m = _post_math(yv, ov, rv, gs5_ref[...], ggla_ref[...], wg, bg_ref[...], gn, ps5t, pglat)
        dh2b = dh2_ref[...].astype(BF16)
        dmerged = _nt(dh2b, wo_ref[...])
        s5g, glag = m["s5g"], m["glag"]
        dgs5_ref[...] = dmerged * m["m5"] * s5g * (1.0 - s5g)
        dggla_ref[...] = dmerged * m["mg"] * glag * (1.0 - glag)
        dm5b = (dmerged * s5g).astype(BF16)
        dmgb = (dmerged * glag).astype(BF16)
        dys5 = _nn(dm5b, ps5t)
        dygla = _nn(dmgb, pglat)
        gate, z5, th = m["gate"], m["z5"], m["th"]
        dgpre = dys5 * z5 * gate * (1.0 - gate)
        dgpb = dgpre.astype(BF16)
        dz5 = dys5 * gate + _nt(dgpb, wg)
        dgelu = 0.5 * (1.0 + th) + 0.5 * yv * (1.0 - th * th) * GELU_C0 * (1.0 + 3.0 * GELU_C1 * m["y2"])
        dy_ref[...] = dz5 * dgelu
        on, sr, silu_r = m["on"], m["sr"], m["silu_r"]
        dr_ref[...] = dygla * on * gn * sr * (1.0 + rv * (1.0 - sr))
        dgn = jnp.sum(dygla * on * silu_r, axis=0, keepdims=True)
        don = dygla * gn * silu_r
        for h in range(GLA_HEADS):
            cols = slice(h * GLA_DV, (h + 1) * GLA_DV)
            donh, onh = don[:, cols], on[:, cols]
            do_ref[:, cols] = m["rs"][h] * (donh - onh * jnp.mean(donh * onh, axis=-1, keepdims=True))

        @pl.when(i == 0)
        def _():
            dbg_ref[...] = jnp.zeros_like(dbg_ref)
            dgn_ref[...] = jnp.zeros_like(dgn_ref)

        dbg_ref[...] += jnp.sum(dgpre, axis=0, keepdims=True)
        dgn_ref[...] += dgn
        z5b_ref[...] = m["z5b"]
        dgp_ref[...] = dgpb
        ys5b_ref[...] = m["ys5b"]
        dm5b_ref[...] = dm5b
        yglab_ref[...] = m["yglab"]
        dmgb_ref[...] = dmgb
        mergedb_ref[...] = m["mergedb"]
        dh2b_ref[...] = dh2b

    def f32(d):
        return jax.ShapeDtypeStruct((t, d), F32)

    def b16(d):
        return jax.ShapeDtypeStruct((t, d), BF16)

    widths = (512, 512, 512, 1024, 512, 1024, 1024, 1024)
    return pl.pallas_call(
        body, name="mix_post_bwd", grid=(t // tm,),
        in_specs=[_row_tile(tm, 512)] * 3 + [_row_tile(tm, D_MODEL)] * 3
        + [VMEM_FULL, _acc_row(512), _acc_row(512), VMEM_FULL, VMEM_FULL, VMEM_FULL],
        out_specs=[_row_tile(tm, 512)] * 3 + [_row_tile(tm, D_MODEL)] * 2 + [_acc_row(512)] * 2
        + [_row_tile(tm, w) for w in widths],
        out_shape=[f32(512)] * 3 + [f32(D_MODEL)] * 2 + [jax.ShapeDtypeStruct((1, 512), F32)] * 2
        + [b16(w) for w in widths],
        compiler_params=_cparams(1),
    )(y, o, r, gs5, ggla, dh2, wg, bg, gn, ps5t, pglat, wout)


def _head(h3, g, target):
    t = h3.shape[0]
    tm = _tile(t)

    def body(h_ref, g_ref, t_ref, loss_ref, dh_ref, dg_ref):
        i = pl.program_id(0)
        gv = g_ref[...]
        xhat, r = _rms_parts(h_ref[...])
        err = xhat * gv - t_ref[...]
        dx, dg = _rms_bwd(err * (1.0 / D_MODEL), gv, xhat, r)
        dh_ref[...] = dx

        @pl.when(i == 0)
        def _():
            loss_ref[...] = jnp.zeros_like(loss_ref)
            dg_ref[...] = jnp.zeros_like(dg_ref)

        loss_ref[...] += (0.5 / D_MODEL) * jnp.sum(jnp.sum(err * err, axis=1, keepdims=True), axis=0, keepdims=True)
        dg_ref[...] += dg

    return pl.pallas_call(
        body, name="head", grid=(t // tm,),
        in_specs=[_row_tile(tm, D_MODEL), _acc_row(D_MODEL), _row_tile(tm, D_MODEL)],
        out_specs=[pl.BlockSpec((1, 1), lambda i: (0, 0)), _row_tile(tm, D_MODEL), _acc_row(D_MODEL)],
        out_shape=[jax.ShapeDtypeStruct((1, 1), F32), jax.ShapeDtypeStruct((t, D_MODEL), F32),
                   jax.ShapeDtypeStruct((1, D_MODEL), F32)],
        compiler_params=_cparams(1),
    )(h3, g, target)


def _adamw(w, g, m, v, tr, name):
    rows, cols = w.shape

    def body(w_ref, g_ref, m_ref, v_ref, d_ref, nm_ref, nv_ref):
        gv = g_ref[...]
        nm = ADAM_B1 * m_ref[...] + (1.0 - ADAM_B1) * gv
        nv = ADAM_B2 * v_ref[...] + (1.0 - ADAM_B2) * (gv * gv)
        m_hat = nm / (1.0 - ADAM_B1 ** ADAM_STEP)
        v_hat = nv / (1.0 - ADAM_B2 ** ADAM_STEP)
        d_ref[...] = -ADAM_LR * (m_hat / (jnp.sqrt(v_hat) + ADAM_EPS) + ADAM_WD * w_ref[...])
        nm_ref[...] = nm
        nv_ref[...] = nv

    spec = pl.BlockSpec((tr, cols), lambda i: (i, 0))
    sh = jax.ShapeDtypeStruct((rows, cols), F32)
    return pl.pallas_call(body, name=name, grid=(rows // tr,), in_specs=[spec] * 4, out_specs=[spec] * 3,
                          out_shape=[sh] * 3, compiler_params=_cparams(1))(w, g, m, v)


def _peers():
    x, y, c = lax.axis_index("x"), lax.axis_index("y"), lax.axis_index("c")
    out = []
    for k in range(1, N_DEV):
        px = 1 - x if k & 4 else x
        py = 1 - y if k & 2 else y
        pc = 1 - c if k & 1 else c
        out.append(((px, py, pc), 4 * px + 2 * py + pc))
    return 4 * x + 2 * y + c, out


def _exchange(src, scatter, name):
    shape = src.shape[1:] if scatter else src.shape

    def body(src_ref, out_ref, send_sems, recv_sems, local_sem):
        me, peers = _peers()

        def mine(idx):
            return src_ref.at[idx] if scatter else src_ref

        local = pltpu.make_async_copy(mine(me), out_ref.at[me], local_sem)
        local.start()
        sends = []
        for k, (dev, idx) in enumerate(peers):
            cp = pltpu.make_async_remote_copy(src_ref=mine(idx), dst_ref=out_ref.at[me],
                                              send_sem=send_sems.at[k], recv_sem=recv_sems.at[k],
                                              device_id=dev, device_id_type=pl.DeviceIdType.MESH)
            cp.start()
            sends.append(cp)
        for k, (dev, idx) in enumerate(peers):
            pltpu.make_async_remote_copy(src_ref=mine(idx), dst_ref=out_ref.at[idx],
                                         send_sem=send_sems.at[k], recv_sem=recv_sems.at[k],
                                         device_id=dev, device_id_type=pl.DeviceIdType.MESH).wait_recv()
        for cp in sends:
            cp.wait_send()
        local.wait()

    return pl.pallas_call(
        body, name=name, in_specs=[ANY], out_specs=ANY,
        out_shape=jax.ShapeDtypeStruct((N_DEV,) + tuple(shape), src.dtype),
        scratch_shapes=[pltpu.SemaphoreType.DMA((N_DEV - 1,)), pltpu.SemaphoreType.DMA((N_DEV - 1,)),
                        pltpu.SemaphoreType.DMA(())],
    )(src)


def _sum_slabs(slabs, tr, name):
    n, rows, cols = slabs.shape

    def body(s_ref, o_ref):
        acc = s_ref[0]
        for s in range(1, n):
            acc = acc + s_ref[s]
        o_ref[...] = acc

    return pl.pallas_call(
        body, name=name, grid=(rows // tr,),
        in_specs=[pl.BlockSpec((n, tr, cols), lambda i: (0, i, 0))],
        out_specs=pl.BlockSpec((tr, cols), lambda i: (i, 0)),
        out_shape=jax.ShapeDtypeStruct((rows, cols), F32),
        compiler_params=_cparams(1),
    )(slabs)


BIG = ("ffn1_w1", "ffn1_w3", "ffn1_w2", "w_in", "s5_glu_w", "gla_a_up_w", "proj_s5", "proj_gla", "w_out",
       "ffn2_w1", "ffn2_w3", "ffn2_w2")
BIG_ROWS = dict(ffn1_w1=352, ffn1_w3=352, ffn1_w2=352, w_in=514, s5_glu_w=32, gla_a_up_w=1, proj_s5=64,
                proj_gla=64, w_out=128, ffn2_w1=352, ffn2_w3=352, ffn2_w2=352)
BIG_OFF = {}
_o = 0
for _n in BIG:
    BIG_OFF[_n] = _o
    _o += BIG_ROWS[_n]
BIG_R = -(-_o // 48) * 48
BIG_TR = BIG_R // 6
COL_SHARDED = ("ffn1_w1", "ffn1_w3", "w_in", "proj_s5", "proj_gla", "ffn2_w1", "ffn2_w3")

SMALL = ("ffn1_norm", "mix_norm", "s5_lambda_re", "s5_lambda_im", "s5_log_dt", "s5_b_re", "s5_b_im", "s5_c_re",
         "s5_c_im", "s5_d", "s5_glu_b", "gla_a_up_b", "gla_out_norm", "ffn2_norm", "final_norm")
SMALL_SHAPES = dict(ffn1_norm=(1, 1024), mix_norm=(1, 1024), s5_lambda_re=(1, 32, 64), s5_lambda_im=(1, 32, 64),
                    s5_log_dt=(1, 32), s5_b_re=(1, 32, 64, 16), s5_b_im=(1, 32, 64, 16), s5_c_re=(1, 32, 16, 64),
                    s5_c_im=(1, 32, 16, 64), s5_d=(1, 32, 16), s5_glu_b=(1, 512), gla_a_up_b=(1, 256),
                    gla_out_norm=(1, 512), ffn2_norm=(1, 1024), final_norm=(1024,))
SMALL_N = sum(math.prod(s) for s in SMALL_SHAPES.values())
SMALL_R = -(-SMALL_N // (16 * 1024)) * 16


def _shard_rows(name, a):
    if name in COL_SHARDED:
        a = a.T
    if name == "gla_a_up_w":
        return jnp.pad(a.reshape(1, -1), ((0, 0), (0, 1024 - a.size)))
    return a.reshape(-1, 1024)


def _unshard_rows(name, rows, shape):
    if name == "gla_a_up_w":
        return rows[:, :math.prod(shape)].reshape(shape)
    if name in COL_SHARDED:
        return rows.reshape(shape[1], shape[0]).T
    return rows.reshape(shape)


def _pack_big(shards, dtype):
    parts = [_shard_rows(n, shards[n]).astype(dtype) for n in BIG]
    pad = BIG_R - sum(p.shape[0] for p in parts)
    return jnp.concatenate(parts + [jnp.zeros((pad, 1024), dtype)], axis=0)


def _pack_small(vals):
    flat = jnp.concatenate([vals[n].reshape(-1).astype(F32) for n in SMALL])
    return jnp.pad(flat, (0, SMALL_R * 1024 - SMALL_N)).reshape(SMALL_R, 1024)


def _unpack_small(slab):
    flat = slab.reshape(-1)
    out, off = {}, 0
    for n in SMALL:
        size = math.prod(SMALL_SHAPES[n])
        out[n] = flat[off:off + size].reshape(SMALL_SHAPES[n])
        off += size
    return out


def _full_weights(gathered):
    def take(name):
        o = BIG_OFF[name]
        return gathered[:, o:o + BIG_ROWS[name], :]

    w = {}
    for n in ("ffn1_w1", "ffn1_w3", "ffn1_w2", "ffn2_w1", "ffn2_w3", "ffn2_w2"):
        w[n] = take(n).reshape(D_FF, D_MODEL)
    w["w_in"] = take("w_in").reshape(IN_COLS, D_MODEL)
    w["s5_glu_w"] = take("s5_glu_w").reshape(S5_WIDTH, S5_WIDTH)
    w["gla_a_up_w"] = take("gla_a_up_w")[:, 0, :GLA_RANK * 32].reshape(N_DEV, GLA_RANK, 32).transpose(1, 0, 2).reshape(
        GLA_RANK, GLA_KEY)
    w["proj_s5"] = take("proj_s5").reshape(D_MODEL, S5_WIDTH)
    w["proj_gla"] = take("proj_gla").reshape(D_MODEL, GLA_VAL)
    w["w_out"] = take("w_out").reshape(D_MODEL, D_MODEL)
    return w


def _scatter_slab(grads):
    parts = []
    for n in BIG:
        g = grads[n]
        if n == "gla_a_up_w":
            g = g.reshape(GLA_RANK, N_DEV, 32).transpose(1, 0, 2).reshape(N_DEV, 1, GLA_RANK * 32)
            g = jnp.pad(g, ((0, 0), (0, 0), (0, 1024 - GLA_RANK * 32)))
        else:
            g = g.reshape(N_DEV, BIG_ROWS[n], 1024)
        parts.append(g)
    pad = BIG_R - sum(p.shape[1] for p in parts)
    return jnp.concatenate(parts + [jnp.zeros((N_DEV, pad, 1024), F32)], axis=1)


def _s5_dense(re, im, sign_im):
    eye = jnp.eye(8, dtype=F32)

    def one(a):
        a = a.reshape(S5_BLOCKS, 8, S5_GROUP, S5_STATE)
        return jnp.einsum("cghp,gk->cghkp", a, eye).reshape(S5_BLOCKS, 128, S5_BSTATE)

    return jnp.concatenate([one(re), sign_im * one(im)], axis=-1)


def _s5_undense(d):
    eye = jnp.eye(8, dtype=F32)

    def one(a):
        a = a.reshape(S5_BLOCKS, 8, S5_GROUP, 8, S5_STATE)
        return jnp.einsum("cghkp,gk->cghp", a, eye).reshape(S5_GROUPS, S5_GROUP, S5_STATE)

    return one(d[..., :S5_BSTATE]), one(d[..., S5_BSTATE:])


def _permute_rows(a):
    t = a.shape[0]
    return a.reshape(S5_SEGS, t // S5_SEGS, -1).transpose(1, 0, 2).reshape(t, -1)


def _unpermute_rows(a):
    t = a.shape[0]
    return a.reshape(t // S5_SEGS, S5_SEGS, -1).transpose(1, 0, 2).reshape(t, -1)


def _local_step(x, target, p, w):
    g1, gm, g2 = p["ffn1_norm"], p["mix_norm"], p["ffn2_norm"]
    gf = p["final_norm"].reshape(1, D_MODEL)
    lre, lim = p["s5_lambda_re"][0], p["s5_lambda_im"][0]
    ldt = p["s5_log_dt"][0].reshape(S5_GROUPS, 1)
    bre = p["s5_b_re"][0].transpose(2, 0, 1)
    bim = p["s5_b_im"][0].transpose(2, 0, 1)
    cre, cim = p["s5_c_re"][0], p["s5_c_im"][0]
    dskip = p["s5_d"][0].reshape(1, S5_WIDTH)
    bg, bup, gn = p["s5_glu_b"], p["gla_a_up_b"], p["gla_out_norm"]
    wup = w["gla_a_up_w"].astype(F32)

    h1 = _ffn_fwd(x, g1, w["ffn1_w1"], w["ffn1_w3"], w["ffn1_w2"], "ffn1_fwd")
    u, s5in, q, k, v, r, alow, gs5, ggla = _mix_pre_fwd(h1, gm, w["w_in"])
    ar, ai, bbr, bbi = _s5_disc(lre, lim, ldt, bre, bim)
    bd = _s5_dense(bbr.transpose(1, 0, 2), bbi.transpose(1, 0, 2), 1.0)
    cd = _s5_dense(cre, cim, -1.0)
    bd16, cd16 = bd.astype(BF16), cd.astype(BF16)
    bdt16, ctd16 = bd16.transpose(0, 2, 1), cd16.transpose(0, 2, 1)
    ar4 = ar.reshape(S5_BLOCKS, 1, S5_BSTATE)
    ai4 = ai.reshape(S5_BLOCKS, 1, S5_BSTATE)
    ugp = _permute_rows(s5in)
    xs, yp = _s5_fwd(ugp, bd16, ctd16, ar4, ai4, dskip)
    y = _unpermute_rows(yp)
    o, ssave = _gla_fwd(q, k, v, alow, wup, bup)
    post_w = (w["s5_glu_w"], bg, gn, w["proj_s5"], w["proj_gla"], w["w_out"])
    h2 = _mix_post_fwd(y, o, r, gs5, ggla, h1, *post_w)
    h3 = _ffn_fwd(h2, g2, w["ffn2_w1"], w["ffn2_w3"], w["ffn2_w2"], "ffn2_fwd")
    loss, dh3, dgf = _head(h3, gf, target)

    big, small = {}, {}
    small["final_norm"] = dgf.reshape(D_MODEL)
    dh2, dg2, da3, db3, s3, n2, dhh2 = _ffn_bwd(h2, dh3, g2, w["ffn2_w1"], w["ffn2_w3"], w["ffn2_w2"], "ffn2_bwd")
    small["ffn2_norm"] = dg2
    big["ffn2_w1"] = _mm_tn(da3, n2, "ffn2_dw1")
    big["ffn2_w3"] = _mm_tn(db3, n2, "ffn2_dw3")
    big["ffn2_w2"] = _mm_tn(s3, dhh2, "ffn2_dw2")
    (dy, do, dr, dgs5, dggla, dbg, dgn,
     z5b, dgpb, ys5b, dm5b, yglab, dmgb, mergedb, dh2b) = _mix_post_bwd(y, o, r, gs5, ggla, dh2, *post_w)
    small["s5_glu_b"] = dbg
    small["gla_out_norm"] = dgn
    big["s5_glu_w"] = _mm_tn(z5b, dgpb, "glu_dw")
    big["proj_s5"] = _mm_tn(dm5b, ys5b, "proj_s5_dw")
    big["proj_gla"] = _mm_tn(dmgb, yglab, "proj_gla_dw")
    big["w_out"] = _mm_tn(mergedb, dh2b, "w_out_dw")
    dq, dk, dv, dalow, dwup, dbup = _gla_bwd(q, k, v, alow, wup, bup, ssave, do)
    big["gla_a_up_w"] = dwup
    small["gla_a_up_b"] = dbup
    dugp, dbd, dcd, dd, dar4, dai4 = _s5_bwd(_permute_rows(dy), ugp, xs, cd16, bdt16, ar4, ai4, dskip)
    ds5in = _unpermute_rows(dugp)
    dbbr, dbbi = _s5_undense(dbd)
    dcre, dcim_neg = _s5_undense(dcd)
    glre, glim, gldt, gbre, gbim = _s5_disc_bwd(
        lre, lim, ldt, bre, bim, dar4.reshape(S5_GROUPS, S5_STATE), dai4.reshape(S5_GROUPS, S5_STATE),
        dbbr.transpose(1, 0, 2), dbbi.transpose(1, 0, 2))
    small["s5_lambda_re"] = glre[None]
    small["s5_lambda_im"] = glim[None]
    small["s5_log_dt"] = gldt.reshape(1, S5_GROUPS)
    small["s5_b_re"] = gbre.transpose(1, 2, 0)[None]
    small["s5_b_im"] = gbim.transpose(1, 2, 0)[None]
    small["s5_c_re"] = dcre[None]
    small["s5_c_im"] = -dcim_neg[None]
    small["s5_d"] = dd.reshape(1, S5_GROUPS, S5_GROUP)
    dz = (ds5in, dq, dk, dv, dr, dalow, dgs5, dggla)
    dh1, dgm = _mix_pre_bwd(h1, gm, w["w_in"], dh2, dz)
    small["mix_norm"] = dgm
    big["w_in"] = jnp.concatenate([_mm_tn(d, u, "w_in_dw%d" % i) for i, d in enumerate(dz)], axis=0)
    dx, dg1, da3, db3, s3, n1, dhh1 = _ffn_bwd(x, dh1, g1, w["ffn1_w1"], w["ffn1_w3"], w["ffn1_w2"], "ffn1_bwd")
    small["ffn1_norm"] = dg1
    big["ffn1_w1"] = _mm_tn(da3, n1, "ffn1_dw1")
    big["ffn1_w3"] = _mm_tn(db3, n1, "ffn1_dw3")
    big["ffn1_w2"] = _mm_tn(s3, dhh1, "ffn1_dw2")
    return loss[0, 0], dx, big, small


NAMES = ("ffn1_norm", "ffn1_w1", "ffn1_w3", "ffn1_w2", "mix_norm", "w_in", "s5_lambda_re", "s5_lambda_im",
         "s5_log_dt", "s5_b_re", "s5_b_im", "s5_c_re", "s5_c_im", "s5_d", "s5_glu_w", "s5_glu_b", "gla_a_up_w",
         "gla_a_up_b", "gla_out_norm", "proj_s5", "proj_gla", "w_out", "ffn2_norm", "ffn2_w1", "ffn2_w3", "ffn2_w2",
         "final_norm")


def kernel(*args):
    nw = len(NAMES)
    x = args[0][0]
    wts = dict(zip(NAMES, args[1:1 + nw]))
    target = args[1 + nw][0]
    mom = dict(zip(NAMES, args[2 + nw:2 + 2 * nw]))
    var = dict(zip(NAMES, args[2 + 2 * nw:2 + 3 * nw]))

    shards = {n: wts[n][0] for n in BIG}
    gathered = _exchange(_pack_big(shards, BF16), False, "gather_weights")
    full = _full_weights(gathered)
    loss, dx, big, small = _local_step(x, target, {n: wts[n] for n in SMALL}, full)
    loss = lax.psum(loss, ("x", "y", "c"))

    g_big = _sum_slabs(_exchange(_scatter_slab(big), True, "scatter_grads"), BIG_TR, "sum_big")
    g_small = _sum_slabs(_exchange(_pack_small(small), False, "gather_small"), SMALL_R, "sum_small")

    d_big, m_big, v_big = _adamw(_pack_big(shards, F32), g_big, _pack_big({n: mom[n][0] for n in BIG}, F32),
                                 _pack_big({n: var[n][0] for n in BIG}, F32), BIG_TR, "adamw_big")
    d_small, m_small, v_small = _adamw(_pack_small({n: wts[n] for n in SMALL}), g_small,
                                       _pack_small({n: mom[n] for n in SMALL}),
                                       _pack_small({n: var[n] for n in SMALL}), SMALL_R, "adamw_small")

    def unpack(big_slab, small_slab):
        out = _unpack_small(small_slab)
        for n in BIG:
            rows = big_slab[BIG_OFF[n]:BIG_OFF[n] + BIG_ROWS[n]]
            out[n] = _unshard_rows(n, rows, shards[n].shape)[None]
        return [out[n] for n in NAMES]

    return (loss, dx[None], *unpack(g_big, g_small), *unpack(d_big, d_small), *unpack(m_big, m_small),
            *unpack(v_big, v_small))
```

```python
import functools
import math

import jax
import jax.numpy as jnp
from jax import lax
from jax.experimental import pallas as pl
from jax.experimental.pallas import tpu as pltpu

F32, BF16 = jnp.float32, jnp.bfloat16
HIGHEST = lax.Precision.HIGHEST

D_MODEL = 1024
D_FF = 2816
N_DEV = 8
S5_WIDTH, S5_GROUPS, S5_GROUP, S5_STATE = 512, 32, 16, 64
S5_BLOCKS = 4
S5_BSTATE = 512
S5_SEGS = 8
GLA_HEADS, GLA_DK, GLA_DV = 4, 64, 128
GLA_KEY, GLA_VAL, GLA_RANK, GLA_CHUNK = 256, 512, 16, 64
GLA_TAU = 16.0
EPS = 1e-6
IN_SIZES = (512, 256, 256, 512, 512, 16, 1024, 1024)
IN_OFFS = tuple(sum(IN_SIZES[:i]) for i in range(len(IN_SIZES)))
IN_COLS = sum(IN_SIZES)
ADAM_LR, ADAM_B1, ADAM_B2, ADAM_EPS, ADAM_WD, ADAM_STEP = 0.001, 0.9, 0.999, 1e-08, 0.01, 10
GELU_C0 = math.sqrt(2.0 / math.pi)
GELU_C1 = 0.044715

FFN_FT = 256
VMEM_LIMIT_BYTES = 56 * 1024 * 1024

VMEM_FULL = pl.BlockSpec(memory_space=pltpu.VMEM)
ANY = pl.BlockSpec(memory_space=pl.ANY)


def _cparams(n_grid):
    return pltpu.CompilerParams(dimension_semantics=("arbitrary",) * n_grid, vmem_limit_bytes=VMEM_LIMIT_BYTES)


def _tile(t):
    return 512 if t >= 1024 else t // 2


def _nn(a, b):
    return jnp.dot(a, b, preferred_element_type=F32)


def _nt(a, b):
    return lax.dot_general(a, b, (((1,), (1,)), ((), ())), preferred_element_type=F32)


def _tn(a, b):
    return lax.dot_general(a, b, (((0,), (0,)), ((), ())), preferred_element_type=F32)


def _rms_parts(x):
    r = lax.rsqrt(jnp.mean(x * x, axis=-1, keepdims=True) + EPS)
    return x * r, r


def _rms_bwd(dn, g, xhat, r):
    dxh = dn * g
    dx = r * (dxh - xhat * jnp.mean(dxh * xhat, axis=-1, keepdims=True))
    return dx, jnp.sum(dn * xhat, axis=0, keepdims=True)


def _row_tile(tm, d):
    return pl.BlockSpec((tm, d), lambda i: (i, 0))


def _acc_row(d):
    return pl.BlockSpec((1, d), lambda i: (0, 0))


def _seg_tile(t, tm):
    return min(tm, t // S5_SEGS)


def _perm_tile(t, tm, d):
    nj = (t // S5_SEGS) // tm
    return pl.BlockSpec((tm, d), lambda i: (i % nj, i // nj))


def _ffn_fwd(x, g, w1t, w3t, w2, name):
    t = x.shape[0]
    tm = _tile(t)
    nf = D_FF // FFN_FT

    def body(x_ref, g_ref, w1_ref, w3_ref, w2_ref, o_ref):
        xv = x_ref[...]
        xhat, _ = _rms_parts(xv)
        n = (xhat * g_ref[...]).astype(BF16)
        o_ref[...] = xv

        def fstep(f, c):
            rows = pl.ds(pl.multiple_of(f * FFN_FT, FFN_FT), FFN_FT)
            a = _nt(n, w1_ref[rows, :])
            b = _nt(n, w3_ref[rows, :])
            s = (a * jax.nn.sigmoid(a) * b).astype(BF16)
            o_ref[...] += 0.5 * _nn(s, w2_ref[rows, :])
            return c

        lax.fori_loop(0, nf, fstep, 0)

    return pl.pallas_call(
        body, name=name, grid=(t // tm,),
        in_specs=[_row_tile(tm, D_MODEL), _acc_row(D_MODEL), VMEM_FULL, VMEM_FULL, VMEM_FULL],
        out_specs=_row_tile(tm, D_MODEL),
        out_shape=jax.ShapeDtypeStruct((t, D_MODEL), F32),
        compiler_params=_cparams(1),
    )(x, g, w1t, w3t, w2)


def _ffn_bwd(x, dh, g, w1t, w3t, w2, name):
    t = x.shape[0]
    tm = _tile(t) // 2
    nf = D_FF // FFN_FT

    def body(x_ref, dh_ref, g_ref, w1_ref, w3_ref, w2_ref,
             dx_ref, dg_ref, da_ref, db_ref, s_ref, n_ref, dhh_ref, dn_acc):
        i = pl.program_id(0)
        xv = x_ref[...]
        gv = g_ref[...]
        xhat, r = _rms_parts(xv)
        n = (xhat * gv).astype(BF16)
        n_ref[...] = n
        dhv = dh_ref[...]
        dhh = (0.5 * dhv).astype(BF16)
        dhh_ref[...] = dhh
        dn_acc[...] = jnp.zeros_like(dn_acc)

        def fstep(f, c):
            rows = pl.ds(pl.multiple_of(f * FFN_FT, FFN_FT), FFN_FT)
            w1c, w3c, w2c = w1_ref[rows, :], w3_ref[rows, :], w2_ref[rows, :]
            a = _nt(n, w1c)
            b = _nt(n, w3c)
            sg = jax.nn.sigmoid(a)
            sl = a * sg
            ds = _nt(dhh, w2c)
            da = (ds * b * sg * (1.0 + a * (1.0 - sg))).astype(BF16)
            db = (ds * sl).astype(BF16)
            s_ref[f] = (sl * b).astype(BF16)
            da_ref[f] = da
            db_ref[f] = db
            dn_acc[...] += _nn(da, w1c) + _nn(db, w3c)
            return c

        lax.fori_loop(0, nf, fstep, 0)
        dx, dg = _rms_bwd(dn_acc[...], gv, xhat, r)
        dx_ref[...] = dhv + dx

        @pl.when(i == 0)
        def _():
            dg_ref[...] = jnp.zeros_like(dg_ref)

        dg_ref[...] += dg

    blk3 = pl.BlockSpec((nf, tm, FFN_FT), lambda i: (0, i, 0))
    sh3 = jax.ShapeDtypeStruct((nf, t, FFN_FT), BF16)
    return pl.pallas_call(
        body, name=name, grid=(t // tm,),
        in_specs=[_row_tile(tm, D_MODEL), _row_tile(tm, D_MODEL), _acc_row(D_MODEL), VMEM_FULL, VMEM_FULL, VMEM_FULL],
        out_specs=[_row_tile(tm, D_MODEL), _acc_row(D_MODEL), blk3, blk3, blk3,
                   _row_tile(tm, D_MODEL), _row_tile(tm, D_MODEL)],
        out_shape=[jax.ShapeDtypeStruct((t, D_MODEL), F32), jax.ShapeDtypeStruct((1, D_MODEL), F32), sh3, sh3, sh3,
                   jax.ShapeDtypeStruct((t, D_MODEL), BF16), jax.ShapeDtypeStruct((t, D_MODEL), BF16)],
        scratch_shapes=[pltpu.VMEM((tm, D_MODEL), F32)],
        compiler_params=_cparams(1),
    )(x, dh, g, w1t, w3t, w2)


def _mm_tn(a, b, name):
    t, n = b.shape
    kc = min(512, t)
    if a.ndim == 3:
        nb, _, tb = a.shape
        a_spec = pl.BlockSpec((1, t, tb), lambda i: (i, 0, 0))
    else:
        m = a.shape[1]
        tb = min(m, 256)
        nb = m // tb
        a_spec = pl.BlockSpec((t, tb), lambda i: (0, i))
    three_d = a.ndim == 3

    def body(a_ref, b_ref, o_ref, acc):
        acc[...] = jnp.zeros_like(acc)

        def kstep(k, c):
            rows = pl.ds(pl.multiple_of(k * kc, kc), kc)
            av = a_ref[0, rows, :] if three_d else a_ref[rows, :]
            acc[...] += _tn(av.astype(BF16), b_ref[rows, :])
            return c

        lax.fori_loop(0, t // kc, kstep, 0)
        o_ref[...] = acc[...].astype(BF16)

    return pl.pallas_call(
        body, name=name, grid=(nb,),
        in_specs=[a_spec, VMEM_FULL],
        out_specs=pl.BlockSpec((tb, n), lambda i: (i, 0)),
        out_shape=jax.ShapeDtypeStruct((nb * tb, n), BF16),
        scratch_shapes=[pltpu.VMEM((tb, n), F32)],
        compiler_params=_cparams(1),
    )(a, b)


def _mix_pre_fwd(h, g, wint):
    t = h.shape[0]
    tm = _seg_tile(t, _tile(t))

    def body(h_ref, g_ref, w_ref, u_ref, up_ref, *outs):
        xhat, _ = _rms_parts(h_ref[...])
        u = (xhat * g_ref[...]).astype(BF16)
        u_ref[...] = u
        up_ref[...] = u
        for o_ref, off, size in zip(outs, IN_OFFS, IN_SIZES):
            o_ref[...] = _nt(u, w_ref[off:off + size, :])

    return pl.pallas_call(
        body, name="mix_pre_fwd", grid=(t // tm,),
        in_specs=[_row_tile(tm, D_MODEL), _acc_row(D_MODEL), VMEM_FULL],
        out_specs=[_row_tile(tm, D_MODEL), _perm_tile(t, tm, D_MODEL), _perm_tile(t, tm, S5_WIDTH)]
        + [_row_tile(tm, s) for s in IN_SIZES[1:]],
        out_shape=[jax.ShapeDtypeStruct((t, D_MODEL), BF16),
                   jax.ShapeDtypeStruct((t // S5_SEGS, S5_SEGS * D_MODEL), BF16),
                   jax.ShapeDtypeStruct((t // S5_SEGS, S5_SEGS * S5_WIDTH), F32)]
        + [jax.ShapeDtypeStruct((t, s), F32) for s in IN_SIZES[1:]],
        compiler_params=_cparams(1),
    )(h, g, wint)


def _mix_pre_bwd(h, g, wint, dh2, dz):
    t = h.shape[0]
    tm = _seg_tile(t, _tile(t))

    def body(h_ref, g_ref, w_ref, dh2_ref, *rest):
        dz_refs, (dh1_ref, dg_ref) = rest[:len(IN_SIZES)], rest[len(IN_SIZES):]
        i = pl.program_id(0)
        gv = g_ref[...]
        xhat, r = _rms_parts(h_ref[...])
        du = jnp.zeros((tm, D_MODEL), F32)
        for dz_ref, off, size in zip(dz_refs, IN_OFFS, IN_SIZES):
            du = du + _nn(dz_ref[...].astype(BF16), w_ref[off:off + size, :])
        dx, dg = _rms_bwd(du, gv, xhat, r)
        dh1_ref[...] = dh2_ref[...] + dx

        @pl.when(i == 0)
        def _():
            dg_ref[...] = jnp.zeros_like(dg_ref)

        dg_ref[...] += dg

    return pl.pallas_call(
        body, name="mix_pre_bwd", grid=(t // tm,),
        in_specs=[_row_tile(tm, D_MODEL), _acc_row(D_MODEL), VMEM_FULL, _row_tile(tm, D_MODEL),
                  _perm_tile(t, tm, S5_WIDTH)] + [_row_tile(tm, s) for s in IN_SIZES[1:]],
        out_specs=[_row_tile(tm, D_MODEL), _acc_row(D_MODEL)],
        out_shape=[jax.ShapeDtypeStruct((t, D_MODEL), F32), jax.ShapeDtypeStruct((1, D_MODEL), F32)],
        compiler_params=_cparams(1),
    )(h, g, wint, dh2, *dz)


def _disc_math(lre, lim, ldt, bre, bim):
    dt = jnp.exp(ldt)
    mag = jnp.exp(lre * dt)
    ar = mag * jnp.cos(lim * dt)
    ai = mag * jnp.sin(lim * dt)
    den = lre * lre + lim * lim
    nr = ar - 1.0
    fr = (nr * lre + ai * lim) / den
    fi = (ai * lre - nr * lim) / den
    return ar, ai, fr[None] * bre - fi[None] * bim, fr[None] * bim + fi[None] * bre


def _s5_disc(lre, lim, ldt, bre, bim):
    def body(lre_ref, lim_ref, ldt_ref, bre_ref, bim_ref, ar_ref, ai_ref, bbr_ref, bbi_ref):
        ar, ai, bbr, bbi = _disc_math(lre_ref[...], lim_ref[...], ldt_ref[...], bre_ref[...], bim_ref[...])
        ar_ref[...] = ar
        ai_ref[...] = ai
        bbr_ref[...] = bbr
        bbi_ref[...] = bbi

    small = jax.ShapeDtypeStruct(lre.shape, F32)
    big = jax.ShapeDtypeStruct(bre.shape, F32)
    return pl.pallas_call(body, name="s5_disc", out_shape=[small, small, big, big],
                          in_specs=[VMEM_FULL] * 5, out_specs=[VMEM_FULL] * 4)(lre, lim, ldt, bre, bim)


def _s5_disc_bwd(lre, lim, ldt, bre, bim, dar, dai, dbbr, dbbi):
    def body(lre_ref, lim_ref, ldt_ref, bre_ref, bim_ref, dar_ref, dai_ref, dbbr_ref, dbbi_ref,
             glre_ref, glim_ref, gldt_ref, gbre_ref, gbim_ref):
        _, vjp = jax.vjp(_disc_math, lre_ref[...], lim_ref[...], ldt_ref[...], bre_ref[...], bim_ref[...])
        glre, glim, gldt, gbre, gbim = vjp((dar_ref[...], dai_ref[...], dbbr_ref[...], dbbi_ref[...]))
        glre_ref[...] = glre
        glim_ref[...] = glim
        gldt_ref[...] = gldt
        gbre_ref[...] = gbre
        gbim_ref[...] = gbim

    small = jax.ShapeDtypeStruct(lre.shape, F32)
    big = jax.ShapeDtypeStruct(bre.shape, F32)
    return pl.pallas_call(body, name="s5_disc_bwd",
                          out_shape=[small, small, jax.ShapeDtypeStruct(ldt.shape, F32), big, big],
                          in_specs=[VMEM_FULL] * 9, out_specs=[VMEM_FULL] * 5,
                          )(lre, lim, ldt, bre, bim, dar, dai, dbbr, dbbi)


def _cmul(ar, ai, br, bi):
    return ar * br - ai * bi, ar * bi + ai * br


def _cpow(ar, ai, n):
    rr, ri = None, None
    pr, pi = ar, ai
    while n:
        if n & 1:
            rr, ri = (pr, pi) if rr is None else _cmul(rr, ri, pr, pi)
        n >>= 1
        if n:
            pr, pi = _cmul(pr, pi, pr, pi)
    return rr, ri


def _shift_rows(v, down):
    row = lax.broadcasted_iota(jnp.int32, v.shape, 0)
    if down:
        return jnp.where(row == 0, 0.0, pltpu.roll(v, 1, 0))
    return jnp.where(row == S5_SEGS - 1, 0.0, pltpu.roll(v, S5_SEGS - 1, 0))


def _chain_segments(er, ei, pr, pi, down):
    fr, fi = er, ei
    for _ in range(S5_SEGS - 1):
        sr, si = _shift_rows(fr, down), _shift_rows(fi, down)
        mr, mi = _cmul(pr, pi, sr, si)
        fr, fi = er + mr, ei + mi
    return _shift_rows(fr, down), _shift_rows(fi, down)


def _s5_fwd(ugp, bd, ctd, ar4, ai4, dskip):
    t = ugp.shape[0]
    ls = t // S5_SEGS
    rc = min(512, t)
    ns = S5_BSTATE

    def body(ug_ref, bd_ref, ct_ref, ar_ref, ai_ref, d_ref, xs_hbm, y_ref, buf, sem):
        cb = pl.program_id(0)
        bdv = bd_ref[0]

        def mm(i, c):
            rows = pl.ds(pl.multiple_of(i * rc, rc), rc)
            buf[rows, :] = _nn(ug_ref[rows, :].astype(BF16), bdv)
            return c

        lax.fori_loop(0, t // rc, mm, 0)
        arb = jnp.broadcast_to(ar_ref[0], (S5_SEGS, ns))
        aib = jnp.broadcast_to(ai_ref[0], (S5_SEGS, ns))

        def step(j, c, store):
            sr, si = c
            rows = pl.ds(pl.multiple_of(j * S5_SEGS, S5_SEGS), S5_SEGS)
            nr = arb * sr - aib * si + buf[rows, 0:ns]
            ni = arb * si + aib * sr + buf[rows, ns:2 * ns]
            if store:
                buf[rows, 0:ns] = nr
                buf[rows, ns:2 * ns] = ni
            return nr, ni

        zero = jnp.zeros((S5_SEGS, ns), F32)
        er, ei = lax.fori_loop(0, ls, functools.partial(step, store=False), (zero, zero))
        pr, pi = _cpow(arb, aib, ls)
        init = _chain_segments(er, ei, pr, pi, down=True)
        lax.fori_loop(0, ls, functools.partial(step, store=True), init)

        out = pltpu.make_async_copy(buf, xs_hbm.at[cb], sem)
        out.start()
        ctv = ct_ref[0]
        dv = d_ref[...]

        def ymm(i, c):
            rows = pl.ds(pl.multiple_of(i * rc, rc), rc)
            y_ref[rows, :] = _nn(buf[rows, :].astype(BF16), ctv) + dv * ug_ref[rows, :]
            return c

        lax.fori_loop(0, t // rc, ymm, 0)
        out.wait()

    return pl.pallas_call(
        body, name="s5_fwd", grid=(S5_BLOCKS,),
        in_specs=[pl.BlockSpec((t, 128), lambda i: (0, i)),
                  pl.BlockSpec((1, 128, 2 * ns), lambda i: (i, 0, 0)),
                  pl.BlockSpec((1, 2 * ns, 128), lambda i: (i, 0, 0)),
                  pl.BlockSpec((1, 1, ns), lambda i: (i, 0, 0)),
                  pl.BlockSpec((1, 1, ns), lambda i: (i, 0, 0)),
                  pl.BlockSpec((1, 128), lambda i: (0, i))],
        out_specs=[ANY, pl.BlockSpec((t, 128), lambda i: (0, i))],
        out_shape=[jax.ShapeDtypeStruct((S5_BLOCKS, t, 2 * ns), F32), jax.ShapeDtypeStruct((t, S5_WIDTH), F32)],
        scratch_shapes=[pltpu.VMEM((t, 2 * ns), F32), pltpu.SemaphoreType.DMA(())],
        compiler_params=_cparams(1),
    )(ugp, bd, ctd, ar4, ai4, dskip)


def _s5_bwd(dyp, ugp, xs, cd, bdt, ar4, ai4, dskip):
    t = ugp.shape[0]
    ls = t // S5_SEGS
    rc = min(512, t)
    ns = S5_BSTATE

    def body(dy_ref, ug_ref, xs_hbm, cd_ref, bdt_ref, ar_ref, ai_ref, d_ref,
             dug_ref, dbd_ref, dcd_ref, dd_ref, dar_ref, dai_ref, xbuf, lam, sem):
        cb = pl.program_id(0)
        load = pltpu.make_async_copy(xs_hbm.at[cb], xbuf, sem)
        load.start()
        cdv = cd_ref[0]

        def mm(i, c):
            rows = pl.ds(pl.multiple_of(i * rc, rc), rc)
            lam[rows, :] = _nn(dy_ref[rows, :].astype(BF16), cdv)
            return c

        lax.fori_loop(0, t // rc, mm, 0)
        arb = jnp.broadcast_to(ar_ref[0], (S5_SEGS, ns))
        aib = jnp.broadcast_to(ai_ref[0], (S5_SEGS, ns))

        def lam_step(j, lr, li):
            rows = pl.ds(pl.multiple_of(j * S5_SEGS, S5_SEGS), S5_SEGS)
            nr = arb * lr + aib * li + lam[rows, 0:ns]
            ni = arb * li - aib * lr + lam[rows, ns:2 * ns]
            return rows, nr, ni

        def pass1(jj, c):
            _, nr, ni = lam_step(ls - 1 - jj, *c)
            return nr, ni

        zero = jnp.zeros((S5_SEGS, ns), F32)
        er, ei = lax.fori_loop(0, ls, pass1, (zero, zero))
        pr, pi = _cpow(arb, aib, ls)
        init = _chain_segments(er, ei, pr, -pi, down=False)
        load.wait()

        def accumulate(acc, nr, ni, xpr, xpi):
            return acc[0] + nr * xpr + ni * xpi, acc[1] + ni * xpr - nr * xpi

        def pass2(jj, c):
            lr, li, accr, acci = c
            j = ls - 1 - jj
            rows, nr, ni = lam_step(j, lr, li)
            lam[rows, 0:ns] = nr
            lam[rows, ns:2 * ns] = ni
            prev = pl.ds(pl.multiple_of((j - 1) * S5_SEGS, S5_SEGS), S5_SEGS)
            accr, acci = accumulate((accr, acci), nr, ni, xbuf[prev, 0:ns], xbuf[prev, ns:2 * ns])
            return nr, ni, accr, acci

        lr, li, accr, acci = lax.fori_loop(0, ls - 1, pass2, (init[0], init[1], zero, zero))
        rows, nr, ni = lam_step(0, lr, li)
        lam[rows, 0:ns] = nr
        lam[rows, ns:2 * ns] = ni
        last = pl.ds((ls - 1) * S5_SEGS, S5_SEGS)
        accr, acci = accumulate((accr, acci), nr, ni,
                                _shift_rows(xbuf[last, 0:ns], True), _shift_rows(xbuf[last, ns:2 * ns], True))
        dar_ref[0] = jnp.sum(accr, axis=0, keepdims=True)
        dai_ref[0] = jnp.sum(acci, axis=0, keepdims=True)

        bdtv = bdt_ref[0]
        dv = d_ref[...]
        dbd_ref[...] = jnp.zeros_like(dbd_ref)
        dcd_ref[...] = jnp.zeros_like(dcd_ref)
        dd_ref[...] = jnp.zeros_like(dd_ref)

        def tail(i, c):
            rows = pl.ds(pl.multiple_of(i * rc, rc), rc)
            dy = dy_ref[rows, :]
            ug = ug_ref[rows, :]
            lb = lam[rows, :].astype(BF16)
            dug_ref[rows, :] = _nn(lb, bdtv) + dv * dy
            dbd_ref[0] += _tn(ug.astype(BF16), lb)
            dcd_ref[0] += _tn(dy.astype(BF16), xbuf[rows, :].astype(BF16))
            dd_ref[...] += jnp.sum(dy * ug, axis=0, keepdims=True)
            return c

        lax.fori_loop(0, t // rc, tail, 0)

    chan = pl.BlockSpec((t, 128), lambda i: (0, i))
    dense = pl.BlockSpec((1, 128, 2 * ns), lambda i: (i, 0, 0))
    vec = pl.BlockSpec((1, 1, ns), lambda i: (i, 0, 0))
    return pl.pallas_call(
        body, name="s5_bwd", grid=(S5_BLOCKS,),
        in_specs=[chan, chan, ANY, dense, pl.BlockSpec((1, 2 * ns, 128), lambda i: (i, 0, 0)), vec, vec,
                  pl.BlockSpec((1, 128), lambda i: (0, i))],
        out_specs=[chan, dense, dense, pl.BlockSpec((1, 128), lambda i: (0, i)), vec, vec],
        out_shape=[jax.ShapeDtypeStruct((t, S5_WIDTH), F32),
                   jax.ShapeDtypeStruct((S5_BLOCKS, 128, 2 * ns), F32),
                   jax.ShapeDtypeStruct((S5_BLOCKS, 128, 2 * ns), F32),
                   jax.ShapeDtypeStruct((1, S5_WIDTH), F32),
                   jax.ShapeDtypeStruct((S5_BLOCKS, 1, ns), F32),
                   jax.ShapeDtypeStruct((S5_BLOCKS, 1, ns), F32)],
        scratch_shapes=[pltpu.VMEM((t, 2 * ns), F32), pltpu.VMEM((t, 2 * ns), F32), pltpu.SemaphoreType.DMA(())],
        compiler_params=_cparams(1),
    )(dyp, ugp, xs, cd, bdt, ar4, ai4, dskip)


def _gla_common(q, k, alow, wup, bup):
    c = GLA_CHUNK
    pre = _nn(alow.astype(BF16), wup.astype(BF16)) + bup
    la = (jnp.minimum(pre, 0.0) - jnp.log(1.0 + jnp.exp(-jnp.abs(pre)))) * (1.0 / GLA_TAU)
    rr = lax.broadcasted_iota(jnp.int32, (c, c), 0)
    cc = lax.broadcasted_iota(jnp.int32, (c, c), 1)
    tril = (rr >= cc).astype(F32)
    bc = jnp.dot(tril, la, precision=HIGHEST, preferred_element_type=F32)
    bl = bc[c - 1:c, :]
    e_pos = jnp.exp(bc)
    e_neg = jnp.exp(-bc)
    e_end = jnp.exp(bl - bc)
    qt = q * (GLA_DK ** -0.5) * e_pos
    kt = k * e_neg
    ke = k * e_end
    decb = jnp.exp(lax.dot_general(la, jnp.ones((c, GLA_DV), F32), (((0,), (0,)), ((), ())),
                                   precision=HIGHEST, preferred_element_type=F32))
    lane = lax.broadcasted_iota(jnp.int32, (1, GLA_KEY), 1)
    masks = [((lane >= h * GLA_DK) & (lane < (h + 1) * GLA_DK)).astype(F32) for h in range(GLA_HEADS)]
    return dict(pre=pre, tril=tril, bc=bc, bl=bl, e_pos=e_pos, e_neg=e_neg, e_end=e_end,
                qt=qt, kt=kt, ke=ke, decb=decb, masks=masks)


def _gla_fwd(q, k, v, alow, wup, bup):
    t = q.shape[0]
    c = GLA_CHUNK
    n = t // c

    def body(q_ref, k_ref, v_ref, al_ref, wup_ref, bup_ref, o_ref, ss_ref, s_ref):
        i = pl.program_id(0)

        @pl.when(i == 0)
        def _():
            s_ref[...] = jnp.zeros_like(s_ref)

        m = _gla_common(q_ref[...], k_ref[...], al_ref[...], wup_ref[...], bup_ref[...])
        s = s_ref[...]
        ss_ref[0] = s
        sb = s.astype(BF16)
        ktb = m["kt"].astype(BF16)
        keb = m["ke"].astype(BF16)
        for h in range(GLA_HEADS):
            qm = (m["qt"] * m["masks"][h]).astype(BF16)
            vh = v_ref[:, h * GLA_DV:(h + 1) * GLA_DV].astype(BF16)
            p = (m["tril"] * _nt(qm, ktb)).astype(BF16)
            o_ref[:, h * GLA_DV:(h + 1) * GLA_DV] = _nn(p, vh) + _nn(qm, sb)
            rows = slice(h * GLA_DK, (h + 1) * GLA_DK)
            s_ref[rows, :] = m["decb"][rows, :] * s[rows, :] + _tn(keb, vh)[rows, :]

    return pl.pallas_call(
        body, name="gla_fwd", grid=(n,),
        in_specs=[_row_tile(c, GLA_KEY), _row_tile(c, GLA_KEY), _row_tile(c, GLA_VAL), _row_tile(c, GLA_RANK),
                  VMEM_FULL, VMEM_FULL],
        out_specs=[_row_tile(c, GLA_VAL), pl.BlockSpec((1, GLA_KEY, GLA_DV), lambda i: (i, 0, 0))],
        out_shape=[jax.ShapeDtypeStruct((t, GLA_VAL), F32), jax.ShapeDtypeStruct((n, GLA_KEY, GLA_DV), F32)],
        scratch_shapes=[pltpu.VMEM((GLA_KEY, GLA_DV), F32)],
        compiler_params=_cparams(1),
    )(q, k, v, alow, wup, bup)


def _gla_bwd(q, k, v, alow, wup, bup, ssave, do):
    t = q.shape[0]
    c = GLA_CHUNK
    n = t // c

    def body(q_ref, k_ref, v_ref, al_ref, wup_ref, bup_ref, ss_ref, do_ref,
             dq_ref, dk_ref, dv_ref, dal_ref, dwup_ref, dbup_ref, ds_ref):
        i = pl.program_id(0)

        @pl.when(i == 0)
        def _():
            ds_ref[...] = jnp.zeros_like(ds_ref)
            dwup_ref[...] = jnp.zeros_like(dwup_ref)
            dbup_ref[...] = jnp.zeros_like(dbup_ref)

        alow_v = al_ref[...]
        wup_v = wup_ref[...]
        m = _gla_common(q_ref[...], k_ref[...], alow_v, wup_v, bup_ref[...])
        s = ss_ref[0]
        ds_in = ds_ref[...]
        sb = s.astype(BF16)
        dsb = ds_in.astype(BF16)
        qt, kt, ke = m["qt"], m["kt"], m["ke"]
        ktb = kt.astype(BF16)
        dqt = jnp.zeros((c, GLA_KEY), F32)
        dkt = jnp.zeros((c, GLA_KEY), F32)
        dke = jnp.zeros((c, GLA_KEY), F32)
        for h in range(GLA_HEADS):
            mask = m["masks"][h]
            qm = (qt * mask).astype(BF16)
            km = (kt * mask).astype(BF16)
            kem = (ke * mask).astype(BF16)
            cols = slice(h * GLA_DV, (h + 1) * GLA_DV)
            vh = v_ref[:, cols].astype(BF16)
            doh = do_ref[:, cols].astype(BF16)
            p = (m["tril"] * _nt(qm, ktb)).astype(BF16)
            dp = (m["tril"] * _nt(doh, vh)).astype(BF16)
            dv_ref[:, cols] = _tn(p, doh) + _nn(kem, dsb)
            dqt = dqt + _nn(dp, km) + _nt(doh, sb) * mask
            dkt = dkt + _tn(dp, qm)
            dke = dke + _nt(vh, dsb) * mask
            rows = slice(h * GLA_DK, (h + 1) * GLA_DK)
            ds_ref[rows, :] = m["decb"][rows, :] * ds_in[rows, :] + _tn(qm, doh)[rows, :]
        ddec = lax.dot_general(jnp.ones((8, GLA_DV), F32), ds_in * s, (((1,), (1,)), ((), ())),
                               precision=HIGHEST, preferred_element_type=F32)[0:1, :]
        dq_ref[...] = dqt * m["e_pos"] * (GLA_DK ** -0.5)
        dk_ref[...] = dkt * m["e_neg"] + dke * m["e_end"]
        dkeke = dke * ke
        dbl = jnp.sum(dkeke, axis=0, keepdims=True) + ddec * jnp.exp(m["bl"])
        last = (lax.broadcasted_iota(jnp.int32, (c, 1), 0) == c - 1).astype(F32)
        db_tot = dqt * qt - dkt * kt - dkeke + last * dbl
        dla = lax.dot_general(m["tril"], db_tot, (((0,), (0,)), ((), ())),
                              precision=HIGHEST, preferred_element_type=F32)
        dpre = dla * (1.0 / GLA_TAU) * jax.nn.sigmoid(-m["pre"])
        dpb = dpre.astype(BF16)
        dal_ref[...] = _nt(dpb, wup_v.astype(BF16))
        dwup_ref[...] += _tn(alow_v.astype(BF16), dpb)
        dbup_ref[...] += jnp.sum(dpre, axis=0, keepdims=True)

    def rev(d):
        return pl.BlockSpec((c, d), lambda i: (n - 1 - i, 0))

    return pl.pallas_call(
        body, name="gla_bwd", grid=(n,),
        in_specs=[rev(GLA_KEY), rev(GLA_KEY), rev(GLA_VAL), rev(GLA_RANK), VMEM_FULL, VMEM_FULL,
                  pl.BlockSpec((1, GLA_KEY, GLA_DV), lambda i: (n - 1 - i, 0, 0)), rev(GLA_VAL)],
        out_specs=[rev(GLA_KEY), rev(GLA_KEY), rev(GLA_VAL), rev(GLA_RANK),
                   pl.BlockSpec((GLA_RANK, GLA_KEY), lambda i: (0, 0)), _acc_row(GLA_KEY)],
        out_shape=[jax.ShapeDtypeStruct((t, GLA_KEY), F32), jax.ShapeDtypeStruct((t, GLA_KEY), F32),
                   jax.ShapeDtypeStruct((t, GLA_VAL), F32), jax.ShapeDtypeStruct((t, GLA_RANK), F32),
                   jax.ShapeDtypeStruct((GLA_RANK, GLA_KEY), F32), jax.ShapeDtypeStruct((1, GLA_KEY), F32)],
        scratch_shapes=[pltpu.VMEM((GLA_KEY, GLA_DV), F32)],
        compiler_params=_cparams(1),
    )(q, k, v, alow, wup, bup, ssave, do)


def _post_math(y, o, r, gs5, ggla, wg, bg, gn, ps5t, pglat):
    y2 = y * y
    th = jnp.tanh(GELU_C0 * (y + GELU_C1 * y * y2))
    z5 = 0.5 * y * (1.0 + th)
    z5b = z5.astype(BF16)
    gate = jax.nn.sigmoid(_nn(z5b, wg) + bg)
    ys5 = z5 * gate
    rs, on = [], []
    for h in range(GLA_HEADS):
        oh = o[:, h * GLA_DV:(h + 1) * GLA_DV]
        rh = lax.rsqrt(jnp.mean(oh * oh, axis=-1, keepdims=True) + EPS)
        rs.append(rh)
        on.append(oh * rh)
    on = jnp.concatenate(on, axis=-1)
    sr = jax.nn.sigmoid(r)
    silu_r = r * sr
    ygla = on * gn * silu_r
    ys5b, yglab = ys5.astype(BF16), ygla.astype(BF16)
    m5 = _nt(ys5b, ps5t)
    mg = _nt(yglab, pglat)
    s5g, glag = jax.nn.sigmoid(gs5), jax.nn.sigmoid(ggla)
    merged = s5g * m5 + glag * mg
    return dict(y2=y2, th=th, z5=z5, z5b=z5b, gate=gate, ys5b=ys5b, yglab=yglab, rs=rs, on=on, sr=sr,
                silu_r=silu_r, m5=m5, mg=mg, s5g=s5g, glag=glag, mergedb=merged.astype(BF16))


def _mix_post_fwd(y, o, r, gs5, ggla, h1, wg, bg, gn, ps5t, pglat, wout):
    t = o.shape[0]
    tm = _seg_tile(t, _tile(t))

    def body(y_ref, o_ref, r_ref, gs5_ref, ggla_ref, h1_ref, wg_ref, bg_ref, gn_ref, ps_ref, pg_ref, wo_ref, h2_ref):
        m = _post_math(y_ref[...], o_ref[...], r_ref[...], gs5_ref[...], ggla_ref[...],
                       wg_ref[...], bg_ref[...], gn_ref[...], ps_ref[...], pg_ref[...])
        h2_ref[...] = h1_ref[...] + _nn(m["mergedb"], wo_ref[...])

    return pl.pallas_call(
        body, name="mix_post_fwd", grid=(t // tm,),
        in_specs=[_perm_tile(t, tm, 512)] + [_row_tile(tm, 512)] * 2 + [_row_tile(tm, D_MODEL)] * 3
        + [VMEM_FULL, _acc_row(512), _acc_row(512), VMEM_FULL, VMEM_FULL, VMEM_FULL],
        out_specs=_row_tile(tm, D_MODEL),
        out_shape=jax.ShapeDtypeStruct((t, D_MODEL), F32),
        compiler_params=_cparams(1),
    )(y, o, r, gs5, ggla, h1, wg, bg, gn, ps5t, pglat, wout)


def _mix_post_bwd(y, o, r, gs5, ggla, dh2, wg, bg, gn, ps5t, pglat, wout):
    t = o.shape[0]
    tm = _seg_tile(t, _tile(t) // 2)

    def body(y_ref, o_ref, r_ref, gs5_ref, ggla_ref, dh2_ref, wg_ref, bg_ref, gn_ref, ps_ref, pg_ref, wo_ref,
             dy_ref, do_ref, dr_ref, dgs5_ref, dggla_ref, dbg_ref, dgn_ref,
             z5b_ref, dgp_ref, ys5b_ref, dm5b_ref, yglab_ref, dmgb_ref, mergedb_ref, dh2b_ref):
        i = pl.program_id(0)
        yv, ov, rv = y_ref[...], o_ref[...], r_ref[...]
        wg, gn, ps5t, pglat = wg_ref[...], gn_ref[...], ps_ref[...], pg_ref[...]
        m = _post_math(yv, ov, rv, gs5_ref[...], ggla_ref[...], wg, bg_ref[...], gn, ps5t, pglat)
        dh2b = dh2_ref[...].astype(BF16)
        dmerged = _nt(dh2b, wo_ref[...])
        s5g, glag = m["s5g"], m["glag"]
        dgs5_ref[...] = dmerged * m["m5"] * s5g * (1.0 - s5g)
        dggla_ref[...] = dmerged * m["mg"] * glag * (1.0 - glag)
        dm5b = (dmerged * s5g).astype(BF16)
        dmgb = (dmerged * glag).astype(BF16)
        dys5 = _nn(dm5b, ps5t)
        dygla = _nn(dmgb, pglat)
        gate, z5, th = m["gate"], m["z5"], m["th"]
        dgpre = dys5 * z5 * gate * (1.0 - gate)
        dgpb = dgpre.astype(BF16)
        dz5 = dys5 * gate + _nt(dgpb, wg)
        dgelu = 0.5 * (1.0 + th) + 0.5 * yv * (1.0 - th * th) * GELU_C0 * (1.0 + 3.0 * GELU_C1 * m["y2"])
        dy_ref[...] = dz5 * dgelu
        on, sr, silu_r = m["on"], m["sr"], m["silu_r"]
        dr_ref[...] = dygla * on * gn * sr * (1.0 + rv * (1.0 - sr))
        dgn = jnp.sum(dygla * on * silu_r, axis=0, keepdims=True)
        don = dygla * gn * silu_r
        for h in range(GLA_HEADS):
            cols = slice(h * GLA_DV, (h + 1) * GLA_DV)
            donh, onh = don[:, cols], on[:, cols]
            do_ref[:, cols] = m["rs"][h] * (donh - onh * jnp.mean(donh * onh, axis=-1, keepdims=True))

        @pl.when(i == 0)
        def _():
            dbg_ref[...] = jnp.zeros_like(dbg_ref)
            dgn_ref[...] = jnp.zeros_like(dgn_ref)

        dbg_ref[...] += jnp.sum(dgpre, axis=0, keepdims=True)
        dgn_ref[...] += dgn
        z5b_ref[...] = m["z5b"]
        dgp_ref[...] = dgpb
        ys5b_ref[...] = m["ys5b"]
        dm5b_ref[...] = dm5b
        yglab_ref[...] = m["yglab"]
        dmgb_ref[...] = dmgb
        mergedb_ref[...] = m["mergedb"]
        dh2b_ref[...] = dh2b

    def f32(d):
        return jax.ShapeDtypeStruct((t, d), F32)

    def b16(d):
        return jax.ShapeDtypeStruct((t, d), BF16)

    widths = (512, 512, 512, 1024, 512, 1024, 1024, 1024)
    return pl.pallas_call(
        body, name="mix_post_bwd", grid=(t // tm,),
        in_specs=[_perm_tile(t, tm, 512)] + [_row_tile(tm, 512)] * 2 + [_row_tile(tm, D_MODEL)] * 3
        + [VMEM_FULL, _acc_row(512), _acc_row(512), VMEM_FULL, VMEM_FULL, VMEM_FULL],
        out_specs=[_perm_tile(t, tm, 512)] + [_row_tile(tm, 512)] * 2 + [_row_tile(tm, D_MODEL)] * 2
        + [_acc_row(512)] * 2 + [_row_tile(tm, w) for w in widths],
        out_shape=[jax.ShapeDtypeStruct((t // S5_SEGS, S5_SEGS * 512), F32)] + [f32(512)] * 2 + [f32(D_MODEL)] * 2
        + [jax.ShapeDtypeStruct((1, 512), F32)] * 2
        + [b16(w) for w in widths],
        compiler_params=_cparams(1),
    )(y, o, r, gs5, ggla, dh2, wg, bg, gn, ps5t, pglat, wout)


def _head(h3, g, target):
    t = h3.shape[0]
    tm = _tile(t)

    def body(h_ref, g_ref, t_ref, loss_ref, dh_ref, dg_ref):
        i = pl.program_id(0)
        gv = g_ref[...]
        xhat, r = _rms_parts(h_ref[...])
        err = xhat * gv - t_ref[...]
        dx, dg = _rms_bwd(err * (1.0 / D_MODEL), gv, xhat, r)
        dh_ref[...] = dx

        @pl.when(i == 0)
        def _():
            loss_ref[...] = jnp.zeros_like(loss_ref)
            dg_ref[...] = jnp.zeros_like(dg_ref)

        loss_ref[...] += (0.5 / D_MODEL) * jnp.sum(jnp.sum(err * err, axis=1, keepdims=True), axis=0, keepdims=True)
        dg_ref[...] += dg

    return pl.pallas_call(
        body, name="head", grid=(t // tm,),
        in_specs=[_row_tile(tm, D_MODEL), _acc_row(D_MODEL), _row_tile(tm, D_MODEL)],
        out_specs=[pl.BlockSpec((1, 1), lambda i: (0, 0)), _row_tile(tm, D_MODEL), _acc_row(D_MODEL)],
        out_shape=[jax.ShapeDtypeStruct((1, 1), F32), jax.ShapeDtypeStruct((t, D_MODEL), F32),
                   jax.ShapeDtypeStruct((1, D_MODEL), F32)],
        compiler_params=_cparams(1),
    )(h3, g, target)


ADAM_TILE_ELEMS = 256 * 1024


def _adamw(w, g, m, v, name):
    rows, cols = w.shape
    tr = rows
    while tr * cols > ADAM_TILE_ELEMS and tr % 16 == 0:
        tr //= 2

    def body(w_ref, g_ref, m_ref, v_ref, d_ref, nm_ref, nv_ref):
        gv = g_ref[...]
        nm = ADAM_B1 * m_ref[...] + (1.0 - ADAM_B1) * gv
        nv = ADAM_B2 * v_ref[...] + (1.0 - ADAM_B2) * (gv * gv)
        m_hat = nm / (1.0 - ADAM_B1 ** ADAM_STEP)
        v_hat = nv / (1.0 - ADAM_B2 ** ADAM_STEP)
        d_ref[...] = -ADAM_LR * (m_hat / (jnp.sqrt(v_hat) + ADAM_EPS) + ADAM_WD * w_ref[...])
        nm_ref[...] = nm
        nv_ref[...] = nv

    spec = pl.BlockSpec((tr, cols), lambda i: (i, 0))
    sh = jax.ShapeDtypeStruct((rows, cols), F32)
    return pl.pallas_call(body, name=name, grid=(rows // tr,), in_specs=[spec] * 4, out_specs=[spec] * 3,
                          out_shape=[sh] * 3, compiler_params=_cparams(1))(w, g, m, v)


def _peers():
    x, y, c = lax.axis_index("x"), lax.axis_index("y"), lax.axis_index("c")
    out = []
    for k in range(1, N_DEV):
        px = 1 - x if k & 4 else x
        py = 1 - y if k & 2 else y
        pc = 1 - c if k & 1 else c
        out.append(((px, py, pc), 4 * px + 2 * py + pc))
    return 4 * x + 2 * y + c, out


def _exchange_start(src_refs, out_refs, send_sems, recv_sems, local_sems, scatter):
    me, peers = _peers()
    started = []
    for a, (src_ref, out_ref) in enumerate(zip(src_refs, out_refs)):
        def mine(idx, src_ref=src_ref):
            return src_ref.at[idx] if scatter else src_ref

        local = pltpu.make_async_copy(mine(me), out_ref.at[me], local_sems.at[a])
        local.start()
        sends, recvs = [], []
        for k, (dev, idx) in enumerate(peers):
            cp = pltpu.make_async_remote_copy(src_ref=mine(idx), dst_ref=out_ref.at[me],
                                              send_sem=send_sems.at[a, k], recv_sem=recv_sems.at[a, k],
                                              device_id=dev, device_id_type=pl.DeviceIdType.MESH)
            cp.start()
            sends.append(cp)
            recvs.append(pltpu.make_async_remote_copy(src_ref=mine(idx), dst_ref=out_ref.at[idx],
                                                      send_sem=send_sems.at[a, k], recv_sem=recv_sems.at[a, k],
                                                      device_id=dev, device_id_type=pl.DeviceIdType.MESH))
        started.append((local, sends, recvs))
    return started


def _exchange_wait(started):
    for local, sends, recvs in started:
        for cp in recvs:
            cp.wait_recv()
        for cp in sends:
            cp.wait_send()
        local.wait()


def _exchange_sems(n_arrays):
    return [pltpu.SemaphoreType.DMA((n_arrays, N_DEV - 1)), pltpu.SemaphoreType.DMA((n_arrays, N_DEV - 1)),
            pltpu.SemaphoreType.DMA((n_arrays,))]


def _exchange_shapes(srcs, scatter):
    return [jax.ShapeDtypeStruct((N_DEV,) + tuple(s.shape[1:] if scatter else s.shape), s.dtype) for s in srcs]


def _exchange(srcs, scatter, name):
    n = len(srcs)

    def body(*refs):
        _exchange_wait(_exchange_start(refs[:n], refs[n:2 * n], *refs[2 * n:], scatter=scatter))

    return pl.pallas_call(
        body, name=name, in_specs=[ANY] * n, out_specs=[ANY] * n,
        out_shape=_exchange_shapes(srcs, scatter), scratch_shapes=_exchange_sems(n),
    )(*srcs)


def _sum_slabs(slabs, name):
    n = slabs.shape[0]

    def body(s_ref, o_ref):
        acc = s_ref[0].astype(F32)
        for s in range(1, n):
            acc = acc + s_ref[s].astype(F32)
        o_ref[...] = acc

    return pl.pallas_call(
        body, name=name, in_specs=[VMEM_FULL], out_specs=VMEM_FULL,
        out_shape=jax.ShapeDtypeStruct(slabs.shape[1:], F32),
        compiler_params=pltpu.CompilerParams(vmem_limit_bytes=VMEM_LIMIT_BYTES),
    )(slabs)


BIG = ("ffn1_w1", "ffn1_w3", "ffn1_w2", "w_in", "s5_glu_w", "gla_a_up_w", "proj_s5", "proj_gla", "w_out",
       "ffn2_w1", "ffn2_w3", "ffn2_w2")
GROUPS = (("ffn1_w1", "ffn1_w3", "ffn1_w2"),
          ("w_in", "s5_glu_w", "gla_a_up_w", "proj_s5", "proj_gla", "w_out"),
          ("ffn2_w1", "ffn2_w3", "ffn2_w2"))
W_IN_ROWS = 514
W_IN_PAD = 528
UP_COLS = 32
COL_SHARDED = ("ffn1_w1", "ffn1_w3", "w_in", "proj_s5", "proj_gla", "ffn2_w1", "ffn2_w3")

SMALL = ("ffn1_norm", "mix_norm", "s5_lambda_re", "s5_lambda_im", "s5_log_dt", "s5_b_re", "s5_b_im", "s5_c_re",
         "s5_c_im", "s5_d", "s5_glu_b", "gla_a_up_b", "gla_out_norm", "ffn2_norm", "final_norm")
SMALL_SHAPES = dict(ffn1_norm=(1, 1024), mix_norm=(1, 1024), s5_lambda_re=(1, 32, 64), s5_lambda_im=(1, 32, 64),
                    s5_log_dt=(1, 32), s5_b_re=(1, 32, 64, 16), s5_b_im=(1, 32, 64, 16), s5_c_re=(1, 32, 16, 64),
                    s5_c_im=(1, 32, 16, 64), s5_d=(1, 32, 16), s5_glu_b=(1, 512), gla_a_up_b=(1, 256),
                    gla_out_norm=(1, 512), ffn2_norm=(1, 1024), final_norm=(1024,))
SMALL_N = sum(math.prod(s) for s in SMALL_SHAPES.values())
SMALL_R = -(-SMALL_N // (64 * 1024)) * 64


def _shard_rows(name, a):
    if name == "gla_a_up_w":
        return jnp.pad(a, ((0, 0), (0, 128 - UP_COLS)))
    if name in COL_SHARDED:
        a = a.T
    if name == "w_in":
        return jnp.pad(a, ((0, W_IN_PAD - W_IN_ROWS), (0, 0)))
    return a.reshape(-1, 1024)


def _unshard_rows(name, rows, shape):
    if name == "gla_a_up_w":
        return rows[:, :UP_COLS]
    if name == "w_in":
        rows = rows[:W_IN_ROWS]
    if name in COL_SHARDED:
        return rows.reshape(shape[1], shape[0]).T
    return rows.reshape(shape)


def _pack_small(vals):
    flat = jnp.concatenate([vals[n].reshape(-1).astype(F32) for n in SMALL])
    return jnp.pad(flat, (0, SMALL_R * 1024 - SMALL_N)).reshape(SMALL_R, 1024)


def _unpack_small(slab):
    flat = slab.reshape(-1)
    out, off = {}, 0
    for n in SMALL:
        size = math.prod(SMALL_SHAPES[n])
        out[n] = flat[off:off + size].reshape(SMALL_SHAPES[n])
        off += size
    return out


FULL_SHAPES = dict(w_in=(IN_COLS, D_MODEL), s5_glu_w=(S5_WIDTH, S5_WIDTH), gla_a_up_w=(GLA_RANK, GLA_KEY),
                   proj_s5=(D_MODEL, S5_WIDTH), proj_gla=(D_MODEL, GLA_VAL), w_out=(D_MODEL, D_MODEL))


def _full_weight(name, gathered):
    if name == "gla_a_up_w":
        return gathered[:, :, :UP_COLS].transpose(1, 0, 2).reshape(GLA_RANK, GLA_KEY)
    if name == "w_in":
        gathered = gathered[:, :W_IN_ROWS]
    return gathered.reshape(FULL_SHAPES.get(name, (D_FF, D_MODEL)))


def _grad_slabs(name, g):
    if name == "gla_a_up_w":
        g = g.reshape(GLA_RANK, N_DEV, UP_COLS).transpose(1, 0, 2)
        return jnp.pad(g, ((0, 0), (0, 0), (0, 128 - UP_COLS))).astype(BF16)
    if name == "w_in":
        return jnp.pad(g.reshape(N_DEV, W_IN_ROWS, D_MODEL), ((0, 0), (0, W_IN_PAD - W_IN_ROWS), (0, 0)))
    return g.reshape(N_DEV, -1, 1024)


def _s5_dense(re, im, sign_im):
    eye = jnp.eye(8, dtype=F32)

    def one(a):
        a = a.reshape(S5_BLOCKS, 8, S5_GROUP, S5_STATE)
        return jnp.einsum("cghp,gk->cghkp", a, eye).reshape(S5_BLOCKS, 128, S5_BSTATE)

    return jnp.concatenate([one(re), sign_im * one(im)], axis=-1)


def _s5_undense(d):
    eye = jnp.eye(8, dtype=F32)

    def one(a):
        a = a.reshape(S5_BLOCKS, 8, S5_GROUP, 8, S5_STATE)
        return jnp.einsum("cghkp,gk->cghp", a, eye).reshape(S5_GROUPS, S5_GROUP, S5_STATE)

    return one(d[..., :S5_BSTATE]), one(d[..., S5_BSTATE:])


def _local_step(x, target, p, w):
    g1, gm, g2 = p["ffn1_norm"], p["mix_norm"], p["ffn2_norm"]
    gf = p["final_norm"].reshape(1, D_MODEL)
    lre, lim = p["s5_lambda_re"][0], p["s5_lambda_im"][0]
    ldt = p["s5_log_dt"][0].reshape(S5_GROUPS, 1)
    bre = p["s5_b_re"][0].transpose(2, 0, 1)
    bim = p["s5_b_im"][0].transpose(2, 0, 1)
    cre, cim = p["s5_c_re"][0], p["s5_c_im"][0]
    dskip = p["s5_d"][0].reshape(1, S5_WIDTH)
    bg, bup, gn = p["s5_glu_b"], p["gla_a_up_b"], p["gla_out_norm"]
    wup = w["gla_a_up_w"].astype(F32)

    h1 = _ffn_fwd(x, g1, w["ffn1_w1"], w["ffn1_w3"], w["ffn1_w2"], "ffn1_fwd")
    t = x.shape[0]
    ls = t // S5_SEGS
    u, u_perm, s5in, q, k, v, r, alow, gs5, ggla = _mix_pre_fwd(h1, gm, w["w_in"])
    ar, ai, bbr, bbi = _s5_disc(lre, lim, ldt, bre, bim)
    bd = _s5_dense(bbr.transpose(1, 0, 2), bbi.transpose(1, 0, 2), 1.0)
    cd = _s5_dense(cre, cim, -1.0)
    bd16, cd16 = bd.astype(BF16), cd.astype(BF16)
    bdt16, ctd16 = bd16.transpose(0, 2, 1), cd16.transpose(0, 2, 1)
    ar4 = ar.reshape(S5_BLOCKS, 1, S5_BSTATE)
    ai4 = ai.reshape(S5_BLOCKS, 1, S5_BSTATE)
    ugp = s5in.reshape(t, S5_WIDTH)
    xs, yp = _s5_fwd(ugp, bd16, ctd16, ar4, ai4, dskip)
    y = yp.reshape(ls, S5_SEGS * S5_WIDTH)
    o, ssave = _gla_fwd(q, k, v, alow, wup, bup)
    post_w = (w["s5_glu_w"], bg, gn, w["proj_s5"], w["proj_gla"], w["w_out"])
    h2 = _mix_post_fwd(y, o, r, gs5, ggla, h1, *post_w)
    h3 = _ffn_fwd(h2, g2, w["ffn2_w1"], w["ffn2_w3"], w["ffn2_w2"], "ffn2_fwd")
    loss, dh3, dgf = _head(h3, gf, target)

    big, small = {}, {}
    small["final_norm"] = dgf.reshape(D_MODEL)
    dh2, dg2, da3, db3, s3, n2, dhh2 = _ffn_bwd(h2, dh3, g2, w["ffn2_w1"], w["ffn2_w3"], w["ffn2_w2"], "ffn2_bwd")
    small["ffn2_norm"] = dg2
    big["ffn2_w1"] = _mm_tn(da3, n2, "ffn2_dw1")
    big["ffn2_w3"] = _mm_tn(db3, n2, "ffn2_dw3")
    big["ffn2_w2"] = _mm_tn(s3, dhh2, "ffn2_dw2")
    (dy, do, dr, dgs5, dggla, dbg, dgn,
     z5b, dgpb, ys5b, dm5b, yglab, dmgb, mergedb, dh2b) = _mix_post_bwd(y, o, r, gs5, ggla, dh2, *post_w)
    small["s5_glu_b"] = dbg
    small["gla_out_norm"] = dgn
    big["s5_glu_w"] = _mm_tn(z5b, dgpb, "glu_dw")
    big["proj_s5"] = _mm_tn(dm5b, ys5b, "proj_s5_dw")
    big["proj_gla"] = _mm_tn(dmgb, yglab, "proj_gla_dw")
    big["w_out"] = _mm_tn(mergedb, dh2b, "w_out_dw")
    dq, dk, dv, dalow, dwup, dbup = _gla_bwd(q, k, v, alow, wup, bup, ssave, do)
    big["gla_a_up_w"] = dwup
    small["gla_a_up_b"] = dbup
    dugp, dbd, dcd, dd, dar4, dai4 = _s5_bwd(dy.reshape(t, S5_WIDTH), ugp, xs, cd16, bdt16, ar4, ai4, dskip)
    ds5in = dugp.reshape(ls, S5_SEGS * S5_WIDTH)
    dbbr, dbbi = _s5_undense(dbd)
    dcre, dcim_neg = _s5_undense(dcd)
    glre, glim, gldt, gbre, gbim = _s5_disc_bwd(
        lre, lim, ldt, bre, bim, dar4.reshape(S5_GROUPS, S5_STATE), dai4.reshape(S5_GROUPS, S5_STATE),
        dbbr.transpose(1, 0, 2), dbbi.transpose(1, 0, 2))
    small["s5_lambda_re"] = glre[None]
    small["s5_lambda_im"] = glim[None]
    small["s5_log_dt"] = gldt.reshape(1, S5_GROUPS)
    small["s5_b_re"] = gbre.transpose(1, 2, 0)[None]
    small["s5_b_im"] = gbim.transpose(1, 2, 0)[None]
    small["s5_c_re"] = dcre[None]
    small["s5_c_im"] = -dcim_neg[None]
    small["s5_d"] = dd.reshape(1, S5_GROUPS, S5_GROUP)
    dz = (ds5in, dq, dk, dv, dr, dalow, dgs5, dggla)
    dh1, dgm = _mix_pre_bwd(h1, gm, w["w_in"], dh2, dz)
    small["mix_norm"] = dgm
    dw_in = [_mm_tn(dugp, u_perm.reshape(t, D_MODEL), "w_in_dw0")]
    dw_in += [_mm_tn(d, u, "w_in_dw%d" % i) for i, d in enumerate(dz) if i]
    big["w_in"] = jnp.concatenate(dw_in, axis=0)
    dx, dg1, da3, db3, s3, n1, dhh1 = _ffn_bwd(x, dh1, g1, w["ffn1_w1"], w["ffn1_w3"], w["ffn1_w2"], "ffn1_bwd")
    small["ffn1_norm"] = dg1
    big["ffn1_w1"] = _mm_tn(da3, n1, "ffn1_dw1")
    big["ffn1_w3"] = _mm_tn(db3, n1, "ffn1_dw3")
    big["ffn1_w2"] = _mm_tn(s3, dhh1, "ffn1_dw2")
    return loss[0, 0], dx, big, small


NAMES = ("ffn1_norm", "ffn1_w1", "ffn1_w3", "ffn1_w2", "mix_norm", "w_in", "s5_lambda_re", "s5_lambda_im",
         "s5_log_dt", "s5_b_re", "s5_b_im", "s5_c_re", "s5_c_im", "s5_d", "s5_glu_w", "s5_glu_b", "gla_a_up_w",
         "gla_a_up_b", "gla_out_norm", "proj_s5", "proj_gla", "w_out", "ffn2_norm", "ffn2_w1", "ffn2_w3", "ffn2_w2",
         "final_norm")


def kernel(*args):
    nw = len(NAMES)
    x = args[0][0]
    wts = dict(zip(NAMES, args[1:1 + nw]))
    target = args[1 + nw][0]
    mom = dict(zip(NAMES, args[2 + nw:2 + 2 * nw]))
    var = dict(zip(NAMES, args[2 + 2 * nw:2 + 3 * nw]))

    shards = {n: wts[n][0] for n in BIG}
    full = {}
    for gi, group in enumerate(GROUPS):
        got = _exchange([_shard_rows(n, shards[n]).astype(BF16) for n in group], False, "gather_weights%d" % gi)
        full.update({n: _full_weight(n, g) for n, g in zip(group, got)})
    loss, dx, big, small = _local_step(x, target, {n: wts[n] for n in SMALL}, full)
    loss = lax.psum(loss, ("x", "y", "c"))

    grad, delta, new_m, new_v = {}, {}, {}, {}
    for gi, group in reversed(list(enumerate(GROUPS))):
        landed = _exchange([_grad_slabs(n, big[n]) for n in group], True, "scatter_grads%d" % gi)
        for n, slabs in zip(group, landed):
            g = _unshard_rows(n, _sum_slabs(slabs, "sum_" + n), shards[n].shape)
            grad[n] = g[None]
            delta[n], new_m[n], new_v[n] = (a[None] for a in _adamw(shards[n], g, mom[n][0], var[n][0], "adamw_" + n))

    part = _pack_small(small).reshape(N_DEV, SMALL_R // N_DEV, 1024)
    mine = _sum_slabs(_exchange([part], True, "scatter_small")[0], "sum_small")
    g_small = _exchange([mine], False, "gather_small")[0].reshape(SMALL_R, 1024)
    d_small, m_small, v_small = _adamw(_pack_small({n: wts[n] for n in SMALL}), g_small,
                                       _pack_small({n: mom[n] for n in SMALL}),
                                       _pack_small({n: var[n] for n in SMALL}), "adamw_small")
    for out, slab in ((grad, g_small), (delta, d_small), (new_m, m_small), (new_v, v_small)):
        out.update(_unpack_small(slab))
    return (loss, dx[None], *(d[n] for d in (grad, delta, new_m, new_v) for n in NAMES))
```

```python
import functools
import math

import jax
import jax.numpy as jnp
from jax import lax
from jax.experimental import pallas as pl
from jax.experimental.pallas import tpu as pltpu

F32, BF16 = jnp.float32, jnp.bfloat16
HIGHEST = lax.Precision.HIGHEST

D_MODEL = 1024
D_FF = 2816
N_DEV = 8
S5_WIDTH, S5_GROUPS, S5_GROUP, S5_STATE = 512, 32, 16, 64
S5_BLOCKS = 4
S5_BSTATE = 512
S5_SEGS = 8
GLA_HEADS, GLA_DK, GLA_DV = 4, 64, 128
GLA_KEY, GLA_VAL, GLA_RANK, GLA_CHUNK = 256, 512, 16, 64
GLA_TAU = 16.0
EPS = 1e-6
IN_SIZES = (512, 256, 256, 512, 512, 16, 1024, 1024)
IN_OFFS = tuple(sum(IN_SIZES[:i]) for i in range(len(IN_SIZES)))
IN_COLS = sum(IN_SIZES)
ADAM_LR, ADAM_B1, ADAM_B2, ADAM_EPS, ADAM_WD, ADAM_STEP = 0.001, 0.9, 0.999, 1e-08, 0.01, 10
GELU_C0 = math.sqrt(2.0 / math.pi)
GELU_C1 = 0.044715

FFN_FT = 256
VMEM_LIMIT_BYTES = 56 * 1024 * 1024

VMEM_FULL = pl.BlockSpec(memory_space=pltpu.VMEM)
ANY = pl.BlockSpec(memory_space=pl.ANY)


def _cparams(n_grid):
    return pltpu.CompilerParams(dimension_semantics=("arbitrary",) * n_grid, vmem_limit_bytes=VMEM_LIMIT_BYTES)


def _tile(t):
    return 512 if t >= 1024 else t // 2


def _nn(a, b):
    return jnp.dot(a, b, preferred_element_type=F32)


def _nt(a, b):
    return lax.dot_general(a, b, (((1,), (1,)), ((), ())), preferred_element_type=F32)


def _tn(a, b):
    return lax.dot_general(a, b, (((0,), (0,)), ((), ())), preferred_element_type=F32)


def _rms_parts(x):
    r = lax.rsqrt(jnp.mean(x * x, axis=-1, keepdims=True) + EPS)
    return x * r, r


def _rms_bwd(dn, g, xhat, r):
    dxh = dn * g
    dx = r * (dxh - xhat * jnp.mean(dxh * xhat, axis=-1, keepdims=True))
    return dx, jnp.sum(dn * xhat, axis=0, keepdims=True)


def _peers():
    x, y, c = lax.axis_index("x"), lax.axis_index("y"), lax.axis_index("c")
    out = []
    for k in range(1, N_DEV):
        px = 1 - x if k & 4 else x
        py = 1 - y if k & 2 else y
        pc = 1 - c if k & 1 else c
        out.append(((px, py, pc), 4 * px + 2 * py + pc))
    return 4 * x + 2 * y + c, out


def _exchange_copies(src_refs, out_refs, send_sems, recv_sems, local_sems, scatter, with_recvs):
    me, peers = _peers()
    locals_, sends, recvs = [], [], []
    for a, (src_ref, out_ref) in enumerate(zip(src_refs, out_refs)):
        def mine(idx, src_ref=src_ref):
            return src_ref.at[idx] if scatter else src_ref

        locals_.append(pltpu.make_async_copy(mine(me), out_ref.at[me], local_sems.at[a]))
        for k, (dev, idx) in enumerate(peers):
            sends.append(pltpu.make_async_remote_copy(
                src_ref=mine(idx), dst_ref=out_ref.at[me], send_sem=send_sems.at[a, k], recv_sem=recv_sems.at[a, k],
                device_id=dev, device_id_type=pl.DeviceIdType.MESH))
            if with_recvs:
                recvs.append(pltpu.make_async_remote_copy(
                    src_ref=mine(idx), dst_ref=out_ref.at[idx], send_sem=send_sems.at[a, k],
                    recv_sem=recv_sems.at[a, k], device_id=dev, device_id_type=pl.DeviceIdType.MESH))
    return locals_, sends, recvs


def _exchange_start(*refs, scatter):
    locals_, sends, _ = _exchange_copies(*refs, scatter=scatter, with_recvs=False)
    for cp in locals_ + sends:
        cp.start()


def _exchange_wait(*refs, scatter):
    locals_, sends, recvs = _exchange_copies(*refs, scatter=scatter, with_recvs=True)
    for cp in recvs:
        cp.wait_recv()
    for cp in sends:
        cp.wait_send()
    for cp in locals_:
        cp.wait()


def _exchange_sems(n_arrays):
    return [pltpu.SemaphoreType.DMA((n_arrays, N_DEV - 1)), pltpu.SemaphoreType.DMA((n_arrays, N_DEV - 1)),
            pltpu.SemaphoreType.DMA((n_arrays,))]


def _exchange_shapes(srcs, scatter):
    return [jax.ShapeDtypeStruct((N_DEV,) + tuple(s.shape[1:] if scatter else s.shape), s.dtype) for s in srcs]


def _call(body, *, name, grid, in_specs, out_specs, out_shape, args, scratch_shapes=(), carry=None):
    n_in, n_out, n_scr = len(in_specs), len(out_specs), len(scratch_shapes)
    srcs, scatter = carry if carry is not None else ((), False)
    nc = len(srcs)

    def wrapped(*refs):
        ins, refs = refs[:n_in], refs[n_in:]
        csrc, refs = refs[:nc], refs[nc:]
        outs, refs = refs[:n_out], refs[n_out:]
        cland, refs = refs[:nc], refs[nc:]
        scr, sems = refs[:n_scr], refs[n_scr:]
        if nc:
            @pl.when(pl.program_id(0) == 0)
            def _():
                _exchange_start(csrc, cland, *sems, scatter=scatter)

        body(*ins, *outs, *scr)
        if nc:
            @pl.when(pl.program_id(0) == grid[0] - 1)
            def _():
                _exchange_wait(csrc, cland, *sems, scatter=scatter)

    res = pl.pallas_call(
        wrapped, name=name, grid=grid,
        in_specs=list(in_specs) + [ANY] * nc, out_specs=list(out_specs) + [ANY] * nc,
        out_shape=list(out_shape) + _exchange_shapes(srcs, scatter),
        scratch_shapes=list(scratch_shapes) + (_exchange_sems(nc) if nc else []),
        compiler_params=_cparams(1),
    )(*args, *srcs)
    return res[:n_out], res[n_out:]


def _row_tile(tm, d):
    return pl.BlockSpec((tm, d), lambda i: (i, 0))


def _acc_row(d):
    return pl.BlockSpec((1, d), lambda i: (0, 0))


def _seg_tile(t, tm):
    return min(tm, t // S5_SEGS)


def _perm_tile(t, tm, d):
    nj = (t // S5_SEGS) // tm
    return pl.BlockSpec((tm, d), lambda i: (i % nj, i // nj))


def _ffn_fwd(x, g, w1t, w3t, w2, name, carry=None):
    t = x.shape[0]
    tm = _tile(t)
    nf = D_FF // FFN_FT

    def body(x_ref, g_ref, w1_ref, w3_ref, w2_ref, o_ref):
        xv = x_ref[...]
        xhat, _ = _rms_parts(xv)
        n = (xhat * g_ref[...]).astype(BF16)
        o_ref[...] = xv

        def fstep(f, c):
            rows = pl.ds(pl.multiple_of(f * FFN_FT, FFN_FT), FFN_FT)
            a = _nt(n, w1_ref[rows, :])
            b = _nt(n, w3_ref[rows, :])
            s = (a * jax.nn.sigmoid(a) * b).astype(BF16)
            o_ref[...] += 0.5 * _nn(s, w2_ref[rows, :])
            return c

        lax.fori_loop(0, nf, fstep, 0)

    (h,), landed = _call(
        body, name=name, grid=(t // tm,),
        in_specs=[_row_tile(tm, D_MODEL), _acc_row(D_MODEL), VMEM_FULL, VMEM_FULL, VMEM_FULL],
        out_specs=[_row_tile(tm, D_MODEL)],
        out_shape=[jax.ShapeDtypeStruct((t, D_MODEL), F32)],
        args=(x, g, w1t, w3t, w2), carry=carry)
    return h, landed


def _ffn_bwd(x, dh, g, w1t, w3t, w2, name, carry=None):
    t = x.shape[0]
    tm = _tile(t) // 2
    nf = D_FF // FFN_FT

    def body(x_ref, dh_ref, g_ref, w1_ref, w3_ref, w2_ref,
             dx_ref, dg_ref, da_ref, db_ref, s_ref, n_ref, dhh_ref, dn_acc):
        i = pl.program_id(0)
        xv = x_ref[...]
        gv = g_ref[...]
        xhat, r = _rms_parts(xv)
        n = (xhat * gv).astype(BF16)
        n_ref[...] = n
        dhv = dh_ref[...]
        dhh = (0.5 * dhv).astype(BF16)
        dhh_ref[...] = dhh
        dn_acc[...] = jnp.zeros_like(dn_acc)

        def fstep(f, c):
            rows = pl.ds(pl.multiple_of(f * FFN_FT, FFN_FT), FFN_FT)
            w1c, w3c, w2c = w1_ref[rows, :], w3_ref[rows, :], w2_ref[rows, :]
            a = _nt(n, w1c)
            b = _nt(n, w3c)
            sg = jax.nn.sigmoid(a)
            sl = a * sg
            ds = _nt(dhh, w2c)
            da = (ds * b * sg * (1.0 + a * (1.0 - sg))).astype(BF16)
            db = (ds * sl).astype(BF16)
            s_ref[f] = (sl * b).astype(BF16)
            da_ref[f] = da
            db_ref[f] = db
            dn_acc[...] += _nn(da, w1c) + _nn(db, w3c)
            return c

        lax.fori_loop(0, nf, fstep, 0)
        dx, dg = _rms_bwd(dn_acc[...], gv, xhat, r)
        dx_ref[...] = dhv + dx

        @pl.when(i == 0)
        def _():
            dg_ref[...] = jnp.zeros_like(dg_ref)

        dg_ref[...] += dg

    blk3 = pl.BlockSpec((nf, tm, FFN_FT), lambda i: (0, i, 0))
    sh3 = jax.ShapeDtypeStruct((nf, t, FFN_FT), BF16)
    return _call(
        body, name=name, grid=(t // tm,),
        in_specs=[_row_tile(tm, D_MODEL), _row_tile(tm, D_MODEL), _acc_row(D_MODEL), VMEM_FULL, VMEM_FULL, VMEM_FULL],
        out_specs=[_row_tile(tm, D_MODEL), _acc_row(D_MODEL), blk3, blk3, blk3,
                   _row_tile(tm, D_MODEL), _row_tile(tm, D_MODEL)],
        out_shape=[jax.ShapeDtypeStruct((t, D_MODEL), F32), jax.ShapeDtypeStruct((1, D_MODEL), F32), sh3, sh3, sh3,
                   jax.ShapeDtypeStruct((t, D_MODEL), BF16), jax.ShapeDtypeStruct((t, D_MODEL), BF16)],
        scratch_shapes=[pltpu.VMEM((tm, D_MODEL), F32)],
        args=(x, dh, g, w1t, w3t, w2), carry=carry)


def _mm_tn(a, b, name, carry=None):
    t, n = b.shape
    kc = min(512, t)
    if a.ndim == 3:
        nb, _, tb = a.shape
        a_spec = pl.BlockSpec((1, t, tb), lambda i: (i, 0, 0))
    else:
        m = a.shape[1]
        tb = min(m, 256)
        nb = m // tb
        a_spec = pl.BlockSpec((t, tb), lambda i: (0, i))
    three_d = a.ndim == 3

    def body(a_ref, b_ref, o_ref, acc):
        acc[...] = jnp.zeros_like(acc)

        def kstep(k, c):
            rows = pl.ds(pl.multiple_of(k * kc, kc), kc)
            av = a_ref[0, rows, :] if three_d else a_ref[rows, :]
            acc[...] += _tn(av.astype(BF16), b_ref[rows, :])
            return c

        lax.fori_loop(0, t // kc, kstep, 0)
        o_ref[...] = acc[...].astype(BF16)

    (out,), landed = _call(
        body, name=name, grid=(nb,),
        in_specs=[a_spec, VMEM_FULL],
        out_specs=[pl.BlockSpec((tb, n), lambda i: (i, 0))],
        out_shape=[jax.ShapeDtypeStruct((nb * tb, n), BF16)],
        scratch_shapes=[pltpu.VMEM((tb, n), F32)],
        args=(a, b), carry=carry)
    return (out, landed) if carry is not None else out


def _mix_pre_fwd(h, g, wint, carry=None):
    t = h.shape[0]
    tm = _seg_tile(t, _tile(t))

    def body(h_ref, g_ref, w_ref, u_ref, up_ref, *outs):
        xhat, _ = _rms_parts(h_ref[...])
        u = (xhat * g_ref[...]).astype(BF16)
        u_ref[...] = u
        up_ref[...] = u
        for o_ref, off, size in zip(outs, IN_OFFS, IN_SIZES):
            o_ref[...] = _nt(u, w_ref[off:off + size, :])

    return _call(
        body, name="mix_pre_fwd", grid=(t // tm,),
        in_specs=[_row_tile(tm, D_MODEL), _acc_row(D_MODEL), VMEM_FULL],
        out_specs=[_row_tile(tm, D_MODEL), _perm_tile(t, tm, D_MODEL), _perm_tile(t, tm, S5_WIDTH)]
        + [_row_tile(tm, s) for s in IN_SIZES[1:]],
        out_shape=[jax.ShapeDtypeStruct((t, D_MODEL), BF16),
                   jax.ShapeDtypeStruct((t // S5_SEGS, S5_SEGS * D_MODEL), BF16),
                   jax.ShapeDtypeStruct((t // S5_SEGS, S5_SEGS * S5_WIDTH), F32)]
        + [jax.ShapeDtypeStruct((t, s), F32) for s in IN_SIZES[1:]],
        args=(h, g, wint), carry=carry)


def _mix_pre_bwd(h, g, wint, dh2, dz):
    t = h.shape[0]
    tm = _seg_tile(t, _tile(t))

    def body(h_ref, g_ref, w_ref, dh2_ref, *rest):
        dz_refs, (dh1_ref, dg_ref) = rest[:len(IN_SIZES)], rest[len(IN_SIZES):]
        i = pl.program_id(0)
        gv = g_ref[...]
        xhat, r = _rms_parts(h_ref[...])
        du = jnp.zeros((tm, D_MODEL), F32)
        for dz_ref, off, size in zip(dz_refs, IN_OFFS, IN_SIZES):
            du = du + _nn(dz_ref[...].astype(BF16), w_ref[off:off + size, :])
        dx, dg = _rms_bwd(du, gv, xhat, r)
        dh1_ref[...] = dh2_ref[...] + dx

        @pl.when(i == 0)
        def _():
            dg_ref[...] = jnp.zeros_like(dg_ref)

        dg_ref[...] += dg

    return pl.pallas_call(
        body, name="mix_pre_bwd", grid=(t // tm,),
        in_specs=[_row_tile(tm, D_MODEL), _acc_row(D_MODEL), VMEM_FULL, _row_tile(tm, D_MODEL),
                  _perm_tile(t, tm, S5_WIDTH)] + [_row_tile(tm, s) for s in IN_SIZES[1:]],
        out_specs=[_row_tile(tm, D_MODEL), _acc_row(D_MODEL)],
        out_shape=[jax.ShapeDtypeStruct((t, D_MODEL), F32), jax.ShapeDtypeStruct((1, D_MODEL), F32)],
        compiler_params=_cparams(1),
    )(h, g, wint, dh2, *dz)


def _disc_math(lre, lim, ldt, bre, bim):
    dt = jnp.exp(ldt)
    mag = jnp.exp(lre * dt)
    ar = mag * jnp.cos(lim * dt)
    ai = mag * jnp.sin(lim * dt)
    den = lre * lre + lim * lim
    nr = ar - 1.0
    fr = (nr * lre + ai * lim) / den
    fi = (ai * lre - nr * lim) / den
    return ar, ai, fr[None] * bre - fi[None] * bim, fr[None] * bim + fi[None] * bre


def _s5_disc(lre, lim, ldt, bre, bim):
    def body(lre_ref, lim_ref, ldt_ref, bre_ref, bim_ref, ar_ref, ai_ref, bbr_ref, bbi_ref):
        ar, ai, bbr, bbi = _disc_math(lre_ref[...], lim_ref[...], ldt_ref[...], bre_ref[...], bim_ref[...])
        ar_ref[...] = ar
        ai_ref[...] = ai
        bbr_ref[...] = bbr
        bbi_ref[...] = bbi

    small = jax.ShapeDtypeStruct(lre.shape, F32)
    big = jax.ShapeDtypeStruct(bre.shape, F32)
    return pl.pallas_call(body, name="s5_disc", out_shape=[small, small, big, big],
                          in_specs=[VMEM_FULL] * 5, out_specs=[VMEM_FULL] * 4)(lre, lim, ldt, bre, bim)


def _s5_disc_bwd(lre, lim, ldt, bre, bim, dar, dai, dbbr, dbbi):
    def body(lre_ref, lim_ref, ldt_ref, bre_ref, bim_ref, dar_ref, dai_ref, dbbr_ref, dbbi_ref,
             glre_ref, glim_ref, gldt_ref, gbre_ref, gbim_ref):
        _, vjp = jax.vjp(_disc_math, lre_ref[...], lim_ref[...], ldt_ref[...], bre_ref[...], bim_ref[...])
        glre, glim, gldt, gbre, gbim = vjp((dar_ref[...], dai_ref[...], dbbr_ref[...], dbbi_ref[...]))
        glre_ref[...] = glre
        glim_ref[...] = glim
        gldt_ref[...] = gldt
        gbre_ref[...] = gbre
        gbim_ref[...] = gbim

    small = jax.ShapeDtypeStruct(lre.shape, F32)
    big = jax.ShapeDtypeStruct(bre.shape, F32)
    return pl.pallas_call(body, name="s5_disc_bwd",
                          out_shape=[small, small, jax.ShapeDtypeStruct(ldt.shape, F32), big, big],
                          in_specs=[VMEM_FULL] * 9, out_specs=[VMEM_FULL] * 5,
                          )(lre, lim, ldt, bre, bim, dar, dai, dbbr, dbbi)


def _cmul(ar, ai, br, bi):
    return ar * br - ai * bi, ar * bi + ai * br


def _cpow(ar, ai, n):
    rr, ri = None, None
    pr, pi = ar, ai
    while n:
        if n & 1:
            rr, ri = (pr, pi) if rr is None else _cmul(rr, ri, pr, pi)
        n >>= 1
        if n:
            pr, pi = _cmul(pr, pi, pr, pi)
    return rr, ri


def _shift_rows(v, down):
    row = lax.broadcasted_iota(jnp.int32, v.shape, 0)
    if down:
        return jnp.where(row == 0, 0.0, pltpu.roll(v, 1, 0))
    return jnp.where(row == S5_SEGS - 1, 0.0, pltpu.roll(v, S5_SEGS - 1, 0))


def _chain_segments(er, ei, pr, pi, down):
    fr, fi = er, ei
    for _ in range(S5_SEGS - 1):
        sr, si = _shift_rows(fr, down), _shift_rows(fi, down)
        mr, mi = _cmul(pr, pi, sr, si)
        fr, fi = er + mr, ei + mi
    return _shift_rows(fr, down), _shift_rows(fi, down)


def _s5_fwd(ugp, bd, ctd, ar4, ai4, dskip, carry=None):
    t = ugp.shape[0]
    ls = t // S5_SEGS
    rc = min(512, t)
    ns = S5_BSTATE

    def body(ug_ref, bd_ref, ct_ref, ar_ref, ai_ref, d_ref, xs_hbm, y_ref, buf, sem):
        cb = pl.program_id(0)
        bdv = bd_ref[0]

        def mm(i, c):
            rows = pl.ds(pl.multiple_of(i * rc, rc), rc)
            buf[rows, :] = _nn(ug_ref[rows, :].astype(BF16), bdv)
            return c

        lax.fori_loop(0, t // rc, mm, 0)
        arb = jnp.broadcast_to(ar_ref[0], (S5_SEGS, ns))
        aib = jnp.broadcast_to(ai_ref[0], (S5_SEGS, ns))

        def step(j, c, store):
            sr, si = c
            rows = pl.ds(pl.multiple_of(j * S5_SEGS, S5_SEGS), S5_SEGS)
            nr = arb * sr - aib * si + buf[rows, 0:ns]
            ni = arb * si + aib * sr + buf[rows, ns:2 * ns]
            if store:
                buf[rows, 0:ns] = nr
                buf[rows, ns:2 * ns] = ni
            return nr, ni

        zero = jnp.zeros((S5_SEGS, ns), F32)
        er, ei = lax.fori_loop(0, ls, functools.partial(step, store=False), (zero, zero))
        pr, pi = _cpow(arb, aib, ls)
        init = _chain_segments(er, ei, pr, pi, down=True)
        lax.fori_loop(0, ls, functools.partial(step, store=True), init)

        out = pltpu.make_async_copy(buf, xs_hbm.at[cb], sem)
        out.start()
        ctv = ct_ref[0]
        dv = d_ref[...]

        def ymm(i, c):
            rows = pl.ds(pl.multiple_of(i * rc, rc), rc)
            y_ref[rows, :] = _nn(buf[rows, :].astype(BF16), ctv) + dv * ug_ref[rows, :]
            return c

        lax.fori_loop(0, t // rc, ymm, 0)
        out.wait()

    return _call(
        body, name="s5_fwd", grid=(S5_BLOCKS,),
        in_specs=[pl.BlockSpec((t, 128), lambda i: (0, i)),
                  pl.BlockSpec((1, 128, 2 * ns), lambda i: (i, 0, 0)),
                  pl.BlockSpec((1, 2 * ns, 128), lambda i: (i, 0, 0)),
                  pl.BlockSpec((1, 1, ns), lambda i: (i, 0, 0)),
                  pl.BlockSpec((1, 1, ns), lambda i: (i, 0, 0)),
                  pl.BlockSpec((1, 128), lambda i: (0, i))],
        out_specs=[ANY, pl.BlockSpec((t, 128), lambda i: (0, i))],
        out_shape=[jax.ShapeDtypeStruct((S5_BLOCKS, t, 2 * ns), F32), jax.ShapeDtypeStruct((t, S5_WIDTH), F32)],
        scratch_shapes=[pltpu.VMEM((t, 2 * ns), F32), pltpu.SemaphoreType.DMA(())],
        args=(ugp, bd, ctd, ar4, ai4, dskip), carry=carry)


def _s5_bwd(dyp, ugp, xs, cd, bdt, ar4, ai4, dskip, carry=None):
    t = ugp.shape[0]
    ls = t // S5_SEGS
    rc = min(512, t)
    ns = S5_BSTATE

    def body(dy_ref, ug_ref, xs_hbm, cd_ref, bdt_ref, ar_ref, ai_ref, d_ref,
             dug_ref, dbd_ref, dcd_ref, dd_ref, dar_ref, dai_ref, xbuf, lam, sem):
        cb = pl.program_id(0)
        load = pltpu.make_async_copy(xs_hbm.at[cb], xbuf, sem)
        load.start()
        cdv = cd_ref[0]

        def mm(i, c):
            rows = pl.ds(pl.multiple_of(i * rc, rc), rc)
            lam[rows, :] = _nn(dy_ref[rows, :].astype(BF16), cdv)
            return c

        lax.fori_loop(0, t // rc, mm, 0)
        arb = jnp.broadcast_to(ar_ref[0], (S5_SEGS, ns))
        aib = jnp.broadcast_to(ai_ref[0], (S5_SEGS, ns))

        def lam_step(j, lr, li):
            rows = pl.ds(pl.multiple_of(j * S5_SEGS, S5_SEGS), S5_SEGS)
            nr = arb * lr + aib * li + lam[rows, 0:ns]
            ni = arb * li - aib * lr + lam[rows, ns:2 * ns]
            return rows, nr, ni

        def pass1(jj, c):
            _, nr, ni = lam_step(ls - 1 - jj, *c)
            return nr, ni

        zero = jnp.zeros((S5_SEGS, ns), F32)
        er, ei = lax.fori_loop(0, ls, pass1, (zero, zero))
        pr, pi = _cpow(arb, aib, ls)
        init = _chain_segments(er, ei, pr, -pi, down=False)
        load.wait()

        def accumulate(acc, nr, ni, xpr, xpi):
            return acc[0] + nr * xpr + ni * xpi, acc[1] + ni * xpr - nr * xpi

        def pass2(jj, c):
            lr, li, accr, acci = c
            j = ls - 1 - jj
            rows, nr, ni = lam_step(j, lr, li)
            lam[rows, 0:ns] = nr
            lam[rows, ns:2 * ns] = ni
            prev = pl.ds(pl.multiple_of((j - 1) * S5_SEGS, S5_SEGS), S5_SEGS)
            accr, acci = accumulate((accr, acci), nr, ni, xbuf[prev, 0:ns], xbuf[prev, ns:2 * ns])
            return nr, ni, accr, acci

        lr, li, accr, acci = lax.fori_loop(0, ls - 1, pass2, (init[0], init[1], zero, zero))
        rows, nr, ni = lam_step(0, lr, li)
        lam[rows, 0:ns] = nr
        lam[rows, ns:2 * ns] = ni
        last = pl.ds((ls - 1) * S5_SEGS, S5_SEGS)
        accr, acci = accumulate((accr, acci), nr, ni,
                                _shift_rows(xbuf[last, 0:ns], True), _shift_rows(xbuf[last, ns:2 * ns], True))
        dar_ref[0] = jnp.sum(accr, axis=0, keepdims=True)
        dai_ref[0] = jnp.sum(acci, axis=0, keepdims=True)

        bdtv = bdt_ref[0]
        dv = d_ref[...]
        dbd_ref[...] = jnp.zeros_like(dbd_ref)
        dcd_ref[...] = jnp.zeros_like(dcd_ref)
        dd_ref[...] = jnp.zeros_like(dd_ref)

        def tail(i, c):
            rows = pl.ds(pl.multiple_of(i * rc, rc), rc)
            dy = dy_ref[rows, :]
            ug = ug_ref[rows, :]
            lb = lam[rows, :].astype(BF16)
            dug_ref[rows, :] = _nn(lb, bdtv) + dv * dy
            dbd_ref[0] += _tn(ug.astype(BF16), lb)
            dcd_ref[0] += _tn(dy.astype(BF16), xbuf[rows, :].astype(BF16))
            dd_ref[...] += jnp.sum(dy * ug, axis=0, keepdims=True)
            return c

        lax.fori_loop(0, t // rc, tail, 0)

    chan = pl.BlockSpec((t, 128), lambda i: (0, i))
    dense = pl.BlockSpec((1, 128, 2 * ns), lambda i: (i, 0, 0))
    vec = pl.BlockSpec((1, 1, ns), lambda i: (i, 0, 0))
    return _call(
        body, name="s5_bwd", grid=(S5_BLOCKS,),
        in_specs=[chan, chan, ANY, dense, pl.BlockSpec((1, 2 * ns, 128), lambda i: (i, 0, 0)), vec, vec,
                  pl.BlockSpec((1, 128), lambda i: (0, i))],
        out_specs=[chan, dense, dense, pl.BlockSpec((1, 128), lambda i: (0, i)), vec, vec],
        out_shape=[jax.ShapeDtypeStruct((t, S5_WIDTH), F32),
                   jax.ShapeDtypeStruct((S5_BLOCKS, 128, 2 * ns), F32),
                   jax.ShapeDtypeStruct((S5_BLOCKS, 128, 2 * ns), F32),
                   jax.ShapeDtypeStruct((1, S5_WIDTH), F32),
                   jax.ShapeDtypeStruct((S5_BLOCKS, 1, ns), F32),
                   jax.ShapeDtypeStruct((S5_BLOCKS, 1, ns), F32)],
        scratch_shapes=[pltpu.VMEM((t, 2 * ns), F32), pltpu.VMEM((t, 2 * ns), F32), pltpu.SemaphoreType.DMA(())],
        args=(dyp, ugp, xs, cd, bdt, ar4, ai4, dskip), carry=carry)


def _gla_common(q, k, alow, wup, bup):
    c = GLA_CHUNK
    pre = _nn(alow.astype(BF16), wup.astype(BF16)) + bup
    la = (jnp.minimum(pre, 0.0) - jnp.log(1.0 + jnp.exp(-jnp.abs(pre)))) * (1.0 / GLA_TAU)
    rr = lax.broadcasted_iota(jnp.int32, (c, c), 0)
    cc = lax.broadcasted_iota(jnp.int32, (c, c), 1)
    tril = (rr >= cc).astype(F32)
    bc = jnp.dot(tril, la, precision=HIGHEST, preferred_element_type=F32)
    bl = bc[c - 1:c, :]
    e_pos = jnp.exp(bc)
    e_neg = jnp.exp(-bc)
    e_end = jnp.exp(bl - bc)
    qt = q * (GLA_DK ** -0.5) * e_pos
    kt = k * e_neg
    ke = k * e_end
    decb = jnp.exp(lax.dot_general(la, jnp.ones((c, GLA_DV), F32), (((0,), (0,)), ((), ())),
                                   precision=HIGHEST, preferred_element_type=F32))
    lane = lax.broadcasted_iota(jnp.int32, (1, GLA_KEY), 1)
    masks = [((lane >= h * GLA_DK) & (lane < (h + 1) * GLA_DK)).astype(F32) for h in range(GLA_HEADS)]
    return dict(pre=pre, tril=tril, bc=bc, bl=bl, e_pos=e_pos, e_neg=e_neg, e_end=e_end,
                qt=qt, kt=kt, ke=ke, decb=decb, masks=masks)


def _gla_fwd(q, k, v, alow, wup, bup, carry=None):
    t = q.shape[0]
    c = GLA_CHUNK
    n = t // c

    def body(q_ref, k_ref, v_ref, al_ref, wup_ref, bup_ref, o_ref, ss_ref, s_ref):
        i = pl.program_id(0)

        @pl.when(i == 0)
        def _():
            s_ref[...] = jnp.zeros_like(s_ref)

        m = _gla_common(q_ref[...], k_ref[...], al_ref[...], wup_ref[...], bup_ref[...])
        s = s_ref[...]
        ss_ref[0] = s
        sb = s.astype(BF16)
        ktb = m["kt"].astype(BF16)
        keb = m["ke"].astype(BF16)
        for h in range(GLA_HEADS):
            qm = (m["qt"] * m["masks"][h]).astype(BF16)
            vh = v_ref[:, h * GLA_DV:(h + 1) * GLA_DV].astype(BF16)
            p = (m["tril"] * _nt(qm, ktb)).astype(BF16)
            o_ref[:, h * GLA_DV:(h + 1) * GLA_DV] = _nn(p, vh) + _nn(qm, sb)
            rows = slice(h * GLA_DK, (h + 1) * GLA_DK)
            s_ref[rows, :] = m["decb"][rows, :] * s[rows, :] + _tn(keb, vh)[rows, :]

    return _call(
        body, name="gla_fwd", grid=(n,),
        in_specs=[_row_tile(c, GLA_KEY), _row_tile(c, GLA_KEY), _row_tile(c, GLA_VAL), _row_tile(c, GLA_RANK),
                  VMEM_FULL, VMEM_FULL],
        out_specs=[_row_tile(c, GLA_VAL), pl.BlockSpec((1, GLA_KEY, GLA_DV), lambda i: (i, 0, 0))],
        out_shape=[jax.ShapeDtypeStruct((t, GLA_VAL), F32), jax.ShapeDtypeStruct((n, GLA_KEY, GLA_DV), F32)],
        scratch_shapes=[pltpu.VMEM((GLA_KEY, GLA_DV), F32)],
        args=(q, k, v, alow, wup, bup), carry=carry)


def _gla_bwd(q, k, v, alow, wup, bup, ssave, do, carry=None):
    t = q.shape[0]
    c = GLA_CHUNK
    n = t // c

    def body(q_ref, k_ref, v_ref, al_ref, wup_ref, bup_ref, ss_ref, do_ref,
             dq_ref, dk_ref, dv_ref, dal_ref, dwup_ref, dbup_ref, ds_ref):
        i = pl.program_id(0)

        @pl.when(i == 0)
        def _():
            ds_ref[...] = jnp.zeros_like(ds_ref)
            dwup_ref[...] = jnp.zeros_like(dwup_ref)
            dbup_ref[...] = jnp.zeros_like(dbup_ref)

        alow_v = al_ref[...]
        wup_v = wup_ref[...]
        m = _gla_common(q_ref[...], k_ref[...], alow_v, wup_v, bup_ref[...])
        s = ss_ref[0]
        ds_in = ds_ref[...]
        sb = s.astype(BF16)
        dsb = ds_in.astype(BF16)
        qt, kt, ke = m["qt"], m["kt"], m["ke"]
        ktb = kt.astype(BF16)
        dqt = jnp.zeros((c, GLA_KEY), F32)
        dkt = jnp.zeros((c, GLA_KEY), F32)
        dke = jnp.zeros((c, GLA_KEY), F32)
        for h in range(GLA_HEADS):
            mask = m["masks"][h]
            qm = (qt * mask).astype(BF16)
            km = (kt * mask).astype(BF16)
            kem = (ke * mask).astype(BF16)
            cols = slice(h * GLA_DV, (h + 1) * GLA_DV)
            vh = v_ref[:, cols].astype(BF16)
            doh = do_ref[:, cols].astype(BF16)
            p = (m["tril"] * _nt(qm, ktb)).astype(BF16)
            dp = (m["tril"] * _nt(doh, vh)).astype(BF16)
            dv_ref[:, cols] = _tn(p, doh) + _nn(kem, dsb)
            dqt = dqt + _nn(dp, km) + _nt(doh, sb) * mask
            dkt = dkt + _tn(dp, qm)
            dke = dke + _nt(vh, dsb) * mask
            rows = slice(h * GLA_DK, (h + 1) * GLA_DK)
            ds_ref[rows, :] = m["decb"][rows, :] * ds_in[rows, :] + _tn(qm, doh)[rows, :]
        ddec = lax.dot_general(jnp.ones((8, GLA_DV), F32), ds_in * s, (((1,), (1,)), ((), ())),
                               precision=HIGHEST, preferred_element_type=F32)[0:1, :]
        dq_ref[...] = dqt * m["e_pos"] * (GLA_DK ** -0.5)
        dk_ref[...] = dkt * m["e_neg"] + dke * m["e_end"]
        dkeke = dke * ke
        dbl = jnp.sum(dkeke, axis=0, keepdims=True) + ddec * jnp.exp(m["bl"])
        last = (lax.broadcasted_iota(jnp.int32, (c, 1), 0) == c - 1).astype(F32)
        db_tot = dqt * qt - dkt * kt - dkeke + last * dbl
        dla = lax.dot_general(m["tril"], db_tot, (((0,), (0,)), ((), ())),
                              precision=HIGHEST, preferred_element_type=F32)
        dpre = dla * (1.0 / GLA_TAU) * jax.nn.sigmoid(-m["pre"])
        dpb = dpre.astype(BF16)
        dal_ref[...] = _nt(dpb, wup_v.astype(BF16))
        dwup_ref[...] += _tn(alow_v.astype(BF16), dpb)
        dbup_ref[...] += jnp.sum(dpre, axis=0, keepdims=True)

    def rev(d):
        return pl.BlockSpec((c, d), lambda i: (n - 1 - i, 0))

    return _call(
        body, name="gla_bwd", grid=(n,),
        in_specs=[rev(GLA_KEY), rev(GLA_KEY), rev(GLA_VAL), rev(GLA_RANK), VMEM_FULL, VMEM_FULL,
                  pl.BlockSpec((1, GLA_KEY, GLA_DV), lambda i: (n - 1 - i, 0, 0)), rev(GLA_VAL)],
        out_specs=[rev(GLA_KEY), rev(GLA_KEY), rev(GLA_VAL), rev(GLA_RANK),
                   pl.BlockSpec((GLA_RANK, GLA_KEY), lambda i: (0, 0)), _acc_row(GLA_KEY)],
        out_shape=[jax.ShapeDtypeStruct((t, GLA_KEY), F32), jax.ShapeDtypeStruct((t, GLA_KEY), F32),
                   jax.ShapeDtypeStruct((t, GLA_VAL), F32), jax.ShapeDtypeStruct((t, GLA_RANK), F32),
                   jax.ShapeDtypeStruct((GLA_RANK, GLA_KEY), F32), jax.ShapeDtypeStruct((1, GLA_KEY), F32)],
        scratch_shapes=[pltpu.VMEM((GLA_KEY, GLA_DV), F32)],
        args=(q, k, v, alow, wup, bup, ssave, do), carry=carry)


def _post_math(y, o, r, gs5, ggla, wg, bg, gn, ps5t, pglat):
    y2 = y * y
    th = jnp.tanh(GELU_C0 * (y + GELU_C1 * y * y2))
    z5 = 0.5 * y * (1.0 + th)
    z5b = z5.astype(BF16)
    gate = jax.nn.sigmoid(_nn(z5b, wg) + bg)
    ys5 = z5 * gate
    rs, on = [], []
    for h in range(GLA_HEADS):
        oh = o[:, h * GLA_DV:(h + 1) * GLA_DV]
        rh = lax.rsqrt(jnp.mean(oh * oh, axis=-1, keepdims=True) + EPS)
        rs.append(rh)
        on.append(oh * rh)
    on = jnp.concatenate(on, axis=-1)
    sr = jax.nn.sigmoid(r)
    silu_r = r * sr
    ygla = on * gn * silu_r
    ys5b, yglab = ys5.astype(BF16), ygla.astype(BF16)
    m5 = _nt(ys5b, ps5t)
    mg = _nt(yglab, pglat)
    s5g, glag = jax.nn.sigmoid(gs5), jax.nn.sigmoid(ggla)
    merged = s5g * m5 + glag * mg
    return dict(y2=y2, th=th, z5=z5, z5b=z5b, gate=gate, ys5b=ys5b, yglab=yglab, rs=rs, on=on, sr=sr,
                silu_r=silu_r, m5=m5, mg=mg, s5g=s5g, glag=glag, mergedb=merged.astype(BF16))


def _mix_post_fwd(y, o, r, gs5, ggla, h1, wg, bg, gn, ps5t, pglat, wout):
    t = o.shape[0]
    tm = _seg_tile(t, _tile(t))

    def body(y_ref, o_ref, r_ref, gs5_ref, ggla_ref, h1_ref, wg_ref, bg_ref, gn_ref, ps_ref, pg_ref, wo_ref, h2_ref):
        m = _post_math(y_ref[...], o_ref[...], r_ref[...], gs5_ref[...], ggla_ref[...],
                       wg_ref[...], bg_ref[...], gn_ref[...], ps_ref[...], pg_ref[...])
        h2_ref[...] = h1_ref[...] + _nn(m["mergedb"], wo_ref[...])

    return pl.pallas_call(
        body, name="mix_post_fwd", grid=(t // tm,),
        in_specs=[_perm_tile(t, tm, 512)] + [_row_tile(tm, 512)] * 2 + [_row_tile(tm, D_MODEL)] * 3
        + [VMEM_FULL, _acc_row(512), _acc_row(512), VMEM_FULL, VMEM_FULL, VMEM_FULL],
        out_specs=_row_tile(tm, D_MODEL),
        out_shape=jax.ShapeDtypeStruct((t, D_MODEL), F32),
        compiler_params=_cparams(1),
    )(y, o, r, gs5, ggla, h1, wg, bg, gn, ps5t, pglat, wout)


def _mix_post_bwd(y, o, r, gs5, ggla, dh2, wg, bg, gn, ps5t, pglat, wout, carry=None):
    t = o.shape[0]
    tm = _seg_tile(t, _tile(t) // 2)

    def body(y_ref, o_ref, r_ref, gs5_ref, ggla_ref, dh2_ref, wg_ref, bg_ref, gn_ref, ps_ref, pg_ref, wo_ref,
             dy_ref, do_ref, dr_ref, dgs5_ref, dggla_ref, dbg_ref, dgn_ref,
             z5b_ref, dgp_ref, ys5b_ref, dm5b_ref, yglab_ref, dmgb_ref, mergedb_ref, dh2b_ref):
        i = pl.program_id(0)
        yv, ov, rv = y_ref[...], o_ref[...], r_ref[...]
        wg, gn, ps5t, pglat = wg_ref[...], gn_ref[...], ps_ref[...], pg_ref[...]
        m = _post_math(yv, ov, rv, gs5_ref[...], ggla_ref[...], wg, bg_ref[...], gn, ps5t, pglat)
        dh2b = dh2_ref[...].astype(BF16)
        dmerged = _nt(dh2b, wo_ref[...])
        s5g, glag = m["s5g"], m["glag"]
        dgs5_ref[...] = dmerged * m["m5"] * s5g * (1.0 - s5g)
        dggla_ref[...] = dmerged * m["mg"] * glag * (1.0 - glag)
        dm5b = (dmerged * s5g).astype(BF16)
        dmgb = (dmerged * glag).astype(BF16)
        dys5 = _nn(dm5b, ps5t)
        dygla = _nn(dmgb, pglat)
        gate, z5, th = m["gate"], m["z5"], m["th"]
        dgpre = dys5 * z5 * gate * (1.0 - gate)
        dgpb = dgpre.astype(BF16)
        dz5 = dys5 * gate + _nt(dgpb, wg)
        dgelu = 0.5 * (1.0 + th) + 0.5 * yv * (1.0 - th * th) * GELU_C0 * (1.0 + 3.0 * GELU_C1 * m["y2"])
        dy_ref[...] = dz5 * dgelu
        on, sr, silu_r = m["on"], m["sr"], m["silu_r"]
        dr_ref[...] = dygla * on * gn * sr * (1.0 + rv * (1.0 - sr))
        dgn = jnp.sum(dygla * on * silu_r, axis=0, keepdims=True)
        don = dygla * gn * silu_r
        for h in range(GLA_HEADS):
            cols = slice(h * GLA_DV, (h + 1) * GLA_DV)
            donh, onh = don[:, cols], on[:, cols]
            do_ref[:, cols] = m["rs"][h] * (donh - onh * jnp.mean(donh * onh, axis=-1, keepdims=True))

        @pl.when(i == 0)
        def _():
            dbg_ref[...] = jnp.zeros_like(dbg_ref)
            dgn_ref[...] = jnp.zeros_like(dgn_ref)

        dbg_ref[...] += jnp.sum(dgpre, axis=0, keepdims=True)
        dgn_ref[...] += dgn
        z5b_ref[...] = m["z5b"]
        dgp_ref[...] = dgpb
        ys5b_ref[...] = m["ys5b"]
        dm5b_ref[...] = dm5b
        yglab_ref[...] = m["yglab"]
        dmgb_ref[...] = dmgb
        mergedb_ref[...] = m["mergedb"]
        dh2b_ref[...] = dh2b

    def f32(d):
        return jax.ShapeDtypeStruct((t, d), F32)

    def b16(d):
        return jax.ShapeDtypeStruct((t, d), BF16)

    widths = (512, 512, 512, 1024, 512, 1024, 1024, 1024)
    return _call(
        body, name="mix_post_bwd", grid=(t // tm,),
        in_specs=[_perm_tile(t, tm, 512)] + [_row_tile(tm, 512)] * 2 + [_row_tile(tm, D_MODEL)] * 3
        + [VMEM_FULL, _acc_row(512), _acc_row(512), VMEM_FULL, VMEM_FULL, VMEM_FULL],
        out_specs=[_perm_tile(t, tm, 512)] + [_row_tile(tm, 512)] * 2 + [_row_tile(tm, D_MODEL)] * 2
        + [_acc_row(512)] * 2 + [_row_tile(tm, w) for w in widths],
        out_shape=[jax.ShapeDtypeStruct((t // S5_SEGS, S5_SEGS * 512), F32)] + [f32(512)] * 2 + [f32(D_MODEL)] * 2
        + [jax.ShapeDtypeStruct((1, 512), F32)] * 2
        + [b16(w) for w in widths],
        args=(y, o, r, gs5, ggla, dh2, wg, bg, gn, ps5t, pglat, wout), carry=carry)


def _head(h3, g, target):
    t = h3.shape[0]
    tm = _tile(t)

    def body(h_ref, g_ref, t_ref, loss_ref, dh_ref, dg_ref):
        i = pl.program_id(0)
        gv = g_ref[...]
        xhat, r = _rms_parts(h_ref[...])
        err = xhat * gv - t_ref[...]
        dx, dg = _rms_bwd(err * (1.0 / D_MODEL), gv, xhat, r)
        dh_ref[...] = dx

        @pl.when(i == 0)
        def _():
            loss_ref[...] = jnp.zeros_like(loss_ref)
            dg_ref[...] = jnp.zeros_like(dg_ref)

        loss_ref[...] += (0.5 / D_MODEL) * jnp.sum(jnp.sum(err * err, axis=1, keepdims=True), axis=0, keepdims=True)
        dg_ref[...] += dg

    return pl.pallas_call(
        body, name="head", grid=(t // tm,),
        in_specs=[_row_tile(tm, D_MODEL), _acc_row(D_MODEL), _row_tile(tm, D_MODEL)],
        out_specs=[pl.BlockSpec((1, 1), lambda i: (0, 0)), _row_tile(tm, D_MODEL), _acc_row(D_MODEL)],
        out_shape=[jax.ShapeDtypeStruct((1, 1), F32), jax.ShapeDtypeStruct((t, D_MODEL), F32),
                   jax.ShapeDtypeStruct((1, D_MODEL), F32)],
        compiler_params=_cparams(1),
    )(h3, g, target)


ADAM_TILE_ELEMS = 256 * 1024


def _adamw(w, g, m, v, name):
    rows, cols = w.shape
    tr = rows
    while tr * cols > ADAM_TILE_ELEMS and tr % 16 == 0:
        tr //= 2

    def body(w_ref, g_ref, m_ref, v_ref, d_ref, nm_ref, nv_ref):
        gv = g_ref[...]
        nm = ADAM_B1 * m_ref[...] + (1.0 - ADAM_B1) * gv
        nv = ADAM_B2 * v_ref[...] + (1.0 - ADAM_B2) * (gv * gv)
        m_hat = nm / (1.0 - ADAM_B1 ** ADAM_STEP)
        v_hat = nv / (1.0 - ADAM_B2 ** ADAM_STEP)
        d_ref[...] = -ADAM_LR * (m_hat / (jnp.sqrt(v_hat) + ADAM_EPS) + ADAM_WD * w_ref[...])
        nm_ref[...] = nm
        nv_ref[...] = nv

    spec = pl.BlockSpec((tr, cols), lambda i: (i, 0))
    sh = jax.ShapeDtypeStruct((rows, cols), F32)
    return pl.pallas_call(body, name=name, grid=(rows // tr,), in_specs=[spec] * 4, out_specs=[spec] * 3,
                          out_shape=[sh] * 3, compiler_params=_cparams(1))(w, g, m, v)


def _exchange(srcs, scatter, name):
    n = len(srcs)

    def body(*refs):
        _exchange_start(refs[:n], refs[n:2 * n], *refs[2 * n:], scatter=scatter)
        _exchange_wait(refs[:n], refs[n:2 * n], *refs[2 * n:], scatter=scatter)

    return pl.pallas_call(
        body, name=name, in_specs=[ANY] * n, out_specs=[ANY] * n,
        out_shape=_exchange_shapes(srcs, scatter), scratch_shapes=_exchange_sems(n),
    )(*srcs)


def _sum_slabs(slabs, name):
    n = slabs.shape[0]

    def body(s_ref, o_ref):
        acc = s_ref[0].astype(F32)
        for s in range(1, n):
            acc = acc + s_ref[s].astype(F32)
        o_ref[...] = acc

    return pl.pallas_call(
        body, name=name, in_specs=[VMEM_FULL], out_specs=VMEM_FULL,
        out_shape=jax.ShapeDtypeStruct(slabs.shape[1:], F32),
        compiler_params=pltpu.CompilerParams(vmem_limit_bytes=VMEM_LIMIT_BYTES),
    )(slabs)


BIG = ("ffn1_w1", "ffn1_w3", "ffn1_w2", "w_in", "s5_glu_w", "gla_a_up_w", "proj_s5", "proj_gla", "w_out",
       "ffn2_w1", "ffn2_w3", "ffn2_w2")
GROUPS = (("ffn1_w1", "ffn1_w3", "ffn1_w2"),
          ("w_in", "s5_glu_w", "gla_a_up_w", "proj_s5", "proj_gla", "w_out"),
          ("ffn2_w1", "ffn2_w3", "ffn2_w2"))
W_IN_ROWS = 514
W_IN_PAD = 528
UP_COLS = 32
COL_SHARDED = ("ffn1_w1", "ffn1_w3", "w_in", "proj_s5", "proj_gla", "ffn2_w1", "ffn2_w3")

SMALL = ("ffn1_norm", "mix_norm", "s5_lambda_re", "s5_lambda_im", "s5_log_dt", "s5_b_re", "s5_b_im", "s5_c_re",
         "s5_c_im", "s5_d", "s5_glu_b", "gla_a_up_b", "gla_out_norm", "ffn2_norm", "final_norm")
SMALL_SHAPES = dict(ffn1_norm=(1, 1024), mix_norm=(1, 1024), s5_lambda_re=(1, 32, 64), s5_lambda_im=(1, 32, 64),
                    s5_log_dt=(1, 32), s5_b_re=(1, 32, 64, 16), s5_b_im=(1, 32, 64, 16), s5_c_re=(1, 32, 16, 64),
                    s5_c_im=(1, 32, 16, 64), s5_d=(1, 32, 16), s5_glu_b=(1, 512), gla_a_up_b=(1, 256),
                    gla_out_norm=(1, 512), ffn2_norm=(1, 1024), final_norm=(1024,))
SMALL_N = sum(math.prod(s) for s in SMALL_SHAPES.values())
SMALL_R = -(-SMALL_N // (64 * 1024)) * 64


def _shard_rows(name, a):
    if name == "gla_a_up_w":
        return jnp.pad(a, ((0, 0), (0, 128 - UP_COLS)))
    if name in COL_SHARDED:
        a = a.T
    if name == "w_in":
        return jnp.pad(a, ((0, W_IN_PAD - W_IN_ROWS), (0, 0)))
    return a.reshape(-1, 1024)


def _unshard_rows(name, rows, shape):
    if name == "gla_a_up_w":
        return rows[:, :UP_COLS]
    if name == "w_in":
        rows = rows[:W_IN_ROWS]
    if name in COL_SHARDED:
        return rows.reshape(shape[1], shape[0]).T
    return rows.reshape(shape)


def _pack_small(vals):
    flat = jnp.concatenate([vals[n].reshape(-1).astype(F32) for n in SMALL])
    return jnp.pad(flat, (0, SMALL_R * 1024 - SMALL_N)).reshape(SMALL_R, 1024)


def _unpack_small(slab):
    flat = slab.reshape(-1)
    out, off = {}, 0
    for n in SMALL:
        size = math.prod(SMALL_SHAPES[n])
        out[n] = flat[off:off + size].reshape(SMALL_SHAPES[n])
        off += size
    return out


FULL_SHAPES = dict(w_in=(IN_COLS, D_MODEL), s5_glu_w=(S5_WIDTH, S5_WIDTH), gla_a_up_w=(GLA_RANK, GLA_KEY),
                   proj_s5=(D_MODEL, S5_WIDTH), proj_gla=(D_MODEL, GLA_VAL), w_out=(D_MODEL, D_MODEL))


def _full_weight(name, gathered):
    if name == "gla_a_up_w":
        return gathered[:, :, :UP_COLS].transpose(1, 0, 2).reshape(GLA_RANK, GLA_KEY)
    if name == "w_in":
        gathered = gathered[:, :W_IN_ROWS]
    return gathered.reshape(FULL_SHAPES.get(name, (D_FF, D_MODEL)))


def _grad_slabs(name, g):
    if name == "gla_a_up_w":
        g = g.reshape(GLA_RANK, N_DEV, UP_COLS).transpose(1, 0, 2)
        return jnp.pad(g, ((0, 0), (0, 0), (0, 128 - UP_COLS))).astype(BF16)
    if name == "w_in":
        return jnp.pad(g.reshape(N_DEV, W_IN_ROWS, D_MODEL), ((0, 0), (0, W_IN_PAD - W_IN_ROWS), (0, 0)))
    return g.reshape(N_DEV, -1, 1024)


def _s5_dense(re, im, sign_im):
    eye = jnp.eye(8, dtype=F32)

    def one(a):
        a = a.reshape(S5_BLOCKS, 8, S5_GROUP, S5_STATE)
        return jnp.einsum("cghp,gk->cghkp", a, eye).reshape(S5_BLOCKS, 128, S5_BSTATE)

    return jnp.concatenate([one(re), sign_im * one(im)], axis=-1)


def _s5_undense(d):
    eye = jnp.eye(8, dtype=F32)

    def one(a):
        a = a.reshape(S5_BLOCKS, 8, S5_GROUP, 8, S5_STATE)
        return jnp.einsum("cghkp,gk->cghp", a, eye).reshape(S5_GROUPS, S5_GROUP, S5_STATE)

    return one(d[..., :S5_BSTATE]), one(d[..., S5_BSTATE:])


def _local_step(x, target, p, w, rows=None):
    w = dict(w or {})
    landed_grads = {}

    def gather(names):
        return None if rows is None else ([rows[n] for n in names], False)

    def gathered(names, landed):
        w.update({n: _full_weight(n, g) for n, g in zip(names, landed)})

    def scatter(names):
        return None if rows is None else ([_grad_slabs(n, big[n]) for n in names], True)

    def scattered(names, landed):
        landed_grads.update(zip(names, landed))

    if rows is not None:
        gathered(GROUPS[0], _exchange(gather(GROUPS[0])[0], False, "gather_ffn1"))
    g1, gm, g2 = p["ffn1_norm"], p["mix_norm"], p["ffn2_norm"]
    gf = p["final_norm"].reshape(1, D_MODEL)
    lre, lim = p["s5_lambda_re"][0], p["s5_lambda_im"][0]
    ldt = p["s5_log_dt"][0].reshape(S5_GROUPS, 1)
    bre = p["s5_b_re"][0].transpose(2, 0, 1)
    bim = p["s5_b_im"][0].transpose(2, 0, 1)
    cre, cim = p["s5_c_re"][0], p["s5_c_im"][0]
    dskip = p["s5_d"][0].reshape(1, S5_WIDTH)
    bg, bup, gn = p["s5_glu_b"], p["gla_a_up_b"], p["gla_out_norm"]
    t = x.shape[0]
    ls = t // S5_SEGS

    h1, got = _ffn_fwd(x, g1, w["ffn1_w1"], w["ffn1_w3"], w["ffn1_w2"], "ffn1_fwd", gather(GROUPS[1]))
    gathered(GROUPS[1], got)
    wup = w["gla_a_up_w"].astype(F32)
    (u, u_perm, s5in, q, k, v, r, alow, gs5, ggla), got = _mix_pre_fwd(h1, gm, w["w_in"], gather(GROUPS[2][:1]))
    gathered(GROUPS[2][:1], got)
    ar, ai, bbr, bbi = _s5_disc(lre, lim, ldt, bre, bim)
    bd = _s5_dense(bbr.transpose(1, 0, 2), bbi.transpose(1, 0, 2), 1.0)
    cd = _s5_dense(cre, cim, -1.0)
    bd16, cd16 = bd.astype(BF16), cd.astype(BF16)
    bdt16, ctd16 = bd16.transpose(0, 2, 1), cd16.transpose(0, 2, 1)
    ar4 = ar.reshape(S5_BLOCKS, 1, S5_BSTATE)
    ai4 = ai.reshape(S5_BLOCKS, 1, S5_BSTATE)
    ugp = s5in.reshape(t, S5_WIDTH)
    (xs, yp), got = _s5_fwd(ugp, bd16, ctd16, ar4, ai4, dskip, gather(GROUPS[2][1:2]))
    gathered(GROUPS[2][1:2], got)
    y = yp.reshape(ls, S5_SEGS * S5_WIDTH)
    (o, ssave), got = _gla_fwd(q, k, v, alow, wup, bup, gather(GROUPS[2][2:]))
    gathered(GROUPS[2][2:], got)
    post_w = (w["s5_glu_w"], bg, gn, w["proj_s5"], w["proj_gla"], w["w_out"])
    h2 = _mix_post_fwd(y, o, r, gs5, ggla, h1, *post_w)
    h3, _ = _ffn_fwd(h2, g2, w["ffn2_w1"], w["ffn2_w3"], w["ffn2_w2"], "ffn2_fwd")
    loss, dh3, dgf = _head(h3, gf, target)

    big, small = {}, {}
    small["final_norm"] = dgf.reshape(D_MODEL)
    (dh2, dg2, da3, db3, s3, n2, dhh2), _ = _ffn_bwd(
        h2, dh3, g2, w["ffn2_w1"], w["ffn2_w3"], w["ffn2_w2"], "ffn2_bwd")
    small["ffn2_norm"] = dg2
    big["ffn2_w1"] = _mm_tn(da3, n2, "ffn2_dw1")
    big["ffn2_w3"] = _mm_tn(db3, n2, "ffn2_dw3")
    big["ffn2_w2"] = _mm_tn(s3, dhh2, "ffn2_dw2")
    (dy, do, dr, dgs5, dggla, dbg, dgn, z5b, dgpb, ys5b, dm5b, yglab, dmgb, mergedb, dh2b), got = _mix_post_bwd(
        y, o, r, gs5, ggla, dh2, *post_w, carry=scatter(GROUPS[2][:1]))
    scattered(GROUPS[2][:1], got)
    small["s5_glu_b"] = dbg
    small["gla_out_norm"] = dgn
    big["s5_glu_w"] = _mm_tn(z5b, dgpb, "glu_dw")
    big["proj_s5"] = _mm_tn(dm5b, ys5b, "proj_s5_dw")
    big["proj_gla"] = _mm_tn(dmgb, yglab, "proj_gla_dw")
    big["w_out"] = _mm_tn(mergedb, dh2b, "w_out_dw")
    (dq, dk, dv, dalow, dwup, dbup), got = _gla_bwd(q, k, v, alow, wup, bup, ssave, do, scatter(GROUPS[2][1:2]))
    scattered(GROUPS[2][1:2], got)
    big["gla_a_up_w"] = dwup
    small["gla_a_up_b"] = dbup
    (dugp, dbd, dcd, dd, dar4, dai4), got = _s5_bwd(
        dy.reshape(t, S5_WIDTH), ugp, xs, cd16, bdt16, ar4, ai4, dskip, scatter(GROUPS[2][2:]))
    scattered(GROUPS[2][2:], got)
    ds5in = dugp.reshape(ls, S5_SEGS * S5_WIDTH)
    dbbr, dbbi = _s5_undense(dbd)
    dcre, dcim_neg = _s5_undense(dcd)
    glre, glim, gldt, gbre, gbim = _s5_disc_bwd(
        lre, lim, ldt, bre, bim, dar4.reshape(S5_GROUPS, S5_STATE), dai4.reshape(S5_GROUPS, S5_STATE),
        dbbr.transpose(1, 0, 2), dbbi.transpose(1, 0, 2))
    small["s5_lambda_re"] = glre[None]
    small["s5_lambda_im"] = glim[None]
    small["s5_log_dt"] = gldt.reshape(1, S5_GROUPS)
    small["s5_b_re"] = gbre.transpose(1, 2, 0)[None]
    small["s5_b_im"] = gbim.transpose(1, 2, 0)[None]
    small["s5_c_re"] = dcre[None]
    small["s5_c_im"] = -dcim_neg[None]
    small["s5_d"] = dd.reshape(1, S5_GROUPS, S5_GROUP)
    dz = (ds5in, dq, dk, dv, dr, dalow, dgs5, dggla)
    dh1, dgm = _mix_pre_bwd(h1, gm, w["w_in"], dh2, dz)
    small["mix_norm"] = dgm
    dw_in = [_mm_tn(dugp, u_perm.reshape(t, D_MODEL), "w_in_dw0")]
    dw_in += [_mm_tn(d, u, "w_in_dw%d" % i) for i, d in enumerate(dz) if i]
    big["w_in"] = jnp.concatenate(dw_in, axis=0)
    (dx, dg1, da3, db3, s3, n1, dhh1), got = _ffn_bwd(
        x, dh1, g1, w["ffn1_w1"], w["ffn1_w3"], w["ffn1_w2"], "ffn1_bwd", scatter(GROUPS[1]))
    scattered(GROUPS[1], got)
    small["ffn1_norm"] = dg1
    big["ffn1_w1"] = _mm_tn(da3, n1, "ffn1_dw1")
    if rows is None:
        big["ffn1_w3"] = _mm_tn(db3, n1, "ffn1_dw3")
        big["ffn1_w2"] = _mm_tn(s3, dhh1, "ffn1_dw2")
        return loss[0, 0], dx, big, small
    big["ffn1_w3"], got = _mm_tn(db3, n1, "ffn1_dw3", scatter(GROUPS[0][:1]))
    scattered(GROUPS[0][:1], got)
    big["ffn1_w2"], got = _mm_tn(s3, dhh1, "ffn1_dw2", scatter(GROUPS[0][1:2]))
    scattered(GROUPS[0][1:2], got)
    scattered(GROUPS[0][2:], _exchange(scatter(GROUPS[0][2:])[0], True, "scatter_ffn1_w2"))
    return loss[0, 0], dx, landed_grads, small


NAMES = ("ffn1_norm", "ffn1_w1", "ffn1_w3", "ffn1_w2", "mix_norm", "w_in", "s5_lambda_re", "s5_lambda_im",
         "s5_log_dt", "s5_b_re", "s5_b_im", "s5_c_re", "s5_c_im", "s5_d", "s5_glu_w", "s5_glu_b", "gla_a_up_w",
         "gla_a_up_b", "gla_out_norm", "proj_s5", "proj_gla", "w_out", "ffn2_norm", "ffn2_w1", "ffn2_w3", "ffn2_w2",
         "final_norm")


def kernel(*args):
    nw = len(NAMES)
    x = args[0][0]
    wts = dict(zip(NAMES, args[1:1 + nw]))
    target = args[1 + nw][0]
    mom = dict(zip(NAMES, args[2 + nw:2 + 2 * nw]))
    var = dict(zip(NAMES, args[2 + 2 * nw:2 + 3 * nw]))

    shards = {n: wts[n][0] for n in BIG}
    rows = {n: _shard_rows(n, shards[n]).astype(BF16) for n in BIG}
    loss, dx, landed, small = _local_step(x, target, {n: wts[n] for n in SMALL}, None, rows)
    loss = lax.psum(loss, ("x", "y", "c"))

    grad, delta, new_m, new_v = {}, {}, {}, {}
    for n in BIG:
        g = _unshard_rows(n, _sum_slabs(landed[n], "sum_" + n), shards[n].shape)
        grad[n] = g[None]
        delta[n], new_m[n], new_v[n] = (a[None] for a in _adamw(shards[n], g, mom[n][0], var[n][0], "adamw_" + n))

    part = _pack_small(small).reshape(N_DEV, SMALL_R // N_DEV, 1024)
    mine = _sum_slabs(_exchange([part], True, "scatter_small")[0], "sum_small")
    g_small = _exchange([mine], False, "gather_small")[0].reshape(SMALL_R, 1024)
    d_small, m_small, v_small = _adamw(_pack_small({n: wts[n] for n in SMALL}), g_small,
                                       _pack_small({n: mom[n] for n in SMALL}),
                                       _pack_small({n: var[n] for n in SMALL}), "adamw_small")
    for out, slab in ((grad, g_small), (delta, d_small), (new_m, m_small), (new_v, v_small)):
        out.update(_unpack_small(slab))
    return (loss, dx[None], *(d[n] for d in (grad, delta, new_m, new_v) for n in NAMES))
```

```python
import functools
import math

import jax
import jax.numpy as jnp
from jax import lax
from jax.experimental import pallas as pl
from jax.experimental.pallas import tpu as pltpu

F32, BF16 = jnp.float32, jnp.bfloat16
HIGHEST = lax.Precision.HIGHEST

D_MODEL = 1024
D_FF = 2816
N_DEV = 8
S5_WIDTH, S5_GROUPS, S5_GROUP, S5_STATE = 512, 32, 16, 64
S5_BLOCKS = 4
S5_BSTATE = 512
S5_SEGS = 8
GLA_HEADS, GLA_DK, GLA_DV = 4, 64, 128
GLA_KEY, GLA_VAL, GLA_RANK, GLA_CHUNK = 256, 512, 16, 64
GLA_TAU = 16.0
EPS = 1e-6
IN_SIZES = (512, 256, 256, 512, 512, 16, 1024, 1024)
IN_OFFS = tuple(sum(IN_SIZES[:i]) for i in range(len(IN_SIZES)))
IN_COLS = sum(IN_SIZES)
ADAM_LR, ADAM_B1, ADAM_B2, ADAM_EPS, ADAM_WD, ADAM_STEP = 0.001, 0.9, 0.999, 1e-08, 0.01, 10
GELU_C0 = math.sqrt(2.0 / math.pi)
GELU_C1 = 0.044715

FFN_FT = 256
VMEM_LIMIT_BYTES = 56 * 1024 * 1024

VMEM_FULL = pl.BlockSpec(memory_space=pltpu.VMEM)
ANY = pl.BlockSpec(memory_space=pl.ANY)


def _cparams(n_grid):
    return pltpu.CompilerParams(dimension_semantics=("arbitrary",) * n_grid, vmem_limit_bytes=VMEM_LIMIT_BYTES)


def _tile(t):
    return 512 if t >= 1024 else t // 2


def _nn(a, b):
    return jnp.dot(a, b, preferred_element_type=F32)


def _nt(a, b):
    return lax.dot_general(a, b, (((1,), (1,)), ((), ())), preferred_element_type=F32)


def _tn(a, b):
    return lax.dot_general(a, b, (((0,), (0,)), ((), ())), preferred_element_type=F32)


def _rms_parts(x):
    r = lax.rsqrt(jnp.mean(x * x, axis=-1, keepdims=True) + EPS)
    return x * r, r


def _rms_bwd(dn, g, xhat, r):
    dxh = dn * g
    dx = r * (dxh - xhat * jnp.mean(dxh * xhat, axis=-1, keepdims=True))
    return dx, jnp.sum(dn * xhat, axis=0, keepdims=True)


def _peers():
    x, y, c = lax.axis_index("x"), lax.axis_index("y"), lax.axis_index("c")
    out = []
    for k in range(1, N_DEV):
        px = 1 - x if k & 4 else x
        py = 1 - y if k & 2 else y
        pc = 1 - c if k & 1 else c
        out.append(((px, py, pc), 4 * px + 2 * py + pc))
    return 4 * x + 2 * y + c, out


def _exchange_copies(src_refs, out_refs, send_sems, recv_sems, local_sems, scatter, with_recvs):
    me, peers = _peers()
    locals_, sends, recvs = [], [], []
    for a, (src_ref, out_ref) in enumerate(zip(src_refs, out_refs)):
        def mine(idx, src_ref=src_ref):
            return src_ref.at[idx] if scatter else src_ref

        locals_.append(pltpu.make_async_copy(mine(me), out_ref.at[me], local_sems.at[a]))
        for k, (dev, idx) in enumerate(peers):
            sends.append(pltpu.make_async_remote_copy(
                src_ref=mine(idx), dst_ref=out_ref.at[me], send_sem=send_sems.at[a, k], recv_sem=recv_sems.at[a, k],
                device_id=dev, device_id_type=pl.DeviceIdType.MESH))
            if with_recvs:
                recvs.append(pltpu.make_async_remote_copy(
                    src_ref=mine(idx), dst_ref=out_ref.at[idx], send_sem=send_sems.at[a, k],
                    recv_sem=recv_sems.at[a, k], device_id=dev, device_id_type=pl.DeviceIdType.MESH))
    return locals_, sends, recvs


def _exchange_start(*refs, scatter):
    locals_, sends, _ = _exchange_copies(*refs, scatter=scatter, with_recvs=False)
    for cp in locals_ + sends:
        cp.start()


def _exchange_wait(*refs, scatter):
    locals_, sends, recvs = _exchange_copies(*refs, scatter=scatter, with_recvs=True)
    for cp in recvs:
        cp.wait_recv()
    for cp in sends:
        cp.wait_send()
    for cp in locals_:
        cp.wait()


def _exchange_sems(n_arrays):
    return [pltpu.SemaphoreType.DMA((n_arrays, N_DEV - 1)), pltpu.SemaphoreType.DMA((n_arrays, N_DEV - 1)),
            pltpu.SemaphoreType.DMA((n_arrays,))]


def _exchange_shapes(srcs, scatter):
    return [jax.ShapeDtypeStruct((N_DEV,) + tuple(s.shape[1:] if scatter else s.shape), s.dtype) for s in srcs]


def _call(body, *, name, grid, in_specs, out_specs, out_shape, args, scratch_shapes=(), carry=None):
    n_in, n_out, n_scr = len(in_specs), len(out_specs), len(scratch_shapes)
    srcs, scatter = carry if carry is not None else ((), False)
    nc = len(srcs)

    def wrapped(*refs):
        ins, refs = refs[:n_in], refs[n_in:]
        csrc, refs = refs[:nc], refs[nc:]
        outs, refs = refs[:n_out], refs[n_out:]
        cland, refs = refs[:nc], refs[nc:]
        scr, sems = refs[:n_scr], refs[n_scr:]
        if nc:
            @pl.when(pl.program_id(0) == 0)
            def _():
                _exchange_start(csrc, cland, *sems, scatter=scatter)

        body(*ins, *outs, *scr)
        if nc:
            @pl.when(pl.program_id(0) == grid[0] - 1)
            def _():
                _exchange_wait(csrc, cland, *sems, scatter=scatter)

    res = pl.pallas_call(
        wrapped, name=name, grid=grid,
        in_specs=list(in_specs) + [ANY] * nc, out_specs=list(out_specs) + [ANY] * nc,
        out_shape=list(out_shape) + _exchange_shapes(srcs, scatter),
        scratch_shapes=list(scratch_shapes) + (_exchange_sems(nc) if nc else []),
        compiler_params=_cparams(1),
    )(*args, *srcs)
    return res[:n_out], res[n_out:]


def _row_tile(tm, d):
    return pl.BlockSpec((tm, d), lambda i: (i, 0))


def _acc_row(d):
    return pl.BlockSpec((1, d), lambda i: (0, 0))


def _ffn_fwd(x, g, w1t, w3t, w2, name, carry=None):
    t = x.shape[0]
    tm = _tile(t)
    nf = D_FF // FFN_FT

    def body(x_ref, g_ref, w1_ref, w3_ref, w2_ref, o_ref):
        xv = x_ref[...]
        xhat, _ = _rms_parts(xv)
        n = (xhat * g_ref[...]).astype(BF16)
        o_ref[...] = xv

        def fstep(f, c):
            rows = pl.ds(pl.multiple_of(f * FFN_FT, FFN_FT), FFN_FT)
            a = _nt(n, w1_ref[rows, :])
            b = _nt(n, w3_ref[rows, :])
            s = (a * jax.nn.sigmoid(a) * b).astype(BF16)
            o_ref[...] += 0.5 * _nn(s, w2_ref[rows, :])
            return c

        lax.fori_loop(0, nf, fstep, 0, unroll=True)

    (h,), landed = _call(
        body, name=name, grid=(t // tm,),
        in_specs=[_row_tile(tm, D_MODEL), _acc_row(D_MODEL), VMEM_FULL, VMEM_FULL, VMEM_FULL],
        out_specs=[_row_tile(tm, D_MODEL)],
        out_shape=[jax.ShapeDtypeStruct((t, D_MODEL), F32)],
        args=(x, g, w1t, w3t, w2), carry=carry)
    return h, landed


def _ffn_bwd(x, dh, g, w1t, w3t, w2, name, carry=None):
    t = x.shape[0]
    tm = _tile(t) // 2
    nf = D_FF // FFN_FT

    def body(x_ref, dh_ref, g_ref, w1_ref, w3_ref, w2_ref,
             dx_ref, dg_ref, da_ref, db_ref, s_ref, n_ref, dhh_ref, dn_acc):
        i = pl.program_id(0)
        xv = x_ref[...]
        gv = g_ref[...]
        xhat, r = _rms_parts(xv)
        n = (xhat * gv).astype(BF16)
        n_ref[...] = n
        dhv = dh_ref[...]
        dhh = (0.5 * dhv).astype(BF16)
        dhh_ref[...] = dhh
        dn_acc[...] = jnp.zeros_like(dn_acc)

        def fstep(f, c):
            rows = pl.ds(pl.multiple_of(f * FFN_FT, FFN_FT), FFN_FT)
            w1c, w3c, w2c = w1_ref[rows, :], w3_ref[rows, :], w2_ref[rows, :]
            a = _nt(n, w1c)
            b = _nt(n, w3c)
            sg = jax.nn.sigmoid(a)
            sl = a * sg
            ds = _nt(dhh, w2c)
            da = (ds * b * sg * (1.0 + a * (1.0 - sg))).astype(BF16)
            db = (ds * sl).astype(BF16)
            s_ref[f] = (sl * b).astype(BF16)
            da_ref[f] = da
            db_ref[f] = db
            dn_acc[...] += _nn(da, w1c) + _nn(db, w3c)
            return c

        lax.fori_loop(0, nf, fstep, 0, unroll=True)
        dx, dg = _rms_bwd(dn_acc[...], gv, xhat, r)
        dx_ref[...] = dhv + dx

        @pl.when(i == 0)
        def _():
            dg_ref[...] = jnp.zeros_like(dg_ref)

        dg_ref[...] += dg

    blk3 = pl.BlockSpec((nf, tm, FFN_FT), lambda i: (0, i, 0))
    sh3 = jax.ShapeDtypeStruct((nf, t, FFN_FT), BF16)
    return _call(
        body, name=name, grid=(t // tm,),
        in_specs=[_row_tile(tm, D_MODEL), _row_tile(tm, D_MODEL), _acc_row(D_MODEL), VMEM_FULL, VMEM_FULL, VMEM_FULL],
        out_specs=[_row_tile(tm, D_MODEL), _acc_row(D_MODEL), blk3, blk3, blk3,
                   _row_tile(tm, D_MODEL), _row_tile(tm, D_MODEL)],
        out_shape=[jax.ShapeDtypeStruct((t, D_MODEL), F32), jax.ShapeDtypeStruct((1, D_MODEL), F32), sh3, sh3, sh3,
                   jax.ShapeDtypeStruct((t, D_MODEL), BF16), jax.ShapeDtypeStruct((t, D_MODEL), BF16)],
        scratch_shapes=[pltpu.VMEM((tm, D_MODEL), F32)],
        args=(x, dh, g, w1t, w3t, w2), carry=carry)


def _mm_tn(a, b, name, carry=None):
    t, n = b.shape
    kc = min(512, t)
    if a.ndim == 3:
        nb, _, tb = a.shape
        a_spec = pl.BlockSpec((1, t, tb), lambda i: (i, 0, 0))
    else:
        m = a.shape[1]
        tb = min(m, 256)
        nb = m // tb
        a_spec = pl.BlockSpec((t, tb), lambda i: (0, i))
    three_d = a.ndim == 3

    def body(a_ref, b_ref, o_ref, acc):
        acc[...] = jnp.zeros_like(acc)

        def kstep(k, c):
            rows = pl.ds(pl.multiple_of(k * kc, kc), kc)
            av = a_ref[0, rows, :] if three_d else a_ref[rows, :]
            acc[...] += _tn(av.astype(BF16), b_ref[rows, :])
            return c

        lax.fori_loop(0, t // kc, kstep, 0, unroll=True)
        o_ref[...] = acc[...].astype(BF16)

    (out,), landed = _call(
        body, name=name, grid=(nb,),
        in_specs=[a_spec, VMEM_FULL],
        out_specs=[pl.BlockSpec((tb, n), lambda i: (i, 0))],
        out_shape=[jax.ShapeDtypeStruct((nb * tb, n), BF16)],
        scratch_shapes=[pltpu.VMEM((tb, n), F32)],
        args=(a, b), carry=carry)
    return (out, landed) if carry is not None else out


def _mix_pre_fwd(h, g, wint, carry=None):
    t = h.shape[0]
    tm = _tile(t)

    def body(h_ref, g_ref, w_ref, u_ref, *outs):
        xhat, _ = _rms_parts(h_ref[...])
        u = (xhat * g_ref[...]).astype(BF16)
        u_ref[...] = u
        for o_ref, off, size in zip(outs, IN_OFFS, IN_SIZES):
            o_ref[...] = _nt(u, w_ref[off:off + size, :])

    return _call(
        body, name="mix_pre_fwd", grid=(t // tm,),
        in_specs=[_row_tile(tm, D_MODEL), _acc_row(D_MODEL), VMEM_FULL],
        out_specs=[_row_tile(tm, D_MODEL)] + [_row_tile(tm, s) for s in IN_SIZES],
        out_shape=[jax.ShapeDtypeStruct((t, D_MODEL), BF16)] + [jax.ShapeDtypeStruct((t, s), F32) for s in IN_SIZES],
        args=(h, g, wint), carry=carry)


def _mix_pre_bwd(h, g, wint, dh2, dz):
    t = h.shape[0]
    tm = _tile(t)

    def body(h_ref, g_ref, w_ref, dh2_ref, *rest):
        dz_refs, (dh1_ref, dg_ref) = rest[:len(IN_SIZES)], rest[len(IN_SIZES):]
        i = pl.program_id(0)
        gv = g_ref[...]
        xhat, r = _rms_parts(h_ref[...])
        du = jnp.zeros((tm, D_MODEL), F32)
        for dz_ref, off, size in zip(dz_refs, IN_OFFS, IN_SIZES):
            du = du + _nn(dz_ref[...].astype(BF16), w_ref[off:off + size, :])
        dx, dg = _rms_bwd(du, gv, xhat, r)
        dh1_ref[...] = dh2_ref[...] + dx

        @pl.when(i == 0)
        def _():
            dg_ref[...] = jnp.zeros_like(dg_ref)

        dg_ref[...] += dg

    return pl.pallas_call(
        body, name="mix_pre_bwd", grid=(t // tm,),
        in_specs=[_row_tile(tm, D_MODEL), _acc_row(D_MODEL), VMEM_FULL, _row_tile(tm, D_MODEL)]
        + [_row_tile(tm, s) for s in IN_SIZES],
        out_specs=[_row_tile(tm, D_MODEL), _acc_row(D_MODEL)],
        out_shape=[jax.ShapeDtypeStruct((t, D_MODEL), F32), jax.ShapeDtypeStruct((1, D_MODEL), F32)],
        compiler_params=_cparams(1),
    )(h, g, wint, dh2, *dz)


def _disc_math(lre, lim, ldt, bre, bim):
    dt = jnp.exp(ldt)
    mag = jnp.exp(lre * dt)
    ar = mag * jnp.cos(lim * dt)
    ai = mag * jnp.sin(lim * dt)
    den = lre * lre + lim * lim
    nr = ar - 1.0
    fr = (nr * lre + ai * lim) / den
    fi = (ai * lre - nr * lim) / den
    return ar, ai, fr[None] * bre - fi[None] * bim, fr[None] * bim + fi[None] * bre


def _s5_disc(lre, lim, ldt, bre, bim):
    def body(lre_ref, lim_ref, ldt_ref, bre_ref, bim_ref, ar_ref, ai_ref, bbr_ref, bbi_ref):
        ar, ai, bbr, bbi = _disc_math(lre_ref[...], lim_ref[...], ldt_ref[...], bre_ref[...], bim_ref[...])
        ar_ref[...] = ar
        ai_ref[...] = ai
        bbr_ref[...] = bbr
        bbi_ref[...] = bbi

    small = jax.ShapeDtypeStruct(lre.shape, F32)
    big = jax.ShapeDtypeStruct(bre.shape, F32)
    return pl.pallas_call(body, name="s5_disc", out_shape=[small, small, big, big],
                          in_specs=[VMEM_FULL] * 5, out_specs=[VMEM_FULL] * 4)(lre, lim, ldt, bre, bim)


def _s5_disc_bwd(lre, lim, ldt, bre, bim, dar, dai, dbbr, dbbi):
    def body(lre_ref, lim_ref, ldt_ref, bre_ref, bim_ref, dar_ref, dai_ref, dbbr_ref, dbbi_ref,
             glre_ref, glim_ref, gldt_ref, gbre_ref, gbim_ref):
        _, vjp = jax.vjp(_disc_math, lre_ref[...], lim_ref[...], ldt_ref[...], bre_ref[...], bim_ref[...])
        glre, glim, gldt, gbre, gbim = vjp((dar_ref[...], dai_ref[...], dbbr_ref[...], dbbi_ref[...]))
        glre_ref[...] = glre
        glim_ref[...] = glim
        gldt_ref[...] = gldt
        gbre_ref[...] = gbre
        gbim_ref[...] = gbim

    small = jax.ShapeDtypeStruct(lre.shape, F32)
    big = jax.ShapeDtypeStruct(bre.shape, F32)
    return pl.pallas_call(body, name="s5_disc_bwd",
                          out_shape=[small, small, jax.ShapeDtypeStruct(ldt.shape, F32), big, big],
                          in_specs=[VMEM_FULL] * 9, out_specs=[VMEM_FULL] * 5,
                          )(lre, lim, ldt, bre, bim, dar, dai, dbbr, dbbi)


def _cmul(ar, ai, br, bi):
    return ar * br - ai * bi, ar * bi + ai * br


def _cpow(ar, ai, n):
    rr, ri = None, None
    pr, pi = ar, ai
    while n:
        if n & 1:
            rr, ri = (pr, pi) if rr is None else _cmul(rr, ri, pr, pi)
        n >>= 1
        if n:
            pr, pi = _cmul(pr, pi, pr, pi)
    return rr, ri


def _shift_rows(v, down):
    row = lax.broadcasted_iota(jnp.int32, v.shape, 0)
    if down:
        return jnp.where(row == 0, 0.0, pltpu.roll(v, 1, 0))
    return jnp.where(row == S5_SEGS - 1, 0.0, pltpu.roll(v, S5_SEGS - 1, 0))


def _chain_segments(er, ei, pr, pi, down):
    fr, fi = er, ei
    for _ in range(S5_SEGS - 1):
        sr, si = _shift_rows(fr, down), _shift_rows(fi, down)
        mr, mi = _cmul(pr, pi, sr, si)
        fr, fi = er + mr, ei + mi
    return _shift_rows(fr, down), _shift_rows(fi, down)


def _rows_to_scan_order(src_ref, dst_ref, t):
    ls = t // S5_SEGS

    def tile(j, c):
        dst_ref[pl.ds(pl.multiple_of(j * S5_SEGS, S5_SEGS), S5_SEGS), :] = src_ref[pl.ds(j, S5_SEGS, stride=ls), :]
        return c

    lax.fori_loop(0, ls, tile, 0, unroll=8)


def _rows_from_scan_order(src_ref, dst_ref, t):
    tiles_per_seg = t // S5_SEGS // 8

    def tile(i, c):
        s, jb = i // tiles_per_seg, i % tiles_per_seg
        dst_ref[pl.ds(pl.multiple_of(i * 8, 8), 8), :] = src_ref[pl.ds(jb * 8 * S5_SEGS + s, 8, stride=S5_SEGS), :]
        return c

    lax.fori_loop(0, t // 8, tile, 0, unroll=8)


def _s5_fwd(ug, bd, ctd, ar4, ai4, dskip, carry=None):
    t = ug.shape[0]
    ls = t // S5_SEGS
    rc = min(512, t)
    ns = S5_BSTATE

    def body(ugn_ref, bd_ref, ct_ref, ar_ref, ai_ref, d_ref, xs_hbm, yn_ref, buf, ug_ref, y_ref, sem):
        cb = pl.program_id(0)
        bdv = bd_ref[0]
        _rows_to_scan_order(ugn_ref, ug_ref, t)

        def mm(i, c):
            rows = pl.ds(pl.multiple_of(i * rc, rc), rc)
            buf[rows, :] = _nn(ug_ref[rows, :].astype(BF16), bdv)
            return c

        lax.fori_loop(0, t // rc, mm, 0)
        arb = jnp.broadcast_to(ar_ref[0], (S5_SEGS, ns))
        aib = jnp.broadcast_to(ai_ref[0], (S5_SEGS, ns))

        def step(j, c, store):
            sr, si = c
            rows = pl.ds(pl.multiple_of(j * S5_SEGS, S5_SEGS), S5_SEGS)
            nr = arb * sr - aib * si + buf[rows, 0:ns]
            ni = arb * si + aib * sr + buf[rows, ns:2 * ns]
            if store:
                buf[rows, 0:ns] = nr
                buf[rows, ns:2 * ns] = ni
            return nr, ni

        zero = jnp.zeros((S5_SEGS, ns), F32)
        er, ei = lax.fori_loop(0, ls, functools.partial(step, store=False), (zero, zero))
        pr, pi = _cpow(arb, aib, ls)
        init = _chain_segments(er, ei, pr, pi, down=True)
        lax.fori_loop(0, ls, functools.partial(step, store=True), init)

        out = pltpu.make_async_copy(buf, xs_hbm.at[cb], sem)
        out.start()
        ctv = ct_ref[0]
        dv = d_ref[...]

        def ymm(i, c):
            rows = pl.ds(pl.multiple_of(i * rc, rc), rc)
            y_ref[rows, :] = _nn(buf[rows, :].astype(BF16), ctv) + dv * ug_ref[rows, :]
            return c

        lax.fori_loop(0, t // rc, ymm, 0)
        _rows_from_scan_order(y_ref, yn_ref, t)
        out.wait()

    return _call(
        body, name="s5_fwd", grid=(S5_BLOCKS,),
        in_specs=[pl.BlockSpec((t, 128), lambda i: (0, i)),
                  pl.BlockSpec((1, 128, 2 * ns), lambda i: (i, 0, 0)),
                  pl.BlockSpec((1, 2 * ns, 128), lambda i: (i, 0, 0)),
                  pl.BlockSpec((1, 1, ns), lambda i: (i, 0, 0)),
                  pl.BlockSpec((1, 1, ns), lambda i: (i, 0, 0)),
                  pl.BlockSpec((1, 128), lambda i: (0, i))],
        out_specs=[ANY, pl.BlockSpec((t, 128), lambda i: (0, i))],
        out_shape=[jax.ShapeDtypeStruct((S5_BLOCKS, t, 2 * ns), F32), jax.ShapeDtypeStruct((t, S5_WIDTH), F32)],
        scratch_shapes=[pltpu.VMEM((t, 2 * ns), F32), pltpu.VMEM((t, 128), F32), pltpu.VMEM((t, 128), F32),
                        pltpu.SemaphoreType.DMA(())],
        args=(ug, bd, ctd, ar4, ai4, dskip), carry=carry)


def _s5_bwd(dy, ug, xs, cd, bdt, ar4, ai4, dskip, carry=None):
    t = ug.shape[0]
    ls = t // S5_SEGS
    rc = min(512, t)
    ns = S5_BSTATE

    def body(dyn_ref, ugn_ref, xs_hbm, cd_ref, bdt_ref, ar_ref, ai_ref, d_ref,
             dugn_ref, dbd_ref, dcd_ref, dd_ref, dar_ref, dai_ref, xbuf, lam, dy_ref, ug_ref, dug_ref, sem):
        cb = pl.program_id(0)
        load = pltpu.make_async_copy(xs_hbm.at[cb], xbuf, sem)
        load.start()
        cdv = cd_ref[0]
        _rows_to_scan_order(dyn_ref, dy_ref, t)
        _rows_to_scan_order(ugn_ref, ug_ref, t)

        def mm(i, c):
            rows = pl.ds(pl.multiple_of(i * rc, rc), rc)
            lam[rows, :] = _nn(dy_ref[rows, :].astype(BF16), cdv)
            return c

        lax.fori_loop(0, t // rc, mm, 0)
        arb = jnp.broadcast_to(ar_ref[0], (S5_SEGS, ns))
        aib = jnp.broadcast_to(ai_ref[0], (S5_SEGS, ns))

        def lam_step(j, lr, li):
            rows = pl.ds(pl.multiple_of(j * S5_SEGS, S5_SEGS), S5_SEGS)
            nr = arb * lr + aib * li + lam[rows, 0:ns]
            ni = arb * li - aib * lr + lam[rows, ns:2 * ns]
            return rows, nr, ni

        def pass1(jj, c):
            _, nr, ni = lam_step(ls - 1 - jj, *c)
            return nr, ni

        zero = jnp.zeros((S5_SEGS, ns), F32)
        er, ei = lax.fori_loop(0, ls, pass1, (zero, zero))
        pr, pi = _cpow(arb, aib, ls)
        init = _chain_segments(er, ei, pr, -pi, down=False)
        load.wait()

        def accumulate(acc, nr, ni, xpr, xpi):
            return acc[0] + nr * xpr + ni * xpi, acc[1] + ni * xpr - nr * xpi

        def pass2(jj, c):
            lr, li, accr, acci = c
            j = ls - 1 - jj
            rows, nr, ni = lam_step(j, lr, li)
            lam[rows, 0:ns] = nr
            lam[rows, ns:2 * ns] = ni
            prev = pl.ds(pl.multiple_of((j - 1) * S5_SEGS, S5_SEGS), S5_SEGS)
            accr, acci = accumulate((accr, acci), nr, ni, xbuf[prev, 0:ns], xbuf[prev, ns:2 * ns])
            return nr, ni, accr, acci

        lr, li, accr, acci = lax.fori_loop(0, ls - 1, pass2, (init[0], init[1], zero, zero))
        rows, nr, ni = lam_step(0, lr, li)
        lam[rows, 0:ns] = nr
        lam[rows, ns:2 * ns] = ni
        last = pl.ds((ls - 1) * S5_SEGS, S5_SEGS)
        accr, acci = accumulate((accr, acci), nr, ni,
                                _shift_rows(xbuf[last, 0:ns], True), _shift_rows(xbuf[last, ns:2 * ns], True))
        dar_ref[0] = jnp.sum(accr, axis=0, keepdims=True)
        dai_ref[0] = jnp.sum(acci, axis=0, keepdims=True)

        bdtv = bdt_ref[0]
        dv = d_ref[...]
        dbd_ref[...] = jnp.zeros_like(dbd_ref)
        dcd_ref[...] = jnp.zeros_like(dcd_ref)
        dd_ref[...] = jnp.zeros_like(dd_ref)

        def tail(i, c):
            rows = pl.ds(pl.multiple_of(i * rc, rc), rc)
            dy = dy_ref[rows, :]
            ug = ug_ref[rows, :]
            lb = lam[rows, :].astype(BF16)
            dug_ref[rows, :] = _nn(lb, bdtv) + dv * dy
            dbd_ref[0] += _tn(ug.astype(BF16), lb)
            dcd_ref[0] += _tn(dy.astype(BF16), xbuf[rows, :].astype(BF16))
            dd_ref[...] += jnp.sum(dy * ug, axis=0, keepdims=True)
            return c

        lax.fori_loop(0, t // rc, tail, 0)
        _rows_from_scan_order(dug_ref, dugn_ref, t)

    chan = pl.BlockSpec((t, 128), lambda i: (0, i))
    dense = pl.BlockSpec((1, 128, 2 * ns), lambda i: (i, 0, 0))
    vec = pl.BlockSpec((1, 1, ns), lambda i: (i, 0, 0))
    return _call(
        body, name="s5_bwd", grid=(S5_BLOCKS,),
        in_specs=[chan, chan, ANY, dense, pl.BlockSpec((1, 2 * ns, 128), lambda i: (i, 0, 0)), vec, vec,
                  pl.BlockSpec((1, 128), lambda i: (0, i))],
        out_specs=[chan, dense, dense, pl.BlockSpec((1, 128), lambda i: (0, i)), vec, vec],
        out_shape=[jax.ShapeDtypeStruct((t, S5_WIDTH), F32),
                   jax.ShapeDtypeStruct((S5_BLOCKS, 128, 2 * ns), F32),
                   jax.ShapeDtypeStruct((S5_BLOCKS, 128, 2 * ns), F32),
                   jax.ShapeDtypeStruct((1, S5_WIDTH), F32),
                   jax.ShapeDtypeStruct((S5_BLOCKS, 1, ns), F32),
                   jax.ShapeDtypeStruct((S5_BLOCKS, 1, ns), F32)],
        scratch_shapes=[pltpu.VMEM((t, 2 * ns), F32), pltpu.VMEM((t, 2 * ns), F32)]
        + [pltpu.VMEM((t, 128), F32)] * 3 + [pltpu.SemaphoreType.DMA(())],
        args=(dy, ug, xs, cd, bdt, ar4, ai4, dskip), carry=carry)


def _gla_common(q, k, alow, wup, bup):
    c = GLA_CHUNK
    pre = _nn(alow.astype(BF16), wup.astype(BF16)) + bup
    la = (jnp.minimum(pre, 0.0) - jnp.log(1.0 + jnp.exp(-jnp.abs(pre)))) * (1.0 / GLA_TAU)
    rr = lax.broadcasted_iota(jnp.int32, (c, c), 0)
    cc = lax.broadcasted_iota(jnp.int32, (c, c), 1)
    tril = (rr >= cc).astype(F32)
    bc = jnp.dot(tril, la, precision=HIGHEST, preferred_element_type=F32)
    bl = bc[c - 1:c, :]
    e_pos = jnp.exp(bc)
    e_neg = jnp.exp(-bc)
    e_end = jnp.exp(bl - bc)
    qt = q * (GLA_DK ** -0.5) * e_pos
    kt = k * e_neg
    ke = k * e_end
    decb = jnp.exp(lax.dot_general(la, jnp.ones((c, GLA_DV), F32), (((0,), (0,)), ((), ())),
                                   precision=HIGHEST, preferred_element_type=F32))
    lane = lax.broadcasted_iota(jnp.int32, (1, GLA_KEY), 1)
    masks = [((lane >= h * GLA_DK) & (lane < (h + 1) * GLA_DK)).astype(F32) for h in range(GLA_HEADS)]
    return dict(pre=pre, tril=tril, bc=bc, bl=bl, e_pos=e_pos, e_neg=e_neg, e_end=e_end,
                qt=qt, kt=kt, ke=ke, decb=decb, masks=masks)


def _gla_fwd(q, k, v, alow, wup, bup, carry=None):
    t = q.shape[0]
    c = GLA_CHUNK
    n = t // c

    def body(q_ref, k_ref, v_ref, al_ref, wup_ref, bup_ref, o_ref, ss_ref, s_ref):
        i = pl.program_id(0)

        @pl.when(i == 0)
        def _():
            s_ref[...] = jnp.zeros_like(s_ref)

        m = _gla_common(q_ref[...], k_ref[...], al_ref[...], wup_ref[...], bup_ref[...])
        s = s_ref[...]
        ss_ref[0] = s
        sb = s.astype(BF16)
        ktb = m["kt"].astype(BF16)
        keb = m["ke"].astype(BF16)
        for h in range(GLA_HEADS):
            qm = (m["qt"] * m["masks"][h]).astype(BF16)
            vh = v_ref[:, h * GLA_DV:(h + 1) * GLA_DV].astype(BF16)
            p = (m["tril"] * _nt(qm, ktb)).astype(BF16)
            o_ref[:, h * GLA_DV:(h + 1) * GLA_DV] = _nn(p, vh) + _nn(qm, sb)
            rows = slice(h * GLA_DK, (h + 1) * GLA_DK)
            s_ref[rows, :] = m["decb"][rows, :] * s[rows, :] + _tn(keb, vh)[rows, :]

    return _call(
        body, name="gla_fwd", grid=(n,),
        in_specs=[_row_tile(c, GLA_KEY), _row_tile(c, GLA_KEY), _row_tile(c, GLA_VAL), _row_tile(c, GLA_RANK),
                  VMEM_FULL, VMEM_FULL],
        out_specs=[_row_tile(c, GLA_VAL), pl.BlockSpec((1, GLA_KEY, GLA_DV), lambda i: (i, 0, 0))],
        out_shape=[jax.ShapeDtypeStruct((t, GLA_VAL), F32), jax.ShapeDtypeStruct((n, GLA_KEY, GLA_DV), F32)],
        scratch_shapes=[pltpu.VMEM((GLA_KEY, GLA_DV), F32)],
        args=(q, k, v, alow, wup, bup), carry=carry)


def _gla_bwd(q, k, v, alow, wup, bup, ssave, do, carry=None):
    t = q.shape[0]
    c = GLA_CHUNK
    n = t // c

    def body(q_ref, k_ref, v_ref, al_ref, wup_ref, bup_ref, ss_ref, do_ref,
             dq_ref, dk_ref, dv_ref, dal_ref, dwup_ref, dbup_ref, ds_ref):
        i = pl.program_id(0)

        @pl.when(i == 0)
        def _():
            ds_ref[...] = jnp.zeros_like(ds_ref)
            dwup_ref[...] = jnp.zeros_like(dwup_ref)
            dbup_ref[...] = jnp.zeros_like(dbup_ref)

        alow_v = al_ref[...]
        wup_v = wup_ref[...]
        m = _gla_common(q_ref[...], k_ref[...], alow_v, wup_v, bup_ref[...])
        s = ss_ref[0]
        ds_in = ds_ref[...]
        sb = s.astype(BF16)
        dsb = ds_in.astype(BF16)
        qt, kt, ke = m["qt"], m["kt"], m["ke"]
        ktb = kt.astype(BF16)
        dqt = jnp.zeros((c, GLA_KEY), F32)
        dkt = jnp.zeros((c, GLA_KEY), F32)
        dke = jnp.zeros((c, GLA_KEY), F32)
        for h in range(GLA_HEADS):
            mask = m["masks"][h]
            qm = (qt * mask).astype(BF16)
            km = (kt * mask).astype(BF16)
            kem = (ke * mask).astype(BF16)
            cols = slice(h * GLA_DV, (h + 1) * GLA_DV)
            vh = v_ref[:, cols].astype(BF16)
            doh = do_ref[:, cols].astype(BF16)
            p = (m["tril"] * _nt(qm, ktb)).astype(BF16)
            dp = (m["tril"] * _nt(doh, vh)).astype(BF16)
            dv_ref[:, cols] = _tn(p, doh) + _nn(kem, dsb)
            dqt = dqt + _nn(dp, km) + _nt(doh, sb) * mask
            dkt = dkt + _tn(dp, qm)
            dke = dke + _nt(vh, dsb) * mask
            rows = slice(h * GLA_DK, (h + 1) * GLA_DK)
            ds_ref[rows, :] = m["decb"][rows, :] * ds_in[rows, :] + _tn(qm, doh)[rows, :]
        ddec = lax.dot_general(jnp.ones((8, GLA_DV), F32), ds_in * s, (((1,), (1,)), ((), ())),
                               precision=HIGHEST, preferred_element_type=F32)[0:1, :]
        dq_ref[...] = dqt * m["e_pos"] * (GLA_DK ** -0.5)
        dk_ref[...] = dkt * m["e_neg"] + dke * m["e_end"]
        dkeke = dke * ke
        dbl = jnp.sum(dkeke, axis=0, keepdims=True) + ddec * jnp.exp(m["bl"])
        last = (lax.broadcasted_iota(jnp.int32, (c, 1), 0) == c - 1).astype(F32)
        db_tot = dqt * qt - dkt * kt - dkeke + last * dbl
        dla = lax.dot_general(m["tril"], db_tot, (((0,), (0,)), ((), ())),
                              precision=HIGHEST, preferred_element_type=F32)
        dpre = dla * (1.0 / GLA_TAU) * jax.nn.sigmoid(-m["pre"])
        dpb = dpre.astype(BF16)
        dal_ref[...] = _nt(dpb, wup_v.astype(BF16))
        dwup_ref[...] += _tn(alow_v.astype(BF16), dpb)
        dbup_ref[...] += jnp.sum(dpre, axis=0, keepdims=True)

    def rev(d):
        return pl.BlockSpec((c, d), lambda i: (n - 1 - i, 0))

    return _call(
        body, name="gla_bwd", grid=(n,),
        in_specs=[rev(GLA_KEY), rev(GLA_KEY), rev(GLA_VAL), rev(GLA_RANK), VMEM_FULL, VMEM_FULL,
                  pl.BlockSpec((1, GLA_KEY, GLA_DV), lambda i: (n - 1 - i, 0, 0)), rev(GLA_VAL)],
        out_specs=[rev(GLA_KEY), rev(GLA_KEY), rev(GLA_VAL), rev(GLA_RANK),
                   pl.BlockSpec((GLA_RANK, GLA_KEY), lambda i: (0, 0)), _acc_row(GLA_KEY)],
        out_shape=[jax.ShapeDtypeStruct((t, GLA_KEY), F32), jax.ShapeDtypeStruct((t, GLA_KEY), F32),
                   jax.ShapeDtypeStruct((t, GLA_VAL), F32), jax.ShapeDtypeStruct((t, GLA_RANK), F32),
                   jax.ShapeDtypeStruct((GLA_RANK, GLA_KEY), F32), jax.ShapeDtypeStruct((1, GLA_KEY), F32)],
        scratch_shapes=[pltpu.VMEM((GLA_KEY, GLA_DV), F32)],
        args=(q, k, v, alow, wup, bup, ssave, do), carry=carry)


def _post_math(y, o, r, gs5, ggla, wg, bg, gn, ps5t, pglat):
    y2 = y * y
    th = jnp.tanh(GELU_C0 * (y + GELU_C1 * y * y2))
    z5 = 0.5 * y * (1.0 + th)
    z5b = z5.astype(BF16)
    gate = jax.nn.sigmoid(_nn(z5b, wg) + bg)
    ys5 = z5 * gate
    rs, on = [], []
    for h in range(GLA_HEADS):
        oh = o[:, h * GLA_DV:(h + 1) * GLA_DV]
        rh = lax.rsqrt(jnp.mean(oh * oh, axis=-1, keepdims=True) + EPS)
        rs.append(rh)
        on.append(oh * rh)
    on = jnp.concatenate(on, axis=-1)
    sr = jax.nn.sigmoid(r)
    silu_r = r * sr
    ygla = on * gn * silu_r
    ys5b, yglab = ys5.astype(BF16), ygla.astype(BF16)
    m5 = _nt(ys5b, ps5t)
    mg = _nt(yglab, pglat)
    s5g, glag = jax.nn.sigmoid(gs5), jax.nn.sigmoid(ggla)
    merged = s5g * m5 + glag * mg
    return dict(y2=y2, th=th, z5=z5, z5b=z5b, gate=gate, ys5b=ys5b, yglab=yglab, rs=rs, on=on, sr=sr,
                silu_r=silu_r, m5=m5, mg=mg, s5g=s5g, glag=glag, mergedb=merged.astype(BF16))


def _mix_post_fwd(y, o, r, gs5, ggla, h1, wg, bg, gn, ps5t, pglat, wout):
    t = o.shape[0]
    tm = _tile(t)

    def body(y_ref, o_ref, r_ref, gs5_ref, ggla_ref, h1_ref, wg_ref, bg_ref, gn_ref, ps_ref, pg_ref, wo_ref, h2_ref):
        m = _post_math(y_ref[...], o_ref[...], r_ref[...], gs5_ref[...], ggla_ref[...],
                       wg_ref[...], bg_ref[...], gn_ref[...], ps_ref[...], pg_ref[...])
        h2_ref[...] = h1_ref[...] + _nn(m["mergedb"], wo_ref[...])

    return pl.pallas_call(
        body, name="mix_post_fwd", grid=(t // tm,),
        in_specs=[_row_tile(tm, 512)] * 3 + [_row_tile(tm, D_MODEL)] * 3
        + [VMEM_FULL, _acc_row(512), _acc_row(512), VMEM_FULL, VMEM_FULL, VMEM_FULL],
        out_specs=_row_tile(tm, D_MODEL),
        out_shape=jax.ShapeDtypeStruct((t, D_MODEL), F32),
        compiler_params=_cparams(1),
    )(y, o, r, gs5, ggla, h1, wg, bg, gn, ps5t, pglat, wout)


def _mix_post_bwd(y, o, r, gs5, ggla, dh2, wg, bg, gn, ps5t, pglat, wout, carry=None):
    t = o.shape[0]
    tm = _tile(t) // 2

    def body(y_ref, o_ref, r_ref, gs5_ref, ggla_ref, dh2_ref, wg_ref, bg_ref, gn_ref, ps_ref, pg_ref, wo_ref,
             dy_ref, do_ref, dr_ref, dgs5_ref, dggla_ref, dbg_ref, dgn_ref,
             z5b_ref, dgp_ref, ys5b_ref, dm5b_ref, yglab_ref, dmgb_ref, mergedb_ref, dh2b_ref):
        i = pl.program_id(0)
        yv, ov, rv = y_ref[...], o_ref[...], r_ref[...]
        wg, gn, ps5t, pglat = wg_ref[...], gn_ref[...], ps_ref[...], pg_ref[...]
        m = _post_math(yv, ov, rv, gs5_ref[...], ggla_ref[...], wg, bg_ref[...], gn, ps5t, pglat)
        dh2b = dh2_ref[...].astype(BF16)
        dmerged = _nt(dh2b, wo_ref[...])
        s5g, glag = m["s5g"], m["glag"]
        dgs5_ref[...] = dmerged * m["m5"] * s5g * (1.0 - s5g)
        dggla_ref[...] = dmerged * m["mg"] * glag * (1.0 - glag)
        dm5b = (dmerged * s5g).astype(BF16)
        dmgb = (dmerged * glag).astype(BF16)
        dys5 = _nn(dm5b, ps5t)
        dygla = _nn(dmgb, pglat)
        gate, z5, th = m["gate"], m["z5"], m["th"]
        dgpre = dys5 * z5 * gate * (1.0 - gate)
        dgpb = dgpre.astype(BF16)
        dz5 = dys5 * gate + _nt(dgpb, wg)
        dgelu = 0.5 * (1.0 + th) + 0.5 * yv * (1.0 - th * th) * GELU_C0 * (1.0 + 3.0 * GELU_C1 * m["y2"])
        dy_ref[...] = dz5 * dgelu
        on, sr, silu_r = m["on"], m["sr"], m["silu_r"]
        dr_ref[...] = dygla * on * gn * sr * (1.0 + rv * (1.0 - sr))
        dgn = jnp.sum(dygla * on * silu_r, axis=0, keepdims=True)
        don = dygla * gn * silu_r
        for h in range(GLA_HEADS):
            cols = slice(h * GLA_DV, (h + 1) * GLA_DV)
            donh, onh = don[:, cols], on[:, cols]
            do_ref[:, cols] = m["rs"][h] * (donh - onh * jnp.mean(donh * onh, axis=-1, keepdims=True))

        @pl.when(i == 0)
        def _():
            dbg_ref[...] = jnp.zeros_like(dbg_ref)
            dgn_ref[...] = jnp.zeros_like(dgn_ref)

        dbg_ref[...] += jnp.sum(dgpre, axis=0, keepdims=True)
        dgn_ref[...] += dgn
        z5b_ref[...] = m["z5b"]
        dgp_ref[...] = dgpb
        ys5b_ref[...] = m["ys5b"]
        dm5b_ref[...] = dm5b
        yglab_ref[...] = m["yglab"]
        dmgb_ref[...] = dmgb
        mergedb_ref[...] = m["mergedb"]
        dh2b_ref[...] = dh2b

    def f32(d):
        return jax.ShapeDtypeStruct((t, d), F32)

    def b16(d):
        return jax.ShapeDtypeStruct((t, d), BF16)

    widths = (512, 512, 512, 1024, 512, 1024, 1024, 1024)
    return _call(
        body, name="mix_post_bwd", grid=(t // tm,),
        in_specs=[_row_tile(tm, 512)] * 3 + [_row_tile(tm, D_MODEL)] * 3
        + [VMEM_FULL, _acc_row(512), _acc_row(512), VMEM_FULL, VMEM_FULL, VMEM_FULL],
        out_specs=[_row_tile(tm, 512)] * 3 + [_row_tile(tm, D_MODEL)] * 2
        + [_acc_row(512)] * 2 + [_row_tile(tm, w) for w in widths],
        out_shape=[f32(512)] * 3 + [f32(D_MODEL)] * 2
        + [jax.ShapeDtypeStruct((1, 512), F32)] * 2
        + [b16(w) for w in widths],
        args=(y, o, r, gs5, ggla, dh2, wg, bg, gn, ps5t, pglat, wout), carry=carry)


def _head(h3, g, target):
    t = h3.shape[0]
    tm = _tile(t)

    def body(h_ref, g_ref, t_ref, loss_ref, dh_ref, dg_ref):
        i = pl.program_id(0)
        gv = g_ref[...]
        xhat, r = _rms_parts(h_ref[...])
        err = xhat * gv - t_ref[...]
        dx, dg = _rms_bwd(err * (1.0 / D_MODEL), gv, xhat, r)
        dh_ref[...] = dx

        @pl.when(i == 0)
        def _():
            loss_ref[...] = jnp.zeros_like(loss_ref)
            dg_ref[...] = jnp.zeros_like(dg_ref)

        loss_ref[...] += (0.5 / D_MODEL) * jnp.sum(jnp.sum(err * err, axis=1, keepdims=True), axis=0, keepdims=True)
        dg_ref[...] += dg

    return pl.pallas_call(
        body, name="head", grid=(t // tm,),
        in_specs=[_row_tile(tm, D_MODEL), _acc_row(D_MODEL), _row_tile(tm, D_MODEL)],
        out_specs=[pl.BlockSpec((1, 1), lambda i: (0, 0)), _row_tile(tm, D_MODEL), _acc_row(D_MODEL)],
        out_shape=[jax.ShapeDtypeStruct((1, 1), F32), jax.ShapeDtypeStruct((t, D_MODEL), F32),
                   jax.ShapeDtypeStruct((1, D_MODEL), F32)],
        compiler_params=_cparams(1),
    )(h3, g, target)


ADAM_TILE_ELEMS = 256 * 1024


def _adamw(w, g, m, v, name):
    rows, cols = w.shape
    tr = rows
    while tr * cols > ADAM_TILE_ELEMS and tr % 16 == 0:
        tr //= 2

    spec = pl.BlockSpec((tr, cols), lambda i: (i, 0))
    sh = jax.ShapeDtypeStruct((rows, cols), F32)
    return pl.pallas_call(functools.partial(_adamw_body), name=name, grid=(rows // tr,), in_specs=[spec] * 4,
                          out_specs=[spec] * 3, out_shape=[sh] * 3, compiler_params=_cparams(1))(w, g, m, v)


def _adamw_body(w_ref, g_ref, m_ref, v_ref, d_ref, nm_ref, nv_ref):
    gv = g_ref[...]
    nm = ADAM_B1 * m_ref[...] + (1.0 - ADAM_B1) * gv
    nv = ADAM_B2 * v_ref[...] + (1.0 - ADAM_B2) * (gv * gv)
    m_hat = nm / (1.0 - ADAM_B1 ** ADAM_STEP)
    v_hat = nv / (1.0 - ADAM_B2 ** ADAM_STEP)
    d_ref[...] = -ADAM_LR * (m_hat / (jnp.sqrt(v_hat) + ADAM_EPS) + ADAM_WD * w_ref[...])
    nm_ref[...] = nm
    nv_ref[...] = nv


def _adamw_many(ws, gs, ms, vs, name):
    n = len(ws)

    def body(*refs):
        ins, outs = refs[:4 * n], refs[4 * n:]
        for i in range(n):
            _adamw_body(*(ins[j * n + i] for j in range(4)), *(outs[j * n + i] for j in range(3)))

    shapes = [jax.ShapeDtypeStruct(w.shape, F32) for w in ws]
    res = pl.pallas_call(body, name=name, in_specs=[VMEM_FULL] * (4 * n), out_specs=[VMEM_FULL] * (3 * n),
                         out_shape=shapes * 3)(*ws, *gs, *ms, *vs)
    return res[:n], res[n:2 * n], res[2 * n:]


def _exchange(srcs, scatter, name):
    n = len(srcs)

    def body(*refs):
        _exchange_start(refs[:n], refs[n:2 * n], *refs[2 * n:], scatter=scatter)
        _exchange_wait(refs[:n], refs[n:2 * n], *refs[2 * n:], scatter=scatter)

    return pl.pallas_call(
        body, name=name, in_specs=[ANY] * n, out_specs=[ANY] * n,
        out_shape=_exchange_shapes(srcs, scatter), scratch_shapes=_exchange_sems(n),
    )(*srcs)


def _sum_slabs(slabs, name):
    n = slabs.shape[0]

    def body(s_ref, o_ref):
        acc = s_ref[0].astype(F32)
        for s in range(1, n):
            acc = acc + s_ref[s].astype(F32)
        o_ref[...] = acc

    return pl.pallas_call(
        body, name=name, in_specs=[VMEM_FULL], out_specs=VMEM_FULL,
        out_shape=jax.ShapeDtypeStruct(slabs.shape[1:], F32),
        compiler_params=pltpu.CompilerParams(vmem_limit_bytes=VMEM_LIMIT_BYTES),
    )(slabs)


BIG = ("ffn1_w1", "ffn1_w3", "ffn1_w2", "w_in", "s5_glu_w", "gla_a_up_w", "proj_s5", "proj_gla", "w_out",
       "ffn2_w1", "ffn2_w3", "ffn2_w2")
GROUPS = (("ffn1_w1", "ffn1_w3", "ffn1_w2"),
          ("w_in", "s5_glu_w", "gla_a_up_w", "proj_s5", "proj_gla", "w_out"),
          ("ffn2_w1", "ffn2_w3", "ffn2_w2"))
W_IN_ROWS = 514
W_IN_PAD = 528
UP_COLS = 32
COL_SHARDED = ("ffn1_w1", "ffn1_w3", "w_in", "proj_s5", "proj_gla", "ffn2_w1", "ffn2_w3")

SMALL = ("ffn1_norm", "mix_norm", "s5_lambda_re", "s5_lambda_im", "s5_log_dt", "s5_b_re", "s5_b_im", "s5_c_re",
         "s5_c_im", "s5_d", "s5_glu_b", "gla_a_up_b", "gla_out_norm", "ffn2_norm", "final_norm")
SMALL_SHAPES = dict(ffn1_norm=(1, 1024), mix_norm=(1, 1024), s5_lambda_re=(1, 32, 64), s5_lambda_im=(1, 32, 64),
                    s5_log_dt=(1, 32), s5_b_re=(1, 32, 64, 16), s5_b_im=(1, 32, 64, 16), s5_c_re=(1, 32, 16, 64),
                    s5_c_im=(1, 32, 16, 64), s5_d=(1, 32, 16), s5_glu_b=(1, 512), gla_a_up_b=(1, 256),
                    gla_out_norm=(1, 512), ffn2_norm=(1, 1024), final_norm=(1024,))
SMALL_N = sum(math.prod(s) for s in SMALL_SHAPES.values())
SMALL_R = -(-SMALL_N // (64 * 1024)) * 64


def _shard_rows(name, a):
    if name == "gla_a_up_w":
        return jnp.pad(a, ((0, 0), (0, 128 - UP_COLS)))
    if name in COL_SHARDED:
        a = a.T
    if name == "w_in":
        return jnp.pad(a, ((0, W_IN_PAD - W_IN_ROWS), (0, 0)))
    return a.reshape(-1, 1024)


def _unshard_rows(name, rows, shape):
    if name == "gla_a_up_w":
        return rows[:, :UP_COLS]
    if name == "w_in":
        rows = rows[:W_IN_ROWS]
    if name in COL_SHARDED:
        return rows.reshape(shape[1], shape[0]).T
    return rows.reshape(shape)


def _pack_small(vals):
    flat = jnp.concatenate([vals[n].reshape(-1).astype(F32) for n in SMALL])
    return jnp.pad(flat, (0, SMALL_R * 1024 - SMALL_N)).reshape(SMALL_R, 1024)


def _unpack_small(slab):
    flat = slab.reshape(-1)
    out, off = {}, 0
    for n in SMALL:
        size = math.prod(SMALL_SHAPES[n])
        out[n] = flat[off:off + size].reshape(SMALL_SHAPES[n])
        off += size
    return out


FULL_SHAPES = dict(w_in=(IN_COLS, D_MODEL), s5_glu_w=(S5_WIDTH, S5_WIDTH), gla_a_up_w=(GLA_RANK, GLA_KEY),
                   proj_s5=(D_MODEL, S5_WIDTH), proj_gla=(D_MODEL, GLA_VAL), w_out=(D_MODEL, D_MODEL))


def _full_weight(name, gathered):
    if name == "gla_a_up_w":
        return gathered[:, :, :UP_COLS].transpose(1, 0, 2).reshape(GLA_RANK, GLA_KEY)
    if name == "w_in":
        gathered = gathered[:, :W_IN_ROWS]
    return gathered.reshape(FULL_SHAPES.get(name, (D_FF, D_MODEL)))


def _grad_slabs(name, g):
    if name == "gla_a_up_w":
        g = g.reshape(GLA_RANK, N_DEV, UP_COLS).transpose(1, 0, 2)
        return jnp.pad(g, ((0, 0), (0, 0), (0, 128 - UP_COLS))).astype(BF16)
    if name == "w_in":
        return jnp.pad(g.reshape(N_DEV, W_IN_ROWS, D_MODEL), ((0, 0), (0, W_IN_PAD - W_IN_ROWS), (0, 0)))
    return g.reshape(N_DEV, -1, 1024)


def _s5_dense(re, im, sign_im):
    eye = jnp.eye(8, dtype=F32)

    def one(a):
        a = a.reshape(S5_BLOCKS, 8, S5_GROUP, S5_STATE)
        return jnp.einsum("cghp,gk->cghkp", a, eye).reshape(S5_BLOCKS, 128, S5_BSTATE)

    return jnp.concatenate([one(re), sign_im * one(im)], axis=-1)


def _s5_undense(d):
    eye = jnp.eye(8, dtype=F32)

    def one(a):
        a = a.reshape(S5_BLOCKS, 8, S5_GROUP, 8, S5_STATE)
        return jnp.einsum("cghkp,gk->cghp", a, eye).reshape(S5_GROUPS, S5_GROUP, S5_STATE)

    return one(d[..., :S5_BSTATE]), one(d[..., S5_BSTATE:])


def _local_step(x, target, p, w, rows=None):
    w = dict(w or {})
    landed_grads = {}

    def gather(names):
        return None if rows is None else ([rows[n] for n in names], False)

    def gathered(names, landed):
        w.update({n: _full_weight(n, g) for n, g in zip(names, landed)})

    def scatter(names):
        return None if rows is None else ([_grad_slabs(n, big[n]) for n in names], True)

    def scattered(names, landed):
        landed_grads.update(zip(names, landed))

    if rows is not None:
        gathered(GROUPS[0], _exchange(gather(GROUPS[0])[0], False, "gather_ffn1"))
    g1, gm, g2 = p["ffn1_norm"], p["mix_norm"], p["ffn2_norm"]
    gf = p["final_norm"].reshape(1, D_MODEL)
    lre, lim = p["s5_lambda_re"][0], p["s5_lambda_im"][0]
    ldt = p["s5_log_dt"][0].reshape(S5_GROUPS, 1)
    bre = p["s5_b_re"][0].transpose(2, 0, 1)
    bim = p["s5_b_im"][0].transpose(2, 0, 1)
    cre, cim = p["s5_c_re"][0], p["s5_c_im"][0]
    dskip = p["s5_d"][0].reshape(1, S5_WIDTH)
    bg, bup, gn = p["s5_glu_b"], p["gla_a_up_b"], p["gla_out_norm"]

    h1, got = _ffn_fwd(x, g1, w["ffn1_w1"], w["ffn1_w3"], w["ffn1_w2"], "ffn1_fwd", gather(GROUPS[1]))
    gathered(GROUPS[1], got)
    wup = w["gla_a_up_w"].astype(F32)
    (u, s5in, q, k, v, r, alow, gs5, ggla), got = _mix_pre_fwd(h1, gm, w["w_in"], gather(GROUPS[2][:1]))
    gathered(GROUPS[2][:1], got)
    ar, ai, bbr, bbi = _s5_disc(lre, lim, ldt, bre, bim)
    bd = _s5_dense(bbr.transpose(1, 0, 2), bbi.transpose(1, 0, 2), 1.0)
    cd = _s5_dense(cre, cim, -1.0)
    bd16, cd16 = bd.astype(BF16), cd.astype(BF16)
    bdt16, ctd16 = bd16.transpose(0, 2, 1), cd16.transpose(0, 2, 1)
    ar4 = ar.reshape(S5_BLOCKS, 1, S5_BSTATE)
    ai4 = ai.reshape(S5_BLOCKS, 1, S5_BSTATE)
    (xs, y), got = _s5_fwd(s5in, bd16, ctd16, ar4, ai4, dskip, gather(GROUPS[2][1:2]))
    gathered(GROUPS[2][1:2], got)
    (o, ssave), got = _gla_fwd(q, k, v, alow, wup, bup, gather(GROUPS[2][2:]))
    gathered(GROUPS[2][2:], got)
    post_w = (w["s5_glu_w"], bg, gn, w["proj_s5"], w["proj_gla"], w["w_out"])
    h2 = _mix_post_fwd(y, o, r, gs5, ggla, h1, *post_w)
    h3, _ = _ffn_fwd(h2, g2, w["ffn2_w1"], w["ffn2_w3"], w["ffn2_w2"], "ffn2_fwd")
    loss, dh3, dgf = _head(h3, gf, target)

    big, small = {}, {}
    small["final_norm"] = dgf.reshape(D_MODEL)
    (dh2, dg2, da3, db3, s3, n2, dhh2), _ = _ffn_bwd(
        h2, dh3, g2, w["ffn2_w1"], w["ffn2_w3"], w["ffn2_w2"], "ffn2_bwd")
    small["ffn2_norm"] = dg2
    big["ffn2_w1"] = _mm_tn(da3, n2, "ffn2_dw1")
    big["ffn2_w3"] = _mm_tn(db3, n2, "ffn2_dw3")
    big["ffn2_w2"] = _mm_tn(s3, dhh2, "ffn2_dw2")
    (dy, do, dr, dgs5, dggla, dbg, dgn, z5b, dgpb, ys5b, dm5b, yglab, dmgb, mergedb, dh2b), got = _mix_post_bwd(
        y, o, r, gs5, ggla, dh2, *post_w, carry=scatter(GROUPS[2][:1]))
    scattered(GROUPS[2][:1], got)
    small["s5_glu_b"] = dbg
    small["gla_out_norm"] = dgn
    big["s5_glu_w"] = _mm_tn(z5b, dgpb, "glu_dw")
    big["proj_s5"] = _mm_tn(dm5b, ys5b, "proj_s5_dw")
    big["proj_gla"] = _mm_tn(dmgb, yglab, "proj_gla_dw")
    big["w_out"] = _mm_tn(mergedb, dh2b, "w_out_dw")
    (dq, dk, dv, dalow, dwup, dbup), got = _gla_bwd(q, k, v, alow, wup, bup, ssave, do, scatter(GROUPS[2][1:2]))
    scattered(GROUPS[2][1:2], got)
    big["gla_a_up_w"] = dwup
    small["gla_a_up_b"] = dbup
    (ds5in, dbd, dcd, dd, dar4, dai4), got = _s5_bwd(
        dy, s5in, xs, cd16, bdt16, ar4, ai4, dskip, scatter(GROUPS[2][2:]))
    scattered(GROUPS[2][2:], got)
    dbbr, dbbi = _s5_undense(dbd)
    dcre, dcim_neg = _s5_undense(dcd)
    glre, glim, gldt, gbre, gbim = _s5_disc_bwd(
        lre, lim, ldt, bre, bim, dar4.reshape(S5_GROUPS, S5_STATE), dai4.reshape(S5_GROUPS, S5_STATE),
        dbbr.transpose(1, 0, 2), dbbi.transpose(1, 0, 2))
    small["s5_lambda_re"] = glre[None]
    small["s5_lambda_im"] = glim[None]
    small["s5_log_dt"] = gldt.reshape(1, S5_GROUPS)
    small["s5_b_re"] = gbre.transpose(1, 2, 0)[None]
    small["s5_b_im"] = gbim.transpose(1, 2, 0)[None]
    small["s5_c_re"] = dcre[None]
    small["s5_c_im"] = -dcim_neg[None]
    small["s5_d"] = dd.reshape(1, S5_GROUPS, S5_GROUP)
    dz = (ds5in, dq, dk, dv, dr, dalow, dgs5, dggla)
    dh1, dgm = _mix_pre_bwd(h1, gm, w["w_in"], dh2, dz)
    small["mix_norm"] = dgm
    big["w_in"] = jnp.concatenate([_mm_tn(d, u, "w_in_dw%d" % i) for i, d in enumerate(dz)], axis=0)
    (dx, dg1, da3, db3, s3, n1, dhh1), got = _ffn_bwd(
        x, dh1, g1, w["ffn1_w1"], w["ffn1_w3"], w["ffn1_w2"], "ffn1_bwd", scatter(GROUPS[1]))
    scattered(GROUPS[1], got)
    small["ffn1_norm"] = dg1
    big["ffn1_w1"] = _mm_tn(da3, n1, "ffn1_dw1")
    if rows is None:
        big["ffn1_w3"] = _mm_tn(db3, n1, "ffn1_dw3")
        big["ffn1_w2"] = _mm_tn(s3, dhh1, "ffn1_dw2")
        return loss[0, 0], dx, big, small
    big["ffn1_w3"], got = _mm_tn(db3, n1, "ffn1_dw3", scatter(GROUPS[0][:1]))
    scattered(GROUPS[0][:1], got)
    big["ffn1_w2"], got = _mm_tn(s3, dhh1, "ffn1_dw2", scatter(GROUPS[0][1:2]))
    scattered(GROUPS[0][1:2], got)
    scattered(GROUPS[0][2:], _exchange(scatter(GROUPS[0][2:])[0], True, "scatter_ffn1_w2"))
    return loss[0, 0], dx, landed_grads, small


NAMES = ("ffn1_norm", "ffn1_w1", "ffn1_w3", "ffn1_w2", "mix_norm", "w_in", "s5_lambda_re", "s5_lambda_im",
         "s5_log_dt", "s5_b_re", "s5_b_im", "s5_c_re", "s5_c_im", "s5_d", "s5_glu_w", "s5_glu_b", "gla_a_up_w",
         "gla_a_up_b", "gla_out_norm", "proj_s5", "proj_gla", "w_out", "ffn2_norm", "ffn2_w1", "ffn2_w3", "ffn2_w2",
         "final_norm")


def kernel(*args):
    nw = len(NAMES)
    x = args[0][0]
    wts = dict(zip(NAMES, args[1:1 + nw]))
    target = args[1 + nw][0]
    mom = dict(zip(NAMES, args[2 + nw:2 + 2 * nw]))
    var = dict(zip(NAMES, args[2 + 2 * nw:2 + 3 * nw]))

    shards = {n: wts[n][0] for n in BIG}
    rows = {n: _shard_rows(n, shards[n]).astype(BF16) for n in BIG}
    loss, dx, landed, small = _local_step(x, target, {n: wts[n] for n in SMALL}, None, rows)
    loss = lax.psum(loss, ("x", "y", "c"))

    grad, delta, new_m, new_v = {}, {}, {}, {}
    for n in BIG:
        g = _unshard_rows(n, _sum_slabs(landed[n], "sum_" + n), shards[n].shape)
        grad[n] = g[None]
        delta[n], new_m[n], new_v[n] = (a[None] for a in _adamw(shards[n], g, mom[n][0], var[n][0], "adamw_" + n))

    part = _pack_small(small).reshape(N_DEV, SMALL_R // N_DEV, 1024)
    mine = _sum_slabs(_exchange([part], True, "scatter_small")[0], "sum_small")
    g_small = _exchange([mine], False, "gather_small")[0].reshape(SMALL_R, 1024)
    grad.update(_unpack_small(g_small))

    def flat2d(a):
        return a.reshape(-1, a.shape[-1])

    outs = _adamw_many(*([flat2d(d[n]) for n in SMALL] for d in (wts, grad, mom, var)), "adamw_small")
    for out, arrays in zip((delta, new_m, new_v), outs):
        out.update({n: a.reshape(SMALL_SHAPES[n]) for n, a in zip(SMALL, arrays)})
    return (loss, dx[None], *(d[n] for d in (grad, delta, new_m, new_v) for n in NAMES))
```

```python
import functools
import math

import jax
import jax.numpy as jnp
from jax import lax
from jax.experimental import pallas as pl
from jax.experimental.pallas import tpu as pltpu

F32, BF16 = jnp.float32, jnp.bfloat16
HIGHEST = lax.Precision.HIGHEST

D_MODEL = 1024
D_FF = 2816
N_DEV = 8
S5_WIDTH, S5_GROUPS, S5_GROUP, S5_STATE = 512, 32, 16, 64
S5_BLOCKS = 4
S5_BSTATE = 512
S5_SEGS = 8
GLA_HEADS, GLA_DK, GLA_DV = 4, 64, 128
GLA_KEY, GLA_VAL, GLA_RANK, GLA_CHUNK = 256, 512, 16, 64
GLA_TAU = 16.0
GLA_STEP_CHUNKS = 4
EPS = 1e-6
IN_SIZES = (512, 256, 256, 512, 512, 16, 1024, 1024)
IN_OFFS = tuple(sum(IN_SIZES[:i]) for i in range(len(IN_SIZES)))
IN_COLS = sum(IN_SIZES)
ADAM_LR, ADAM_B1, ADAM_B2, ADAM_EPS, ADAM_WD, ADAM_STEP = 0.001, 0.9, 0.999, 1e-08, 0.01, 10
GELU_C0 = math.sqrt(2.0 / math.pi)
GELU_C1 = 0.044715

FFN_FT = 256
VMEM_LIMIT_BYTES = 56 * 1024 * 1024

VMEM_FULL = pl.BlockSpec(memory_space=pltpu.VMEM)
ANY = pl.BlockSpec(memory_space=pl.ANY)


def _cparams(n_grid):
    return pltpu.CompilerParams(dimension_semantics=("arbitrary",) * n_grid, vmem_limit_bytes=VMEM_LIMIT_BYTES)


def _tile(t):
    return 512 if t >= 1024 else t // 2


def _nn(a, b):
    return jnp.dot(a, b, preferred_element_type=F32)


def _nt(a, b):
    return lax.dot_general(a, b, (((1,), (1,)), ((), ())), preferred_element_type=F32)


def _tn(a, b):
    return lax.dot_general(a, b, (((0,), (0,)), ((), ())), preferred_element_type=F32)


def _rms_parts(x):
    r = lax.rsqrt(jnp.mean(x * x, axis=-1, keepdims=True) + EPS)
    return x * r, r


def _rms_bwd(dn, g, xhat, r):
    dxh = dn * g
    dx = r * (dxh - xhat * jnp.mean(dxh * xhat, axis=-1, keepdims=True))
    return dx, jnp.sum(dn * xhat, axis=0, keepdims=True)


def _peers():
    x, y, c = lax.axis_index("x"), lax.axis_index("y"), lax.axis_index("c")
    out = []
    for k in range(1, N_DEV):
        px = 1 - x if k & 4 else x
        py = 1 - y if k & 2 else y
        pc = 1 - c if k & 1 else c
        out.append(((px, py, pc), 4 * px + 2 * py + pc))
    return 4 * x + 2 * y + c, out


def _exchange_copies(src_refs, out_refs, send_sems, recv_sems, local_sems, scatter, with_recvs):
    me, peers = _peers()
    locals_, sends, recvs = [], [], []
    for a, (src_ref, out_ref) in enumerate(zip(src_refs, out_refs)):
        def mine(idx, src_ref=src_ref):
            return src_ref.at[idx] if scatter else src_ref

        locals_.append(pltpu.make_async_copy(mine(me), out_ref.at[me], local_sems.at[a]))
        for k, (dev, idx) in enumerate(peers):
            sends.append(pltpu.make_async_remote_copy(
                src_ref=mine(idx), dst_ref=out_ref.at[me], send_sem=send_sems.at[a, k], recv_sem=recv_sems.at[a, k],
                device_id=dev, device_id_type=pl.DeviceIdType.MESH))
            if with_recvs:
                recvs.append(pltpu.make_async_remote_copy(
                    src_ref=mine(idx), dst_ref=out_ref.at[idx], send_sem=send_sems.at[a, k],
                    recv_sem=recv_sems.at[a, k], device_id=dev, device_id_type=pl.DeviceIdType.MESH))
    return locals_, sends, recvs


def _exchange_start(*refs, scatter):
    locals_, sends, _ = _exchange_copies(*refs, scatter=scatter, with_recvs=False)
    for cp in locals_ + sends:
        cp.start()


def _exchange_wait(*refs, scatter):
    locals_, sends, recvs = _exchange_copies(*refs, scatter=scatter, with_recvs=True)
    for cp in recvs:
        cp.wait_recv()
    for cp in sends:
        cp.wait_send()
    for cp in locals_:
        cp.wait()


def _exchange_sems(n_arrays):
    return [pltpu.SemaphoreType.DMA((n_arrays, N_DEV - 1)), pltpu.SemaphoreType.DMA((n_arrays, N_DEV - 1)),
            pltpu.SemaphoreType.DMA((n_arrays,))]


def _exchange_shapes(srcs, scatter):
    return [jax.ShapeDtypeStruct((N_DEV,) + tuple(s.shape[1:] if scatter else s.shape), s.dtype) for s in srcs]


def _call(body, *, name, grid, in_specs, out_specs, out_shape, args, scratch_shapes=(), carry=None):
    n_in, n_out, n_scr = len(in_specs), len(out_specs), len(scratch_shapes)
    srcs, scatter = carry if carry is not None else ((), False)
    nc = len(srcs)

    def wrapped(*refs):
        ins, refs = refs[:n_in], refs[n_in:]
        csrc, refs = refs[:nc], refs[nc:]
        outs, refs = refs[:n_out], refs[n_out:]
        cland, refs = refs[:nc], refs[nc:]
        scr, sems = refs[:n_scr], refs[n_scr:]
        if nc:
            @pl.when(pl.program_id(0) == 0)
            def _():
                _exchange_start(csrc, cland, *sems, scatter=scatter)

        body(*ins, *outs, *scr)
        if nc:
            @pl.when(pl.program_id(0) == grid[0] - 1)
            def _():
                _exchange_wait(csrc, cland, *sems, scatter=scatter)

    res = pl.pallas_call(
        wrapped, name=name, grid=grid,
        in_specs=list(in_specs) + [ANY] * nc, out_specs=list(out_specs) + [ANY] * nc,
        out_shape=list(out_shape) + _exchange_shapes(srcs, scatter),
        scratch_shapes=list(scratch_shapes) + (_exchange_sems(nc) if nc else []),
        compiler_params=_cparams(1),
    )(*args, *srcs)
    return res[:n_out], res[n_out:]


def _row_tile(tm, d):
    return pl.BlockSpec((tm, d), lambda i: (i, 0))


def _acc_row(d):
    return pl.BlockSpec((1, d), lambda i: (0, 0))


def _ffn_fwd(x, g, w1t, w3t, w2, name, carry=None):
    t = x.shape[0]
    tm = _tile(t)
    nf = D_FF // FFN_FT

    def body(x_ref, g_ref, w1_ref, w3_ref, w2_ref, o_ref):
        xv = x_ref[...]
        xhat, _ = _rms_parts(xv)
        n = (xhat * g_ref[...]).astype(BF16)
        o_ref[...] = xv

        def fstep(f, c):
            rows = pl.ds(pl.multiple_of(f * FFN_FT, FFN_FT), FFN_FT)
            a = _nt(n, w1_ref[rows, :])
            b = _nt(n, w3_ref[rows, :])
            s = (a * jax.nn.sigmoid(a) * b).astype(BF16)
            o_ref[...] += 0.5 * _nn(s, w2_ref[rows, :])
            return c

        lax.fori_loop(0, nf, fstep, 0, unroll=True)

    (h,), landed = _call(
        body, name=name, grid=(t // tm,),
        in_specs=[_row_tile(tm, D_MODEL), _acc_row(D_MODEL), VMEM_FULL, VMEM_FULL, VMEM_FULL],
        out_specs=[_row_tile(tm, D_MODEL)],
        out_shape=[jax.ShapeDtypeStruct((t, D_MODEL), F32)],
        args=(x, g, w1t, w3t, w2), carry=carry)
    return h, landed


def _ffn_bwd(x, dh, g, w1t, w3t, w2, name, carry=None):
    t = x.shape[0]
    tm = _tile(t) // 2
    nf = D_FF // FFN_FT

    def body(x_ref, dh_ref, g_ref, w1_ref, w3_ref, w2_ref,
             dx_ref, dg_ref, da_ref, db_ref, s_ref, n_ref, dhh_ref, dn_acc):
        i = pl.program_id(0)
        xv = x_ref[...]
        gv = g_ref[...]
        xhat, r = _rms_parts(xv)
        n = (xhat * gv).astype(BF16)
        n_ref[...] = n
        dhv = dh_ref[...]
        dhh = (0.5 * dhv).astype(BF16)
        dhh_ref[...] = dhh
        dn_acc[...] = jnp.zeros_like(dn_acc)

        def fstep(f, c):
            rows = pl.ds(pl.multiple_of(f * FFN_FT, FFN_FT), FFN_FT)
            w1c, w3c, w2c = w1_ref[rows, :], w3_ref[rows, :], w2_ref[rows, :]
            a = _nt(n, w1c)
            b = _nt(n, w3c)
            sg = jax.nn.sigmoid(a)
            sl = a * sg
            ds = _nt(dhh, w2c)
            da = (ds * b * sg * (1.0 + a * (1.0 - sg))).astype(BF16)
            db = (ds * sl).astype(BF16)
            s_ref[f] = (sl * b).astype(BF16)
            da_ref[f] = da
            db_ref[f] = db
            dn_acc[...] += _nn(da, w1c) + _nn(db, w3c)
            return c

        lax.fori_loop(0, nf, fstep, 0, unroll=True)
        dx, dg = _rms_bwd(dn_acc[...], gv, xhat, r)
        dx_ref[...] = dhv + dx

        @pl.when(i == 0)
        def _():
            dg_ref[...] = jnp.zeros_like(dg_ref)

        dg_ref[...] += dg

    blk3 = pl.BlockSpec((nf, tm, FFN_FT), lambda i: (0, i, 0))
    sh3 = jax.ShapeDtypeStruct((nf, t, FFN_FT), BF16)
    return _call(
        body, name=name, grid=(t // tm,),
        in_specs=[_row_tile(tm, D_MODEL), _row_tile(tm, D_MODEL), _acc_row(D_MODEL), VMEM_FULL, VMEM_FULL, VMEM_FULL],
        out_specs=[_row_tile(tm, D_MODEL), _acc_row(D_MODEL), blk3, blk3, blk3,
                   _row_tile(tm, D_MODEL), _row_tile(tm, D_MODEL)],
        out_shape=[jax.ShapeDtypeStruct((t, D_MODEL), F32), jax.ShapeDtypeStruct((1, D_MODEL), F32), sh3, sh3, sh3,
                   jax.ShapeDtypeStruct((t, D_MODEL), BF16), jax.ShapeDtypeStruct((t, D_MODEL), BF16)],
        scratch_shapes=[pltpu.VMEM((tm, D_MODEL), F32)],
        args=(x, dh, g, w1t, w3t, w2), carry=carry)


def _mm_tn(a, b, name, carry=None):
    t, n = b.shape
    kc = min(512, t)
    if a.ndim == 3:
        nb, _, tb = a.shape
        a_spec = pl.BlockSpec((1, t, tb), lambda i: (i, 0, 0))
    else:
        m = a.shape[1]
        tb = min(m, 256)
        nb = m // tb
        a_spec = pl.BlockSpec((t, tb), lambda i: (0, i))
    three_d = a.ndim == 3

    def body(a_ref, b_ref, o_ref, acc):
        acc[...] = jnp.zeros_like(acc)

        def kstep(k, c):
            rows = pl.ds(pl.multiple_of(k * kc, kc), kc)
            av = a_ref[0, rows, :] if three_d else a_ref[rows, :]
            acc[...] += _tn(av.astype(BF16), b_ref[rows, :])
            return c

        lax.fori_loop(0, t // kc, kstep, 0, unroll=True)
        o_ref[...] = acc[...].astype(BF16)

    (out,), landed = _call(
        body, name=name, grid=(nb,),
        in_specs=[a_spec, VMEM_FULL],
        out_specs=[pl.BlockSpec((tb, n), lambda i: (i, 0))],
        out_shape=[jax.ShapeDtypeStruct((nb * tb, n), BF16)],
        scratch_shapes=[pltpu.VMEM((tb, n), F32)],
        args=(a, b), carry=carry)
    return (out, landed) if carry is not None else out


def _mix_pre_fwd(h, g, wint, carry=None):
    t = h.shape[0]
    tm = _tile(t)

    def body(h_ref, g_ref, w_ref, u_ref, *outs):
        xhat, _ = _rms_parts(h_ref[...])
        u = (xhat * g_ref[...]).astype(BF16)
        u_ref[...] = u
        for o_ref, off, size in zip(outs, IN_OFFS, IN_SIZES):
            o_ref[...] = _nt(u, w_ref[off:off + size, :])

    return _call(
        body, name="mix_pre_fwd", grid=(t // tm,),
        in_specs=[_row_tile(tm, D_MODEL), _acc_row(D_MODEL), VMEM_FULL],
        out_specs=[_row_tile(tm, D_MODEL)] + [_row_tile(tm, s) for s in IN_SIZES],
        out_shape=[jax.ShapeDtypeStruct((t, D_MODEL), BF16)] + [jax.ShapeDtypeStruct((t, s), F32) for s in IN_SIZES],
        args=(h, g, wint), carry=carry)


def _mix_pre_bwd(h, g, wint, dh2, dz):
    t = h.shape[0]
    tm = _tile(t)

    def body(h_ref, g_ref, w_ref, dh2_ref, *rest):
        dz_refs, (dh1_ref, dg_ref) = rest[:len(IN_SIZES)], rest[len(IN_SIZES):]
        i = pl.program_id(0)
        gv = g_ref[...]
        xhat, r = _rms_parts(h_ref[...])
        du = jnp.zeros((tm, D_MODEL), F32)
        for dz_ref, off, size in zip(dz_refs, IN_OFFS, IN_SIZES):
            du = du + _nn(dz_ref[...].astype(BF16), w_ref[off:off + size, :])
        dx, dg = _rms_bwd(du, gv, xhat, r)
        dh1_ref[...] = dh2_ref[...] + dx

        @pl.when(i == 0)
        def _():
            dg_ref[...] = jnp.zeros_like(dg_ref)

        dg_ref[...] += dg

    return pl.pallas_call(
        body, name="mix_pre_bwd", grid=(t // tm,),
        in_specs=[_row_tile(tm, D_MODEL), _acc_row(D_MODEL), VMEM_FULL, _row_tile(tm, D_MODEL)]
        + [_row_tile(tm, s) for s in IN_SIZES],
        out_specs=[_row_tile(tm, D_MODEL), _acc_row(D_MODEL)],
        out_shape=[jax.ShapeDtypeStruct((t, D_MODEL), F32), jax.ShapeDtypeStruct((1, D_MODEL), F32)],
        compiler_params=_cparams(1),
    )(h, g, wint, dh2, *dz)


def _disc_math(lre, lim, ldt, bre, bim):
    dt = jnp.exp(ldt)
    mag = jnp.exp(lre * dt)
    ar = mag * jnp.cos(lim * dt)
    ai = mag * jnp.sin(lim * dt)
    den = lre * lre + lim * lim
    nr = ar - 1.0
    fr = (nr * lre + ai * lim) / den
    fi = (ai * lre - nr * lim) / den
    return ar, ai, fr[None] * bre - fi[None] * bim, fr[None] * bim + fi[None] * bre


def _s5_disc(lre, lim, ldt, bre, bim):
    def body(lre_ref, lim_ref, ldt_ref, bre_ref, bim_ref, ar_ref, ai_ref, bbr_ref, bbi_ref):
        ar, ai, bbr, bbi = _disc_math(lre_ref[...], lim_ref[...], ldt_ref[...], bre_ref[...], bim_ref[...])
        ar_ref[...] = ar
        ai_ref[...] = ai
        bbr_ref[...] = bbr
        bbi_ref[...] = bbi

    small = jax.ShapeDtypeStruct(lre.shape, F32)
    big = jax.ShapeDtypeStruct(bre.shape, F32)
    return pl.pallas_call(body, name="s5_disc", out_shape=[small, small, big, big],
                          in_specs=[VMEM_FULL] * 5, out_specs=[VMEM_FULL] * 4)(lre, lim, ldt, bre, bim)


def _s5_disc_bwd(lre, lim, ldt, bre, bim, dar, dai, dbbr, dbbi):
    def body(lre_ref, lim_ref, ldt_ref, bre_ref, bim_ref, dar_ref, dai_ref, dbbr_ref, dbbi_ref,
             glre_ref, glim_ref, gldt_ref, gbre_ref, gbim_ref):
        _, vjp = jax.vjp(_disc_math, lre_ref[...], lim_ref[...], ldt_ref[...], bre_ref[...], bim_ref[...])
        glre, glim, gldt, gbre, gbim = vjp((dar_ref[...], dai_ref[...], dbbr_ref[...], dbbi_ref[...]))
        glre_ref[...] = glre
        glim_ref[...] = glim
        gldt_ref[...] = gldt
        gbre_ref[...] = gbre
        gbim_ref[...] = gbim

    small = jax.ShapeDtypeStruct(lre.shape, F32)
    big = jax.ShapeDtypeStruct(bre.shape, F32)
    return pl.pallas_call(body, name="s5_disc_bwd",
                          out_shape=[small, small, jax.ShapeDtypeStruct(ldt.shape, F32), big, big],
                          in_specs=[VMEM_FULL] * 9, out_specs=[VMEM_FULL] * 5,
                          )(lre, lim, ldt, bre, bim, dar, dai, dbbr, dbbi)


def _cmul(ar, ai, br, bi):
    return ar * br - ai * bi, ar * bi + ai * br


def _cpow(ar, ai, n):
    rr, ri = None, None
    pr, pi = ar, ai
    while n:
        if n & 1:
            rr, ri = (pr, pi) if rr is None else _cmul(rr, ri, pr, pi)
        n >>= 1
        if n:
            pr, pi = _cmul(pr, pi, pr, pi)
    return rr, ri


def _shift_rows(v, down):
    row = lax.broadcasted_iota(jnp.int32, v.shape, 0)
    if down:
        return jnp.where(row == 0, 0.0, pltpu.roll(v, 1, 0))
    return jnp.where(row == S5_SEGS - 1, 0.0, pltpu.roll(v, S5_SEGS - 1, 0))


def _chain_segments(er, ei, pr, pi, down):
    fr, fi = er, ei
    for _ in range(S5_SEGS - 1):
        sr, si = _shift_rows(fr, down), _shift_rows(fi, down)
        mr, mi = _cmul(pr, pi, sr, si)
        fr, fi = er + mr, ei + mi
    return _shift_rows(fr, down), _shift_rows(fi, down)


def _rows_to_scan_order(src_ref, dst_ref, t):
    ls = t // S5_SEGS

    def tile(j, c):
        dst_ref[pl.ds(pl.multiple_of(j * S5_SEGS, S5_SEGS), S5_SEGS), :] = src_ref[pl.ds(j, S5_SEGS, stride=ls), :]
        return c

    lax.fori_loop(0, ls, tile, 0, unroll=8)


def _rows_from_scan_order(src_ref, dst_ref, t):
    ls = t // S5_SEGS
    for s in range(S5_SEGS):
        def tile(jb, c, s=s):
            dst_ref[pl.ds(pl.multiple_of(s * ls + jb * 8, 8), 8), :] = (
                src_ref[pl.ds(jb * 8 * S5_SEGS + s, 8, stride=S5_SEGS), :])
            return c

        lax.fori_loop(0, ls // 8, tile, 0, unroll=8)


def _s5_fwd(ug, bd, ctd, ar4, ai4, dskip, carry=None):
    t = ug.shape[0]
    ls = t // S5_SEGS
    rc = min(512, t)
    ns = S5_BSTATE

    def body(ugn_ref, bd_ref, ct_ref, ar_ref, ai_ref, d_ref, xs_hbm, yn_ref, buf, ug_ref, y_ref, sem):
        cb = pl.program_id(0)
        bdv = bd_ref[0]
        _rows_to_scan_order(ugn_ref, ug_ref, t)

        def mm(i, c):
            rows = pl.ds(pl.multiple_of(i * rc, rc), rc)
            buf[rows, :] = _nn(ug_ref[rows, :].astype(BF16), bdv)
            return c

        lax.fori_loop(0, t // rc, mm, 0, unroll=True)
        arb = jnp.broadcast_to(ar_ref[0], (S5_SEGS, ns))
        aib = jnp.broadcast_to(ai_ref[0], (S5_SEGS, ns))

        def step(j, c, store):
            sr, si = c
            rows = pl.ds(pl.multiple_of(j * S5_SEGS, S5_SEGS), S5_SEGS)
            nr = arb * sr - aib * si + buf[rows, 0:ns]
            ni = arb * si + aib * sr + buf[rows, ns:2 * ns]
            if store:
                buf[rows, 0:ns] = nr
                buf[rows, ns:2 * ns] = ni
            return nr, ni

        zero = jnp.zeros((S5_SEGS, ns), F32)
        er, ei = lax.fori_loop(0, ls, functools.partial(step, store=False), (zero, zero))
        pr, pi = _cpow(arb, aib, ls)
        init = _chain_segments(er, ei, pr, pi, down=True)
        lax.fori_loop(0, ls, functools.partial(step, store=True), init)

        out = pltpu.make_async_copy(buf, xs_hbm.at[cb], sem)
        out.start()
        ctv = ct_ref[0]
        dv = d_ref[...]

        def ymm(i, c):
            rows = pl.ds(pl.multiple_of(i * rc, rc), rc)
            y_ref[rows, :] = _nn(buf[rows, :].astype(BF16), ctv) + dv * ug_ref[rows, :]
            return c

        lax.fori_loop(0, t // rc, ymm, 0, unroll=True)
        _rows_from_scan_order(y_ref, yn_ref, t)
        out.wait()

    return _call(
        body, name="s5_fwd", grid=(S5_BLOCKS,),
        in_specs=[pl.BlockSpec((t, 128), lambda i: (0, i)),
                  pl.BlockSpec((1, 128, 2 * ns), lambda i: (i, 0, 0)),
                  pl.BlockSpec((1, 2 * ns, 128), lambda i: (i, 0, 0)),
                  pl.BlockSpec((1, 1, ns), lambda i: (i, 0, 0)),
                  pl.BlockSpec((1, 1, ns), lambda i: (i, 0, 0)),
                  pl.BlockSpec((1, 128), lambda i: (0, i))],
        out_specs=[ANY, pl.BlockSpec((t, 128), lambda i: (0, i))],
        out_shape=[jax.ShapeDtypeStruct((S5_BLOCKS, t, 2 * ns), F32), jax.ShapeDtypeStruct((t, S5_WIDTH), F32)],
        scratch_shapes=[pltpu.VMEM((t, 2 * ns), F32), pltpu.VMEM((t, 128), F32), pltpu.VMEM((t, 128), F32),
                        pltpu.SemaphoreType.DMA(())],
        args=(ug, bd, ctd, ar4, ai4, dskip), carry=carry)


def _s5_bwd(dy, ug, xs, cd, bdt, ar4, ai4, dskip, carry=None):
    t = ug.shape[0]
    ls = t // S5_SEGS
    rc = min(512, t)
    ns = S5_BSTATE

    def body(dyn_ref, ugn_ref, xs_hbm, cd_ref, bdt_ref, ar_ref, ai_ref, d_ref,
             dugn_ref, dbd_ref, dcd_ref, dd_ref, dar_ref, dai_ref, xbuf, lam, dy_ref, ug_ref, dug_ref, sem):
        cb = pl.program_id(0)
        load = pltpu.make_async_copy(xs_hbm.at[cb], xbuf, sem)
        load.start()
        cdv = cd_ref[0]
        _rows_to_scan_order(dyn_ref, dy_ref, t)
        _rows_to_scan_order(ugn_ref, ug_ref, t)

        def mm(i, c):
            rows = pl.ds(pl.multiple_of(i * rc, rc), rc)
            lam[rows, :] = _nn(dy_ref[rows, :].astype(BF16), cdv)
            return c

        lax.fori_loop(0, t // rc, mm, 0, unroll=True)
        arb = jnp.broadcast_to(ar_ref[0], (S5_SEGS, ns))
        aib = jnp.broadcast_to(ai_ref[0], (S5_SEGS, ns))

        def lam_step(j, lr, li):
            rows = pl.ds(pl.multiple_of(j * S5_SEGS, S5_SEGS), S5_SEGS)
            nr = arb * lr + aib * li + lam[rows, 0:ns]
            ni = arb * li - aib * lr + lam[rows, ns:2 * ns]
            return rows, nr, ni

        def pass1(jj, c):
            _, nr, ni = lam_step(ls - 1 - jj, *c)
            return nr, ni

        zero = jnp.zeros((S5_SEGS, ns), F32)
        er, ei = lax.fori_loop(0, ls, pass1, (zero, zero))
        pr, pi = _cpow(arb, aib, ls)
        init = _chain_segments(er, ei, pr, -pi, down=False)
        load.wait()

        def accumulate(acc, nr, ni, xpr, xpi):
            return acc[0] + nr * xpr + ni * xpi, acc[1] + ni * xpr - nr * xpi

        def pass2(jj, c):
            lr, li, accr, acci = c
            j = ls - 1 - jj
            rows, nr, ni = lam_step(j, lr, li)
            lam[rows, 0:ns] = nr
            lam[rows, ns:2 * ns] = ni
            prev = pl.ds(pl.multiple_of((j - 1) * S5_SEGS, S5_SEGS), S5_SEGS)
            accr, acci = accumulate((accr, acci), nr, ni, xbuf[prev, 0:ns], xbuf[prev, ns:2 * ns])
            return nr, ni, accr, acci

        lr, li, accr, acci = lax.fori_loop(0, ls - 1, pass2, (init[0], init[1], zero, zero))
        rows, nr, ni = lam_step(0, lr, li)
        lam[rows, 0:ns] = nr
        lam[rows, ns:2 * ns] = ni
        last = pl.ds((ls - 1) * S5_SEGS, S5_SEGS)
        accr, acci = accumulate((accr, acci), nr, ni,
                                _shift_rows(xbuf[last, 0:ns], True), _shift_rows(xbuf[last, ns:2 * ns], True))
        dar_ref[0] = jnp.sum(accr, axis=0, keepdims=True)
        dai_ref[0] = jnp.sum(acci, axis=0, keepdims=True)

        bdtv = bdt_ref[0]
        dv = d_ref[...]
        dbd_ref[...] = jnp.zeros_like(dbd_ref)
        dcd_ref[...] = jnp.zeros_like(dcd_ref)
        dd_ref[...] = jnp.zeros_like(dd_ref)

        def tail(i, c):
            rows = pl.ds(pl.multiple_of(i * rc, rc), rc)
            dy = dy_ref[rows, :]
            ug = ug_ref[rows, :]
            lb = lam[rows, :].astype(BF16)
            dug_ref[rows, :] = _nn(lb, bdtv) + dv * dy
            dbd_ref[0] += _tn(ug.astype(BF16), lb)
            dcd_ref[0] += _tn(dy.astype(BF16), xbuf[rows, :].astype(BF16))
            dd_ref[...] += jnp.sum(dy * ug, axis=0, keepdims=True)
            return c

        lax.fori_loop(0, t // rc, tail, 0, unroll=True)
        _rows_from_scan_order(dug_ref, dugn_ref, t)

    chan = pl.BlockSpec((t, 128), lambda i: (0, i))
    dense = pl.BlockSpec((1, 128, 2 * ns), lambda i: (i, 0, 0))
    vec = pl.BlockSpec((1, 1, ns), lambda i: (i, 0, 0))
    return _call(
        body, name="s5_bwd", grid=(S5_BLOCKS,),
        in_specs=[chan, chan, ANY, dense, pl.BlockSpec((1, 2 * ns, 128), lambda i: (i, 0, 0)), vec, vec,
                  pl.BlockSpec((1, 128), lambda i: (0, i))],
        out_specs=[chan, dense, dense, pl.BlockSpec((1, 128), lambda i: (0, i)), vec, vec],
        out_shape=[jax.ShapeDtypeStruct((t, S5_WIDTH), F32),
                   jax.ShapeDtypeStruct((S5_BLOCKS, 128, 2 * ns), F32),
                   jax.ShapeDtypeStruct((S5_BLOCKS, 128, 2 * ns), F32),
                   jax.ShapeDtypeStruct((1, S5_WIDTH), F32),
                   jax.ShapeDtypeStruct((S5_BLOCKS, 1, ns), F32),
                   jax.ShapeDtypeStruct((S5_BLOCKS, 1, ns), F32)],
        scratch_shapes=[pltpu.VMEM((t, 2 * ns), F32), pltpu.VMEM((t, 2 * ns), F32)]
        + [pltpu.VMEM((t, 128), F32)] * 3 + [pltpu.SemaphoreType.DMA(())],
        args=(dy, ug, xs, cd, bdt, ar4, ai4, dskip), carry=carry)


def _cumsum_rows(x, reverse):
    c = x.shape[0]
    row = lax.broadcasted_iota(jnp.int32, x.shape, 0)
    d = 1
    while d < c:
        if reverse:
            x = x + jnp.where(row < c - d, pltpu.roll(x, c - d, 0), 0.0)
        else:
            x = x + jnp.where(row >= d, pltpu.roll(x, d, 0), 0.0)
        d *= 2
    return x


def _gla_common(q, k, alow, wup, bup):
    c = GLA_CHUNK
    pre = _nn(alow.astype(BF16), wup.astype(BF16)) + bup
    la = (jnp.minimum(pre, 0.0) - jnp.log(1.0 + jnp.exp(-jnp.abs(pre)))) * (1.0 / GLA_TAU)
    rr = lax.broadcasted_iota(jnp.int32, (c, c), 0)
    cc = lax.broadcasted_iota(jnp.int32, (c, c), 1)
    tril = (rr >= cc).astype(F32)
    bc = _cumsum_rows(la, reverse=False)
    bl = bc[c - 1:c, :]
    e_pos = jnp.exp(bc)
    e_neg = jnp.exp(-bc)
    e_end = jnp.exp(bl - bc)
    qt = q * (GLA_DK ** -0.5) * e_pos
    kt = k * e_neg
    ke = k * e_end
    lane = lax.broadcasted_iota(jnp.int32, (1, GLA_KEY), 1)
    masks = [((lane >= h * GLA_DK) & (lane < (h + 1) * GLA_DK)).astype(F32) for h in range(GLA_HEADS)]
    return dict(pre=pre, tril=tril, bc=bc, bl=bl, e_pos=e_pos, e_neg=e_neg, e_end=e_end,
                qt=qt, kt=kt, ke=ke, dec=jnp.exp(bl), masks=masks)


def _gla_fwd(q, k, v, alow, wup, bup, carry=None):
    t = q.shape[0]
    c = GLA_CHUNK
    n = t // c
    step = GLA_STEP_CHUNKS * c

    def body(q_ref, k_ref, v_ref, al_ref, wup_ref, bup_ref, o_ref, ss_ref, s_ref):
        i = pl.program_id(0)

        @pl.when(i == 0)
        def _():
            s_ref[...] = jnp.zeros_like(s_ref)

        wup_v, bup_v = wup_ref[...], bup_ref[...]
        s = s_ref[...]
        for j in range(GLA_STEP_CHUNKS):
            tok = slice(j * c, (j + 1) * c)
            m = _gla_common(q_ref[tok, :], k_ref[tok, :], al_ref[tok, :], wup_v, bup_v)
            ss_ref[j] = s
            sb = s.astype(BF16)
            ktb = m["kt"].astype(BF16)
            update = jnp.zeros_like(s)
            for h in range(GLA_HEADS):
                mask = m["masks"][h]
                qm = (m["qt"] * mask).astype(BF16)
                vh = v_ref[tok, h * GLA_DV:(h + 1) * GLA_DV].astype(BF16)
                p = (m["tril"] * _nt(qm, ktb)).astype(BF16)
                o_ref[tok, h * GLA_DV:(h + 1) * GLA_DV] = _nn(p, vh) + _nt(qm, sb)
                update = update + _tn(vh, (m["ke"] * mask).astype(BF16))
            s = m["dec"] * s + update
        s_ref[...] = s

    return _call(
        body, name="gla_fwd", grid=(t // step,),
        in_specs=[_row_tile(step, GLA_KEY), _row_tile(step, GLA_KEY), _row_tile(step, GLA_VAL),
                  _row_tile(step, GLA_RANK), VMEM_FULL, VMEM_FULL],
        out_specs=[_row_tile(step, GLA_VAL), pl.BlockSpec((GLA_STEP_CHUNKS, GLA_DV, GLA_KEY), lambda i: (i, 0, 0))],
        out_shape=[jax.ShapeDtypeStruct((t, GLA_VAL), F32), jax.ShapeDtypeStruct((n, GLA_DV, GLA_KEY), F32)],
        scratch_shapes=[pltpu.VMEM((GLA_DV, GLA_KEY), F32)],
        args=(q, k, v, alow, wup, bup), carry=carry)


def _gla_bwd(q, k, v, alow, wup, bup, ssave, do, carry=None):
    t = q.shape[0]
    c = GLA_CHUNK
    n = t // c

    def body(q_ref, k_ref, v_ref, al_ref, wup_ref, bup_ref, ss_ref, do_ref,
             dq_ref, dk_ref, dv_ref, dal_ref, dwup_ref, dbup_ref, ds_ref):
        i = pl.program_id(0)

        @pl.when(i == 0)
        def _():
            ds_ref[...] = jnp.zeros_like(ds_ref)
            dwup_ref[...] = jnp.zeros_like(dwup_ref)
            dbup_ref[...] = jnp.zeros_like(dbup_ref)

        wup_v, bup_v = wup_ref[...], bup_ref[...]
        ds_in = ds_ref[...]
        dwup = jnp.zeros((GLA_RANK, GLA_KEY), F32)
        dbup = jnp.zeros((1, GLA_KEY), F32)
        for j in reversed(range(GLA_STEP_CHUNKS)):
            tok = slice(j * c, (j + 1) * c)
            alow_v = al_ref[tok, :]
            m = _gla_common(q_ref[tok, :], k_ref[tok, :], alow_v, wup_v, bup_v)
            s = ss_ref[j]
            sb = s.astype(BF16)
            dsb = ds_in.astype(BF16)
            qt, kt, ke = m["qt"], m["kt"], m["ke"]
            ktb = kt.astype(BF16)
            dqt = jnp.zeros((c, GLA_KEY), F32)
            dkt = jnp.zeros((c, GLA_KEY), F32)
            dke = jnp.zeros((c, GLA_KEY), F32)
            update = jnp.zeros_like(ds_in)
            for h in range(GLA_HEADS):
                mask = m["masks"][h]
                qm = (qt * mask).astype(BF16)
                km = (kt * mask).astype(BF16)
                kem = (ke * mask).astype(BF16)
                cols = slice(h * GLA_DV, (h + 1) * GLA_DV)
                vh = v_ref[tok, cols].astype(BF16)
                doh = do_ref[tok, cols].astype(BF16)
                p = (m["tril"] * _nt(qm, ktb)).astype(BF16)
                dp = (m["tril"] * _nt(doh, vh)).astype(BF16)
                dv_ref[tok, cols] = _tn(p, doh) + _nt(kem, dsb)
                dqt = dqt + _nn(dp, km) + _nn(doh, sb) * mask
                dkt = dkt + _tn(dp, qm)
                dke = dke + _nn(vh, dsb) * mask
                update = update + _tn(doh, qm)
            ddec = jnp.sum(ds_in * s, axis=0, keepdims=True)
            dq_ref[tok, :] = dqt * m["e_pos"] * (GLA_DK ** -0.5)
            dk_ref[tok, :] = dkt * m["e_neg"] + dke * m["e_end"]
            dkeke = dke * ke
            dbl = jnp.sum(dkeke, axis=0, keepdims=True) + ddec * m["dec"]
            last = (lax.broadcasted_iota(jnp.int32, (c, 1), 0) == c - 1).astype(F32)
            dla = _cumsum_rows(dqt * qt - dkt * kt - dkeke + last * dbl, reverse=True)
            dpre = dla * (1.0 / GLA_TAU) * jax.nn.sigmoid(-m["pre"])
            dpb = dpre.astype(BF16)
            dal_ref[tok, :] = _nt(dpb, wup_v.astype(BF16))
            dwup = dwup + _tn(alow_v.astype(BF16), dpb)
            dbup = dbup + jnp.sum(dpre, axis=0, keepdims=True)
            ds_in = m["dec"] * ds_in + update
        ds_ref[...] = ds_in
        dwup_ref[...] += dwup
        dbup_ref[...] += dbup

    step = GLA_STEP_CHUNKS * c
    nsteps = t // step

    def rev(d):
        return pl.BlockSpec((step, d), lambda i: (nsteps - 1 - i, 0))

    return _call(
        body, name="gla_bwd", grid=(nsteps,),
        in_specs=[rev(GLA_KEY), rev(GLA_KEY), rev(GLA_VAL), rev(GLA_RANK), VMEM_FULL, VMEM_FULL,
                  pl.BlockSpec((GLA_STEP_CHUNKS, GLA_DV, GLA_KEY), lambda i: (nsteps - 1 - i, 0, 0)), rev(GLA_VAL)],
        out_specs=[rev(GLA_KEY), rev(GLA_KEY), rev(GLA_VAL), rev(GLA_RANK),
                   pl.BlockSpec((GLA_RANK, GLA_KEY), lambda i: (0, 0)), _acc_row(GLA_KEY)],
        out_shape=[jax.ShapeDtypeStruct((t, GLA_KEY), F32), jax.ShapeDtypeStruct((t, GLA_KEY), F32),
                   jax.ShapeDtypeStruct((t, GLA_VAL), F32), jax.ShapeDtypeStruct((t, GLA_RANK), F32),
                   jax.ShapeDtypeStruct((GLA_RANK, GLA_KEY), F32), jax.ShapeDtypeStruct((1, GLA_KEY), F32)],
        scratch_shapes=[pltpu.VMEM((GLA_DV, GLA_KEY), F32)],
        args=(q, k, v, alow, wup, bup, ssave, do), carry=carry)


def _post_math(y, o, r, gs5, ggla, wg, bg, gn, ps5t, pglat):
    y2 = y * y
    th = jnp.tanh(GELU_C0 * (y + GELU_C1 * y * y2))
    z5 = 0.5 * y * (1.0 + th)
    z5b = z5.astype(BF16)
    gate = jax.nn.sigmoid(_nn(z5b, wg) + bg)
    ys5 = z5 * gate
    rs, on = [], []
    for h in range(GLA_HEADS):
        oh = o[:, h * GLA_DV:(h + 1) * GLA_DV]
        rh = lax.rsqrt(jnp.mean(oh * oh, axis=-1, keepdims=True) + EPS)
        rs.append(rh)
        on.append(oh * rh)
    on = jnp.concatenate(on, axis=-1)
    sr = jax.nn.sigmoid(r)
    silu_r = r * sr
    ygla = on * gn * silu_r
    ys5b, yglab = ys5.astype(BF16), ygla.astype(BF16)
    m5 = _nt(ys5b, ps5t)
    mg = _nt(yglab, pglat)
    s5g, glag = jax.nn.sigmoid(gs5), jax.nn.sigmoid(ggla)
    merged = s5g * m5 + glag * mg
    return dict(y2=y2, th=th, z5=z5, z5b=z5b, gate=gate, ys5b=ys5b, yglab=yglab, rs=rs, on=on, sr=sr,
                silu_r=silu_r, m5=m5, mg=mg, s5g=s5g, glag=glag, mergedb=merged.astype(BF16))


def _mix_post_fwd(y, o, r, gs5, ggla, h1, wg, bg, gn, ps5t, pglat, wout, carry=None):
    t = o.shape[0]
    tm = _tile(t)

    def body(y_ref, o_ref, r_ref, gs5_ref, ggla_ref, h1_ref, wg_ref, bg_ref, gn_ref, ps_ref, pg_ref, wo_ref, h2_ref):
        m = _post_math(y_ref[...], o_ref[...], r_ref[...], gs5_ref[...], ggla_ref[...],
                       wg_ref[...], bg_ref[...], gn_ref[...], ps_ref[...], pg_ref[...])
        h2_ref[...] = h1_ref[...] + _nn(m["mergedb"], wo_ref[...])

    (h2,), landed = _call(
        body, name="mix_post_fwd", grid=(t // tm,),
        in_specs=[_row_tile(tm, 512)] * 3 + [_row_tile(tm, D_MODEL)] * 3
        + [VMEM_FULL, _acc_row(512), _acc_row(512), VMEM_FULL, VMEM_FULL, VMEM_FULL],
        out_specs=[_row_tile(tm, D_MODEL)],
        out_shape=[jax.ShapeDtypeStruct((t, D_MODEL), F32)],
        args=(y, o, r, gs5, ggla, h1, wg, bg, gn, ps5t, pglat, wout), carry=carry)
    return h2, landed


def _mix_post_bwd(y, o, r, gs5, ggla, dh2, wg, bg, gn, ps5t, pglat, wout, carry=None):
    t = o.shape[0]
    tm = _tile(t) // 2

    def body(y_ref, o_ref, r_ref, gs5_ref, ggla_ref, dh2_ref, wg_ref, bg_ref, gn_ref, ps_ref, pg_ref, wo_ref,
             dy_ref, do_ref, dr_ref, dgs5_ref, dggla_ref, dbg_ref, dgn_ref,
             z5b_ref, dgp_ref, ys5b_ref, dm5b_ref, yglab_ref, dmgb_ref, mergedb_ref, dh2b_ref):
        i = pl.program_id(0)
        yv, ov, rv = y_ref[...], o_ref[...], r_ref[...]
        wg, gn, ps5t, pglat = wg_ref[...], gn_ref[...], ps_ref[...], pg_ref[...]
        m = _post_math(yv, ov, rv, gs5_ref[...], ggla_ref[...], wg, bg_ref[...], gn, ps5t, pglat)
        dh2b = dh2_ref[...].astype(BF16)
        dmerged = _nt(dh2b, wo_ref[...])
        s5g, glag = m["s5g"], m["glag"]
        dgs5_ref[...] = dmerged * m["m5"] * s5g * (1.0 - s5g)
        dggla_ref[...] = dmerged * m["mg"] * glag * (1.0 - glag)
        dm5b = (dmerged * s5g).astype(BF16)
        dmgb = (dmerged * glag).astype(BF16)
        dys5 = _nn(dm5b, ps5t)
        dygla = _nn(dmgb, pglat)
        gate, z5, th = m["gate"], m["z5"], m["th"]
        dgpre = dys5 * z5 * gate * (1.0 - gate)
        dgpb = dgpre.astype(BF16)
        dz5 = dys5 * gate + _nt(dgpb, wg)
        dgelu = 0.5 * (1.0 + th) + 0.5 * yv * (1.0 - th * th) * GELU_C0 * (1.0 + 3.0 * GELU_C1 * m["y2"])
        dy_ref[...] = dz5 * dgelu
        on, sr, silu_r = m["on"], m["sr"], m["silu_r"]
        dr_ref[...] = dygla * on * gn * sr * (1.0 + rv * (1.0 - sr))
        dgn = jnp.sum(dygla * on * silu_r, axis=0, keepdims=True)
        don = dygla * gn * silu_r
        for h in range(GLA_HEADS):
            cols = slice(h * GLA_DV, (h + 1) * GLA_DV)
            donh, onh = don[:, cols], on[:, cols]
            do_ref[:, cols] = m["rs"][h] * (donh - onh * jnp.mean(donh * onh, axis=-1, keepdims=True))

        @pl.when(i == 0)
        def _():
            dbg_ref[...] = jnp.zeros_like(dbg_ref)
            dgn_ref[...] = jnp.zeros_like(dgn_ref)

        dbg_ref[...] += jnp.sum(dgpre, axis=0, keepdims=True)
        dgn_ref[...] += dgn
        z5b_ref[...] = m["z5b"]
        dgp_ref[...] = dgpb
        ys5b_ref[...] = m["ys5b"]
        dm5b_ref[...] = dm5b
        yglab_ref[...] = m["yglab"]
        dmgb_ref[...] = dmgb
        mergedb_ref[...] = m["mergedb"]
        dh2b_ref[...] = dh2b

    def f32(d):
        return jax.ShapeDtypeStruct((t, d), F32)

    def b16(d):
        return jax.ShapeDtypeStruct((t, d), BF16)

    widths = (512, 512, 512, 1024, 512, 1024, 1024, 1024)
    return _call(
        body, name="mix_post_bwd", grid=(t // tm,),
        in_specs=[_row_tile(tm, 512)] * 3 + [_row_tile(tm, D_MODEL)] * 3
        + [VMEM_FULL, _acc_row(512), _acc_row(512), VMEM_FULL, VMEM_FULL, VMEM_FULL],
        out_specs=[_row_tile(tm, 512)] * 3 + [_row_tile(tm, D_MODEL)] * 2
        + [_acc_row(512)] * 2 + [_row_tile(tm, w) for w in widths],
        out_shape=[f32(512)] * 3 + [f32(D_MODEL)] * 2
        + [jax.ShapeDtypeStruct((1, 512), F32)] * 2
        + [b16(w) for w in widths],
        args=(y, o, r, gs5, ggla, dh2, wg, bg, gn, ps5t, pglat, wout), carry=carry)


def _head(h3, g, target):
    t = h3.shape[0]
    tm = _tile(t)

    def body(h_ref, g_ref, t_ref, loss_ref, dh_ref, dg_ref):
        i = pl.program_id(0)
        gv = g_ref[...]
        xhat, r = _rms_parts(h_ref[...])
        err = xhat * gv - t_ref[...]
        dx, dg = _rms_bwd(err * (1.0 / D_MODEL), gv, xhat, r)
        dh_ref[...] = dx

        @pl.when(i == 0)
        def _():
            loss_ref[...] = jnp.zeros_like(loss_ref)
            dg_ref[...] = jnp.zeros_like(dg_ref)

        loss_ref[...] += (0.5 / D_MODEL) * jnp.sum(jnp.sum(err * err, axis=1, keepdims=True), axis=0, keepdims=True)
        dg_ref[...] += dg

    return pl.pallas_call(
        body, name="head", grid=(t // tm,),
        in_specs=[_row_tile(tm, D_MODEL), _acc_row(D_MODEL), _row_tile(tm, D_MODEL)],
        out_specs=[pl.BlockSpec((1, 1), lambda i: (0, 0)), _row_tile(tm, D_MODEL), _acc_row(D_MODEL)],
        out_shape=[jax.ShapeDtypeStruct((1, 1), F32), jax.ShapeDtypeStruct((t, D_MODEL), F32),
                   jax.ShapeDtypeStruct((1, D_MODEL), F32)],
        compiler_params=_cparams(1),
    )(h3, g, target)


ADAM_TILE_ELEMS = 256 * 1024


def _adamw(w, g, m, v, name):
    rows, cols = w.shape
    tr = rows
    while tr * cols > ADAM_TILE_ELEMS and tr % 16 == 0:
        tr //= 2

    spec = pl.BlockSpec((tr, cols), lambda i: (i, 0))
    sh = jax.ShapeDtypeStruct((rows, cols), F32)
    return pl.pallas_call(functools.partial(_adamw_body), name=name, grid=(rows // tr,), in_specs=[spec] * 4,
                          out_specs=[spec] * 3, out_shape=[sh] * 3, compiler_params=_cparams(1))(w, g, m, v)


def _adamw_body(w_ref, g_ref, m_ref, v_ref, d_ref, nm_ref, nv_ref):
    gv = g_ref[...]
    nm = ADAM_B1 * m_ref[...] + (1.0 - ADAM_B1) * gv
    nv = ADAM_B2 * v_ref[...] + (1.0 - ADAM_B2) * (gv * gv)
    m_hat = nm / (1.0 - ADAM_B1 ** ADAM_STEP)
    v_hat = nv / (1.0 - ADAM_B2 ** ADAM_STEP)
    d_ref[...] = -ADAM_LR * (m_hat / (jnp.sqrt(v_hat) + ADAM_EPS) + ADAM_WD * w_ref[...])
    nm_ref[...] = nm
    nv_ref[...] = nv


def _adamw_many(ws, gs, ms, vs, name):
    n = len(ws)

    def body(*refs):
        ins, outs = refs[:4 * n], refs[4 * n:]
        for i in range(n):
            _adamw_body(*(ins[j * n + i] for j in range(4)), *(outs[j * n + i] for j in range(3)))

    shapes = [jax.ShapeDtypeStruct(w.shape, F32) for w in ws]
    res = pl.pallas_call(body, name=name, in_specs=[VMEM_FULL] * (4 * n), out_specs=[VMEM_FULL] * (3 * n),
                         out_shape=shapes * 3)(*ws, *gs, *ms, *vs)
    return res[:n], res[n:2 * n], res[2 * n:]


def _exchange(srcs, scatter, name):
    n = len(srcs)

    def body(*refs):
        _exchange_start(refs[:n], refs[n:2 * n], *refs[2 * n:], scatter=scatter)
        _exchange_wait(refs[:n], refs[n:2 * n], *refs[2 * n:], scatter=scatter)

    return pl.pallas_call(
        body, name=name, in_specs=[ANY] * n, out_specs=[ANY] * n,
        out_shape=_exchange_shapes(srcs, scatter), scratch_shapes=_exchange_sems(n),
    )(*srcs)


def _sum_slabs(slabs, name):
    n = slabs.shape[0]

    def body(s_ref, o_ref):
        acc = s_ref[0].astype(F32)
        for s in range(1, n):
            acc = acc + s_ref[s].astype(F32)
        o_ref[...] = acc

    return pl.pallas_call(
        body, name=name, in_specs=[VMEM_FULL], out_specs=VMEM_FULL,
        out_shape=jax.ShapeDtypeStruct(slabs.shape[1:], F32),
        compiler_params=pltpu.CompilerParams(vmem_limit_bytes=VMEM_LIMIT_BYTES),
    )(slabs)


BIG = ("ffn1_w1", "ffn1_w3", "ffn1_w2", "w_in", "s5_glu_w", "gla_a_up_w", "proj_s5", "proj_gla", "w_out",
       "ffn2_w1", "ffn2_w3", "ffn2_w2")
GROUPS = (("ffn1_w1", "ffn1_w3", "ffn1_w2"),
          ("w_in", "s5_glu_w", "gla_a_up_w", "proj_s5", "proj_gla", "w_out"),
          ("ffn2_w1", "ffn2_w3", "ffn2_w2"))
W_IN_ROWS = 514
W_IN_PAD = 528
UP_COLS = 32
COL_SHARDED = ("ffn1_w1", "ffn1_w3", "w_in", "proj_s5", "proj_gla", "ffn2_w1", "ffn2_w3")

SMALL = ("ffn1_norm", "mix_norm", "s5_lambda_re", "s5_lambda_im", "s5_log_dt", "s5_b_re", "s5_b_im", "s5_c_re",
         "s5_c_im", "s5_d", "s5_glu_b", "gla_a_up_b", "gla_out_norm", "ffn2_norm", "final_norm")
SMALL_SHAPES = dict(ffn1_norm=(1, 1024), mix_norm=(1, 1024), s5_lambda_re=(1, 32, 64), s5_lambda_im=(1, 32, 64),
                    s5_log_dt=(1, 32), s5_b_re=(1, 32, 64, 16), s5_b_im=(1, 32, 64, 16), s5_c_re=(1, 32, 16, 64),
                    s5_c_im=(1, 32, 16, 64), s5_d=(1, 32, 16), s5_glu_b=(1, 512), gla_a_up_b=(1, 256),
                    gla_out_norm=(1, 512), ffn2_norm=(1, 1024), final_norm=(1024,))
SMALL_N = sum(math.prod(s) for s in SMALL_SHAPES.values())
SMALL_R = -(-SMALL_N // (64 * 1024)) * 64


def _shard_rows(name, a):
    if name == "gla_a_up_w":
        return jnp.pad(a, ((0, 0), (0, 128 - UP_COLS)))
    if name in COL_SHARDED:
        a = a.T
    if name == "w_in":
        return jnp.pad(a, ((0, W_IN_PAD - W_IN_ROWS), (0, 0)))
    return a.reshape(-1, 1024)


def _unshard_rows(name, rows, shape):
    if name == "gla_a_up_w":
        return rows[:, :UP_COLS]
    if name == "w_in":
        rows = rows[:W_IN_ROWS]
    if name in COL_SHARDED:
        return rows.reshape(shape[1], shape[0]).T
    return rows.reshape(shape)


def _pack_small(vals):
    flat = jnp.concatenate([vals[n].reshape(-1).astype(F32) for n in SMALL])
    return jnp.pad(flat, (0, SMALL_R * 1024 - SMALL_N)).reshape(SMALL_R, 1024)


def _unpack_small(slab):
    flat = slab.reshape(-1)
    out, off = {}, 0
    for n in SMALL:
        size = math.prod(SMALL_SHAPES[n])
        out[n] = flat[off:off + size].reshape(SMALL_SHAPES[n])
        off += size
    return out


FULL_SHAPES = dict(w_in=(IN_COLS, D_MODEL), s5_glu_w=(S5_WIDTH, S5_WIDTH), gla_a_up_w=(GLA_RANK, GLA_KEY),
                   proj_s5=(D_MODEL, S5_WIDTH), proj_gla=(D_MODEL, GLA_VAL), w_out=(D_MODEL, D_MODEL))


def _full_weight(name, gathered):
    if name == "gla_a_up_w":
        return gathered[:, :, :UP_COLS].transpose(1, 0, 2).reshape(GLA_RANK, GLA_KEY)
    if name == "w_in":
        gathered = gathered[:, :W_IN_ROWS]
    return gathered.reshape(FULL_SHAPES.get(name, (D_FF, D_MODEL)))


def _grad_slabs(name, g):
    if name == "gla_a_up_w":
        g = g.reshape(GLA_RANK, N_DEV, UP_COLS).transpose(1, 0, 2)
        return jnp.pad(g, ((0, 0), (0, 0), (0, 128 - UP_COLS))).astype(BF16)
    if name == "w_in":
        return jnp.pad(g.reshape(N_DEV, W_IN_ROWS, D_MODEL), ((0, 0), (0, W_IN_PAD - W_IN_ROWS), (0, 0)))
    return g.reshape(N_DEV, -1, 1024)


def _s5_dense(re, im, sign_im):
    eye = jnp.eye(8, dtype=F32)

    def one(a):
        a = a.reshape(S5_BLOCKS, 8, S5_GROUP, S5_STATE)
        return jnp.einsum("cghp,gk->cghkp", a, eye).reshape(S5_BLOCKS, 128, S5_BSTATE)

    return jnp.concatenate([one(re), sign_im * one(im)], axis=-1)


def _s5_undense(d):
    eye = jnp.eye(8, dtype=F32)

    def one(a):
        a = a.reshape(S5_BLOCKS, 8, S5_GROUP, 8, S5_STATE)
        return jnp.einsum("cghkp,gk->cghp", a, eye).reshape(S5_GROUPS, S5_GROUP, S5_STATE)

    return one(d[..., :S5_BSTATE]), one(d[..., S5_BSTATE:])


def _local_step(x, target, p, w, rows=None):
    w = dict(w or {})
    landed_grads = {}

    def gather(names):
        return None if rows is None else ([rows[n] for n in names], False)

    def gathered(names, landed):
        w.update({n: _full_weight(n, g) for n, g in zip(names, landed)})

    def scatter(names):
        return None if rows is None else ([_grad_slabs(n, big[n]) for n in names], True)

    def scattered(names, landed):
        landed_grads.update(zip(names, landed))

    if rows is not None:
        gathered(GROUPS[0], _exchange(gather(GROUPS[0])[0], False, "gather_ffn1"))
    g1, gm, g2 = p["ffn1_norm"], p["mix_norm"], p["ffn2_norm"]
    gf = p["final_norm"].reshape(1, D_MODEL)
    lre, lim = p["s5_lambda_re"][0], p["s5_lambda_im"][0]
    ldt = p["s5_log_dt"][0].reshape(S5_GROUPS, 1)
    bre = p["s5_b_re"][0].transpose(2, 0, 1)
    bim = p["s5_b_im"][0].transpose(2, 0, 1)
    cre, cim = p["s5_c_re"][0], p["s5_c_im"][0]
    dskip = p["s5_d"][0].reshape(1, S5_WIDTH)
    bg, bup, gn = p["s5_glu_b"], p["gla_a_up_b"], p["gla_out_norm"]

    mix_first, mix_rest = ("w_in", "gla_a_up_w"), ("s5_glu_w", "proj_s5", "proj_gla", "w_out")
    h1, got = _ffn_fwd(x, g1, w["ffn1_w1"], w["ffn1_w3"], w["ffn1_w2"], "ffn1_fwd", gather(mix_first))
    gathered(mix_first, got)
    wup = w["gla_a_up_w"].astype(F32)
    (u, s5in, q, k, v, r, alow, gs5, ggla), got = _mix_pre_fwd(h1, gm, w["w_in"], gather(mix_rest))
    gathered(mix_rest, got)
    ar, ai, bbr, bbi = _s5_disc(lre, lim, ldt, bre, bim)
    bd = _s5_dense(bbr.transpose(1, 0, 2), bbi.transpose(1, 0, 2), 1.0)
    cd = _s5_dense(cre, cim, -1.0)
    bd16, cd16 = bd.astype(BF16), cd.astype(BF16)
    bdt16, ctd16 = bd16.transpose(0, 2, 1), cd16.transpose(0, 2, 1)
    ar4 = ar.reshape(S5_BLOCKS, 1, S5_BSTATE)
    ai4 = ai.reshape(S5_BLOCKS, 1, S5_BSTATE)
    (xs, y), got = _s5_fwd(s5in, bd16, ctd16, ar4, ai4, dskip, gather(GROUPS[2][:1]))
    gathered(GROUPS[2][:1], got)
    (o, ssave), got = _gla_fwd(q, k, v, alow, wup, bup, gather(GROUPS[2][1:2]))
    gathered(GROUPS[2][1:2], got)
    post_w = (w["s5_glu_w"], bg, gn, w["proj_s5"], w["proj_gla"], w["w_out"])
    h2, got = _mix_post_fwd(y, o, r, gs5, ggla, h1, *post_w, carry=gather(GROUPS[2][2:]))
    gathered(GROUPS[2][2:], got)
    h3, _ = _ffn_fwd(h2, g2, w["ffn2_w1"], w["ffn2_w3"], w["ffn2_w2"], "ffn2_fwd")
    loss, dh3, dgf = _head(h3, gf, target)

    big, small = {}, {}
    small["final_norm"] = dgf.reshape(D_MODEL)
    (dh2, dg2, da3, db3, s3, n2, dhh2), _ = _ffn_bwd(
        h2, dh3, g2, w["ffn2_w1"], w["ffn2_w3"], w["ffn2_w2"], "ffn2_bwd")
    small["ffn2_norm"] = dg2
    big["ffn2_w1"] = _mm_tn(da3, n2, "ffn2_dw1")
    big["ffn2_w3"] = _mm_tn(db3, n2, "ffn2_dw3")
    big["ffn2_w2"] = _mm_tn(s3, dhh2, "ffn2_dw2")
    (dy, do, dr, dgs5, dggla, dbg, dgn, z5b, dgpb, ys5b, dm5b, yglab, dmgb, mergedb, dh2b), got = _mix_post_bwd(
        y, o, r, gs5, ggla, dh2, *post_w, carry=scatter(GROUPS[2][:1]))
    scattered(GROUPS[2][:1], got)
    small["s5_glu_b"] = dbg
    small["gla_out_norm"] = dgn
    big["s5_glu_w"] = _mm_tn(z5b, dgpb, "glu_dw")
    big["proj_s5"] = _mm_tn(dm5b, ys5b, "proj_s5_dw")
    big["proj_gla"] = _mm_tn(dmgb, yglab, "proj_gla_dw")
    big["w_out"] = _mm_tn(mergedb, dh2b, "w_out_dw")
    (dq, dk, dv, dalow, dwup, dbup), got = _gla_bwd(q, k, v, alow, wup, bup, ssave, do, scatter(GROUPS[2][1:2]))
    scattered(GROUPS[2][1:2], got)
    big["gla_a_up_w"] = dwup
    small["gla_a_up_b"] = dbup
    (ds5in, dbd, dcd, dd, dar4, dai4), got = _s5_bwd(
        dy, s5in, xs, cd16, bdt16, ar4, ai4, dskip, scatter(GROUPS[2][2:]))
    scattered(GROUPS[2][2:], got)
    dbbr, dbbi = _s5_undense(dbd)
    dcre, dcim_neg = _s5_undense(dcd)
    glre, glim, gldt, gbre, gbim = _s5_disc_bwd(
        lre, lim, ldt, bre, bim, dar4.reshape(S5_GROUPS, S5_STATE), dai4.reshape(S5_GROUPS, S5_STATE),
        dbbr.transpose(1, 0, 2), dbbi.transpose(1, 0, 2))
    small["s5_lambda_re"] = glre[None]
    small["s5_lambda_im"] = glim[None]
    small["s5_log_dt"] = gldt.reshape(1, S5_GROUPS)
    small["s5_b_re"] = gbre.transpose(1, 2, 0)[None]
    small["s5_b_im"] = gbim.transpose(1, 2, 0)[None]
    small["s5_c_re"] = dcre[None]
    small["s5_c_im"] = -dcim_neg[None]
    small["s5_d"] = dd.reshape(1, S5_GROUPS, S5_GROUP)
    dz = (ds5in, dq, dk, dv, dr, dalow, dgs5, dggla)
    dh1, dgm = _mix_pre_bwd(h1, gm, w["w_in"], dh2, dz)
    small["mix_norm"] = dgm
    big["w_in"] = jnp.concatenate([_mm_tn(d, u, "w_in_dw%d" % i) for i, d in enumerate(dz)], axis=0)
    (dx, dg1, da3, db3, s3, n1, dhh1), got = _ffn_bwd(
        x, dh1, g1, w["ffn1_w1"], w["ffn1_w3"], w["ffn1_w2"], "ffn1_bwd", scatter(GROUPS[1]))
    scattered(GROUPS[1], got)
    small["ffn1_norm"] = dg1
    big["ffn1_w1"] = _mm_tn(da3, n1, "ffn1_dw1")
    if rows is None:
        big["ffn1_w3"] = _mm_tn(db3, n1, "ffn1_dw3")
        big["ffn1_w2"] = _mm_tn(s3, dhh1, "ffn1_dw2")
        return loss[0, 0], dx, big, small
    big["ffn1_w3"], got = _mm_tn(db3, n1, "ffn1_dw3", scatter(GROUPS[0][:1]))
    scattered(GROUPS[0][:1], got)
    big["ffn1_w2"], got = _mm_tn(s3, dhh1, "ffn1_dw2", scatter(GROUPS[0][1:2]))
    scattered(GROUPS[0][1:2], got)
    scattered(GROUPS[0][2:], _exchange(scatter(GROUPS[0][2:])[0], True, "scatter_ffn1_w2"))
    return loss[0, 0], dx, landed_grads, small


NAMES = ("ffn1_norm", "ffn1_w1", "ffn1_w3", "ffn1_w2", "mix_norm", "w_in", "s5_lambda_re", "s5_lambda_im",
         "s5_log_dt", "s5_b_re", "s5_b_im", "s5_c_re", "s5_c_im", "s5_d", "s5_glu_w", "s5_glu_b", "gla_a_up_w",
         "gla_a_up_b", "gla_out_norm", "proj_s5", "proj_gla", "w_out", "ffn2_norm", "ffn2_w1", "ffn2_w3", "ffn2_w2",
         "final_norm")


def kernel(*args):
    nw = len(NAMES)
    x = args[0][0]
    wts = dict(zip(NAMES, args[1:1 + nw]))
    target = args[1 + nw][0]
    mom = dict(zip(NAMES, args[2 + nw:2 + 2 * nw]))
    var = dict(zip(NAMES, args[2 + 2 * nw:2 + 3 * nw]))

    shards = {n: wts[n][0] for n in BIG}
    rows = {n: _shard_rows(n, shards[n]).astype(BF16) for n in BIG}
    loss, dx, landed, small = _local_step(x, target, {n: wts[n] for n in SMALL}, None, rows)
    loss = lax.psum(loss, ("x", "y", "c"))

    grad, delta, new_m, new_v = {}, {}, {}, {}
    for n in BIG:
        g = _unshard_rows(n, _sum_slabs(landed[n], "sum_" + n), shards[n].shape)
        grad[n] = g[None]
        delta[n], new_m[n], new_v[n] = (a[None] for a in _adamw(shards[n], g, mom[n][0], var[n][0], "adamw_" + n))

    part = _pack_small(small).reshape(N_DEV, SMALL_R // N_DEV, 1024)
    mine = _sum_slabs(_exchange([part], True, "scatter_small")[0], "sum_small")
    g_small = _exchange([mine], False, "gather_small")[0].reshape(SMALL_R, 1024)
    grad.update(_unpack_small(g_small))

    def flat2d(a):
        return a.reshape(-1, a.shape[-1])

    outs = _adamw_many(*([flat2d(d[n]) for n in SMALL] for d in (wts, grad, mom, var)), "adamw_small")
    for out, arrays in zip((delta, new_m, new_v), outs):
        out.update({n: a.reshape(SMALL_SHAPES[n]) for n, a in zip(SMALL, arrays)})
    return (loss, dx[None], *(d[n] for d in (grad, delta, new_m, new_v) for n in NAMES))
```

```python
import functools
import math

import jax
import jax.numpy as jnp
from jax import lax
from jax.experimental import pallas as pl
from jax.experimental.pallas import tpu as pltpu

F32, BF16 = jnp.float32, jnp.bfloat16
HIGHEST = lax.Precision.HIGHEST

D_MODEL = 1024
D_FF = 2816
N_DEV = 8
S5_WIDTH, S5_GROUPS, S5_GROUP, S5_STATE = 512, 32, 16, 64
S5_BLOCKS = 4
S5_BSTATE = 512
S5_SEGS = 8
GLA_HEADS, GLA_DK, GLA_DV = 4, 64, 128
GLA_KEY, GLA_VAL, GLA_RANK, GLA_CHUNK = 256, 512, 16, 64
GLA_TAU = 16.0
GLA_STEP_CHUNKS = 4
EPS = 1e-6
IN_SIZES = (512, 256, 256, 512, 512, 16, 1024, 1024)
IN_OFFS = tuple(sum(IN_SIZES[:i]) for i in range(len(IN_SIZES)))
IN_COLS = sum(IN_SIZES)
ADAM_LR, ADAM_B1, ADAM_B2, ADAM_EPS, ADAM_WD, ADAM_STEP = 0.001, 0.9, 0.999, 1e-08, 0.01, 10
GELU_C0 = math.sqrt(2.0 / math.pi)
GELU_C1 = 0.044715

FFN_FT = 256
VMEM_LIMIT_BYTES = 56 * 1024 * 1024

VMEM_FULL = pl.BlockSpec(memory_space=pltpu.VMEM)
ANY = pl.BlockSpec(memory_space=pl.ANY)


def _cparams(n_grid):
    return pltpu.CompilerParams(dimension_semantics=("arbitrary",) * n_grid, vmem_limit_bytes=VMEM_LIMIT_BYTES)


def _tile(t):
    return 512 if t >= 1024 else t // 2


def _nn(a, b):
    return jnp.dot(a, b, preferred_element_type=F32)


def _nt(a, b):
    return lax.dot_general(a, b, (((1,), (1,)), ((), ())), preferred_element_type=F32)


def _tn(a, b):
    return lax.dot_general(a, b, (((0,), (0,)), ((), ())), preferred_element_type=F32)


def _rms_parts(x):
    r = lax.rsqrt(jnp.mean(x * x, axis=-1, keepdims=True) + EPS)
    return x * r, r


def _rms_bwd(dn, g, xhat, r):
    dxh = dn * g
    dx = r * (dxh - xhat * jnp.mean(dxh * xhat, axis=-1, keepdims=True))
    return dx, jnp.sum(dn * xhat, axis=0, keepdims=True)


def _peers():
    x, y, c = lax.axis_index("x"), lax.axis_index("y"), lax.axis_index("c")
    out = []
    for k in range(1, N_DEV):
        px = 1 - x if k & 4 else x
        py = 1 - y if k & 2 else y
        pc = 1 - c if k & 1 else c
        out.append(((px, py, pc), 4 * px + 2 * py + pc))
    return 4 * x + 2 * y + c, out


def _exchange_copies(src_refs, out_refs, send_sems, recv_sems, local_sems, scatter, with_recvs):
    me, peers = _peers()
    locals_, sends, recvs = [], [], []
    for a, (src_ref, out_ref) in enumerate(zip(src_refs, out_refs)):
        def mine(idx, src_ref=src_ref):
            return src_ref.at[idx] if scatter else src_ref

        locals_.append(pltpu.make_async_copy(mine(me), out_ref.at[me], local_sems.at[a]))
        for k, (dev, idx) in enumerate(peers):
            sends.append(pltpu.make_async_remote_copy(
                src_ref=mine(idx), dst_ref=out_ref.at[me], send_sem=send_sems.at[a, k], recv_sem=recv_sems.at[a, k],
                device_id=dev, device_id_type=pl.DeviceIdType.MESH))
            if with_recvs:
                recvs.append(pltpu.make_async_remote_copy(
                    src_ref=mine(idx), dst_ref=out_ref.at[idx], send_sem=send_sems.at[a, k],
                    recv_sem=recv_sems.at[a, k], device_id=dev, device_id_type=pl.DeviceIdType.MESH))
    return locals_, sends, recvs


def _remote(src, dst, send_sems, recv_sems, a, k, dev):
    return pltpu.make_async_remote_copy(src_ref=src, dst_ref=dst, send_sem=send_sems.at[a, k],
                                        recv_sem=recv_sems.at[a, k], device_id=dev,
                                        device_id_type=pl.DeviceIdType.MESH)


def _gather_places():
    x, y, c = lax.axis_index("x"), lax.axis_index("y"), lax.axis_index("c")
    chips = [(1 - x, y), (x, 1 - y), (1 - x, 1 - y)]
    sibling = (x, y, 1 - c)
    me_idx, sib_idx = 4 * x + 2 * y + c, 4 * x + 2 * y + 1 - c
    same_core = [((cx, cy, c), 4 * cx + 2 * cy + c) for cx, cy in chips]
    other_core_idx = [4 * cx + 2 * cy + 1 - c for cx, cy in chips]
    return sibling, me_idx, sib_idx, same_core, other_core_idx


def _gather_start(src_refs, out_refs, send_sems, recv_sems, local_sems):
    sibling, me_idx, _, same_core, _ = _gather_places()
    for a, (src, out) in enumerate(zip(src_refs, out_refs)):
        pltpu.make_async_copy(src, out.at[me_idx], local_sems.at[a]).start()
        _remote(src, out.at[me_idx], send_sems, recv_sems, a, 0, sibling).start()
        for j, (dev, _) in enumerate(same_core):
            _remote(src, out.at[me_idx], send_sems, recv_sems, a, 1 + j, dev).start()


def _gather_finish(src_refs, out_refs, send_sems, recv_sems, local_sems):
    sibling, me_idx, sib_idx, same_core, other_core_idx = _gather_places()
    arrays = list(enumerate(zip(src_refs, out_refs)))
    forwards = []
    for a, (src, out) in arrays:
        for j, (dev, idx) in enumerate(same_core):
            _remote(src, out.at[idx], send_sems, recv_sems, a, 1 + j, dev).wait_recv()
            fwd = _remote(out.at[idx], out.at[idx], send_sems, recv_sems, a, 4 + j, sibling)
            fwd.start()
            forwards.append(fwd)
    for a, (src, out) in arrays:
        _remote(src, out.at[sib_idx], send_sems, recv_sems, a, 0, sibling).wait_recv()
        for j, idx in enumerate(other_core_idx):
            _remote(src, out.at[idx], send_sems, recv_sems, a, 4 + j, sibling).wait_recv()
        _remote(src, out.at[me_idx], send_sems, recv_sems, a, 0, sibling).wait_send()
        for j, (dev, _) in enumerate(same_core):
            _remote(src, out.at[me_idx], send_sems, recv_sems, a, 1 + j, dev).wait_send()
        pltpu.make_async_copy(src, out.at[me_idx], local_sems.at[a]).wait()
    for fwd in forwards:
        fwd.wait_send()


def _exchange_start(*refs, scatter):
    if not scatter:
        return _gather_start(*refs)
    locals_, sends, _ = _exchange_copies(*refs, scatter=scatter, with_recvs=False)
    for cp in locals_ + sends:
        cp.start()


def _exchange_wait(*refs, scatter):
    if not scatter:
        return _gather_finish(*refs)
    locals_, sends, recvs = _exchange_copies(*refs, scatter=scatter, with_recvs=True)
    for cp in recvs:
        cp.wait_recv()
    for cp in sends:
        cp.wait_send()
    for cp in locals_:
        cp.wait()


def _exchange_sems(n_arrays):
    return [pltpu.SemaphoreType.DMA((n_arrays, N_DEV - 1)), pltpu.SemaphoreType.DMA((n_arrays, N_DEV - 1)),
            pltpu.SemaphoreType.DMA((n_arrays,))]


def _exchange_shapes(srcs, scatter):
    return [jax.ShapeDtypeStruct((N_DEV,) + tuple(s.shape[1:] if scatter else s.shape), s.dtype) for s in srcs]


def _call(body, *, name, grid, in_specs, out_specs, out_shape, args, scratch_shapes=(), carry=None):
    n_in, n_out, n_scr = len(in_specs), len(out_specs), len(scratch_shapes)
    srcs, scatter = carry if carry is not None else ((), False)
    nc = len(srcs)

    def wrapped(*refs):
        ins, refs = refs[:n_in], refs[n_in:]
        csrc, refs = refs[:nc], refs[nc:]
        outs, refs = refs[:n_out], refs[n_out:]
        cland, refs = refs[:nc], refs[nc:]
        scr, sems = refs[:n_scr], refs[n_scr:]
        if nc:
            @pl.when(pl.program_id(0) == 0)
            def _():
                _exchange_start(csrc, cland, *sems, scatter=scatter)

        body(*ins, *outs, *scr)
        if nc:
            @pl.when(pl.program_id(0) == grid[0] - 1)
            def _():
                _exchange_wait(csrc, cland, *sems, scatter=scatter)

    res = pl.pallas_call(
        wrapped, name=name, grid=grid,
        in_specs=list(in_specs) + [ANY] * nc, out_specs=list(out_specs) + [ANY] * nc,
        out_shape=list(out_shape) + _exchange_shapes(srcs, scatter),
        scratch_shapes=list(scratch_shapes) + (_exchange_sems(nc) if nc else []),
        compiler_params=_cparams(1),
    )(*args, *srcs)
    return res[:n_out], res[n_out:]


def _row_tile(tm, d):
    return pl.BlockSpec((tm, d), lambda i: (i, 0))


def _acc_row(d):
    return pl.BlockSpec((1, d), lambda i: (0, 0))


def _ffn_fwd(x, g, w1t, w3t, w2, name, carry=None):
    t = x.shape[0]
    tm = _tile(t)
    nf = D_FF // FFN_FT

    def body(x_ref, g_ref, w1_ref, w3_ref, w2_ref, o_ref):
        xv = x_ref[...]
        xhat, _ = _rms_parts(xv)
        n = (xhat * g_ref[...]).astype(BF16)
        o_ref[...] = xv

        def fstep(f, c):
            rows = pl.ds(pl.multiple_of(f * FFN_FT, FFN_FT), FFN_FT)
            a = _nt(n, w1_ref[rows, :])
            b = _nt(n, w3_ref[rows, :])
            s = (a * jax.nn.sigmoid(a) * b).astype(BF16)
            o_ref[...] += 0.5 * _nn(s, w2_ref[rows, :])
            return c

        lax.fori_loop(0, nf, fstep, 0, unroll=True)

    (h,), landed = _call(
        body, name=name, grid=(t // tm,),
        in_specs=[_row_tile(tm, D_MODEL), _acc_row(D_MODEL), VMEM_FULL, VMEM_FULL, VMEM_FULL],
        out_specs=[_row_tile(tm, D_MODEL)],
        out_shape=[jax.ShapeDtypeStruct((t, D_MODEL), F32)],
        args=(x, g, w1t, w3t, w2), carry=carry)
    return h, landed


def _ffn_bwd(x, dh, g, w1t, w3t, w2, name, carry=None):
    t = x.shape[0]
    tm = _tile(t) // 2
    nf = D_FF // FFN_FT

    def body(x_ref, dh_ref, g_ref, w1_ref, w3_ref, w2_ref,
             dx_ref, dg_ref, da_ref, db_ref, s_ref, n_ref, dhh_ref, dn_acc):
        i = pl.program_id(0)
        xv = x_ref[...]
        gv = g_ref[...]
        xhat, r = _rms_parts(xv)
        n = (xhat * gv).astype(BF16)
        n_ref[...] = n
        dhv = dh_ref[...]
        dhh = (0.5 * dhv).astype(BF16)
        dhh_ref[...] = dhh
        dn_acc[...] = jnp.zeros_like(dn_acc)

        def fstep(f, c):
            rows = pl.ds(pl.multiple_of(f * FFN_FT, FFN_FT), FFN_FT)
            w1c, w3c, w2c = w1_ref[rows, :], w3_ref[rows, :], w2_ref[rows, :]
            a = _nt(n, w1c)
            b = _nt(n, w3c)
            sg = jax.nn.sigmoid(a)
            sl = a * sg
            ds = _nt(dhh, w2c)
            da = (ds * b * sg * (1.0 + a * (1.0 - sg))).astype(BF16)
            db = (ds * sl).astype(BF16)
            s_ref[f] = (sl * b).astype(BF16)
            da_ref[f] = da
            db_ref[f] = db
            dn_acc[...] += _nn(da, w1c) + _nn(db, w3c)
            return c

        lax.fori_loop(0, nf, fstep, 0, unroll=True)
        dx, dg = _rms_bwd(dn_acc[...], gv, xhat, r)
        dx_ref[...] = dhv + dx

        @pl.when(i == 0)
        def _():
            dg_ref[...] = jnp.zeros_like(dg_ref)

        dg_ref[...] += dg

    blk3 = pl.BlockSpec((nf, tm, FFN_FT), lambda i: (0, i, 0))
    sh3 = jax.ShapeDtypeStruct((nf, t, FFN_FT), BF16)
    return _call(
        body, name=name, grid=(t // tm,),
        in_specs=[_row_tile(tm, D_MODEL), _row_tile(tm, D_MODEL), _acc_row(D_MODEL), VMEM_FULL, VMEM_FULL, VMEM_FULL],
        out_specs=[_row_tile(tm, D_MODEL), _acc_row(D_MODEL), blk3, blk3, blk3,
                   _row_tile(tm, D_MODEL), _row_tile(tm, D_MODEL)],
        out_shape=[jax.ShapeDtypeStruct((t, D_MODEL), F32), jax.ShapeDtypeStruct((1, D_MODEL), F32), sh3, sh3, sh3,
                   jax.ShapeDtypeStruct((t, D_MODEL), BF16), jax.ShapeDtypeStruct((t, D_MODEL), BF16)],
        scratch_shapes=[pltpu.VMEM((tm, D_MODEL), F32)],
        args=(x, dh, g, w1t, w3t, w2), carry=carry)


def _mm_tn(a, b, name, carry=None):
    t, n = b.shape
    kc = min(512, t)
    if a.ndim == 3:
        nb, _, tb = a.shape
        a_spec = pl.BlockSpec((1, t, tb), lambda i: (i, 0, 0))
    else:
        m = a.shape[1]
        tb = min(m, 256)
        nb = m // tb
        a_spec = pl.BlockSpec((t, tb), lambda i: (0, i))
    three_d = a.ndim == 3

    def body(a_ref, b_ref, o_ref, acc):
        acc[...] = jnp.zeros_like(acc)

        def kstep(k, c):
            rows = pl.ds(pl.multiple_of(k * kc, kc), kc)
            av = a_ref[0, rows, :] if three_d else a_ref[rows, :]
            acc[...] += _tn(av.astype(BF16), b_ref[rows, :])
            return c

        lax.fori_loop(0, t // kc, kstep, 0, unroll=True)
        o_ref[...] = acc[...].astype(BF16)

    (out,), landed = _call(
        body, name=name, grid=(nb,),
        in_specs=[a_spec, VMEM_FULL],
        out_specs=[pl.BlockSpec((tb, n), lambda i: (i, 0))],
        out_shape=[jax.ShapeDtypeStruct((nb * tb, n), BF16)],
        scratch_shapes=[pltpu.VMEM((tb, n), F32)],
        args=(a, b), carry=carry)
    return (out, landed) if carry is not None else out


def _mix_pre_fwd(h, g, wint, carry=None):
    t = h.shape[0]
    tm = _tile(t)

    def body(h_ref, g_ref, w_ref, u_ref, *outs):
        xhat, _ = _rms_parts(h_ref[...])
        u = (xhat * g_ref[...]).astype(BF16)
        u_ref[...] = u
        for o_ref, off, size in zip(outs, IN_OFFS, IN_SIZES):
            o_ref[...] = _nt(u, w_ref[off:off + size, :])

    return _call(
        body, name="mix_pre_fwd", grid=(t // tm,),
        in_specs=[_row_tile(tm, D_MODEL), _acc_row(D_MODEL), VMEM_FULL],
        out_specs=[_row_tile(tm, D_MODEL)] + [_row_tile(tm, s) for s in IN_SIZES],
        out_shape=[jax.ShapeDtypeStruct((t, D_MODEL), BF16)] + [jax.ShapeDtypeStruct((t, s), F32) for s in IN_SIZES],
        args=(h, g, wint), carry=carry)


def _mix_pre_bwd(h, g, wint, dh2, dz):
    t = h.shape[0]
    tm = _tile(t)

    def body(h_ref, g_ref, w_ref, dh2_ref, *rest):
        dz_refs, (dh1_ref, dg_ref) = rest[:len(IN_SIZES)], rest[len(IN_SIZES):]
        i = pl.program_id(0)
        gv = g_ref[...]
        xhat, r = _rms_parts(h_ref[...])
        du = jnp.zeros((tm, D_MODEL), F32)
        for dz_ref, off, size in zip(dz_refs, IN_OFFS, IN_SIZES):
            du = du + _nn(dz_ref[...].astype(BF16), w_ref[off:off + size, :])
        dx, dg = _rms_bwd(du, gv, xhat, r)
        dh1_ref[...] = dh2_ref[...] + dx

        @pl.when(i == 0)
        def _():
            dg_ref[...] = jnp.zeros_like(dg_ref)

        dg_ref[...] += dg

    return pl.pallas_call(
        body, name="mix_pre_bwd", grid=(t // tm,),
        in_specs=[_row_tile(tm, D_MODEL), _acc_row(D_MODEL), VMEM_FULL, _row_tile(tm, D_MODEL)]
        + [_row_tile(tm, s) for s in IN_SIZES],
        out_specs=[_row_tile(tm, D_MODEL), _acc_row(D_MODEL)],
        out_shape=[jax.ShapeDtypeStruct((t, D_MODEL), F32), jax.ShapeDtypeStruct((1, D_MODEL), F32)],
        compiler_params=_cparams(1),
    )(h, g, wint, dh2, *dz)


def _disc_math(lre, lim, ldt, bre, bim):
    dt = jnp.exp(ldt)
    mag = jnp.exp(lre * dt)
    ar = mag * jnp.cos(lim * dt)
    ai = mag * jnp.sin(lim * dt)
    den = lre * lre + lim * lim
    nr = ar - 1.0
    fr = (nr * lre + ai * lim) / den
    fi = (ai * lre - nr * lim) / den
    return ar, ai, fr[None] * bre - fi[None] * bim, fr[None] * bim + fi[None] * bre


def _s5_disc(lre, lim, ldt, bre, bim):
    def body(lre_ref, lim_ref, ldt_ref, bre_ref, bim_ref, ar_ref, ai_ref, bbr_ref, bbi_ref):
        ar, ai, bbr, bbi = _disc_math(lre_ref[...], lim_ref[...], ldt_ref[...], bre_ref[...], bim_ref[...])
        ar_ref[...] = ar
        ai_ref[...] = ai
        bbr_ref[...] = bbr
        bbi_ref[...] = bbi

    small = jax.ShapeDtypeStruct(lre.shape, F32)
    big = jax.ShapeDtypeStruct(bre.shape, F32)
    return pl.pallas_call(body, name="s5_disc", out_shape=[small, small, big, big],
                          in_specs=[VMEM_FULL] * 5, out_specs=[VMEM_FULL] * 4)(lre, lim, ldt, bre, bim)


def _s5_disc_bwd(lre, lim, ldt, bre, bim, dar, dai, dbbr, dbbi):
    def body(lre_ref, lim_ref, ldt_ref, bre_ref, bim_ref, dar_ref, dai_ref, dbbr_ref, dbbi_ref,
             glre_ref, glim_ref, gldt_ref, gbre_ref, gbim_ref):
        _, vjp = jax.vjp(_disc_math, lre_ref[...], lim_ref[...], ldt_ref[...], bre_ref[...], bim_ref[...])
        glre, glim, gldt, gbre, gbim = vjp((dar_ref[...], dai_ref[...], dbbr_ref[...], dbbi_ref[...]))
        glre_ref[...] = glre
        glim_ref[...] = glim
        gldt_ref[...] = gldt
        gbre_ref[...] = gbre
        gbim_ref[...] = gbim

    small = jax.ShapeDtypeStruct(lre.shape, F32)
    big = jax.ShapeDtypeStruct(bre.shape, F32)
    return pl.pallas_call(body, name="s5_disc_bwd",
                          out_shape=[small, small, jax.ShapeDtypeStruct(ldt.shape, F32), big, big],
                          in_specs=[VMEM_FULL] * 9, out_specs=[VMEM_FULL] * 5,
                          )(lre, lim, ldt, bre, bim, dar, dai, dbbr, dbbi)


def _cmul(ar, ai, br, bi):
    return ar * br - ai * bi, ar * bi + ai * br


def _cpow(ar, ai, n):
    rr, ri = None, None
    pr, pi = ar, ai
    while n:
        if n & 1:
            rr, ri = (pr, pi) if rr is None else _cmul(rr, ri, pr, pi)
        n >>= 1
        if n:
            pr, pi = _cmul(pr, pi, pr, pi)
    return rr, ri


def _shift_rows(v, down):
    row = lax.broadcasted_iota(jnp.int32, v.shape, 0)
    if down:
        return jnp.where(row == 0, 0.0, pltpu.roll(v, 1, 0))
    return jnp.where(row == S5_SEGS - 1, 0.0, pltpu.roll(v, S5_SEGS - 1, 0))


def _chain_segments(er, ei, pr, pi, down):
    fr, fi = er, ei
    for _ in range(S5_SEGS - 1):
        sr, si = _shift_rows(fr, down), _shift_rows(fi, down)
        mr, mi = _cmul(pr, pi, sr, si)
        fr, fi = er + mr, ei + mi
    return _shift_rows(fr, down), _shift_rows(fi, down)


def _rows_to_scan_order(src_ref, dst_ref, t):
    ls = t // S5_SEGS

    def tile(j, c):
        dst_ref[pl.ds(pl.multiple_of(j * S5_SEGS, S5_SEGS), S5_SEGS), :] = src_ref[pl.ds(j, S5_SEGS, stride=ls), :]
        return c

    lax.fori_loop(0, ls, tile, 0, unroll=8)


def _rows_from_scan_order(src_ref, dst_ref, t):
    ls = t // S5_SEGS
    for s in range(S5_SEGS):
        def tile(jb, c, s=s):
            dst_ref[pl.ds(pl.multiple_of(s * ls + jb * 8, 8), 8), :] = (
                src_ref[pl.ds(jb * 8 * S5_SEGS + s, 8, stride=S5_SEGS), :])
            return c

        lax.fori_loop(0, ls // 8, tile, 0, unroll=8)


def _s5_fwd(ug, bd, ctd, ar4, ai4, dskip, carry=None):
    t = ug.shape[0]
    ls = t // S5_SEGS
    rc = min(512, t)
    ns = S5_BSTATE

    def body(ugn_ref, bd_ref, ct_ref, ar_ref, ai_ref, d_ref, xs_hbm, yn_ref, buf, ug_ref, y_ref, sem):
        cb = pl.program_id(0)
        bdv = bd_ref[0]
        _rows_to_scan_order(ugn_ref, ug_ref, t)

        def mm(i, c):
            rows = pl.ds(pl.multiple_of(i * rc, rc), rc)
            buf[rows, :] = _nn(ug_ref[rows, :].astype(BF16), bdv)
            return c

        lax.fori_loop(0, t // rc, mm, 0, unroll=True)
        arb = jnp.broadcast_to(ar_ref[0], (S5_SEGS, ns))
        aib = jnp.broadcast_to(ai_ref[0], (S5_SEGS, ns))

        def step(j, c, store):
            sr, si = c
            rows = pl.ds(pl.multiple_of(j * S5_SEGS, S5_SEGS), S5_SEGS)
            nr = arb * sr - aib * si + buf[rows, 0:ns]
            ni = arb * si + aib * sr + buf[rows, ns:2 * ns]
            if store:
                buf[rows, 0:ns] = nr
                buf[rows, ns:2 * ns] = ni
            return nr, ni

        zero = jnp.zeros((S5_SEGS, ns), F32)
        er, ei = lax.fori_loop(0, ls, functools.partial(step, store=False), (zero, zero))
        pr, pi = _cpow(arb, aib, ls)
        init = _chain_segments(er, ei, pr, pi, down=True)
        lax.fori_loop(0, ls, functools.partial(step, store=True), init)

        out = pltpu.make_async_copy(buf, xs_hbm.at[cb], sem)
        out.start()
        ctv = ct_ref[0]
        dv = d_ref[...]

        def ymm(i, c):
            rows = pl.ds(pl.multiple_of(i * rc, rc), rc)
            y_ref[rows, :] = _nn(buf[rows, :].astype(BF16), ctv) + dv * ug_ref[rows, :]
            return c

        lax.fori_loop(0, t // rc, ymm, 0, unroll=True)
        _rows_from_scan_order(y_ref, yn_ref, t)
        out.wait()

    return _call(
        body, name="s5_fwd", grid=(S5_BLOCKS,),
        in_specs=[pl.BlockSpec((t, 128), lambda i: (0, i)),
                  pl.BlockSpec((1, 128, 2 * ns), lambda i: (i, 0, 0)),
                  pl.BlockSpec((1, 2 * ns, 128), lambda i: (i, 0, 0)),
                  pl.BlockSpec((1, 1, ns), lambda i: (i, 0, 0)),
                  pl.BlockSpec((1, 1, ns), lambda i: (i, 0, 0)),
                  pl.BlockSpec((1, 128), lambda i: (0, i))],
        out_specs=[ANY, pl.BlockSpec((t, 128), lambda i: (0, i))],
        out_shape=[jax.ShapeDtypeStruct((S5_BLOCKS, t, 2 * ns), F32), jax.ShapeDtypeStruct((t, S5_WIDTH), F32)],
        scratch_shapes=[pltpu.VMEM((t, 2 * ns), F32), pltpu.VMEM((t, 128), F32), pltpu.VMEM((t, 128), F32),
                        pltpu.SemaphoreType.DMA(())],
        args=(ug, bd, ctd, ar4, ai4, dskip), carry=carry)


def _s5_bwd(dy, ug, xs, cd, bdt, ar4, ai4, dskip, carry=None):
    t = ug.shape[0]
    ls = t // S5_SEGS
    rc = min(512, t)
    ns = S5_BSTATE

    def body(dyn_ref, ugn_ref, xs_hbm, cd_ref, bdt_ref, ar_ref, ai_ref, d_ref,
             dugn_ref, dbd_ref, dcd_ref, dd_ref, dar_ref, dai_ref, xbuf, lam, dy_ref, ug_ref, dug_ref, sem):
        cb = pl.program_id(0)
        load = pltpu.make_async_copy(xs_hbm.at[cb], xbuf, sem)
        load.start()
        cdv = cd_ref[0]
        _rows_to_scan_order(dyn_ref, dy_ref, t)
        _rows_to_scan_order(ugn_ref, ug_ref, t)

        def mm(i, c):
            rows = pl.ds(pl.multiple_of(i * rc, rc), rc)
            lam[rows, :] = _nn(dy_ref[rows, :].astype(BF16), cdv)
            return c

        lax.fori_loop(0, t // rc, mm, 0, unroll=True)
        arb = jnp.broadcast_to(ar_ref[0], (S5_SEGS, ns))
        aib = jnp.broadcast_to(ai_ref[0], (S5_SEGS, ns))

        def lam_step(j, lr, li):
            rows = pl.ds(pl.multiple_of(j * S5_SEGS, S5_SEGS), S5_SEGS)
            nr = arb * lr + aib * li + lam[rows, 0:ns]
            ni = arb * li - aib * lr + lam[rows, ns:2 * ns]
            return rows, nr, ni

        def pass1(jj, c):
            _, nr, ni = lam_step(ls - 1 - jj, *c)
            return nr, ni

        zero = jnp.zeros((S5_SEGS, ns), F32)
        er, ei = lax.fori_loop(0, ls, pass1, (zero, zero))
        pr, pi = _cpow(arb, aib, ls)
        init = _chain_segments(er, ei, pr, -pi, down=False)
        load.wait()

        def accumulate(acc, nr, ni, xpr, xpi):
            return acc[0] + nr * xpr + ni * xpi, acc[1] + ni * xpr - nr * xpi

        def pass2(jj, c):
            lr, li, accr, acci = c
            j = ls - 1 - jj
            rows, nr, ni = lam_step(j, lr, li)
            lam[rows, 0:ns] = nr
            lam[rows, ns:2 * ns] = ni
            prev = pl.ds(pl.multiple_of((j - 1) * S5_SEGS, S5_SEGS), S5_SEGS)
            accr, acci = accumulate((accr, acci), nr, ni, xbuf[prev, 0:ns], xbuf[prev, ns:2 * ns])
            return nr, ni, accr, acci

        lr, li, accr, acci = lax.fori_loop(0, ls - 1, pass2, (init[0], init[1], zero, zero))
        rows, nr, ni = lam_step(0, lr, li)
        lam[rows, 0:ns] = nr
        lam[rows, ns:2 * ns] = ni
        last = pl.ds((ls - 1) * S5_SEGS, S5_SEGS)
        accr, acci = accumulate((accr, acci), nr, ni,
                                _shift_rows(xbuf[last, 0:ns], True), _shift_rows(xbuf[last, ns:2 * ns], True))
        dar_ref[0] = jnp.sum(accr, axis=0, keepdims=True)
        dai_ref[0] = jnp.sum(acci, axis=0, keepdims=True)

        bdtv = bdt_ref[0]
        dv = d_ref[...]
        dbd_ref[...] = jnp.zeros_like(dbd_ref)
        dcd_ref[...] = jnp.zeros_like(dcd_ref)
        dd_ref[...] = jnp.zeros_like(dd_ref)

        def tail(i, c):
            rows = pl.ds(pl.multiple_of(i * rc, rc), rc)
            dy = dy_ref[rows, :]
            ug = ug_ref[rows, :]
            lb = lam[rows, :].astype(BF16)
            dug_ref[rows, :] = _nn(lb, bdtv) + dv * dy
            dbd_ref[0] += _tn(ug.astype(BF16), lb)
            dcd_ref[0] += _tn(dy.astype(BF16), xbuf[rows, :].astype(BF16))
            dd_ref[...] += jnp.sum(dy * ug, axis=0, keepdims=True)
            return c

        lax.fori_loop(0, t // rc, tail, 0, unroll=True)
        _rows_from_scan_order(dug_ref, dugn_ref, t)

    chan = pl.BlockSpec((t, 128), lambda i: (0, i))
    dense = pl.BlockSpec((1, 128, 2 * ns), lambda i: (i, 0, 0))
    vec = pl.BlockSpec((1, 1, ns), lambda i: (i, 0, 0))
    return _call(
        body, name="s5_bwd", grid=(S5_BLOCKS,),
        in_specs=[chan, chan, ANY, dense, pl.BlockSpec((1, 2 * ns, 128), lambda i: (i, 0, 0)), vec, vec,
                  pl.BlockSpec((1, 128), lambda i: (0, i))],
        out_specs=[chan, dense, dense, pl.BlockSpec((1, 128), lambda i: (0, i)), vec, vec],
        out_shape=[jax.ShapeDtypeStruct((t, S5_WIDTH), F32),
                   jax.ShapeDtypeStruct((S5_BLOCKS, 128, 2 * ns), F32),
                   jax.ShapeDtypeStruct((S5_BLOCKS, 128, 2 * ns), F32),
                   jax.ShapeDtypeStruct((1, S5_WIDTH), F32),
                   jax.ShapeDtypeStruct((S5_BLOCKS, 1, ns), F32),
                   jax.ShapeDtypeStruct((S5_BLOCKS, 1, ns), F32)],
        scratch_shapes=[pltpu.VMEM((t, 2 * ns), F32), pltpu.VMEM((t, 2 * ns), F32)]
        + [pltpu.VMEM((t, 128), F32)] * 3 + [pltpu.SemaphoreType.DMA(())],
        args=(dy, ug, xs, cd, bdt, ar4, ai4, dskip), carry=carry)


def _cumsum_rows(x, reverse):
    c = x.shape[0]
    row = lax.broadcasted_iota(jnp.int32, x.shape, 0)
    d = 1
    while d < c:
        if reverse:
            x = x + jnp.where(row < c - d, pltpu.roll(x, c - d, 0), 0.0)
        else:
            x = x + jnp.where(row >= d, pltpu.roll(x, d, 0), 0.0)
        d *= 2
    return x


def _gla_common(q, k, alow, wup, bup):
    c = GLA_CHUNK
    pre = _nn(alow.astype(BF16), wup.astype(BF16)) + bup
    la = (jnp.minimum(pre, 0.0) - jnp.log(1.0 + jnp.exp(-jnp.abs(pre)))) * (1.0 / GLA_TAU)
    rr = lax.broadcasted_iota(jnp.int32, (c, c), 0)
    cc = lax.broadcasted_iota(jnp.int32, (c, c), 1)
    tril = (rr >= cc).astype(F32)
    bc = _cumsum_rows(la, reverse=False)
    bl = bc[c - 1:c, :]
    e_pos = jnp.exp(bc)
    e_neg = jnp.exp(-bc)
    e_end = jnp.exp(bl - bc)
    qt = q * (GLA_DK ** -0.5) * e_pos
    kt = k * e_neg
    ke = k * e_end
    lane = lax.broadcasted_iota(jnp.int32, (1, GLA_KEY), 1)
    masks = [((lane >= h * GLA_DK) & (lane < (h + 1) * GLA_DK)).astype(F32) for h in range(GLA_HEADS)]
    return dict(pre=pre, tril=tril, bc=bc, bl=bl, e_pos=e_pos, e_neg=e_neg, e_end=e_end,
                qt=qt, kt=kt, ke=ke, dec=jnp.exp(bl), masks=masks)


def _gla_fwd(q, k, v, alow, wup, bup, carry=None):
    t = q.shape[0]
    c = GLA_CHUNK
    n = t // c
    step = GLA_STEP_CHUNKS * c

    def body(q_ref, k_ref, v_ref, al_ref, wup_ref, bup_ref, o_ref, ss_ref, s_ref):
        i = pl.program_id(0)

        @pl.when(i == 0)
        def _():
            s_ref[...] = jnp.zeros_like(s_ref)

        wup_v, bup_v = wup_ref[...], bup_ref[...]
        s = s_ref[...]
        for j in range(GLA_STEP_CHUNKS):
            tok = slice(j * c, (j + 1) * c)
            m = _gla_common(q_ref[tok, :], k_ref[tok, :], al_ref[tok, :], wup_v, bup_v)
            ss_ref[j] = s
            sb = s.astype(BF16)
            ktb = m["kt"].astype(BF16)
            update = jnp.zeros_like(s)
            for h in range(GLA_HEADS):
                mask = m["masks"][h]
                qm = (m["qt"] * mask).astype(BF16)
                vh = v_ref[tok, h * GLA_DV:(h + 1) * GLA_DV].astype(BF16)
                p = (m["tril"] * _nt(qm, ktb)).astype(BF16)
                o_ref[tok, h * GLA_DV:(h + 1) * GLA_DV] = _nn(p, vh) + _nt(qm, sb)
                update = update + _tn(vh, (m["ke"] * mask).astype(BF16))
            s = m["dec"] * s + update
        s_ref[...] = s

    return _call(
        body, name="gla_fwd", grid=(t // step,),
        in_specs=[_row_tile(step, GLA_KEY), _row_tile(step, GLA_KEY), _row_tile(step, GLA_VAL),
                  _row_tile(step, GLA_RANK), VMEM_FULL, VMEM_FULL],
        out_specs=[_row_tile(step, GLA_VAL), pl.BlockSpec((GLA_STEP_CHUNKS, GLA_DV, GLA_KEY), lambda i: (i, 0, 0))],
        out_shape=[jax.ShapeDtypeStruct((t, GLA_VAL), F32), jax.ShapeDtypeStruct((n, GLA_DV, GLA_KEY), F32)],
        scratch_shapes=[pltpu.VMEM((GLA_DV, GLA_KEY), F32)],
        args=(q, k, v, alow, wup, bup), carry=carry)


def _gla_bwd(q, k, v, alow, wup, bup, ssave, do, carry=None):
    t = q.shape[0]
    c = GLA_CHUNK
    n = t // c

    def body(q_ref, k_ref, v_ref, al_ref, wup_ref, bup_ref, ss_ref, do_ref,
             dq_ref, dk_ref, dv_ref, dal_ref, dwup_ref, dbup_ref, ds_ref):
        i = pl.program_id(0)

        @pl.when(i == 0)
        def _():
            ds_ref[...] = jnp.zeros_like(ds_ref)
            dwup_ref[...] = jnp.zeros_like(dwup_ref)
            dbup_ref[...] = jnp.zeros_like(dbup_ref)

        wup_v, bup_v = wup_ref[...], bup_ref[...]
        ds_in = ds_ref[...]
        dwup = jnp.zeros((GLA_RANK, GLA_KEY), F32)
        dbup = jnp.zeros((1, GLA_KEY), F32)
        for j in reversed(range(GLA_STEP_CHUNKS)):
            tok = slice(j * c, (j + 1) * c)
            alow_v = al_ref[tok, :]
            m = _gla_common(q_ref[tok, :], k_ref[tok, :], alow_v, wup_v, bup_v)
            s = ss_ref[j]
            sb = s.astype(BF16)
            dsb = ds_in.astype(BF16)
            qt, kt, ke = m["qt"], m["kt"], m["ke"]
            ktb = kt.astype(BF16)
            dqt = jnp.zeros((c, GLA_KEY), F32)
            dkt = jnp.zeros((c, GLA_KEY), F32)
            dke = jnp.zeros((c, GLA_KEY), F32)
            update = jnp.zeros_like(ds_in)
            for h in range(GLA_HEADS):
                mask = m["masks"][h]
                qm = (qt * mask).astype(BF16)
                km = (kt * mask).astype(BF16)
                kem = (ke * mask).astype(BF16)
                cols = slice(h * GLA_DV, (h + 1) * GLA_DV)
                vh = v_ref[tok, cols].astype(BF16)
                doh = do_ref[tok, cols].astype(BF16)
                p = (m["tril"] * _nt(qm, ktb)).astype(BF16)
                dp = (m["tril"] * _nt(doh, vh)).astype(BF16)
                dv_ref[tok, cols] = _tn(p, doh) + _nt(kem, dsb)
                dqt = dqt + _nn(dp, km) + _nn(doh, sb) * mask
                dkt = dkt + _tn(dp, qm)
                dke = dke + _nn(vh, dsb) * mask
                update = update + _tn(doh, qm)
            ddec = jnp.sum(ds_in * s, axis=0, keepdims=True)
            dq_ref[tok, :] = dqt * m["e_pos"] * (GLA_DK ** -0.5)
            dk_ref[tok, :] = dkt * m["e_neg"] + dke * m["e_end"]
            dkeke = dke * ke
            dbl = jnp.sum(dkeke, axis=0, keepdims=True) + ddec * m["dec"]
            last = (lax.broadcasted_iota(jnp.int32, (c, 1), 0) == c - 1).astype(F32)
            dla = _cumsum_rows(dqt * qt - dkt * kt - dkeke + last * dbl, reverse=True)
            dpre = dla * (1.0 / GLA_TAU) * jax.nn.sigmoid(-m["pre"])
            dpb = dpre.astype(BF16)
            dal_ref[tok, :] = _nt(dpb, wup_v.astype(BF16))
            dwup = dwup + _tn(alow_v.astype(BF16), dpb)
            dbup = dbup + jnp.sum(dpre, axis=0, keepdims=True)
            ds_in = m["dec"] * ds_in + update
        ds_ref[...] = ds_in
        dwup_ref[...] += dwup
        dbup_ref[...] += dbup

    step = GLA_STEP_CHUNKS * c
    nsteps = t // step

    def rev(d):
        return pl.BlockSpec((step, d), lambda i: (nsteps - 1 - i, 0))

    return _call(
        body, name="gla_bwd", grid=(nsteps,),
        in_specs=[rev(GLA_KEY), rev(GLA_KEY), rev(GLA_VAL), rev(GLA_RANK), VMEM_FULL, VMEM_FULL,
                  pl.BlockSpec((GLA_STEP_CHUNKS, GLA_DV, GLA_KEY), lambda i: (nsteps - 1 - i, 0, 0)), rev(GLA_VAL)],
        out_specs=[rev(GLA_KEY), rev(GLA_KEY), rev(GLA_VAL), rev(GLA_RANK),
                   pl.BlockSpec((GLA_RANK, GLA_KEY), lambda i: (0, 0)), _acc_row(GLA_KEY)],
        out_shape=[jax.ShapeDtypeStruct((t, GLA_KEY), F32), jax.ShapeDtypeStruct((t, GLA_KEY), F32),
                   jax.ShapeDtypeStruct((t, GLA_VAL), F32), jax.ShapeDtypeStruct((t, GLA_RANK), F32),
                   jax.ShapeDtypeStruct((GLA_RANK, GLA_KEY), F32), jax.ShapeDtypeStruct((1, GLA_KEY), F32)],
        scratch_shapes=[pltpu.VMEM((GLA_DV, GLA_KEY), F32)],
        args=(q, k, v, alow, wup, bup, ssave, do), carry=carry)


def _post_math(y, o, r, gs5, ggla, wg, bg, gn, ps5t, pglat):
    y2 = y * y
    th = jnp.tanh(GELU_C0 * (y + GELU_C1 * y * y2))
    z5 = 0.5 * y * (1.0 + th)
    z5b = z5.astype(BF16)
    gate = jax.nn.sigmoid(_nn(z5b, wg) + bg)
    ys5 = z5 * gate
    rs, on = [], []
    for h in range(GLA_HEADS):
        oh = o[:, h * GLA_DV:(h + 1) * GLA_DV]
        rh = lax.rsqrt(jnp.mean(oh * oh, axis=-1, keepdims=True) + EPS)
        rs.append(rh)
        on.append(oh * rh)
    on = jnp.concatenate(on, axis=-1)
    sr = jax.nn.sigmoid(r)
    silu_r = r * sr
    ygla = on * gn * silu_r
    ys5b, yglab = ys5.astype(BF16), ygla.astype(BF16)
    m5 = _nt(ys5b, ps5t)
    mg = _nt(yglab, pglat)
    s5g, glag = jax.nn.sigmoid(gs5), jax.nn.sigmoid(ggla)
    merged = s5g * m5 + glag * mg
    return dict(y2=y2, th=th, z5=z5, z5b=z5b, gate=gate, ys5b=ys5b, yglab=yglab, rs=rs, on=on, sr=sr,
                silu_r=silu_r, m5=m5, mg=mg, s5g=s5g, glag=glag, mergedb=merged.astype(BF16))


def _mix_post_fwd(y, o, r, gs5, ggla, h1, wg, bg, gn, ps5t, pglat, wout, carry=None):
    t = o.shape[0]
    tm = _tile(t)

    def body(y_ref, o_ref, r_ref, gs5_ref, ggla_ref, h1_ref, wg_ref, bg_ref, gn_ref, ps_ref, pg_ref, wo_ref, h2_ref):
        m = _post_math(y_ref[...], o_ref[...], r_ref[...], gs5_ref[...], ggla_ref[...],
                       wg_ref[...], bg_ref[...], gn_ref[...], ps_ref[...], pg_ref[...])
        h2_ref[...] = h1_ref[...] + _nn(m["mergedb"], wo_ref[...])

    (h2,), landed = _call(
        body, name="mix_post_fwd", grid=(t // tm,),
        in_specs=[_row_tile(tm, 512)] * 3 + [_row_tile(tm, D_MODEL)] * 3
        + [VMEM_FULL, _acc_row(512), _acc_row(512), VMEM_FULL, VMEM_FULL, VMEM_FULL],
        out_specs=[_row_tile(tm, D_MODEL)],
        out_shape=[jax.ShapeDtypeStruct((t, D_MODEL), F32)],
        args=(y, o, r, gs5, ggla, h1, wg, bg, gn, ps5t, pglat, wout), carry=carry)
    return h2, landed


def _mix_post_bwd(y, o, r, gs5, ggla, dh2, wg, bg, gn, ps5t, pglat, wout, carry=None):
    t = o.shape[0]
    tm = _tile(t) // 2

    def body(y_ref, o_ref, r_ref, gs5_ref, ggla_ref, dh2_ref, wg_ref, bg_ref, gn_ref, ps_ref, pg_ref, wo_ref,
             dy_ref, do_ref, dr_ref, dgs5_ref, dggla_ref, dbg_ref, dgn_ref,
             z5b_ref, dgp_ref, ys5b_ref, dm5b_ref, yglab_ref, dmgb_ref, mergedb_ref, dh2b_ref):
        i = pl.program_id(0)
        yv, ov, rv = y_ref[...], o_ref[...], r_ref[...]
        wg, gn, ps5t, pglat = wg_ref[...], gn_ref[...], ps_ref[...], pg_ref[...]
        m = _post_math(yv, ov, rv, gs5_ref[...], ggla_ref[...], wg, bg_ref[...], gn, ps5t, pglat)
        dh2b = dh2_ref[...].astype(BF16)
        dmerged = _nt(dh2b, wo_ref[...])
        s5g, glag = m["s5g"], m["glag"]
        dgs5_ref[...] = dmerged * m["m5"] * s5g * (1.0 - s5g)
        dggla_ref[...] = dmerged * m["mg"] * glag * (1.0 - glag)
        dm5b = (dmerged * s5g).astype(BF16)
        dmgb = (dmerged * glag).astype(BF16)
        dys5 = _nn(dm5b, ps5t)
        dygla = _nn(dmgb, pglat)
        gate, z5, th = m["gate"], m["z5"], m["th"]
        dgpre = dys5 * z5 * gate * (1.0 - gate)
        dgpb = dgpre.astype(BF16)
        dz5 = dys5 * gate + _nt(dgpb, wg)
        dgelu = 0.5 * (1.0 + th) + 0.5 * yv * (1.0 - th * th) * GELU_C0 * (1.0 + 3.0 * GELU_C1 * m["y2"])
        dy_ref[...] = dz5 * dgelu
        on, sr, silu_r = m["on"], m["sr"], m["silu_r"]
        dr_ref[...] = dygla * on * gn * sr * (1.0 + rv * (1.0 - sr))
        dgn = jnp.sum(dygla * on * silu_r, axis=0, keepdims=True)
        don = dygla * gn * silu_r
        for h in range(GLA_HEADS):
            cols = slice(h * GLA_DV, (h + 1) * GLA_DV)
            donh, onh = don[:, cols], on[:, cols]
            do_ref[:, cols] = m["rs"][h] * (donh - onh * jnp.mean(donh * onh, axis=-1, keepdims=True))

        @pl.when(i == 0)
        def _():
            dbg_ref[...] = jnp.zeros_like(dbg_ref)
            dgn_ref[...] = jnp.zeros_like(dgn_ref)

        dbg_ref[...] += jnp.sum(dgpre, axis=0, keepdims=True)
        dgn_ref[...] += dgn
        z5b_ref[...] = m["z5b"]
        dgp_ref[...] = dgpb
        ys5b_ref[...] = m["ys5b"]
        dm5b_ref[...] = dm5b
        yglab_ref[...] = m["yglab"]
        dmgb_ref[...] = dmgb
        mergedb_ref[...] = m["mergedb"]
        dh2b_ref[...] = dh2b

    def f32(d):
        return jax.ShapeDtypeStruct((t, d), F32)

    def b16(d):
        return jax.ShapeDtypeStruct((t, d), BF16)

    widths = (512, 512, 512, 1024, 512, 1024, 1024, 1024)
    return _call(
        body, name="mix_post_bwd", grid=(t // tm,),
        in_specs=[_row_tile(tm, 512)] * 3 + [_row_tile(tm, D_MODEL)] * 3
        + [VMEM_FULL, _acc_row(512), _acc_row(512), VMEM_FULL, VMEM_FULL, VMEM_FULL],
        out_specs=[_row_tile(tm, 512)] * 3 + [_row_tile(tm, D_MODEL)] * 2
        + [_acc_row(512)] * 2 + [_row_tile(tm, w) for w in widths],
        out_shape=[f32(512)] * 3 + [f32(D_MODEL)] * 2
        + [jax.ShapeDtypeStruct((1, 512), F32)] * 2
        + [b16(w) for w in widths],
        args=(y, o, r, gs5, ggla, dh2, wg, bg, gn, ps5t, pglat, wout), carry=carry)


def _head(h3, g, target):
    t = h3.shape[0]
    tm = _tile(t)

    def body(h_ref, g_ref, t_ref, loss_ref, dh_ref, dg_ref):
        i = pl.program_id(0)
        gv = g_ref[...]
        xhat, r = _rms_parts(h_ref[...])
        err = xhat * gv - t_ref[...]
        dx, dg = _rms_bwd(err * (1.0 / D_MODEL), gv, xhat, r)
        dh_ref[...] = dx

        @pl.when(i == 0)
        def _():
            loss_ref[...] = jnp.zeros_like(loss_ref)
            dg_ref[...] = jnp.zeros_like(dg_ref)

        loss_ref[...] += (0.5 / D_MODEL) * jnp.sum(jnp.sum(err * err, axis=1, keepdims=True), axis=0, keepdims=True)
        dg_ref[...] += dg

    return pl.pallas_call(
        body, name="head", grid=(t // tm,),
        in_specs=[_row_tile(tm, D_MODEL), _acc_row(D_MODEL), _row_tile(tm, D_MODEL)],
        out_specs=[pl.BlockSpec((1, 1), lambda i: (0, 0)), _row_tile(tm, D_MODEL), _acc_row(D_MODEL)],
        out_shape=[jax.ShapeDtypeStruct((1, 1), F32), jax.ShapeDtypeStruct((t, D_MODEL), F32),
                   jax.ShapeDtypeStruct((1, D_MODEL), F32)],
        compiler_params=_cparams(1),
    )(h3, g, target)


ADAM_TILE_ELEMS = 256 * 1024


def _adamw(w, g, m, v, name):
    rows, cols = w.shape
    tr = rows
    while tr * cols > ADAM_TILE_ELEMS and tr % 16 == 0:
        tr //= 2

    spec = pl.BlockSpec((tr, cols), lambda i: (i, 0))
    sh = jax.ShapeDtypeStruct((rows, cols), F32)
    return pl.pallas_call(functools.partial(_adamw_body), name=name, grid=(rows // tr,), in_specs=[spec] * 4,
                          out_specs=[spec] * 3, out_shape=[sh] * 3, compiler_params=_cparams(1))(w, g, m, v)


def _adamw_body(w_ref, g_ref, m_ref, v_ref, d_ref, nm_ref, nv_ref):
    gv = g_ref[...]
    nm = ADAM_B1 * m_ref[...] + (1.0 - ADAM_B1) * gv
    nv = ADAM_B2 * v_ref[...] + (1.0 - ADAM_B2) * (gv * gv)
    m_hat = nm / (1.0 - ADAM_B1 ** ADAM_STEP)
    v_hat = nv / (1.0 - ADAM_B2 ** ADAM_STEP)
    d_ref[...] = -ADAM_LR * (m_hat / (jnp.sqrt(v_hat) + ADAM_EPS) + ADAM_WD * w_ref[...])
    nm_ref[...] = nm
    nv_ref[...] = nv


def _adamw_many(ws, gs, ms, vs, name):
    n = len(ws)

    def body(*refs):
        ins, outs = refs[:4 * n], refs[4 * n:]
        for i in range(n):
            _adamw_body(*(ins[j * n + i] for j in range(4)), *(outs[j * n + i] for j in range(3)))

    shapes = [jax.ShapeDtypeStruct(w.shape, F32) for w in ws]
    res = pl.pallas_call(body, name=name, in_specs=[VMEM_FULL] * (4 * n), out_specs=[VMEM_FULL] * (3 * n),
                         out_shape=shapes * 3)(*ws, *gs, *ms, *vs)
    return res[:n], res[n:2 * n], res[2 * n:]


def _exchange(srcs, scatter, name):
    n = len(srcs)

    def body(*refs):
        _exchange_start(refs[:n], refs[n:2 * n], *refs[2 * n:], scatter=scatter)
        _exchange_wait(refs[:n], refs[n:2 * n], *refs[2 * n:], scatter=scatter)

    return pl.pallas_call(
        body, name=name, in_specs=[ANY] * n, out_specs=[ANY] * n,
        out_shape=_exchange_shapes(srcs, scatter), scratch_shapes=_exchange_sems(n),
    )(*srcs)


def _sum_slabs(slabs, name):
    n = slabs.shape[0]

    def body(s_ref, o_ref):
        acc = s_ref[0].astype(F32)
        for s in range(1, n):
            acc = acc + s_ref[s].astype(F32)
        o_ref[...] = acc

    return pl.pallas_call(
        body, name=name, in_specs=[VMEM_FULL], out_specs=VMEM_FULL,
        out_shape=jax.ShapeDtypeStruct(slabs.shape[1:], F32),
        compiler_params=pltpu.CompilerParams(vmem_limit_bytes=VMEM_LIMIT_BYTES),
    )(slabs)


BIG = ("ffn1_w1", "ffn1_w3", "ffn1_w2", "w_in", "s5_glu_w", "gla_a_up_w", "proj_s5", "proj_gla", "w_out",
       "ffn2_w1", "ffn2_w3", "ffn2_w2")
GROUPS = (("ffn1_w1", "ffn1_w3", "ffn1_w2"),
          ("w_in", "s5_glu_w", "gla_a_up_w", "proj_s5", "proj_gla", "w_out"),
          ("ffn2_w1", "ffn2_w3", "ffn2_w2"))
W_IN_ROWS = 514
W_IN_PAD = 528
UP_COLS = 32
COL_SHARDED = ("ffn1_w1", "ffn1_w3", "w_in", "proj_s5", "proj_gla", "ffn2_w1", "ffn2_w3")

SMALL = ("ffn1_norm", "mix_norm", "s5_lambda_re", "s5_lambda_im", "s5_log_dt", "s5_b_re", "s5_b_im", "s5_c_re",
         "s5_c_im", "s5_d", "s5_glu_b", "gla_a_up_b", "gla_out_norm", "ffn2_norm", "final_norm")
SMALL_SHAPES = dict(ffn1_norm=(1, 1024), mix_norm=(1, 1024), s5_lambda_re=(1, 32, 64), s5_lambda_im=(1, 32, 64),
                    s5_log_dt=(1, 32), s5_b_re=(1, 32, 64, 16), s5_b_im=(1, 32, 64, 16), s5_c_re=(1, 32, 16, 64),
                    s5_c_im=(1, 32, 16, 64), s5_d=(1, 32, 16), s5_glu_b=(1, 512), gla_a_up_b=(1, 256),
                    gla_out_norm=(1, 512), ffn2_norm=(1, 1024), final_norm=(1024,))
SMALL_N = sum(math.prod(s) for s in SMALL_SHAPES.values())
SMALL_R = -(-SMALL_N // (64 * 1024)) * 64


def _shard_rows(name, a):
    if name == "gla_a_up_w":
        return jnp.pad(a, ((0, 0), (0, 128 - UP_COLS)))
    if name in COL_SHARDED:
        a = a.T
    if name == "w_in":
        return jnp.pad(a, ((0, W_IN_PAD - W_IN_ROWS), (0, 0)))
    return a.reshape(-1, 1024)


def _unshard_rows(name, rows, shape):
    if name == "gla_a_up_w":
        return rows[:, :UP_COLS]
    if name == "w_in":
        rows = rows[:W_IN_ROWS]
    if name in COL_SHARDED:
        return rows.reshape(shape[1], shape[0]).T
    return rows.reshape(shape)


def _pack_small(vals):
    flat = jnp.concatenate([vals[n].reshape(-1).astype(F32) for n in SMALL])
    return jnp.pad(flat, (0, SMALL_R * 1024 - SMALL_N)).reshape(SMALL_R, 1024)


def _unpack_small(slab):
    flat = slab.reshape(-1)
    out, off = {}, 0
    for n in SMALL:
        size = math.prod(SMALL_SHAPES[n])
        out[n] = flat[off:off + size].reshape(SMALL_SHAPES[n])
        off += size
    return out


FULL_SHAPES = dict(w_in=(IN_COLS, D_MODEL), s5_glu_w=(S5_WIDTH, S5_WIDTH), gla_a_up_w=(GLA_RANK, GLA_KEY),
                   proj_s5=(D_MODEL, S5_WIDTH), proj_gla=(D_MODEL, GLA_VAL), w_out=(D_MODEL, D_MODEL))


def _full_weight(name, gathered):
    if name == "gla_a_up_w":
        return gathered[:, :, :UP_COLS].transpose(1, 0, 2).reshape(GLA_RANK, GLA_KEY)
    if name == "w_in":
        gathered = gathered[:, :W_IN_ROWS]
    return gathered.reshape(FULL_SHAPES.get(name, (D_FF, D_MODEL)))


def _grad_slabs(name, g):
    if name == "gla_a_up_w":
        g = g.reshape(GLA_RANK, N_DEV, UP_COLS).transpose(1, 0, 2)
        return jnp.pad(g, ((0, 0), (0, 0), (0, 128 - UP_COLS))).astype(BF16)
    if name == "w_in":
        return jnp.pad(g.reshape(N_DEV, W_IN_ROWS, D_MODEL), ((0, 0), (0, W_IN_PAD - W_IN_ROWS), (0, 0)))
    return g.reshape(N_DEV, -1, 1024)


def _s5_dense(re, im, sign_im):
    eye = jnp.eye(8, dtype=F32)

    def one(a):
        a = a.reshape(S5_BLOCKS, 8, S5_GROUP, S5_STATE)
        return jnp.einsum("cghp,gk->cghkp", a, eye).reshape(S5_BLOCKS, 128, S5_BSTATE)

    return jnp.concatenate([one(re), sign_im * one(im)], axis=-1)


def _s5_undense(d):
    eye = jnp.eye(8, dtype=F32)

    def one(a):
        a = a.reshape(S5_BLOCKS, 8, S5_GROUP, 8, S5_STATE)
        return jnp.einsum("cghkp,gk->cghp", a, eye).reshape(S5_GROUPS, S5_GROUP, S5_STATE)

    return one(d[..., :S5_BSTATE]), one(d[..., S5_BSTATE:])


def _local_step(x, target, p, w, rows=None):
    w = dict(w or {})
    landed_grads = {}

    def gather(names):
        return None if rows is None else ([rows[n] for n in names], False)

    def gathered(names, landed):
        w.update({n: _full_weight(n, g) for n, g in zip(names, landed)})

    def scatter(names):
        return None if rows is None else ([_grad_slabs(n, big[n]) for n in names], True)

    def scattered(names, landed):
        landed_grads.update(zip(names, landed))

    if rows is not None:
        gathered(GROUPS[0], _exchange(gather(GROUPS[0])[0], False, "gather_ffn1"))
    g1, gm, g2 = p["ffn1_norm"], p["mix_norm"], p["ffn2_norm"]
    gf = p["final_norm"].reshape(1, D_MODEL)
    lre, lim = p["s5_lambda_re"][0], p["s5_lambda_im"][0]
    ldt = p["s5_log_dt"][0].reshape(S5_GROUPS, 1)
    bre = p["s5_b_re"][0].transpose(2, 0, 1)
    bim = p["s5_b_im"][0].transpose(2, 0, 1)
    cre, cim = p["s5_c_re"][0], p["s5_c_im"][0]
    dskip = p["s5_d"][0].reshape(1, S5_WIDTH)
    bg, bup, gn = p["s5_glu_b"], p["gla_a_up_b"], p["gla_out_norm"]

    mix_first, mix_rest = ("w_in", "gla_a_up_w"), ("s5_glu_w", "proj_s5", "proj_gla", "w_out")
    h1, got = _ffn_fwd(x, g1, w["ffn1_w1"], w["ffn1_w3"], w["ffn1_w2"], "ffn1_fwd", gather(mix_first))
    gathered(mix_first, got)
    wup = w["gla_a_up_w"].astype(F32)
    (u, s5in, q, k, v, r, alow, gs5, ggla), got = _mix_pre_fwd(h1, gm, w["w_in"], gather(mix_rest))
    gathered(mix_rest, got)
    ar, ai, bbr, bbi = _s5_disc(lre, lim, ldt, bre, bim)
    bd = _s5_dense(bbr.transpose(1, 0, 2), bbi.transpose(1, 0, 2), 1.0)
    cd = _s5_dense(cre, cim, -1.0)
    bd16, cd16 = bd.astype(BF16), cd.astype(BF16)
    bdt16, ctd16 = bd16.transpose(0, 2, 1), cd16.transpose(0, 2, 1)
    ar4 = ar.reshape(S5_BLOCKS, 1, S5_BSTATE)
    ai4 = ai.reshape(S5_BLOCKS, 1, S5_BSTATE)
    (xs, y), got = _s5_fwd(s5in, bd16, ctd16, ar4, ai4, dskip, gather(GROUPS[2][:1]))
    gathered(GROUPS[2][:1], got)
    (o, ssave), got = _gla_fwd(q, k, v, alow, wup, bup, gather(GROUPS[2][1:2]))
    gathered(GROUPS[2][1:2], got)
    post_w = (w["s5_glu_w"], bg, gn, w["proj_s5"], w["proj_gla"], w["w_out"])
    h2, got = _mix_post_fwd(y, o, r, gs5, ggla, h1, *post_w, carry=gather(GROUPS[2][2:]))
    gathered(GROUPS[2][2:], got)
    h3, _ = _ffn_fwd(h2, g2, w["ffn2_w1"], w["ffn2_w3"], w["ffn2_w2"], "ffn2_fwd")
    loss, dh3, dgf = _head(h3, gf, target)

    big, small = {}, {}
    small["final_norm"] = dgf.reshape(D_MODEL)
    (dh2, dg2, da3, db3, s3, n2, dhh2), _ = _ffn_bwd(
        h2, dh3, g2, w["ffn2_w1"], w["ffn2_w3"], w["ffn2_w2"], "ffn2_bwd")
    small["ffn2_norm"] = dg2
    big["ffn2_w1"] = _mm_tn(da3, n2, "ffn2_dw1")
    big["ffn2_w3"] = _mm_tn(db3, n2, "ffn2_dw3")
    big["ffn2_w2"] = _mm_tn(s3, dhh2, "ffn2_dw2")
    (dy, do, dr, dgs5, dggla, dbg, dgn, z5b, dgpb, ys5b, dm5b, yglab, dmgb, mergedb, dh2b), got = _mix_post_bwd(
        y, o, r, gs5, ggla, dh2, *post_w, carry=scatter(GROUPS[2][:1]))
    scattered(GROUPS[2][:1], got)
    small["s5_glu_b"] = dbg
    small["gla_out_norm"] = dgn
    big["s5_glu_w"] = _mm_tn(z5b, dgpb, "glu_dw")
    big["proj_s5"] = _mm_tn(dm5b, ys5b, "proj_s5_dw")
    big["proj_gla"] = _mm_tn(dmgb, yglab, "proj_gla_dw")
    big["w_out"] = _mm_tn(mergedb, dh2b, "w_out_dw")
    (dq, dk, dv, dalow, dwup, dbup), got = _gla_bwd(q, k, v, alow, wup, bup, ssave, do, scatter(GROUPS[2][1:2]))
    scattered(GROUPS[2][1:2], got)
    big["gla_a_up_w"] = dwup
    small["gla_a_up_b"] = dbup
    (ds5in, dbd, dcd, dd, dar4, dai4), got = _s5_bwd(
        dy, s5in, xs, cd16, bdt16, ar4, ai4, dskip, scatter(GROUPS[2][2:]))
    scattered(GROUPS[2][2:], got)
    dbbr, dbbi = _s5_undense(dbd)
    dcre, dcim_neg = _s5_undense(dcd)
    glre, glim, gldt, gbre, gbim = _s5_disc_bwd(
        lre, lim, ldt, bre, bim, dar4.reshape(S5_GROUPS, S5_STATE), dai4.reshape(S5_GROUPS, S5_STATE),
        dbbr.transpose(1, 0, 2), dbbi.transpose(1, 0, 2))
    small["s5_lambda_re"] = glre[None]
    small["s5_lambda_im"] = glim[None]
    small["s5_log_dt"] = gldt.reshape(1, S5_GROUPS)
    small["s5_b_re"] = gbre.transpose(1, 2, 0)[None]
    small["s5_b_im"] = gbim.transpose(1, 2, 0)[None]
    small["s5_c_re"] = dcre[None]
    small["s5_c_im"] = -dcim_neg[None]
    small["s5_d"] = dd.reshape(1, S5_GROUPS, S5_GROUP)
    dz = (ds5in, dq, dk, dv, dr, dalow, dgs5, dggla)
    dh1, dgm = _mix_pre_bwd(h1, gm, w["w_in"], dh2, dz)
    small["mix_norm"] = dgm
    big["w_in"] = jnp.concatenate([_mm_tn(d, u, "w_in_dw%d" % i) for i, d in enumerate(dz)], axis=0)
    (dx, dg1, da3, db3, s3, n1, dhh1), got = _ffn_bwd(
        x, dh1, g1, w["ffn1_w1"], w["ffn1_w3"], w["ffn1_w2"], "ffn1_bwd", scatter(GROUPS[1]))
    scattered(GROUPS[1], got)
    small["ffn1_norm"] = dg1
    big["ffn1_w1"] = _mm_tn(da3, n1, "ffn1_dw1")
    if rows is None:
        big["ffn1_w3"] = _mm_tn(db3, n1, "ffn1_dw3")
        big["ffn1_w2"] = _mm_tn(s3, dhh1, "ffn1_dw2")
        return loss[0, 0], dx, big, small
    big["ffn1_w3"], got = _mm_tn(db3, n1, "ffn1_dw3", scatter(GROUPS[0][:1]))
    scattered(GROUPS[0][:1], got)
    big["ffn1_w2"], got = _mm_tn(s3, dhh1, "ffn1_dw2", scatter(GROUPS[0][1:2]))
    scattered(GROUPS[0][1:2], got)
    scattered(GROUPS[0][2:], _exchange(scatter(GROUPS[0][2:])[0], True, "scatter_ffn1_w2"))
    return loss[0, 0], dx, landed_grads, small


NAMES = ("ffn1_norm", "ffn1_w1", "ffn1_w3", "ffn1_w2", "mix_norm", "w_in", "s5_lambda_re", "s5_lambda_im",
         "s5_log_dt", "s5_b_re", "s5_b_im", "s5_c_re", "s5_c_im", "s5_d", "s5_glu_w", "s5_glu_b", "gla_a_up_w",
         "gla_a_up_b", "gla_out_norm", "proj_s5", "proj_gla", "w_out", "ffn2_norm", "ffn2_w1", "ffn2_w3", "ffn2_w2",
         "final_norm")


def kernel(*args):
    nw = len(NAMES)
    x = args[0][0]
    wts = dict(zip(NAMES, args[1:1 + nw]))
    target = args[1 + nw][0]
    mom = dict(zip(NAMES, args[2 + nw:2 + 2 * nw]))
    var = dict(zip(NAMES, args[2 + 2 * nw:2 + 3 * nw]))

    shards = {n: wts[n][0] for n in BIG}
    rows = {n: _shard_rows(n, shards[n]).astype(BF16) for n in BIG}
    loss, dx, landed, small = _local_step(x, target, {n: wts[n] for n in SMALL}, None, rows)
    loss = lax.psum(loss, ("x", "y", "c"))

    grad, delta, new_m, new_v = {}, {}, {}, {}
    for n in BIG:
        g = _unshard_rows(n, _sum_slabs(landed[n], "sum_" + n), shards[n].shape)
        grad[n] = g[None]
        delta[n], new_m[n], new_v[n] = (a[None] for a in _adamw(shards[n], g, mom[n][0], var[n][0], "adamw_" + n))

    part = _pack_small(small).reshape(N_DEV, SMALL_R // N_DEV, 1024)
    mine = _sum_slabs(_exchange([part], True, "scatter_small")[0], "sum_small")
    g_small = _exchange([mine], False, "gather_small")[0].reshape(SMALL_R, 1024)
    grad.update(_unpack_small(g_small))

    def flat2d(a):
        return a.reshape(-1, a.shape[-1])

    outs = _adamw_many(*([flat2d(d[n]) for n in SMALL] for d in (wts, grad, mom, var)), "adamw_small")
    for out, arrays in zip((delta, new_m, new_v), outs):
        out.update({n: a.reshape(SMALL_SHAPES[n]) for n, a in zip(SMALL, arrays)})
    return (loss, dx[None], *(d[n] for d in (grad, delta, new_m, new_v) for n in NAMES))
```

```python
import functools
import math

import jax
import jax.numpy as jnp
from jax import lax
from jax.experimental import pallas as pl
from jax.experimental.pallas import tpu as pltpu

F32, BF16 = jnp.float32, jnp.bfloat16
HIGHEST = lax.Precision.HIGHEST

D_MODEL = 1024
D_FF = 2816
N_DEV = 8
S5_WIDTH, S5_GROUPS, S5_GROUP, S5_STATE = 512, 32, 16, 64
S5_BLOCKS = 4
S5_BSTATE = 512
S5_SEGS = 8
GLA_HEADS, GLA_DK, GLA_DV = 4, 64, 128
GLA_KEY, GLA_VAL, GLA_RANK, GLA_CHUNK = 256, 512, 16, 64
GLA_TAU = 16.0
GLA_STEP_CHUNKS = 4
EPS = 1e-6
IN_SIZES = (512, 256, 256, 512, 512, 16, 1024, 1024)
IN_OFFS = tuple(sum(IN_SIZES[:i]) for i in range(len(IN_SIZES)))
IN_COLS = sum(IN_SIZES)
ADAM_LR, ADAM_B1, ADAM_B2, ADAM_EPS, ADAM_WD, ADAM_STEP = 0.001, 0.9, 0.999, 1e-08, 0.01, 10
GELU_C0 = math.sqrt(2.0 / math.pi)
GELU_C1 = 0.044715

FFN_FT = 256
VMEM_LIMIT_BYTES = 56 * 1024 * 1024

VMEM_FULL = pl.BlockSpec(memory_space=pltpu.VMEM)
ANY = pl.BlockSpec(memory_space=pl.ANY)


def _cparams(n_grid):
    return pltpu.CompilerParams(dimension_semantics=("arbitrary",) * n_grid, vmem_limit_bytes=VMEM_LIMIT_BYTES)


def _tile(t):
    return 512 if t >= 1024 else t // 2


def _nn(a, b):
    return jnp.dot(a, b, preferred_element_type=F32)


def _nt(a, b):
    return lax.dot_general(a, b, (((1,), (1,)), ((), ())), preferred_element_type=F32)


def _tn(a, b):
    return lax.dot_general(a, b, (((0,), (0,)), ((), ())), preferred_element_type=F32)


def _rms_parts(x):
    r = lax.rsqrt(jnp.mean(x * x, axis=-1, keepdims=True) + EPS)
    return x * r, r


def _rms_bwd(dn, g, xhat, r):
    dxh = dn * g
    dx = r * (dxh - xhat * jnp.mean(dxh * xhat, axis=-1, keepdims=True))
    return dx, jnp.sum(dn * xhat, axis=0, keepdims=True)


def _peers():
    x, y, c = lax.axis_index("x"), lax.axis_index("y"), lax.axis_index("c")
    out = []
    for k in range(1, N_DEV):
        px = 1 - x if k & 4 else x
        py = 1 - y if k & 2 else y
        pc = 1 - c if k & 1 else c
        out.append(((px, py, pc), 4 * px + 2 * py + pc))
    return 4 * x + 2 * y + c, out


def _exchange_copies(src_refs, out_refs, send_sems, recv_sems, local_sems, scatter, with_recvs):
    me, peers = _peers()
    locals_, sends, recvs = [], [], []
    for a, (src_ref, out_ref) in enumerate(zip(src_refs, out_refs)):
        def mine(idx, src_ref=src_ref):
            return src_ref.at[idx] if scatter else src_ref

        locals_.append(pltpu.make_async_copy(mine(me), out_ref.at[me], local_sems.at[a]))
        for k, (dev, idx) in enumerate(peers):
            sends.append(pltpu.make_async_remote_copy(
                src_ref=mine(idx), dst_ref=out_ref.at[me], send_sem=send_sems.at[a, k], recv_sem=recv_sems.at[a, k],
                device_id=dev, device_id_type=pl.DeviceIdType.MESH))
            if with_recvs:
                recvs.append(pltpu.make_async_remote_copy(
                    src_ref=mine(idx), dst_ref=out_ref.at[idx], send_sem=send_sems.at[a, k],
                    recv_sem=recv_sems.at[a, k], device_id=dev, device_id_type=pl.DeviceIdType.MESH))
    return locals_, sends, recvs


def _remote(src, dst, send_sems, recv_sems, a, k, dev):
    return pltpu.make_async_remote_copy(src_ref=src, dst_ref=dst, send_sem=send_sems.at[a, k],
                                        recv_sem=recv_sems.at[a, k], device_id=dev,
                                        device_id_type=pl.DeviceIdType.MESH)


def _gather_places():
    x, y, c = lax.axis_index("x"), lax.axis_index("y"), lax.axis_index("c")
    chips = [(1 - x, y), (x, 1 - y), (1 - x, 1 - y)]
    sibling = (x, y, 1 - c)
    me_idx, sib_idx = 4 * x + 2 * y + c, 4 * x + 2 * y + 1 - c
    same_core = [((cx, cy, c), 4 * cx + 2 * cy + c) for cx, cy in chips]
    other_core_idx = [4 * cx + 2 * cy + 1 - c for cx, cy in chips]
    return sibling, me_idx, sib_idx, same_core, other_core_idx


def _gather_start(src_refs, out_refs, send_sems, recv_sems, local_sems):
    sibling, me_idx, _, same_core, _ = _gather_places()
    for a, (src, out) in enumerate(zip(src_refs, out_refs)):
        pltpu.make_async_copy(src, out.at[me_idx], local_sems.at[a]).start()
        _remote(src, out.at[me_idx], send_sems, recv_sems, a, 0, sibling).start()
        for j, (dev, _) in enumerate(same_core):
            _remote(src, out.at[me_idx], send_sems, recv_sems, a, 1 + j, dev).start()


def _gather_finish(src_refs, out_refs, send_sems, recv_sems, local_sems):
    sibling, me_idx, sib_idx, same_core, other_core_idx = _gather_places()
    arrays = list(enumerate(zip(src_refs, out_refs)))
    forwards = []
    for a, (src, out) in arrays:
        for j, (dev, idx) in enumerate(same_core):
            _remote(src, out.at[idx], send_sems, recv_sems, a, 1 + j, dev).wait_recv()
            fwd = _remote(out.at[idx], out.at[idx], send_sems, recv_sems, a, 4 + j, sibling)
            fwd.start()
            forwards.append(fwd)
    for a, (src, out) in arrays:
        _remote(src, out.at[sib_idx], send_sems, recv_sems, a, 0, sibling).wait_recv()
        for j, idx in enumerate(other_core_idx):
            _remote(src, out.at[idx], send_sems, recv_sems, a, 4 + j, sibling).wait_recv()
        _remote(src, out.at[me_idx], send_sems, recv_sems, a, 0, sibling).wait_send()
        for j, (dev, _) in enumerate(same_core):
            _remote(src, out.at[me_idx], send_sems, recv_sems, a, 1 + j, dev).wait_send()
        pltpu.make_async_copy(src, out.at[me_idx], local_sems.at[a]).wait()
    for fwd in forwards:
        fwd.wait_send()


def _exchange_start(*refs, scatter):
    if not scatter:
        return _gather_start(*refs)
    locals_, sends, _ = _exchange_copies(*refs, scatter=scatter, with_recvs=False)
    for cp in locals_ + sends:
        cp.start()


def _exchange_wait(*refs, scatter):
    if not scatter:
        return _gather_finish(*refs)
    locals_, sends, recvs = _exchange_copies(*refs, scatter=scatter, with_recvs=True)
    for cp in recvs:
        cp.wait_recv()
    for cp in sends:
        cp.wait_send()
    for cp in locals_:
        cp.wait()


def _exchange_sems(n_arrays):
    return [pltpu.SemaphoreType.DMA((n_arrays, N_DEV - 1)), pltpu.SemaphoreType.DMA((n_arrays, N_DEV - 1)),
            pltpu.SemaphoreType.DMA((n_arrays,))]


def _exchange_shapes(srcs, scatter):
    return [jax.ShapeDtypeStruct((N_DEV,) + tuple(s.shape[1:] if scatter else s.shape), s.dtype) for s in srcs]


def _call(body, *, name, grid, in_specs, out_specs, out_shape, args, scratch_shapes=(), carry=None):
    n_in, n_out, n_scr = len(in_specs), len(out_specs), len(scratch_shapes)
    srcs, scatter = carry if carry is not None else ((), False)
    nc = len(srcs)

    def wrapped(*refs):
        ins, refs = refs[:n_in], refs[n_in:]
        csrc, refs = refs[:nc], refs[nc:]
        outs, refs = refs[:n_out], refs[n_out:]
        cland, refs = refs[:nc], refs[nc:]
        scr, sems = refs[:n_scr], refs[n_scr:]
        if nc:
            @pl.when(pl.program_id(0) == 0)
            def _():
                _exchange_start(csrc, cland, *sems, scatter=scatter)

        body(*ins, *outs, *scr)
        if nc:
            @pl.when(pl.program_id(0) == grid[0] - 1)
            def _():
                _exchange_wait(csrc, cland, *sems, scatter=scatter)

    res = pl.pallas_call(
        wrapped, name=name, grid=grid,
        in_specs=list(in_specs) + [ANY] * nc, out_specs=list(out_specs) + [ANY] * nc,
        out_shape=list(out_shape) + _exchange_shapes(srcs, scatter),
        scratch_shapes=list(scratch_shapes) + (_exchange_sems(nc) if nc else []),
        compiler_params=_cparams(1),
    )(*args, *srcs)
    return res[:n_out], res[n_out:]


def _row_tile(tm, d):
    return pl.BlockSpec((tm, d), lambda i: (i, 0))


def _acc_row(d):
    return pl.BlockSpec((1, d), lambda i: (0, 0))


def _ffn_fwd(x, g, w1t, w3t, w2, name, carry=None):
    t = x.shape[0]
    tm = _tile(t)
    nf = D_FF // FFN_FT

    def body(x_ref, g_ref, w1_ref, w3_ref, w2_ref, o_ref):
        xv = x_ref[...]
        xhat, _ = _rms_parts(xv)
        n = (xhat * g_ref[...]).astype(BF16)
        o_ref[...] = xv

        def fstep(f, c):
            rows = pl.ds(pl.multiple_of(f * FFN_FT, FFN_FT), FFN_FT)
            a = _nt(n, w1_ref[rows, :])
            b = _nt(n, w3_ref[rows, :])
            s = (a * jax.nn.sigmoid(a) * b).astype(BF16)
            o_ref[...] += 0.5 * _nn(s, w2_ref[rows, :])
            return c

        lax.fori_loop(0, nf, fstep, 0, unroll=True)

    (h,), landed = _call(
        body, name=name, grid=(t // tm,),
        in_specs=[_row_tile(tm, D_MODEL), _acc_row(D_MODEL), VMEM_FULL, VMEM_FULL, VMEM_FULL],
        out_specs=[_row_tile(tm, D_MODEL)],
        out_shape=[jax.ShapeDtypeStruct((t, D_MODEL), F32)],
        args=(x, g, w1t, w3t, w2), carry=carry)
    return h, landed


def _ffn_bwd(x, dh, g, w1t, w3t, w2, name, carry=None):
    t = x.shape[0]
    tm = _tile(t) // 2
    nf = D_FF // FFN_FT

    def body(x_ref, dh_ref, g_ref, w1_ref, w3_ref, w2_ref,
             dx_ref, dg_ref, da_ref, db_ref, s_ref, n_ref, dhh_ref, dn_acc):
        i = pl.program_id(0)
        xv = x_ref[...]
        gv = g_ref[...]
        xhat, r = _rms_parts(xv)
        n = (xhat * gv).astype(BF16)
        n_ref[...] = n
        dhv = dh_ref[...]
        dhh = (0.5 * dhv).astype(BF16)
        dhh_ref[...] = dhh
        dn_acc[...] = jnp.zeros_like(dn_acc)

        def fstep(f, c):
            rows = pl.ds(pl.multiple_of(f * FFN_FT, FFN_FT), FFN_FT)
            w1c, w3c, w2c = w1_ref[rows, :], w3_ref[rows, :], w2_ref[rows, :]
            a = _nt(n, w1c)
            b = _nt(n, w3c)
            sg = jax.nn.sigmoid(a)
            sl = a * sg
            ds = _nt(dhh, w2c)
            da = (ds * b * sg * (1.0 + a * (1.0 - sg))).astype(BF16)
            db = (ds * sl).astype(BF16)
            s_ref[f] = (sl * b).astype(BF16)
            da_ref[f] = da
            db_ref[f] = db
            dn_acc[...] += _nn(da, w1c) + _nn(db, w3c)
            return c

        lax.fori_loop(0, nf, fstep, 0, unroll=True)
        dx, dg = _rms_bwd(dn_acc[...], gv, xhat, r)
        dx_ref[...] = dhv + dx

        @pl.when(i == 0)
        def _():
            dg_ref[...] = jnp.zeros_like(dg_ref)

        dg_ref[...] += dg

    blk3 = pl.BlockSpec((nf, tm, FFN_FT), lambda i: (0, i, 0))
    sh3 = jax.ShapeDtypeStruct((nf, t, FFN_FT), BF16)
    return _call(
        body, name=name, grid=(t // tm,),
        in_specs=[_row_tile(tm, D_MODEL), _row_tile(tm, D_MODEL), _acc_row(D_MODEL), VMEM_FULL, VMEM_FULL, VMEM_FULL],
        out_specs=[_row_tile(tm, D_MODEL), _acc_row(D_MODEL), blk3, blk3, blk3,
                   _row_tile(tm, D_MODEL), _row_tile(tm, D_MODEL)],
        out_shape=[jax.ShapeDtypeStruct((t, D_MODEL), F32), jax.ShapeDtypeStruct((1, D_MODEL), F32), sh3, sh3, sh3,
                   jax.ShapeDtypeStruct((t, D_MODEL), BF16), jax.ShapeDtypeStruct((t, D_MODEL), BF16)],
        scratch_shapes=[pltpu.VMEM((tm, D_MODEL), F32)],
        args=(x, dh, g, w1t, w3t, w2), carry=carry)


def _mm_tn(a, b, name, carry=None):
    t, n = b.shape
    kc = min(512, t)
    if a.ndim == 3:
        nb, _, tb = a.shape
        a_spec = pl.BlockSpec((1, t, tb), lambda i: (i, 0, 0))
    else:
        m = a.shape[1]
        tb = min(m, 256)
        nb = m // tb
        a_spec = pl.BlockSpec((t, tb), lambda i: (0, i))
    three_d = a.ndim == 3

    def body(a_ref, b_ref, o_ref, acc):
        acc[...] = jnp.zeros_like(acc)

        def kstep(k, c):
            rows = pl.ds(pl.multiple_of(k * kc, kc), kc)
            av = a_ref[0, rows, :] if three_d else a_ref[rows, :]
            acc[...] += _tn(av.astype(BF16), b_ref[rows, :])
            return c

        lax.fori_loop(0, t // kc, kstep, 0, unroll=True)
        o_ref[...] = acc[...].astype(BF16)

    (out,), landed = _call(
        body, name=name, grid=(nb,),
        in_specs=[a_spec, VMEM_FULL],
        out_specs=[pl.BlockSpec((tb, n), lambda i: (i, 0))],
        out_shape=[jax.ShapeDtypeStruct((nb * tb, n), BF16)],
        scratch_shapes=[pltpu.VMEM((tb, n), F32)],
        args=(a, b), carry=carry)
    return (out, landed) if carry is not None else out


def _mix_pre_fwd(h, g, wint, carry=None):
    t = h.shape[0]
    tm = _tile(t)

    def body(h_ref, g_ref, w_ref, u_ref, *outs):
        xhat, _ = _rms_parts(h_ref[...])
        u = (xhat * g_ref[...]).astype(BF16)
        u_ref[...] = u
        for o_ref, off, size in zip(outs, IN_OFFS, IN_SIZES):
            o_ref[...] = _nt(u, w_ref[off:off + size, :])

    return _call(
        body, name="mix_pre_fwd", grid=(t // tm,),
        in_specs=[_row_tile(tm, D_MODEL), _acc_row(D_MODEL), VMEM_FULL],
        out_specs=[_row_tile(tm, D_MODEL)] + [_row_tile(tm, s) for s in IN_SIZES],
        out_shape=[jax.ShapeDtypeStruct((t, D_MODEL), BF16)] + [jax.ShapeDtypeStruct((t, s), F32) for s in IN_SIZES],
        args=(h, g, wint), carry=carry)


def _mix_pre_bwd(h, g, wint, dh2, dz):
    t = h.shape[0]
    tm = _tile(t)

    def body(h_ref, g_ref, w_ref, dh2_ref, *rest):
        dz_refs, (dh1_ref, dg_ref) = rest[:len(IN_SIZES)], rest[len(IN_SIZES):]
        i = pl.program_id(0)
        gv = g_ref[...]
        xhat, r = _rms_parts(h_ref[...])
        du = jnp.zeros((tm, D_MODEL), F32)
        for dz_ref, off, size in zip(dz_refs, IN_OFFS, IN_SIZES):
            du = du + _nn(dz_ref[...].astype(BF16), w_ref[off:off + size, :])
        dx, dg = _rms_bwd(du, gv, xhat, r)
        dh1_ref[...] = dh2_ref[...] + dx

        @pl.when(i == 0)
        def _():
            dg_ref[...] = jnp.zeros_like(dg_ref)

        dg_ref[...] += dg

    return pl.pallas_call(
        body, name="mix_pre_bwd", grid=(t // tm,),
        in_specs=[_row_tile(tm, D_MODEL), _acc_row(D_MODEL), VMEM_FULL, _row_tile(tm, D_MODEL)]
        + [_row_tile(tm, s) for s in IN_SIZES],
        out_specs=[_row_tile(tm, D_MODEL), _acc_row(D_MODEL)],
        out_shape=[jax.ShapeDtypeStruct((t, D_MODEL), F32), jax.ShapeDtypeStruct((1, D_MODEL), F32)],
        compiler_params=_cparams(1),
    )(h, g, wint, dh2, *dz)


def _disc_math(lre, lim, ldt, bre, bim):
    dt = jnp.exp(ldt)
    mag = jnp.exp(lre * dt)
    ar = mag * jnp.cos(lim * dt)
    ai = mag * jnp.sin(lim * dt)
    den = lre * lre + lim * lim
    nr = ar - 1.0
    fr = (nr * lre + ai * lim) / den
    fi = (ai * lre - nr * lim) / den
    fr, fi = fr[:, None, :], fi[:, None, :]
    return ar, ai, fr * bre - fi * bim, fr * bim + fi * bre


def _s5_disc(lre, lim, ldt, bre, bim):
    def body(lre_ref, lim_ref, ldt_ref, bre_ref, bim_ref, ar_ref, ai_ref, bbr_ref, bbi_ref):
        ar, ai, bbr, bbi = _disc_math(lre_ref[...], lim_ref[...], ldt_ref[...], bre_ref[...], bim_ref[...])
        ar_ref[...] = ar
        ai_ref[...] = ai
        bbr_ref[...] = bbr
        bbi_ref[...] = bbi

    small = jax.ShapeDtypeStruct(lre.shape, F32)
    big = jax.ShapeDtypeStruct(bre.shape, F32)
    return pl.pallas_call(body, name="s5_disc", out_shape=[small, small, big, big],
                          in_specs=[VMEM_FULL] * 5, out_specs=[VMEM_FULL] * 4)(lre, lim, ldt, bre, bim)


def _s5_disc_bwd(lre, lim, ldt, bre, bim, dar, dai, dbbr, dbbi):
    def body(lre_ref, lim_ref, ldt_ref, bre_ref, bim_ref, dar_ref, dai_ref, dbbr_ref, dbbi_ref,
             glre_ref, glim_ref, gldt_ref, gbre_ref, gbim_ref):
        _, vjp = jax.vjp(_disc_math, lre_ref[...], lim_ref[...], ldt_ref[...], bre_ref[...], bim_ref[...])
        glre, glim, gldt, gbre, gbim = vjp((dar_ref[...], dai_ref[...], dbbr_ref[...], dbbi_ref[...]))
        glre_ref[...] = glre
        glim_ref[...] = glim
        gldt_ref[...] = gldt
        gbre_ref[...] = gbre
        gbim_ref[...] = gbim

    small = jax.ShapeDtypeStruct(lre.shape, F32)
    big = jax.ShapeDtypeStruct(bre.shape, F32)
    return pl.pallas_call(body, name="s5_disc_bwd",
                          out_shape=[small, small, jax.ShapeDtypeStruct(ldt.shape, F32), big, big],
                          in_specs=[VMEM_FULL] * 9, out_specs=[VMEM_FULL] * 5,
                          )(lre, lim, ldt, bre, bim, dar, dai, dbbr, dbbi)


def _cmul(ar, ai, br, bi):
    return ar * br - ai * bi, ar * bi + ai * br


def _cpow(ar, ai, n):
    rr, ri = None, None
    pr, pi = ar, ai
    while n:
        if n & 1:
            rr, ri = (pr, pi) if rr is None else _cmul(rr, ri, pr, pi)
        n >>= 1
        if n:
            pr, pi = _cmul(pr, pi, pr, pi)
    return rr, ri


def _shift_rows(v, down):
    row = lax.broadcasted_iota(jnp.int32, v.shape, 0)
    if down:
        return jnp.where(row == 0, 0.0, pltpu.roll(v, 1, 0))
    return jnp.where(row == S5_SEGS - 1, 0.0, pltpu.roll(v, S5_SEGS - 1, 0))


def _chain_segments(er, ei, pr, pi, down):
    fr, fi = er, ei
    for _ in range(S5_SEGS - 1):
        sr, si = _shift_rows(fr, down), _shift_rows(fi, down)
        mr, mi = _cmul(pr, pi, sr, si)
        fr, fi = er + mr, ei + mi
    return _shift_rows(fr, down), _shift_rows(fi, down)


def _rows_to_scan_order(src_ref, dst_ref, t):
    ls = t // S5_SEGS

    def tile(j, c):
        dst_ref[pl.ds(pl.multiple_of(j * S5_SEGS, S5_SEGS), S5_SEGS), :] = src_ref[pl.ds(j, S5_SEGS, stride=ls), :]
        return c

    lax.fori_loop(0, ls, tile, 0, unroll=8)


def _rows_from_scan_order(src_ref, dst_ref, t):
    ls = t // S5_SEGS
    for s in range(S5_SEGS):
        def tile(jb, c, s=s):
            dst_ref[pl.ds(pl.multiple_of(s * ls + jb * 8, 8), 8), :] = (
                src_ref[pl.ds(jb * 8 * S5_SEGS + s, 8, stride=S5_SEGS), :])
            return c

        lax.fori_loop(0, ls // 8, tile, 0, unroll=8)


def _s5_fwd(ug, bd, ctd, ar4, ai4, dskip, carry=None):
    t = ug.shape[0]
    ls = t // S5_SEGS
    rc = min(512, t)
    ns = S5_BSTATE

    def body(ugn_ref, bd_ref, ct_ref, ar_ref, ai_ref, d_ref, xs_hbm, yn_ref, buf, ug_ref, y_ref, sem):
        cb = pl.program_id(0)
        bdv = bd_ref[0]
        _rows_to_scan_order(ugn_ref, ug_ref, t)

        def mm(i, c):
            rows = pl.ds(pl.multiple_of(i * rc, rc), rc)
            buf[rows, :] = _nn(ug_ref[rows, :].astype(BF16), bdv)
            return c

        lax.fori_loop(0, t // rc, mm, 0, unroll=True)
        arb = jnp.broadcast_to(ar_ref[0], (S5_SEGS, ns))
        aib = jnp.broadcast_to(ai_ref[0], (S5_SEGS, ns))

        def step(j, c, store):
            sr, si = c
            rows = pl.ds(pl.multiple_of(j * S5_SEGS, S5_SEGS), S5_SEGS)
            nr = arb * sr - aib * si + buf[rows, 0:ns]
            ni = arb * si + aib * sr + buf[rows, ns:2 * ns]
            if store:
                buf[rows, 0:ns] = nr
                buf[rows, ns:2 * ns] = ni
            return nr, ni

        zero = jnp.zeros((S5_SEGS, ns), F32)
        er, ei = lax.fori_loop(0, ls, functools.partial(step, store=False), (zero, zero))
        pr, pi = _cpow(arb, aib, ls)
        init = _chain_segments(er, ei, pr, pi, down=True)
        lax.fori_loop(0, ls, functools.partial(step, store=True), init)

        out = pltpu.make_async_copy(buf, xs_hbm.at[cb], sem)
        out.start()
        ctv = ct_ref[0]
        dv = d_ref[...]

        def ymm(i, c):
            rows = pl.ds(pl.multiple_of(i * rc, rc), rc)
            y_ref[rows, :] = _nn(buf[rows, :].astype(BF16), ctv) + dv * ug_ref[rows, :]
            return c

        lax.fori_loop(0, t // rc, ymm, 0, unroll=True)
        _rows_from_scan_order(y_ref, yn_ref, t)
        out.wait()

    return _call(
        body, name="s5_fwd", grid=(S5_BLOCKS,),
        in_specs=[pl.BlockSpec((t, 128), lambda i: (0, i)),
                  pl.BlockSpec((1, 128, 2 * ns), lambda i: (i, 0, 0)),
                  pl.BlockSpec((1, 2 * ns, 128), lambda i: (i, 0, 0)),
                  pl.BlockSpec((1, 1, ns), lambda i: (i, 0, 0)),
                  pl.BlockSpec((1, 1, ns), lambda i: (i, 0, 0)),
                  pl.BlockSpec((1, 128), lambda i: (0, i))],
        out_specs=[ANY, pl.BlockSpec((t, 128), lambda i: (0, i))],
        out_shape=[jax.ShapeDtypeStruct((S5_BLOCKS, t, 2 * ns), F32), jax.ShapeDtypeStruct((t, S5_WIDTH), F32)],
        scratch_shapes=[pltpu.VMEM((t, 2 * ns), F32), pltpu.VMEM((t, 128), F32), pltpu.VMEM((t, 128), F32),
                        pltpu.SemaphoreType.DMA(())],
        args=(ug, bd, ctd, ar4, ai4, dskip), carry=carry)


def _s5_bwd(dy, ug, xs, cd, bdt, ar4, ai4, dskip, carry=None):
    t = ug.shape[0]
    ls = t // S5_SEGS
    rc = min(512, t)
    ns = S5_BSTATE

    def body(dyn_ref, ugn_ref, xs_hbm, cd_ref, bdt_ref, ar_ref, ai_ref, d_ref,
             dugn_ref, dbd_ref, dcd_ref, dd_ref, dar_ref, dai_ref, xbuf, lam, dy_ref, ug_ref, dug_ref, sem):
        cb = pl.program_id(0)
        load = pltpu.make_async_copy(xs_hbm.at[cb], xbuf, sem)
        load.start()
        cdv = cd_ref[0]
        _rows_to_scan_order(dyn_ref, dy_ref, t)
        _rows_to_scan_order(ugn_ref, ug_ref, t)

        def mm(i, c):
            rows = pl.ds(pl.multiple_of(i * rc, rc), rc)
            lam[rows, :] = _nn(dy_ref[rows, :].astype(BF16), cdv)
            return c

        lax.fori_loop(0, t // rc, mm, 0, unroll=True)
        arb = jnp.broadcast_to(ar_ref[0], (S5_SEGS, ns))
        aib = jnp.broadcast_to(ai_ref[0], (S5_SEGS, ns))

        def lam_step(j, lr, li):
            rows = pl.ds(pl.multiple_of(j * S5_SEGS, S5_SEGS), S5_SEGS)
            nr = arb * lr + aib * li + lam[rows, 0:ns]
            ni = arb * li - aib * lr + lam[rows, ns:2 * ns]
            return rows, nr, ni

        def pass1(jj, c):
            _, nr, ni = lam_step(ls - 1 - jj, *c)
            return nr, ni

        zero = jnp.zeros((S5_SEGS, ns), F32)
        er, ei = lax.fori_loop(0, ls, pass1, (zero, zero))
        pr, pi = _cpow(arb, aib, ls)
        init = _chain_segments(er, ei, pr, -pi, down=False)
        load.wait()

        def accumulate(acc, nr, ni, xpr, xpi):
            return acc[0] + nr * xpr + ni * xpi, acc[1] + ni * xpr - nr * xpi

        def pass2(jj, c):
            lr, li, accr, acci = c
            j = ls - 1 - jj
            rows, nr, ni = lam_step(j, lr, li)
            lam[rows, 0:ns] = nr
            lam[rows, ns:2 * ns] = ni
            prev = pl.ds(pl.multiple_of((j - 1) * S5_SEGS, S5_SEGS), S5_SEGS)
            accr, acci = accumulate((accr, acci), nr, ni, xbuf[prev, 0:ns], xbuf[prev, ns:2 * ns])
            return nr, ni, accr, acci

        lr, li, accr, acci = lax.fori_loop(0, ls - 1, pass2, (init[0], init[1], zero, zero))
        rows, nr, ni = lam_step(0, lr, li)
        lam[rows, 0:ns] = nr
        lam[rows, ns:2 * ns] = ni
        last = pl.ds((ls - 1) * S5_SEGS, S5_SEGS)
        accr, acci = accumulate((accr, acci), nr, ni,
                                _shift_rows(xbuf[last, 0:ns], True), _shift_rows(xbuf[last, ns:2 * ns], True))
        dar_ref[0] = jnp.sum(accr, axis=0, keepdims=True)
        dai_ref[0] = jnp.sum(acci, axis=0, keepdims=True)

        bdtv = bdt_ref[0]
        dv = d_ref[...]
        dbd_ref[...] = jnp.zeros_like(dbd_ref)
        dcd_ref[...] = jnp.zeros_like(dcd_ref)
        dd_ref[...] = jnp.zeros_like(dd_ref)

        def tail(i, c):
            rows = pl.ds(pl.multiple_of(i * rc, rc), rc)
            dy = dy_ref[rows, :]
            ug = ug_ref[rows, :]
            lb = lam[rows, :].astype(BF16)
            dug_ref[rows, :] = _nn(lb, bdtv) + dv * dy
            dbd_ref[0] += _tn(ug.astype(BF16), lb)
            dcd_ref[0] += _tn(dy.astype(BF16), xbuf[rows, :].astype(BF16))
            dd_ref[...] += jnp.sum(dy * ug, axis=0, keepdims=True)
            return c

        lax.fori_loop(0, t // rc, tail, 0, unroll=True)
        _rows_from_scan_order(dug_ref, dugn_ref, t)

    chan = pl.BlockSpec((t, 128), lambda i: (0, i))
    dense = pl.BlockSpec((1, 128, 2 * ns), lambda i: (i, 0, 0))
    vec = pl.BlockSpec((1, 1, ns), lambda i: (i, 0, 0))
    return _call(
        body, name="s5_bwd", grid=(S5_BLOCKS,),
        in_specs=[chan, chan, ANY, dense, pl.BlockSpec((1, 2 * ns, 128), lambda i: (i, 0, 0)), vec, vec,
                  pl.BlockSpec((1, 128), lambda i: (0, i))],
        out_specs=[chan, dense, dense, pl.BlockSpec((1, 128), lambda i: (0, i)), vec, vec],
        out_shape=[jax.ShapeDtypeStruct((t, S5_WIDTH), F32),
                   jax.ShapeDtypeStruct((S5_BLOCKS, 128, 2 * ns), F32),
                   jax.ShapeDtypeStruct((S5_BLOCKS, 128, 2 * ns), F32),
                   jax.ShapeDtypeStruct((1, S5_WIDTH), F32),
                   jax.ShapeDtypeStruct((S5_BLOCKS, 1, ns), F32),
                   jax.ShapeDtypeStruct((S5_BLOCKS, 1, ns), F32)],
        scratch_shapes=[pltpu.VMEM((t, 2 * ns), F32), pltpu.VMEM((t, 2 * ns), F32)]
        + [pltpu.VMEM((t, 128), F32)] * 3 + [pltpu.SemaphoreType.DMA(())],
        args=(dy, ug, xs, cd, bdt, ar4, ai4, dskip), carry=carry)


def _cumsum_rows(x, reverse):
    c = x.shape[0]
    row = lax.broadcasted_iota(jnp.int32, x.shape, 0)
    d = 1
    while d < c:
        if reverse:
            x = x + jnp.where(row < c - d, pltpu.roll(x, c - d, 0), 0.0)
        else:
            x = x + jnp.where(row >= d, pltpu.roll(x, d, 0), 0.0)
        d *= 2
    return x


def _gla_common(q, k, alow, wup, bup):
    c = GLA_CHUNK
    pre = _nn(alow.astype(BF16), wup.astype(BF16)) + bup
    la = (jnp.minimum(pre, 0.0) - jnp.log(1.0 + jnp.exp(-jnp.abs(pre)))) * (1.0 / GLA_TAU)
    rr = lax.broadcasted_iota(jnp.int32, (c, c), 0)
    cc = lax.broadcasted_iota(jnp.int32, (c, c), 1)
    tril = (rr >= cc).astype(F32)
    bc = _cumsum_rows(la, reverse=False)
    bl = bc[c - 1:c, :]
    e_pos = jnp.exp(bc)
    e_neg = jnp.exp(-bc)
    e_end = jnp.exp(bl - bc)
    qt = q * (GLA_DK ** -0.5) * e_pos
    kt = k * e_neg
    ke = k * e_end
    lane = lax.broadcasted_iota(jnp.int32, (1, GLA_KEY), 1)
    masks = [((lane >= h * GLA_DK) & (lane < (h + 1) * GLA_DK)).astype(F32) for h in range(GLA_HEADS)]
    return dict(pre=pre, tril=tril, bc=bc, bl=bl, e_pos=e_pos, e_neg=e_neg, e_end=e_end,
                qt=qt, kt=kt, ke=ke, dec=jnp.exp(bl), masks=masks)


def _gla_fwd(q, k, v, alow, wup, bup, carry=None):
    t = q.shape[0]
    c = GLA_CHUNK
    n = t // c
    step = GLA_STEP_CHUNKS * c

    def body(q_ref, k_ref, v_ref, al_ref, wup_ref, bup_ref, o_ref, ss_ref, s_ref):
        i = pl.program_id(0)

        @pl.when(i == 0)
        def _():
            s_ref[...] = jnp.zeros_like(s_ref)

        wup_v, bup_v = wup_ref[...], bup_ref[...]
        s = s_ref[...]
        for j in range(GLA_STEP_CHUNKS):
            tok = slice(j * c, (j + 1) * c)
            m = _gla_common(q_ref[tok, :], k_ref[tok, :], al_ref[tok, :], wup_v, bup_v)
            ss_ref[j] = s
            sb = s.astype(BF16)
            ktb = m["kt"].astype(BF16)
            update = jnp.zeros_like(s)
            for h in range(GLA_HEADS):
                mask = m["masks"][h]
                qm = (m["qt"] * mask).astype(BF16)
                vh = v_ref[tok, h * GLA_DV:(h + 1) * GLA_DV].astype(BF16)
                p = (m["tril"] * _nt(qm, ktb)).astype(BF16)
                o_ref[tok, h * GLA_DV:(h + 1) * GLA_DV] = _nn(p, vh) + _nt(qm, sb)
                update = update + _tn(vh, (m["ke"] * mask).astype(BF16))
            s = m["dec"] * s + update
        s_ref[...] = s

    return _call(
        body, name="gla_fwd", grid=(t // step,),
        in_specs=[_row_tile(step, GLA_KEY), _row_tile(step, GLA_KEY), _row_tile(step, GLA_VAL),
                  _row_tile(step, GLA_RANK), VMEM_FULL, VMEM_FULL],
        out_specs=[_row_tile(step, GLA_VAL), pl.BlockSpec((GLA_STEP_CHUNKS, GLA_DV, GLA_KEY), lambda i: (i, 0, 0))],
        out_shape=[jax.ShapeDtypeStruct((t, GLA_VAL), F32), jax.ShapeDtypeStruct((n, GLA_DV, GLA_KEY), F32)],
        scratch_shapes=[pltpu.VMEM((GLA_DV, GLA_KEY), F32)],
        args=(q, k, v, alow, wup, bup), carry=carry)


def _gla_bwd(q, k, v, alow, wup, bup, ssave, do, carry=None):
    t = q.shape[0]
    c = GLA_CHUNK
    n = t // c

    def body(q_ref, k_ref, v_ref, al_ref, wup_ref, bup_ref, ss_ref, do_ref,
             dq_ref, dk_ref, dv_ref, dal_ref, dwup_ref, dbup_ref, ds_ref):
        i = pl.program_id(0)

        @pl.when(i == 0)
        def _():
            ds_ref[...] = jnp.zeros_like(ds_ref)
            dwup_ref[...] = jnp.zeros_like(dwup_ref)
            dbup_ref[...] = jnp.zeros_like(dbup_ref)

        wup_v, bup_v = wup_ref[...], bup_ref[...]
        ds_in = ds_ref[...]
        dwup = jnp.zeros((GLA_RANK, GLA_KEY), F32)
        dbup = jnp.zeros((1, GLA_KEY), F32)
        for j in reversed(range(GLA_STEP_CHUNKS)):
            tok = slice(j * c, (j + 1) * c)
            alow_v = al_ref[tok, :]
            m = _gla_common(q_ref[tok, :], k_ref[tok, :], alow_v, wup_v, bup_v)
            s = ss_ref[j]
            sb = s.astype(BF16)
            dsb = ds_in.astype(BF16)
            qt, kt, ke = m["qt"], m["kt"], m["ke"]
            ktb = kt.astype(BF16)
            dqt = jnp.zeros((c, GLA_KEY), F32)
            dkt = jnp.zeros((c, GLA_KEY), F32)
            dke = jnp.zeros((c, GLA_KEY), F32)
            update = jnp.zeros_like(ds_in)
            for h in range(GLA_HEADS):
                mask = m["masks"][h]
                qm = (qt * mask).astype(BF16)
                km = (kt * mask).astype(BF16)
                kem = (ke * mask).astype(BF16)
                cols = slice(h * GLA_DV, (h + 1) * GLA_DV)
                vh = v_ref[tok, cols].astype(BF16)
                doh = do_ref[tok, cols].astype(BF16)
                p = (m["tril"] * _nt(qm, ktb)).astype(BF16)
                dp = (m["tril"] * _nt(doh, vh)).astype(BF16)
                dv_ref[tok, cols] = _tn(p, doh) + _nt(kem, dsb)
                dqt = dqt + _nn(dp, km) + _nn(doh, sb) * mask
                dkt = dkt + _tn(dp, qm)
                dke = dke + _nn(vh, dsb) * mask
                update = update + _tn(doh, qm)
            ddec = jnp.sum(ds_in * s, axis=0, keepdims=True)
            dq_ref[tok, :] = dqt * m["e_pos"] * (GLA_DK ** -0.5)
            dk_ref[tok, :] = dkt * m["e_neg"] + dke * m["e_end"]
            dkeke = dke * ke
            dbl = jnp.sum(dkeke, axis=0, keepdims=True) + ddec * m["dec"]
            last = (lax.broadcasted_iota(jnp.int32, (c, 1), 0) == c - 1).astype(F32)
            dla = _cumsum_rows(dqt * qt - dkt * kt - dkeke + last * dbl, reverse=True)
            dpre = dla * (1.0 / GLA_TAU) * jax.nn.sigmoid(-m["pre"])
            dpb = dpre.astype(BF16)
            dal_ref[tok, :] = _nt(dpb, wup_v.astype(BF16))
            dwup = dwup + _tn(alow_v.astype(BF16), dpb)
            dbup = dbup + jnp.sum(dpre, axis=0, keepdims=True)
            ds_in = m["dec"] * ds_in + update
        ds_ref[...] = ds_in
        dwup_ref[...] += dwup
        dbup_ref[...] += dbup

    step = GLA_STEP_CHUNKS * c
    nsteps = t // step

    def rev(d):
        return pl.BlockSpec((step, d), lambda i: (nsteps - 1 - i, 0))

    return _call(
        body, name="gla_bwd", grid=(nsteps,),
        in_specs=[rev(GLA_KEY), rev(GLA_KEY), rev(GLA_VAL), rev(GLA_RANK), VMEM_FULL, VMEM_FULL,
                  pl.BlockSpec((GLA_STEP_CHUNKS, GLA_DV, GLA_KEY), lambda i: (nsteps - 1 - i, 0, 0)), rev(GLA_VAL)],
        out_specs=[rev(GLA_KEY), rev(GLA_KEY), rev(GLA_VAL), rev(GLA_RANK),
                   pl.BlockSpec((GLA_RANK, GLA_KEY), lambda i: (0, 0)), _acc_row(GLA_KEY)],
        out_shape=[jax.ShapeDtypeStruct((t, GLA_KEY), F32), jax.ShapeDtypeStruct((t, GLA_KEY), F32),
                   jax.ShapeDtypeStruct((t, GLA_VAL), F32), jax.ShapeDtypeStruct((t, GLA_RANK), F32),
                   jax.ShapeDtypeStruct((GLA_RANK, GLA_KEY), F32), jax.ShapeDtypeStruct((1, GLA_KEY), F32)],
        scratch_shapes=[pltpu.VMEM((GLA_DV, GLA_KEY), F32)],
        args=(q, k, v, alow, wup, bup, ssave, do), carry=carry)


def _post_math(y, o, r, gs5, ggla, wg, bg, gn, ps5t, pglat):
    y2 = y * y
    th = jnp.tanh(GELU_C0 * (y + GELU_C1 * y * y2))
    z5 = 0.5 * y * (1.0 + th)
    z5b = z5.astype(BF16)
    gate = jax.nn.sigmoid(_nn(z5b, wg) + bg)
    ys5 = z5 * gate
    rs, on = [], []
    for h in range(GLA_HEADS):
        oh = o[:, h * GLA_DV:(h + 1) * GLA_DV]
        rh = lax.rsqrt(jnp.mean(oh * oh, axis=-1, keepdims=True) + EPS)
        rs.append(rh)
        on.append(oh * rh)
    on = jnp.concatenate(on, axis=-1)
    sr = jax.nn.sigmoid(r)
    silu_r = r * sr
    ygla = on * gn * silu_r
    ys5b, yglab = ys5.astype(BF16), ygla.astype(BF16)
    m5 = _nt(ys5b, ps5t)
    mg = _nt(yglab, pglat)
    s5g, glag = jax.nn.sigmoid(gs5), jax.nn.sigmoid(ggla)
    merged = s5g * m5 + glag * mg
    return dict(y2=y2, th=th, z5=z5, z5b=z5b, gate=gate, ys5b=ys5b, yglab=yglab, rs=rs, on=on, sr=sr,
                silu_r=silu_r, m5=m5, mg=mg, s5g=s5g, glag=glag, mergedb=merged.astype(BF16))


def _mix_post_fwd(y, o, r, gs5, ggla, h1, wg, bg, gn, ps5t, pglat, wout, carry=None):
    t = o.shape[0]
    tm = _tile(t)

    def body(y_ref, o_ref, r_ref, gs5_ref, ggla_ref, h1_ref, wg_ref, bg_ref, gn_ref, ps_ref, pg_ref, wo_ref, h2_ref):
        m = _post_math(y_ref[...], o_ref[...], r_ref[...], gs5_ref[...], ggla_ref[...],
                       wg_ref[...], bg_ref[...], gn_ref[...], ps_ref[...], pg_ref[...])
        h2_ref[...] = h1_ref[...] + _nn(m["mergedb"], wo_ref[...])

    (h2,), landed = _call(
        body, name="mix_post_fwd", grid=(t // tm,),
        in_specs=[_row_tile(tm, 512)] * 3 + [_row_tile(tm, D_MODEL)] * 3
        + [VMEM_FULL, _acc_row(512), _acc_row(512), VMEM_FULL, VMEM_FULL, VMEM_FULL],
        out_specs=[_row_tile(tm, D_MODEL)],
        out_shape=[jax.ShapeDtypeStruct((t, D_MODEL), F32)],
        args=(y, o, r, gs5, ggla, h1, wg, bg, gn, ps5t, pglat, wout), carry=carry)
    return h2, landed


def _mix_post_bwd(y, o, r, gs5, ggla, dh2, wg, bg, gn, ps5t, pglat, wout, carry=None):
    t = o.shape[0]
    tm = _tile(t) // 2

    def body(y_ref, o_ref, r_ref, gs5_ref, ggla_ref, dh2_ref, wg_ref, bg_ref, gn_ref, ps_ref, pg_ref, wo_ref,
             dy_ref, do_ref, dr_ref, dgs5_ref, dggla_ref, dbg_ref, dgn_ref,
             z5b_ref, dgp_ref, ys5b_ref, dm5b_ref, yglab_ref, dmgb_ref, mergedb_ref, dh2b_ref):
        i = pl.program_id(0)
        yv, ov, rv = y_ref[...], o_ref[...], r_ref[...]
        wg, gn, ps5t, pglat = wg_ref[...], gn_ref[...], ps_ref[...], pg_ref[...]
        m = _post_math(yv, ov, rv, gs5_ref[...], ggla_ref[...], wg, bg_ref[...], gn, ps5t, pglat)
        dh2b = dh2_ref[...].astype(BF16)
        dmerged = _nt(dh2b, wo_ref[...])
        s5g, glag = m["s5g"], m["glag"]
        dgs5_ref[...] = dmerged * m["m5"] * s5g * (1.0 - s5g)
        dggla_ref[...] = dmerged * m["mg"] * glag * (1.0 - glag)
        dm5b = (dmerged * s5g).astype(BF16)
        dmgb = (dmerged * glag).astype(BF16)
        dys5 = _nn(dm5b, ps5t)
        dygla = _nn(dmgb, pglat)
        gate, z5, th = m["gate"], m["z5"], m["th"]
        dgpre = dys5 * z5 * gate * (1.0 - gate)
        dgpb = dgpre.astype(BF16)
        dz5 = dys5 * gate + _nt(dgpb, wg)
        dgelu = 0.5 * (1.0 + th) + 0.5 * yv * (1.0 - th * th) * GELU_C0 * (1.0 + 3.0 * GELU_C1 * m["y2"])
        dy_ref[...] = dz5 * dgelu
        on, sr, silu_r = m["on"], m["sr"], m["silu_r"]
        dr_ref[...] = dygla * on * gn * sr * (1.0 + rv * (1.0 - sr))
        dgn = jnp.sum(dygla * on * silu_r, axis=0, keepdims=True)
        don = dygla * gn * silu_r
        for h in range(GLA_HEADS):
            cols = slice(h * GLA_DV, (h + 1) * GLA_DV)
            donh, onh = don[:, cols], on[:, cols]
            do_ref[:, cols] = m["rs"][h] * (donh - onh * jnp.mean(donh * onh, axis=-1, keepdims=True))

        @pl.when(i == 0)
        def _():
            dbg_ref[...] = jnp.zeros_like(dbg_ref)
            dgn_ref[...] = jnp.zeros_like(dgn_ref)

        dbg_ref[...] += jnp.sum(dgpre, axis=0, keepdims=True)
        dgn_ref[...] += dgn
        z5b_ref[...] = m["z5b"]
        dgp_ref[...] = dgpb
        ys5b_ref[...] = m["ys5b"]
        dm5b_ref[...] = dm5b
        yglab_ref[...] = m["yglab"]
        dmgb_ref[...] = dmgb
        mergedb_ref[...] = m["mergedb"]
        dh2b_ref[...] = dh2b

    def f32(d):
        return jax.ShapeDtypeStruct((t, d), F32)

    def b16(d):
        return jax.ShapeDtypeStruct((t, d), BF16)

    widths = (512, 512, 512, 1024, 512, 1024, 1024, 1024)
    return _call(
        body, name="mix_post_bwd", grid=(t // tm,),
        in_specs=[_row_tile(tm, 512)] * 3 + [_row_tile(tm, D_MODEL)] * 3
        + [VMEM_FULL, _acc_row(512), _acc_row(512), VMEM_FULL, VMEM_FULL, VMEM_FULL],
        out_specs=[_row_tile(tm, 512)] * 3 + [_row_tile(tm, D_MODEL)] * 2
        + [_acc_row(512)] * 2 + [_row_tile(tm, w) for w in widths],
        out_shape=[f32(512)] * 3 + [f32(D_MODEL)] * 2
        + [jax.ShapeDtypeStruct((1, 512), F32)] * 2
        + [b16(w) for w in widths],
        args=(y, o, r, gs5, ggla, dh2, wg, bg, gn, ps5t, pglat, wout), carry=carry)


def _head(h3, g, target):
    t = h3.shape[0]
    tm = _tile(t)

    def body(h_ref, g_ref, t_ref, loss_ref, dh_ref, dg_ref):
        i = pl.program_id(0)
        gv = g_ref[...]
        xhat, r = _rms_parts(h_ref[...])
        err = xhat * gv - t_ref[...]
        dx, dg = _rms_bwd(err * (1.0 / D_MODEL), gv, xhat, r)
        dh_ref[...] = dx

        @pl.when(i == 0)
        def _():
            loss_ref[...] = jnp.zeros_like(loss_ref)
            dg_ref[...] = jnp.zeros_like(dg_ref)

        loss_ref[...] += (0.5 / D_MODEL) * jnp.sum(jnp.sum(err * err, axis=1, keepdims=True), axis=0, keepdims=True)
        dg_ref[...] += dg

    return pl.pallas_call(
        body, name="head", grid=(t // tm,),
        in_specs=[_row_tile(tm, D_MODEL), _acc_row(D_MODEL), _row_tile(tm, D_MODEL)],
        out_specs=[pl.BlockSpec((1, 1), lambda i: (0, 0)), _row_tile(tm, D_MODEL), _acc_row(D_MODEL)],
        out_shape=[jax.ShapeDtypeStruct((1, 1), F32), jax.ShapeDtypeStruct((t, D_MODEL), F32),
                   jax.ShapeDtypeStruct((1, D_MODEL), F32)],
        compiler_params=_cparams(1),
    )(h3, g, target)


ADAM_TILE_ELEMS = 256 * 1024


def _adamw(w, g, m, v, name):
    rows, cols = w.shape
    tr = rows
    while tr * cols > ADAM_TILE_ELEMS and tr % 16 == 0:
        tr //= 2

    spec = pl.BlockSpec((tr, cols), lambda i: (i, 0))
    sh = jax.ShapeDtypeStruct((rows, cols), F32)
    return pl.pallas_call(functools.partial(_adamw_body), name=name, grid=(rows // tr,), in_specs=[spec] * 4,
                          out_specs=[spec] * 3, out_shape=[sh] * 3, compiler_params=_cparams(1))(w, g, m, v)


def _adamw_body(w_ref, g_ref, m_ref, v_ref, d_ref, nm_ref, nv_ref):
    gv = g_ref[...]
    nm = ADAM_B1 * m_ref[...] + (1.0 - ADAM_B1) * gv
    nv = ADAM_B2 * v_ref[...] + (1.0 - ADAM_B2) * (gv * gv)
    m_hat = nm / (1.0 - ADAM_B1 ** ADAM_STEP)
    v_hat = nv / (1.0 - ADAM_B2 ** ADAM_STEP)
    d_ref[...] = -ADAM_LR * (m_hat / (jnp.sqrt(v_hat) + ADAM_EPS) + ADAM_WD * w_ref[...])
    nm_ref[...] = nm
    nv_ref[...] = nv


def _adamw_many(ws, gs, ms, vs, name):
    n = len(ws)

    def body(*refs):
        ins, outs = refs[:4 * n], refs[4 * n:]
        for i in range(n):
            _adamw_body(*(ins[j * n + i] for j in range(4)), *(outs[j * n + i] for j in range(3)))

    shapes = [jax.ShapeDtypeStruct(w.shape, F32) for w in ws]
    res = pl.pallas_call(body, name=name, in_specs=[VMEM_FULL] * (4 * n), out_specs=[VMEM_FULL] * (3 * n),
                         out_shape=shapes * 3)(*ws, *gs, *ms, *vs)
    return res[:n], res[n:2 * n], res[2 * n:]


def _exchange(srcs, scatter, name):
    n = len(srcs)

    def body(*refs):
        _exchange_start(refs[:n], refs[n:2 * n], *refs[2 * n:], scatter=scatter)
        _exchange_wait(refs[:n], refs[n:2 * n], *refs[2 * n:], scatter=scatter)

    return pl.pallas_call(
        body, name=name, in_specs=[ANY] * n, out_specs=[ANY] * n,
        out_shape=_exchange_shapes(srcs, scatter), scratch_shapes=_exchange_sems(n),
    )(*srcs)


def _sum_slabs(slabs, name):
    n = slabs.shape[0]

    def body(s_ref, o_ref):
        acc = s_ref[0].astype(F32)
        for s in range(1, n):
            acc = acc + s_ref[s].astype(F32)
        o_ref[...] = acc

    return pl.pallas_call(
        body, name=name, in_specs=[VMEM_FULL], out_specs=VMEM_FULL,
        out_shape=jax.ShapeDtypeStruct(slabs.shape[1:], F32),
        compiler_params=pltpu.CompilerParams(vmem_limit_bytes=VMEM_LIMIT_BYTES),
    )(slabs)


BIG = ("ffn1_w1", "ffn1_w3", "ffn1_w2", "w_in", "s5_glu_w", "gla_a_up_w", "proj_s5", "proj_gla", "w_out",
       "ffn2_w1", "ffn2_w3", "ffn2_w2")
GROUPS = (("ffn1_w1", "ffn1_w3", "ffn1_w2"),
          ("w_in", "s5_glu_w", "gla_a_up_w", "proj_s5", "proj_gla", "w_out"),
          ("ffn2_w1", "ffn2_w3", "ffn2_w2"))
W_IN_ROWS = 514
W_IN_PAD = 528
UP_COLS = 32
ROW_ADAM = ("ffn1_w1", "ffn1_w3", "w_in", "ffn2_w1", "ffn2_w3")
COL_SHARDED = ("ffn1_w1", "ffn1_w3", "w_in", "proj_s5", "proj_gla", "ffn2_w1", "ffn2_w3")

SMALL = ("ffn1_norm", "mix_norm", "s5_lambda_re", "s5_lambda_im", "s5_log_dt", "s5_b_re", "s5_b_im", "s5_c_re",
         "s5_c_im", "s5_d", "s5_glu_b", "gla_a_up_b", "gla_out_norm", "ffn2_norm", "final_norm")
SMALL_SHAPES = dict(ffn1_norm=(1, 1024), mix_norm=(1, 1024), s5_lambda_re=(1, 32, 64), s5_lambda_im=(1, 32, 64),
                    s5_log_dt=(1, 32), s5_b_re=(1, 32, 64, 16), s5_b_im=(1, 32, 64, 16), s5_c_re=(1, 32, 16, 64),
                    s5_c_im=(1, 32, 16, 64), s5_d=(1, 32, 16), s5_glu_b=(1, 512), gla_a_up_b=(1, 256),
                    gla_out_norm=(1, 512), ffn2_norm=(1, 1024), final_norm=(1024,))
SMALL_N = sum(math.prod(s) for s in SMALL_SHAPES.values())
SMALL_R = -(-SMALL_N // (64 * 1024)) * 64


def _shard_rows(name, a):
    if name == "gla_a_up_w":
        return jnp.pad(a, ((0, 0), (0, 128 - UP_COLS)))
    if name in COL_SHARDED:
        a = a.T
    if name == "w_in":
        return jnp.pad(a, ((0, W_IN_PAD - W_IN_ROWS), (0, 0)))
    return a.reshape(-1, 1024)


def _unshard_rows(name, rows, shape):
    if name == "gla_a_up_w":
        return rows[:, :UP_COLS]
    if name == "w_in":
        rows = rows[:W_IN_ROWS]
    if name in COL_SHARDED:
        return rows.reshape(shape[1], shape[0]).T
    return rows.reshape(shape)


def _pack_small(vals):
    flat = jnp.concatenate([vals[n].reshape(-1).astype(F32) for n in SMALL])
    return jnp.pad(flat, (0, SMALL_R * 1024 - SMALL_N)).reshape(SMALL_R, 1024)


S5_B = ("s5_b_re", "s5_b_im")


def _working(name, a):
    return a[0].transpose(0, 2, 1) if name in S5_B else a


def _declared(name, a):
    return a.transpose(0, 2, 1)[None] if name in S5_B else a.reshape(SMALL_SHAPES[name])


def _unpack_small(slab):
    flat = slab.reshape(-1)
    out, off = {}, 0
    for n in SMALL:
        size = math.prod(SMALL_SHAPES[n])
        shape = (S5_GROUPS, S5_GROUP, S5_STATE) if n in S5_B else SMALL_SHAPES[n]
        out[n] = flat[off:off + size].reshape(shape)
        off += size
    return out


FULL_SHAPES = dict(w_in=(IN_COLS, D_MODEL), s5_glu_w=(S5_WIDTH, S5_WIDTH), gla_a_up_w=(GLA_RANK, GLA_KEY),
                   proj_s5=(D_MODEL, S5_WIDTH), proj_gla=(D_MODEL, GLA_VAL), w_out=(D_MODEL, D_MODEL))


def _full_weight(name, gathered):
    if name == "gla_a_up_w":
        return gathered[:, :, :UP_COLS].transpose(1, 0, 2).reshape(GLA_RANK, GLA_KEY)
    if name == "w_in":
        gathered = gathered[:, :W_IN_ROWS]
    return gathered.reshape(FULL_SHAPES.get(name, (D_FF, D_MODEL)))


def _grad_slabs(name, g):
    if name == "gla_a_up_w":
        g = g.reshape(GLA_RANK, N_DEV, UP_COLS).transpose(1, 0, 2)
        return jnp.pad(g, ((0, 0), (0, 0), (0, 128 - UP_COLS))).astype(BF16)
    if name == "w_in":
        return jnp.pad(g.reshape(N_DEV, W_IN_ROWS, D_MODEL), ((0, 0), (0, W_IN_PAD - W_IN_ROWS), (0, 0)))
    return g.reshape(N_DEV, -1, 1024)


def _s5_dense(re, im, sign_im):
    eye = jnp.eye(8, dtype=F32)

    def one(a):
        a = a.reshape(S5_BLOCKS, 8, S5_GROUP, S5_STATE)
        return jnp.einsum("cghp,gk->cghkp", a, eye).reshape(S5_BLOCKS, 128, S5_BSTATE)

    return jnp.concatenate([one(re), sign_im * one(im)], axis=-1)


def _s5_undense(d):
    eye = jnp.eye(8, dtype=F32)

    def one(a):
        a = a.reshape(S5_BLOCKS, 8, S5_GROUP, 8, S5_STATE)
        return jnp.einsum("cghkp,gk->cghp", a, eye).reshape(S5_GROUPS, S5_GROUP, S5_STATE)

    return one(d[..., :S5_BSTATE]), one(d[..., S5_BSTATE:])


def _local_step(x, target, p, w, rows=None):
    w = dict(w or {})
    landed_grads = {}

    def gather(names):
        return None if rows is None else ([rows[n] for n in names], False)

    def gathered(names, landed):
        w.update({n: _full_weight(n, g) for n, g in zip(names, landed)})

    def scatter(names):
        return None if rows is None else ([_grad_slabs(n, big[n]) for n in names], True)

    def scattered(names, landed):
        landed_grads.update(zip(names, landed))

    if rows is not None:
        gathered(GROUPS[0], _exchange(gather(GROUPS[0])[0], False, "gather_ffn1"))
    g1, gm, g2 = p["ffn1_norm"], p["mix_norm"], p["ffn2_norm"]
    gf = p["final_norm"].reshape(1, D_MODEL)
    lre, lim = p["s5_lambda_re"][0], p["s5_lambda_im"][0]
    ldt = p["s5_log_dt"][0].reshape(S5_GROUPS, 1)
    bre = p["s5_b_re"][0].transpose(0, 2, 1)
    bim = p["s5_b_im"][0].transpose(0, 2, 1)
    cre, cim = p["s5_c_re"][0], p["s5_c_im"][0]
    dskip = p["s5_d"][0].reshape(1, S5_WIDTH)
    bg, bup, gn = p["s5_glu_b"], p["gla_a_up_b"], p["gla_out_norm"]

    mix_first, mix_rest = ("w_in", "gla_a_up_w"), ("s5_glu_w", "proj_s5", "proj_gla", "w_out")
    h1, got = _ffn_fwd(x, g1, w["ffn1_w1"], w["ffn1_w3"], w["ffn1_w2"], "ffn1_fwd", gather(mix_first))
    gathered(mix_first, got)
    wup = w["gla_a_up_w"].astype(F32)
    (u, s5in, q, k, v, r, alow, gs5, ggla), got = _mix_pre_fwd(h1, gm, w["w_in"], gather(mix_rest))
    gathered(mix_rest, got)
    ar, ai, bbr, bbi = _s5_disc(lre, lim, ldt, bre, bim)
    bd = _s5_dense(bbr, bbi, 1.0)
    cd = _s5_dense(cre, cim, -1.0)
    bd16, cd16 = bd.astype(BF16), cd.astype(BF16)
    bdt16, ctd16 = bd16.transpose(0, 2, 1), cd16.transpose(0, 2, 1)
    ar4 = ar.reshape(S5_BLOCKS, 1, S5_BSTATE)
    ai4 = ai.reshape(S5_BLOCKS, 1, S5_BSTATE)
    (xs, y), got = _s5_fwd(s5in, bd16, ctd16, ar4, ai4, dskip, gather(GROUPS[2][:1]))
    gathered(GROUPS[2][:1], got)
    (o, ssave), got = _gla_fwd(q, k, v, alow, wup, bup, gather(GROUPS[2][1:2]))
    gathered(GROUPS[2][1:2], got)
    post_w = (w["s5_glu_w"], bg, gn, w["proj_s5"], w["proj_gla"], w["w_out"])
    h2, got = _mix_post_fwd(y, o, r, gs5, ggla, h1, *post_w, carry=gather(GROUPS[2][2:]))
    gathered(GROUPS[2][2:], got)
    h3, _ = _ffn_fwd(h2, g2, w["ffn2_w1"], w["ffn2_w3"], w["ffn2_w2"], "ffn2_fwd")
    loss, dh3, dgf = _head(h3, gf, target)

    big, small = {}, {}
    small["final_norm"] = dgf.reshape(D_MODEL)
    (dh2, dg2, da3, db3, s3, n2, dhh2), _ = _ffn_bwd(
        h2, dh3, g2, w["ffn2_w1"], w["ffn2_w3"], w["ffn2_w2"], "ffn2_bwd")
    small["ffn2_norm"] = dg2
    big["ffn2_w1"] = _mm_tn(da3, n2, "ffn2_dw1")
    big["ffn2_w3"] = _mm_tn(db3, n2, "ffn2_dw3")
    big["ffn2_w2"] = _mm_tn(s3, dhh2, "ffn2_dw2")
    (dy, do, dr, dgs5, dggla, dbg, dgn, z5b, dgpb, ys5b, dm5b, yglab, dmgb, mergedb, dh2b), got = _mix_post_bwd(
        y, o, r, gs5, ggla, dh2, *post_w, carry=scatter(GROUPS[2][:1]))
    scattered(GROUPS[2][:1], got)
    small["s5_glu_b"] = dbg
    small["gla_out_norm"] = dgn
    big["s5_glu_w"] = _mm_tn(z5b, dgpb, "glu_dw")
    big["proj_s5"] = _mm_tn(dm5b, ys5b, "proj_s5_dw")
    big["proj_gla"] = _mm_tn(dmgb, yglab, "proj_gla_dw")
    big["w_out"] = _mm_tn(mergedb, dh2b, "w_out_dw")
    (dq, dk, dv, dalow, dwup, dbup), got = _gla_bwd(q, k, v, alow, wup, bup, ssave, do, scatter(GROUPS[2][1:2]))
    scattered(GROUPS[2][1:2], got)
    big["gla_a_up_w"] = dwup
    small["gla_a_up_b"] = dbup
    (ds5in, dbd, dcd, dd, dar4, dai4), got = _s5_bwd(
        dy, s5in, xs, cd16, bdt16, ar4, ai4, dskip, scatter(GROUPS[2][2:]))
    scattered(GROUPS[2][2:], got)
    dbbr, dbbi = _s5_undense(dbd)
    dcre, dcim_neg = _s5_undense(dcd)
    glre, glim, gldt, gbre, gbim = _s5_disc_bwd(
        lre, lim, ldt, bre, bim, dar4.reshape(S5_GROUPS, S5_STATE), dai4.reshape(S5_GROUPS, S5_STATE),
        dbbr, dbbi)
    small["s5_lambda_re"] = glre[None]
    small["s5_lambda_im"] = glim[None]
    small["s5_log_dt"] = gldt.reshape(1, S5_GROUPS)
    small["s5_b_re"] = gbre
    small["s5_b_im"] = gbim
    small["s5_c_re"] = dcre[None]
    small["s5_c_im"] = -dcim_neg[None]
    small["s5_d"] = dd.reshape(1, S5_GROUPS, S5_GROUP)
    dz = (ds5in, dq, dk, dv, dr, dalow, dgs5, dggla)
    dh1, dgm = _mix_pre_bwd(h1, gm, w["w_in"], dh2, dz)
    small["mix_norm"] = dgm
    big["w_in"] = jnp.concatenate([_mm_tn(d, u, "w_in_dw%d" % i) for i, d in enumerate(dz)], axis=0)
    (dx, dg1, da3, db3, s3, n1, dhh1), got = _ffn_bwd(
        x, dh1, g1, w["ffn1_w1"], w["ffn1_w3"], w["ffn1_w2"], "ffn1_bwd", scatter(GROUPS[1]))
    scattered(GROUPS[1], got)
    small["ffn1_norm"] = dg1
    big["ffn1_w1"] = _mm_tn(da3, n1, "ffn1_dw1")
    if rows is None:
        big["ffn1_w3"] = _mm_tn(db3, n1, "ffn1_dw3")
        big["ffn1_w2"] = _mm_tn(s3, dhh1, "ffn1_dw2")
        return loss[0, 0], dx, big, small
    big["ffn1_w3"], got = _mm_tn(db3, n1, "ffn1_dw3", scatter(GROUPS[0][:1]))
    scattered(GROUPS[0][:1], got)
    big["ffn1_w2"], got = _mm_tn(s3, dhh1, "ffn1_dw2", scatter(GROUPS[0][1:2]))
    scattered(GROUPS[0][1:2], got)
    scattered(GROUPS[0][2:], _exchange(scatter(GROUPS[0][2:])[0], True, "scatter_ffn1_w2"))
    return loss[0, 0], dx, landed_grads, small


NAMES = ("ffn1_norm", "ffn1_w1", "ffn1_w3", "ffn1_w2", "mix_norm", "w_in", "s5_lambda_re", "s5_lambda_im",
         "s5_log_dt", "s5_b_re", "s5_b_im", "s5_c_re", "s5_c_im", "s5_d", "s5_glu_w", "s5_glu_b", "gla_a_up_w",
         "gla_a_up_b", "gla_out_norm", "proj_s5", "proj_gla", "w_out", "ffn2_norm", "ffn2_w1", "ffn2_w3", "ffn2_w2",
         "final_norm")


def kernel(*args):
    nw = len(NAMES)
    x = args[0][0]
    wts = dict(zip(NAMES, args[1:1 + nw]))
    target = args[1 + nw][0]
    mom = dict(zip(NAMES, args[2 + nw:2 + 2 * nw]))
    var = dict(zip(NAMES, args[2 + 2 * nw:2 + 3 * nw]))

    shards = {n: wts[n][0] for n in BIG}
    rows = {n: _shard_rows(n, shards[n]).astype(BF16) for n in BIG}
    loss, dx, landed, small = _local_step(x, target, {n: wts[n] for n in SMALL}, None, rows)
    loss = lax.psum(loss, ("x", "y", "c"))

    grad, delta, new_m, new_v = {}, {}, {}, {}
    for n in BIG:
        g_rows = _sum_slabs(landed[n], "sum_" + n)
        if n in ROW_ADAM:
            g = g_rows[:W_IN_ROWS] if n == "w_in" else g_rows
            outs = _adamw(shards[n].T, g, mom[n][0].T, var[n][0].T, "adamw_" + n)
            grad[n], delta[n], new_m[n], new_v[n] = (a.T[None] for a in (g, *outs))
        else:
            g = _unshard_rows(n, g_rows, shards[n].shape)
            outs = _adamw(shards[n], g, mom[n][0], var[n][0], "adamw_" + n)
            grad[n], delta[n], new_m[n], new_v[n] = (a[None] for a in (g, *outs))

    part = _pack_small(small).reshape(N_DEV, SMALL_R // N_DEV, 1024)
    mine = _sum_slabs(_exchange([part], True, "scatter_small")[0], "sum_small")
    g_small = _unpack_small(_exchange([mine], False, "gather_small")[0].reshape(SMALL_R, 1024))

    def flat2d(a):
        return a.reshape(-1, a.shape[-1])

    operands = ([flat2d(_working(n, d[n])) for n in SMALL] for d in (wts, mom, var))
    w2d, m2d, v2d = operands
    outs = _adamw_many(w2d, [flat2d(g_small[n]) for n in SMALL], m2d, v2d, "adamw_small")
    for out, arrays in zip((grad, delta, new_m, new_v), ([g_small[n] for n in SMALL], *outs)):
        out.update({n: _declared(n, a.reshape(g_small[n].shape)) for n, a in zip(SMALL, arrays)})
    return (loss, dx[None], *(d[n] for d in (grad, delta, new_m, new_v) for n in NAMES))
```

```python
import functools
import math

import jax
import jax.numpy as jnp
from jax import lax
from jax.experimental import pallas as pl
from jax.experimental.pallas import tpu as pltpu

F32, BF16 = jnp.float32, jnp.bfloat16
HIGHEST = lax.Precision.HIGHEST

D_MODEL = 1024
D_FF = 2816
N_DEV = 8
S5_WIDTH, S5_GROUPS, S5_GROUP, S5_STATE = 512, 32, 16, 64
S5_BLOCKS = 4
S5_BSTATE = 512
S5_SEGS = 8
GLA_HEADS, GLA_DK, GLA_DV = 4, 64, 128
GLA_KEY, GLA_VAL, GLA_RANK, GLA_CHUNK = 256, 512, 16, 64
GLA_TAU = 16.0
GLA_STEP_CHUNKS = 4
EPS = 1e-6
IN_SIZES = (512, 256, 256, 512, 512, 16, 1024, 1024)
IN_OFFS = tuple(sum(IN_SIZES[:i]) for i in range(len(IN_SIZES)))
IN_COLS = sum(IN_SIZES)
ADAM_LR, ADAM_B1, ADAM_B2, ADAM_EPS, ADAM_WD, ADAM_STEP = 0.001, 0.9, 0.999, 1e-08, 0.01, 10
GELU_C0 = math.sqrt(2.0 / math.pi)
GELU_C1 = 0.044715

FFN_FT = 256
VMEM_LIMIT_BYTES = 56 * 1024 * 1024

VMEM_FULL = pl.BlockSpec(memory_space=pltpu.VMEM)
ANY = pl.BlockSpec(memory_space=pl.ANY)


def _cparams(n_grid):
    return pltpu.CompilerParams(dimension_semantics=("arbitrary",) * n_grid, vmem_limit_bytes=VMEM_LIMIT_BYTES)


def _tile(t):
    return 512 if t >= 1024 else t // 2


def _nn(a, b):
    return jnp.dot(a, b, preferred_element_type=F32)


def _nt(a, b):
    return lax.dot_general(a, b, (((1,), (1,)), ((), ())), preferred_element_type=F32)


def _tn(a, b):
    return lax.dot_general(a, b, (((0,), (0,)), ((), ())), preferred_element_type=F32)


def _rms_parts(x):
    r = lax.rsqrt(jnp.mean(x * x, axis=-1, keepdims=True) + EPS)
    return x * r, r


def _rms_bwd(dn, g, xhat, r):
    dxh = dn * g
    dx = r * (dxh - xhat * jnp.mean(dxh * xhat, axis=-1, keepdims=True))
    return dx, jnp.sum(dn * xhat, axis=0, keepdims=True)


def _peers():
    x, y, c = lax.axis_index("x"), lax.axis_index("y"), lax.axis_index("c")
    out = []
    for k in range(1, N_DEV):
        px = 1 - x if k & 4 else x
        py = 1 - y if k & 2 else y
        pc = 1 - c if k & 1 else c
        out.append(((px, py, pc), 4 * px + 2 * py + pc))
    return 4 * x + 2 * y + c, out


def _exchange_copies(src_refs, out_refs, send_sems, recv_sems, local_sems, scatter, with_recvs):
    me, peers = _peers()
    locals_, sends, recvs = [], [], []
    for a, (src_ref, out_ref) in enumerate(zip(src_refs, out_refs)):
        def mine(idx, src_ref=src_ref):
            return src_ref.at[idx] if scatter else src_ref

        locals_.append(pltpu.make_async_copy(mine(me), out_ref.at[me], local_sems.at[a]))
        for k, (dev, idx) in enumerate(peers):
            sends.append(pltpu.make_async_remote_copy(
                src_ref=mine(idx), dst_ref=out_ref.at[me], send_sem=send_sems.at[a, k], recv_sem=recv_sems.at[a, k],
                device_id=dev, device_id_type=pl.DeviceIdType.MESH))
            if with_recvs:
                recvs.append(pltpu.make_async_remote_copy(
                    src_ref=mine(idx), dst_ref=out_ref.at[idx], send_sem=send_sems.at[a, k],
                    recv_sem=recv_sems.at[a, k], device_id=dev, device_id_type=pl.DeviceIdType.MESH))
    return locals_, sends, recvs


def _remote(src, dst, send_sems, recv_sems, a, k, dev):
    return pltpu.make_async_remote_copy(src_ref=src, dst_ref=dst, send_sem=send_sems.at[a, k],
                                        recv_sem=recv_sems.at[a, k], device_id=dev,
                                        device_id_type=pl.DeviceIdType.MESH)


def _gather_places():
    x, y, c = lax.axis_index("x"), lax.axis_index("y"), lax.axis_index("c")
    chips = [(1 - x, y), (x, 1 - y), (1 - x, 1 - y)]
    sibling = (x, y, 1 - c)
    me_idx, sib_idx = 4 * x + 2 * y + c, 4 * x + 2 * y + 1 - c
    same_core = [((cx, cy, c), 4 * cx + 2 * cy + c) for cx, cy in chips]
    other_core_idx = [4 * cx + 2 * cy + 1 - c for cx, cy in chips]
    return sibling, me_idx, sib_idx, same_core, other_core_idx


def _gather_start(src_refs, out_refs, send_sems, recv_sems, local_sems):
    sibling, me_idx, _, same_core, _ = _gather_places()
    for a, (src, out) in enumerate(zip(src_refs, out_refs)):
        pltpu.make_async_copy(src, out.at[me_idx], local_sems.at[a]).start()
        _remote(src, out.at[me_idx], send_sems, recv_sems, a, 0, sibling).start()
        for j, (dev, _) in enumerate(same_core):
            _remote(src, out.at[me_idx], send_sems, recv_sems, a, 1 + j, dev).start()


def _gather_finish(src_refs, out_refs, send_sems, recv_sems, local_sems):
    sibling, me_idx, sib_idx, same_core, other_core_idx = _gather_places()
    arrays = list(enumerate(zip(src_refs, out_refs)))
    forwards = []
    for a, (src, out) in arrays:
        for j, (dev, idx) in enumerate(same_core):
            _remote(src, out.at[idx], send_sems, recv_sems, a, 1 + j, dev).wait_recv()
            fwd = _remote(out.at[idx], out.at[idx], send_sems, recv_sems, a, 4 + j, sibling)
            fwd.start()
            forwards.append(fwd)
    for a, (src, out) in arrays:
        _remote(src, out.at[sib_idx], send_sems, recv_sems, a, 0, sibling).wait_recv()
        for j, idx in enumerate(other_core_idx):
            _remote(src, out.at[idx], send_sems, recv_sems, a, 4 + j, sibling).wait_recv()
        _remote(src, out.at[me_idx], send_sems, recv_sems, a, 0, sibling).wait_send()
        for j, (dev, _) in enumerate(same_core):
            _remote(src, out.at[me_idx], send_sems, recv_sems, a, 1 + j, dev).wait_send()
        pltpu.make_async_copy(src, out.at[me_idx], local_sems.at[a]).wait()
    for fwd in forwards:
        fwd.wait_send()


def _exchange_start(*refs, scatter):
    if not scatter:
        return _gather_start(*refs)
    locals_, sends, _ = _exchange_copies(*refs, scatter=scatter, with_recvs=False)
    for cp in locals_ + sends:
        cp.start()


def _exchange_wait(*refs, scatter):
    if not scatter:
        return _gather_finish(*refs)
    locals_, sends, recvs = _exchange_copies(*refs, scatter=scatter, with_recvs=True)
    for cp in recvs:
        cp.wait_recv()
    for cp in sends:
        cp.wait_send()
    for cp in locals_:
        cp.wait()


def _exchange_sems(n_arrays):
    return [pltpu.SemaphoreType.DMA((n_arrays, N_DEV - 1)), pltpu.SemaphoreType.DMA((n_arrays, N_DEV - 1)),
            pltpu.SemaphoreType.DMA((n_arrays,))]


def _exchange_shapes(srcs, scatter):
    return [jax.ShapeDtypeStruct((N_DEV,) + tuple(s.shape[1:] if scatter else s.shape), s.dtype) for s in srcs]


def _call(body, *, name, grid, in_specs, out_specs, out_shape, args, scratch_shapes=(), carry=None):
    n_in, n_out, n_scr = len(in_specs), len(out_specs), len(scratch_shapes)
    srcs, scatter = carry if carry is not None else ((), False)
    nc = len(srcs)

    def wrapped(*refs):
        ins, refs = refs[:n_in], refs[n_in:]
        csrc, refs = refs[:nc], refs[nc:]
        outs, refs = refs[:n_out], refs[n_out:]
        cland, refs = refs[:nc], refs[nc:]
        scr, sems = refs[:n_scr], refs[n_scr:]
        if nc:
            @pl.when(pl.program_id(0) == 0)
            def _():
                _exchange_start(csrc, cland, *sems, scatter=scatter)

        body(*ins, *outs, *scr)
        if nc:
            @pl.when(pl.program_id(0) == grid[0] - 1)
            def _():
                _exchange_wait(csrc, cland, *sems, scatter=scatter)

    res = pl.pallas_call(
        wrapped, name=name, grid=grid,
        in_specs=list(in_specs) + [ANY] * nc, out_specs=list(out_specs) + [ANY] * nc,
        out_shape=list(out_shape) + _exchange_shapes(srcs, scatter),
        scratch_shapes=list(scratch_shapes) + (_exchange_sems(nc) if nc else []),
        compiler_params=_cparams(1),
    )(*args, *srcs)
    return res[:n_out], res[n_out:]


def _row_tile(tm, d):
    return pl.BlockSpec((tm, d), lambda i: (i, 0))


def _acc_row(d):
    return pl.BlockSpec((1, d), lambda i: (0, 0))


def _ffn_fwd(x, g, w1t, w3t, w2, name, carry=None):
    t = x.shape[0]
    tm = _tile(t)
    nf = D_FF // FFN_FT

    def body(x_ref, g_ref, w1_ref, w3_ref, w2_ref, o_ref, a_ref, b_ref, n_ref):
        xv = x_ref[...]
        xhat, _ = _rms_parts(xv)
        n = (xhat * g_ref[...]).astype(BF16)
        n_ref[...] = n
        o_ref[...] = xv

        def fstep(f, c):
            rows = pl.ds(pl.multiple_of(f * FFN_FT, FFN_FT), FFN_FT)
            a = _nt(n, w1_ref[rows, :])
            b = _nt(n, w3_ref[rows, :])
            a_ref[f] = a.astype(BF16)
            b_ref[f] = b.astype(BF16)
            s = (a * jax.nn.sigmoid(a) * b).astype(BF16)
            o_ref[...] += 0.5 * _nn(s, w2_ref[rows, :])
            return c

        lax.fori_loop(0, nf, fstep, 0, unroll=True)

    blk3 = pl.BlockSpec((nf, tm, FFN_FT), lambda i: (0, i, 0))
    sh3 = jax.ShapeDtypeStruct((nf, t, FFN_FT), BF16)
    (h, a3, b3, n), landed = _call(
        body, name=name, grid=(t // tm,),
        in_specs=[_row_tile(tm, D_MODEL), _acc_row(D_MODEL), VMEM_FULL, VMEM_FULL, VMEM_FULL],
        out_specs=[_row_tile(tm, D_MODEL), blk3, blk3, _row_tile(tm, D_MODEL)],
        out_shape=[jax.ShapeDtypeStruct((t, D_MODEL), F32), sh3, sh3, jax.ShapeDtypeStruct((t, D_MODEL), BF16)],
        args=(x, g, w1t, w3t, w2), carry=carry)
    return h, (a3, b3, n), landed


def _ffn_bwd(x, dh, g, a3, b3, w1t, w3t, w2, name, carry=None):
    t = x.shape[0]
    tm = _tile(t) // 2
    nf = D_FF // FFN_FT

    def body(x_ref, dh_ref, g_ref, a_ref, b_ref, w1_ref, w3_ref, w2_ref,
             dx_ref, dg_ref, da_ref, db_ref, s_ref, dhh_ref, dn_acc):
        i = pl.program_id(0)
        xv = x_ref[...]
        gv = g_ref[...]
        xhat, r = _rms_parts(xv)
        dhv = dh_ref[...]
        dhh = (0.5 * dhv).astype(BF16)
        dhh_ref[...] = dhh
        dn_acc[...] = jnp.zeros_like(dn_acc)

        def fstep(f, c):
            rows = pl.ds(pl.multiple_of(f * FFN_FT, FFN_FT), FFN_FT)
            w1c, w3c, w2c = w1_ref[rows, :], w3_ref[rows, :], w2_ref[rows, :]
            a = a_ref[f].astype(F32)
            b = b_ref[f].astype(F32)
            sg = jax.nn.sigmoid(a)
            sl = a * sg
            ds = _nt(dhh, w2c)
            da = (ds * b * sg * (1.0 + a * (1.0 - sg))).astype(BF16)
            db = (ds * sl).astype(BF16)
            s_ref[f] = (sl * b).astype(BF16)
            da_ref[f] = da
            db_ref[f] = db
            dn_acc[...] += _nn(da, w1c) + _nn(db, w3c)
            return c

        lax.fori_loop(0, nf, fstep, 0, unroll=True)
        dx, dg = _rms_bwd(dn_acc[...], gv, xhat, r)
        dx_ref[...] = dhv + dx

        @pl.when(i == 0)
        def _():
            dg_ref[...] = jnp.zeros_like(dg_ref)

        dg_ref[...] += dg

    blk3 = pl.BlockSpec((nf, tm, FFN_FT), lambda i: (0, i, 0))
    sh3 = jax.ShapeDtypeStruct((nf, t, FFN_FT), BF16)
    return _call(
        body, name=name, grid=(t // tm,),
        in_specs=[_row_tile(tm, D_MODEL), _row_tile(tm, D_MODEL), _acc_row(D_MODEL), blk3, blk3,
                  VMEM_FULL, VMEM_FULL, VMEM_FULL],
        out_specs=[_row_tile(tm, D_MODEL), _acc_row(D_MODEL), blk3, blk3, blk3, _row_tile(tm, D_MODEL)],
        out_shape=[jax.ShapeDtypeStruct((t, D_MODEL), F32), jax.ShapeDtypeStruct((1, D_MODEL), F32), sh3, sh3, sh3,
                   jax.ShapeDtypeStruct((t, D_MODEL), BF16)],
        scratch_shapes=[pltpu.VMEM((tm, D_MODEL), F32)],
        args=(x, dh, g, a3, b3, w1t, w3t, w2), carry=carry)


def _mm_tn(a, b, name, carry=None):
    t, n = b.shape
    kc = min(512, t)
    if a.ndim == 3:
        nb, _, tb = a.shape
        a_spec = pl.BlockSpec((1, t, tb), lambda i: (i, 0, 0))
    else:
        m = a.shape[1]
        tb = min(m, 256)
        nb = m // tb
        a_spec = pl.BlockSpec((t, tb), lambda i: (0, i))
    three_d = a.ndim == 3

    def body(a_ref, b_ref, o_ref, acc):
        acc[...] = jnp.zeros_like(acc)

        def kstep(k, c):
            rows = pl.ds(pl.multiple_of(k * kc, kc), kc)
            av = a_ref[0, rows, :] if three_d else a_ref[rows, :]
            acc[...] += _tn(av.astype(BF16), b_ref[rows, :])
            return c

        lax.fori_loop(0, t // kc, kstep, 0, unroll=True)
        o_ref[...] = acc[...].astype(BF16)

    (out,), landed = _call(
        body, name=name, grid=(nb,),
        in_specs=[a_spec, VMEM_FULL],
        out_specs=[pl.BlockSpec((tb, n), lambda i: (i, 0))],
        out_shape=[jax.ShapeDtypeStruct((nb * tb, n), BF16)],
        scratch_shapes=[pltpu.VMEM((tb, n), F32)],
        args=(a, b), carry=carry)
    return (out, landed) if carry is not None else out


def _mix_pre_fwd(h, g, wint, carry=None):
    t = h.shape[0]
    tm = _tile(t)

    def body(h_ref, g_ref, w_ref, u_ref, *outs):
        xhat, _ = _rms_parts(h_ref[...])
        u = (xhat * g_ref[...]).astype(BF16)
        u_ref[...] = u
        for o_ref, off, size in zip(outs, IN_OFFS, IN_SIZES):
            o_ref[...] = _nt(u, w_ref[off:off + size, :])

    return _call(
        body, name="mix_pre_fwd", grid=(t // tm,),
        in_specs=[_row_tile(tm, D_MODEL), _acc_row(D_MODEL), VMEM_FULL],
        out_specs=[_row_tile(tm, D_MODEL)] + [_row_tile(tm, s) for s in IN_SIZES],
        out_shape=[jax.ShapeDtypeStruct((t, D_MODEL), BF16)] + [jax.ShapeDtypeStruct((t, s), F32) for s in IN_SIZES],
        args=(h, g, wint), carry=carry)


def _mix_pre_bwd(h, g, wint, dh2, dz):
    t = h.shape[0]
    tm = _tile(t)

    def body(h_ref, g_ref, w_ref, dh2_ref, *rest):
        dz_refs, (dh1_ref, dg_ref) = rest[:len(IN_SIZES)], rest[len(IN_SIZES):]
        i = pl.program_id(0)
        gv = g_ref[...]
        xhat, r = _rms_parts(h_ref[...])
        du = jnp.zeros((tm, D_MODEL), F32)
        for dz_ref, off, size in zip(dz_refs, IN_OFFS, IN_SIZES):
            du = du + _nn(dz_ref[...].astype(BF16), w_ref[off:off + size, :])
        dx, dg = _rms_bwd(du, gv, xhat, r)
        dh1_ref[...] = dh2_ref[...] + dx

        @pl.when(i == 0)
        def _():
            dg_ref[...] = jnp.zeros_like(dg_ref)

        dg_ref[...] += dg

    return pl.pallas_call(
        body, name="mix_pre_bwd", grid=(t // tm,),
        in_specs=[_row_tile(tm, D_MODEL), _acc_row(D_MODEL), VMEM_FULL, _row_tile(tm, D_MODEL)]
        + [_row_tile(tm, s) for s in IN_SIZES],
        out_specs=[_row_tile(tm, D_MODEL), _acc_row(D_MODEL)],
        out_shape=[jax.ShapeDtypeStruct((t, D_MODEL), F32), jax.ShapeDtypeStruct((1, D_MODEL), F32)],
        compiler_params=_cparams(1),
    )(h, g, wint, dh2, *dz)


def _disc_math(lre, lim, ldt, bre, bim):
    dt = jnp.exp(ldt)
    mag = jnp.exp(lre * dt)
    ar = mag * jnp.cos(lim * dt)
    ai = mag * jnp.sin(lim * dt)
    den = lre * lre + lim * lim
    nr = ar - 1.0
    fr = (nr * lre + ai * lim) / den
    fi = (ai * lre - nr * lim) / den
    fr, fi = fr[:, None, :], fi[:, None, :]
    return ar, ai, fr * bre - fi * bim, fr * bim + fi * bre


def _s5_disc(lre, lim, ldt, bre, bim):
    def body(lre_ref, lim_ref, ldt_ref, bre_ref, bim_ref, ar_ref, ai_ref, bbr_ref, bbi_ref):
        ar, ai, bbr, bbi = _disc_math(lre_ref[...], lim_ref[...], ldt_ref[...], bre_ref[...], bim_ref[...])
        ar_ref[...] = ar
        ai_ref[...] = ai
        bbr_ref[...] = bbr
        bbi_ref[...] = bbi

    small = jax.ShapeDtypeStruct(lre.shape, F32)
    big = jax.ShapeDtypeStruct(bre.shape, F32)
    return pl.pallas_call(body, name="s5_disc", out_shape=[small, small, big, big],
                          in_specs=[VMEM_FULL] * 5, out_specs=[VMEM_FULL] * 4)(lre, lim, ldt, bre, bim)


def _s5_disc_bwd(lre, lim, ldt, bre, bim, dar, dai, dbbr, dbbi):
    def body(lre_ref, lim_ref, ldt_ref, bre_ref, bim_ref, dar_ref, dai_ref, dbbr_ref, dbbi_ref,
             glre_ref, glim_ref, gldt_ref, gbre_ref, gbim_ref):
        _, vjp = jax.vjp(_disc_math, lre_ref[...], lim_ref[...], ldt_ref[...], bre_ref[...], bim_ref[...])
        glre, glim, gldt, gbre, gbim = vjp((dar_ref[...], dai_ref[...], dbbr_ref[...], dbbi_ref[...]))
        glre_ref[...] = glre
        glim_ref[...] = glim
        gldt_ref[...] = gldt
        gbre_ref[...] = gbre
        gbim_ref[...] = gbim

    small = jax.ShapeDtypeStruct(lre.shape, F32)
    big = jax.ShapeDtypeStruct(bre.shape, F32)
    return pl.pallas_call(body, name="s5_disc_bwd",
                          out_shape=[small, small, jax.ShapeDtypeStruct(ldt.shape, F32), big, big],
                          in_specs=[VMEM_FULL] * 9, out_specs=[VMEM_FULL] * 5,
                          )(lre, lim, ldt, bre, bim, dar, dai, dbbr, dbbi)


def _cmul(ar, ai, br, bi):
    return ar * br - ai * bi, ar * bi + ai * br


def _cpow(ar, ai, n):
    rr, ri = None, None
    pr, pi = ar, ai
    while n:
        if n & 1:
            rr, ri = (pr, pi) if rr is None else _cmul(rr, ri, pr, pi)
        n >>= 1
        if n:
            pr, pi = _cmul(pr, pi, pr, pi)
    return rr, ri


def _shift_rows(v, down):
    row = lax.broadcasted_iota(jnp.int32, v.shape, 0)
    if down:
        return jnp.where(row == 0, 0.0, pltpu.roll(v, 1, 0))
    return jnp.where(row == S5_SEGS - 1, 0.0, pltpu.roll(v, S5_SEGS - 1, 0))


def _chain_segments(er, ei, pr, pi, down):
    fr, fi = er, ei
    for _ in range(S5_SEGS - 1):
        sr, si = _shift_rows(fr, down), _shift_rows(fi, down)
        mr, mi = _cmul(pr, pi, sr, si)
        fr, fi = er + mr, ei + mi
    return _shift_rows(fr, down), _shift_rows(fi, down)


def _rows_to_scan_order(src_ref, dst_ref, t):
    ls = t // S5_SEGS

    def tile(j, c):
        dst_ref[pl.ds(pl.multiple_of(j * S5_SEGS, S5_SEGS), S5_SEGS), :] = src_ref[pl.ds(j, S5_SEGS, stride=ls), :]
        return c

    lax.fori_loop(0, ls, tile, 0, unroll=8)


def _rows_from_scan_order(src_ref, dst_ref, t):
    ls = t // S5_SEGS
    for s in range(S5_SEGS):
        def tile(jb, c, s=s):
            dst_ref[pl.ds(pl.multiple_of(s * ls + jb * 8, 8), 8), :] = (
                src_ref[pl.ds(jb * 8 * S5_SEGS + s, 8, stride=S5_SEGS), :])
            return c

        lax.fori_loop(0, ls // 8, tile, 0, unroll=8)


def _s5_fwd(ug, bd, ctd, ar4, ai4, dskip, carry=None):
    t = ug.shape[0]
    ls = t // S5_SEGS
    rc = min(512, t)
    ns = S5_BSTATE

    def body(ugn_ref, bd_ref, ct_ref, ar_ref, ai_ref, d_ref, xs_hbm, yn_ref, buf, ug_ref, y_ref, sem):
        cb = pl.program_id(0)
        bdv = bd_ref[0]
        _rows_to_scan_order(ugn_ref, ug_ref, t)

        def mm(i, c):
            rows = pl.ds(pl.multiple_of(i * rc, rc), rc)
            buf[rows, :] = _nn(ug_ref[rows, :].astype(BF16), bdv)
            return c

        lax.fori_loop(0, t // rc, mm, 0, unroll=True)
        arb = jnp.broadcast_to(ar_ref[0], (S5_SEGS, ns))
        aib = jnp.broadcast_to(ai_ref[0], (S5_SEGS, ns))

        def step(j, c, store):
            sr, si = c
            rows = pl.ds(pl.multiple_of(j * S5_SEGS, S5_SEGS), S5_SEGS)
            nr = arb * sr - aib * si + buf[rows, 0:ns]
            ni = arb * si + aib * sr + buf[rows, ns:2 * ns]
            if store:
                buf[rows, 0:ns] = nr
                buf[rows, ns:2 * ns] = ni
            return nr, ni

        zero = jnp.zeros((S5_SEGS, ns), F32)
        er, ei = lax.fori_loop(0, ls, functools.partial(step, store=False), (zero, zero))
        pr, pi = _cpow(arb, aib, ls)
        init = _chain_segments(er, ei, pr, pi, down=True)
        lax.fori_loop(0, ls, functools.partial(step, store=True), init)

        out = pltpu.make_async_copy(buf, xs_hbm.at[cb], sem)
        out.start()
        ctv = ct_ref[0]
        dv = d_ref[...]

        def ymm(i, c):
            rows = pl.ds(pl.multiple_of(i * rc, rc), rc)
            y_ref[rows, :] = _nn(buf[rows, :].astype(BF16), ctv) + dv * ug_ref[rows, :]
            return c

        lax.fori_loop(0, t // rc, ymm, 0, unroll=True)
        _rows_from_scan_order(y_ref, yn_ref, t)
        out.wait()

    return _call(
        body, name="s5_fwd", grid=(S5_BLOCKS,),
        in_specs=[pl.BlockSpec((t, 128), lambda i: (0, i)),
                  pl.BlockSpec((1, 128, 2 * ns), lambda i: (i, 0, 0)),
                  pl.BlockSpec((1, 2 * ns, 128), lambda i: (i, 0, 0)),
                  pl.BlockSpec((1, 1, ns), lambda i: (i, 0, 0)),
                  pl.BlockSpec((1, 1, ns), lambda i: (i, 0, 0)),
                  pl.BlockSpec((1, 128), lambda i: (0, i))],
        out_specs=[ANY, pl.BlockSpec((t, 128), lambda i: (0, i))],
        out_shape=[jax.ShapeDtypeStruct((S5_BLOCKS, t, 2 * ns), F32), jax.ShapeDtypeStruct((t, S5_WIDTH), F32)],
        scratch_shapes=[pltpu.VMEM((t, 2 * ns), F32), pltpu.VMEM((t, 128), F32), pltpu.VMEM((t, 128), F32),
                        pltpu.SemaphoreType.DMA(())],
        args=(ug, bd, ctd, ar4, ai4, dskip), carry=carry)


def _s5_bwd(dy, ug, xs, cd, bdt, ar4, ai4, dskip, carry=None):
    t = ug.shape[0]
    ls = t // S5_SEGS
    rc = min(512, t)
    ns = S5_BSTATE

    def body(dyn_ref, ugn_ref, xs_hbm, cd_ref, bdt_ref, ar_ref, ai_ref, d_ref,
             dugn_ref, dbd_ref, dcd_ref, dd_ref, dar_ref, dai_ref, xbuf, lam, dy_ref, ug_ref, dug_ref, sem):
        cb = pl.program_id(0)
        load = pltpu.make_async_copy(xs_hbm.at[cb], xbuf, sem)
        load.start()
        cdv = cd_ref[0]
        _rows_to_scan_order(dyn_ref, dy_ref, t)
        _rows_to_scan_order(ugn_ref, ug_ref, t)

        def mm(i, c):
            rows = pl.ds(pl.multiple_of(i * rc, rc), rc)
            lam[rows, :] = _nn(dy_ref[rows, :].astype(BF16), cdv)
            return c

        lax.fori_loop(0, t // rc, mm, 0, unroll=True)
        arb = jnp.broadcast_to(ar_ref[0], (S5_SEGS, ns))
        aib = jnp.broadcast_to(ai_ref[0], (S5_SEGS, ns))

        def lam_step(j, lr, li):
            rows = pl.ds(pl.multiple_of(j * S5_SEGS, S5_SEGS), S5_SEGS)
            nr = arb * lr + aib * li + lam[rows, 0:ns]
            ni = arb * li - aib * lr + lam[rows, ns:2 * ns]
            return rows, nr, ni

        def pass1(jj, c):
            _, nr, ni = lam_step(ls - 1 - jj, *c)
            return nr, ni

        zero = jnp.zeros((S5_SEGS, ns), F32)
        er, ei = lax.fori_loop(0, ls, pass1, (zero, zero))
        pr, pi = _cpow(arb, aib, ls)
        init = _chain_segments(er, ei, pr, -pi, down=False)
        load.wait()

        def accumulate(acc, nr, ni, xpr, xpi):
            return acc[0] + nr * xpr + ni * xpi, acc[1] + ni * xpr - nr * xpi

        def pass2(jj, c):
            lr, li, accr, acci = c
            j = ls - 1 - jj
            rows, nr, ni = lam_step(j, lr, li)
            lam[rows, 0:ns] = nr
            lam[rows, ns:2 * ns] = ni
            prev = pl.ds(pl.multiple_of((j - 1) * S5_SEGS, S5_SEGS), S5_SEGS)
            accr, acci = accumulate((accr, acci), nr, ni, xbuf[prev, 0:ns], xbuf[prev, ns:2 * ns])
            return nr, ni, accr, acci

        lr, li, accr, acci = lax.fori_loop(0, ls - 1, pass2, (init[0], init[1], zero, zero))
        rows, nr, ni = lam_step(0, lr, li)
        lam[rows, 0:ns] = nr
        lam[rows, ns:2 * ns] = ni
        last = pl.ds((ls - 1) * S5_SEGS, S5_SEGS)
        accr, acci = accumulate((accr, acci), nr, ni,
                                _shift_rows(xbuf[last, 0:ns], True), _shift_rows(xbuf[last, ns:2 * ns], True))
        dar_ref[0] = jnp.sum(accr, axis=0, keepdims=True)
        dai_ref[0] = jnp.sum(acci, axis=0, keepdims=True)

        bdtv = bdt_ref[0]
        dv = d_ref[...]
        dbd_ref[...] = jnp.zeros_like(dbd_ref)
        dcd_ref[...] = jnp.zeros_like(dcd_ref)
        dd_ref[...] = jnp.zeros_like(dd_ref)

        def tail(i, c):
            rows = pl.ds(pl.multiple_of(i * rc, rc), rc)
            dy = dy_ref[rows, :]
            ug = ug_ref[rows, :]
            lb = lam[rows, :].astype(BF16)
            dug_ref[rows, :] = _nn(lb, bdtv) + dv * dy
            dbd_ref[0] += _tn(ug.astype(BF16), lb)
            dcd_ref[0] += _tn(dy.astype(BF16), xbuf[rows, :].astype(BF16))
            dd_ref[...] += jnp.sum(dy * ug, axis=0, keepdims=True)
            return c

        lax.fori_loop(0, t // rc, tail, 0, unroll=True)
        _rows_from_scan_order(dug_ref, dugn_ref, t)

    chan = pl.BlockSpec((t, 128), lambda i: (0, i))
    dense = pl.BlockSpec((1, 128, 2 * ns), lambda i: (i, 0, 0))
    vec = pl.BlockSpec((1, 1, ns), lambda i: (i, 0, 0))
    return _call(
        body, name="s5_bwd", grid=(S5_BLOCKS,),
        in_specs=[chan, chan, ANY, dense, pl.BlockSpec((1, 2 * ns, 128), lambda i: (i, 0, 0)), vec, vec,
                  pl.BlockSpec((1, 128), lambda i: (0, i))],
        out_specs=[chan, dense, dense, pl.BlockSpec((1, 128), lambda i: (0, i)), vec, vec],
        out_shape=[jax.ShapeDtypeStruct((t, S5_WIDTH), F32),
                   jax.ShapeDtypeStruct((S5_BLOCKS, 128, 2 * ns), F32),
                   jax.ShapeDtypeStruct((S5_BLOCKS, 128, 2 * ns), F32),
                   jax.ShapeDtypeStruct((1, S5_WIDTH), F32),
                   jax.ShapeDtypeStruct((S5_BLOCKS, 1, ns), F32),
                   jax.ShapeDtypeStruct((S5_BLOCKS, 1, ns), F32)],
        scratch_shapes=[pltpu.VMEM((t, 2 * ns), F32), pltpu.VMEM((t, 2 * ns), F32)]
        + [pltpu.VMEM((t, 128), F32)] * 3 + [pltpu.SemaphoreType.DMA(())],
        args=(dy, ug, xs, cd, bdt, ar4, ai4, dskip), carry=carry)


def _cumsum_rows(x, reverse):
    c = x.shape[0]
    row = lax.broadcasted_iota(jnp.int32, x.shape, 0)
    d = 1
    while d < c:
        if reverse:
            x = x + jnp.where(row < c - d, pltpu.roll(x, c - d, 0), 0.0)
        else:
            x = x + jnp.where(row >= d, pltpu.roll(x, d, 0), 0.0)
        d *= 2
    return x


def _gla_common(q, k, alow, wup, bup):
    c = GLA_CHUNK
    pre = _nn(alow.astype(BF16), wup.astype(BF16)) + bup
    la = (jnp.minimum(pre, 0.0) - jnp.log(1.0 + jnp.exp(-jnp.abs(pre)))) * (1.0 / GLA_TAU)
    rr = lax.broadcasted_iota(jnp.int32, (c, c), 0)
    cc = lax.broadcasted_iota(jnp.int32, (c, c), 1)
    tril = (rr >= cc).astype(F32)
    bc = _cumsum_rows(la, reverse=False)
    bl = bc[c - 1:c, :]
    e_pos = jnp.exp(bc)
    e_neg = jnp.exp(-bc)
    e_end = jnp.exp(bl - bc)
    qt = q * (GLA_DK ** -0.5) * e_pos
    kt = k * e_neg
    ke = k * e_end
    lane = lax.broadcasted_iota(jnp.int32, (1, GLA_KEY), 1)
    masks = [((lane >= h * GLA_DK) & (lane < (h + 1) * GLA_DK)).astype(F32) for h in range(GLA_HEADS)]
    return dict(pre=pre, tril=tril, bc=bc, bl=bl, e_pos=e_pos, e_neg=e_neg, e_end=e_end,
                qt=qt, kt=kt, ke=ke, dec=jnp.exp(bl), masks=masks)


def _gla_fwd(q, k, v, alow, wup, bup, carry=None):
    t = q.shape[0]
    c = GLA_CHUNK
    n = t // c
    step = GLA_STEP_CHUNKS * c

    def body(q_ref, k_ref, v_ref, al_ref, wup_ref, bup_ref, o_ref, ss_ref, s_ref):
        i = pl.program_id(0)

        @pl.when(i == 0)
        def _():
            s_ref[...] = jnp.zeros_like(s_ref)

        wup_v, bup_v = wup_ref[...], bup_ref[...]
        s = s_ref[...]
        for j in range(GLA_STEP_CHUNKS):
            tok = slice(j * c, (j + 1) * c)
            m = _gla_common(q_ref[tok, :], k_ref[tok, :], al_ref[tok, :], wup_v, bup_v)
            ss_ref[j] = s
            sb = s.astype(BF16)
            ktb = m["kt"].astype(BF16)
            update = jnp.zeros_like(s)
            for h in range(GLA_HEADS):
                mask = m["masks"][h]
                qm = (m["qt"] * mask).astype(BF16)
                vh = v_ref[tok, h * GLA_DV:(h + 1) * GLA_DV].astype(BF16)
                p = (m["tril"] * _nt(qm, ktb)).astype(BF16)
                o_ref[tok, h * GLA_DV:(h + 1) * GLA_DV] = _nn(p, vh) + _nt(qm, sb)
                update = update + _tn(vh, (m["ke"] * mask).astype(BF16))
            s = m["dec"] * s + update
        s_ref[...] = s

    return _call(
        body, name="gla_fwd", grid=(t // step,),
        in_specs=[_row_tile(step, GLA_KEY), _row_tile(step, GLA_KEY), _row_tile(step, GLA_VAL),
                  _row_tile(step, GLA_RANK), VMEM_FULL, VMEM_FULL],
        out_specs=[_row_tile(step, GLA_VAL), pl.BlockSpec((GLA_STEP_CHUNKS, GLA_DV, GLA_KEY), lambda i: (i, 0, 0))],
        out_shape=[jax.ShapeDtypeStruct((t, GLA_VAL), F32), jax.ShapeDtypeStruct((n, GLA_DV, GLA_KEY), F32)],
        scratch_shapes=[pltpu.VMEM((GLA_DV, GLA_KEY), F32)],
        args=(q, k, v, alow, wup, bup), carry=carry)


def _gla_bwd(q, k, v, alow, wup, bup, ssave, do, carry=None):
    t = q.shape[0]
    c = GLA_CHUNK
    n = t // c

    def body(q_ref, k_ref, v_ref, al_ref, wup_ref, bup_ref, ss_ref, do_ref,
             dq_ref, dk_ref, dv_ref, dal_ref, dwup_ref, dbup_ref, ds_ref):
        i = pl.program_id(0)

        @pl.when(i == 0)
        def _():
            ds_ref[...] = jnp.zeros_like(ds_ref)
            dwup_ref[...] = jnp.zeros_like(dwup_ref)
            dbup_ref[...] = jnp.zeros_like(dbup_ref)

        wup_v, bup_v = wup_ref[...], bup_ref[...]
        ds_in = ds_ref[...]
        dwup = jnp.zeros((GLA_RANK, GLA_KEY), F32)
        dbup = jnp.zeros((1, GLA_KEY), F32)
        for j in reversed(range(GLA_STEP_CHUNKS)):
            tok = slice(j * c, (j + 1) * c)
            alow_v = al_ref[tok, :]
            m = _gla_common(q_ref[tok, :], k_ref[tok, :], alow_v, wup_v, bup_v)
            s = ss_ref[j]
            sb = s.astype(BF16)
            dsb = ds_in.astype(BF16)
            qt, kt, ke = m["qt"], m["kt"], m["ke"]
            ktb = kt.astype(BF16)
            dqt = jnp.zeros((c, GLA_KEY), F32)
            dkt = jnp.zeros((c, GLA_KEY), F32)
            dke = jnp.zeros((c, GLA_KEY), F32)
            update = jnp.zeros_like(ds_in)
            for h in range(GLA_HEADS):
                mask = m["masks"][h]
                qm = (qt * mask).astype(BF16)
                km = (kt * mask).astype(BF16)
                kem = (ke * mask).astype(BF16)
                cols = slice(h * GLA_DV, (h + 1) * GLA_DV)
                vh = v_ref[tok, cols].astype(BF16)
                doh = do_ref[tok, cols].astype(BF16)
                p = (m["tril"] * _nt(qm, ktb)).astype(BF16)
                dp = (m["tril"] * _nt(doh, vh)).astype(BF16)
                dv_ref[tok, cols] = _tn(p, doh) + _nt(kem, dsb)
                dqt = dqt + _nn(dp, km) + _nn(doh, sb) * mask
                dkt = dkt + _tn(dp, qm)
                dke = dke + _nn(vh, dsb) * mask
                update = update + _tn(doh, qm)
            ddec = jnp.sum(ds_in * s, axis=0, keepdims=True)
            dq_ref[tok, :] = dqt * m["e_pos"] * (GLA_DK ** -0.5)
            dk_ref[tok, :] = dkt * m["e_neg"] + dke * m["e_end"]
            dkeke = dke * ke
            dbl = jnp.sum(dkeke, axis=0, keepdims=True) + ddec * m["dec"]
            last = (lax.broadcasted_iota(jnp.int32, (c, 1), 0) == c - 1).astype(F32)
            dla = _cumsum_rows(dqt * qt - dkt * kt - dkeke + last * dbl, reverse=True)
            dpre = dla * (1.0 / GLA_TAU) * jax.nn.sigmoid(-m["pre"])
            dpb = dpre.astype(BF16)
            dal_ref[tok, :] = _nt(dpb, wup_v.astype(BF16))
            dwup = dwup + _tn(alow_v.astype(BF16), dpb)
            dbup = dbup + jnp.sum(dpre, axis=0, keepdims=True)
            ds_in = m["dec"] * ds_in + update
        ds_ref[...] = ds_in
        dwup_ref[...] += dwup
        dbup_ref[...] += dbup

    step = GLA_STEP_CHUNKS * c
    nsteps = t // step

    def rev(d):
        return pl.BlockSpec((step, d), lambda i: (nsteps - 1 - i, 0))

    return _call(
        body, name="gla_bwd", grid=(nsteps,),
        in_specs=[rev(GLA_KEY), rev(GLA_KEY), rev(GLA_VAL), rev(GLA_RANK), VMEM_FULL, VMEM_FULL,
                  pl.BlockSpec((GLA_STEP_CHUNKS, GLA_DV, GLA_KEY), lambda i: (nsteps - 1 - i, 0, 0)), rev(GLA_VAL)],
        out_specs=[rev(GLA_KEY), rev(GLA_KEY), rev(GLA_VAL), rev(GLA_RANK),
                   pl.BlockSpec((GLA_RANK, GLA_KEY), lambda i: (0, 0)), _acc_row(GLA_KEY)],
        out_shape=[jax.ShapeDtypeStruct((t, GLA_KEY), F32), jax.ShapeDtypeStruct((t, GLA_KEY), F32),
                   jax.ShapeDtypeStruct((t, GLA_VAL), F32), jax.ShapeDtypeStruct((t, GLA_RANK), F32),
                   jax.ShapeDtypeStruct((GLA_RANK, GLA_KEY), F32), jax.ShapeDtypeStruct((1, GLA_KEY), F32)],
        scratch_shapes=[pltpu.VMEM((GLA_DV, GLA_KEY), F32)],
        args=(q, k, v, alow, wup, bup, ssave, do), carry=carry)


def _post_math(y, o, r, gs5, ggla, wg, bg, gn, ps5t, pglat):
    y2 = y * y
    th = jnp.tanh(GELU_C0 * (y + GELU_C1 * y * y2))
    z5 = 0.5 * y * (1.0 + th)
    z5b = z5.astype(BF16)
    gate = jax.nn.sigmoid(_nn(z5b, wg) + bg)
    ys5 = z5 * gate
    rs, on = [], []
    for h in range(GLA_HEADS):
        oh = o[:, h * GLA_DV:(h + 1) * GLA_DV]
        rh = lax.rsqrt(jnp.mean(oh * oh, axis=-1, keepdims=True) + EPS)
        rs.append(rh)
        on.append(oh * rh)
    on = jnp.concatenate(on, axis=-1)
    sr = jax.nn.sigmoid(r)
    silu_r = r * sr
    ygla = on * gn * silu_r
    ys5b, yglab = ys5.astype(BF16), ygla.astype(BF16)
    m5 = _nt(ys5b, ps5t)
    mg = _nt(yglab, pglat)
    s5g, glag = jax.nn.sigmoid(gs5), jax.nn.sigmoid(ggla)
    merged = s5g * m5 + glag * mg
    return dict(y2=y2, th=th, z5=z5, z5b=z5b, gate=gate, ys5b=ys5b, yglab=yglab, rs=rs, on=on, sr=sr,
                silu_r=silu_r, m5=m5, mg=mg, s5g=s5g, glag=glag, mergedb=merged.astype(BF16))


def _mix_post_fwd(y, o, r, gs5, ggla, h1, wg, bg, gn, ps5t, pglat, wout, carry=None):
    t = o.shape[0]
    tm = _tile(t)

    def body(y_ref, o_ref, r_ref, gs5_ref, ggla_ref, h1_ref, wg_ref, bg_ref, gn_ref, ps_ref, pg_ref, wo_ref, h2_ref):
        m = _post_math(y_ref[...], o_ref[...], r_ref[...], gs5_ref[...], ggla_ref[...],
                       wg_ref[...], bg_ref[...], gn_ref[...], ps_ref[...], pg_ref[...])
        h2_ref[...] = h1_ref[...] + _nn(m["mergedb"], wo_ref[...])

    (h2,), landed = _call(
        body, name="mix_post_fwd", grid=(t // tm,),
        in_specs=[_row_tile(tm, 512)] * 3 + [_row_tile(tm, D_MODEL)] * 3
        + [VMEM_FULL, _acc_row(512), _acc_row(512), VMEM_FULL, VMEM_FULL, VMEM_FULL],
        out_specs=[_row_tile(tm, D_MODEL)],
        out_shape=[jax.ShapeDtypeStruct((t, D_MODEL), F32)],
        args=(y, o, r, gs5, ggla, h1, wg, bg, gn, ps5t, pglat, wout), carry=carry)
    return h2, landed


def _mix_post_bwd(y, o, r, gs5, ggla, dh2, wg, bg, gn, ps5t, pglat, wout, carry=None):
    t = o.shape[0]
    tm = _tile(t) // 2

    def body(y_ref, o_ref, r_ref, gs5_ref, ggla_ref, dh2_ref, wg_ref, bg_ref, gn_ref, ps_ref, pg_ref, wo_ref,
             dy_ref, do_ref, dr_ref, dgs5_ref, dggla_ref, dbg_ref, dgn_ref,
             z5b_ref, dgp_ref, ys5b_ref, dm5b_ref, yglab_ref, dmgb_ref, mergedb_ref, dh2b_ref):
        i = pl.program_id(0)
        yv, ov, rv = y_ref[...], o_ref[...], r_ref[...]
        wg, gn, ps5t, pglat = wg_ref[...], gn_ref[...], ps_ref[...], pg_ref[...]
        m = _post_math(yv, ov, rv, gs5_ref[...], ggla_ref[...], wg, bg_ref[...], gn, ps5t, pglat)
        dh2b = dh2_ref[...].astype(BF16)
        dmerged = _nt(dh2b, wo_ref[...])
        s5g, glag = m["s5g"], m["glag"]
        dgs5_ref[...] = dmerged * m["m5"] * s5g * (1.0 - s5g)
        dggla_ref[...] = dmerged * m["mg"] * glag * (1.0 - glag)
        dm5b = (dmerged * s5g).astype(BF16)
        dmgb = (dmerged * glag).astype(BF16)
        dys5 = _nn(dm5b, ps5t)
        dygla = _nn(dmgb, pglat)
        gate, z5, th = m["gate"], m["z5"], m["th"]
        dgpre = dys5 * z5 * gate * (1.0 - gate)
        dgpb = dgpre.astype(BF16)
        dz5 = dys5 * gate + _nt(dgpb, wg)
        dgelu = 0.5 * (1.0 + th) + 0.5 * yv * (1.0 - th * th) * GELU_C0 * (1.0 + 3.0 * GELU_C1 * m["y2"])
        dy_ref[...] = dz5 * dgelu
        on, sr, silu_r = m["on"], m["sr"], m["silu_r"]
        dr_ref[...] = dygla * on * gn * sr * (1.0 + rv * (1.0 - sr))
        dgn = jnp.sum(dygla * on * silu_r, axis=0, keepdims=True)
        don = dygla * gn * silu_r
        for h in range(GLA_HEADS):
            cols = slice(h * GLA_DV, (h + 1) * GLA_DV)
            donh, onh = don[:, cols], on[:, cols]
            do_ref[:, cols] = m["rs"][h] * (donh - onh * jnp.mean(donh * onh, axis=-1, keepdims=True))

        @pl.when(i == 0)
        def _():
            dbg_ref[...] = jnp.zeros_like(dbg_ref)
            dgn_ref[...] = jnp.zeros_like(dgn_ref)

        dbg_ref[...] += jnp.sum(dgpre, axis=0, keepdims=True)
        dgn_ref[...] += dgn
        z5b_ref[...] = m["z5b"]
        dgp_ref[...] = dgpb
        ys5b_ref[...] = m["ys5b"]
        dm5b_ref[...] = dm5b
        yglab_ref[...] = m["yglab"]
        dmgb_ref[...] = dmgb
        mergedb_ref[...] = m["mergedb"]
        dh2b_ref[...] = dh2b

    def f32(d):
        return jax.ShapeDtypeStruct((t, d), F32)

    def b16(d):
        return jax.ShapeDtypeStruct((t, d), BF16)

    widths = (512, 512, 512, 1024, 512, 1024, 1024, 1024)
    return _call(
        body, name="mix_post_bwd", grid=(t // tm,),
        in_specs=[_row_tile(tm, 512)] * 3 + [_row_tile(tm, D_MODEL)] * 3
        + [VMEM_FULL, _acc_row(512), _acc_row(512), VMEM_FULL, VMEM_FULL, VMEM_FULL],
        out_specs=[_row_tile(tm, 512)] * 3 + [_row_tile(tm, D_MODEL)] * 2
        + [_acc_row(512)] * 2 + [_row_tile(tm, w) for w in widths],
        out_shape=[f32(512)] * 3 + [f32(D_MODEL)] * 2
        + [jax.ShapeDtypeStruct((1, 512), F32)] * 2
        + [b16(w) for w in widths],
        args=(y, o, r, gs5, ggla, dh2, wg, bg, gn, ps5t, pglat, wout), carry=carry)


def _head(h3, g, target):
    t = h3.shape[0]
    tm = _tile(t)

    def body(h_ref, g_ref, t_ref, loss_ref, dh_ref, dg_ref):
        i = pl.program_id(0)
        gv = g_ref[...]
        xhat, r = _rms_parts(h_ref[...])
        err = xhat * gv - t_ref[...]
        dx, dg = _rms_bwd(err * (1.0 / D_MODEL), gv, xhat, r)
        dh_ref[...] = dx

        @pl.when(i == 0)
        def _():
            loss_ref[...] = jnp.zeros_like(loss_ref)
            dg_ref[...] = jnp.zeros_like(dg_ref)

        loss_ref[...] += (0.5 / D_MODEL) * jnp.sum(jnp.sum(err * err, axis=1, keepdims=True), axis=0, keepdims=True)
        dg_ref[...] += dg

    return pl.pallas_call(
        body, name="head", grid=(t // tm,),
        in_specs=[_row_tile(tm, D_MODEL), _acc_row(D_MODEL), _row_tile(tm, D_MODEL)],
        out_specs=[pl.BlockSpec((1, 1), lambda i: (0, 0)), _row_tile(tm, D_MODEL), _acc_row(D_MODEL)],
        out_shape=[jax.ShapeDtypeStruct((1, 1), F32), jax.ShapeDtypeStruct((t, D_MODEL), F32),
                   jax.ShapeDtypeStruct((1, D_MODEL), F32)],
        compiler_params=_cparams(1),
    )(h3, g, target)


ADAM_TILE_ELEMS = 256 * 1024


def _adamw(w, g, m, v, name):
    rows, cols = w.shape
    tr = rows
    while tr * cols > ADAM_TILE_ELEMS and tr % 16 == 0:
        tr //= 2

    spec = pl.BlockSpec((tr, cols), lambda i: (i, 0))
    sh = jax.ShapeDtypeStruct((rows, cols), F32)
    return pl.pallas_call(functools.partial(_adamw_body), name=name, grid=(rows // tr,), in_specs=[spec] * 4,
                          out_specs=[spec] * 3, out_shape=[sh] * 3, compiler_params=_cparams(1))(w, g, m, v)


def _adamw_body(w_ref, g_ref, m_ref, v_ref, d_ref, nm_ref, nv_ref):
    gv = g_ref[...]
    nm = ADAM_B1 * m_ref[...] + (1.0 - ADAM_B1) * gv
    nv = ADAM_B2 * v_ref[...] + (1.0 - ADAM_B2) * (gv * gv)
    m_hat = nm / (1.0 - ADAM_B1 ** ADAM_STEP)
    v_hat = nv / (1.0 - ADAM_B2 ** ADAM_STEP)
    d_ref[...] = -ADAM_LR * (m_hat / (jnp.sqrt(v_hat) + ADAM_EPS) + ADAM_WD * w_ref[...])
    nm_ref[...] = nm
    nv_ref[...] = nv


def _adamw_many(ws, gs, ms, vs, name):
    n = len(ws)

    def body(*refs):
        ins, outs = refs[:4 * n], refs[4 * n:]
        for i in range(n):
            _adamw_body(*(ins[j * n + i] for j in range(4)), *(outs[j * n + i] for j in range(3)))

    shapes = [jax.ShapeDtypeStruct(w.shape, F32) for w in ws]
    res = pl.pallas_call(body, name=name, in_specs=[VMEM_FULL] * (4 * n), out_specs=[VMEM_FULL] * (3 * n),
                         out_shape=shapes * 3)(*ws, *gs, *ms, *vs)
    return res[:n], res[n:2 * n], res[2 * n:]


def _exchange(srcs, scatter, name):
    n = len(srcs)

    def body(*refs):
        _exchange_start(refs[:n], refs[n:2 * n], *refs[2 * n:], scatter=scatter)
        _exchange_wait(refs[:n], refs[n:2 * n], *refs[2 * n:], scatter=scatter)

    return pl.pallas_call(
        body, name=name, in_specs=[ANY] * n, out_specs=[ANY] * n,
        out_shape=_exchange_shapes(srcs, scatter), scratch_shapes=_exchange_sems(n),
    )(*srcs)


def _sum_slabs(slabs, name):
    n = slabs.shape[0]

    def body(s_ref, o_ref):
        acc = s_ref[0].astype(F32)
        for s in range(1, n):
            acc = acc + s_ref[s].astype(F32)
        o_ref[...] = acc

    return pl.pallas_call(
        body, name=name, in_specs=[VMEM_FULL], out_specs=VMEM_FULL,
        out_shape=jax.ShapeDtypeStruct(slabs.shape[1:], F32),
        compiler_params=pltpu.CompilerParams(vmem_limit_bytes=VMEM_LIMIT_BYTES),
    )(slabs)


BIG = ("ffn1_w1", "ffn1_w3", "ffn1_w2", "w_in", "s5_glu_w", "gla_a_up_w", "proj_s5", "proj_gla", "w_out",
       "ffn2_w1", "ffn2_w3", "ffn2_w2")
GROUPS = (("ffn1_w1", "ffn1_w3", "ffn1_w2"),
          ("w_in", "s5_glu_w", "gla_a_up_w", "proj_s5", "proj_gla", "w_out"),
          ("ffn2_w1", "ffn2_w3", "ffn2_w2"))
W_IN_ROWS = 514
W_IN_PAD = 528
UP_COLS = 32
ROW_ADAM = ("ffn1_w1", "ffn1_w3", "w_in", "ffn2_w1", "ffn2_w3")
COL_SHARDED = ("ffn1_w1", "ffn1_w3", "w_in", "proj_s5", "proj_gla", "ffn2_w1", "ffn2_w3")

SMALL = ("ffn1_norm", "mix_norm", "s5_lambda_re", "s5_lambda_im", "s5_log_dt", "s5_b_re", "s5_b_im", "s5_c_re",
         "s5_c_im", "s5_d", "s5_glu_b", "gla_a_up_b", "gla_out_norm", "ffn2_norm", "final_norm")
SMALL_SHAPES = dict(ffn1_norm=(1, 1024), mix_norm=(1, 1024), s5_lambda_re=(1, 32, 64), s5_lambda_im=(1, 32, 64),
                    s5_log_dt=(1, 32), s5_b_re=(1, 32, 64, 16), s5_b_im=(1, 32, 64, 16), s5_c_re=(1, 32, 16, 64),
                    s5_c_im=(1, 32, 16, 64), s5_d=(1, 32, 16), s5_glu_b=(1, 512), gla_a_up_b=(1, 256),
                    gla_out_norm=(1, 512), ffn2_norm=(1, 1024), final_norm=(1024,))
SMALL_N = sum(math.prod(s) for s in SMALL_SHAPES.values())
SMALL_R = -(-SMALL_N // (64 * 1024)) * 64


def _shard_rows(name, a):
    if name == "gla_a_up_w":
        return jnp.pad(a, ((0, 0), (0, 128 - UP_COLS)))
    if name in COL_SHARDED:
        a = a.T
    if name == "w_in":
        return jnp.pad(a, ((0, W_IN_PAD - W_IN_ROWS), (0, 0)))
    return a.reshape(-1, 1024)


def _unshard_rows(name, rows, shape):
    if name == "gla_a_up_w":
        return rows[:, :UP_COLS]
    if name == "w_in":
        rows = rows[:W_IN_ROWS]
    if name in COL_SHARDED:
        return rows.reshape(shape[1], shape[0]).T
    return rows.reshape(shape)


def _pack_small(vals):
    flat = jnp.concatenate([vals[n].reshape(-1).astype(F32) for n in SMALL])
    return jnp.pad(flat, (0, SMALL_R * 1024 - SMALL_N)).reshape(SMALL_R, 1024)


S5_B = ("s5_b_re", "s5_b_im")


def _working(name, a):
    return a[0].transpose(0, 2, 1) if name in S5_B else a


def _declared(name, a):
    return a.transpose(0, 2, 1)[None] if name in S5_B else a.reshape(SMALL_SHAPES[name])


def _unpack_small(slab):
    flat = slab.reshape(-1)
    out, off = {}, 0
    for n in SMALL:
        size = math.prod(SMALL_SHAPES[n])
        shape = (S5_GROUPS, S5_GROUP, S5_STATE) if n in S5_B else SMALL_SHAPES[n]
        out[n] = flat[off:off + size].reshape(shape)
        off += size
    return out


FULL_SHAPES = dict(w_in=(IN_COLS, D_MODEL), s5_glu_w=(S5_WIDTH, S5_WIDTH), gla_a_up_w=(GLA_RANK, GLA_KEY),
                   proj_s5=(D_MODEL, S5_WIDTH), proj_gla=(D_MODEL, GLA_VAL), w_out=(D_MODEL, D_MODEL))


def _full_weight(name, gathered):
    if name == "gla_a_up_w":
        return gathered[:, :, :UP_COLS].transpose(1, 0, 2).reshape(GLA_RANK, GLA_KEY)
    if name == "w_in":
        gathered = gathered[:, :W_IN_ROWS]
    return gathered.reshape(FULL_SHAPES.get(name, (D_FF, D_MODEL)))


def _grad_slabs(name, g):
    if name == "gla_a_up_w":
        g = g.reshape(GLA_RANK, N_DEV, UP_COLS).transpose(1, 0, 2)
        return jnp.pad(g, ((0, 0), (0, 0), (0, 128 - UP_COLS))).astype(BF16)
    if name == "w_in":
        return jnp.pad(g.reshape(N_DEV, W_IN_ROWS, D_MODEL), ((0, 0), (0, W_IN_PAD - W_IN_ROWS), (0, 0)))
    return g.reshape(N_DEV, -1, 1024)


def _s5_dense(re, im, sign_im):
    eye = jnp.eye(8, dtype=F32)

    def one(a):
        a = a.reshape(S5_BLOCKS, 8, S5_GROUP, S5_STATE)
        return jnp.einsum("cghp,gk->cghkp", a, eye).reshape(S5_BLOCKS, 128, S5_BSTATE)

    return jnp.concatenate([one(re), sign_im * one(im)], axis=-1)


def _s5_undense(d):
    eye = jnp.eye(8, dtype=F32)

    def one(a):
        a = a.reshape(S5_BLOCKS, 8, S5_GROUP, 8, S5_STATE)
        return jnp.einsum("cghkp,gk->cghp", a, eye).reshape(S5_GROUPS, S5_GROUP, S5_STATE)

    return one(d[..., :S5_BSTATE]), one(d[..., S5_BSTATE:])


def _local_step(x, target, p, w, rows=None):
    w = dict(w or {})
    landed_grads = {}

    def gather(names):
        return None if rows is None else ([rows[n] for n in names], False)

    def gathered(names, landed):
        w.update({n: _full_weight(n, g) for n, g in zip(names, landed)})

    def scatter(names):
        return None if rows is None else ([_grad_slabs(n, big[n]) for n in names], True)

    def scattered(names, landed):
        landed_grads.update(zip(names, landed))

    if rows is not None:
        gathered(GROUPS[0], _exchange(gather(GROUPS[0])[0], False, "gather_ffn1"))
    g1, gm, g2 = p["ffn1_norm"], p["mix_norm"], p["ffn2_norm"]
    gf = p["final_norm"].reshape(1, D_MODEL)
    lre, lim = p["s5_lambda_re"][0], p["s5_lambda_im"][0]
    ldt = p["s5_log_dt"][0].reshape(S5_GROUPS, 1)
    bre = p["s5_b_re"][0].transpose(0, 2, 1)
    bim = p["s5_b_im"][0].transpose(0, 2, 1)
    cre, cim = p["s5_c_re"][0], p["s5_c_im"][0]
    dskip = p["s5_d"][0].reshape(1, S5_WIDTH)
    bg, bup, gn = p["s5_glu_b"], p["gla_a_up_b"], p["gla_out_norm"]

    mix_first, mix_rest = ("w_in", "gla_a_up_w"), ("s5_glu_w", "proj_s5", "proj_gla", "w_out")
    h1, (a3_1, b3_1, n1), got = _ffn_fwd(x, g1, w["ffn1_w1"], w["ffn1_w3"], w["ffn1_w2"], "ffn1_fwd",
                                         gather(mix_first))
    gathered(mix_first, got)
    wup = w["gla_a_up_w"].astype(F32)
    (u, s5in, q, k, v, r, alow, gs5, ggla), got = _mix_pre_fwd(h1, gm, w["w_in"], gather(mix_rest))
    gathered(mix_rest, got)
    ar, ai, bbr, bbi = _s5_disc(lre, lim, ldt, bre, bim)
    bd = _s5_dense(bbr, bbi, 1.0)
    cd = _s5_dense(cre, cim, -1.0)
    bd16, cd16 = bd.astype(BF16), cd.astype(BF16)
    bdt16, ctd16 = bd16.transpose(0, 2, 1), cd16.transpose(0, 2, 1)
    ar4 = ar.reshape(S5_BLOCKS, 1, S5_BSTATE)
    ai4 = ai.reshape(S5_BLOCKS, 1, S5_BSTATE)
    (xs, y), got = _s5_fwd(s5in, bd16, ctd16, ar4, ai4, dskip, gather(GROUPS[2][:1]))
    gathered(GROUPS[2][:1], got)
    (o, ssave), got = _gla_fwd(q, k, v, alow, wup, bup, gather(GROUPS[2][1:2]))
    gathered(GROUPS[2][1:2], got)
    post_w = (w["s5_glu_w"], bg, gn, w["proj_s5"], w["proj_gla"], w["w_out"])
    h2, got = _mix_post_fwd(y, o, r, gs5, ggla, h1, *post_w, carry=gather(GROUPS[2][2:]))
    gathered(GROUPS[2][2:], got)
    h3, (a3_2, b3_2, n2), _ = _ffn_fwd(h2, g2, w["ffn2_w1"], w["ffn2_w3"], w["ffn2_w2"], "ffn2_fwd")
    loss, dh3, dgf = _head(h3, gf, target)

    big, small = {}, {}
    small["final_norm"] = dgf.reshape(D_MODEL)
    (dh2, dg2, da3, db3, s3, dhh2), _ = _ffn_bwd(
        h2, dh3, g2, a3_2, b3_2, w["ffn2_w1"], w["ffn2_w3"], w["ffn2_w2"], "ffn2_bwd")
    small["ffn2_norm"] = dg2
    big["ffn2_w1"] = _mm_tn(da3, n2, "ffn2_dw1")
    big["ffn2_w3"] = _mm_tn(db3, n2, "ffn2_dw3")
    big["ffn2_w2"] = _mm_tn(s3, dhh2, "ffn2_dw2")
    (dy, do, dr, dgs5, dggla, dbg, dgn, z5b, dgpb, ys5b, dm5b, yglab, dmgb, mergedb, dh2b), got = _mix_post_bwd(
        y, o, r, gs5, ggla, dh2, *post_w, carry=scatter(GROUPS[2][:1]))
    scattered(GROUPS[2][:1], got)
    small["s5_glu_b"] = dbg
    small["gla_out_norm"] = dgn
    big["s5_glu_w"] = _mm_tn(z5b, dgpb, "glu_dw")
    big["proj_s5"] = _mm_tn(dm5b, ys5b, "proj_s5_dw")
    big["proj_gla"] = _mm_tn(dmgb, yglab, "proj_gla_dw")
    big["w_out"] = _mm_tn(mergedb, dh2b, "w_out_dw")
    (dq, dk, dv, dalow, dwup, dbup), got = _gla_bwd(q, k, v, alow, wup, bup, ssave, do, scatter(GROUPS[2][1:2]))
    scattered(GROUPS[2][1:2], got)
    big["gla_a_up_w"] = dwup
    small["gla_a_up_b"] = dbup
    (ds5in, dbd, dcd, dd, dar4, dai4), got = _s5_bwd(
        dy, s5in, xs, cd16, bdt16, ar4, ai4, dskip, scatter(GROUPS[2][2:]))
    scattered(GROUPS[2][2:], got)
    dbbr, dbbi = _s5_undense(dbd)
    dcre, dcim_neg = _s5_undense(dcd)
    glre, glim, gldt, gbre, gbim = _s5_disc_bwd(
        lre, lim, ldt, bre, bim, dar4.reshape(S5_GROUPS, S5_STATE), dai4.reshape(S5_GROUPS, S5_STATE),
        dbbr, dbbi)
    small["s5_lambda_re"] = glre[None]
    small["s5_lambda_im"] = glim[None]
    small["s5_log_dt"] = gldt.reshape(1, S5_GROUPS)
    small["s5_b_re"] = gbre
    small["s5_b_im"] = gbim
    small["s5_c_re"] = dcre[None]
    small["s5_c_im"] = -dcim_neg[None]
    small["s5_d"] = dd.reshape(1, S5_GROUPS, S5_GROUP)
    dz = (ds5in, dq, dk, dv, dr, dalow, dgs5, dggla)
    dh1, dgm = _mix_pre_bwd(h1, gm, w["w_in"], dh2, dz)
    small["mix_norm"] = dgm
    big["w_in"] = jnp.concatenate([_mm_tn(d, u, "w_in_dw%d" % i) for i, d in enumerate(dz)], axis=0)
    (dx, dg1, da3, db3, s3, dhh1), got = _ffn_bwd(
        x, dh1, g1, a3_1, b3_1, w["ffn1_w1"], w["ffn1_w3"], w["ffn1_w2"], "ffn1_bwd", scatter(GROUPS[1]))
    scattered(GROUPS[1], got)
    small["ffn1_norm"] = dg1
    big["ffn1_w1"] = _mm_tn(da3, n1, "ffn1_dw1")
    if rows is None:
        big["ffn1_w3"] = _mm_tn(db3, n1, "ffn1_dw3")
        big["ffn1_w2"] = _mm_tn(s3, dhh1, "ffn1_dw2")
        return loss[0, 0], dx, big, small
    big["ffn1_w3"], got = _mm_tn(db3, n1, "ffn1_dw3", scatter(GROUPS[0][:1]))
    scattered(GROUPS[0][:1], got)
    big["ffn1_w2"], got = _mm_tn(s3, dhh1, "ffn1_dw2", scatter(GROUPS[0][1:2]))
    scattered(GROUPS[0][1:2], got)
    scattered(GROUPS[0][2:], _exchange(scatter(GROUPS[0][2:])[0], True, "scatter_ffn1_w2"))
    return loss[0, 0], dx, landed_grads, small


NAMES = ("ffn1_norm", "ffn1_w1", "ffn1_w3", "ffn1_w2", "mix_norm", "w_in", "s5_lambda_re", "s5_lambda_im",
         "s5_log_dt", "s5_b_re", "s5_b_im", "s5_c_re", "s5_c_im", "s5_d", "s5_glu_w", "s5_glu_b", "gla_a_up_w",
         "gla_a_up_b", "gla_out_norm", "proj_s5", "proj_gla", "w_out", "ffn2_norm", "ffn2_w1", "ffn2_w3", "ffn2_w2",
         "final_norm")


def kernel(*args):
    nw = len(NAMES)
    x = args[0][0]
    wts = dict(zip(NAMES, args[1:1 + nw]))
    target = args[1 + nw][0]
    mom = dict(zip(NAMES, args[2 + nw:2 + 2 * nw]))
    var = dict(zip(NAMES, args[2 + 2 * nw:2 + 3 * nw]))

    shards = {n: wts[n][0] for n in BIG}
    rows = {n: _shard_rows(n, shards[n]).astype(BF16) for n in BIG}
    loss, dx, landed, small = _local_step(x, target, {n: wts[n] for n in SMALL}, None, rows)
    loss = lax.psum(loss, ("x", "y", "c"))

    grad, delta, new_m, new_v = {}, {}, {}, {}
    for n in BIG:
        g_rows = _sum_slabs(landed[n], "sum_" + n)
        if n in ROW_ADAM:
            g = g_rows[:W_IN_ROWS] if n == "w_in" else g_rows
            outs = _adamw(shards[n].T, g, mom[n][0].T, var[n][0].T, "adamw_" + n)
            grad[n], delta[n], new_m[n], new_v[n] = (a.T[None] for a in (g, *outs))
        else:
            g = _unshard_rows(n, g_rows, shards[n].shape)
            outs = _adamw(shards[n], g, mom[n][0], var[n][0], "adamw_" + n)
            grad[n], delta[n], new_m[n], new_v[n] = (a[None] for a in (g, *outs))

    part = _pack_small(small).reshape(N_DEV, SMALL_R // N_DEV, 1024)
    mine = _sum_slabs(_exchange([part], True, "scatter_small")[0], "sum_small")
    g_small = _unpack_small(_exchange([mine], False, "gather_small")[0].reshape(SMALL_R, 1024))

    def flat2d(a):
        return a.reshape(-1, a.shape[-1])

    operands = ([flat2d(_working(n, d[n])) for n in SMALL] for d in (wts, mom, var))
    w2d, m2d, v2d = operands
    outs = _adamw_many(w2d, [flat2d(g_small[n]) for n in SMALL], m2d, v2d, "adamw_small")
    for out, arrays in zip((grad, delta, new_m, new_v), ([g_small[n] for n in SMALL], *outs)):
        out.update({n: _declared(n, a.reshape(g_small[n].shape)) for n, a in zip(SMALL, arrays)})
    return (loss, dx[None], *(d[n] for d in (grad, delta, new_m, new_v) for n in NAMES))
```

```python
import functools
import math

import jax
import jax.numpy as jnp
from jax import lax
from jax.experimental import pallas as pl
from jax.experimental.pallas import tpu as pltpu

F32, BF16 = jnp.float32, jnp.bfloat16
HIGHEST = lax.Precision.HIGHEST

D_MODEL = 1024
D_FF = 2816
N_DEV = 8
S5_WIDTH, S5_GROUPS, S5_GROUP, S5_STATE = 512, 32, 16, 64
S5_BLOCKS = 4
S5_BSTATE = 512
S5_SEGS = 8
GLA_HEADS, GLA_DK, GLA_DV = 4, 64, 128
GLA_KEY, GLA_VAL, GLA_RANK, GLA_CHUNK = 256, 512, 16, 64
GLA_TAU = 16.0
GLA_STEP_CHUNKS = 4
EPS = 1e-6
IN_SIZES = (512, 256, 256, 512, 512, 16, 1024, 1024)
IN_OFFS = tuple(sum(IN_SIZES[:i]) for i in range(len(IN_SIZES)))
IN_COLS = sum(IN_SIZES)
ADAM_LR, ADAM_B1, ADAM_B2, ADAM_EPS, ADAM_WD, ADAM_STEP = 0.001, 0.9, 0.999, 1e-08, 0.01, 10
GELU_C0 = math.sqrt(2.0 / math.pi)
GELU_C1 = 0.044715

FFN_FT = 256
VMEM_LIMIT_BYTES = 56 * 1024 * 1024

VMEM_FULL = pl.BlockSpec(memory_space=pltpu.VMEM)
ANY = pl.BlockSpec(memory_space=pl.ANY)


def _cparams(n_grid):
    return pltpu.CompilerParams(dimension_semantics=("arbitrary",) * n_grid, vmem_limit_bytes=VMEM_LIMIT_BYTES)


def _tile(t):
    return 512 if t >= 1024 else t // 2


def _nn(a, b):
    return jnp.dot(a, b, preferred_element_type=F32)


def _nt(a, b):
    return lax.dot_general(a, b, (((1,), (1,)), ((), ())), preferred_element_type=F32)


def _tn(a, b):
    return lax.dot_general(a, b, (((0,), (0,)), ((), ())), preferred_element_type=F32)


def _rms_parts(x):
    r = lax.rsqrt(jnp.mean(x * x, axis=-1, keepdims=True) + EPS)
    return x * r, r


def _rms_bwd(dn, g, xhat, r):
    dxh = dn * g
    dx = r * (dxh - xhat * jnp.mean(dxh * xhat, axis=-1, keepdims=True))
    return dx, jnp.sum(dn * xhat, axis=0, keepdims=True)


def _peers():
    x, y, c = lax.axis_index("x"), lax.axis_index("y"), lax.axis_index("c")
    out = []
    for k in range(1, N_DEV):
        px = 1 - x if k & 4 else x
        py = 1 - y if k & 2 else y
        pc = 1 - c if k & 1 else c
        out.append(((px, py, pc), 4 * px + 2 * py + pc))
    return 4 * x + 2 * y + c, out


def _exchange_copies(src_refs, out_refs, send_sems, recv_sems, local_sems, scatter, with_recvs):
    me, peers = _peers()
    locals_, sends, recvs = [], [], []
    for a, (src_ref, out_ref) in enumerate(zip(src_refs, out_refs)):
        def mine(idx, src_ref=src_ref):
            return src_ref.at[idx] if scatter else src_ref

        locals_.append(pltpu.make_async_copy(mine(me), out_ref.at[me], local_sems.at[a]))
        for k, (dev, idx) in enumerate(peers):
            sends.append(pltpu.make_async_remote_copy(
                src_ref=mine(idx), dst_ref=out_ref.at[me], send_sem=send_sems.at[a, k], recv_sem=recv_sems.at[a, k],
                device_id=dev, device_id_type=pl.DeviceIdType.MESH))
            if with_recvs:
                recvs.append(pltpu.make_async_remote_copy(
                    src_ref=mine(idx), dst_ref=out_ref.at[idx], send_sem=send_sems.at[a, k],
                    recv_sem=recv_sems.at[a, k], device_id=dev, device_id_type=pl.DeviceIdType.MESH))
    return locals_, sends, recvs


def _remote(src, dst, send_sems, recv_sems, a, k, dev):
    return pltpu.make_async_remote_copy(src_ref=src, dst_ref=dst, send_sem=send_sems.at[a, k],
                                        recv_sem=recv_sems.at[a, k], device_id=dev,
                                        device_id_type=pl.DeviceIdType.MESH)


def _gather_places():
    x, y, c = lax.axis_index("x"), lax.axis_index("y"), lax.axis_index("c")
    chips = [(1 - x, y), (x, 1 - y), (1 - x, 1 - y)]
    sibling = (x, y, 1 - c)
    me_idx, sib_idx = 4 * x + 2 * y + c, 4 * x + 2 * y + 1 - c
    same_core = [((cx, cy, c), 4 * cx + 2 * cy + c) for cx, cy in chips]
    other_core_idx = [4 * cx + 2 * cy + 1 - c for cx, cy in chips]
    return sibling, me_idx, sib_idx, same_core, other_core_idx


def _gather_start(src_refs, out_refs, send_sems, recv_sems, local_sems):
    sibling, me_idx, _, same_core, _ = _gather_places()
    for a, (src, out) in enumerate(zip(src_refs, out_refs)):
        pltpu.make_async_copy(src, out.at[me_idx], local_sems.at[a]).start()
        _remote(src, out.at[me_idx], send_sems, recv_sems, a, 0, sibling).start()
        for j, (dev, _) in enumerate(same_core):
            _remote(src, out.at[me_idx], send_sems, recv_sems, a, 1 + j, dev).start()


def _gather_finish(src_refs, out_refs, send_sems, recv_sems, local_sems):
    sibling, me_idx, sib_idx, same_core, other_core_idx = _gather_places()
    arrays = list(enumerate(zip(src_refs, out_refs)))
    forwards = []
    for a, (src, out) in arrays:
        for j, (dev, idx) in enumerate(same_core):
            _remote(src, out.at[idx], send_sems, recv_sems, a, 1 + j, dev).wait_recv()
            fwd = _remote(out.at[idx], out.at[idx], send_sems, recv_sems, a, 4 + j, sibling)
            fwd.start()
            forwards.append(fwd)
    for a, (src, out) in arrays:
        _remote(src, out.at[sib_idx], send_sems, recv_sems, a, 0, sibling).wait_recv()
        for j, idx in enumerate(other_core_idx):
            _remote(src, out.at[idx], send_sems, recv_sems, a, 4 + j, sibling).wait_recv()
        _remote(src, out.at[me_idx], send_sems, recv_sems, a, 0, sibling).wait_send()
        for j, (dev, _) in enumerate(same_core):
            _remote(src, out.at[me_idx], send_sems, recv_sems, a, 1 + j, dev).wait_send()
        pltpu.make_async_copy(src, out.at[me_idx], local_sems.at[a]).wait()
    for fwd in forwards:
        fwd.wait_send()


def _exchange_start(*refs, scatter):
    if not scatter:
        return _gather_start(*refs)
    locals_, sends, _ = _exchange_copies(*refs, scatter=scatter, with_recvs=False)
    for cp in locals_ + sends:
        cp.start()


def _exchange_wait(*refs, scatter):
    if not scatter:
        return _gather_finish(*refs)
    locals_, sends, recvs = _exchange_copies(*refs, scatter=scatter, with_recvs=True)
    for cp in recvs:
        cp.wait_recv()
    for cp in sends:
        cp.wait_send()
    for cp in locals_:
        cp.wait()


def _halves_places():
    x, y, c = lax.axis_index("x"), lax.axis_index("y"), lax.axis_index("c")
    flips = [(1 - x, y), (x, 1 - y), (1 - x, 1 - y)]
    return (x, y, 1 - c), c, 2 * x + y, [((fx, fy, c), 2 * fx + fy) for fx, fy in flips]


def _pair_start(src_refs, out_refs, send_sems, recv_sems, local_sems):
    sibling, c, _, _ = _halves_places()
    for a, (src, out) in enumerate(zip(src_refs, out_refs)):
        for i in range(4):
            _remote(src.at[2 * i + 1 - c], out.at[i], send_sems, recv_sems, a, i, sibling).start()


def _pair_finish(src_refs, out_refs, send_sems, recv_sems, local_sems):
    sibling, c, _, _ = _halves_places()
    for a, (src, out) in enumerate(zip(src_refs, out_refs)):
        for i in range(4):
            _remote(src.at[2 * i + 1 - c], out.at[i], send_sems, recv_sems, a, i, sibling).wait()


def _chips_start(src_refs, out_refs, send_sems, recv_sems, local_sems):
    _, _, chip, others = _halves_places()
    for a, (src, out) in enumerate(zip(src_refs, out_refs)):
        pltpu.make_async_copy(src.at[chip], out.at[chip], local_sems.at[a]).start()
        for k, (dev, their_chip) in enumerate(others):
            _remote(src.at[their_chip], out.at[chip], send_sems, recv_sems, a, k, dev).start()


def _chips_finish(src_refs, out_refs, send_sems, recv_sems, local_sems):
    _, _, chip, others = _halves_places()
    for a, (src, out) in enumerate(zip(src_refs, out_refs)):
        for k, (dev, their_chip) in enumerate(others):
            _remote(src.at[their_chip], out.at[their_chip], send_sems, recv_sems, a, k, dev).wait_recv()
        for k, (dev, their_chip) in enumerate(others):
            _remote(src.at[their_chip], out.at[chip], send_sems, recv_sems, a, k, dev).wait_send()
        pltpu.make_async_copy(src.at[chip], out.at[chip], local_sems.at[a]).wait()


EXCHANGES = {
    "gather": (functools.partial(_exchange_start, scatter=False), functools.partial(_exchange_wait, scatter=False),
               N_DEV, False),
    "scatter": (functools.partial(_exchange_start, scatter=True), functools.partial(_exchange_wait, scatter=True),
                N_DEV, True),
    "pair": (_pair_start, _pair_finish, 4, True),
    "chips": (_chips_start, _chips_finish, 4, True),
}


def _exchange_sems(n_arrays):
    return [pltpu.SemaphoreType.DMA((n_arrays, N_DEV - 1)), pltpu.SemaphoreType.DMA((n_arrays, N_DEV - 1)),
            pltpu.SemaphoreType.DMA((n_arrays,))]


def _exchange_shapes(srcs, kind):
    lead, slabbed = EXCHANGES[kind][2:]
    return [jax.ShapeDtypeStruct((lead,) + tuple(s.shape[1:] if slabbed else s.shape), s.dtype) for s in srcs]


def _carries(carry):
    if carry is None:
        return []
    return [carry] if isinstance(carry, tuple) else list(carry)


def _call(body, *, name, grid, in_specs, out_specs, out_shape, args, scratch_shapes=(), carry=None):
    n_in, n_out, n_scr = len(in_specs), len(out_specs), len(scratch_shapes)
    groups = _carries(carry)
    sizes = [len(arrays) for arrays, _ in groups]
    nc = sum(sizes)

    def wrapped(*refs):
        ins, refs = refs[:n_in], refs[n_in:]
        csrc, refs = refs[:nc], refs[nc:]
        outs, refs = refs[:n_out], refs[n_out:]
        cland, refs = refs[:nc], refs[nc:]
        scr, sems = refs[:n_scr], refs[n_scr:]

        def run(phase):
            at = 0
            for gi, ((_, kind), size) in enumerate(zip(groups, sizes)):
                EXCHANGES[kind][phase](csrc[at:at + size], cland[at:at + size], *sems[3 * gi:3 * gi + 3])
                at += size

        if nc:
            pl.when(pl.program_id(0) == 0)(functools.partial(run, 0))
        if body is not None:
            body(*ins, *outs, *scr)
        if nc:
            pl.when(pl.program_id(0) == grid[0] - 1)(functools.partial(run, 1))

    res = pl.pallas_call(
        wrapped, name=name, grid=grid,
        in_specs=list(in_specs) + [ANY] * nc, out_specs=list(out_specs) + [ANY] * nc,
        out_shape=list(out_shape) + [s for arrays, kind in groups for s in _exchange_shapes(arrays, kind)],
        scratch_shapes=list(scratch_shapes) + [s for size in sizes for s in _exchange_sems(size)],
        compiler_params=_cparams(1),
    )(*args, *[a for arrays, _ in groups for a in arrays])
    return res[:n_out], res[n_out:]


def _row_tile(tm, d):
    return pl.BlockSpec((tm, d), lambda i: (i, 0))


def _acc_row(d):
    return pl.BlockSpec((1, d), lambda i: (0, 0))


def _ffn_fwd(x, g, w1t, w3t, w2, name, carry=None):
    t = x.shape[0]
    tm = _tile(t)
    nf = D_FF // FFN_FT

    def body(x_ref, g_ref, w1_ref, w3_ref, w2_ref, o_ref, a_ref, b_ref, n_ref):
        xv = x_ref[...]
        xhat, _ = _rms_parts(xv)
        n = (xhat * g_ref[...]).astype(BF16)
        n_ref[...] = n
        o_ref[...] = xv

        def fstep(f, c):
            rows = pl.ds(pl.multiple_of(f * FFN_FT, FFN_FT), FFN_FT)
            a = _nt(n, w1_ref[rows, :])
            b = _nt(n, w3_ref[rows, :])
            a_ref[f] = a.astype(BF16)
            b_ref[f] = b.astype(BF16)
            s = (a * jax.nn.sigmoid(a) * b).astype(BF16)
            o_ref[...] += 0.5 * _nn(s, w2_ref[rows, :])
            return c

        lax.fori_loop(0, nf, fstep, 0, unroll=True)

    blk3 = pl.BlockSpec((nf, tm, FFN_FT), lambda i: (0, i, 0))
    sh3 = jax.ShapeDtypeStruct((nf, t, FFN_FT), BF16)
    (h, a3, b3, n), landed = _call(
        body, name=name, grid=(t // tm,),
        in_specs=[_row_tile(tm, D_MODEL), _acc_row(D_MODEL), VMEM_FULL, VMEM_FULL, VMEM_FULL],
        out_specs=[_row_tile(tm, D_MODEL), blk3, blk3, _row_tile(tm, D_MODEL)],
        out_shape=[jax.ShapeDtypeStruct((t, D_MODEL), F32), sh3, sh3, jax.ShapeDtypeStruct((t, D_MODEL), BF16)],
        args=(x, g, w1t, w3t, w2), carry=carry)
    return h, (a3, b3, n), landed


def _ffn_bwd(x, dh, g, a3, b3, w1t, w3t, w2, name, carry=None):
    t = x.shape[0]
    tm = _tile(t) // 2
    nf = D_FF // FFN_FT

    def body(x_ref, dh_ref, g_ref, a_ref, b_ref, w1_ref, w3_ref, w2_ref,
             dx_ref, dg_ref, da_ref, db_ref, s_ref, dhh_ref, dn_acc):
        i = pl.program_id(0)
        xv = x_ref[...]
        gv = g_ref[...]
        xhat, r = _rms_parts(xv)
        dhv = dh_ref[...]
        dhh = (0.5 * dhv).astype(BF16)
        dhh_ref[...] = dhh
        dn_acc[...] = jnp.zeros_like(dn_acc)

        def fstep(f, c):
            rows = pl.ds(pl.multiple_of(f * FFN_FT, FFN_FT), FFN_FT)
            w1c, w3c, w2c = w1_ref[rows, :], w3_ref[rows, :], w2_ref[rows, :]
            a = a_ref[f].astype(F32)
            b = b_ref[f].astype(F32)
            sg = jax.nn.sigmoid(a)
            sl = a * sg
            ds = _nt(dhh, w2c)
            da = (ds * b * sg * (1.0 + a * (1.0 - sg))).astype(BF16)
            db = (ds * sl).astype(BF16)
            s_ref[f] = (sl * b).astype(BF16)
            da_ref[f] = da
            db_ref[f] = db
            dn_acc[...] += _nn(da, w1c) + _nn(db, w3c)
            return c

        lax.fori_loop(0, nf, fstep, 0, unroll=True)
        dx, dg = _rms_bwd(dn_acc[...], gv, xhat, r)
        dx_ref[...] = dhv + dx

        @pl.when(i == 0)
        def _():
            dg_ref[...] = jnp.zeros_like(dg_ref)

        dg_ref[...] += dg

    blk3 = pl.BlockSpec((nf, tm, FFN_FT), lambda i: (0, i, 0))
    sh3 = jax.ShapeDtypeStruct((nf, t, FFN_FT), BF16)
    return _call(
        body, name=name, grid=(t // tm,),
        in_specs=[_row_tile(tm, D_MODEL), _row_tile(tm, D_MODEL), _acc_row(D_MODEL), blk3, blk3,
                  VMEM_FULL, VMEM_FULL, VMEM_FULL],
        out_specs=[_row_tile(tm, D_MODEL), _acc_row(D_MODEL), blk3, blk3, blk3, _row_tile(tm, D_MODEL)],
        out_shape=[jax.ShapeDtypeStruct((t, D_MODEL), F32), jax.ShapeDtypeStruct((1, D_MODEL), F32), sh3, sh3, sh3,
                   jax.ShapeDtypeStruct((t, D_MODEL), BF16)],
        scratch_shapes=[pltpu.VMEM((tm, D_MODEL), F32)],
        args=(x, dh, g, a3, b3, w1t, w3t, w2), carry=carry)


def _mm_tn(a, b, name, carry=None):
    t, n = b.shape
    kc = min(512, t)
    if a.ndim == 3:
        nb, _, tb = a.shape
        a_spec = pl.BlockSpec((1, t, tb), lambda i: (i, 0, 0))
    else:
        m = a.shape[1]
        tb = min(m, 256)
        nb = m // tb
        a_spec = pl.BlockSpec((t, tb), lambda i: (0, i))
    three_d = a.ndim == 3

    def body(a_ref, b_ref, o_ref, acc):
        acc[...] = jnp.zeros_like(acc)

        def kstep(k, c):
            rows = pl.ds(pl.multiple_of(k * kc, kc), kc)
            av = a_ref[0, rows, :] if three_d else a_ref[rows, :]
            acc[...] += _tn(av.astype(BF16), b_ref[rows, :])
            return c

        lax.fori_loop(0, t // kc, kstep, 0, unroll=True)
        o_ref[...] = acc[...].astype(BF16)

    (out,), landed = _call(
        body, name=name, grid=(nb,),
        in_specs=[a_spec, VMEM_FULL],
        out_specs=[pl.BlockSpec((tb, n), lambda i: (i, 0))],
        out_shape=[jax.ShapeDtypeStruct((nb * tb, n), BF16)],
        scratch_shapes=[pltpu.VMEM((tb, n), F32)],
        args=(a, b), carry=carry)
    return (out, landed) if carry is not None else out


def _mix_pre_fwd(h, g, wint, carry=None):
    t = h.shape[0]
    tm = _tile(t)

    def body(h_ref, g_ref, w_ref, u_ref, *outs):
        xhat, _ = _rms_parts(h_ref[...])
        u = (xhat * g_ref[...]).astype(BF16)
        u_ref[...] = u
        for o_ref, off, size in zip(outs, IN_OFFS, IN_SIZES):
            o_ref[...] = _nt(u, w_ref[off:off + size, :])

    return _call(
        body, name="mix_pre_fwd", grid=(t // tm,),
        in_specs=[_row_tile(tm, D_MODEL), _acc_row(D_MODEL), VMEM_FULL],
        out_specs=[_row_tile(tm, D_MODEL)] + [_row_tile(tm, s) for s in IN_SIZES],
        out_shape=[jax.ShapeDtypeStruct((t, D_MODEL), BF16)] + [jax.ShapeDtypeStruct((t, s), F32) for s in IN_SIZES],
        args=(h, g, wint), carry=carry)


def _mix_pre_bwd(h, g, wint, dh2, dz, carry=None):
    t = h.shape[0]
    tm = _tile(t)

    def body(h_ref, g_ref, w_ref, dh2_ref, *rest):
        dz_refs, (dh1_ref, dg_ref) = rest[:len(IN_SIZES)], rest[len(IN_SIZES):]
        i = pl.program_id(0)
        gv = g_ref[...]
        xhat, r = _rms_parts(h_ref[...])
        du = jnp.zeros((tm, D_MODEL), F32)
        for dz_ref, off, size in zip(dz_refs, IN_OFFS, IN_SIZES):
            du = du + _nn(dz_ref[...].astype(BF16), w_ref[off:off + size, :])
        dx, dg = _rms_bwd(du, gv, xhat, r)
        dh1_ref[...] = dh2_ref[...] + dx

        @pl.when(i == 0)
        def _():
            dg_ref[...] = jnp.zeros_like(dg_ref)

        dg_ref[...] += dg

    return _call(
        body, name="mix_pre_bwd", grid=(t // tm,),
        in_specs=[_row_tile(tm, D_MODEL), _acc_row(D_MODEL), VMEM_FULL, _row_tile(tm, D_MODEL)]
        + [_row_tile(tm, s) for s in IN_SIZES],
        out_specs=[_row_tile(tm, D_MODEL), _acc_row(D_MODEL)],
        out_shape=[jax.ShapeDtypeStruct((t, D_MODEL), F32), jax.ShapeDtypeStruct((1, D_MODEL), F32)],
        args=(h, g, wint, dh2, *dz), carry=carry)


def _disc_math(lre, lim, ldt, bre, bim):
    dt = jnp.exp(ldt)
    mag = jnp.exp(lre * dt)
    ar = mag * jnp.cos(lim * dt)
    ai = mag * jnp.sin(lim * dt)
    den = lre * lre + lim * lim
    nr = ar - 1.0
    fr = (nr * lre + ai * lim) / den
    fi = (ai * lre - nr * lim) / den
    fr, fi = fr[:, None, :], fi[:, None, :]
    return ar, ai, fr * bre - fi * bim, fr * bim + fi * bre


def _s5_disc(lre, lim, ldt, bre, bim):
    def body(lre_ref, lim_ref, ldt_ref, bre_ref, bim_ref, ar_ref, ai_ref, bbr_ref, bbi_ref):
        ar, ai, bbr, bbi = _disc_math(lre_ref[...], lim_ref[...], ldt_ref[...], bre_ref[...], bim_ref[...])
        ar_ref[...] = ar
        ai_ref[...] = ai
        bbr_ref[...] = bbr
        bbi_ref[...] = bbi

    small = jax.ShapeDtypeStruct(lre.shape, F32)
    big = jax.ShapeDtypeStruct(bre.shape, F32)
    return pl.pallas_call(body, name="s5_disc", out_shape=[small, small, big, big],
                          in_specs=[VMEM_FULL] * 5, out_specs=[VMEM_FULL] * 4)(lre, lim, ldt, bre, bim)


def _s5_disc_bwd(lre, lim, ldt, bre, bim, dar, dai, dbbr, dbbi):
    def body(lre_ref, lim_ref, ldt_ref, bre_ref, bim_ref, dar_ref, dai_ref, dbbr_ref, dbbi_ref,
             glre_ref, glim_ref, gldt_ref, gbre_ref, gbim_ref):
        _, vjp = jax.vjp(_disc_math, lre_ref[...], lim_ref[...], ldt_ref[...], bre_ref[...], bim_ref[...])
        glre, glim, gldt, gbre, gbim = vjp((dar_ref[...], dai_ref[...], dbbr_ref[...], dbbi_ref[...]))
        glre_ref[...] = glre
        glim_ref[...] = glim
        gldt_ref[...] = gldt
        gbre_ref[...] = gbre
        gbim_ref[...] = gbim

    small = jax.ShapeDtypeStruct(lre.shape, F32)
    big = jax.ShapeDtypeStruct(bre.shape, F32)
    return pl.pallas_call(body, name="s5_disc_bwd",
                          out_shape=[small, small, jax.ShapeDtypeStruct(ldt.shape, F32), big, big],
                          in_specs=[VMEM_FULL] * 9, out_specs=[VMEM_FULL] * 5,
                          )(lre, lim, ldt, bre, bim, dar, dai, dbbr, dbbi)


def _cmul(ar, ai, br, bi):
    return ar * br - ai * bi, ar * bi + ai * br


def _cpow(ar, ai, n):
    rr, ri = None, None
    pr, pi = ar, ai
    while n:
        if n & 1:
            rr, ri = (pr, pi) if rr is None else _cmul(rr, ri, pr, pi)
        n >>= 1
        if n:
            pr, pi = _cmul(pr, pi, pr, pi)
    return rr, ri


def _shift_rows(v, down):
    row = lax.broadcasted_iota(jnp.int32, v.shape, 0)
    if down:
        return jnp.where(row == 0, 0.0, pltpu.roll(v, 1, 0))
    return jnp.where(row == S5_SEGS - 1, 0.0, pltpu.roll(v, S5_SEGS - 1, 0))


def _chain_segments(er, ei, pr, pi, down):
    fr, fi = er, ei
    for _ in range(S5_SEGS - 1):
        sr, si = _shift_rows(fr, down), _shift_rows(fi, down)
        mr, mi = _cmul(pr, pi, sr, si)
        fr, fi = er + mr, ei + mi
    return _shift_rows(fr, down), _shift_rows(fi, down)


def _rows_to_scan_order(src_ref, dst_ref, t):
    ls = t // S5_SEGS

    def tile(j, c):
        dst_ref[pl.ds(pl.multiple_of(j * S5_SEGS, S5_SEGS), S5_SEGS), :] = src_ref[pl.ds(j, S5_SEGS, stride=ls), :]
        return c

    lax.fori_loop(0, ls, tile, 0, unroll=8)


def _rows_from_scan_order(src_ref, dst_ref, t):
    ls = t // S5_SEGS
    for s in range(S5_SEGS):
        def tile(jb, c, s=s):
            dst_ref[pl.ds(pl.multiple_of(s * ls + jb * 8, 8), 8), :] = (
                src_ref[pl.ds(jb * 8 * S5_SEGS + s, 8, stride=S5_SEGS), :])
            return c

        lax.fori_loop(0, ls // 8, tile, 0, unroll=8)


def _s5_fwd(ug, bd, ctd, ar4, ai4, dskip, carry=None):
    t = ug.shape[0]
    ls = t // S5_SEGS
    rc = min(512, t)
    ns = S5_BSTATE

    def body(ugn_ref, bd_ref, ct_ref, ar_ref, ai_ref, d_ref, xs_hbm, yn_ref, buf, ug_ref, y_ref, sem):
        cb = pl.program_id(0)
        bdv = bd_ref[0]
        _rows_to_scan_order(ugn_ref, ug_ref, t)

        def mm(i, c):
            rows = pl.ds(pl.multiple_of(i * rc, rc), rc)
            buf[rows, :] = _nn(ug_ref[rows, :].astype(BF16), bdv)
            return c

        lax.fori_loop(0, t // rc, mm, 0, unroll=True)
        arb = jnp.broadcast_to(ar_ref[0], (S5_SEGS, ns))
        aib = jnp.broadcast_to(ai_ref[0], (S5_SEGS, ns))

        def step(j, c, store):
            sr, si = c
            rows = pl.ds(pl.multiple_of(j * S5_SEGS, S5_SEGS), S5_SEGS)
            nr = arb * sr - aib * si + buf[rows, 0:ns]
            ni = arb * si + aib * sr + buf[rows, ns:2 * ns]
            if store:
                buf[rows, 0:ns] = nr
                buf[rows, ns:2 * ns] = ni
            return nr, ni

        zero = jnp.zeros((S5_SEGS, ns), F32)
        er, ei = lax.fori_loop(0, ls, functools.partial(step, store=False), (zero, zero))
        pr, pi = _cpow(arb, aib, ls)
        init = _chain_segments(er, ei, pr, pi, down=True)
        lax.fori_loop(0, ls, functools.partial(step, store=True), init)

        out = pltpu.make_async_copy(buf, xs_hbm.at[cb], sem)
        out.start()
        ctv = ct_ref[0]
        dv = d_ref[...]

        def ymm(i, c):
            rows = pl.ds(pl.multiple_of(i * rc, rc), rc)
            y_ref[rows, :] = _nn(buf[rows, :].astype(BF16), ctv) + dv * ug_ref[rows, :]
            return c

        lax.fori_loop(0, t // rc, ymm, 0, unroll=True)
        _rows_from_scan_order(y_ref, yn_ref, t)
        out.wait()

    return _call(
        body, name="s5_fwd", grid=(S5_BLOCKS,),
        in_specs=[pl.BlockSpec((t, 128), lambda i: (0, i)),
                  pl.BlockSpec((1, 128, 2 * ns), lambda i: (i, 0, 0)),
                  pl.BlockSpec((1, 2 * ns, 128), lambda i: (i, 0, 0)),
                  pl.BlockSpec((1, 1, ns), lambda i: (i, 0, 0)),
                  pl.BlockSpec((1, 1, ns), lambda i: (i, 0, 0)),
                  pl.BlockSpec((1, 128), lambda i: (0, i))],
        out_specs=[ANY, pl.BlockSpec((t, 128), lambda i: (0, i))],
        out_shape=[jax.ShapeDtypeStruct((S5_BLOCKS, t, 2 * ns), F32), jax.ShapeDtypeStruct((t, S5_WIDTH), F32)],
        scratch_shapes=[pltpu.VMEM((t, 2 * ns), F32), pltpu.VMEM((t, 128), F32), pltpu.VMEM((t, 128), F32),
                        pltpu.SemaphoreType.DMA(())],
        args=(ug, bd, ctd, ar4, ai4, dskip), carry=carry)


def _s5_bwd(dy, ug, xs, cd, bdt, ar4, ai4, dskip, carry=None):
    t = ug.shape[0]
    ls = t // S5_SEGS
    rc = min(512, t)
    ns = S5_BSTATE

    def body(dyn_ref, ugn_ref, xs_hbm, cd_ref, bdt_ref, ar_ref, ai_ref, d_ref,
             dugn_ref, dbd_ref, dcd_ref, dd_ref, dar_ref, dai_ref, xbuf, lam, dy_ref, ug_ref, dug_ref, sem):
        cb = pl.program_id(0)
        load = pltpu.make_async_copy(xs_hbm.at[cb], xbuf, sem)
        load.start()
        cdv = cd_ref[0]
        _rows_to_scan_order(dyn_ref, dy_ref, t)
        _rows_to_scan_order(ugn_ref, ug_ref, t)

        def mm(i, c):
            rows = pl.ds(pl.multiple_of(i * rc, rc), rc)
            lam[rows, :] = _nn(dy_ref[rows, :].astype(BF16), cdv)
            return c

        lax.fori_loop(0, t // rc, mm, 0, unroll=True)
        arb = jnp.broadcast_to(ar_ref[0], (S5_SEGS, ns))
        aib = jnp.broadcast_to(ai_ref[0], (S5_SEGS, ns))

        def lam_step(j, lr, li):
            rows = pl.ds(pl.multiple_of(j * S5_SEGS, S5_SEGS), S5_SEGS)
            nr = arb * lr + aib * li + lam[rows, 0:ns]
            ni = arb * li - aib * lr + lam[rows, ns:2 * ns]
            return rows, nr, ni

        def pass1(jj, c):
            _, nr, ni = lam_step(ls - 1 - jj, *c)
            return nr, ni

        zero = jnp.zeros((S5_SEGS, ns), F32)
        er, ei = lax.fori_loop(0, ls, pass1, (zero, zero))
        pr, pi = _cpow(arb, aib, ls)
        init = _chain_segments(er, ei, pr, -pi, down=False)
        load.wait()

        def accumulate(acc, nr, ni, xpr, xpi):
            return acc[0] + nr * xpr + ni * xpi, acc[1] + ni * xpr - nr * xpi

        def pass2(jj, c):
            lr, li, accr, acci = c
            j = ls - 1 - jj
            rows, nr, ni = lam_step(j, lr, li)
            lam[rows, 0:ns] = nr
            lam[rows, ns:2 * ns] = ni
            prev = pl.ds(pl.multiple_of((j - 1) * S5_SEGS, S5_SEGS), S5_SEGS)
            accr, acci = accumulate((accr, acci), nr, ni, xbuf[prev, 0:ns], xbuf[prev, ns:2 * ns])
            return nr, ni, accr, acci

        lr, li, accr, acci = lax.fori_loop(0, ls - 1, pass2, (init[0], init[1], zero, zero))
        rows, nr, ni = lam_step(0, lr, li)
        lam[rows, 0:ns] = nr
        lam[rows, ns:2 * ns] = ni
        last = pl.ds((ls - 1) * S5_SEGS, S5_SEGS)
        accr, acci = accumulate((accr, acci), nr, ni,
                                _shift_rows(xbuf[last, 0:ns], True), _shift_rows(xbuf[last, ns:2 * ns], True))
        dar_ref[0] = jnp.sum(accr, axis=0, keepdims=True)
        dai_ref[0] = jnp.sum(acci, axis=0, keepdims=True)

        bdtv = bdt_ref[0]
        dv = d_ref[...]
        dbd_ref[...] = jnp.zeros_like(dbd_ref)
        dcd_ref[...] = jnp.zeros_like(dcd_ref)
        dd_ref[...] = jnp.zeros_like(dd_ref)

        def tail(i, c):
            rows = pl.ds(pl.multiple_of(i * rc, rc), rc)
            dy = dy_ref[rows, :]
            ug = ug_ref[rows, :]
            lb = lam[rows, :].astype(BF16)
            dug_ref[rows, :] = _nn(lb, bdtv) + dv * dy
            dbd_ref[0] += _tn(ug.astype(BF16), lb)
            dcd_ref[0] += _tn(dy.astype(BF16), xbuf[rows, :].astype(BF16))
            dd_ref[...] += jnp.sum(dy * ug, axis=0, keepdims=True)
            return c

        lax.fori_loop(0, t // rc, tail, 0, unroll=True)
        _rows_from_scan_order(dug_ref, dugn_ref, t)

    chan = pl.BlockSpec((t, 128), lambda i: (0, i))
    dense = pl.BlockSpec((1, 128, 2 * ns), lambda i: (i, 0, 0))
    vec = pl.BlockSpec((1, 1, ns), lambda i: (i, 0, 0))
    return _call(
        body, name="s5_bwd", grid=(S5_BLOCKS,),
        in_specs=[chan, chan, ANY, dense, pl.BlockSpec((1, 2 * ns, 128), lambda i: (i, 0, 0)), vec, vec,
                  pl.BlockSpec((1, 128), lambda i: (0, i))],
        out_specs=[chan, dense, dense, pl.BlockSpec((1, 128), lambda i: (0, i)), vec, vec],
        out_shape=[jax.ShapeDtypeStruct((t, S5_WIDTH), F32),
                   jax.ShapeDtypeStruct((S5_BLOCKS, 128, 2 * ns), F32),
                   jax.ShapeDtypeStruct((S5_BLOCKS, 128, 2 * ns), F32),
                   jax.ShapeDtypeStruct((1, S5_WIDTH), F32),
                   jax.ShapeDtypeStruct((S5_BLOCKS, 1, ns), F32),
                   jax.ShapeDtypeStruct((S5_BLOCKS, 1, ns), F32)],
        scratch_shapes=[pltpu.VMEM((t, 2 * ns), F32), pltpu.VMEM((t, 2 * ns), F32)]
        + [pltpu.VMEM((t, 128), F32)] * 3 + [pltpu.SemaphoreType.DMA(())],
        args=(dy, ug, xs, cd, bdt, ar4, ai4, dskip), carry=carry)


def _cumsum_rows(x, reverse):
    c = x.shape[0]
    row = lax.broadcasted_iota(jnp.int32, x.shape, 0)
    d = 1
    while d < c:
        if reverse:
            x = x + jnp.where(row < c - d, pltpu.roll(x, c - d, 0), 0.0)
        else:
            x = x + jnp.where(row >= d, pltpu.roll(x, d, 0), 0.0)
        d *= 2
    return x


def _gla_common(q, k, alow, wup, bup):
    c = GLA_CHUNK
    pre = _nn(alow.astype(BF16), wup.astype(BF16)) + bup
    la = (jnp.minimum(pre, 0.0) - jnp.log(1.0 + jnp.exp(-jnp.abs(pre)))) * (1.0 / GLA_TAU)
    rr = lax.broadcasted_iota(jnp.int32, (c, c), 0)
    cc = lax.broadcasted_iota(jnp.int32, (c, c), 1)
    tril = (rr >= cc).astype(F32)
    bc = _cumsum_rows(la, reverse=False)
    bl = bc[c - 1:c, :]
    e_pos = jnp.exp(bc)
    e_neg = jnp.exp(-bc)
    e_end = jnp.exp(bl - bc)
    qt = q * (GLA_DK ** -0.5) * e_pos
    kt = k * e_neg
    ke = k * e_end
    lane = lax.broadcasted_iota(jnp.int32, (1, GLA_KEY), 1)
    masks = [((lane >= h * GLA_DK) & (lane < (h + 1) * GLA_DK)).astype(F32) for h in range(GLA_HEADS)]
    return dict(pre=pre, tril=tril, bc=bc, bl=bl, e_pos=e_pos, e_neg=e_neg, e_end=e_end,
                qt=qt, kt=kt, ke=ke, dec=jnp.exp(bl), masks=masks)


def _gla_fwd(q, k, v, alow, wup, bup, carry=None):
    t = q.shape[0]
    c = GLA_CHUNK
    n = t // c
    step = GLA_STEP_CHUNKS * c

    def body(q_ref, k_ref, v_ref, al_ref, wup_ref, bup_ref, o_ref, ss_ref, s_ref):
        i = pl.program_id(0)

        @pl.when(i == 0)
        def _():
            s_ref[...] = jnp.zeros_like(s_ref)

        wup_v, bup_v = wup_ref[...], bup_ref[...]
        s = s_ref[...]
        for j in range(GLA_STEP_CHUNKS):
            tok = slice(j * c, (j + 1) * c)
            m = _gla_common(q_ref[tok, :], k_ref[tok, :], al_ref[tok, :], wup_v, bup_v)
            ss_ref[j] = s
            sb = s.astype(BF16)
            ktb = m["kt"].astype(BF16)
            update = jnp.zeros_like(s)
            for h in range(GLA_HEADS):
                mask = m["masks"][h]
                qm = (m["qt"] * mask).astype(BF16)
                vh = v_ref[tok, h * GLA_DV:(h + 1) * GLA_DV].astype(BF16)
                p = (m["tril"] * _nt(qm, ktb)).astype(BF16)
                o_ref[tok, h * GLA_DV:(h + 1) * GLA_DV] = _nn(p, vh) + _nt(qm, sb)
                update = update + _tn(vh, (m["ke"] * mask).astype(BF16))
            s = m["dec"] * s + update
        s_ref[...] = s

    return _call(
        body, name="gla_fwd", grid=(t // step,),
        in_specs=[_row_tile(step, GLA_KEY), _row_tile(step, GLA_KEY), _row_tile(step, GLA_VAL),
                  _row_tile(step, GLA_RANK), VMEM_FULL, VMEM_FULL],
        out_specs=[_row_tile(step, GLA_VAL), pl.BlockSpec((GLA_STEP_CHUNKS, GLA_DV, GLA_KEY), lambda i: (i, 0, 0))],
        out_shape=[jax.ShapeDtypeStruct((t, GLA_VAL), F32), jax.ShapeDtypeStruct((n, GLA_DV, GLA_KEY), F32)],
        scratch_shapes=[pltpu.VMEM((GLA_DV, GLA_KEY), F32)],
        args=(q, k, v, alow, wup, bup), carry=carry)


def _gla_bwd(q, k, v, alow, wup, bup, ssave, do, carry=None):
    t = q.shape[0]
    c = GLA_CHUNK
    n = t // c

    def body(q_ref, k_ref, v_ref, al_ref, wup_ref, bup_ref, ss_ref, do_ref,
             dq_ref, dk_ref, dv_ref, dal_ref, dwup_ref, dbup_ref, ds_ref):
        i = pl.program_id(0)

        @pl.when(i == 0)
        def _():
            ds_ref[...] = jnp.zeros_like(ds_ref)
            dwup_ref[...] = jnp.zeros_like(dwup_ref)
            dbup_ref[...] = jnp.zeros_like(dbup_ref)

        wup_v, bup_v = wup_ref[...], bup_ref[...]
        ds_in = ds_ref[...]
        dwup = jnp.zeros((GLA_RANK, GLA_KEY), F32)
        dbup = jnp.zeros((1, GLA_KEY), F32)
        for j in reversed(range(GLA_STEP_CHUNKS)):
            tok = slice(j * c, (j + 1) * c)
            alow_v = al_ref[tok, :]
            m = _gla_common(q_ref[tok, :], k_ref[tok, :], alow_v, wup_v, bup_v)
            s = ss_ref[j]
            sb = s.astype(BF16)
            dsb = ds_in.astype(BF16)
            qt, kt, ke = m["qt"], m["kt"], m["ke"]
            ktb = kt.astype(BF16)
            dqt = jnp.zeros((c, GLA_KEY), F32)
            dkt = jnp.zeros((c, GLA_KEY), F32)
            dke = jnp.zeros((c, GLA_KEY), F32)
            update = jnp.zeros_like(ds_in)
            for h in range(GLA_HEADS):
                mask = m["masks"][h]
                qm = (qt * mask).astype(BF16)
                km = (kt * mask).astype(BF16)
                kem = (ke * mask).astype(BF16)
                cols = slice(h * GLA_DV, (h + 1) * GLA_DV)
                vh = v_ref[tok, cols].astype(BF16)
                doh = do_ref[tok, cols].astype(BF16)
                p = (m["tril"] * _nt(qm, ktb)).astype(BF16)
                dp = (m["tril"] * _nt(doh, vh)).astype(BF16)
                dv_ref[tok, cols] = _tn(p, doh) + _nt(kem, dsb)
                dqt = dqt + _nn(dp, km) + _nn(doh, sb) * mask
                dkt = dkt + _tn(dp, qm)
                dke = dke + _nn(vh, dsb) * mask
                update = update + _tn(doh, qm)
            ddec = jnp.sum(ds_in * s, axis=0, keepdims=True)
            dq_ref[tok, :] = dqt * m["e_pos"] * (GLA_DK ** -0.5)
            dk_ref[tok, :] = dkt * m["e_neg"] + dke * m["e_end"]
            dkeke = dke * ke
            dbl = jnp.sum(dkeke, axis=0, keepdims=True) + ddec * m["dec"]
            last = (lax.broadcasted_iota(jnp.int32, (c, 1), 0) == c - 1).astype(F32)
            dla = _cumsum_rows(dqt * qt - dkt * kt - dkeke + last * dbl, reverse=True)
            dpre = dla * (1.0 / GLA_TAU) * jax.nn.sigmoid(-m["pre"])
            dpb = dpre.astype(BF16)
            dal_ref[tok, :] = _nt(dpb, wup_v.astype(BF16))
            dwup = dwup + _tn(alow_v.astype(BF16), dpb)
            dbup = dbup + jnp.sum(dpre, axis=0, keepdims=True)
            ds_in = m["dec"] * ds_in + update
        ds_ref[...] = ds_in
        dwup_ref[...] += dwup
        dbup_ref[...] += dbup

    step = GLA_STEP_CHUNKS * c
    nsteps = t // step

    def rev(d):
        return pl.BlockSpec((step, d), lambda i: (nsteps - 1 - i, 0))

    return _call(
        body, name="gla_bwd", grid=(nsteps,),
        in_specs=[rev(GLA_KEY), rev(GLA_KEY), rev(GLA_VAL), rev(GLA_RANK), VMEM_FULL, VMEM_FULL,
                  pl.BlockSpec((GLA_STEP_CHUNKS, GLA_DV, GLA_KEY), lambda i: (nsteps - 1 - i, 0, 0)), rev(GLA_VAL)],
        out_specs=[rev(GLA_KEY), rev(GLA_KEY), rev(GLA_VAL), rev(GLA_RANK),
                   pl.BlockSpec((GLA_RANK, GLA_KEY), lambda i: (0, 0)), _acc_row(GLA_KEY)],
        out_shape=[jax.ShapeDtypeStruct((t, GLA_KEY), F32), jax.ShapeDtypeStruct((t, GLA_KEY), F32),
                   jax.ShapeDtypeStruct((t, GLA_VAL), F32), jax.ShapeDtypeStruct((t, GLA_RANK), F32),
                   jax.ShapeDtypeStruct((GLA_RANK, GLA_KEY), F32), jax.ShapeDtypeStruct((1, GLA_KEY), F32)],
        scratch_shapes=[pltpu.VMEM((GLA_DV, GLA_KEY), F32)],
        args=(q, k, v, alow, wup, bup, ssave, do), carry=carry)


def _post_math(y, o, r, gs5, ggla, wg, bg, gn, ps5t, pglat):
    y2 = y * y
    th = jnp.tanh(GELU_C0 * (y + GELU_C1 * y * y2))
    z5 = 0.5 * y * (1.0 + th)
    z5b = z5.astype(BF16)
    gate = jax.nn.sigmoid(_nn(z5b, wg) + bg)
    ys5 = z5 * gate
    rs, on = [], []
    for h in range(GLA_HEADS):
        oh = o[:, h * GLA_DV:(h + 1) * GLA_DV]
        rh = lax.rsqrt(jnp.mean(oh * oh, axis=-1, keepdims=True) + EPS)
        rs.append(rh)
        on.append(oh * rh)
    on = jnp.concatenate(on, axis=-1)
    sr = jax.nn.sigmoid(r)
    silu_r = r * sr
    ygla = on * gn * silu_r
    ys5b, yglab = ys5.astype(BF16), ygla.astype(BF16)
    m5 = _nt(ys5b, ps5t)
    mg = _nt(yglab, pglat)
    s5g, glag = jax.nn.sigmoid(gs5), jax.nn.sigmoid(ggla)
    merged = s5g * m5 + glag * mg
    return dict(y2=y2, th=th, z5=z5, z5b=z5b, gate=gate, ys5b=ys5b, yglab=yglab, rs=rs, on=on, sr=sr,
                silu_r=silu_r, m5=m5, mg=mg, s5g=s5g, glag=glag, mergedb=merged.astype(BF16))


def _mix_post_fwd(y, o, r, gs5, ggla, h1, wg, bg, gn, ps5t, pglat, wout, carry=None):
    t = o.shape[0]
    tm = _tile(t)

    def body(y_ref, o_ref, r_ref, gs5_ref, ggla_ref, h1_ref, wg_ref, bg_ref, gn_ref, ps_ref, pg_ref, wo_ref, h2_ref):
        m = _post_math(y_ref[...], o_ref[...], r_ref[...], gs5_ref[...], ggla_ref[...],
                       wg_ref[...], bg_ref[...], gn_ref[...], ps_ref[...], pg_ref[...])
        h2_ref[...] = h1_ref[...] + _nn(m["mergedb"], wo_ref[...])

    (h2,), landed = _call(
        body, name="mix_post_fwd", grid=(t // tm,),
        in_specs=[_row_tile(tm, 512)] * 3 + [_row_tile(tm, D_MODEL)] * 3
        + [VMEM_FULL, _acc_row(512), _acc_row(512), VMEM_FULL, VMEM_FULL, VMEM_FULL],
        out_specs=[_row_tile(tm, D_MODEL)],
        out_shape=[jax.ShapeDtypeStruct((t, D_MODEL), F32)],
        args=(y, o, r, gs5, ggla, h1, wg, bg, gn, ps5t, pglat, wout), carry=carry)
    return h2, landed


def _mix_post_bwd(y, o, r, gs5, ggla, dh2, wg, bg, gn, ps5t, pglat, wout, carry=None):
    t = o.shape[0]
    tm = _tile(t) // 2

    def body(y_ref, o_ref, r_ref, gs5_ref, ggla_ref, dh2_ref, wg_ref, bg_ref, gn_ref, ps_ref, pg_ref, wo_ref,
             dy_ref, do_ref, dr_ref, dgs5_ref, dggla_ref, dbg_ref, dgn_ref,
             z5b_ref, dgp_ref, ys5b_ref, dm5b_ref, yglab_ref, dmgb_ref, mergedb_ref, dh2b_ref):
        i = pl.program_id(0)
        yv, ov, rv = y_ref[...], o_ref[...], r_ref[...]
        wg, gn, ps5t, pglat = wg_ref[...], gn_ref[...], ps_ref[...], pg_ref[...]
        m = _post_math(yv, ov, rv, gs5_ref[...], ggla_ref[...], wg, bg_ref[...], gn, ps5t, pglat)
        dh2b = dh2_ref[...].astype(BF16)
        dmerged = _nt(dh2b, wo_ref[...])
        s5g, glag = m["s5g"], m["glag"]
        dgs5_ref[...] = dmerged * m["m5"] * s5g * (1.0 - s5g)
        dggla_ref[...] = dmerged * m["mg"] * glag * (1.0 - glag)
        dm5b = (dmerged * s5g).astype(BF16)
        dmgb = (dmerged * glag).astype(BF16)
        dys5 = _nn(dm5b, ps5t)
        dygla = _nn(dmgb, pglat)
        gate, z5, th = m["gate"], m["z5"], m["th"]
        dgpre = dys5 * z5 * gate * (1.0 - gate)
        dgpb = dgpre.astype(BF16)
        dz5 = dys5 * gate + _nt(dgpb, wg)
        dgelu = 0.5 * (1.0 + th) + 0.5 * yv * (1.0 - th * th) * GELU_C0 * (1.0 + 3.0 * GELU_C1 * m["y2"])
        dy_ref[...] = dz5 * dgelu
        on, sr, silu_r = m["on"], m["sr"], m["silu_r"]
        dr_ref[...] = dygla * on * gn * sr * (1.0 + rv * (1.0 - sr))
        dgn = jnp.sum(dygla * on * silu_r, axis=0, keepdims=True)
        don = dygla * gn * silu_r
        for h in range(GLA_HEADS):
            cols = slice(h * GLA_DV, (h + 1) * GLA_DV)
            donh, onh = don[:, cols], on[:, cols]
            do_ref[:, cols] = m["rs"][h] * (donh - onh * jnp.mean(donh * onh, axis=-1, keepdims=True))

        @pl.when(i == 0)
        def _():
            dbg_ref[...] = jnp.zeros_like(dbg_ref)
            dgn_ref[...] = jnp.zeros_like(dgn_ref)

        dbg_ref[...] += jnp.sum(dgpre, axis=0, keepdims=True)
        dgn_ref[...] += dgn
        z5b_ref[...] = m["z5b"]
        dgp_ref[...] = dgpb
        ys5b_ref[...] = m["ys5b"]
        dm5b_ref[...] = dm5b
        yglab_ref[...] = m["yglab"]
        dmgb_ref[...] = dmgb
        mergedb_ref[...] = m["mergedb"]
        dh2b_ref[...] = dh2b

    def f32(d):
        return jax.ShapeDtypeStruct((t, d), F32)

    def b16(d):
        return jax.ShapeDtypeStruct((t, d), BF16)

    widths = (512, 512, 512, 1024, 512, 1024, 1024, 1024)
    return _call(
        body, name="mix_post_bwd", grid=(t // tm,),
        in_specs=[_row_tile(tm, 512)] * 3 + [_row_tile(tm, D_MODEL)] * 3
        + [VMEM_FULL, _acc_row(512), _acc_row(512), VMEM_FULL, VMEM_FULL, VMEM_FULL],
        out_specs=[_row_tile(tm, 512)] * 3 + [_row_tile(tm, D_MODEL)] * 2
        + [_acc_row(512)] * 2 + [_row_tile(tm, w) for w in widths],
        out_shape=[f32(512)] * 3 + [f32(D_MODEL)] * 2
        + [jax.ShapeDtypeStruct((1, 512), F32)] * 2
        + [b16(w) for w in widths],
        args=(y, o, r, gs5, ggla, dh2, wg, bg, gn, ps5t, pglat, wout), carry=carry)


def _head(h3, g, target):
    t = h3.shape[0]
    tm = _tile(t)

    def body(h_ref, g_ref, t_ref, loss_ref, dh_ref, dg_ref):
        i = pl.program_id(0)
        gv = g_ref[...]
        xhat, r = _rms_parts(h_ref[...])
        err = xhat * gv - t_ref[...]
        dx, dg = _rms_bwd(err * (1.0 / D_MODEL), gv, xhat, r)
        dh_ref[...] = dx

        @pl.when(i == 0)
        def _():
            loss_ref[...] = jnp.zeros_like(loss_ref)
            dg_ref[...] = jnp.zeros_like(dg_ref)

        loss_ref[...] += (0.5 / D_MODEL) * jnp.sum(jnp.sum(err * err, axis=1, keepdims=True), axis=0, keepdims=True)
        dg_ref[...] += dg

    return pl.pallas_call(
        body, name="head", grid=(t // tm,),
        in_specs=[_row_tile(tm, D_MODEL), _acc_row(D_MODEL), _row_tile(tm, D_MODEL)],
        out_specs=[pl.BlockSpec((1, 1), lambda i: (0, 0)), _row_tile(tm, D_MODEL), _acc_row(D_MODEL)],
        out_shape=[jax.ShapeDtypeStruct((1, 1), F32), jax.ShapeDtypeStruct((t, D_MODEL), F32),
                   jax.ShapeDtypeStruct((1, D_MODEL), F32)],
        compiler_params=_cparams(1),
    )(h3, g, target)


ADAM_TILE_ELEMS = 256 * 1024


def _adamw(w, g, m, v, name):
    rows, cols = w.shape
    tr = rows
    while tr * cols > ADAM_TILE_ELEMS and tr % 16 == 0:
        tr //= 2

    spec = pl.BlockSpec((tr, cols), lambda i: (i, 0))
    sh = jax.ShapeDtypeStruct((rows, cols), F32)
    return pl.pallas_call(functools.partial(_adamw_body), name=name, grid=(rows // tr,), in_specs=[spec] * 4,
                          out_specs=[spec] * 3, out_shape=[sh] * 3, compiler_params=_cparams(1))(w, g, m, v)


def _adamw_body(w_ref, g_ref, m_ref, v_ref, d_ref, nm_ref, nv_ref):
    gv = g_ref[...]
    nm = ADAM_B1 * m_ref[...] + (1.0 - ADAM_B1) * gv
    nv = ADAM_B2 * v_ref[...] + (1.0 - ADAM_B2) * (gv * gv)
    m_hat = nm / (1.0 - ADAM_B1 ** ADAM_STEP)
    v_hat = nv / (1.0 - ADAM_B2 ** ADAM_STEP)
    d_ref[...] = -ADAM_LR * (m_hat / (jnp.sqrt(v_hat) + ADAM_EPS) + ADAM_WD * w_ref[...])
    nm_ref[...] = nm
    nv_ref[...] = nv


def _adamw_many(ws, gs, ms, vs, name):
    n = len(ws)

    def body(*refs):
        ins, outs = refs[:4 * n], refs[4 * n:]
        for i in range(n):
            _adamw_body(*(ins[j * n + i] for j in range(4)), *(outs[j * n + i] for j in range(3)))

    shapes = [jax.ShapeDtypeStruct(w.shape, F32) for w in ws]
    res = pl.pallas_call(body, name=name, in_specs=[VMEM_FULL] * (4 * n), out_specs=[VMEM_FULL] * (3 * n),
                         out_shape=shapes * 3)(*ws, *gs, *ms, *vs)
    return res[:n], res[n:2 * n], res[2 * n:]


def _exchange(carry, name):
    return _call(None, name=name, grid=(1,), in_specs=[], out_specs=[], out_shape=[], args=(), carry=carry)[1]


def _pair_add(slabs, from_pair, name):
    _, r, cols = slabs.shape

    def body(s_ref, p_ref, o_ref):
        c = lax.axis_index("c")
        mine = jnp.where(c == 0, s_ref[0, 0].astype(F32), s_ref[0, 1].astype(F32))
        o_ref[0] = (mine + p_ref[0].astype(F32)).astype(BF16)

    return pl.pallas_call(
        body, name=name, grid=(4,),
        in_specs=[pl.BlockSpec((1, 2, r, cols), lambda i: (i, 0, 0, 0)), pl.BlockSpec((1, r, cols), lambda i: (i, 0, 0))],
        out_specs=pl.BlockSpec((1, r, cols), lambda i: (i, 0, 0)),
        out_shape=jax.ShapeDtypeStruct((4, r, cols), BF16),
        compiler_params=_cparams(1),
    )(slabs.reshape(4, 2, r, cols), from_pair)


def _sum_slabs(slabs, name):
    n = slabs.shape[0]

    def body(s_ref, o_ref):
        acc = s_ref[0].astype(F32)
        for s in range(1, n):
            acc = acc + s_ref[s].astype(F32)
        o_ref[...] = acc

    return pl.pallas_call(
        body, name=name, in_specs=[VMEM_FULL], out_specs=VMEM_FULL,
        out_shape=jax.ShapeDtypeStruct(slabs.shape[1:], F32),
        compiler_params=pltpu.CompilerParams(vmem_limit_bytes=VMEM_LIMIT_BYTES),
    )(slabs)


BIG = ("ffn1_w1", "ffn1_w3", "ffn1_w2", "w_in", "s5_glu_w", "gla_a_up_w", "proj_s5", "proj_gla", "w_out",
       "ffn2_w1", "ffn2_w3", "ffn2_w2")
GROUPS = (("ffn1_w1", "ffn1_w3", "ffn1_w2"),
          ("w_in", "s5_glu_w", "gla_a_up_w", "proj_s5", "proj_gla", "w_out"),
          ("ffn2_w1", "ffn2_w3", "ffn2_w2"))
W_IN_ROWS = 514
W_IN_PAD = 528
UP_COLS = 32
ROW_ADAM = ("ffn1_w1", "ffn1_w3", "w_in", "ffn2_w1", "ffn2_w3")
COL_SHARDED = ("ffn1_w1", "ffn1_w3", "w_in", "proj_s5", "proj_gla", "ffn2_w1", "ffn2_w3")

SMALL = ("ffn1_norm", "mix_norm", "s5_lambda_re", "s5_lambda_im", "s5_log_dt", "s5_b_re", "s5_b_im", "s5_c_re",
         "s5_c_im", "s5_d", "s5_glu_b", "gla_a_up_b", "gla_out_norm", "ffn2_norm", "final_norm")
SMALL_SHAPES = dict(ffn1_norm=(1, 1024), mix_norm=(1, 1024), s5_lambda_re=(1, 32, 64), s5_lambda_im=(1, 32, 64),
                    s5_log_dt=(1, 32), s5_b_re=(1, 32, 64, 16), s5_b_im=(1, 32, 64, 16), s5_c_re=(1, 32, 16, 64),
                    s5_c_im=(1, 32, 16, 64), s5_d=(1, 32, 16), s5_glu_b=(1, 512), gla_a_up_b=(1, 256),
                    gla_out_norm=(1, 512), ffn2_norm=(1, 1024), final_norm=(1024,))
SMALL_N = sum(math.prod(s) for s in SMALL_SHAPES.values())
SMALL_R = -(-SMALL_N // (64 * 1024)) * 64


def _shard_rows(name, a):
    if name == "gla_a_up_w":
        return jnp.pad(a, ((0, 0), (0, 128 - UP_COLS)))
    if name in COL_SHARDED:
        a = a.T
    if name == "w_in":
        return jnp.pad(a, ((0, W_IN_PAD - W_IN_ROWS), (0, 0)))
    return a.reshape(-1, 1024)


def _unshard_rows(name, rows, shape):
    if name == "gla_a_up_w":
        return rows[:, :UP_COLS]
    if name == "w_in":
        rows = rows[:W_IN_ROWS]
    if name in COL_SHARDED:
        return rows.reshape(shape[1], shape[0]).T
    return rows.reshape(shape)


def _pack_small(vals):
    flat = jnp.concatenate([vals[n].reshape(-1).astype(F32) for n in SMALL])
    return jnp.pad(flat, (0, SMALL_R * 1024 - SMALL_N)).reshape(SMALL_R, 1024)


S5_B = ("s5_b_re", "s5_b_im")


def _working(name, a):
    return a[0].transpose(0, 2, 1) if name in S5_B else a


def _declared(name, a):
    return a.transpose(0, 2, 1)[None] if name in S5_B else a.reshape(SMALL_SHAPES[name])


def _unpack_small(slab):
    flat = slab.reshape(-1)
    out, off = {}, 0
    for n in SMALL:
        size = math.prod(SMALL_SHAPES[n])
        shape = (S5_GROUPS, S5_GROUP, S5_STATE) if n in S5_B else SMALL_SHAPES[n]
        out[n] = flat[off:off + size].reshape(shape)
        off += size
    return out


FULL_SHAPES = dict(w_in=(IN_COLS, D_MODEL), s5_glu_w=(S5_WIDTH, S5_WIDTH), gla_a_up_w=(GLA_RANK, GLA_KEY),
                   proj_s5=(D_MODEL, S5_WIDTH), proj_gla=(D_MODEL, GLA_VAL), w_out=(D_MODEL, D_MODEL))


def _full_weight(name, gathered):
    if name == "gla_a_up_w":
        return gathered[:, :, :UP_COLS].transpose(1, 0, 2).reshape(GLA_RANK, GLA_KEY)
    if name == "w_in":
        gathered = gathered[:, :W_IN_ROWS]
    return gathered.reshape(FULL_SHAPES.get(name, (D_FF, D_MODEL)))


def _grad_slabs(name, g):
    if name == "gla_a_up_w":
        g = g.reshape(GLA_RANK, N_DEV, UP_COLS).transpose(1, 0, 2)
        return jnp.pad(g, ((0, 0), (0, 0), (0, 128 - UP_COLS))).astype(BF16)
    if name == "w_in":
        return jnp.pad(g.reshape(N_DEV, W_IN_ROWS, D_MODEL), ((0, 0), (0, W_IN_PAD - W_IN_ROWS), (0, 0)))
    return g.reshape(N_DEV, -1, 1024)


def _s5_dense(re, im, sign_im):
    eye = jnp.eye(8, dtype=F32)

    def one(a):
        a = a.reshape(S5_BLOCKS, 8, S5_GROUP, S5_STATE)
        return jnp.einsum("cghp,gk->cghkp", a, eye).reshape(S5_BLOCKS, 128, S5_BSTATE)

    return jnp.concatenate([one(re), sign_im * one(im)], axis=-1)


def _s5_undense(d):
    eye = jnp.eye(8, dtype=F32)

    def one(a):
        a = a.reshape(S5_BLOCKS, 8, S5_GROUP, 8, S5_STATE)
        return jnp.einsum("cghkp,gk->cghp", a, eye).reshape(S5_GROUPS, S5_GROUP, S5_STATE)

    return one(d[..., :S5_BSTATE]), one(d[..., S5_BSTATE:])


def _local_step(x, target, p, w, rows=None):
    w = dict(w or {})
    landed_grads = {}

    def gather(names):
        return None if rows is None else ([rows[n] for n in names], "gather")

    def gathered(names, landed):
        w.update({n: _full_weight(n, g) for n, g in zip(names, landed)})

    def scatter(names):
        return None if rows is None else ([_grad_slabs(n, big[n]) for n in names], "scatter")

    def scattered(names, landed):
        landed_grads.update(zip(names, landed))

    if rows is not None:
        gathered(GROUPS[0], _exchange(gather(GROUPS[0]), "gather_ffn1"))
    g1, gm, g2 = p["ffn1_norm"], p["mix_norm"], p["ffn2_norm"]
    gf = p["final_norm"].reshape(1, D_MODEL)
    lre, lim = p["s5_lambda_re"][0], p["s5_lambda_im"][0]
    ldt = p["s5_log_dt"][0].reshape(S5_GROUPS, 1)
    bre = p["s5_b_re"][0].transpose(0, 2, 1)
    bim = p["s5_b_im"][0].transpose(0, 2, 1)
    cre, cim = p["s5_c_re"][0], p["s5_c_im"][0]
    dskip = p["s5_d"][0].reshape(1, S5_WIDTH)
    bg, bup, gn = p["s5_glu_b"], p["gla_a_up_b"], p["gla_out_norm"]

    mix_first, mix_rest = ("w_in", "gla_a_up_w"), ("s5_glu_w", "proj_s5", "proj_gla", "w_out")
    h1, (a3_1, b3_1, n1), got = _ffn_fwd(x, g1, w["ffn1_w1"], w["ffn1_w3"], w["ffn1_w2"], "ffn1_fwd",
                                         gather(mix_first))
    gathered(mix_first, got)
    wup = w["gla_a_up_w"].astype(F32)
    (u, s5in, q, k, v, r, alow, gs5, ggla), got = _mix_pre_fwd(h1, gm, w["w_in"], gather(mix_rest))
    gathered(mix_rest, got)
    ar, ai, bbr, bbi = _s5_disc(lre, lim, ldt, bre, bim)
    bd = _s5_dense(bbr, bbi, 1.0)
    cd = _s5_dense(cre, cim, -1.0)
    bd16, cd16 = bd.astype(BF16), cd.astype(BF16)
    bdt16, ctd16 = bd16.transpose(0, 2, 1), cd16.transpose(0, 2, 1)
    ar4 = ar.reshape(S5_BLOCKS, 1, S5_BSTATE)
    ai4 = ai.reshape(S5_BLOCKS, 1, S5_BSTATE)
    (xs, y), got = _s5_fwd(s5in, bd16, ctd16, ar4, ai4, dskip, gather(GROUPS[2][:1]))
    gathered(GROUPS[2][:1], got)
    (o, ssave), got = _gla_fwd(q, k, v, alow, wup, bup, gather(GROUPS[2][1:2]))
    gathered(GROUPS[2][1:2], got)
    post_w = (w["s5_glu_w"], bg, gn, w["proj_s5"], w["proj_gla"], w["w_out"])
    h2, got = _mix_post_fwd(y, o, r, gs5, ggla, h1, *post_w, carry=gather(GROUPS[2][2:]))
    gathered(GROUPS[2][2:], got)
    h3, (a3_2, b3_2, n2), _ = _ffn_fwd(h2, g2, w["ffn2_w1"], w["ffn2_w3"], w["ffn2_w2"], "ffn2_fwd")
    loss, dh3, dgf = _head(h3, gf, target)

    big, small = {}, {}
    small["final_norm"] = dgf.reshape(D_MODEL)
    (dh2, dg2, da3, db3, s3, dhh2), _ = _ffn_bwd(
        h2, dh3, g2, a3_2, b3_2, w["ffn2_w1"], w["ffn2_w3"], w["ffn2_w2"], "ffn2_bwd")
    small["ffn2_norm"] = dg2
    big["ffn2_w1"] = _mm_tn(da3, n2, "ffn2_dw1")
    big["ffn2_w3"] = _mm_tn(db3, n2, "ffn2_dw3")
    big["ffn2_w2"] = _mm_tn(s3, dhh2, "ffn2_dw2")
    (dy, do, dr, dgs5, dggla, dbg, dgn, z5b, dgpb, ys5b, dm5b, yglab, dmgb, mergedb, dh2b), got = _mix_post_bwd(
        y, o, r, gs5, ggla, dh2, *post_w, carry=scatter(GROUPS[2][:1]))
    scattered(GROUPS[2][:1], got)
    small["s5_glu_b"] = dbg
    small["gla_out_norm"] = dgn
    big["s5_glu_w"] = _mm_tn(z5b, dgpb, "glu_dw")
    big["proj_s5"] = _mm_tn(dm5b, ys5b, "proj_s5_dw")
    big["proj_gla"] = _mm_tn(dmgb, yglab, "proj_gla_dw")
    big["w_out"] = _mm_tn(mergedb, dh2b, "w_out_dw")
    (dq, dk, dv, dalow, dwup, dbup), got = _gla_bwd(q, k, v, alow, wup, bup, ssave, do, scatter(GROUPS[2][1:2]))
    scattered(GROUPS[2][1:2], got)
    big["gla_a_up_w"] = dwup
    small["gla_a_up_b"] = dbup
    (ds5in, dbd, dcd, dd, dar4, dai4), got = _s5_bwd(
        dy, s5in, xs, cd16, bdt16, ar4, ai4, dskip, scatter(GROUPS[2][2:]))
    scattered(GROUPS[2][2:], got)
    dbbr, dbbi = _s5_undense(dbd)
    dcre, dcim_neg = _s5_undense(dcd)
    glre, glim, gldt, gbre, gbim = _s5_disc_bwd(
        lre, lim, ldt, bre, bim, dar4.reshape(S5_GROUPS, S5_STATE), dai4.reshape(S5_GROUPS, S5_STATE),
        dbbr, dbbi)
    small["s5_lambda_re"] = glre[None]
    small["s5_lambda_im"] = glim[None]
    small["s5_log_dt"] = gldt.reshape(1, S5_GROUPS)
    small["s5_b_re"] = gbre
    small["s5_b_im"] = gbim
    small["s5_c_re"] = dcre[None]
    small["s5_c_im"] = -dcim_neg[None]
    small["s5_d"] = dd.reshape(1, S5_GROUPS, S5_GROUP)
    dz = (ds5in, dq, dk, dv, dr, dalow, dgs5, dggla)
    (dh1, dgm), got = _mix_pre_bwd(h1, gm, w["w_in"], dh2, dz, scatter(mix_rest))
    scattered(mix_rest, got)
    small["mix_norm"] = dgm
    big["w_in"] = jnp.concatenate([_mm_tn(d, u, "w_in_dw%d" % i) for i, d in enumerate(dz)], axis=0)
    (dx, dg1, da3, db3, s3, dhh1), got = _ffn_bwd(
        x, dh1, g1, a3_1, b3_1, w["ffn1_w1"], w["ffn1_w3"], w["ffn1_w2"], "ffn1_bwd", scatter(mix_first))
    scattered(mix_first, got)
    small["ffn1_norm"] = dg1
    big["ffn1_w1"] = _mm_tn(da3, n1, "ffn1_dw1")
    if rows is None:
        big["ffn1_w3"] = _mm_tn(db3, n1, "ffn1_dw3")
        big["ffn1_w2"] = _mm_tn(s3, dhh1, "ffn1_dw2")
        return loss[0, 0], dx, big, small
    slabs1 = _grad_slabs("ffn1_w1", big["ffn1_w1"])
    big["ffn1_w3"], (from_pair,) = _mm_tn(db3, n1, "ffn1_dw3", ([slabs1], "pair"))
    sums1 = _pair_add(slabs1, from_pair, "ffn1_w1_pair")
    slabs3 = _grad_slabs("ffn1_w3", big["ffn1_w3"])
    big["ffn1_w2"], (landed1, from_pair) = _mm_tn(s3, dhh1, "ffn1_dw2", [([sums1], "chips"), ([slabs3], "pair")])
    sums3 = _pair_add(slabs3, from_pair, "ffn1_w3_pair")
    slabs2 = _grad_slabs("ffn1_w2", big["ffn1_w2"])
    landed3, from_pair = _exchange([([sums3], "chips"), ([slabs2], "pair")], "scatter_ffn1_a")
    sums2 = _pair_add(slabs2, from_pair, "ffn1_w2_pair")
    (landed2,) = _exchange(([sums2], "chips"), "scatter_ffn1_b")
    scattered(GROUPS[0], (landed1, landed3, landed2))
    return loss[0, 0], dx, landed_grads, small


NAMES = ("ffn1_norm", "ffn1_w1", "ffn1_w3", "ffn1_w2", "mix_norm", "w_in", "s5_lambda_re", "s5_lambda_im",
         "s5_log_dt", "s5_b_re", "s5_b_im", "s5_c_re", "s5_c_im", "s5_d", "s5_glu_w", "s5_glu_b", "gla_a_up_w",
         "gla_a_up_b", "gla_out_norm", "proj_s5", "proj_gla", "w_out", "ffn2_norm", "ffn2_w1", "ffn2_w3", "ffn2_w2",
         "final_norm")


def kernel(*args):
    nw = len(NAMES)
    x = args[0][0]
    wts = dict(zip(NAMES, args[1:1 + nw]))
    target = args[1 + nw][0]
    mom = dict(zip(NAMES, args[2 + nw:2 + 2 * nw]))
    var = dict(zip(NAMES, args[2 + 2 * nw:2 + 3 * nw]))

    shards = {n: wts[n][0] for n in BIG}
    rows = {n: _shard_rows(n, shards[n]).astype(BF16) for n in BIG}
    loss, dx, landed, small = _local_step(x, target, {n: wts[n] for n in SMALL}, None, rows)
    loss = lax.psum(loss, ("x", "y", "c"))

    grad, delta, new_m, new_v = {}, {}, {}, {}
    for n in BIG:
        g_rows = _sum_slabs(landed[n], "sum_" + n)
        if n in ROW_ADAM:
            g = g_rows[:W_IN_ROWS] if n == "w_in" else g_rows
            outs = _adamw(shards[n].T, g, mom[n][0].T, var[n][0].T, "adamw_" + n)
            grad[n], delta[n], new_m[n], new_v[n] = (a.T[None] for a in (g, *outs))
        else:
            g = _unshard_rows(n, g_rows, shards[n].shape)
            outs = _adamw(shards[n], g, mom[n][0], var[n][0], "adamw_" + n)
            grad[n], delta[n], new_m[n], new_v[n] = (a[None] for a in (g, *outs))

    part = _pack_small(small).reshape(N_DEV, SMALL_R // N_DEV, 1024)
    mine = _sum_slabs(_exchange(([part], "scatter"), "scatter_small")[0], "sum_small")
    g_small = _unpack_small(_exchange(([mine], "gather"), "gather_small")[0].reshape(SMALL_R, 1024))

    def flat2d(a):
        return a.reshape(-1, a.shape[-1])

    operands = ([flat2d(_working(n, d[n])) for n in SMALL] for d in (wts, mom, var))
    w2d, m2d, v2d = operands
    outs = _adamw_many(w2d, [flat2d(g_small[n]) for n in SMALL], m2d, v2d, "adamw_small")
    for out, arrays in zip((grad, delta, new_m, new_v), ([g_small[n] for n in SMALL], *outs)):
        out.update({n: _declared(n, a.reshape(g_small[n].shape)) for n, a in zip(SMALL, arrays)})
    return (loss, dx[None], *(d[n] for d in (grad, delta, new_m, new_v) for n in NAMES))
```

```python
import functools
import math

import jax
import jax.numpy as jnp
from jax import lax
from jax.experimental import pallas as pl
from jax.experimental.pallas import tpu as pltpu

F32, BF16 = jnp.float32, jnp.bfloat16
HIGHEST = lax.Precision.HIGHEST

D_MODEL = 1024
D_FF = 2816
N_DEV = 8
S5_WIDTH, S5_GROUPS, S5_GROUP, S5_STATE = 512, 32, 16, 64
S5_BLOCKS = 4
S5_BSTATE = 512
S5_SEGS = 8
GLA_HEADS, GLA_DK, GLA_DV = 4, 64, 128
GLA_KEY, GLA_VAL, GLA_RANK, GLA_CHUNK = 256, 512, 16, 64
GLA_TAU = 16.0
GLA_STEP_CHUNKS = 4
EPS = 1e-6
IN_SIZES = (512, 256, 256, 512, 512, 16, 1024, 1024)
IN_OFFS = tuple(sum(IN_SIZES[:i]) for i in range(len(IN_SIZES)))
IN_COLS = sum(IN_SIZES)
ADAM_LR, ADAM_B1, ADAM_B2, ADAM_EPS, ADAM_WD, ADAM_STEP = 0.001, 0.9, 0.999, 1e-08, 0.01, 10
GELU_C0 = math.sqrt(2.0 / math.pi)
GELU_C1 = 0.044715

FFN_FT = 256
VMEM_LIMIT_BYTES = 56 * 1024 * 1024

VMEM_FULL = pl.BlockSpec(memory_space=pltpu.VMEM)
ANY = pl.BlockSpec(memory_space=pl.ANY)


def _cparams(n_grid):
    return pltpu.CompilerParams(dimension_semantics=("arbitrary",) * n_grid, vmem_limit_bytes=VMEM_LIMIT_BYTES)


def _tile(t):
    return 512 if t >= 1024 else t // 2


def _nn(a, b):
    return jnp.dot(a, b, preferred_element_type=F32)


def _nt(a, b):
    return lax.dot_general(a, b, (((1,), (1,)), ((), ())), preferred_element_type=F32)


def _tn(a, b):
    return lax.dot_general(a, b, (((0,), (0,)), ((), ())), preferred_element_type=F32)


def _rms_parts(x):
    r = lax.rsqrt(jnp.mean(x * x, axis=-1, keepdims=True) + EPS)
    return x * r, r


def _rms_bwd(dn, g, xhat, r):
    dxh = dn * g
    dx = r * (dxh - xhat * jnp.mean(dxh * xhat, axis=-1, keepdims=True))
    return dx, jnp.sum(dn * xhat, axis=0, keepdims=True)


def _peers():
    x, y, c = lax.axis_index("x"), lax.axis_index("y"), lax.axis_index("c")
    out = []
    for k in range(1, N_DEV):
        px = 1 - x if k & 4 else x
        py = 1 - y if k & 2 else y
        pc = 1 - c if k & 1 else c
        out.append(((px, py, pc), 4 * px + 2 * py + pc))
    return 4 * x + 2 * y + c, out


def _exchange_copies(src_refs, out_refs, send_sems, recv_sems, local_sems, scatter, with_recvs):
    me, peers = _peers()
    locals_, sends, recvs = [], [], []
    for a, (src_ref, out_ref) in enumerate(zip(src_refs, out_refs)):
        def mine(idx, src_ref=src_ref):
            return src_ref.at[idx] if scatter else src_ref

        locals_.append(pltpu.make_async_copy(mine(me), out_ref.at[me], local_sems.at[a]))
        for k, (dev, idx) in enumerate(peers):
            sends.append(pltpu.make_async_remote_copy(
                src_ref=mine(idx), dst_ref=out_ref.at[me], send_sem=send_sems.at[a, k], recv_sem=recv_sems.at[a, k],
                device_id=dev, device_id_type=pl.DeviceIdType.MESH))
            if with_recvs:
                recvs.append(pltpu.make_async_remote_copy(
                    src_ref=mine(idx), dst_ref=out_ref.at[idx], send_sem=send_sems.at[a, k],
                    recv_sem=recv_sems.at[a, k], device_id=dev, device_id_type=pl.DeviceIdType.MESH))
    return locals_, sends, recvs


def _remote(src, dst, send_sems, recv_sems, a, k, dev):
    return pltpu.make_async_remote_copy(src_ref=src, dst_ref=dst, send_sem=send_sems.at[a, k],
                                        recv_sem=recv_sems.at[a, k], device_id=dev,
                                        device_id_type=pl.DeviceIdType.MESH)


def _gather_places():
    x, y, c = lax.axis_index("x"), lax.axis_index("y"), lax.axis_index("c")
    chips = [(1 - x, y), (x, 1 - y), (1 - x, 1 - y)]
    sibling = (x, y, 1 - c)
    me_idx, sib_idx = 4 * x + 2 * y + c, 4 * x + 2 * y + 1 - c
    same_core = [((cx, cy, c), 4 * cx + 2 * cy + c) for cx, cy in chips]
    other_core_idx = [4 * cx + 2 * cy + 1 - c for cx, cy in chips]
    return sibling, me_idx, sib_idx, same_core, other_core_idx


def _gather_start(src_refs, out_refs, send_sems, recv_sems, local_sems):
    sibling, me_idx, _, same_core, _ = _gather_places()
    for a, (src, out) in enumerate(zip(src_refs, out_refs)):
        pltpu.make_async_copy(src, out.at[me_idx], local_sems.at[a]).start()
        _remote(src, out.at[me_idx], send_sems, recv_sems, a, 0, sibling).start()
        for j, (dev, _) in enumerate(same_core):
            _remote(src, out.at[me_idx], send_sems, recv_sems, a, 1 + j, dev).start()


def _gather_finish(src_refs, out_refs, send_sems, recv_sems, local_sems):
    sibling, me_idx, sib_idx, same_core, other_core_idx = _gather_places()
    arrays = list(enumerate(zip(src_refs, out_refs)))
    forwards = []
    for a, (src, out) in arrays:
        for j, (dev, idx) in enumerate(same_core):
            _remote(src, out.at[idx], send_sems, recv_sems, a, 1 + j, dev).wait_recv()
            fwd = _remote(out.at[idx], out.at[idx], send_sems, recv_sems, a, 4 + j, sibling)
            fwd.start()
            forwards.append(fwd)
    for a, (src, out) in arrays:
        _remote(src, out.at[sib_idx], send_sems, recv_sems, a, 0, sibling).wait_recv()
        for j, idx in enumerate(other_core_idx):
            _remote(src, out.at[idx], send_sems, recv_sems, a, 4 + j, sibling).wait_recv()
        _remote(src, out.at[me_idx], send_sems, recv_sems, a, 0, sibling).wait_send()
        for j, (dev, _) in enumerate(same_core):
            _remote(src, out.at[me_idx], send_sems, recv_sems, a, 1 + j, dev).wait_send()
        pltpu.make_async_copy(src, out.at[me_idx], local_sems.at[a]).wait()
    for fwd in forwards:
        fwd.wait_send()


def _exchange_start(*refs, scatter):
    if not scatter:
        return _gather_start(*refs)
    locals_, sends, _ = _exchange_copies(*refs, scatter=scatter, with_recvs=False)
    for cp in locals_ + sends:
        cp.start()


def _exchange_wait(*refs, scatter):
    if not scatter:
        return _gather_finish(*refs)
    locals_, sends, recvs = _exchange_copies(*refs, scatter=scatter, with_recvs=True)
    for cp in recvs:
        cp.wait_recv()
    for cp in sends:
        cp.wait_send()
    for cp in locals_:
        cp.wait()


def _halves_places():
    x, y, c = lax.axis_index("x"), lax.axis_index("y"), lax.axis_index("c")
    flips = [(1 - x, y), (x, 1 - y), (1 - x, 1 - y)]
    return (x, y, 1 - c), c, 2 * x + y, [((fx, fy, c), 2 * fx + fy) for fx, fy in flips]


def _pair_start(src_refs, out_refs, send_sems, recv_sems, local_sems):
    sibling, c, _, _ = _halves_places()
    for a, (src, out) in enumerate(zip(src_refs, out_refs)):
        for i in range(4):
            _remote(src.at[2 * i + 1 - c], out.at[i], send_sems, recv_sems, a, i, sibling).start()


def _pair_finish(src_refs, out_refs, send_sems, recv_sems, local_sems):
    sibling, c, _, _ = _halves_places()
    for a, (src, out) in enumerate(zip(src_refs, out_refs)):
        for i in range(4):
            _remote(src.at[2 * i + 1 - c], out.at[i], send_sems, recv_sems, a, i, sibling).wait()


def _chips_start(src_refs, out_refs, send_sems, recv_sems, local_sems):
    _, _, chip, others = _halves_places()
    for a, (src, out) in enumerate(zip(src_refs, out_refs)):
        pltpu.make_async_copy(src.at[chip], out.at[chip], local_sems.at[a]).start()
        for k, (dev, their_chip) in enumerate(others):
            _remote(src.at[their_chip], out.at[chip], send_sems, recv_sems, a, k, dev).start()


def _chips_finish(src_refs, out_refs, send_sems, recv_sems, local_sems):
    _, _, chip, others = _halves_places()
    for a, (src, out) in enumerate(zip(src_refs, out_refs)):
        for k, (dev, their_chip) in enumerate(others):
            _remote(src.at[their_chip], out.at[their_chip], send_sems, recv_sems, a, k, dev).wait_recv()
        for k, (dev, their_chip) in enumerate(others):
            _remote(src.at[their_chip], out.at[chip], send_sems, recv_sems, a, k, dev).wait_send()
        pltpu.make_async_copy(src.at[chip], out.at[chip], local_sems.at[a]).wait()


EXCHANGES = {
    "gather": (functools.partial(_exchange_start, scatter=False), functools.partial(_exchange_wait, scatter=False),
               N_DEV, False),
    "scatter": (functools.partial(_exchange_start, scatter=True), functools.partial(_exchange_wait, scatter=True),
                N_DEV, True),
    "pair": (_pair_start, _pair_finish, 4, True),
    "chips": (_chips_start, _chips_finish, 4, True),
}


def _exchange_sems(n_arrays):
    return [pltpu.SemaphoreType.DMA((n_arrays, N_DEV - 1)), pltpu.SemaphoreType.DMA((n_arrays, N_DEV - 1)),
            pltpu.SemaphoreType.DMA((n_arrays,))]


def _exchange_shapes(srcs, kind):
    lead, slabbed = EXCHANGES[kind][2:]
    return [jax.ShapeDtypeStruct((lead,) + tuple(s.shape[1:] if slabbed else s.shape), s.dtype) for s in srcs]


def _carries(carry):
    if carry is None:
        return []
    return [carry] if isinstance(carry, tuple) else list(carry)


def _call(body, *, name, grid, in_specs, out_specs, out_shape, args, scratch_shapes=(), carry=None):
    n_in, n_out, n_scr = len(in_specs), len(out_specs), len(scratch_shapes)
    groups = _carries(carry)
    sizes = [len(arrays) for arrays, _ in groups]
    nc = sum(sizes)

    def wrapped(*refs):
        ins, refs = refs[:n_in], refs[n_in:]
        csrc, refs = refs[:nc], refs[nc:]
        outs, refs = refs[:n_out], refs[n_out:]
        cland, refs = refs[:nc], refs[nc:]
        scr, sems = refs[:n_scr], refs[n_scr:]

        def run(phase):
            at = 0
            for gi, ((_, kind), size) in enumerate(zip(groups, sizes)):
                EXCHANGES[kind][phase](csrc[at:at + size], cland[at:at + size], *sems[3 * gi:3 * gi + 3])
                at += size

        if nc:
            pl.when(pl.program_id(0) == 0)(functools.partial(run, 0))
        if body is not None:
            body(*ins, *outs, *scr)
        if nc:
            pl.when(pl.program_id(0) == grid[0] - 1)(functools.partial(run, 1))

    res = pl.pallas_call(
        wrapped, name=name, grid=grid,
        in_specs=list(in_specs) + [ANY] * nc, out_specs=list(out_specs) + [ANY] * nc,
        out_shape=list(out_shape) + [s for arrays, kind in groups for s in _exchange_shapes(arrays, kind)],
        scratch_shapes=list(scratch_shapes) + [s for size in sizes for s in _exchange_sems(size)],
        compiler_params=_cparams(1),
    )(*args, *[a for arrays, _ in groups for a in arrays])
    return res[:n_out], res[n_out:]


def _row_tile(tm, d):
    return pl.BlockSpec((tm, d), lambda i: (i, 0))


def _acc_row(d):
    return pl.BlockSpec((1, d), lambda i: (0, 0))


def _ffn_fwd(x, g, w1t, w3t, w2, name, carry=None):
    t = x.shape[0]
    tm = _tile(t)
    nf = D_FF // FFN_FT

    def body(x_ref, g_ref, w1_ref, w3_ref, w2_ref, o_ref, a_ref, b_ref, n_ref):
        xv = x_ref[...]
        xhat, _ = _rms_parts(xv)
        n = (xhat * g_ref[...]).astype(BF16)
        n_ref[...] = n
        o_ref[...] = xv

        def fstep(f, c):
            rows = pl.ds(pl.multiple_of(f * FFN_FT, FFN_FT), FFN_FT)
            a = _nt(n, w1_ref[rows, :])
            b = _nt(n, w3_ref[rows, :])
            a_ref[f] = a.astype(BF16)
            b_ref[f] = b.astype(BF16)
            s = (a * jax.nn.sigmoid(a) * b).astype(BF16)
            o_ref[...] += 0.5 * _nn(s, w2_ref[rows, :])
            return c

        lax.fori_loop(0, nf, fstep, 0, unroll=True)

    blk3 = pl.BlockSpec((nf, tm, FFN_FT), lambda i: (0, i, 0))
    sh3 = jax.ShapeDtypeStruct((nf, t, FFN_FT), BF16)
    (h, a3, b3, n), landed = _call(
        body, name=name, grid=(t // tm,),
        in_specs=[_row_tile(tm, D_MODEL), _acc_row(D_MODEL), VMEM_FULL, VMEM_FULL, VMEM_FULL],
        out_specs=[_row_tile(tm, D_MODEL), blk3, blk3, _row_tile(tm, D_MODEL)],
        out_shape=[jax.ShapeDtypeStruct((t, D_MODEL), F32), sh3, sh3, jax.ShapeDtypeStruct((t, D_MODEL), BF16)],
        args=(x, g, w1t, w3t, w2), carry=carry)
    return h, (a3, b3, n), landed


def _ffn_bwd(x, dh, g, a3, b3, w1t, w3t, w2, name, carry=None):
    t = x.shape[0]
    tm = _tile(t) // 2
    nf = D_FF // FFN_FT

    def body(x_ref, dh_ref, g_ref, a_ref, b_ref, w1_ref, w3_ref, w2_ref,
             dx_ref, dg_ref, da_ref, db_ref, s_ref, dhh_ref, dn_acc):
        i = pl.program_id(0)
        xv = x_ref[...]
        gv = g_ref[...]
        xhat, r = _rms_parts(xv)
        dhv = dh_ref[...]
        dhh = (0.5 * dhv).astype(BF16)
        dhh_ref[...] = dhh
        dn_acc[...] = jnp.zeros_like(dn_acc)

        def fstep(f, c):
            rows = pl.ds(pl.multiple_of(f * FFN_FT, FFN_FT), FFN_FT)
            w1c, w3c, w2c = w1_ref[rows, :], w3_ref[rows, :], w2_ref[rows, :]
            a = a_ref[f].astype(F32)
            b = b_ref[f].astype(F32)
            sg = jax.nn.sigmoid(a)
            sl = a * sg
            ds = _nt(dhh, w2c)
            da = (ds * b * sg * (1.0 + a * (1.0 - sg))).astype(BF16)
            db = (ds * sl).astype(BF16)
            s_ref[f] = (sl * b).astype(BF16)
            da_ref[f] = da
            db_ref[f] = db
            dn_acc[...] += _nn(da, w1c) + _nn(db, w3c)
            return c

        lax.fori_loop(0, nf, fstep, 0, unroll=True)
        dx, dg = _rms_bwd(dn_acc[...], gv, xhat, r)
        dx_ref[...] = dhv + dx

        @pl.when(i == 0)
        def _():
            dg_ref[...] = jnp.zeros_like(dg_ref)

        dg_ref[...] += dg

    blk3 = pl.BlockSpec((nf, tm, FFN_FT), lambda i: (0, i, 0))
    sh3 = jax.ShapeDtypeStruct((nf, t, FFN_FT), BF16)
    return _call(
        body, name=name, grid=(t // tm,),
        in_specs=[_row_tile(tm, D_MODEL), _row_tile(tm, D_MODEL), _acc_row(D_MODEL), blk3, blk3,
                  VMEM_FULL, VMEM_FULL, VMEM_FULL],
        out_specs=[_row_tile(tm, D_MODEL), _acc_row(D_MODEL), blk3, blk3, blk3, _row_tile(tm, D_MODEL)],
        out_shape=[jax.ShapeDtypeStruct((t, D_MODEL), F32), jax.ShapeDtypeStruct((1, D_MODEL), F32), sh3, sh3, sh3,
                   jax.ShapeDtypeStruct((t, D_MODEL), BF16)],
        scratch_shapes=[pltpu.VMEM((tm, D_MODEL), F32)],
        args=(x, dh, g, a3, b3, w1t, w3t, w2), carry=carry)


def _mm_tn(a, b, name, carry=None):
    t, n = b.shape
    kc = min(512, t)
    if a.ndim == 3:
        nb, _, tb = a.shape
        a_spec = pl.BlockSpec((1, t, tb), lambda i: (i, 0, 0))
    else:
        m = a.shape[1]
        tb = min(m, 256)
        nb = m // tb
        a_spec = pl.BlockSpec((t, tb), lambda i: (0, i))
    three_d = a.ndim == 3

    def body(a_ref, b_ref, o_ref, acc):
        acc[...] = jnp.zeros_like(acc)

        def kstep(k, c):
            rows = pl.ds(pl.multiple_of(k * kc, kc), kc)
            av = a_ref[0, rows, :] if three_d else a_ref[rows, :]
            acc[...] += _tn(av.astype(BF16), b_ref[rows, :])
            return c

        lax.fori_loop(0, t // kc, kstep, 0, unroll=True)
        o_ref[...] = acc[...].astype(BF16)

    (out,), landed = _call(
        body, name=name, grid=(nb,),
        in_specs=[a_spec, VMEM_FULL],
        out_specs=[pl.BlockSpec((tb, n), lambda i: (i, 0))],
        out_shape=[jax.ShapeDtypeStruct((nb * tb, n), BF16)],
        scratch_shapes=[pltpu.VMEM((tb, n), F32)],
        args=(a, b), carry=carry)
    return (out, landed) if carry is not None else out


def _mix_pre_fwd(h, g, wint, carry=None):
    t = h.shape[0]
    tm = _tile(t)

    def body(h_ref, g_ref, w_ref, u_ref, *outs):
        xhat, _ = _rms_parts(h_ref[...])
        u = (xhat * g_ref[...]).astype(BF16)
        u_ref[...] = u
        for o_ref, off, size in zip(outs, IN_OFFS, IN_SIZES):
            o_ref[...] = _nt(u, w_ref[off:off + size, :])

    return _call(
        body, name="mix_pre_fwd", grid=(t // tm,),
        in_specs=[_row_tile(tm, D_MODEL), _acc_row(D_MODEL), VMEM_FULL],
        out_specs=[_row_tile(tm, D_MODEL)] + [_row_tile(tm, s) for s in IN_SIZES],
        out_shape=[jax.ShapeDtypeStruct((t, D_MODEL), BF16)] + [jax.ShapeDtypeStruct((t, s), F32) for s in IN_SIZES],
        args=(h, g, wint), carry=carry)


def _mix_pre_bwd(h, g, wint, dh2, dz, carry=None):
    t = h.shape[0]
    tm = _tile(t)

    def body(h_ref, g_ref, w_ref, dh2_ref, *rest):
        dz_refs, (dh1_ref, dg_ref) = rest[:len(IN_SIZES)], rest[len(IN_SIZES):]
        i = pl.program_id(0)
        gv = g_ref[...]
        xhat, r = _rms_parts(h_ref[...])
        du = jnp.zeros((tm, D_MODEL), F32)
        for dz_ref, off, size in zip(dz_refs, IN_OFFS, IN_SIZES):
            du = du + _nn(dz_ref[...].astype(BF16), w_ref[off:off + size, :])
        dx, dg = _rms_bwd(du, gv, xhat, r)
        dh1_ref[...] = dh2_ref[...] + dx

        @pl.when(i == 0)
        def _():
            dg_ref[...] = jnp.zeros_like(dg_ref)

        dg_ref[...] += dg

    return _call(
        body, name="mix_pre_bwd", grid=(t // tm,),
        in_specs=[_row_tile(tm, D_MODEL), _acc_row(D_MODEL), VMEM_FULL, _row_tile(tm, D_MODEL)]
        + [_row_tile(tm, s) for s in IN_SIZES],
        out_specs=[_row_tile(tm, D_MODEL), _acc_row(D_MODEL)],
        out_shape=[jax.ShapeDtypeStruct((t, D_MODEL), F32), jax.ShapeDtypeStruct((1, D_MODEL), F32)],
        args=(h, g, wint, dh2, *dz), carry=carry)


def _disc_math(lre, lim, ldt, bre, bim):
    dt = jnp.exp(ldt)
    mag = jnp.exp(lre * dt)
    ar = mag * jnp.cos(lim * dt)
    ai = mag * jnp.sin(lim * dt)
    den = lre * lre + lim * lim
    nr = ar - 1.0
    fr = (nr * lre + ai * lim) / den
    fi = (ai * lre - nr * lim) / den
    fr, fi = fr[:, None, :], fi[:, None, :]
    return ar, ai, fr * bre - fi * bim, fr * bim + fi * bre


def _s5_disc(lre, lim, ldt, bre, bim):
    def body(lre_ref, lim_ref, ldt_ref, bre_ref, bim_ref, ar_ref, ai_ref, bbr_ref, bbi_ref):
        ar, ai, bbr, bbi = _disc_math(lre_ref[...], lim_ref[...], ldt_ref[...], bre_ref[...], bim_ref[...])
        ar_ref[...] = ar
        ai_ref[...] = ai
        bbr_ref[...] = bbr
        bbi_ref[...] = bbi

    small = jax.ShapeDtypeStruct(lre.shape, F32)
    big = jax.ShapeDtypeStruct(bre.shape, F32)
    return pl.pallas_call(body, name="s5_disc", out_shape=[small, small, big, big],
                          in_specs=[VMEM_FULL] * 5, out_specs=[VMEM_FULL] * 4)(lre, lim, ldt, bre, bim)


def _s5_disc_bwd(lre, lim, ldt, bre, bim, dar, dai, dbbr, dbbi):
    def body(lre_ref, lim_ref, ldt_ref, bre_ref, bim_ref, dar_ref, dai_ref, dbbr_ref, dbbi_ref,
             glre_ref, glim_ref, gldt_ref, gbre_ref, gbim_ref):
        _, vjp = jax.vjp(_disc_math, lre_ref[...], lim_ref[...], ldt_ref[...], bre_ref[...], bim_ref[...])
        glre, glim, gldt, gbre, gbim = vjp((dar_ref[...], dai_ref[...], dbbr_ref[...], dbbi_ref[...]))
        glre_ref[...] = glre
        glim_ref[...] = glim
        gldt_ref[...] = gldt
        gbre_ref[...] = gbre
        gbim_ref[...] = gbim

    small = jax.ShapeDtypeStruct(lre.shape, F32)
    big = jax.ShapeDtypeStruct(bre.shape, F32)
    return pl.pallas_call(body, name="s5_disc_bwd",
                          out_shape=[small, small, jax.ShapeDtypeStruct(ldt.shape, F32), big, big],
                          in_specs=[VMEM_FULL] * 9, out_specs=[VMEM_FULL] * 5,
                          )(lre, lim, ldt, bre, bim, dar, dai, dbbr, dbbi)


def _cmul(ar, ai, br, bi):
    return ar * br - ai * bi, ar * bi + ai * br


def _cpow(ar, ai, n):
    rr, ri = None, None
    pr, pi = ar, ai
    while n:
        if n & 1:
            rr, ri = (pr, pi) if rr is None else _cmul(rr, ri, pr, pi)
        n >>= 1
        if n:
            pr, pi = _cmul(pr, pi, pr, pi)
    return rr, ri


def _shift_rows(v, down):
    row = lax.broadcasted_iota(jnp.int32, v.shape, 0)
    if down:
        return jnp.where(row == 0, 0.0, pltpu.roll(v, 1, 0))
    return jnp.where(row == S5_SEGS - 1, 0.0, pltpu.roll(v, S5_SEGS - 1, 0))


def _chain_segments(er, ei, pr, pi, down):
    fr, fi = er, ei
    for _ in range(S5_SEGS - 1):
        sr, si = _shift_rows(fr, down), _shift_rows(fi, down)
        mr, mi = _cmul(pr, pi, sr, si)
        fr, fi = er + mr, ei + mi
    return _shift_rows(fr, down), _shift_rows(fi, down)


def _rows_to_scan_order(src_ref, dst_ref, t):
    ls = t // S5_SEGS

    def tile(j, c):
        dst_ref[pl.ds(pl.multiple_of(j * S5_SEGS, S5_SEGS), S5_SEGS), :] = src_ref[pl.ds(j, S5_SEGS, stride=ls), :]
        return c

    lax.fori_loop(0, ls, tile, 0, unroll=8)


def _rows_from_scan_order(src_ref, dst_ref, t):
    ls = t // S5_SEGS
    for s in range(S5_SEGS):
        def tile(jb, c, s=s):
            dst_ref[pl.ds(pl.multiple_of(s * ls + jb * 8, 8), 8), :] = (
                src_ref[pl.ds(jb * 8 * S5_SEGS + s, 8, stride=S5_SEGS), :])
            return c

        lax.fori_loop(0, ls // 8, tile, 0, unroll=8)


def _s5_fwd(ug, bd, ctd, ar4, ai4, dskip, carry=None):
    t = ug.shape[0]
    ls = t // S5_SEGS
    rc = min(512, t)
    ns = S5_BSTATE

    def body(ugn_ref, bd_ref, ct_ref, ar_ref, ai_ref, d_ref, xs_hbm, yn_ref, buf, ug_ref, y_ref, sem):
        cb = pl.program_id(0)
        bdv = bd_ref[0]
        _rows_to_scan_order(ugn_ref, ug_ref, t)

        def mm(i, c):
            rows = pl.ds(pl.multiple_of(i * rc, rc), rc)
            buf[rows, :] = _nn(ug_ref[rows, :].astype(BF16), bdv)
            return c

        lax.fori_loop(0, t // rc, mm, 0, unroll=True)
        arb = jnp.broadcast_to(ar_ref[0], (S5_SEGS, ns))
        aib = jnp.broadcast_to(ai_ref[0], (S5_SEGS, ns))

        def step(j, c, store):
            sr, si = c
            rows = pl.ds(pl.multiple_of(j * S5_SEGS, S5_SEGS), S5_SEGS)
            nr = arb * sr - aib * si + buf[rows, 0:ns]
            ni = arb * si + aib * sr + buf[rows, ns:2 * ns]
            if store:
                buf[rows, 0:ns] = nr
                buf[rows, ns:2 * ns] = ni
            return nr, ni

        zero = jnp.zeros((S5_SEGS, ns), F32)
        er, ei = lax.fori_loop(0, ls, functools.partial(step, store=False), (zero, zero))
        pr, pi = _cpow(arb, aib, ls)
        init = _chain_segments(er, ei, pr, pi, down=True)
        lax.fori_loop(0, ls, functools.partial(step, store=True), init)

        out = pltpu.make_async_copy(buf, xs_hbm.at[cb], sem)
        out.start()
        ctv = ct_ref[0]
        dv = d_ref[...]

        def ymm(i, c):
            rows = pl.ds(pl.multiple_of(i * rc, rc), rc)
            y_ref[rows, :] = _nn(buf[rows, :].astype(BF16), ctv) + dv * ug_ref[rows, :]
            return c

        lax.fori_loop(0, t // rc, ymm, 0, unroll=True)
        _rows_from_scan_order(y_ref, yn_ref, t)
        out.wait()

    return _call(
        body, name="s5_fwd", grid=(S5_BLOCKS,),
        in_specs=[pl.BlockSpec((t, 128), lambda i: (0, i)),
                  pl.BlockSpec((1, 128, 2 * ns), lambda i: (i, 0, 0)),
                  pl.BlockSpec((1, 2 * ns, 128), lambda i: (i, 0, 0)),
                  pl.BlockSpec((1, 1, ns), lambda i: (i, 0, 0)),
                  pl.BlockSpec((1, 1, ns), lambda i: (i, 0, 0)),
                  pl.BlockSpec((1, 128), lambda i: (0, i))],
        out_specs=[ANY, pl.BlockSpec((t, 128), lambda i: (0, i))],
        out_shape=[jax.ShapeDtypeStruct((S5_BLOCKS, t, 2 * ns), F32), jax.ShapeDtypeStruct((t, S5_WIDTH), F32)],
        scratch_shapes=[pltpu.VMEM((t, 2 * ns), F32), pltpu.VMEM((t, 128), F32), pltpu.VMEM((t, 128), F32),
                        pltpu.SemaphoreType.DMA(())],
        args=(ug, bd, ctd, ar4, ai4, dskip), carry=carry)


def _s5_bwd(dy, ug, xs, cd, bdt, ar4, ai4, dskip, carry=None):
    t = ug.shape[0]
    ls = t // S5_SEGS
    rc = min(512, t)
    ns = S5_BSTATE

    def body(dyn_ref, ugn_ref, xs_hbm, cd_ref, bdt_ref, ar_ref, ai_ref, d_ref,
             dugn_ref, dbd_ref, dcd_ref, dd_ref, dar_ref, dai_ref, xbuf, lam, dy_ref, ug_ref, dug_ref, sem):
        cb = pl.program_id(0)
        load = pltpu.make_async_copy(xs_hbm.at[cb], xbuf, sem)
        load.start()
        cdv = cd_ref[0]
        _rows_to_scan_order(dyn_ref, dy_ref, t)
        _rows_to_scan_order(ugn_ref, ug_ref, t)

        def mm(i, c):
            rows = pl.ds(pl.multiple_of(i * rc, rc), rc)
            lam[rows, :] = _nn(dy_ref[rows, :].astype(BF16), cdv)
            return c

        lax.fori_loop(0, t // rc, mm, 0, unroll=True)
        arb = jnp.broadcast_to(ar_ref[0], (S5_SEGS, ns))
        aib = jnp.broadcast_to(ai_ref[0], (S5_SEGS, ns))

        def lam_step(j, lr, li):
            rows = pl.ds(pl.multiple_of(j * S5_SEGS, S5_SEGS), S5_SEGS)
            nr = arb * lr + aib * li + lam[rows, 0:ns]
            ni = arb * li - aib * lr + lam[rows, ns:2 * ns]
            return rows, nr, ni

        def pass1(jj, c):
            _, nr, ni = lam_step(ls - 1 - jj, *c)
            return nr, ni

        zero = jnp.zeros((S5_SEGS, ns), F32)
        er, ei = lax.fori_loop(0, ls, pass1, (zero, zero))
        pr, pi = _cpow(arb, aib, ls)
        init = _chain_segments(er, ei, pr, -pi, down=False)
        load.wait()

        def accumulate(acc, nr, ni, xpr, xpi):
            return acc[0] + nr * xpr + ni * xpi, acc[1] + ni * xpr - nr * xpi

        def pass2(jj, c):
            lr, li, accr, acci = c
            j = ls - 1 - jj
            rows, nr, ni = lam_step(j, lr, li)
            lam[rows, 0:ns] = nr
            lam[rows, ns:2 * ns] = ni
            prev = pl.ds(pl.multiple_of((j - 1) * S5_SEGS, S5_SEGS), S5_SEGS)
            accr, acci = accumulate((accr, acci), nr, ni, xbuf[prev, 0:ns], xbuf[prev, ns:2 * ns])
            return nr, ni, accr, acci

        lr, li, accr, acci = lax.fori_loop(0, ls - 1, pass2, (init[0], init[1], zero, zero))
        rows, nr, ni = lam_step(0, lr, li)
        lam[rows, 0:ns] = nr
        lam[rows, ns:2 * ns] = ni
        last = pl.ds((ls - 1) * S5_SEGS, S5_SEGS)
        accr, acci = accumulate((accr, acci), nr, ni,
                                _shift_rows(xbuf[last, 0:ns], True), _shift_rows(xbuf[last, ns:2 * ns], True))
        dar_ref[0] = jnp.sum(accr, axis=0, keepdims=True)
        dai_ref[0] = jnp.sum(acci, axis=0, keepdims=True)

        bdtv = bdt_ref[0]
        dv = d_ref[...]
        dbd_ref[...] = jnp.zeros_like(dbd_ref)
        dcd_ref[...] = jnp.zeros_like(dcd_ref)
        dd_ref[...] = jnp.zeros_like(dd_ref)

        def tail(i, c):
            rows = pl.ds(pl.multiple_of(i * rc, rc), rc)
            dy = dy_ref[rows, :]
            ug = ug_ref[rows, :]
            lb = lam[rows, :].astype(BF16)
            dug_ref[rows, :] = _nn(lb, bdtv) + dv * dy
            dbd_ref[0] += _tn(ug.astype(BF16), lb)
            dcd_ref[0] += _tn(dy.astype(BF16), xbuf[rows, :].astype(BF16))
            dd_ref[...] += jnp.sum(dy * ug, axis=0, keepdims=True)
            return c

        lax.fori_loop(0, t // rc, tail, 0, unroll=True)
        _rows_from_scan_order(dug_ref, dugn_ref, t)

    chan = pl.BlockSpec((t, 128), lambda i: (0, i))
    dense = pl.BlockSpec((1, 128, 2 * ns), lambda i: (i, 0, 0))
    vec = pl.BlockSpec((1, 1, ns), lambda i: (i, 0, 0))
    return _call(
        body, name="s5_bwd", grid=(S5_BLOCKS,),
        in_specs=[chan, chan, ANY, dense, pl.BlockSpec((1, 2 * ns, 128), lambda i: (i, 0, 0)), vec, vec,
                  pl.BlockSpec((1, 128), lambda i: (0, i))],
        out_specs=[chan, dense, dense, pl.BlockSpec((1, 128), lambda i: (0, i)), vec, vec],
        out_shape=[jax.ShapeDtypeStruct((t, S5_WIDTH), F32),
                   jax.ShapeDtypeStruct((S5_BLOCKS, 128, 2 * ns), F32),
                   jax.ShapeDtypeStruct((S5_BLOCKS, 128, 2 * ns), F32),
                   jax.ShapeDtypeStruct((1, S5_WIDTH), F32),
                   jax.ShapeDtypeStruct((S5_BLOCKS, 1, ns), F32),
                   jax.ShapeDtypeStruct((S5_BLOCKS, 1, ns), F32)],
        scratch_shapes=[pltpu.VMEM((t, 2 * ns), F32), pltpu.VMEM((t, 2 * ns), F32)]
        + [pltpu.VMEM((t, 128), F32)] * 3 + [pltpu.SemaphoreType.DMA(())],
        args=(dy, ug, xs, cd, bdt, ar4, ai4, dskip), carry=carry)


def _cumsum_rows(x, reverse):
    c = x.shape[0]
    row = lax.broadcasted_iota(jnp.int32, x.shape, 0)
    d = 1
    while d < c:
        if reverse:
            x = x + jnp.where(row < c - d, pltpu.roll(x, c - d, 0), 0.0)
        else:
            x = x + jnp.where(row >= d, pltpu.roll(x, d, 0), 0.0)
        d *= 2
    return x


def _gla_common(q, k, alow, wup, bup):
    c = GLA_CHUNK
    pre = _nn(alow.astype(BF16), wup.astype(BF16)) + bup
    la = (jnp.minimum(pre, 0.0) - jnp.log(1.0 + jnp.exp(-jnp.abs(pre)))) * (1.0 / GLA_TAU)
    rr = lax.broadcasted_iota(jnp.int32, (c, c), 0)
    cc = lax.broadcasted_iota(jnp.int32, (c, c), 1)
    tril = (rr >= cc).astype(F32)
    bc = _cumsum_rows(la, reverse=False)
    bl = bc[c - 1:c, :]
    e_pos = jnp.exp(bc)
    e_neg = jnp.exp(-bc)
    e_end = jnp.exp(bl - bc)
    qt = q * (GLA_DK ** -0.5) * e_pos
    kt = k * e_neg
    ke = k * e_end
    lane = lax.broadcasted_iota(jnp.int32, (1, GLA_KEY), 1)
    masks = [((lane >= h * GLA_DK) & (lane < (h + 1) * GLA_DK)).astype(F32) for h in range(GLA_HEADS)]
    return dict(pre=pre, tril=tril, bc=bc, bl=bl, e_pos=e_pos, e_neg=e_neg, e_end=e_end,
                qt=qt, kt=kt, ke=ke, dec=jnp.exp(bl), masks=masks)


def _gla_fwd(q, k, v, alow, wup, bup, carry=None):
    t = q.shape[0]
    c = GLA_CHUNK
    n = t // c
    step = GLA_STEP_CHUNKS * c

    def body(q_ref, k_ref, v_ref, al_ref, wup_ref, bup_ref, o_ref, ss_ref, s_ref):
        i = pl.program_id(0)

        @pl.when(i == 0)
        def _():
            s_ref[...] = jnp.zeros_like(s_ref)

        wup_v, bup_v = wup_ref[...], bup_ref[...]
        s = s_ref[...]
        for j in range(GLA_STEP_CHUNKS):
            tok = slice(j * c, (j + 1) * c)
            m = _gla_common(q_ref[tok, :], k_ref[tok, :], al_ref[tok, :], wup_v, bup_v)
            ss_ref[j] = s
            sb = s.astype(BF16)
            ktb = m["kt"].astype(BF16)
            update = jnp.zeros_like(s)
            for h in range(GLA_HEADS):
                mask = m["masks"][h]
                qm = (m["qt"] * mask).astype(BF16)
                vh = v_ref[tok, h * GLA_DV:(h + 1) * GLA_DV].astype(BF16)
                p = (m["tril"] * _nt(qm, ktb)).astype(BF16)
                o_ref[tok, h * GLA_DV:(h + 1) * GLA_DV] = _nn(p, vh) + _nt(qm, sb)
                update = update + _tn(vh, (m["ke"] * mask).astype(BF16))
            s = m["dec"] * s + update
        s_ref[...] = s

    return _call(
        body, name="gla_fwd", grid=(t // step,),
        in_specs=[_row_tile(step, GLA_KEY), _row_tile(step, GLA_KEY), _row_tile(step, GLA_VAL),
                  _row_tile(step, GLA_RANK), VMEM_FULL, VMEM_FULL],
        out_specs=[_row_tile(step, GLA_VAL), pl.BlockSpec((GLA_STEP_CHUNKS, GLA_DV, GLA_KEY), lambda i: (i, 0, 0))],
        out_shape=[jax.ShapeDtypeStruct((t, GLA_VAL), F32), jax.ShapeDtypeStruct((n, GLA_DV, GLA_KEY), F32)],
        scratch_shapes=[pltpu.VMEM((GLA_DV, GLA_KEY), F32)],
        args=(q, k, v, alow, wup, bup), carry=carry)


def _gla_bwd(q, k, v, alow, wup, bup, ssave, do, carry=None):
    t = q.shape[0]
    c = GLA_CHUNK
    n = t // c

    def body(q_ref, k_ref, v_ref, al_ref, wup_ref, bup_ref, ss_ref, do_ref,
             dq_ref, dk_ref, dv_ref, dal_ref, dwup_ref, dbup_ref, ds_ref):
        i = pl.program_id(0)

        @pl.when(i == 0)
        def _():
            ds_ref[...] = jnp.zeros_like(ds_ref)
            dwup_ref[...] = jnp.zeros_like(dwup_ref)
            dbup_ref[...] = jnp.zeros_like(dbup_ref)

        wup_v, bup_v = wup_ref[...], bup_ref[...]
        ds_in = ds_ref[...]
        dwup = jnp.zeros((GLA_RANK, GLA_KEY), F32)
        dbup = jnp.zeros((1, GLA_KEY), F32)
        for j in reversed(range(GLA_STEP_CHUNKS)):
            tok = slice(j * c, (j + 1) * c)
            alow_v = al_ref[tok, :]
            m = _gla_common(q_ref[tok, :], k_ref[tok, :], alow_v, wup_v, bup_v)
            s = ss_ref[j]
            sb = s.astype(BF16)
            dsb = ds_in.astype(BF16)
            qt, kt, ke = m["qt"], m["kt"], m["ke"]
            ktb = kt.astype(BF16)
            dqt = jnp.zeros((c, GLA_KEY), F32)
            dkt = jnp.zeros((c, GLA_KEY), F32)
            dke = jnp.zeros((c, GLA_KEY), F32)
            update = jnp.zeros_like(ds_in)
            for h in range(GLA_HEADS):
                mask = m["masks"][h]
                qm = (qt * mask).astype(BF16)
                km = (kt * mask).astype(BF16)
                kem = (ke * mask).astype(BF16)
                cols = slice(h * GLA_DV, (h + 1) * GLA_DV)
                vh = v_ref[tok, cols].astype(BF16)
                doh = do_ref[tok, cols].astype(BF16)
                p = (m["tril"] * _nt(qm, ktb)).astype(BF16)
                dp = (m["tril"] * _nt(doh, vh)).astype(BF16)
                dv_ref[tok, cols] = _tn(p, doh) + _nt(kem, dsb)
                dqt = dqt + _nn(dp, km) + _nn(doh, sb) * mask
                dkt = dkt + _tn(dp, qm)
                dke = dke + _nn(vh, dsb) * mask
                update = update + _tn(doh, qm)
            ddec = jnp.sum(ds_in * s, axis=0, keepdims=True)
            dq_ref[tok, :] = dqt * m["e_pos"] * (GLA_DK ** -0.5)
            dk_ref[tok, :] = dkt * m["e_neg"] + dke * m["e_end"]
            dkeke = dke * ke
            dbl = jnp.sum(dkeke, axis=0, keepdims=True) + ddec * m["dec"]
            last = (lax.broadcasted_iota(jnp.int32, (c, 1), 0) == c - 1).astype(F32)
            dla = _cumsum_rows(dqt * qt - dkt * kt - dkeke + last * dbl, reverse=True)
            dpre = dla * (1.0 / GLA_TAU) * jax.nn.sigmoid(-m["pre"])
            dpb = dpre.astype(BF16)
            dal_ref[tok, :] = _nt(dpb, wup_v.astype(BF16))
            dwup = dwup + _tn(alow_v.astype(BF16), dpb)
            dbup = dbup + jnp.sum(dpre, axis=0, keepdims=True)
            ds_in = m["dec"] * ds_in + update
        ds_ref[...] = ds_in
        dwup_ref[...] += dwup
        dbup_ref[...] += dbup

    step = GLA_STEP_CHUNKS * c
    nsteps = t // step

    def rev(d):
        return pl.BlockSpec((step, d), lambda i: (nsteps - 1 - i, 0))

    return _call(
        body, name="gla_bwd", grid=(nsteps,),
        in_specs=[rev(GLA_KEY), rev(GLA_KEY), rev(GLA_VAL), rev(GLA_RANK), VMEM_FULL, VMEM_FULL,
                  pl.BlockSpec((GLA_STEP_CHUNKS, GLA_DV, GLA_KEY), lambda i: (nsteps - 1 - i, 0, 0)), rev(GLA_VAL)],
        out_specs=[rev(GLA_KEY), rev(GLA_KEY), rev(GLA_VAL), rev(GLA_RANK),
                   pl.BlockSpec((GLA_RANK, GLA_KEY), lambda i: (0, 0)), _acc_row(GLA_KEY)],
        out_shape=[jax.ShapeDtypeStruct((t, GLA_KEY), F32), jax.ShapeDtypeStruct((t, GLA_KEY), F32),
                   jax.ShapeDtypeStruct((t, GLA_VAL), F32), jax.ShapeDtypeStruct((t, GLA_RANK), F32),
                   jax.ShapeDtypeStruct((GLA_RANK, GLA_KEY), F32), jax.ShapeDtypeStruct((1, GLA_KEY), F32)],
        scratch_shapes=[pltpu.VMEM((GLA_DV, GLA_KEY), F32)],
        args=(q, k, v, alow, wup, bup, ssave, do), carry=carry)


def _post_math(y, o, r, gs5, ggla, wg, bg, gn, ps5t, pglat):
    y2 = y * y
    th = jnp.tanh(GELU_C0 * (y + GELU_C1 * y * y2))
    z5 = 0.5 * y * (1.0 + th)
    z5b = z5.astype(BF16)
    gate = jax.nn.sigmoid(_nn(z5b, wg) + bg)
    ys5 = z5 * gate
    rs, on = [], []
    for h in range(GLA_HEADS):
        oh = o[:, h * GLA_DV:(h + 1) * GLA_DV]
        rh = lax.rsqrt(jnp.mean(oh * oh, axis=-1, keepdims=True) + EPS)
        rs.append(rh)
        on.append(oh * rh)
    on = jnp.concatenate(on, axis=-1)
    sr = jax.nn.sigmoid(r)
    silu_r = r * sr
    ygla = on * gn * silu_r
    ys5b, yglab = ys5.astype(BF16), ygla.astype(BF16)
    m5 = _nt(ys5b, ps5t)
    mg = _nt(yglab, pglat)
    s5g, glag = jax.nn.sigmoid(gs5), jax.nn.sigmoid(ggla)
    merged = s5g * m5 + glag * mg
    return dict(y2=y2, th=th, z5=z5, z5b=z5b, gate=gate, ys5b=ys5b, yglab=yglab, rs=rs, on=on, sr=sr,
                silu_r=silu_r, m5=m5, mg=mg, s5g=s5g, glag=glag, mergedb=merged.astype(BF16))


def _mix_post_fwd(y, o, r, gs5, ggla, h1, wg, bg, gn, ps5t, pglat, wout, carry=None):
    t = o.shape[0]
    tm = _tile(t)

    def body(y_ref, o_ref, r_ref, gs5_ref, ggla_ref, h1_ref, wg_ref, bg_ref, gn_ref, ps_ref, pg_ref, wo_ref, h2_ref):
        m = _post_math(y_ref[...], o_ref[...], r_ref[...], gs5_ref[...], ggla_ref[...],
                       wg_ref[...], bg_ref[...], gn_ref[...], ps_ref[...], pg_ref[...])
        h2_ref[...] = h1_ref[...] + _nn(m["mergedb"], wo_ref[...])

    (h2,), landed = _call(
        body, name="mix_post_fwd", grid=(t // tm,),
        in_specs=[_row_tile(tm, 512)] * 3 + [_row_tile(tm, D_MODEL)] * 3
        + [VMEM_FULL, _acc_row(512), _acc_row(512), VMEM_FULL, VMEM_FULL, VMEM_FULL],
        out_specs=[_row_tile(tm, D_MODEL)],
        out_shape=[jax.ShapeDtypeStruct((t, D_MODEL), F32)],
        args=(y, o, r, gs5, ggla, h1, wg, bg, gn, ps5t, pglat, wout), carry=carry)
    return h2, landed


def _mix_post_bwd(y, o, r, gs5, ggla, dh2, wg, bg, gn, ps5t, pglat, wout, carry=None):
    t = o.shape[0]
    tm = _tile(t) // 2

    def body(y_ref, o_ref, r_ref, gs5_ref, ggla_ref, dh2_ref, wg_ref, bg_ref, gn_ref, ps_ref, pg_ref, wo_ref,
             dy_ref, do_ref, dr_ref, dgs5_ref, dggla_ref, dbg_ref, dgn_ref,
             z5b_ref, dgp_ref, ys5b_ref, dm5b_ref, yglab_ref, dmgb_ref, mergedb_ref, dh2b_ref):
        i = pl.program_id(0)
        yv, ov, rv = y_ref[...], o_ref[...], r_ref[...]
        wg, gn, ps5t, pglat = wg_ref[...], gn_ref[...], ps_ref[...], pg_ref[...]
        m = _post_math(yv, ov, rv, gs5_ref[...], ggla_ref[...], wg, bg_ref[...], gn, ps5t, pglat)
        dh2b = dh2_ref[...].astype(BF16)
        dmerged = _nt(dh2b, wo_ref[...])
        s5g, glag = m["s5g"], m["glag"]
        dgs5_ref[...] = dmerged * m["m5"] * s5g * (1.0 - s5g)
        dggla_ref[...] = dmerged * m["mg"] * glag * (1.0 - glag)
        dm5b = (dmerged * s5g).astype(BF16)
        dmgb = (dmerged * glag).astype(BF16)
        dys5 = _nn(dm5b, ps5t)
        dygla = _nn(dmgb, pglat)
        gate, z5, th = m["gate"], m["z5"], m["th"]
        dgpre = dys5 * z5 * gate * (1.0 - gate)
        dgpb = dgpre.astype(BF16)
        dz5 = dys5 * gate + _nt(dgpb, wg)
        dgelu = 0.5 * (1.0 + th) + 0.5 * yv * (1.0 - th * th) * GELU_C0 * (1.0 + 3.0 * GELU_C1 * m["y2"])
        dy_ref[...] = dz5 * dgelu
        on, sr, silu_r = m["on"], m["sr"], m["silu_r"]
        dr_ref[...] = dygla * on * gn * sr * (1.0 + rv * (1.0 - sr))
        dgn = jnp.sum(dygla * on * silu_r, axis=0, keepdims=True)
        don = dygla * gn * silu_r
        for h in range(GLA_HEADS):
            cols = slice(h * GLA_DV, (h + 1) * GLA_DV)
            donh, onh = don[:, cols], on[:, cols]
            do_ref[:, cols] = m["rs"][h] * (donh - onh * jnp.mean(donh * onh, axis=-1, keepdims=True))

        @pl.when(i == 0)
        def _():
            dbg_ref[...] = jnp.zeros_like(dbg_ref)
            dgn_ref[...] = jnp.zeros_like(dgn_ref)

        dbg_ref[...] += jnp.sum(dgpre, axis=0, keepdims=True)
        dgn_ref[...] += dgn
        z5b_ref[...] = m["z5b"]
        dgp_ref[...] = dgpb
        ys5b_ref[...] = m["ys5b"]
        dm5b_ref[...] = dm5b
        yglab_ref[...] = m["yglab"]
        dmgb_ref[...] = dmgb
        mergedb_ref[...] = m["mergedb"]
        dh2b_ref[...] = dh2b

    def f32(d):
        return jax.ShapeDtypeStruct((t, d), F32)

    def b16(d):
        return jax.ShapeDtypeStruct((t, d), BF16)

    widths = (512, 512, 512, 1024, 512, 1024, 1024, 1024)
    return _call(
        body, name="mix_post_bwd", grid=(t // tm,),
        in_specs=[_row_tile(tm, 512)] * 3 + [_row_tile(tm, D_MODEL)] * 3
        + [VMEM_FULL, _acc_row(512), _acc_row(512), VMEM_FULL, VMEM_FULL, VMEM_FULL],
        out_specs=[_row_tile(tm, 512)] * 3 + [_row_tile(tm, D_MODEL)] * 2
        + [_acc_row(512)] * 2 + [_row_tile(tm, w) for w in widths],
        out_shape=[f32(512)] * 3 + [f32(D_MODEL)] * 2
        + [jax.ShapeDtypeStruct((1, 512), F32)] * 2
        + [b16(w) for w in widths],
        args=(y, o, r, gs5, ggla, dh2, wg, bg, gn, ps5t, pglat, wout), carry=carry)


def _head(h3, g, target):
    t = h3.shape[0]
    tm = _tile(t)

    def body(h_ref, g_ref, t_ref, loss_ref, dh_ref, dg_ref):
        i = pl.program_id(0)
        gv = g_ref[...]
        xhat, r = _rms_parts(h_ref[...])
        err = xhat * gv - t_ref[...]
        dx, dg = _rms_bwd(err * (1.0 / D_MODEL), gv, xhat, r)
        dh_ref[...] = dx

        @pl.when(i == 0)
        def _():
            loss_ref[...] = jnp.zeros_like(loss_ref)
            dg_ref[...] = jnp.zeros_like(dg_ref)

        loss_ref[...] += (0.5 / D_MODEL) * jnp.sum(jnp.sum(err * err, axis=1, keepdims=True), axis=0, keepdims=True)
        dg_ref[...] += dg

    return pl.pallas_call(
        body, name="head", grid=(t // tm,),
        in_specs=[_row_tile(tm, D_MODEL), _acc_row(D_MODEL), _row_tile(tm, D_MODEL)],
        out_specs=[pl.BlockSpec((1, 1), lambda i: (0, 0)), _row_tile(tm, D_MODEL), _acc_row(D_MODEL)],
        out_shape=[jax.ShapeDtypeStruct((1, 1), F32), jax.ShapeDtypeStruct((t, D_MODEL), F32),
                   jax.ShapeDtypeStruct((1, D_MODEL), F32)],
        compiler_params=_cparams(1),
    )(h3, g, target)


ADAM_TILE_ELEMS = 256 * 1024


def _adamw(w, g, m, v, name):
    rows, cols = w.shape
    tr = rows
    while tr * cols > ADAM_TILE_ELEMS and tr % 16 == 0:
        tr //= 2

    spec = pl.BlockSpec((tr, cols), lambda i: (i, 0))
    sh = jax.ShapeDtypeStruct((rows, cols), F32)
    return pl.pallas_call(functools.partial(_adamw_body), name=name, grid=(rows // tr,), in_specs=[spec] * 4,
                          out_specs=[spec] * 3, out_shape=[sh] * 3, compiler_params=_cparams(1))(w, g, m, v)


def _adamw_body(w_ref, g_ref, m_ref, v_ref, d_ref, nm_ref, nv_ref):
    gv = g_ref[...]
    nm = ADAM_B1 * m_ref[...] + (1.0 - ADAM_B1) * gv
    nv = ADAM_B2 * v_ref[...] + (1.0 - ADAM_B2) * (gv * gv)
    m_hat = nm / (1.0 - ADAM_B1 ** ADAM_STEP)
    v_hat = nv / (1.0 - ADAM_B2 ** ADAM_STEP)
    d_ref[...] = -ADAM_LR * (m_hat / (jnp.sqrt(v_hat) + ADAM_EPS) + ADAM_WD * w_ref[...])
    nm_ref[...] = nm
    nv_ref[...] = nv


def _adamw_many(ws, gs, ms, vs, name):
    n = len(ws)

    def body(*refs):
        ins, outs = refs[:4 * n], refs[4 * n:]
        for i in range(n):
            _adamw_body(*(ins[j * n + i] for j in range(4)), *(outs[j * n + i] for j in range(3)))

    shapes = [jax.ShapeDtypeStruct(w.shape, F32) for w in ws]
    res = pl.pallas_call(body, name=name, in_specs=[VMEM_FULL] * (4 * n), out_specs=[VMEM_FULL] * (3 * n),
                         out_shape=shapes * 3)(*ws, *gs, *ms, *vs)
    return res[:n], res[n:2 * n], res[2 * n:]


def _exchange(carry, name):
    return _call(None, name=name, grid=(1,), in_specs=[], out_specs=[], out_shape=[], args=(), carry=carry)[1]


def _pair_add(slabs, from_pair, name):
    _, r, cols = slabs.shape

    def body(s_ref, p_ref, o_ref):
        c = lax.axis_index("c")
        mine = jnp.where(c == 0, s_ref[0, 0].astype(F32), s_ref[0, 1].astype(F32))
        o_ref[0] = (mine + p_ref[0].astype(F32)).astype(BF16)

    return pl.pallas_call(
        body, name=name, grid=(4,),
        in_specs=[pl.BlockSpec((1, 2, r, cols), lambda i: (i, 0, 0, 0)), pl.BlockSpec((1, r, cols), lambda i: (i, 0, 0))],
        out_specs=pl.BlockSpec((1, r, cols), lambda i: (i, 0, 0)),
        out_shape=jax.ShapeDtypeStruct((4, r, cols), BF16),
        compiler_params=_cparams(1),
    )(slabs.reshape(4, 2, r, cols), from_pair)


def _sum_slabs(slabs, name):
    n = slabs.shape[0]

    def body(s_ref, o_ref):
        acc = s_ref[0].astype(F32)
        for s in range(1, n):
            acc = acc + s_ref[s].astype(F32)
        o_ref[...] = acc

    return pl.pallas_call(
        body, name=name, in_specs=[VMEM_FULL], out_specs=VMEM_FULL,
        out_shape=jax.ShapeDtypeStruct(slabs.shape[1:], F32),
        compiler_params=pltpu.CompilerParams(vmem_limit_bytes=VMEM_LIMIT_BYTES),
    )(slabs)


BIG = ("ffn1_w1", "ffn1_w3", "ffn1_w2", "w_in", "s5_glu_w", "gla_a_up_w", "proj_s5", "proj_gla", "w_out",
       "ffn2_w1", "ffn2_w3", "ffn2_w2")
GROUPS = (("ffn1_w1", "ffn1_w3", "ffn1_w2"),
          ("w_in", "s5_glu_w", "gla_a_up_w", "proj_s5", "proj_gla", "w_out"),
          ("ffn2_w1", "ffn2_w3", "ffn2_w2"))
W_IN_ROWS = 514
W_IN_PAD = 528
UP_COLS = 32
ROW_ADAM = ("ffn1_w1", "ffn1_w3", "w_in", "ffn2_w1", "ffn2_w3")
COL_SHARDED = ("ffn1_w1", "ffn1_w3", "w_in", "proj_s5", "proj_gla", "ffn2_w1", "ffn2_w3")

SMALL = ("ffn1_norm", "mix_norm", "s5_lambda_re", "s5_lambda_im", "s5_log_dt", "s5_b_re", "s5_b_im", "s5_c_re",
         "s5_c_im", "s5_d", "s5_glu_b", "gla_a_up_b", "gla_out_norm", "ffn2_norm", "final_norm")
SMALL_SHAPES = dict(ffn1_norm=(1, 1024), mix_norm=(1, 1024), s5_lambda_re=(1, 32, 64), s5_lambda_im=(1, 32, 64),
                    s5_log_dt=(1, 32), s5_b_re=(1, 32, 64, 16), s5_b_im=(1, 32, 64, 16), s5_c_re=(1, 32, 16, 64),
                    s5_c_im=(1, 32, 16, 64), s5_d=(1, 32, 16), s5_glu_b=(1, 512), gla_a_up_b=(1, 256),
                    gla_out_norm=(1, 512), ffn2_norm=(1, 1024), final_norm=(1024,))
SMALL_N = sum(math.prod(s) for s in SMALL_SHAPES.values())
SMALL_R = -(-SMALL_N // (64 * 1024)) * 64


def _shard_rows(name, a):
    if name == "gla_a_up_w":
        return jnp.pad(a, ((0, 0), (0, 128 - UP_COLS)))
    if name in COL_SHARDED:
        a = a.T
    if name == "w_in":
        return jnp.pad(a, ((0, W_IN_PAD - W_IN_ROWS), (0, 0)))
    return a.reshape(-1, 1024)


def _unshard_rows(name, rows, shape):
    if name == "gla_a_up_w":
        return rows[:, :UP_COLS]
    if name == "w_in":
        rows = rows[:W_IN_ROWS]
    if name in COL_SHARDED:
        return rows.reshape(shape[1], shape[0]).T
    return rows.reshape(shape)


def _pack_small(vals, loss):
    flat = jnp.concatenate([vals[n].reshape(-1).astype(F32) for n in SMALL] + [loss.reshape(1)])
    return jnp.pad(flat, (0, SMALL_R * 1024 - SMALL_N - 1)).reshape(SMALL_R, 1024)


S5_B = ("s5_b_re", "s5_b_im")


def _working(name, a):
    return a[0].transpose(0, 2, 1) if name in S5_B else a


def _declared(name, a):
    return a.transpose(0, 2, 1)[None] if name in S5_B else a.reshape(SMALL_SHAPES[name])


def _unpack_small(slab):
    flat = slab.reshape(-1)
    out, off = {}, 0
    for n in SMALL:
        size = math.prod(SMALL_SHAPES[n])
        shape = (S5_GROUPS, S5_GROUP, S5_STATE) if n in S5_B else SMALL_SHAPES[n]
        out[n] = flat[off:off + size].reshape(shape)
        off += size
    return out


FULL_SHAPES = dict(w_in=(IN_COLS, D_MODEL), s5_glu_w=(S5_WIDTH, S5_WIDTH), gla_a_up_w=(GLA_RANK, GLA_KEY),
                   proj_s5=(D_MODEL, S5_WIDTH), proj_gla=(D_MODEL, GLA_VAL), w_out=(D_MODEL, D_MODEL))


def _full_weight(name, gathered):
    if name == "gla_a_up_w":
        return gathered[:, :, :UP_COLS].transpose(1, 0, 2).reshape(GLA_RANK, GLA_KEY)
    if name == "w_in":
        gathered = gathered[:, :W_IN_ROWS]
    return gathered.reshape(FULL_SHAPES.get(name, (D_FF, D_MODEL)))


def _grad_slabs(name, g):
    if name == "gla_a_up_w":
        g = g.reshape(GLA_RANK, N_DEV, UP_COLS).transpose(1, 0, 2)
        return jnp.pad(g, ((0, 0), (0, 0), (0, 128 - UP_COLS))).astype(BF16)
    if name == "w_in":
        return jnp.pad(g.reshape(N_DEV, W_IN_ROWS, D_MODEL), ((0, 0), (0, W_IN_PAD - W_IN_ROWS), (0, 0)))
    return g.reshape(N_DEV, -1, 1024)


def _s5_dense(re, im, sign_im):
    eye = jnp.eye(8, dtype=F32)

    def one(a):
        a = a.reshape(S5_BLOCKS, 8, S5_GROUP, S5_STATE)
        return jnp.einsum("cghp,gk->cghkp", a, eye).reshape(S5_BLOCKS, 128, S5_BSTATE)

    return jnp.concatenate([one(re), sign_im * one(im)], axis=-1)


def _s5_undense(d):
    eye = jnp.eye(8, dtype=F32)

    def one(a):
        a = a.reshape(S5_BLOCKS, 8, S5_GROUP, 8, S5_STATE)
        return jnp.einsum("cghkp,gk->cghp", a, eye).reshape(S5_GROUPS, S5_GROUP, S5_STATE)

    return one(d[..., :S5_BSTATE]), one(d[..., S5_BSTATE:])


def _local_step(x, target, p, w, rows=None):
    w = dict(w or {})
    landed_grads = {}

    def gather(names):
        return None if rows is None else ([rows[n] for n in names], "gather")

    def gathered(names, landed):
        w.update({n: _full_weight(n, g) for n, g in zip(names, landed)})

    def scatter(names):
        return None if rows is None else ([_grad_slabs(n, big[n]) for n in names], "scatter")

    def scattered(names, landed):
        landed_grads.update(zip(names, landed))

    if rows is not None:
        gathered(GROUPS[0], _exchange(gather(GROUPS[0]), "gather_ffn1"))
    g1, gm, g2 = p["ffn1_norm"], p["mix_norm"], p["ffn2_norm"]
    gf = p["final_norm"].reshape(1, D_MODEL)
    lre, lim = p["s5_lambda_re"][0], p["s5_lambda_im"][0]
    ldt = p["s5_log_dt"][0].reshape(S5_GROUPS, 1)
    bre = p["s5_b_re"][0].transpose(0, 2, 1)
    bim = p["s5_b_im"][0].transpose(0, 2, 1)
    cre, cim = p["s5_c_re"][0], p["s5_c_im"][0]
    dskip = p["s5_d"][0].reshape(1, S5_WIDTH)
    bg, bup, gn = p["s5_glu_b"], p["gla_a_up_b"], p["gla_out_norm"]

    mix_first, mix_rest = ("w_in", "gla_a_up_w"), ("s5_glu_w", "proj_s5", "proj_gla", "w_out")
    h1, (a3_1, b3_1, n1), got = _ffn_fwd(x, g1, w["ffn1_w1"], w["ffn1_w3"], w["ffn1_w2"], "ffn1_fwd",
                                         gather(mix_first))
    gathered(mix_first, got)
    wup = w["gla_a_up_w"].astype(F32)
    (u, s5in, q, k, v, r, alow, gs5, ggla), got = _mix_pre_fwd(h1, gm, w["w_in"], gather(mix_rest))
    gathered(mix_rest, got)
    ar, ai, bbr, bbi = _s5_disc(lre, lim, ldt, bre, bim)
    bd = _s5_dense(bbr, bbi, 1.0)
    cd = _s5_dense(cre, cim, -1.0)
    bd16, cd16 = bd.astype(BF16), cd.astype(BF16)
    bdt16, ctd16 = bd16.transpose(0, 2, 1), cd16.transpose(0, 2, 1)
    ar4 = ar.reshape(S5_BLOCKS, 1, S5_BSTATE)
    ai4 = ai.reshape(S5_BLOCKS, 1, S5_BSTATE)
    (xs, y), got = _s5_fwd(s5in, bd16, ctd16, ar4, ai4, dskip, gather(GROUPS[2][:1]))
    gathered(GROUPS[2][:1], got)
    (o, ssave), got = _gla_fwd(q, k, v, alow, wup, bup, gather(GROUPS[2][1:2]))
    gathered(GROUPS[2][1:2], got)
    post_w = (w["s5_glu_w"], bg, gn, w["proj_s5"], w["proj_gla"], w["w_out"])
    h2, got = _mix_post_fwd(y, o, r, gs5, ggla, h1, *post_w, carry=gather(GROUPS[2][2:]))
    gathered(GROUPS[2][2:], got)
    h3, (a3_2, b3_2, n2), _ = _ffn_fwd(h2, g2, w["ffn2_w1"], w["ffn2_w3"], w["ffn2_w2"], "ffn2_fwd")
    loss, dh3, dgf = _head(h3, gf, target)

    big, small = {}, {}
    small["final_norm"] = dgf.reshape(D_MODEL)
    (dh2, dg2, da3, db3, s3, dhh2), _ = _ffn_bwd(
        h2, dh3, g2, a3_2, b3_2, w["ffn2_w1"], w["ffn2_w3"], w["ffn2_w2"], "ffn2_bwd")
    small["ffn2_norm"] = dg2
    big["ffn2_w1"] = _mm_tn(da3, n2, "ffn2_dw1")
    big["ffn2_w3"] = _mm_tn(db3, n2, "ffn2_dw3")
    big["ffn2_w2"] = _mm_tn(s3, dhh2, "ffn2_dw2")
    (dy, do, dr, dgs5, dggla, dbg, dgn, z5b, dgpb, ys5b, dm5b, yglab, dmgb, mergedb, dh2b), got = _mix_post_bwd(
        y, o, r, gs5, ggla, dh2, *post_w, carry=scatter(GROUPS[2][:1]))
    scattered(GROUPS[2][:1], got)
    small["s5_glu_b"] = dbg
    small["gla_out_norm"] = dgn
    big["s5_glu_w"] = _mm_tn(z5b, dgpb, "glu_dw")
    big["proj_s5"] = _mm_tn(dm5b, ys5b, "proj_s5_dw")
    big["proj_gla"] = _mm_tn(dmgb, yglab, "proj_gla_dw")
    big["w_out"] = _mm_tn(mergedb, dh2b, "w_out_dw")
    (dq, dk, dv, dalow, dwup, dbup), got = _gla_bwd(q, k, v, alow, wup, bup, ssave, do, scatter(GROUPS[2][1:2]))
    scattered(GROUPS[2][1:2], got)
    big["gla_a_up_w"] = dwup
    small["gla_a_up_b"] = dbup
    (ds5in, dbd, dcd, dd, dar4, dai4), got = _s5_bwd(
        dy, s5in, xs, cd16, bdt16, ar4, ai4, dskip, scatter(GROUPS[2][2:]))
    scattered(GROUPS[2][2:], got)
    dbbr, dbbi = _s5_undense(dbd)
    dcre, dcim_neg = _s5_undense(dcd)
    glre, glim, gldt, gbre, gbim = _s5_disc_bwd(
        lre, lim, ldt, bre, bim, dar4.reshape(S5_GROUPS, S5_STATE), dai4.reshape(S5_GROUPS, S5_STATE),
        dbbr, dbbi)
    small["s5_lambda_re"] = glre[None]
    small["s5_lambda_im"] = glim[None]
    small["s5_log_dt"] = gldt.reshape(1, S5_GROUPS)
    small["s5_b_re"] = gbre
    small["s5_b_im"] = gbim
    small["s5_c_re"] = dcre[None]
    small["s5_c_im"] = -dcim_neg[None]
    small["s5_d"] = dd.reshape(1, S5_GROUPS, S5_GROUP)
    dz = (ds5in, dq, dk, dv, dr, dalow, dgs5, dggla)
    (dh1, dgm), got = _mix_pre_bwd(h1, gm, w["w_in"], dh2, dz, scatter(mix_rest))
    scattered(mix_rest, got)
    small["mix_norm"] = dgm
    big["w_in"] = jnp.concatenate([_mm_tn(d, u, "w_in_dw%d" % i) for i, d in enumerate(dz)], axis=0)
    (dx, dg1, da3, db3, s3, dhh1), got = _ffn_bwd(
        x, dh1, g1, a3_1, b3_1, w["ffn1_w1"], w["ffn1_w3"], w["ffn1_w2"], "ffn1_bwd", scatter(mix_first))
    scattered(mix_first, got)
    small["ffn1_norm"] = dg1
    if rows is None:
        big["ffn1_w1"] = _mm_tn(da3, n1, "ffn1_dw1")
        big["ffn1_w3"] = _mm_tn(db3, n1, "ffn1_dw3")
        big["ffn1_w2"] = _mm_tn(s3, dhh1, "ffn1_dw2")
        return loss[0, 0], dx, big, small
    part = _pack_small(small, loss).reshape(N_DEV, SMALL_R // N_DEV, 1024)
    big["ffn1_w1"], (small_landed,) = _mm_tn(da3, n1, "ffn1_dw1", ([part], "scatter"))
    small_mine = _sum_slabs(small_landed, "sum_small")
    slabs1 = _grad_slabs("ffn1_w1", big["ffn1_w1"])
    big["ffn1_w3"], (from_pair, small_all) = _mm_tn(db3, n1, "ffn1_dw3",
                                                    [([slabs1], "pair"), ([small_mine], "gather")])
    small = small_all.reshape(SMALL_R, 1024)
    sums1 = _pair_add(slabs1, from_pair, "ffn1_w1_pair")
    slabs3 = _grad_slabs("ffn1_w3", big["ffn1_w3"])
    big["ffn1_w2"], (landed1, from_pair) = _mm_tn(s3, dhh1, "ffn1_dw2", [([sums1], "chips"), ([slabs3], "pair")])
    sums3 = _pair_add(slabs3, from_pair, "ffn1_w3_pair")
    slabs2 = _grad_slabs("ffn1_w2", big["ffn1_w2"])
    landed3, from_pair = _exchange([([sums3], "chips"), ([slabs2], "pair")], "scatter_ffn1_a")
    sums2 = _pair_add(slabs2, from_pair, "ffn1_w2_pair")
    (landed2,) = _exchange(([sums2], "chips"), "scatter_ffn1_b")
    scattered(GROUPS[0], (landed1, landed3, landed2))
    return loss[0, 0], dx, landed_grads, small


NAMES = ("ffn1_norm", "ffn1_w1", "ffn1_w3", "ffn1_w2", "mix_norm", "w_in", "s5_lambda_re", "s5_lambda_im",
         "s5_log_dt", "s5_b_re", "s5_b_im", "s5_c_re", "s5_c_im", "s5_d", "s5_glu_w", "s5_glu_b", "gla_a_up_w",
         "gla_a_up_b", "gla_out_norm", "proj_s5", "proj_gla", "w_out", "ffn2_norm", "ffn2_w1", "ffn2_w3", "ffn2_w2",
         "final_norm")


def kernel(*args):
    nw = len(NAMES)
    x = args[0][0]
    wts = dict(zip(NAMES, args[1:1 + nw]))
    target = args[1 + nw][0]
    mom = dict(zip(NAMES, args[2 + nw:2 + 2 * nw]))
    var = dict(zip(NAMES, args[2 + 2 * nw:2 + 3 * nw]))

    shards = {n: wts[n][0] for n in BIG}
    rows = {n: _shard_rows(n, shards[n]).astype(BF16) for n in BIG}
    _, dx, landed, small_slab = _local_step(x, target, {n: wts[n] for n in SMALL}, None, rows)
    loss = small_slab.reshape(-1)[SMALL_N]
    g_small = _unpack_small(small_slab)

    grad, delta, new_m, new_v = {}, {}, {}, {}
    for n in BIG:
        g_rows = _sum_slabs(landed[n], "sum_" + n)
        if n in ROW_ADAM:
            g = g_rows[:W_IN_ROWS] if n == "w_in" else g_rows
            outs = _adamw(shards[n].T, g, mom[n][0].T, var[n][0].T, "adamw_" + n)
            grad[n], delta[n], new_m[n], new_v[n] = (a.T[None] for a in (g, *outs))
        else:
            g = _unshard_rows(n, g_rows, shards[n].shape)
            outs = _adamw(shards[n], g, mom[n][0], var[n][0], "adamw_" + n)
            grad[n], delta[n], new_m[n], new_v[n] = (a[None] for a in (g, *outs))

    def flat2d(a):
        return a.reshape(-1, a.shape[-1])

    operands = ([flat2d(_working(n, d[n])) for n in SMALL] for d in (wts, mom, var))
    w2d, m2d, v2d = operands
    outs = _adamw_many(w2d, [flat2d(g_small[n]) for n in SMALL], m2d, v2d, "adamw_small")
    for out, arrays in zip((grad, delta, new_m, new_v), ([g_small[n] for n in SMALL], *outs)):
        out.update({n: _declared(n, a.reshape(g_small[n].shape)) for n, a in zip(SMALL, arrays)})
    return (loss, dx[None], *(d[n] for d in (grad, delta, new_m, new_v) for n in NAMES))
```

```python
import functools
import math

import jax
import jax.numpy as jnp
from jax import lax
from jax.experimental import pallas as pl
from jax.experimental.pallas import tpu as pltpu

F32, BF16 = jnp.float32, jnp.bfloat16
HIGHEST = lax.Precision.HIGHEST

D_MODEL = 1024
D_FF = 2816
N_DEV = 8
S5_WIDTH, S5_GROUPS, S5_GROUP, S5_STATE = 512, 32, 16, 64
S5_BLOCKS = 4
S5_BSTATE = 512
S5_SEGS = 8
GLA_HEADS, GLA_DK, GLA_DV = 4, 64, 128
GLA_KEY, GLA_VAL, GLA_RANK, GLA_CHUNK = 256, 512, 16, 64
GLA_TAU = 16.0
GLA_STEP_CHUNKS = 4
EPS = 1e-6
IN_SIZES = (512, 256, 256, 512, 512, 16, 1024, 1024)
IN_OFFS = tuple(sum(IN_SIZES[:i]) for i in range(len(IN_SIZES)))
IN_COLS = sum(IN_SIZES)
ADAM_LR, ADAM_B1, ADAM_B2, ADAM_EPS, ADAM_WD, ADAM_STEP = 0.001, 0.9, 0.999, 1e-08, 0.01, 10
GELU_C0 = math.sqrt(2.0 / math.pi)
GELU_C1 = 0.044715

FFN_FT = 256
VMEM_LIMIT_BYTES = 56 * 1024 * 1024

VMEM_FULL = pl.BlockSpec(memory_space=pltpu.VMEM)
ANY = pl.BlockSpec(memory_space=pl.ANY)


def _cparams(n_grid):
    return pltpu.CompilerParams(dimension_semantics=("arbitrary",) * n_grid, vmem_limit_bytes=VMEM_LIMIT_BYTES)


def _tile(t):
    return 512 if t >= 1024 else t // 2


def _nn(a, b):
    return jnp.dot(a, b, preferred_element_type=F32)


def _nt(a, b):
    return lax.dot_general(a, b, (((1,), (1,)), ((), ())), preferred_element_type=F32)


def _tn(a, b):
    return lax.dot_general(a, b, (((0,), (0,)), ((), ())), preferred_element_type=F32)


def _rms_parts(x):
    r = lax.rsqrt(jnp.mean(x * x, axis=-1, keepdims=True) + EPS)
    return x * r, r


def _rms_bwd(dn, g, xhat, r):
    dxh = dn * g
    dx = r * (dxh - xhat * jnp.mean(dxh * xhat, axis=-1, keepdims=True))
    return dx, jnp.sum(dn * xhat, axis=0, keepdims=True)


def _peers():
    x, y, c = lax.axis_index("x"), lax.axis_index("y"), lax.axis_index("c")
    out = []
    for k in range(1, N_DEV):
        px = 1 - x if k & 4 else x
        py = 1 - y if k & 2 else y
        pc = 1 - c if k & 1 else c
        out.append(((px, py, pc), 4 * px + 2 * py + pc))
    return 4 * x + 2 * y + c, out


def _exchange_copies(src_refs, out_refs, send_sems, recv_sems, local_sems, scatter, with_recvs):
    me, peers = _peers()
    locals_, sends, recvs = [], [], []
    for a, (src_ref, out_ref) in enumerate(zip(src_refs, out_refs)):
        def mine(idx, src_ref=src_ref):
            return src_ref.at[idx] if scatter else src_ref

        locals_.append(pltpu.make_async_copy(mine(me), out_ref.at[me], local_sems.at[a]))
        for k, (dev, idx) in enumerate(peers):
            sends.append(pltpu.make_async_remote_copy(
                src_ref=mine(idx), dst_ref=out_ref.at[me], send_sem=send_sems.at[a, k], recv_sem=recv_sems.at[a, k],
                device_id=dev, device_id_type=pl.DeviceIdType.MESH))
            if with_recvs:
                recvs.append(pltpu.make_async_remote_copy(
                    src_ref=mine(idx), dst_ref=out_ref.at[idx], send_sem=send_sems.at[a, k],
                    recv_sem=recv_sems.at[a, k], device_id=dev, device_id_type=pl.DeviceIdType.MESH))
    return locals_, sends, recvs


def _remote(src, dst, send_sems, recv_sems, a, k, dev):
    return pltpu.make_async_remote_copy(src_ref=src, dst_ref=dst, send_sem=send_sems.at[a, k],
                                        recv_sem=recv_sems.at[a, k], device_id=dev,
                                        device_id_type=pl.DeviceIdType.MESH)


def _gather_places():
    x, y, c = lax.axis_index("x"), lax.axis_index("y"), lax.axis_index("c")
    chips = [(1 - x, y), (x, 1 - y), (1 - x, 1 - y)]
    sibling = (x, y, 1 - c)
    me_idx, sib_idx = 4 * x + 2 * y + c, 4 * x + 2 * y + 1 - c
    same_core = [((cx, cy, c), 4 * cx + 2 * cy + c) for cx, cy in chips]
    other_core_idx = [4 * cx + 2 * cy + 1 - c for cx, cy in chips]
    return sibling, me_idx, sib_idx, same_core, other_core_idx


def _gather_start(src_refs, out_refs, send_sems, recv_sems, local_sems):
    sibling, me_idx, _, same_core, _ = _gather_places()
    for a, (src, out) in enumerate(zip(src_refs, out_refs)):
        pltpu.make_async_copy(src, out.at[me_idx], local_sems.at[a]).start()
        _remote(src, out.at[me_idx], send_sems, recv_sems, a, 0, sibling).start()
        for j, (dev, _) in enumerate(same_core):
            _remote(src, out.at[me_idx], send_sems, recv_sems, a, 1 + j, dev).start()


def _gather_forward(src_refs, out_refs, send_sems, recv_sems, local_sems):
    sibling, _, _, same_core, _ = _gather_places()
    for a, (src, out) in enumerate(zip(src_refs, out_refs)):
        for j, (dev, idx) in enumerate(same_core):
            _remote(src, out.at[idx], send_sems, recv_sems, a, 1 + j, dev).wait_recv()
            _remote(out.at[idx], out.at[idx], send_sems, recv_sems, a, 4 + j, sibling).start()


def _gather_finish(src_refs, out_refs, send_sems, recv_sems, local_sems):
    sibling, me_idx, sib_idx, same_core, other_core_idx = _gather_places()
    for a, (src, out) in enumerate(zip(src_refs, out_refs)):
        _remote(src, out.at[sib_idx], send_sems, recv_sems, a, 0, sibling).wait_recv()
        for j, idx in enumerate(other_core_idx):
            _remote(src, out.at[idx], send_sems, recv_sems, a, 4 + j, sibling).wait_recv()
        _remote(src, out.at[me_idx], send_sems, recv_sems, a, 0, sibling).wait_send()
        for j, (dev, idx) in enumerate(same_core):
            _remote(src, out.at[me_idx], send_sems, recv_sems, a, 1 + j, dev).wait_send()
            _remote(out.at[idx], out.at[idx], send_sems, recv_sems, a, 4 + j, sibling).wait_send()
        pltpu.make_async_copy(src, out.at[me_idx], local_sems.at[a]).wait()


def _exchange_start(*refs, scatter):
    locals_, sends, _ = _exchange_copies(*refs, scatter=scatter, with_recvs=False)
    for cp in locals_ + sends:
        cp.start()


def _exchange_wait(*refs, scatter):
    locals_, sends, recvs = _exchange_copies(*refs, scatter=scatter, with_recvs=True)
    for cp in recvs:
        cp.wait_recv()
    for cp in sends:
        cp.wait_send()
    for cp in locals_:
        cp.wait()


def _halves_places():
    x, y, c = lax.axis_index("x"), lax.axis_index("y"), lax.axis_index("c")
    flips = [(1 - x, y), (x, 1 - y), (1 - x, 1 - y)]
    return (x, y, 1 - c), c, 2 * x + y, [((fx, fy, c), 2 * fx + fy) for fx, fy in flips]


def _pair_start(src_refs, out_refs, send_sems, recv_sems, local_sems):
    sibling, c, _, _ = _halves_places()
    for a, (src, out) in enumerate(zip(src_refs, out_refs)):
        for i in range(4):
            _remote(src.at[2 * i + 1 - c], out.at[i], send_sems, recv_sems, a, i, sibling).start()


def _pair_finish(src_refs, out_refs, send_sems, recv_sems, local_sems):
    sibling, c, _, _ = _halves_places()
    for a, (src, out) in enumerate(zip(src_refs, out_refs)):
        for i in range(4):
            _remote(src.at[2 * i + 1 - c], out.at[i], send_sems, recv_sems, a, i, sibling).wait()


def _chips_start(src_refs, out_refs, send_sems, recv_sems, local_sems):
    _, _, chip, others = _halves_places()
    for a, (src, out) in enumerate(zip(src_refs, out_refs)):
        pltpu.make_async_copy(src.at[chip], out.at[chip], local_sems.at[a]).start()
        for k, (dev, their_chip) in enumerate(others):
            _remote(src.at[their_chip], out.at[chip], send_sems, recv_sems, a, k, dev).start()


def _chips_finish(src_refs, out_refs, send_sems, recv_sems, local_sems):
    _, _, chip, others = _halves_places()
    for a, (src, out) in enumerate(zip(src_refs, out_refs)):
        for k, (dev, their_chip) in enumerate(others):
            _remote(src.at[their_chip], out.at[their_chip], send_sems, recv_sems, a, k, dev).wait_recv()
        for k, (dev, their_chip) in enumerate(others):
            _remote(src.at[their_chip], out.at[chip], send_sems, recv_sems, a, k, dev).wait_send()
        pltpu.make_async_copy(src.at[chip], out.at[chip], local_sems.at[a]).wait()


EXCHANGES = {
    "gather": (_gather_start, _gather_forward, _gather_finish, N_DEV, False),
    "scatter": (functools.partial(_exchange_start, scatter=True), None, functools.partial(_exchange_wait, scatter=True),
                N_DEV, True),
    "pair": (_pair_start, None, _pair_finish, 4, True),
    "chips": (_chips_start, None, _chips_finish, 4, True),
}


def _exchange_sems(n_arrays):
    return [pltpu.SemaphoreType.DMA((n_arrays, N_DEV - 1)), pltpu.SemaphoreType.DMA((n_arrays, N_DEV - 1)),
            pltpu.SemaphoreType.DMA((n_arrays,))]


def _exchange_shapes(srcs, kind):
    lead, slabbed = EXCHANGES[kind][3:]
    return [jax.ShapeDtypeStruct((lead,) + tuple(s.shape[1:] if slabbed else s.shape), s.dtype) for s in srcs]


def _carries(carry):
    if carry is None:
        return []
    return [carry] if isinstance(carry, tuple) else list(carry)


def _call(body, *, name, grid, in_specs, out_specs, out_shape, args, scratch_shapes=(), carry=None):
    n_in, n_out, n_scr = len(in_specs), len(out_specs), len(scratch_shapes)
    groups = _carries(carry)
    sizes = [len(arrays) for arrays, _ in groups]
    nc = sum(sizes)

    def wrapped(*refs):
        ins, refs = refs[:n_in], refs[n_in:]
        csrc, refs = refs[:nc], refs[nc:]
        outs, refs = refs[:n_out], refs[n_out:]
        cland, refs = refs[:nc], refs[nc:]
        scr, sems = refs[:n_scr], refs[n_scr:]

        def run(phase):
            at = 0
            for gi, ((_, kind), size) in enumerate(zip(groups, sizes)):
                if EXCHANGES[kind][phase] is not None:
                    EXCHANGES[kind][phase](csrc[at:at + size], cland[at:at + size], *sems[3 * gi:3 * gi + 3])
                at += size

        last = pl.program_id(0) == grid[0] - 1
        if nc:
            pl.when(pl.program_id(0) == 0)(functools.partial(run, 0))
            pl.when(last)(functools.partial(run, 1))
        if body is not None:
            body(*ins, *outs, *scr)
        if nc:
            pl.when(last)(functools.partial(run, 2))

    res = pl.pallas_call(
        wrapped, name=name, grid=grid,
        in_specs=list(in_specs) + [ANY] * nc, out_specs=list(out_specs) + [ANY] * nc,
        out_shape=list(out_shape) + [s for arrays, kind in groups for s in _exchange_shapes(arrays, kind)],
        scratch_shapes=list(scratch_shapes) + [s for size in sizes for s in _exchange_sems(size)],
        compiler_params=_cparams(1),
    )(*args, *[a for arrays, _ in groups for a in arrays])
    return res[:n_out], res[n_out:]


def _row_tile(tm, d):
    return pl.BlockSpec((tm, d), lambda i: (i, 0))


def _acc_row(d):
    return pl.BlockSpec((1, d), lambda i: (0, 0))


def _ffn_body(x_ref, g_ref, w1_ref, w3_ref, w2_ref, acc_ref, a_ref, b_ref, n_ref):
    xv = x_ref[...]
    xhat, _ = _rms_parts(xv)
    n = (xhat * g_ref[...]).astype(BF16)
    n_ref[...] = n
    acc_ref[...] = xv

    def fstep(f, c):
        rows = pl.ds(pl.multiple_of(f * FFN_FT, FFN_FT), FFN_FT)
        a = _nt(n, w1_ref[rows, :])
        b = _nt(n, w3_ref[rows, :])
        a_ref[f] = a.astype(BF16)
        b_ref[f] = b.astype(BF16)
        s = (a * jax.nn.sigmoid(a) * b).astype(BF16)
        acc_ref[...] += 0.5 * _nn(s, w2_ref[rows, :])
        return c

    lax.fori_loop(0, D_FF // FFN_FT, fstep, 0, unroll=True)


def _ffn_fwd(x, g, w1t, w3t, w2, name, carry=None):
    t = x.shape[0]
    tm = _tile(t)
    nf = D_FF // FFN_FT
    blk3 = pl.BlockSpec((nf, tm, FFN_FT), lambda i: (0, i, 0))
    sh3 = jax.ShapeDtypeStruct((nf, t, FFN_FT), BF16)
    (h, a3, b3, n), landed = _call(
        functools.partial(_ffn_body), name=name, grid=(t // tm,),
        in_specs=[_row_tile(tm, D_MODEL), _acc_row(D_MODEL), VMEM_FULL, VMEM_FULL, VMEM_FULL],
        out_specs=[_row_tile(tm, D_MODEL), blk3, blk3, _row_tile(tm, D_MODEL)],
        out_shape=[jax.ShapeDtypeStruct((t, D_MODEL), F32), sh3, sh3, jax.ShapeDtypeStruct((t, D_MODEL), BF16)],
        args=(x, g, w1t, w3t, w2), carry=carry)
    return h, (a3, b3, n), landed


def _ffn_fwd_head(x, g, w1t, w3t, w2, gf, target, name):
    t = x.shape[0]
    tm = _tile(t)
    nf = D_FF // FFN_FT

    def body(x_ref, g_ref, w1_ref, w3_ref, w2_ref, gf_ref, t_ref, loss_ref, dh_ref, dgf_ref, a_ref, b_ref, n_ref, acc):
        _ffn_body(x_ref, g_ref, w1_ref, w3_ref, w2_ref, acc, a_ref, b_ref, n_ref)
        _head_math(acc[...], gf_ref[...], t_ref[...], loss_ref, dh_ref, dgf_ref)

    blk3 = pl.BlockSpec((nf, tm, FFN_FT), lambda i: (0, i, 0))
    sh3 = jax.ShapeDtypeStruct((nf, t, FFN_FT), BF16)
    (loss, dh, dgf, a3, b3, n), _ = _call(
        body, name=name, grid=(t // tm,),
        in_specs=[_row_tile(tm, D_MODEL), _acc_row(D_MODEL), VMEM_FULL, VMEM_FULL, VMEM_FULL, _acc_row(D_MODEL),
                  _row_tile(tm, D_MODEL)],
        out_specs=[pl.BlockSpec((1, 1), lambda i: (0, 0)), _row_tile(tm, D_MODEL), _acc_row(D_MODEL), blk3, blk3,
                   _row_tile(tm, D_MODEL)],
        out_shape=[jax.ShapeDtypeStruct((1, 1), F32), jax.ShapeDtypeStruct((t, D_MODEL), F32),
                   jax.ShapeDtypeStruct((1, D_MODEL), F32), sh3, sh3, jax.ShapeDtypeStruct((t, D_MODEL), BF16)],
        scratch_shapes=[pltpu.VMEM((tm, D_MODEL), F32)],
        args=(x, g, w1t, w3t, w2, gf, target))
    return loss, dh, dgf, (a3, b3, n)


def _head_math(h, gv, target, loss_ref, dh_ref, dg_ref):
    i = pl.program_id(0)
    xhat, r = _rms_parts(h)
    err = xhat * gv - target
    dx, dg = _rms_bwd(err * (1.0 / D_MODEL), gv, xhat, r)
    dh_ref[...] = dx

    @pl.when(i == 0)
    def _():
        loss_ref[...] = jnp.zeros_like(loss_ref)
        dg_ref[...] = jnp.zeros_like(dg_ref)

    loss_ref[...] += (0.5 / D_MODEL) * jnp.sum(jnp.sum(err * err, axis=1, keepdims=True), axis=0, keepdims=True)
    dg_ref[...] += dg


def _ffn_bwd(x, dh, g, a3, b3, w1t, w3t, w2, name, carry=None):
    t = x.shape[0]
    tm = _tile(t) // 2
    nf = D_FF // FFN_FT

    def body(x_ref, dh_ref, g_ref, a_ref, b_ref, w1_ref, w3_ref, w2_ref,
             dx_ref, dg_ref, da_ref, db_ref, s_ref, dhh_ref, dn_acc):
        i = pl.program_id(0)
        xv = x_ref[...]
        gv = g_ref[...]
        xhat, r = _rms_parts(xv)
        dhv = dh_ref[...]
        dhh = (0.5 * dhv).astype(BF16)
        dhh_ref[...] = dhh
        dn_acc[...] = jnp.zeros_like(dn_acc)

        def fstep(f, c):
            rows = pl.ds(pl.multiple_of(f * FFN_FT, FFN_FT), FFN_FT)
            w1c, w3c, w2c = w1_ref[rows, :], w3_ref[rows, :], w2_ref[rows, :]
            a = a_ref[f].astype(F32)
            b = b_ref[f].astype(F32)
            sg = jax.nn.sigmoid(a)
            sl = a * sg
            ds = _nt(dhh, w2c)
            da = (ds * b * sg * (1.0 + a * (1.0 - sg))).astype(BF16)
            db = (ds * sl).astype(BF16)
            s_ref[f] = (sl * b).astype(BF16)
            da_ref[f] = da
            db_ref[f] = db
            dn_acc[...] += _nn(da, w1c) + _nn(db, w3c)
            return c

        lax.fori_loop(0, nf, fstep, 0, unroll=True)
        dx, dg = _rms_bwd(dn_acc[...], gv, xhat, r)
        dx_ref[...] = dhv + dx

        @pl.when(i == 0)
        def _():
            dg_ref[...] = jnp.zeros_like(dg_ref)

        dg_ref[...] += dg

    blk3 = pl.BlockSpec((nf, tm, FFN_FT), lambda i: (0, i, 0))
    sh3 = jax.ShapeDtypeStruct((nf, t, FFN_FT), BF16)
    return _call(
        body, name=name, grid=(t // tm,),
        in_specs=[_row_tile(tm, D_MODEL), _row_tile(tm, D_MODEL), _acc_row(D_MODEL), blk3, blk3,
                  VMEM_FULL, VMEM_FULL, VMEM_FULL],
        out_specs=[_row_tile(tm, D_MODEL), _acc_row(D_MODEL), blk3, blk3, blk3, _row_tile(tm, D_MODEL)],
        out_shape=[jax.ShapeDtypeStruct((t, D_MODEL), F32), jax.ShapeDtypeStruct((1, D_MODEL), F32), sh3, sh3, sh3,
                   jax.ShapeDtypeStruct((t, D_MODEL), BF16)],
        scratch_shapes=[pltpu.VMEM((tm, D_MODEL), F32)],
        args=(x, dh, g, a3, b3, w1t, w3t, w2), carry=carry)


def _mm_tn(a, b, name, carry=None):
    t, n = b.shape
    kc = min(512, t)
    if a.ndim == 3:
        nb, _, tb = a.shape
        a_spec = pl.BlockSpec((1, t, tb), lambda i: (i, 0, 0))
    else:
        m = a.shape[1]
        tb = min(m, 256)
        nb = m // tb
        a_spec = pl.BlockSpec((t, tb), lambda i: (0, i))
    three_d = a.ndim == 3

    def body(a_ref, b_ref, o_ref, acc):
        acc[...] = jnp.zeros_like(acc)

        def kstep(k, c):
            rows = pl.ds(pl.multiple_of(k * kc, kc), kc)
            av = a_ref[0, rows, :] if three_d else a_ref[rows, :]
            acc[...] += _tn(av.astype(BF16), b_ref[rows, :])
            return c

        lax.fori_loop(0, t // kc, kstep, 0, unroll=True)
        o_ref[...] = acc[...].astype(BF16)

    (out,), landed = _call(
        body, name=name, grid=(nb,),
        in_specs=[a_spec, VMEM_FULL],
        out_specs=[pl.BlockSpec((tb, n), lambda i: (i, 0))],
        out_shape=[jax.ShapeDtypeStruct((nb * tb, n), BF16)],
        scratch_shapes=[pltpu.VMEM((tb, n), F32)],
        args=(a, b), carry=carry)
    return (out, landed) if carry is not None else out


def _mix_pre_fwd(h, g, wint, carry=None):
    t = h.shape[0]
    tm = _tile(t)

    def body(h_ref, g_ref, w_ref, u_ref, *outs):
        xhat, _ = _rms_parts(h_ref[...])
        u = (xhat * g_ref[...]).astype(BF16)
        u_ref[...] = u
        for o_ref, off, size in zip(outs, IN_OFFS, IN_SIZES):
            o_ref[...] = _nt(u, w_ref[off:off + size, :])

    return _call(
        body, name="mix_pre_fwd", grid=(t // tm,),
        in_specs=[_row_tile(tm, D_MODEL), _acc_row(D_MODEL), VMEM_FULL],
        out_specs=[_row_tile(tm, D_MODEL)] + [_row_tile(tm, s) for s in IN_SIZES],
        out_shape=[jax.ShapeDtypeStruct((t, D_MODEL), BF16)] + [jax.ShapeDtypeStruct((t, s), F32) for s in IN_SIZES],
        args=(h, g, wint), carry=carry)


def _mix_pre_bwd(h, g, wint, dh2, dz, carry=None):
    t = h.shape[0]
    tm = _tile(t)

    def body(h_ref, g_ref, w_ref, dh2_ref, *rest):
        dz_refs, (dh1_ref, dg_ref) = rest[:len(IN_SIZES)], rest[len(IN_SIZES):]
        i = pl.program_id(0)
        gv = g_ref[...]
        xhat, r = _rms_parts(h_ref[...])
        du = jnp.zeros((tm, D_MODEL), F32)
        for dz_ref, off, size in zip(dz_refs, IN_OFFS, IN_SIZES):
            du = du + _nn(dz_ref[...].astype(BF16), w_ref[off:off + size, :])
        dx, dg = _rms_bwd(du, gv, xhat, r)
        dh1_ref[...] = dh2_ref[...] + dx

        @pl.when(i == 0)
        def _():
            dg_ref[...] = jnp.zeros_like(dg_ref)

        dg_ref[...] += dg

    return _call(
        body, name="mix_pre_bwd", grid=(t // tm,),
        in_specs=[_row_tile(tm, D_MODEL), _acc_row(D_MODEL), VMEM_FULL, _row_tile(tm, D_MODEL)]
        + [_row_tile(tm, s) for s in IN_SIZES],
        out_specs=[_row_tile(tm, D_MODEL), _acc_row(D_MODEL)],
        out_shape=[jax.ShapeDtypeStruct((t, D_MODEL), F32), jax.ShapeDtypeStruct((1, D_MODEL), F32)],
        args=(h, g, wint, dh2, *dz), carry=carry)


def _disc_math(lre, lim, ldt, bre, bim):
    dt = jnp.exp(ldt)
    mag = jnp.exp(lre * dt)
    ar = mag * jnp.cos(lim * dt)
    ai = mag * jnp.sin(lim * dt)
    den = lre * lre + lim * lim
    nr = ar - 1.0
    fr = (nr * lre + ai * lim) / den
    fi = (ai * lre - nr * lim) / den
    fr, fi = fr[:, None, :], fi[:, None, :]
    return ar, ai, fr * bre - fi * bim, fr * bim + fi * bre


def _s5_disc(lre, lim, ldt, bre, bim):
    def body(lre_ref, lim_ref, ldt_ref, bre_ref, bim_ref, ar_ref, ai_ref, bbr_ref, bbi_ref):
        ar, ai, bbr, bbi = _disc_math(lre_ref[...], lim_ref[...], ldt_ref[...], bre_ref[...], bim_ref[...])
        ar_ref[...] = ar
        ai_ref[...] = ai
        bbr_ref[...] = bbr
        bbi_ref[...] = bbi

    small = jax.ShapeDtypeStruct(lre.shape, F32)
    big = jax.ShapeDtypeStruct(bre.shape, F32)
    return pl.pallas_call(body, name="s5_disc", out_shape=[small, small, big, big],
                          in_specs=[VMEM_FULL] * 5, out_specs=[VMEM_FULL] * 4)(lre, lim, ldt, bre, bim)


def _s5_disc_bwd(lre, lim, ldt, bre, bim, dar, dai, dbbr, dbbi):
    def body(lre_ref, lim_ref, ldt_ref, bre_ref, bim_ref, dar_ref, dai_ref, dbbr_ref, dbbi_ref,
             glre_ref, glim_ref, gldt_ref, gbre_ref, gbim_ref):
        _, vjp = jax.vjp(_disc_math, lre_ref[...], lim_ref[...], ldt_ref[...], bre_ref[...], bim_ref[...])
        glre, glim, gldt, gbre, gbim = vjp((dar_ref[...], dai_ref[...], dbbr_ref[...], dbbi_ref[...]))
        glre_ref[...] = glre
        glim_ref[...] = glim
        gldt_ref[...] = gldt
        gbre_ref[...] = gbre
        gbim_ref[...] = gbim

    small = jax.ShapeDtypeStruct(lre.shape, F32)
    big = jax.ShapeDtypeStruct(bre.shape, F32)
    return pl.pallas_call(body, name="s5_disc_bwd",
                          out_shape=[small, small, jax.ShapeDtypeStruct(ldt.shape, F32), big, big],
                          in_specs=[VMEM_FULL] * 9, out_specs=[VMEM_FULL] * 5,
                          )(lre, lim, ldt, bre, bim, dar, dai, dbbr, dbbi)


def _cmul(ar, ai, br, bi):
    return ar * br - ai * bi, ar * bi + ai * br


def _cpow(ar, ai, n):
    rr, ri = None, None
    pr, pi = ar, ai
    while n:
        if n & 1:
            rr, ri = (pr, pi) if rr is None else _cmul(rr, ri, pr, pi)
        n >>= 1
        if n:
            pr, pi = _cmul(pr, pi, pr, pi)
    return rr, ri


def _shift_rows(v, down):
    row = lax.broadcasted_iota(jnp.int32, v.shape, 0)
    if down:
        return jnp.where(row == 0, 0.0, pltpu.roll(v, 1, 0))
    return jnp.where(row == S5_SEGS - 1, 0.0, pltpu.roll(v, S5_SEGS - 1, 0))


def _chain_segments(er, ei, pr, pi, down):
    fr, fi = er, ei
    for _ in range(S5_SEGS - 1):
        sr, si = _shift_rows(fr, down), _shift_rows(fi, down)
        mr, mi = _cmul(pr, pi, sr, si)
        fr, fi = er + mr, ei + mi
    return _shift_rows(fr, down), _shift_rows(fi, down)


def _rows_to_scan_order(src_ref, dst_ref, t):
    ls = t // S5_SEGS

    def tile(j, c):
        dst_ref[pl.ds(pl.multiple_of(j * S5_SEGS, S5_SEGS), S5_SEGS), :] = src_ref[pl.ds(j, S5_SEGS, stride=ls), :]
        return c

    lax.fori_loop(0, ls, tile, 0, unroll=8)


def _rows_from_scan_order(src_ref, dst_ref, t):
    ls = t // S5_SEGS
    for s in range(S5_SEGS):
        def tile(jb, c, s=s):
            dst_ref[pl.ds(pl.multiple_of(s * ls + jb * 8, 8), 8), :] = (
                src_ref[pl.ds(jb * 8 * S5_SEGS + s, 8, stride=S5_SEGS), :])
            return c

        lax.fori_loop(0, ls // 8, tile, 0, unroll=8)


def _s5_fwd(ug, bd, ctd, ar4, ai4, dskip, carry=None):
    t = ug.shape[0]
    ls = t // S5_SEGS
    rc = min(512, t)
    ns = S5_BSTATE

    def body(ugn_ref, bd_ref, ct_ref, ar_ref, ai_ref, d_ref, xs_hbm, yn_ref, buf, ug_ref, y_ref, sem):
        cb = pl.program_id(0)
        bdv = bd_ref[0]
        _rows_to_scan_order(ugn_ref, ug_ref, t)

        def mm(i, c):
            rows = pl.ds(pl.multiple_of(i * rc, rc), rc)
            buf[rows, :] = _nn(ug_ref[rows, :].astype(BF16), bdv)
            return c

        lax.fori_loop(0, t // rc, mm, 0, unroll=True)
        arb = jnp.broadcast_to(ar_ref[0], (S5_SEGS, ns))
        aib = jnp.broadcast_to(ai_ref[0], (S5_SEGS, ns))

        def step(j, c, store):
            sr, si = c
            rows = pl.ds(pl.multiple_of(j * S5_SEGS, S5_SEGS), S5_SEGS)
            nr = arb * sr - aib * si + buf[rows, 0:ns]
            ni = arb * si + aib * sr + buf[rows, ns:2 * ns]
            if store:
                buf[rows, 0:ns] = nr
                buf[rows, ns:2 * ns] = ni
            return nr, ni

        zero = jnp.zeros((S5_SEGS, ns), F32)
        er, ei = lax.fori_loop(0, ls, functools.partial(step, store=False), (zero, zero))
        pr, pi = _cpow(arb, aib, ls)
        init = _chain_segments(er, ei, pr, pi, down=True)
        lax.fori_loop(0, ls, functools.partial(step, store=True), init)

        out = pltpu.make_async_copy(buf, xs_hbm.at[cb], sem)
        out.start()
        ctv = ct_ref[0]
        dv = d_ref[...]

        def ymm(i, c):
            rows = pl.ds(pl.multiple_of(i * rc, rc), rc)
            y_ref[rows, :] = _nn(buf[rows, :].astype(BF16), ctv) + dv * ug_ref[rows, :]
            return c

        lax.fori_loop(0, t // rc, ymm, 0, unroll=True)
        _rows_from_scan_order(y_ref, yn_ref, t)
        out.wait()

    return _call(
        body, name="s5_fwd", grid=(S5_BLOCKS,),
        in_specs=[pl.BlockSpec((t, 128), lambda i: (0, i)),
                  pl.BlockSpec((1, 128, 2 * ns), lambda i: (i, 0, 0)),
                  pl.BlockSpec((1, 2 * ns, 128), lambda i: (i, 0, 0)),
                  pl.BlockSpec((1, 1, ns), lambda i: (i, 0, 0)),
                  pl.BlockSpec((1, 1, ns), lambda i: (i, 0, 0)),
                  pl.BlockSpec((1, 128), lambda i: (0, i))],
        out_specs=[ANY, pl.BlockSpec((t, 128), lambda i: (0, i))],
        out_shape=[jax.ShapeDtypeStruct((S5_BLOCKS, t, 2 * ns), F32), jax.ShapeDtypeStruct((t, S5_WIDTH), F32)],
        scratch_shapes=[pltpu.VMEM((t, 2 * ns), F32), pltpu.VMEM((t, 128), F32), pltpu.VMEM((t, 128), F32),
                        pltpu.SemaphoreType.DMA(())],
        args=(ug, bd, ctd, ar4, ai4, dskip), carry=carry)


def _s5_bwd(dy, ug, xs, cd, bdt, ar4, ai4, dskip, carry=None):
    t = ug.shape[0]
    ls = t // S5_SEGS
    rc = min(512, t)
    ns = S5_BSTATE

    def body(dyn_ref, ugn_ref, xs_hbm, cd_ref, bdt_ref, ar_ref, ai_ref, d_ref,
             dugn_ref, dbd_ref, dcd_ref, dd_ref, dar_ref, dai_ref, xbuf, lam, dy_ref, ug_ref, dug_ref, sem):
        cb = pl.program_id(0)
        load = pltpu.make_async_copy(xs_hbm.at[cb], xbuf, sem)
        load.start()
        cdv = cd_ref[0]
        _rows_to_scan_order(dyn_ref, dy_ref, t)
        _rows_to_scan_order(ugn_ref, ug_ref, t)

        def mm(i, c):
            rows = pl.ds(pl.multiple_of(i * rc, rc), rc)
            lam[rows, :] = _nn(dy_ref[rows, :].astype(BF16), cdv)
            return c

        lax.fori_loop(0, t // rc, mm, 0, unroll=True)
        arb = jnp.broadcast_to(ar_ref[0], (S5_SEGS, ns))
        aib = jnp.broadcast_to(ai_ref[0], (S5_SEGS, ns))

        def lam_step(j, lr, li):
            rows = pl.ds(pl.multiple_of(j * S5_SEGS, S5_SEGS), S5_SEGS)
            nr = arb * lr + aib * li + lam[rows, 0:ns]
            ni = arb * li - aib * lr + lam[rows, ns:2 * ns]
            return rows, nr, ni

        def pass1(jj, c):
            _, nr, ni = lam_step(ls - 1 - jj, *c)
            return nr, ni

        zero = jnp.zeros((S5_SEGS, ns), F32)
        er, ei = lax.fori_loop(0, ls, pass1, (zero, zero))
        pr, pi = _cpow(arb, aib, ls)
        init = _chain_segments(er, ei, pr, -pi, down=False)
        load.wait()

        def accumulate(acc, nr, ni, xpr, xpi):
            return acc[0] + nr * xpr + ni * xpi, acc[1] + ni * xpr - nr * xpi

        def pass2(jj, c):
            lr, li, accr, acci = c
            j = ls - 1 - jj
            rows, nr, ni = lam_step(j, lr, li)
            lam[rows, 0:ns] = nr
            lam[rows, ns:2 * ns] = ni
            prev = pl.ds(pl.multiple_of((j - 1) * S5_SEGS, S5_SEGS), S5_SEGS)
            accr, acci = accumulate((accr, acci), nr, ni, xbuf[prev, 0:ns], xbuf[prev, ns:2 * ns])
            return nr, ni, accr, acci

        lr, li, accr, acci = lax.fori_loop(0, ls - 1, pass2, (init[0], init[1], zero, zero))
        rows, nr, ni = lam_step(0, lr, li)
        lam[rows, 0:ns] = nr
        lam[rows, ns:2 * ns] = ni
        last = pl.ds((ls - 1) * S5_SEGS, S5_SEGS)
        accr, acci = accumulate((accr, acci), nr, ni,
                                _shift_rows(xbuf[last, 0:ns], True), _shift_rows(xbuf[last, ns:2 * ns], True))
        dar_ref[0] = jnp.sum(accr, axis=0, keepdims=True)
        dai_ref[0] = jnp.sum(acci, axis=0, keepdims=True)

        bdtv = bdt_ref[0]
        dv = d_ref[...]
        dbd_ref[...] = jnp.zeros_like(dbd_ref)
        dcd_ref[...] = jnp.zeros_like(dcd_ref)
        dd_ref[...] = jnp.zeros_like(dd_ref)

        def tail(i, c):
            rows = pl.ds(pl.multiple_of(i * rc, rc), rc)
            dy = dy_ref[rows, :]
            ug = ug_ref[rows, :]
            lb = lam[rows, :].astype(BF16)
            dug_ref[rows, :] = _nn(lb, bdtv) + dv * dy
            dbd_ref[0] += _tn(ug.astype(BF16), lb)
            dcd_ref[0] += _tn(dy.astype(BF16), xbuf[rows, :].astype(BF16))
            dd_ref[...] += jnp.sum(dy * ug, axis=0, keepdims=True)
            return c

        lax.fori_loop(0, t // rc, tail, 0, unroll=True)
        _rows_from_scan_order(dug_ref, dugn_ref, t)

    chan = pl.BlockSpec((t, 128), lambda i: (0, i))
    dense = pl.BlockSpec((1, 128, 2 * ns), lambda i: (i, 0, 0))
    vec = pl.BlockSpec((1, 1, ns), lambda i: (i, 0, 0))
    return _call(
        body, name="s5_bwd", grid=(S5_BLOCKS,),
        in_specs=[chan, chan, ANY, dense, pl.BlockSpec((1, 2 * ns, 128), lambda i: (i, 0, 0)), vec, vec,
                  pl.BlockSpec((1, 128), lambda i: (0, i))],
        out_specs=[chan, dense, dense, pl.BlockSpec((1, 128), lambda i: (0, i)), vec, vec],
        out_shape=[jax.ShapeDtypeStruct((t, S5_WIDTH), F32),
                   jax.ShapeDtypeStruct((S5_BLOCKS, 128, 2 * ns), F32),
                   jax.ShapeDtypeStruct((S5_BLOCKS, 128, 2 * ns), F32),
                   jax.ShapeDtypeStruct((1, S5_WIDTH), F32),
                   jax.ShapeDtypeStruct((S5_BLOCKS, 1, ns), F32),
                   jax.ShapeDtypeStruct((S5_BLOCKS, 1, ns), F32)],
        scratch_shapes=[pltpu.VMEM((t, 2 * ns), F32), pltpu.VMEM((t, 2 * ns), F32)]
        + [pltpu.VMEM((t, 128), F32)] * 3 + [pltpu.SemaphoreType.DMA(())],
        args=(dy, ug, xs, cd, bdt, ar4, ai4, dskip), carry=carry)


def _cumsum_rows(x, reverse):
    c = x.shape[0]
    row = lax.broadcasted_iota(jnp.int32, x.shape, 0)
    d = 1
    while d < c:
        if reverse:
            x = x + jnp.where(row < c - d, pltpu.roll(x, c - d, 0), 0.0)
        else:
            x = x + jnp.where(row >= d, pltpu.roll(x, d, 0), 0.0)
        d *= 2
    return x


def _gla_common(q, k, alow, wup, bup):
    c = GLA_CHUNK
    pre = _nn(alow.astype(BF16), wup.astype(BF16)) + bup
    la = (jnp.minimum(pre, 0.0) - jnp.log(1.0 + jnp.exp(-jnp.abs(pre)))) * (1.0 / GLA_TAU)
    rr = lax.broadcasted_iota(jnp.int32, (c, c), 0)
    cc = lax.broadcasted_iota(jnp.int32, (c, c), 1)
    tril = (rr >= cc).astype(F32)
    bc = _cumsum_rows(la, reverse=False)
    bl = bc[c - 1:c, :]
    e_pos = jnp.exp(bc)
    e_neg = jnp.exp(-bc)
    e_end = jnp.exp(bl - bc)
    qt = q * (GLA_DK ** -0.5) * e_pos
    kt = k * e_neg
    ke = k * e_end
    lane = lax.broadcasted_iota(jnp.int32, (1, GLA_KEY), 1)
    masks = [((lane >= h * GLA_DK) & (lane < (h + 1) * GLA_DK)).astype(F32) for h in range(GLA_HEADS)]
    return dict(pre=pre, tril=tril, bc=bc, bl=bl, e_pos=e_pos, e_neg=e_neg, e_end=e_end,
                qt=qt, kt=kt, ke=ke, dec=jnp.exp(bl), masks=masks)


def _gla_fwd(q, k, v, alow, wup, bup, carry=None):
    t = q.shape[0]
    c = GLA_CHUNK
    n = t // c
    step = GLA_STEP_CHUNKS * c

    def body(q_ref, k_ref, v_ref, al_ref, wup_ref, bup_ref, o_ref, ss_ref, s_ref):
        i = pl.program_id(0)

        @pl.when(i == 0)
        def _():
            s_ref[...] = jnp.zeros_like(s_ref)

        wup_v, bup_v = wup_ref[...], bup_ref[...]
        s = s_ref[...]
        for j in range(GLA_STEP_CHUNKS):
            tok = slice(j * c, (j + 1) * c)
            m = _gla_common(q_ref[tok, :], k_ref[tok, :], al_ref[tok, :], wup_v, bup_v)
            ss_ref[j] = s
            sb = s.astype(BF16)
            ktb = m["kt"].astype(BF16)
            update = jnp.zeros_like(s)
            for h in range(GLA_HEADS):
                mask = m["masks"][h]
                qm = (m["qt"] * mask).astype(BF16)
                vh = v_ref[tok, h * GLA_DV:(h + 1) * GLA_DV].astype(BF16)
                p = (m["tril"] * _nt(qm, ktb)).astype(BF16)
                o_ref[tok, h * GLA_DV:(h + 1) * GLA_DV] = _nn(p, vh) + _nt(qm, sb)
                update = update + _tn(vh, (m["ke"] * mask).astype(BF16))
            s = m["dec"] * s + update
        s_ref[...] = s

    return _call(
        body, name="gla_fwd", grid=(t // step,),
        in_specs=[_row_tile(step, GLA_KEY), _row_tile(step, GLA_KEY), _row_tile(step, GLA_VAL),
                  _row_tile(step, GLA_RANK), VMEM_FULL, VMEM_FULL],
        out_specs=[_row_tile(step, GLA_VAL), pl.BlockSpec((GLA_STEP_CHUNKS, GLA_DV, GLA_KEY), lambda i: (i, 0, 0))],
        out_shape=[jax.ShapeDtypeStruct((t, GLA_VAL), F32), jax.ShapeDtypeStruct((n, GLA_DV, GLA_KEY), F32)],
        scratch_shapes=[pltpu.VMEM((GLA_DV, GLA_KEY), F32)],
        args=(q, k, v, alow, wup, bup), carry=carry)


def _gla_bwd(q, k, v, alow, wup, bup, ssave, do, carry=None):
    t = q.shape[0]
    c = GLA_CHUNK
    n = t // c

    def body(q_ref, k_ref, v_ref, al_ref, wup_ref, bup_ref, ss_ref, do_ref,
             dq_ref, dk_ref, dv_ref, dal_ref, dwup_ref, dbup_ref, ds_ref):
        i = pl.program_id(0)

        @pl.when(i == 0)
        def _():
            ds_ref[...] = jnp.zeros_like(ds_ref)
            dwup_ref[...] = jnp.zeros_like(dwup_ref)
            dbup_ref[...] = jnp.zeros_like(dbup_ref)

        wup_v, bup_v = wup_ref[...], bup_ref[...]
        ds_in = ds_ref[...]
        dwup = jnp.zeros((GLA_RANK, GLA_KEY), F32)
        dbup = jnp.zeros((1, GLA_KEY), F32)
        for j in reversed(range(GLA_STEP_CHUNKS)):
            tok = slice(j * c, (j + 1) * c)
            alow_v = al_ref[tok, :]
            m = _gla_common(q_ref[tok, :], k_ref[tok, :], alow_v, wup_v, bup_v)
            s = ss_ref[j]
            sb = s.astype(BF16)
            dsb = ds_in.astype(BF16)
            qt, kt, ke = m["qt"], m["kt"], m["ke"]
            ktb = kt.astype(BF16)
            dqt = jnp.zeros((c, GLA_KEY), F32)
            dkt = jnp.zeros((c, GLA_KEY), F32)
            dke = jnp.zeros((c, GLA_KEY), F32)
            update = jnp.zeros_like(ds_in)
            for h in range(GLA_HEADS):
                mask = m["masks"][h]
                qm = (qt * mask).astype(BF16)
                km = (kt * mask).astype(BF16)
                kem = (ke * mask).astype(BF16)
                cols = slice(h * GLA_DV, (h + 1) * GLA_DV)
                vh = v_ref[tok, cols].astype(BF16)
                doh = do_ref[tok, cols].astype(BF16)
                p = (m["tril"] * _nt(qm, ktb)).astype(BF16)
                dp = (m["tril"] * _nt(doh, vh)).astype(BF16)
                dv_ref[tok, cols] = _tn(p, doh) + _nt(kem, dsb)
                dqt = dqt + _nn(dp, km) + _nn(doh, sb) * mask
                dkt = dkt + _tn(dp, qm)
                dke = dke + _nn(vh, dsb) * mask
                update = update + _tn(doh, qm)
            ddec = jnp.sum(ds_in * s, axis=0, keepdims=True)
            dq_ref[tok, :] = dqt * m["e_pos"] * (GLA_DK ** -0.5)
            dk_ref[tok, :] = dkt * m["e_neg"] + dke * m["e_end"]
            dkeke = dke * ke
            dbl = jnp.sum(dkeke, axis=0, keepdims=True) + ddec * m["dec"]
            last = (lax.broadcasted_iota(jnp.int32, (c, 1), 0) == c - 1).astype(F32)
            dla = _cumsum_rows(dqt * qt - dkt * kt - dkeke + last * dbl, reverse=True)
            dpre = dla * (1.0 / GLA_TAU) * jax.nn.sigmoid(-m["pre"])
            dpb = dpre.astype(BF16)
            dal_ref[tok, :] = _nt(dpb, wup_v.astype(BF16))
            dwup = dwup + _tn(alow_v.astype(BF16), dpb)
            dbup = dbup + jnp.sum(dpre, axis=0, keepdims=True)
            ds_in = m["dec"] * ds_in + update
        ds_ref[...] = ds_in
        dwup_ref[...] += dwup
        dbup_ref[...] += dbup

    step = GLA_STEP_CHUNKS * c
    nsteps = t // step

    def rev(d):
        return pl.BlockSpec((step, d), lambda i: (nsteps - 1 - i, 0))

    return _call(
        body, name="gla_bwd", grid=(nsteps,),
        in_specs=[rev(GLA_KEY), rev(GLA_KEY), rev(GLA_VAL), rev(GLA_RANK), VMEM_FULL, VMEM_FULL,
                  pl.BlockSpec((GLA_STEP_CHUNKS, GLA_DV, GLA_KEY), lambda i: (nsteps - 1 - i, 0, 0)), rev(GLA_VAL)],
        out_specs=[rev(GLA_KEY), rev(GLA_KEY), rev(GLA_VAL), rev(GLA_RANK),
                   pl.BlockSpec((GLA_RANK, GLA_KEY), lambda i: (0, 0)), _acc_row(GLA_KEY)],
        out_shape=[jax.ShapeDtypeStruct((t, GLA_KEY), F32), jax.ShapeDtypeStruct((t, GLA_KEY), F32),
                   jax.ShapeDtypeStruct((t, GLA_VAL), F32), jax.ShapeDtypeStruct((t, GLA_RANK), F32),
                   jax.ShapeDtypeStruct((GLA_RANK, GLA_KEY), F32), jax.ShapeDtypeStruct((1, GLA_KEY), F32)],
        scratch_shapes=[pltpu.VMEM((GLA_DV, GLA_KEY), F32)],
        args=(q, k, v, alow, wup, bup, ssave, do), carry=carry)


def _post_math(y, o, r, gs5, ggla, wg, bg, gn, ps5t, pglat):
    y2 = y * y
    th = jnp.tanh(GELU_C0 * (y + GELU_C1 * y * y2))
    z5 = 0.5 * y * (1.0 + th)
    z5b = z5.astype(BF16)
    gate = jax.nn.sigmoid(_nn(z5b, wg) + bg)
    ys5 = z5 * gate
    rs, on = [], []
    for h in range(GLA_HEADS):
        oh = o[:, h * GLA_DV:(h + 1) * GLA_DV]
        rh = lax.rsqrt(jnp.mean(oh * oh, axis=-1, keepdims=True) + EPS)
        rs.append(rh)
        on.append(oh * rh)
    on = jnp.concatenate(on, axis=-1)
    sr = jax.nn.sigmoid(r)
    silu_r = r * sr
    ygla = on * gn * silu_r
    ys5b, yglab = ys5.astype(BF16), ygla.astype(BF16)
    m5 = _nt(ys5b, ps5t)
    mg = _nt(yglab, pglat)
    s5g, glag = jax.nn.sigmoid(gs5), jax.nn.sigmoid(ggla)
    merged = s5g * m5 + glag * mg
    return dict(y2=y2, th=th, z5=z5, z5b=z5b, gate=gate, ys5b=ys5b, yglab=yglab, rs=rs, on=on, sr=sr,
                silu_r=silu_r, m5=m5, mg=mg, s5g=s5g, glag=glag, mergedb=merged.astype(BF16))


def _mix_post_fwd(y, o, r, gs5, ggla, h1, wg, bg, gn, ps5t, pglat, wout, carry=None):
    t = o.shape[0]
    tm = _tile(t)

    def body(y_ref, o_ref, r_ref, gs5_ref, ggla_ref, h1_ref, wg_ref, bg_ref, gn_ref, ps_ref, pg_ref, wo_ref, h2_ref):
        m = _post_math(y_ref[...], o_ref[...], r_ref[...], gs5_ref[...], ggla_ref[...],
                       wg_ref[...], bg_ref[...], gn_ref[...], ps_ref[...], pg_ref[...])
        h2_ref[...] = h1_ref[...] + _nn(m["mergedb"], wo_ref[...])

    (h2,), landed = _call(
        body, name="mix_post_fwd", grid=(t // tm,),
        in_specs=[_row_tile(tm, 512)] * 3 + [_row_tile(tm, D_MODEL)] * 3
        + [VMEM_FULL, _acc_row(512), _acc_row(512), VMEM_FULL, VMEM_FULL, VMEM_FULL],
        out_specs=[_row_tile(tm, D_MODEL)],
        out_shape=[jax.ShapeDtypeStruct((t, D_MODEL), F32)],
        args=(y, o, r, gs5, ggla, h1, wg, bg, gn, ps5t, pglat, wout), carry=carry)
    return h2, landed


def _mix_post_bwd(y, o, r, gs5, ggla, dh2, wg, bg, gn, ps5t, pglat, wout, carry=None):
    t = o.shape[0]
    tm = _tile(t) // 2

    def body(y_ref, o_ref, r_ref, gs5_ref, ggla_ref, dh2_ref, wg_ref, bg_ref, gn_ref, ps_ref, pg_ref, wo_ref,
             dy_ref, do_ref, dr_ref, dgs5_ref, dggla_ref, dbg_ref, dgn_ref,
             z5b_ref, dgp_ref, ys5b_ref, dm5b_ref, yglab_ref, dmgb_ref, mergedb_ref, dh2b_ref):
        i = pl.program_id(0)
        yv, ov, rv = y_ref[...], o_ref[...], r_ref[...]
        wg, gn, ps5t, pglat = wg_ref[...], gn_ref[...], ps_ref[...], pg_ref[...]
        m = _post_math(yv, ov, rv, gs5_ref[...], ggla_ref[...], wg, bg_ref[...], gn, ps5t, pglat)
        dh2b = dh2_ref[...].astype(BF16)
        dmerged = _nt(dh2b, wo_ref[...])
        s5g, glag = m["s5g"], m["glag"]
        dgs5_ref[...] = dmerged * m["m5"] * s5g * (1.0 - s5g)
        dggla_ref[...] = dmerged * m["mg"] * glag * (1.0 - glag)
        dm5b = (dmerged * s5g).astype(BF16)
        dmgb = (dmerged * glag).astype(BF16)
        dys5 = _nn(dm5b, ps5t)
        dygla = _nn(dmgb, pglat)
        gate, z5, th = m["gate"], m["z5"], m["th"]
        dgpre = dys5 * z5 * gate * (1.0 - gate)
        dgpb = dgpre.astype(BF16)
        dz5 = dys5 * gate + _nt(dgpb, wg)
        dgelu = 0.5 * (1.0 + th) + 0.5 * yv * (1.0 - th * th) * GELU_C0 * (1.0 + 3.0 * GELU_C1 * m["y2"])
        dy_ref[...] = dz5 * dgelu
        on, sr, silu_r = m["on"], m["sr"], m["silu_r"]
        dr_ref[...] = dygla * on * gn * sr * (1.0 + rv * (1.0 - sr))
        dgn = jnp.sum(dygla * on * silu_r, axis=0, keepdims=True)
        don = dygla * gn * silu_r
        for h in range(GLA_HEADS):
            cols = slice(h * GLA_DV, (h + 1) * GLA_DV)
            donh, onh = don[:, cols], on[:, cols]
            do_ref[:, cols] = m["rs"][h] * (donh - onh * jnp.mean(donh * onh, axis=-1, keepdims=True))

        @pl.when(i == 0)
        def _():
            dbg_ref[...] = jnp.zeros_like(dbg_ref)
            dgn_ref[...] = jnp.zeros_like(dgn_ref)

        dbg_ref[...] += jnp.sum(dgpre, axis=0, keepdims=True)
        dgn_ref[...] += dgn
        z5b_ref[...] = m["z5b"]
        dgp_ref[...] = dgpb
        ys5b_ref[...] = m["ys5b"]
        dm5b_ref[...] = dm5b
        yglab_ref[...] = m["yglab"]
        dmgb_ref[...] = dmgb
        mergedb_ref[...] = m["mergedb"]
        dh2b_ref[...] = dh2b

    def f32(d):
        return jax.ShapeDtypeStruct((t, d), F32)

    def b16(d):
        return jax.ShapeDtypeStruct((t, d), BF16)

    widths = (512, 512, 512, 1024, 512, 1024, 1024, 1024)
    return _call(
        body, name="mix_post_bwd", grid=(t // tm,),
        in_specs=[_row_tile(tm, 512)] * 3 + [_row_tile(tm, D_MODEL)] * 3
        + [VMEM_FULL, _acc_row(512), _acc_row(512), VMEM_FULL, VMEM_FULL, VMEM_FULL],
        out_specs=[_row_tile(tm, 512)] * 3 + [_row_tile(tm, D_MODEL)] * 2
        + [_acc_row(512)] * 2 + [_row_tile(tm, w) for w in widths],
        out_shape=[f32(512)] * 3 + [f32(D_MODEL)] * 2
        + [jax.ShapeDtypeStruct((1, 512), F32)] * 2
        + [b16(w) for w in widths],
        args=(y, o, r, gs5, ggla, dh2, wg, bg, gn, ps5t, pglat, wout), carry=carry)


ADAM_TILE_ELEMS = 256 * 1024


def _adamw(w, g, m, v, name):
    rows, cols = w.shape
    tr = rows
    while tr * cols > ADAM_TILE_ELEMS and tr % 16 == 0:
        tr //= 2

    spec = pl.BlockSpec((tr, cols), lambda i: (i, 0))
    sh = jax.ShapeDtypeStruct((rows, cols), F32)
    return pl.pallas_call(functools.partial(_adamw_body), name=name, grid=(rows // tr,), in_specs=[spec] * 4,
                          out_specs=[spec] * 3, out_shape=[sh] * 3, compiler_params=_cparams(1))(w, g, m, v)


def _adamw_body(w_ref, g_ref, m_ref, v_ref, d_ref, nm_ref, nv_ref):
    gv = g_ref[...]
    nm = ADAM_B1 * m_ref[...] + (1.0 - ADAM_B1) * gv
    nv = ADAM_B2 * v_ref[...] + (1.0 - ADAM_B2) * (gv * gv)
    m_hat = nm / (1.0 - ADAM_B1 ** ADAM_STEP)
    v_hat = nv / (1.0 - ADAM_B2 ** ADAM_STEP)
    d_ref[...] = -ADAM_LR * (m_hat / (jnp.sqrt(v_hat) + ADAM_EPS) + ADAM_WD * w_ref[...])
    nm_ref[...] = nm
    nv_ref[...] = nv


def _adamw_many(ws, gs, ms, vs, name):
    n = len(ws)

    def body(*refs):
        ins, outs = refs[:4 * n], refs[4 * n:]
        for i in range(n):
            _adamw_body(*(ins[j * n + i] for j in range(4)), *(outs[j * n + i] for j in range(3)))

    shapes = [jax.ShapeDtypeStruct(w.shape, F32) for w in ws]
    res = pl.pallas_call(body, name=name, in_specs=[VMEM_FULL] * (4 * n), out_specs=[VMEM_FULL] * (3 * n),
                         out_shape=shapes * 3)(*ws, *gs, *ms, *vs)
    return res[:n], res[n:2 * n], res[2 * n:]


def _exchange(carry, name):
    return _call(None, name=name, grid=(1,), in_specs=[], out_specs=[], out_shape=[], args=(), carry=carry)[1]


def _pair_add(slabs, from_pair, name):
    _, r, cols = slabs.shape

    def body(s_ref, p_ref, o_ref):
        c = lax.axis_index("c")
        mine = jnp.where(c == 0, s_ref[0, 0].astype(F32), s_ref[0, 1].astype(F32))
        o_ref[0] = (mine + p_ref[0].astype(F32)).astype(BF16)

    return pl.pallas_call(
        body, name=name, grid=(4,),
        in_specs=[pl.BlockSpec((1, 2, r, cols), lambda i: (i, 0, 0, 0)), pl.BlockSpec((1, r, cols), lambda i: (i, 0, 0))],
        out_specs=pl.BlockSpec((1, r, cols), lambda i: (i, 0, 0)),
        out_shape=jax.ShapeDtypeStruct((4, r, cols), BF16),
        compiler_params=_cparams(1),
    )(slabs.reshape(4, 2, r, cols), from_pair)


def _sum_slabs(slabs, name):
    n = slabs.shape[0]

    def body(s_ref, o_ref):
        acc = s_ref[0].astype(F32)
        for s in range(1, n):
            acc = acc + s_ref[s].astype(F32)
        o_ref[...] = acc

    return pl.pallas_call(
        body, name=name, in_specs=[VMEM_FULL], out_specs=VMEM_FULL,
        out_shape=jax.ShapeDtypeStruct(slabs.shape[1:], F32),
        compiler_params=pltpu.CompilerParams(vmem_limit_bytes=VMEM_LIMIT_BYTES),
    )(slabs)


BIG = ("ffn1_w1", "ffn1_w3", "ffn1_w2", "w_in", "s5_glu_w", "gla_a_up_w", "proj_s5", "proj_gla", "w_out",
       "ffn2_w1", "ffn2_w3", "ffn2_w2")
GROUPS = (("ffn1_w1", "ffn1_w3", "ffn1_w2"),
          ("w_in", "s5_glu_w", "gla_a_up_w", "proj_s5", "proj_gla", "w_out"),
          ("ffn2_w1", "ffn2_w3", "ffn2_w2"))
W_IN_ROWS = 514
W_IN_PAD = 528
UP_COLS = 32
ROW_ADAM = ("ffn1_w1", "ffn1_w3", "w_in", "ffn2_w1", "ffn2_w3")
COL_SHARDED = ("ffn1_w1", "ffn1_w3", "w_in", "proj_s5", "proj_gla", "ffn2_w1", "ffn2_w3")

SMALL = ("ffn1_norm", "mix_norm", "s5_lambda_re", "s5_lambda_im", "s5_log_dt", "s5_b_re", "s5_b_im", "s5_c_re",
         "s5_c_im", "s5_d", "s5_glu_b", "gla_a_up_b", "gla_out_norm", "ffn2_norm", "final_norm")
SMALL_SHAPES = dict(ffn1_norm=(1, 1024), mix_norm=(1, 1024), s5_lambda_re=(1, 32, 64), s5_lambda_im=(1, 32, 64),
                    s5_log_dt=(1, 32), s5_b_re=(1, 32, 64, 16), s5_b_im=(1, 32, 64, 16), s5_c_re=(1, 32, 16, 64),
                    s5_c_im=(1, 32, 16, 64), s5_d=(1, 32, 16), s5_glu_b=(1, 512), gla_a_up_b=(1, 256),
                    gla_out_norm=(1, 512), ffn2_norm=(1, 1024), final_norm=(1024,))
SMALL_N = sum(math.prod(s) for s in SMALL_SHAPES.values())
SMALL_R = -(-SMALL_N // (64 * 1024)) * 64


def _shard_rows(name, a):
    if name == "gla_a_up_w":
        return jnp.pad(a, ((0, 0), (0, 128 - UP_COLS)))
    if name in COL_SHARDED:
        a = a.T
    if name == "w_in":
        return jnp.pad(a, ((0, W_IN_PAD - W_IN_ROWS), (0, 0)))
    return a.reshape(-1, 1024)


def _unshard_rows(name, rows, shape):
    if name == "gla_a_up_w":
        return rows[:, :UP_COLS]
    if name == "w_in":
        rows = rows[:W_IN_ROWS]
    if name in COL_SHARDED:
        return rows.reshape(shape[1], shape[0]).T
    return rows.reshape(shape)


def _pack_small(vals, loss):
    flat = jnp.concatenate([vals[n].reshape(-1).astype(F32) for n in SMALL] + [loss.reshape(1)])
    return jnp.pad(flat, (0, SMALL_R * 1024 - SMALL_N - 1)).reshape(SMALL_R, 1024)


S5_B = ("s5_b_re", "s5_b_im")


def _working(name, a):
    return a[0].transpose(0, 2, 1) if name in S5_B else a


def _declared(name, a):
    return a.transpose(0, 2, 1)[None] if name in S5_B else a.reshape(SMALL_SHAPES[name])


def _unpack_small(slab):
    flat = slab.reshape(-1)
    out, off = {}, 0
    for n in SMALL:
        size = math.prod(SMALL_SHAPES[n])
        shape = (S5_GROUPS, S5_GROUP, S5_STATE) if n in S5_B else SMALL_SHAPES[n]
        out[n] = flat[off:off + size].reshape(shape)
        off += size
    return out


FULL_SHAPES = dict(w_in=(IN_COLS, D_MODEL), s5_glu_w=(S5_WIDTH, S5_WIDTH), gla_a_up_w=(GLA_RANK, GLA_KEY),
                   proj_s5=(D_MODEL, S5_WIDTH), proj_gla=(D_MODEL, GLA_VAL), w_out=(D_MODEL, D_MODEL))


def _full_weight(name, gathered):
    if name == "gla_a_up_w":
        return gathered[:, :, :UP_COLS].transpose(1, 0, 2).reshape(GLA_RANK, GLA_KEY)
    if name == "w_in":
        gathered = gathered[:, :W_IN_ROWS]
    return gathered.reshape(FULL_SHAPES.get(name, (D_FF, D_MODEL)))


def _grad_slabs(name, g):
    if name == "gla_a_up_w":
        g = g.reshape(GLA_RANK, N_DEV, UP_COLS).transpose(1, 0, 2)
        return jnp.pad(g, ((0, 0), (0, 0), (0, 128 - UP_COLS))).astype(BF16)
    if name == "w_in":
        return jnp.pad(g.reshape(N_DEV, W_IN_ROWS, D_MODEL), ((0, 0), (0, W_IN_PAD - W_IN_ROWS), (0, 0)))
    return g.reshape(N_DEV, -1, 1024)


def _s5_dense(re, im, sign_im):
    eye = jnp.eye(8, dtype=F32)

    def one(a):
        a = a.reshape(S5_BLOCKS, 8, S5_GROUP, S5_STATE)
        return jnp.einsum("cghp,gk->cghkp", a, eye).reshape(S5_BLOCKS, 128, S5_BSTATE)

    return jnp.concatenate([one(re), sign_im * one(im)], axis=-1)


def _s5_undense(d):
    eye = jnp.eye(8, dtype=F32)

    def one(a):
        a = a.reshape(S5_BLOCKS, 8, S5_GROUP, 8, S5_STATE)
        return jnp.einsum("cghkp,gk->cghp", a, eye).reshape(S5_GROUPS, S5_GROUP, S5_STATE)

    return one(d[..., :S5_BSTATE]), one(d[..., S5_BSTATE:])


def _local_step(x, target, p, w, rows=None):
    w = dict(w or {})
    landed_grads = {}

    def gather(names):
        return None if rows is None else ([rows[n] for n in names], "gather")

    def gathered(names, landed):
        w.update({n: _full_weight(n, g) for n, g in zip(names, landed)})

    def scatter(names):
        return None if rows is None else ([_grad_slabs(n, big[n]) for n in names], "scatter")

    def scattered(names, landed):
        landed_grads.update(zip(names, landed))

    if rows is not None:
        gathered(GROUPS[0], _exchange(gather(GROUPS[0]), "gather_ffn1"))
    g1, gm, g2 = p["ffn1_norm"], p["mix_norm"], p["ffn2_norm"]
    gf = p["final_norm"].reshape(1, D_MODEL)
    lre, lim = p["s5_lambda_re"][0], p["s5_lambda_im"][0]
    ldt = p["s5_log_dt"][0].reshape(S5_GROUPS, 1)
    bre = p["s5_b_re"][0].transpose(0, 2, 1)
    bim = p["s5_b_im"][0].transpose(0, 2, 1)
    cre, cim = p["s5_c_re"][0], p["s5_c_im"][0]
    dskip = p["s5_d"][0].reshape(1, S5_WIDTH)
    bg, bup, gn = p["s5_glu_b"], p["gla_a_up_b"], p["gla_out_norm"]

    mix_first, mix_rest = ("w_in", "gla_a_up_w"), ("s5_glu_w", "proj_s5", "proj_gla", "w_out")
    h1, (a3_1, b3_1, n1), got = _ffn_fwd(x, g1, w["ffn1_w1"], w["ffn1_w3"], w["ffn1_w2"], "ffn1_fwd",
                                         gather(mix_first))
    gathered(mix_first, got)
    wup = w["gla_a_up_w"].astype(F32)
    (u, s5in, q, k, v, r, alow, gs5, ggla), got = _mix_pre_fwd(h1, gm, w["w_in"], gather(mix_rest))
    gathered(mix_rest, got)
    ar, ai, bbr, bbi = _s5_disc(lre, lim, ldt, bre, bim)
    bd = _s5_dense(bbr, bbi, 1.0)
    cd = _s5_dense(cre, cim, -1.0)
    bd16, cd16 = bd.astype(BF16), cd.astype(BF16)
    bdt16, ctd16 = bd16.transpose(0, 2, 1), cd16.transpose(0, 2, 1)
    ar4 = ar.reshape(S5_BLOCKS, 1, S5_BSTATE)
    ai4 = ai.reshape(S5_BLOCKS, 1, S5_BSTATE)
    (xs, y), got = _s5_fwd(s5in, bd16, ctd16, ar4, ai4, dskip, gather(GROUPS[2][:1]))
    gathered(GROUPS[2][:1], got)
    (o, ssave), got = _gla_fwd(q, k, v, alow, wup, bup, gather(GROUPS[2][1:2]))
    gathered(GROUPS[2][1:2], got)
    post_w = (w["s5_glu_w"], bg, gn, w["proj_s5"], w["proj_gla"], w["w_out"])
    h2, got = _mix_post_fwd(y, o, r, gs5, ggla, h1, *post_w, carry=gather(GROUPS[2][2:]))
    gathered(GROUPS[2][2:], got)
    loss, dh3, dgf, (a3_2, b3_2, n2) = _ffn_fwd_head(h2, g2, w["ffn2_w1"], w["ffn2_w3"], w["ffn2_w2"], gf, target,
                                                     "ffn2_fwd")

    big, small = {}, {}
    small["final_norm"] = dgf.reshape(D_MODEL)
    (dh2, dg2, da3, db3, s3, dhh2), _ = _ffn_bwd(
        h2, dh3, g2, a3_2, b3_2, w["ffn2_w1"], w["ffn2_w3"], w["ffn2_w2"], "ffn2_bwd")
    small["ffn2_norm"] = dg2
    big["ffn2_w1"] = _mm_tn(da3, n2, "ffn2_dw1")
    big["ffn2_w3"] = _mm_tn(db3, n2, "ffn2_dw3")
    big["ffn2_w2"] = _mm_tn(s3, dhh2, "ffn2_dw2")
    (dy, do, dr, dgs5, dggla, dbg, dgn, z5b, dgpb, ys5b, dm5b, yglab, dmgb, mergedb, dh2b), got = _mix_post_bwd(
        y, o, r, gs5, ggla, dh2, *post_w, carry=scatter(GROUPS[2][:1]))
    scattered(GROUPS[2][:1], got)
    small["s5_glu_b"] = dbg
    small["gla_out_norm"] = dgn
    big["s5_glu_w"] = _mm_tn(z5b, dgpb, "glu_dw")
    big["proj_s5"] = _mm_tn(dm5b, ys5b, "proj_s5_dw")
    big["proj_gla"] = _mm_tn(dmgb, yglab, "proj_gla_dw")
    big["w_out"] = _mm_tn(mergedb, dh2b, "w_out_dw")
    (dq, dk, dv, dalow, dwup, dbup), got = _gla_bwd(q, k, v, alow, wup, bup, ssave, do, scatter(GROUPS[2][1:2]))
    scattered(GROUPS[2][1:2], got)
    big["gla_a_up_w"] = dwup
    small["gla_a_up_b"] = dbup
    (ds5in, dbd, dcd, dd, dar4, dai4), got = _s5_bwd(
        dy, s5in, xs, cd16, bdt16, ar4, ai4, dskip, scatter(GROUPS[2][2:]))
    scattered(GROUPS[2][2:], got)
    dbbr, dbbi = _s5_undense(dbd)
    dcre, dcim_neg = _s5_undense(dcd)
    glre, glim, gldt, gbre, gbim = _s5_disc_bwd(
        lre, lim, ldt, bre, bim, dar4.reshape(S5_GROUPS, S5_STATE), dai4.reshape(S5_GROUPS, S5_STATE),
        dbbr, dbbi)
    small["s5_lambda_re"] = glre[None]
    small["s5_lambda_im"] = glim[None]
    small["s5_log_dt"] = gldt.reshape(1, S5_GROUPS)
    small["s5_b_re"] = gbre
    small["s5_b_im"] = gbim
    small["s5_c_re"] = dcre[None]
    small["s5_c_im"] = -dcim_neg[None]
    small["s5_d"] = dd.reshape(1, S5_GROUPS, S5_GROUP)
    dz = (ds5in, dq, dk, dv, dr, dalow, dgs5, dggla)
    (dh1, dgm), got = _mix_pre_bwd(h1, gm, w["w_in"], dh2, dz, scatter(mix_rest[:3]))
    scattered(mix_rest[:3], got)
    small["mix_norm"] = dgm
    big["w_in"] = jnp.concatenate([_mm_tn(d, u, "w_in_dw%d" % i) for i, d in enumerate(dz)], axis=0)
    (dx, dg1, da3, db3, s3, dhh1), got = _ffn_bwd(
        x, dh1, g1, a3_1, b3_1, w["ffn1_w1"], w["ffn1_w3"], w["ffn1_w2"], "ffn1_bwd",
        scatter(mix_first + mix_rest[3:]))
    scattered(mix_first + mix_rest[3:], got)
    small["ffn1_norm"] = dg1
    if rows is None:
        big["ffn1_w1"] = _mm_tn(da3, n1, "ffn1_dw1")
        big["ffn1_w3"] = _mm_tn(db3, n1, "ffn1_dw3")
        big["ffn1_w2"] = _mm_tn(s3, dhh1, "ffn1_dw2")
        return loss[0, 0], dx, big, small
    part = _pack_small(small, loss).reshape(N_DEV, SMALL_R // N_DEV, 1024)
    big["ffn1_w1"], (small_landed,) = _mm_tn(da3, n1, "ffn1_dw1", ([part], "scatter"))
    small_mine = _sum_slabs(small_landed, "sum_small")
    slabs1 = _grad_slabs("ffn1_w1", big["ffn1_w1"])
    big["ffn1_w3"], (from_pair, small_all) = _mm_tn(db3, n1, "ffn1_dw3",
                                                    [([slabs1], "pair"), ([small_mine], "gather")])
    small = small_all.reshape(SMALL_R, 1024)
    sums1 = _pair_add(slabs1, from_pair, "ffn1_w1_pair")
    slabs3 = _grad_slabs("ffn1_w3", big["ffn1_w3"])
    big["ffn1_w2"], (landed1, from_pair) = _mm_tn(s3, dhh1, "ffn1_dw2", [([sums1], "chips"), ([slabs3], "pair")])
    sums3 = _pair_add(slabs3, from_pair, "ffn1_w3_pair")
    slabs2 = _grad_slabs("ffn1_w2", big["ffn1_w2"])
    landed3, from_pair = _exchange([([sums3], "chips"), ([slabs2], "pair")], "scatter_ffn1_a")
    sums2 = _pair_add(slabs2, from_pair, "ffn1_w2_pair")
    (landed2,) = _exchange(([sums2], "chips"), "scatter_ffn1_b")
    scattered(GROUPS[0], (landed1, landed3, landed2))
    return loss[0, 0], dx, landed_grads, small


NAMES = ("ffn1_norm", "ffn1_w1", "ffn1_w3", "ffn1_w2", "mix_norm", "w_in", "s5_lambda_re", "s5_lambda_im",
         "s5_log_dt", "s5_b_re", "s5_b_im", "s5_c_re", "s5_c_im", "s5_d", "s5_glu_w", "s5_glu_b", "gla_a_up_w",
         "gla_a_up_b", "gla_out_norm", "proj_s5", "proj_gla", "w_out", "ffn2_norm", "ffn2_w1", "ffn2_w3", "ffn2_w2",
         "final_norm")


def kernel(*args):
    nw = len(NAMES)
    x = args[0][0]
    wts = dict(zip(NAMES, args[1:1 + nw]))
    target = args[1 + nw][0]
    mom = dict(zip(NAMES, args[2 + nw:2 + 2 * nw]))
    var = dict(zip(NAMES, args[2 + 2 * nw:2 + 3 * nw]))

    shards = {n: wts[n][0] for n in BIG}
    rows = {n: _shard_rows(n, shards[n]).astype(BF16) for n in BIG}
    _, dx, landed, small_slab = _local_step(x, target, {n: wts[n] for n in SMALL}, None, rows)
    loss = small_slab.reshape(-1)[SMALL_N]
    g_small = _unpack_small(small_slab)

    grad, delta, new_m, new_v = {}, {}, {}, {}
    for n in BIG:
        g_rows = _sum_slabs(landed[n], "sum_" + n)
        if n in ROW_ADAM:
            g = g_rows[:W_IN_ROWS] if n == "w_in" else g_rows
            outs = _adamw(shards[n].T, g, mom[n][0].T, var[n][0].T, "adamw_" + n)
            grad[n], delta[n], new_m[n], new_v[n] = (a.T[None] for a in (g, *outs))
        else:
            g = _unshard_rows(n, g_rows, shards[n].shape)
            outs = _adamw(shards[n], g, mom[n][0], var[n][0], "adamw_" + n)
            grad[n], delta[n], new_m[n], new_v[n] = (a[None] for a in (g, *outs))

    def flat2d(a):
        return a.reshape(-1, a.shape[-1])

    operands = ([flat2d(_working(n, d[n])) for n in SMALL] for d in (wts, mom, var))
    w2d, m2d, v2d = operands
    outs = _adamw_many(w2d, [flat2d(g_small[n]) for n in SMALL], m2d, v2d, "adamw_small")
    for out, arrays in zip((grad, delta, new_m, new_v), ([g_small[n] for n in SMALL], *outs)):
        out.update({n: _declared(n, a.reshape(g_small[n].shape)) for n, a in zip(SMALL, arrays)})
    return (loss, dx[None], *(d[n] for d in (grad, delta, new_m, new_v) for n in NAMES))
```

```python
import functools
import math

import jax
import jax.numpy as jnp
from jax import lax
from jax.experimental import pallas as pl
from jax.experimental.pallas import tpu as pltpu

F32, BF16 = jnp.float32, jnp.bfloat16
HIGHEST = lax.Precision.HIGHEST

D_MODEL = 1024
D_FF = 2816
N_DEV = 8
S5_WIDTH, S5_GROUPS, S5_GROUP, S5_STATE = 512, 32, 16, 64
S5_BLOCKS = 4
S5_BSTATE = 512
S5_SEGS = 8
GLA_HEADS, GLA_DK, GLA_DV = 4, 64, 128
GLA_KEY, GLA_VAL, GLA_RANK, GLA_CHUNK = 256, 512, 16, 64
GLA_TAU = 16.0
GLA_STEP_CHUNKS = 4
EPS = 1e-6
IN_SIZES = (512, 256, 256, 512, 512, 16, 1024, 1024)
IN_OFFS = tuple(sum(IN_SIZES[:i]) for i in range(len(IN_SIZES)))
IN_COLS = sum(IN_SIZES)
ADAM_LR, ADAM_B1, ADAM_B2, ADAM_EPS, ADAM_WD, ADAM_STEP = 0.001, 0.9, 0.999, 1e-08, 0.01, 10
GELU_C0 = math.sqrt(2.0 / math.pi)
GELU_C1 = 0.044715

FFN_FT = 256
VMEM_LIMIT_BYTES = 56 * 1024 * 1024

VMEM_FULL = pl.BlockSpec(memory_space=pltpu.VMEM)
ANY = pl.BlockSpec(memory_space=pl.ANY)


def _cparams(n_grid):
    return pltpu.CompilerParams(dimension_semantics=("arbitrary",) * n_grid, vmem_limit_bytes=VMEM_LIMIT_BYTES)


def _tile(t):
    return 512 if t >= 1024 else t // 2


def _nn(a, b):
    return jnp.dot(a, b, preferred_element_type=F32)


def _nt(a, b):
    return lax.dot_general(a, b, (((1,), (1,)), ((), ())), preferred_element_type=F32)


def _tn(a, b):
    return lax.dot_general(a, b, (((0,), (0,)), ((), ())), preferred_element_type=F32)


def _rms_parts(x):
    r = lax.rsqrt(jnp.mean(x * x, axis=-1, keepdims=True) + EPS)
    return x * r, r


def _rms_bwd(dn, g, xhat, r):
    dxh = dn * g
    dx = r * (dxh - xhat * jnp.mean(dxh * xhat, axis=-1, keepdims=True))
    return dx, jnp.sum(dn * xhat, axis=0, keepdims=True)


def _peers():
    x, y, c = lax.axis_index("x"), lax.axis_index("y"), lax.axis_index("c")
    out = []
    for k in range(1, N_DEV):
        px = 1 - x if k & 4 else x
        py = 1 - y if k & 2 else y
        pc = 1 - c if k & 1 else c
        out.append(((px, py, pc), 4 * px + 2 * py + pc))
    return 4 * x + 2 * y + c, out


def _exchange_copies(src_refs, out_refs, send_sems, recv_sems, local_sems, scatter, with_recvs):
    me, peers = _peers()
    locals_, sends, recvs = [], [], []
    for a, (src_ref, out_ref) in enumerate(zip(src_refs, out_refs)):
        def mine(idx, src_ref=src_ref):
            return src_ref.at[idx] if scatter else src_ref

        locals_.append(pltpu.make_async_copy(mine(me), out_ref.at[me], local_sems.at[a]))
        for k, (dev, idx) in enumerate(peers):
            sends.append(pltpu.make_async_remote_copy(
                src_ref=mine(idx), dst_ref=out_ref.at[me], send_sem=send_sems.at[a, k], recv_sem=recv_sems.at[a, k],
                device_id=dev, device_id_type=pl.DeviceIdType.MESH))
            if with_recvs:
                recvs.append(pltpu.make_async_remote_copy(
                    src_ref=mine(idx), dst_ref=out_ref.at[idx], send_sem=send_sems.at[a, k],
                    recv_sem=recv_sems.at[a, k], device_id=dev, device_id_type=pl.DeviceIdType.MESH))
    return locals_, sends, recvs


def _remote(src, dst, send_sems, recv_sems, a, k, dev):
    return pltpu.make_async_remote_copy(src_ref=src, dst_ref=dst, send_sem=send_sems.at[a, k],
                                        recv_sem=recv_sems.at[a, k], device_id=dev,
                                        device_id_type=pl.DeviceIdType.MESH)


def _gather_places():
    x, y, c = lax.axis_index("x"), lax.axis_index("y"), lax.axis_index("c")
    chips = [(1 - x, y), (x, 1 - y), (1 - x, 1 - y)]
    sibling = (x, y, 1 - c)
    me_idx, sib_idx = 4 * x + 2 * y + c, 4 * x + 2 * y + 1 - c
    same_core = [((cx, cy, c), 4 * cx + 2 * cy + c) for cx, cy in chips]
    other_core_idx = [4 * cx + 2 * cy + 1 - c for cx, cy in chips]
    return sibling, me_idx, sib_idx, same_core, other_core_idx


def _gather_start(src_refs, out_refs, send_sems, recv_sems, local_sems):
    sibling, me_idx, _, same_core, _ = _gather_places()
    for a, (src, out) in enumerate(zip(src_refs, out_refs)):
        pltpu.make_async_copy(src, out.at[me_idx], local_sems.at[a]).start()
        _remote(src, out.at[me_idx], send_sems, recv_sems, a, 0, sibling).start()
        for j, (dev, _) in enumerate(same_core):
            _remote(src, out.at[me_idx], send_sems, recv_sems, a, 1 + j, dev).start()


def _gather_forward(src_refs, out_refs, send_sems, recv_sems, local_sems):
    sibling, _, _, same_core, _ = _gather_places()
    for a, (src, out) in enumerate(zip(src_refs, out_refs)):
        for j, (dev, idx) in enumerate(same_core):
            _remote(src, out.at[idx], send_sems, recv_sems, a, 1 + j, dev).wait_recv()
            _remote(out.at[idx], out.at[idx], send_sems, recv_sems, a, 4 + j, sibling).start()


def _gather_finish(src_refs, out_refs, send_sems, recv_sems, local_sems):
    sibling, me_idx, sib_idx, same_core, other_core_idx = _gather_places()
    for a, (src, out) in enumerate(zip(src_refs, out_refs)):
        _remote(src, out.at[sib_idx], send_sems, recv_sems, a, 0, sibling).wait_recv()
        for j, idx in enumerate(other_core_idx):
            _remote(src, out.at[idx], send_sems, recv_sems, a, 4 + j, sibling).wait_recv()
        _remote(src, out.at[me_idx], send_sems, recv_sems, a, 0, sibling).wait_send()
        for j, (dev, idx) in enumerate(same_core):
            _remote(src, out.at[me_idx], send_sems, recv_sems, a, 1 + j, dev).wait_send()
            _remote(out.at[idx], out.at[idx], send_sems, recv_sems, a, 4 + j, sibling).wait_send()
        pltpu.make_async_copy(src, out.at[me_idx], local_sems.at[a]).wait()


def _exchange_start(*refs, scatter):
    locals_, sends, _ = _exchange_copies(*refs, scatter=scatter, with_recvs=False)
    for cp in locals_ + sends:
        cp.start()


def _exchange_wait(*refs, scatter):
    locals_, sends, recvs = _exchange_copies(*refs, scatter=scatter, with_recvs=True)
    for cp in recvs:
        cp.wait_recv()
    for cp in sends:
        cp.wait_send()
    for cp in locals_:
        cp.wait()


def _halves_places():
    x, y, c = lax.axis_index("x"), lax.axis_index("y"), lax.axis_index("c")
    flips = [(1 - x, y), (x, 1 - y), (1 - x, 1 - y)]
    return (x, y, 1 - c), c, 2 * x + y, [((fx, fy, c), 2 * fx + fy) for fx, fy in flips]


def _pair_start(src_refs, out_refs, send_sems, recv_sems, local_sems):
    sibling, c, _, _ = _halves_places()
    for a, (src, out) in enumerate(zip(src_refs, out_refs)):
        for i in range(4):
            _remote(src.at[2 * i + 1 - c], out.at[i], send_sems, recv_sems, a, i, sibling).start()


def _pair_finish(src_refs, out_refs, send_sems, recv_sems, local_sems):
    sibling, c, _, _ = _halves_places()
    for a, (src, out) in enumerate(zip(src_refs, out_refs)):
        for i in range(4):
            _remote(src.at[2 * i + 1 - c], out.at[i], send_sems, recv_sems, a, i, sibling).wait()


def _chips_start(src_refs, out_refs, send_sems, recv_sems, local_sems):
    _, _, chip, others = _halves_places()
    for a, (src, out) in enumerate(zip(src_refs, out_refs)):
        pltpu.make_async_copy(src.at[chip], out.at[chip], local_sems.at[a]).start()
        for k, (dev, their_chip) in enumerate(others):
            _remote(src.at[their_chip], out.at[chip], send_sems, recv_sems, a, k, dev).start()


def _chips_finish(src_refs, out_refs, send_sems, recv_sems, local_sems):
    _, _, chip, others = _halves_places()
    for a, (src, out) in enumerate(zip(src_refs, out_refs)):
        for k, (dev, their_chip) in enumerate(others):
            _remote(src.at[their_chip], out.at[their_chip], send_sems, recv_sems, a, k, dev).wait_recv()
        for k, (dev, their_chip) in enumerate(others):
            _remote(src.at[their_chip], out.at[chip], send_sems, recv_sems, a, k, dev).wait_send()
        pltpu.make_async_copy(src.at[chip], out.at[chip], local_sems.at[a]).wait()


EXCHANGES = {
    "gather": (_gather_start, _gather_forward, _gather_finish, N_DEV, False),
    "scatter": (functools.partial(_exchange_start, scatter=True), None, functools.partial(_exchange_wait, scatter=True),
                N_DEV, True),
    "pair": (_pair_start, None, _pair_finish, 4, True),
    "chips": (_chips_start, None, _chips_finish, 4, True),
}


def _exchange_sems(n_arrays):
    return [pltpu.SemaphoreType.DMA((n_arrays, N_DEV - 1)), pltpu.SemaphoreType.DMA((n_arrays, N_DEV - 1)),
            pltpu.SemaphoreType.DMA((n_arrays,))]


def _exchange_shapes(srcs, kind):
    lead, slabbed = EXCHANGES[kind][3:]
    return [jax.ShapeDtypeStruct((lead,) + tuple(s.shape[1:] if slabbed else s.shape), s.dtype) for s in srcs]


def _carries(carry):
    if carry is None:
        return []
    return [carry] if isinstance(carry, tuple) else list(carry)


def _call(body, *, name, grid, in_specs, out_specs, out_shape, args, scratch_shapes=(), carry=None):
    n_in, n_out, n_scr = len(in_specs), len(out_specs), len(scratch_shapes)
    groups = _carries(carry)
    sizes = [len(arrays) for arrays, _ in groups]
    nc = sum(sizes)

    def wrapped(*refs):
        ins, refs = refs[:n_in], refs[n_in:]
        csrc, refs = refs[:nc], refs[nc:]
        outs, refs = refs[:n_out], refs[n_out:]
        cland, refs = refs[:nc], refs[nc:]
        scr, sems = refs[:n_scr], refs[n_scr:]

        def run(phase):
            at = 0
            for gi, ((_, kind), size) in enumerate(zip(groups, sizes)):
                if EXCHANGES[kind][phase] is not None:
                    EXCHANGES[kind][phase](csrc[at:at + size], cland[at:at + size], *sems[3 * gi:3 * gi + 3])
                at += size

        last = pl.program_id(0) == grid[0] - 1
        if nc:
            pl.when(pl.program_id(0) == 0)(functools.partial(run, 0))
            pl.when(last)(functools.partial(run, 1))
        if body is not None:
            body(*ins, *outs, *scr)
        if nc:
            pl.when(last)(functools.partial(run, 2))

    res = pl.pallas_call(
        wrapped, name=name, grid=grid,
        in_specs=list(in_specs) + [ANY] * nc, out_specs=list(out_specs) + [ANY] * nc,
        out_shape=list(out_shape) + [s for arrays, kind in groups for s in _exchange_shapes(arrays, kind)],
        scratch_shapes=list(scratch_shapes) + [s for size in sizes for s in _exchange_sems(size)],
        compiler_params=_cparams(1),
    )(*args, *[a for arrays, _ in groups for a in arrays])
    return res[:n_out], res[n_out:]


def _row_tile(tm, d):
    return pl.BlockSpec((tm, d), lambda i: (i, 0))


def _acc_row(d):
    return pl.BlockSpec((1, d), lambda i: (0, 0))


def _ffn_body(x_ref, g_ref, w1_ref, w3_ref, w2_ref, acc_ref, a_ref, b_ref, n_ref):
    xv = x_ref[...]
    xhat, _ = _rms_parts(xv)
    n = (xhat * g_ref[...]).astype(BF16)
    n_ref[...] = n
    acc_ref[...] = xv

    def fstep(f, c):
        rows = pl.ds(pl.multiple_of(f * FFN_FT, FFN_FT), FFN_FT)
        a = _nt(n, w1_ref[rows, :])
        b = _nt(n, w3_ref[rows, :])
        a_ref[f] = a.astype(BF16)
        b_ref[f] = b.astype(BF16)
        s = (a * jax.nn.sigmoid(a) * b).astype(BF16)
        acc_ref[...] += 0.5 * _nn(s, w2_ref[rows, :])
        return c

    lax.fori_loop(0, D_FF // FFN_FT, fstep, 0, unroll=True)


def _ffn_fwd(x, g, w1t, w3t, w2, name, carry=None):
    t = x.shape[0]
    tm = _tile(t)
    nf = D_FF // FFN_FT
    blk3 = pl.BlockSpec((nf, tm, FFN_FT), lambda i: (0, i, 0))
    sh3 = jax.ShapeDtypeStruct((nf, t, FFN_FT), BF16)
    (h, a3, b3, n), landed = _call(
        functools.partial(_ffn_body), name=name, grid=(t // tm,),
        in_specs=[_row_tile(tm, D_MODEL), _acc_row(D_MODEL), VMEM_FULL, VMEM_FULL, VMEM_FULL],
        out_specs=[_row_tile(tm, D_MODEL), blk3, blk3, _row_tile(tm, D_MODEL)],
        out_shape=[jax.ShapeDtypeStruct((t, D_MODEL), F32), sh3, sh3, jax.ShapeDtypeStruct((t, D_MODEL), BF16)],
        args=(x, g, w1t, w3t, w2), carry=carry)
    return h, (a3, b3, n), landed


def _ffn_fwd_head(x, g, w1t, w3t, w2, gf, target, name):
    t = x.shape[0]
    tm = _tile(t)
    nf = D_FF // FFN_FT

    def body(x_ref, g_ref, w1_ref, w3_ref, w2_ref, gf_ref, t_ref, loss_ref, dh_ref, dgf_ref, a_ref, b_ref, n_ref, acc):
        _ffn_body(x_ref, g_ref, w1_ref, w3_ref, w2_ref, acc, a_ref, b_ref, n_ref)
        _head_math(acc[...], gf_ref[...], t_ref[...], loss_ref, dh_ref, dgf_ref)

    blk3 = pl.BlockSpec((nf, tm, FFN_FT), lambda i: (0, i, 0))
    sh3 = jax.ShapeDtypeStruct((nf, t, FFN_FT), BF16)
    (loss, dh, dgf, a3, b3, n), _ = _call(
        body, name=name, grid=(t // tm,),
        in_specs=[_row_tile(tm, D_MODEL), _acc_row(D_MODEL), VMEM_FULL, VMEM_FULL, VMEM_FULL, _acc_row(D_MODEL),
                  _row_tile(tm, D_MODEL)],
        out_specs=[pl.BlockSpec((1, 1), lambda i: (0, 0)), _row_tile(tm, D_MODEL), _acc_row(D_MODEL), blk3, blk3,
                   _row_tile(tm, D_MODEL)],
        out_shape=[jax.ShapeDtypeStruct((1, 1), F32), jax.ShapeDtypeStruct((t, D_MODEL), F32),
                   jax.ShapeDtypeStruct((1, D_MODEL), F32), sh3, sh3, jax.ShapeDtypeStruct((t, D_MODEL), BF16)],
        scratch_shapes=[pltpu.VMEM((tm, D_MODEL), F32)],
        args=(x, g, w1t, w3t, w2, gf, target))
    return loss, dh, dgf, (a3, b3, n)


def _head_math(h, gv, target, loss_ref, dh_ref, dg_ref):
    i = pl.program_id(0)
    xhat, r = _rms_parts(h)
    err = xhat * gv - target
    dx, dg = _rms_bwd(err * (1.0 / D_MODEL), gv, xhat, r)
    dh_ref[...] = dx

    @pl.when(i == 0)
    def _():
        loss_ref[...] = jnp.zeros_like(loss_ref)
        dg_ref[...] = jnp.zeros_like(dg_ref)

    loss_ref[...] += (0.5 / D_MODEL) * jnp.sum(jnp.sum(err * err, axis=1, keepdims=True), axis=0, keepdims=True)
    dg_ref[...] += dg


def _ffn_bwd(x, dh, g, a3, b3, w1t, w3t, w2, name, carry=None):
    t = x.shape[0]
    tm = _tile(t) // 2
    nf = D_FF // FFN_FT

    def body(x_ref, dh_ref, g_ref, a_ref, b_ref, w1_ref, w3_ref, w2_ref,
             dx_ref, dg_ref, da_ref, db_ref, s_ref, dhh_ref, dn_acc):
        i = pl.program_id(0)
        xv = x_ref[...]
        gv = g_ref[...]
        xhat, r = _rms_parts(xv)
        dhv = dh_ref[...]
        dhh = (0.5 * dhv).astype(BF16)
        dhh_ref[...] = dhh
        dn_acc[...] = jnp.zeros_like(dn_acc)

        def fstep(f, c):
            rows = pl.ds(pl.multiple_of(f * FFN_FT, FFN_FT), FFN_FT)
            w1c, w3c, w2c = w1_ref[rows, :], w3_ref[rows, :], w2_ref[rows, :]
            a = a_ref[f].astype(F32)
            b = b_ref[f].astype(F32)
            sg = jax.nn.sigmoid(a)
            sl = a * sg
            ds = _nt(dhh, w2c)
            da = (ds * b * sg * (1.0 + a * (1.0 - sg))).astype(BF16)
            db = (ds * sl).astype(BF16)
            s_ref[f] = (sl * b).astype(BF16)
            da_ref[f] = da
            db_ref[f] = db
            dn_acc[...] += _nn(da, w1c) + _nn(db, w3c)
            return c

        lax.fori_loop(0, nf, fstep, 0, unroll=True)
        dx, dg = _rms_bwd(dn_acc[...], gv, xhat, r)
        dx_ref[...] = dhv + dx

        @pl.when(i == 0)
        def _():
            dg_ref[...] = jnp.zeros_like(dg_ref)

        dg_ref[...] += dg

    blk3 = pl.BlockSpec((nf, tm, FFN_FT), lambda i: (0, i, 0))
    sh3 = jax.ShapeDtypeStruct((nf, t, FFN_FT), BF16)
    return _call(
        body, name=name, grid=(t // tm,),
        in_specs=[_row_tile(tm, D_MODEL), _row_tile(tm, D_MODEL), _acc_row(D_MODEL), blk3, blk3,
                  VMEM_FULL, VMEM_FULL, VMEM_FULL],
        out_specs=[_row_tile(tm, D_MODEL), _acc_row(D_MODEL), blk3, blk3, blk3, _row_tile(tm, D_MODEL)],
        out_shape=[jax.ShapeDtypeStruct((t, D_MODEL), F32), jax.ShapeDtypeStruct((1, D_MODEL), F32), sh3, sh3, sh3,
                   jax.ShapeDtypeStruct((t, D_MODEL), BF16)],
        scratch_shapes=[pltpu.VMEM((tm, D_MODEL), F32)],
        args=(x, dh, g, a3, b3, w1t, w3t, w2), carry=carry)


def _mm_tn(a, b, name, carry=None):
    t, n = b.shape
    kc = min(512, t)
    if a.ndim == 3:
        nb, _, tb = a.shape
        a_spec = pl.BlockSpec((1, t, tb), lambda i: (i, 0, 0))
    else:
        m = a.shape[1]
        tb = min(m, 256)
        nb = m // tb
        a_spec = pl.BlockSpec((t, tb), lambda i: (0, i))
    three_d = a.ndim == 3

    def body(a_ref, b_ref, o_ref, acc):
        acc[...] = jnp.zeros_like(acc)

        def kstep(k, c):
            rows = pl.ds(pl.multiple_of(k * kc, kc), kc)
            av = a_ref[0, rows, :] if three_d else a_ref[rows, :]
            acc[...] += _tn(av.astype(BF16), b_ref[rows, :])
            return c

        lax.fori_loop(0, t // kc, kstep, 0, unroll=True)
        o_ref[...] = acc[...].astype(BF16)

    (out,), landed = _call(
        body, name=name, grid=(nb,),
        in_specs=[a_spec, VMEM_FULL],
        out_specs=[pl.BlockSpec((tb, n), lambda i: (i, 0))],
        out_shape=[jax.ShapeDtypeStruct((nb * tb, n), BF16)],
        scratch_shapes=[pltpu.VMEM((tb, n), F32)],
        args=(a, b), carry=carry)
    return (out, landed) if carry is not None else out


def _mix_pre_fwd(h, g, wint, carry=None):
    t = h.shape[0]
    tm = _tile(t)

    def body(h_ref, g_ref, w_ref, u_ref, *outs):
        xhat, _ = _rms_parts(h_ref[...])
        u = (xhat * g_ref[...]).astype(BF16)
        u_ref[...] = u
        for o_ref, off, size in zip(outs, IN_OFFS, IN_SIZES):
            o_ref[...] = _nt(u, w_ref[off:off + size, :])

    return _call(
        body, name="mix_pre_fwd", grid=(t // tm,),
        in_specs=[_row_tile(tm, D_MODEL), _acc_row(D_MODEL), VMEM_FULL],
        out_specs=[_row_tile(tm, D_MODEL)] + [_row_tile(tm, s) for s in IN_SIZES],
        out_shape=[jax.ShapeDtypeStruct((t, D_MODEL), BF16)] + [jax.ShapeDtypeStruct((t, s), F32) for s in IN_SIZES],
        args=(h, g, wint), carry=carry)


def _mix_pre_bwd(h, g, wint, dh2, dz, carry=None):
    t = h.shape[0]
    tm = _tile(t)

    def body(h_ref, g_ref, w_ref, dh2_ref, *rest):
        dz_refs, (dh1_ref, dg_ref) = rest[:len(IN_SIZES)], rest[len(IN_SIZES):]
        i = pl.program_id(0)
        gv = g_ref[...]
        xhat, r = _rms_parts(h_ref[...])
        du = jnp.zeros((tm, D_MODEL), F32)
        for dz_ref, off, size in zip(dz_refs, IN_OFFS, IN_SIZES):
            du = du + _nn(dz_ref[...].astype(BF16), w_ref[off:off + size, :])
        dx, dg = _rms_bwd(du, gv, xhat, r)
        dh1_ref[...] = dh2_ref[...] + dx

        @pl.when(i == 0)
        def _():
            dg_ref[...] = jnp.zeros_like(dg_ref)

        dg_ref[...] += dg

    return _call(
        body, name="mix_pre_bwd", grid=(t // tm,),
        in_specs=[_row_tile(tm, D_MODEL), _acc_row(D_MODEL), VMEM_FULL, _row_tile(tm, D_MODEL)]
        + [_row_tile(tm, s) for s in IN_SIZES],
        out_specs=[_row_tile(tm, D_MODEL), _acc_row(D_MODEL)],
        out_shape=[jax.ShapeDtypeStruct((t, D_MODEL), F32), jax.ShapeDtypeStruct((1, D_MODEL), F32)],
        args=(h, g, wint, dh2, *dz), carry=carry)


def _disc_math(lre, lim, ldt, bre, bim):
    dt = jnp.exp(ldt)
    mag = jnp.exp(lre * dt)
    ar = mag * jnp.cos(lim * dt)
    ai = mag * jnp.sin(lim * dt)
    den = lre * lre + lim * lim
    nr = ar - 1.0
    fr = (nr * lre + ai * lim) / den
    fi = (ai * lre - nr * lim) / den
    fr, fi = fr[:, None, :], fi[:, None, :]
    return ar, ai, fr * bre - fi * bim, fr * bim + fi * bre


def _s5_disc(lre, lim, ldt, bre, bim):
    def body(lre_ref, lim_ref, ldt_ref, bre_ref, bim_ref, ar_ref, ai_ref, bbr_ref, bbi_ref):
        ar, ai, bbr, bbi = _disc_math(lre_ref[...], lim_ref[...], ldt_ref[...], bre_ref[...], bim_ref[...])
        ar_ref[...] = ar
        ai_ref[...] = ai
        bbr_ref[...] = bbr
        bbi_ref[...] = bbi

    small = jax.ShapeDtypeStruct(lre.shape, F32)
    big = jax.ShapeDtypeStruct(bre.shape, F32)
    return pl.pallas_call(body, name="s5_disc", out_shape=[small, small, big, big],
                          in_specs=[VMEM_FULL] * 5, out_specs=[VMEM_FULL] * 4)(lre, lim, ldt, bre, bim)


def _s5_disc_bwd(lre, lim, ldt, bre, bim, dar, dai, dbbr, dbbi):
    def body(lre_ref, lim_ref, ldt_ref, bre_ref, bim_ref, dar_ref, dai_ref, dbbr_ref, dbbi_ref,
             glre_ref, glim_ref, gldt_ref, gbre_ref, gbim_ref):
        _, vjp = jax.vjp(_disc_math, lre_ref[...], lim_ref[...], ldt_ref[...], bre_ref[...], bim_ref[...])
        glre, glim, gldt, gbre, gbim = vjp((dar_ref[...], dai_ref[...], dbbr_ref[...], dbbi_ref[...]))
        glre_ref[...] = glre
        glim_ref[...] = glim
        gldt_ref[...] = gldt
        gbre_ref[...] = gbre
        gbim_ref[...] = gbim

    small = jax.ShapeDtypeStruct(lre.shape, F32)
    big = jax.ShapeDtypeStruct(bre.shape, F32)
    return pl.pallas_call(body, name="s5_disc_bwd",
                          out_shape=[small, small, jax.ShapeDtypeStruct(ldt.shape, F32), big, big],
                          in_specs=[VMEM_FULL] * 9, out_specs=[VMEM_FULL] * 5,
                          )(lre, lim, ldt, bre, bim, dar, dai, dbbr, dbbi)


def _cmul(ar, ai, br, bi):
    return ar * br - ai * bi, ar * bi + ai * br


def _cpow(ar, ai, n):
    rr, ri = None, None
    pr, pi = ar, ai
    while n:
        if n & 1:
            rr, ri = (pr, pi) if rr is None else _cmul(rr, ri, pr, pi)
        n >>= 1
        if n:
            pr, pi = _cmul(pr, pi, pr, pi)
    return rr, ri


def _shift_rows(v, down):
    row = lax.broadcasted_iota(jnp.int32, v.shape, 0)
    if down:
        return jnp.where(row == 0, 0.0, pltpu.roll(v, 1, 0))
    return jnp.where(row == S5_SEGS - 1, 0.0, pltpu.roll(v, S5_SEGS - 1, 0))


def _chain_segments(er, ei, pr, pi, down):
    fr, fi = er, ei
    for _ in range(S5_SEGS - 1):
        sr, si = _shift_rows(fr, down), _shift_rows(fi, down)
        mr, mi = _cmul(pr, pi, sr, si)
        fr, fi = er + mr, ei + mi
    return _shift_rows(fr, down), _shift_rows(fi, down)


def _rows_to_scan_order(src_ref, dst_ref, t):
    ls = t // S5_SEGS

    def tile(j, c):
        dst_ref[pl.ds(pl.multiple_of(j * S5_SEGS, S5_SEGS), S5_SEGS), :] = src_ref[pl.ds(j, S5_SEGS, stride=ls), :]
        return c

    lax.fori_loop(0, ls, tile, 0, unroll=8)


def _rows_from_scan_order(src_ref, dst_ref, t):
    ls = t // S5_SEGS
    for s in range(S5_SEGS):
        def tile(jb, c, s=s):
            dst_ref[pl.ds(pl.multiple_of(s * ls + jb * 8, 8), 8), :] = (
                src_ref[pl.ds(jb * 8 * S5_SEGS + s, 8, stride=S5_SEGS), :])
            return c

        lax.fori_loop(0, ls // 8, tile, 0, unroll=8)


def _s5_fwd(ug, bd, ctd, ar4, ai4, dskip, carry=None):
    t = ug.shape[0]
    ls = t // S5_SEGS
    rc = min(512, t)
    ns = S5_BSTATE

    def body(ugn_ref, bd_ref, ct_ref, ar_ref, ai_ref, d_ref, xs_hbm, yn_ref, buf, ug_ref, y_ref, sem):
        cb = pl.program_id(0)
        bdv = bd_ref[0]
        _rows_to_scan_order(ugn_ref, ug_ref, t)

        def mm(i, c):
            rows = pl.ds(pl.multiple_of(i * rc, rc), rc)
            buf[rows, :] = _nn(ug_ref[rows, :].astype(BF16), bdv)
            return c

        lax.fori_loop(0, t // rc, mm, 0, unroll=True)
        arb = jnp.broadcast_to(ar_ref[0], (S5_SEGS, ns))
        aib = jnp.broadcast_to(ai_ref[0], (S5_SEGS, ns))

        def step(j, c, store):
            sr, si = c
            rows = pl.ds(pl.multiple_of(j * S5_SEGS, S5_SEGS), S5_SEGS)
            nr = arb * sr - aib * si + buf[rows, 0:ns]
            ni = arb * si + aib * sr + buf[rows, ns:2 * ns]
            if store:
                buf[rows, 0:ns] = nr
                buf[rows, ns:2 * ns] = ni
            return nr, ni

        zero = jnp.zeros((S5_SEGS, ns), F32)
        er, ei = lax.fori_loop(0, ls, functools.partial(step, store=False), (zero, zero))
        pr, pi = _cpow(arb, aib, ls)
        init = _chain_segments(er, ei, pr, pi, down=True)
        lax.fori_loop(0, ls, functools.partial(step, store=True), init)

        out = pltpu.make_async_copy(buf, xs_hbm.at[cb], sem)
        out.start()
        ctv = ct_ref[0]
        dv = d_ref[...]

        def ymm(i, c):
            rows = pl.ds(pl.multiple_of(i * rc, rc), rc)
            y_ref[rows, :] = _nn(buf[rows, :].astype(BF16), ctv) + dv * ug_ref[rows, :]
            return c

        lax.fori_loop(0, t // rc, ymm, 0, unroll=True)
        _rows_from_scan_order(y_ref, yn_ref, t)
        out.wait()

    return _call(
        body, name="s5_fwd", grid=(S5_BLOCKS,),
        in_specs=[pl.BlockSpec((t, 128), lambda i: (0, i)),
                  pl.BlockSpec((1, 128, 2 * ns), lambda i: (i, 0, 0)),
                  pl.BlockSpec((1, 2 * ns, 128), lambda i: (i, 0, 0)),
                  pl.BlockSpec((1, 1, ns), lambda i: (i, 0, 0)),
                  pl.BlockSpec((1, 1, ns), lambda i: (i, 0, 0)),
                  pl.BlockSpec((1, 128), lambda i: (0, i))],
        out_specs=[ANY, pl.BlockSpec((t, 128), lambda i: (0, i))],
        out_shape=[jax.ShapeDtypeStruct((S5_BLOCKS, t, 2 * ns), F32), jax.ShapeDtypeStruct((t, S5_WIDTH), F32)],
        scratch_shapes=[pltpu.VMEM((t, 2 * ns), F32), pltpu.VMEM((t, 128), F32), pltpu.VMEM((t, 128), F32),
                        pltpu.SemaphoreType.DMA(())],
        args=(ug, bd, ctd, ar4, ai4, dskip), carry=carry)


def _s5_bwd(dy, ug, xs, cd, bdt, ar4, ai4, dskip, carry=None):
    t = ug.shape[0]
    ls = t // S5_SEGS
    rc = min(512, t)
    ns = S5_BSTATE

    def body(dyn_ref, ugn_ref, xs_hbm, cd_ref, bdt_ref, ar_ref, ai_ref, d_ref,
             dugn_ref, dbd_ref, dcd_ref, dd_ref, dar_ref, dai_ref, xbuf, lam, dy_ref, ug_ref, dug_ref, sem):
        cb = pl.program_id(0)
        load = pltpu.make_async_copy(xs_hbm.at[cb], xbuf, sem)
        load.start()
        cdv = cd_ref[0]
        _rows_to_scan_order(dyn_ref, dy_ref, t)
        _rows_to_scan_order(ugn_ref, ug_ref, t)

        def mm(i, c):
            rows = pl.ds(pl.multiple_of(i * rc, rc), rc)
            lam[rows, :] = _nn(dy_ref[rows, :].astype(BF16), cdv)
            return c

        lax.fori_loop(0, t // rc, mm, 0, unroll=True)
        arb = jnp.broadcast_to(ar_ref[0], (S5_SEGS, ns))
        aib = jnp.broadcast_to(ai_ref[0], (S5_SEGS, ns))

        def lam_step(j, lr, li):
            rows = pl.ds(pl.multiple_of(j * S5_SEGS, S5_SEGS), S5_SEGS)
            nr = arb * lr + aib * li + lam[rows, 0:ns]
            ni = arb * li - aib * lr + lam[rows, ns:2 * ns]
            return rows, nr, ni

        def pass1(jj, c):
            _, nr, ni = lam_step(ls - 1 - jj, *c)
            return nr, ni

        zero = jnp.zeros((S5_SEGS, ns), F32)
        er, ei = lax.fori_loop(0, ls, pass1, (zero, zero))
        pr, pi = _cpow(arb, aib, ls)
        init = _chain_segments(er, ei, pr, -pi, down=False)
        load.wait()

        def accumulate(acc, nr, ni, xpr, xpi):
            return acc[0] + nr * xpr + ni * xpi, acc[1] + ni * xpr - nr * xpi

        def pass2(jj, c):
            lr, li, accr, acci = c
            j = ls - 1 - jj
            rows, nr, ni = lam_step(j, lr, li)
            lam[rows, 0:ns] = nr
            lam[rows, ns:2 * ns] = ni
            prev = pl.ds(pl.multiple_of((j - 1) * S5_SEGS, S5_SEGS), S5_SEGS)
            accr, acci = accumulate((accr, acci), nr, ni, xbuf[prev, 0:ns], xbuf[prev, ns:2 * ns])
            return nr, ni, accr, acci

        lr, li, accr, acci = lax.fori_loop(0, ls - 1, pass2, (init[0], init[1], zero, zero))
        rows, nr, ni = lam_step(0, lr, li)
        lam[rows, 0:ns] = nr
        lam[rows, ns:2 * ns] = ni
        last = pl.ds((ls - 1) * S5_SEGS, S5_SEGS)
        accr, acci = accumulate((accr, acci), nr, ni,
                                _shift_rows(xbuf[last, 0:ns], True), _shift_rows(xbuf[last, ns:2 * ns], True))
        dar_ref[0] = jnp.sum(accr, axis=0, keepdims=True)
        dai_ref[0] = jnp.sum(acci, axis=0, keepdims=True)

        bdtv = bdt_ref[0]
        dv = d_ref[...]
        dbd_ref[...] = jnp.zeros_like(dbd_ref)
        dcd_ref[...] = jnp.zeros_like(dcd_ref)
        dd_ref[...] = jnp.zeros_like(dd_ref)

        def tail(i, c):
            rows = pl.ds(pl.multiple_of(i * rc, rc), rc)
            dy = dy_ref[rows, :]
            ug = ug_ref[rows, :]
            lb = lam[rows, :].astype(BF16)
            dug_ref[rows, :] = _nn(lb, bdtv) + dv * dy
            dbd_ref[0] += _tn(ug.astype(BF16), lb)
            dcd_ref[0] += _tn(dy.astype(BF16), xbuf[rows, :].astype(BF16))
            dd_ref[...] += jnp.sum(dy * ug, axis=0, keepdims=True)
            return c

        lax.fori_loop(0, t // rc, tail, 0, unroll=True)
        _rows_from_scan_order(dug_ref, dugn_ref, t)

    chan = pl.BlockSpec((t, 128), lambda i: (0, i))
    dense = pl.BlockSpec((1, 128, 2 * ns), lambda i: (i, 0, 0))
    vec = pl.BlockSpec((1, 1, ns), lambda i: (i, 0, 0))
    return _call(
        body, name="s5_bwd", grid=(S5_BLOCKS,),
        in_specs=[chan, chan, ANY, dense, pl.BlockSpec((1, 2 * ns, 128), lambda i: (i, 0, 0)), vec, vec,
                  pl.BlockSpec((1, 128), lambda i: (0, i))],
        out_specs=[chan, dense, dense, pl.BlockSpec((1, 128), lambda i: (0, i)), vec, vec],
        out_shape=[jax.ShapeDtypeStruct((t, S5_WIDTH), F32),
                   jax.ShapeDtypeStruct((S5_BLOCKS, 128, 2 * ns), F32),
                   jax.ShapeDtypeStruct((S5_BLOCKS, 128, 2 * ns), F32),
                   jax.ShapeDtypeStruct((1, S5_WIDTH), F32),
                   jax.ShapeDtypeStruct((S5_BLOCKS, 1, ns), F32),
                   jax.ShapeDtypeStruct((S5_BLOCKS, 1, ns), F32)],
        scratch_shapes=[pltpu.VMEM((t, 2 * ns), F32), pltpu.VMEM((t, 2 * ns), F32)]
        + [pltpu.VMEM((t, 128), F32)] * 3 + [pltpu.SemaphoreType.DMA(())],
        args=(dy, ug, xs, cd, bdt, ar4, ai4, dskip), carry=carry)


def _cumsum_rows(x, reverse):
    c = x.shape[0]
    row = lax.broadcasted_iota(jnp.int32, x.shape, 0)
    d = 1
    while d < c:
        if reverse:
            x = x + jnp.where(row < c - d, pltpu.roll(x, c - d, 0), 0.0)
        else:
            x = x + jnp.where(row >= d, pltpu.roll(x, d, 0), 0.0)
        d *= 2
    return x


def _gla_common(q, k, alow, wup, bup):
    c = GLA_CHUNK
    pre = _nn(alow.astype(BF16), wup.astype(BF16)) + bup
    la = (jnp.minimum(pre, 0.0) - jnp.log(1.0 + jnp.exp(-jnp.abs(pre)))) * (1.0 / GLA_TAU)
    rr = lax.broadcasted_iota(jnp.int32, (c, c), 0)
    cc = lax.broadcasted_iota(jnp.int32, (c, c), 1)
    tril = (rr >= cc).astype(F32)
    bc = _cumsum_rows(la, reverse=False)
    bl = bc[c - 1:c, :]
    e_pos = jnp.exp(bc)
    e_neg = jnp.exp(-bc)
    e_end = jnp.exp(bl - bc)
    qt = q * (GLA_DK ** -0.5) * e_pos
    kt = k * e_neg
    ke = k * e_end
    lane = lax.broadcasted_iota(jnp.int32, (1, GLA_KEY), 1)
    masks = [((lane >= h * GLA_DK) & (lane < (h + 1) * GLA_DK)).astype(F32) for h in range(GLA_HEADS)]
    return dict(pre=pre, tril=tril, bc=bc, bl=bl, e_pos=e_pos, e_neg=e_neg, e_end=e_end,
                qt=qt, kt=kt, ke=ke, dec=jnp.exp(bl), masks=masks)


def _gla_fwd(q, k, v, alow, wup, bup, carry=None):
    t = q.shape[0]
    c = GLA_CHUNK
    n = t // c
    step = GLA_STEP_CHUNKS * c

    def body(q_ref, k_ref, v_ref, al_ref, wup_ref, bup_ref, o_ref, ss_ref, s_ref):
        i = pl.program_id(0)

        @pl.when(i == 0)
        def _():
            s_ref[...] = jnp.zeros_like(s_ref)

        wup_v, bup_v = wup_ref[...], bup_ref[...]
        s = s_ref[...]
        for j in range(GLA_STEP_CHUNKS):
            tok = slice(j * c, (j + 1) * c)
            m = _gla_common(q_ref[tok, :], k_ref[tok, :], al_ref[tok, :], wup_v, bup_v)
            ss_ref[j] = s
            sb = s.astype(BF16)
            ktb = m["kt"].astype(BF16)
            update = jnp.zeros_like(s)
            for h in range(GLA_HEADS):
                mask = m["masks"][h]
                qm = (m["qt"] * mask).astype(BF16)
                vh = v_ref[tok, h * GLA_DV:(h + 1) * GLA_DV].astype(BF16)
                p = (m["tril"] * _nt(qm, ktb)).astype(BF16)
                o_ref[tok, h * GLA_DV:(h + 1) * GLA_DV] = _nn(p, vh) + _nt(qm, sb)
                update = update + _tn(vh, (m["ke"] * mask).astype(BF16))
            s = m["dec"] * s + update
        s_ref[...] = s

    return _call(
        body, name="gla_fwd", grid=(t // step,),
        in_specs=[_row_tile(step, GLA_KEY), _row_tile(step, GLA_KEY), _row_tile(step, GLA_VAL),
                  _row_tile(step, GLA_RANK), VMEM_FULL, VMEM_FULL],
        out_specs=[_row_tile(step, GLA_VAL), pl.BlockSpec((GLA_STEP_CHUNKS, GLA_DV, GLA_KEY), lambda i: (i, 0, 0))],
        out_shape=[jax.ShapeDtypeStruct((t, GLA_VAL), F32), jax.ShapeDtypeStruct((n, GLA_DV, GLA_KEY), F32)],
        scratch_shapes=[pltpu.VMEM((GLA_DV, GLA_KEY), F32)],
        args=(q, k, v, alow, wup, bup), carry=carry)


def _gla_bwd(q, k, v, alow, wup, bup, ssave, do, carry=None):
    t = q.shape[0]
    c = GLA_CHUNK
    n = t // c

    def body(q_ref, k_ref, v_ref, al_ref, wup_ref, bup_ref, ss_ref, do_ref,
             dq_ref, dk_ref, dv_ref, dal_ref, dwup_ref, dbup_ref, ds_ref):
        i = pl.program_id(0)

        @pl.when(i == 0)
        def _():
            ds_ref[...] = jnp.zeros_like(ds_ref)
            dwup_ref[...] = jnp.zeros_like(dwup_ref)
            dbup_ref[...] = jnp.zeros_like(dbup_ref)

        wup_v, bup_v = wup_ref[...], bup_ref[...]
        ds_in = ds_ref[...]
        dwup = jnp.zeros((GLA_RANK, GLA_KEY), F32)
        dbup = jnp.zeros((1, GLA_KEY), F32)
        for j in reversed(range(GLA_STEP_CHUNKS)):
            tok = slice(j * c, (j + 1) * c)
            alow_v = al_ref[tok, :]
            m = _gla_common(q_ref[tok, :], k_ref[tok, :], alow_v, wup_v, bup_v)
            s = ss_ref[j]
            sb = s.astype(BF16)
            dsb = ds_in.astype(BF16)
            qt, kt, ke = m["qt"], m["kt"], m["ke"]
            ktb = kt.astype(BF16)
            dqt = jnp.zeros((c, GLA_KEY), F32)
            dkt = jnp.zeros((c, GLA_KEY), F32)
            dke = jnp.zeros((c, GLA_KEY), F32)
            update = jnp.zeros_like(ds_in)
            for h in range(GLA_HEADS):
                mask = m["masks"][h]
                qm = (qt * mask).astype(BF16)
                km = (kt * mask).astype(BF16)
                kem = (ke * mask).astype(BF16)
                cols = slice(h * GLA_DV, (h + 1) * GLA_DV)
                vh = v_ref[tok, cols].astype(BF16)
                doh = do_ref[tok, cols].astype(BF16)
                p = (m["tril"] * _nt(qm, ktb)).astype(BF16)
                dp = (m["tril"] * _nt(doh, vh)).astype(BF16)
                dv_ref[tok, cols] = _tn(p, doh) + _nt(kem, dsb)
                dqt = dqt + _nn(dp, km) + _nn(doh, sb) * mask
                dkt = dkt + _tn(dp, qm)
                dke = dke + _nn(vh, dsb) * mask
                update = update + _tn(doh, qm)
            ddec = jnp.sum(ds_in * s, axis=0, keepdims=True)
            dq_ref[tok, :] = dqt * m["e_pos"] * (GLA_DK ** -0.5)
            dk_ref[tok, :] = dkt * m["e_neg"] + dke * m["e_end"]
            dkeke = dke * ke
            dbl = jnp.sum(dkeke, axis=0, keepdims=True) + ddec * m["dec"]
            last = (lax.broadcasted_iota(jnp.int32, (c, 1), 0) == c - 1).astype(F32)
            dla = _cumsum_rows(dqt * qt - dkt * kt - dkeke + last * dbl, reverse=True)
            dpre = dla * (1.0 / GLA_TAU) * jax.nn.sigmoid(-m["pre"])
            dpb = dpre.astype(BF16)
            dal_ref[tok, :] = _nt(dpb, wup_v.astype(BF16))
            dwup = dwup + _tn(alow_v.astype(BF16), dpb)
            dbup = dbup + jnp.sum(dpre, axis=0, keepdims=True)
            ds_in = m["dec"] * ds_in + update
        ds_ref[...] = ds_in
        dwup_ref[...] += dwup
        dbup_ref[...] += dbup

    step = GLA_STEP_CHUNKS * c
    nsteps = t // step

    def rev(d):
        return pl.BlockSpec((step, d), lambda i: (nsteps - 1 - i, 0))

    return _call(
        body, name="gla_bwd", grid=(nsteps,),
        in_specs=[rev(GLA_KEY), rev(GLA_KEY), rev(GLA_VAL), rev(GLA_RANK), VMEM_FULL, VMEM_FULL,
                  pl.BlockSpec((GLA_STEP_CHUNKS, GLA_DV, GLA_KEY), lambda i: (nsteps - 1 - i, 0, 0)), rev(GLA_VAL)],
        out_specs=[rev(GLA_KEY), rev(GLA_KEY), rev(GLA_VAL), rev(GLA_RANK),
                   pl.BlockSpec((GLA_RANK, GLA_KEY), lambda i: (0, 0)), _acc_row(GLA_KEY)],
        out_shape=[jax.ShapeDtypeStruct((t, GLA_KEY), F32), jax.ShapeDtypeStruct((t, GLA_KEY), F32),
                   jax.ShapeDtypeStruct((t, GLA_VAL), F32), jax.ShapeDtypeStruct((t, GLA_RANK), F32),
                   jax.ShapeDtypeStruct((GLA_RANK, GLA_KEY), F32), jax.ShapeDtypeStruct((1, GLA_KEY), F32)],
        scratch_shapes=[pltpu.VMEM((GLA_DV, GLA_KEY), F32)],
        args=(q, k, v, alow, wup, bup, ssave, do), carry=carry)


def _post_math(y, o, r, gs5, ggla, wg, bg, gn, ps5t, pglat):
    y2 = y * y
    th = jnp.tanh(GELU_C0 * (y + GELU_C1 * y * y2))
    z5 = 0.5 * y * (1.0 + th)
    z5b = z5.astype(BF16)
    gate = jax.nn.sigmoid(_nn(z5b, wg) + bg)
    ys5 = z5 * gate
    rs, on = [], []
    for h in range(GLA_HEADS):
        oh = o[:, h * GLA_DV:(h + 1) * GLA_DV]
        rh = lax.rsqrt(jnp.mean(oh * oh, axis=-1, keepdims=True) + EPS)
        rs.append(rh)
        on.append(oh * rh)
    on = jnp.concatenate(on, axis=-1)
    sr = jax.nn.sigmoid(r)
    silu_r = r * sr
    ygla = on * gn * silu_r
    ys5b, yglab = ys5.astype(BF16), ygla.astype(BF16)
    m5 = _nt(ys5b, ps5t)
    mg = _nt(yglab, pglat)
    s5g, glag = jax.nn.sigmoid(gs5), jax.nn.sigmoid(ggla)
    merged = s5g * m5 + glag * mg
    return dict(y2=y2, th=th, z5=z5, z5b=z5b, gate=gate, ys5b=ys5b, yglab=yglab, rs=rs, on=on, sr=sr,
                silu_r=silu_r, m5=m5, mg=mg, s5g=s5g, glag=glag, mergedb=merged.astype(BF16))


def _mix_post_fwd(y, o, r, gs5, ggla, h1, wg, bg, gn, ps5t, pglat, wout, carry=None):
    t = o.shape[0]
    tm = _tile(t)

    def body(y_ref, o_ref, r_ref, gs5_ref, ggla_ref, h1_ref, wg_ref, bg_ref, gn_ref, ps_ref, pg_ref, wo_ref, h2_ref):
        m = _post_math(y_ref[...], o_ref[...], r_ref[...], gs5_ref[...], ggla_ref[...],
                       wg_ref[...], bg_ref[...], gn_ref[...], ps_ref[...], pg_ref[...])
        h2_ref[...] = h1_ref[...] + _nn(m["mergedb"], wo_ref[...])

    (h2,), landed = _call(
        body, name="mix_post_fwd", grid=(t // tm,),
        in_specs=[_row_tile(tm, 512)] * 3 + [_row_tile(tm, D_MODEL)] * 3
        + [VMEM_FULL, _acc_row(512), _acc_row(512), VMEM_FULL, VMEM_FULL, VMEM_FULL],
        out_specs=[_row_tile(tm, D_MODEL)],
        out_shape=[jax.ShapeDtypeStruct((t, D_MODEL), F32)],
        args=(y, o, r, gs5, ggla, h1, wg, bg, gn, ps5t, pglat, wout), carry=carry)
    return h2, landed


def _mix_post_bwd(y, o, r, gs5, ggla, dh2, wg, bg, gn, ps5t, pglat, wout, carry=None):
    t = o.shape[0]
    tm = _tile(t) // 2

    def body(y_ref, o_ref, r_ref, gs5_ref, ggla_ref, dh2_ref, wg_ref, bg_ref, gn_ref, ps_ref, pg_ref, wo_ref,
             dy_ref, do_ref, dr_ref, dgs5_ref, dggla_ref, dbg_ref, dgn_ref,
             z5b_ref, dgp_ref, ys5b_ref, dm5b_ref, yglab_ref, dmgb_ref, mergedb_ref, dh2b_ref):
        i = pl.program_id(0)
        yv, ov, rv = y_ref[...], o_ref[...], r_ref[...]
        wg, gn, ps5t, pglat = wg_ref[...], gn_ref[...], ps_ref[...], pg_ref[...]
        m = _post_math(yv, ov, rv, gs5_ref[...], ggla_ref[...], wg, bg_ref[...], gn, ps5t, pglat)
        dh2b = dh2_ref[...].astype(BF16)
        dmerged = _nt(dh2b, wo_ref[...])
        s5g, glag = m["s5g"], m["glag"]
        dgs5_ref[...] = dmerged * m["m5"] * s5g * (1.0 - s5g)
        dggla_ref[...] = dmerged * m["mg"] * glag * (1.0 - glag)
        dm5b = (dmerged * s5g).astype(BF16)
        dmgb = (dmerged * glag).astype(BF16)
        dys5 = _nn(dm5b, ps5t)
        dygla = _nn(dmgb, pglat)
        gate, z5, th = m["gate"], m["z5"], m["th"]
        dgpre = dys5 * z5 * gate * (1.0 - gate)
        dgpb = dgpre.astype(BF16)
        dz5 = dys5 * gate + _nt(dgpb, wg)
        dgelu = 0.5 * (1.0 + th) + 0.5 * yv * (1.0 - th * th) * GELU_C0 * (1.0 + 3.0 * GELU_C1 * m["y2"])
        dy_ref[...] = dz5 * dgelu
        on, sr, silu_r = m["on"], m["sr"], m["silu_r"]
        dr_ref[...] = dygla * on * gn * sr * (1.0 + rv * (1.0 - sr))
        dgn = jnp.sum(dygla * on * silu_r, axis=0, keepdims=True)
        don = dygla * gn * silu_r
        for h in range(GLA_HEADS):
            cols = slice(h * GLA_DV, (h + 1) * GLA_DV)
            donh, onh = don[:, cols], on[:, cols]
            do_ref[:, cols] = m["rs"][h] * (donh - onh * jnp.mean(donh * onh, axis=-1, keepdims=True))

        @pl.when(i == 0)
        def _():
            dbg_ref[...] = jnp.zeros_like(dbg_ref)
            dgn_ref[...] = jnp.zeros_like(dgn_ref)

        dbg_ref[...] += jnp.sum(dgpre, axis=0, keepdims=True)
        dgn_ref[...] += dgn
        z5b_ref[...] = m["z5b"]
        dgp_ref[...] = dgpb
        ys5b_ref[...] = m["ys5b"]
        dm5b_ref[...] = dm5b
        yglab_ref[...] = m["yglab"]
        dmgb_ref[...] = dmgb
        mergedb_ref[...] = m["mergedb"]
        dh2b_ref[...] = dh2b

    def f32(d):
        return jax.ShapeDtypeStruct((t, d), F32)

    def b16(d):
        return jax.ShapeDtypeStruct((t, d), BF16)

    widths = (512, 512, 512, 1024, 512, 1024, 1024, 1024)
    return _call(
        body, name="mix_post_bwd", grid=(t // tm,),
        in_specs=[_row_tile(tm, 512)] * 3 + [_row_tile(tm, D_MODEL)] * 3
        + [VMEM_FULL, _acc_row(512), _acc_row(512), VMEM_FULL, VMEM_FULL, VMEM_FULL],
        out_specs=[_row_tile(tm, 512)] * 3 + [_row_tile(tm, D_MODEL)] * 2
        + [_acc_row(512)] * 2 + [_row_tile(tm, w) for w in widths],
        out_shape=[f32(512)] * 3 + [f32(D_MODEL)] * 2
        + [jax.ShapeDtypeStruct((1, 512), F32)] * 2
        + [b16(w) for w in widths],
        args=(y, o, r, gs5, ggla, dh2, wg, bg, gn, ps5t, pglat, wout), carry=carry)


ADAM_TILE_ELEMS = 256 * 1024


def _adamw(w, g, m, v, name):
    rows, cols = w.shape
    tr = rows
    while tr * cols > ADAM_TILE_ELEMS and tr % 16 == 0:
        tr //= 2

    spec = pl.BlockSpec((tr, cols), lambda i: (i, 0))
    sh = jax.ShapeDtypeStruct((rows, cols), F32)
    return pl.pallas_call(functools.partial(_adamw_body), name=name, grid=(rows // tr,), in_specs=[spec] * 4,
                          out_specs=[spec] * 3, out_shape=[sh] * 3, compiler_params=_cparams(1))(w, g, m, v)


def _adamw_math(w, g, m, v):
    nm = ADAM_B1 * m + (1.0 - ADAM_B1) * g
    nv = ADAM_B2 * v + (1.0 - ADAM_B2) * (g * g)
    m_hat = nm / (1.0 - ADAM_B1 ** ADAM_STEP)
    v_hat = nv / (1.0 - ADAM_B2 ** ADAM_STEP)
    return -ADAM_LR * (m_hat / (jnp.sqrt(v_hat) + ADAM_EPS) + ADAM_WD * w), nm, nv


def _adamw_body(w_ref, g_ref, m_ref, v_ref, d_ref, nm_ref, nv_ref):
    d_ref[...], nm_ref[...], nv_ref[...] = _adamw_math(w_ref[...], g_ref[...], m_ref[...], v_ref[...])


SUM_ADAM_ROWS = 32


def _sum_adamw(landed, ws, ms, vs, name, carry=None):
    k = len(ws)
    n = landed[0].shape[0]
    r, c = ws[0].shape
    tr = SUM_ADAM_ROWS

    def body(*refs):
        lands, (w_refs, m_refs, v_refs), outs = refs[:k], (refs[k:2 * k], refs[2 * k:3 * k], refs[3 * k:4 * k]), refs[4 * k:]
        for i in range(k):
            g = lands[i][0].astype(F32)
            for s in range(1, n):
                g = g + lands[i][s].astype(F32)
            outs[i][...] = g
            outs[k + i][...], outs[2 * k + i][...], outs[3 * k + i][...] = _adamw_math(
                w_refs[i][...], g, m_refs[i][...], v_refs[i][...])

    row = pl.BlockSpec((tr, c), lambda i: (i, 0))
    return _call(
        body, name=name, grid=(r // tr,),
        in_specs=[pl.BlockSpec((n, tr, c), lambda i: (0, i, 0))] * k + [row] * (3 * k),
        out_specs=[row] * (4 * k), out_shape=[jax.ShapeDtypeStruct((r, c), F32)] * (4 * k),
        args=(*landed, *ws, *ms, *vs), carry=carry)


def _adamw_many(ws, gs, ms, vs, name):
    n = len(ws)

    def body(*refs):
        ins, outs = refs[:4 * n], refs[4 * n:]
        for i in range(n):
            _adamw_body(*(ins[j * n + i] for j in range(4)), *(outs[j * n + i] for j in range(3)))

    shapes = [jax.ShapeDtypeStruct(w.shape, F32) for w in ws]
    res = pl.pallas_call(body, name=name, in_specs=[VMEM_FULL] * (4 * n), out_specs=[VMEM_FULL] * (3 * n),
                         out_shape=shapes * 3)(*ws, *gs, *ms, *vs)
    return res[:n], res[n:2 * n], res[2 * n:]


def _exchange(carry, name):
    return _call(None, name=name, grid=(1,), in_specs=[], out_specs=[], out_shape=[], args=(), carry=carry)[1]


def _pair_add(slabs, from_pair, name):
    _, r, cols = slabs.shape

    def body(s_ref, p_ref, o_ref):
        c = lax.axis_index("c")
        mine = jnp.where(c == 0, s_ref[0, 0].astype(F32), s_ref[0, 1].astype(F32))
        o_ref[0] = (mine + p_ref[0].astype(F32)).astype(BF16)

    return pl.pallas_call(
        body, name=name, grid=(4,),
        in_specs=[pl.BlockSpec((1, 2, r, cols), lambda i: (i, 0, 0, 0)), pl.BlockSpec((1, r, cols), lambda i: (i, 0, 0))],
        out_specs=pl.BlockSpec((1, r, cols), lambda i: (i, 0, 0)),
        out_shape=jax.ShapeDtypeStruct((4, r, cols), BF16),
        compiler_params=_cparams(1),
    )(slabs.reshape(4, 2, r, cols), from_pair)


def _sum_slabs(slabs, name):
    n = slabs.shape[0]

    def body(s_ref, o_ref):
        acc = s_ref[0].astype(F32)
        for s in range(1, n):
            acc = acc + s_ref[s].astype(F32)
        o_ref[...] = acc

    return pl.pallas_call(
        body, name=name, in_specs=[VMEM_FULL], out_specs=VMEM_FULL,
        out_shape=jax.ShapeDtypeStruct(slabs.shape[1:], F32),
        compiler_params=pltpu.CompilerParams(vmem_limit_bytes=VMEM_LIMIT_BYTES),
    )(slabs)


BIG = ("ffn1_w1", "ffn1_w3", "ffn1_w2", "w_in", "s5_glu_w", "gla_a_up_w", "proj_s5", "proj_gla", "w_out",
       "ffn2_w1", "ffn2_w3", "ffn2_w2")
GROUPS = (("ffn1_w1", "ffn1_w3", "ffn1_w2"),
          ("w_in", "s5_glu_w", "gla_a_up_w", "proj_s5", "proj_gla", "w_out"),
          ("ffn2_w1", "ffn2_w3", "ffn2_w2"))
W_IN_ROWS = 514
W_IN_PAD = 528
UP_COLS = 32
ROW_ADAM = ("ffn1_w1", "ffn1_w3", "w_in", "ffn2_w1", "ffn2_w3")
COL_SHARDED = ("ffn1_w1", "ffn1_w3", "w_in", "proj_s5", "proj_gla", "ffn2_w1", "ffn2_w3")

SMALL = ("ffn1_norm", "mix_norm", "s5_lambda_re", "s5_lambda_im", "s5_log_dt", "s5_b_re", "s5_b_im", "s5_c_re",
         "s5_c_im", "s5_d", "s5_glu_b", "gla_a_up_b", "gla_out_norm", "ffn2_norm", "final_norm")
SMALL_SHAPES = dict(ffn1_norm=(1, 1024), mix_norm=(1, 1024), s5_lambda_re=(1, 32, 64), s5_lambda_im=(1, 32, 64),
                    s5_log_dt=(1, 32), s5_b_re=(1, 32, 64, 16), s5_b_im=(1, 32, 64, 16), s5_c_re=(1, 32, 16, 64),
                    s5_c_im=(1, 32, 16, 64), s5_d=(1, 32, 16), s5_glu_b=(1, 512), gla_a_up_b=(1, 256),
                    gla_out_norm=(1, 512), ffn2_norm=(1, 1024), final_norm=(1024,))
SMALL_N = sum(math.prod(s) for s in SMALL_SHAPES.values())
SMALL_R = -(-SMALL_N // (64 * 1024)) * 64


def _shard_rows(name, a):
    if name == "gla_a_up_w":
        return jnp.pad(a, ((0, 0), (0, 128 - UP_COLS)))
    if name in COL_SHARDED:
        a = a.T
    if name == "w_in":
        return jnp.pad(a, ((0, W_IN_PAD - W_IN_ROWS), (0, 0)))
    return a.reshape(-1, 1024)


def _unshard_rows(name, rows, shape):
    if name == "gla_a_up_w":
        return rows[:, :UP_COLS]
    if name == "w_in":
        rows = rows[:W_IN_ROWS]
    if name in COL_SHARDED:
        return rows.reshape(shape[1], shape[0]).T
    return rows.reshape(shape)


def _pack_small(vals, loss):
    flat = jnp.concatenate([vals[n].reshape(-1).astype(F32) for n in SMALL] + [loss.reshape(1)])
    return jnp.pad(flat, (0, SMALL_R * 1024 - SMALL_N - 1)).reshape(SMALL_R, 1024)


S5_B = ("s5_b_re", "s5_b_im")


def _working(name, a):
    return a[0].transpose(0, 2, 1) if name in S5_B else a


def _declared(name, a):
    return a.transpose(0, 2, 1)[None] if name in S5_B else a.reshape(SMALL_SHAPES[name])


def _unpack_small(slab):
    flat = slab.reshape(-1)
    out, off = {}, 0
    for n in SMALL:
        size = math.prod(SMALL_SHAPES[n])
        shape = (S5_GROUPS, S5_GROUP, S5_STATE) if n in S5_B else SMALL_SHAPES[n]
        out[n] = flat[off:off + size].reshape(shape)
        off += size
    return out


FULL_SHAPES = dict(w_in=(IN_COLS, D_MODEL), s5_glu_w=(S5_WIDTH, S5_WIDTH), gla_a_up_w=(GLA_RANK, GLA_KEY),
                   proj_s5=(D_MODEL, S5_WIDTH), proj_gla=(D_MODEL, GLA_VAL), w_out=(D_MODEL, D_MODEL))


def _full_weight(name, gathered):
    if name == "gla_a_up_w":
        return gathered[:, :, :UP_COLS].transpose(1, 0, 2).reshape(GLA_RANK, GLA_KEY)
    if name == "w_in":
        gathered = gathered[:, :W_IN_ROWS]
    return gathered.reshape(FULL_SHAPES.get(name, (D_FF, D_MODEL)))


def _grad_slabs(name, g):
    if name == "gla_a_up_w":
        g = g.reshape(GLA_RANK, N_DEV, UP_COLS).transpose(1, 0, 2)
        return jnp.pad(g, ((0, 0), (0, 0), (0, 128 - UP_COLS))).astype(BF16)
    if name == "w_in":
        return jnp.pad(g.reshape(N_DEV, W_IN_ROWS, D_MODEL), ((0, 0), (0, W_IN_PAD - W_IN_ROWS), (0, 0)))
    return g.reshape(N_DEV, -1, 1024)


def _s5_dense(re, im, sign_im):
    eye = jnp.eye(8, dtype=F32)

    def one(a):
        a = a.reshape(S5_BLOCKS, 8, S5_GROUP, S5_STATE)
        return jnp.einsum("cghp,gk->cghkp", a, eye).reshape(S5_BLOCKS, 128, S5_BSTATE)

    return jnp.concatenate([one(re), sign_im * one(im)], axis=-1)


def _s5_undense(d):
    eye = jnp.eye(8, dtype=F32)

    def one(a):
        a = a.reshape(S5_BLOCKS, 8, S5_GROUP, 8, S5_STATE)
        return jnp.einsum("cghkp,gk->cghp", a, eye).reshape(S5_GROUPS, S5_GROUP, S5_STATE)

    return one(d[..., :S5_BSTATE]), one(d[..., S5_BSTATE:])


def _local_step(x, target, p, w, rows=None, opt=None):
    w = dict(w or {})
    landed_grads = {}

    def gather(names):
        return None if rows is None else ([rows[n] for n in names], "gather")

    def gathered(names, landed):
        w.update({n: _full_weight(n, g) for n, g in zip(names, landed)})

    def scatter(names):
        return None if rows is None else ([_grad_slabs(n, big[n]) for n in names], "scatter")

    def scattered(names, landed):
        landed_grads.update(zip(names, landed))

    if rows is not None:
        gathered(GROUPS[0], _exchange(gather(GROUPS[0]), "gather_ffn1"))
    g1, gm, g2 = p["ffn1_norm"], p["mix_norm"], p["ffn2_norm"]
    gf = p["final_norm"].reshape(1, D_MODEL)
    lre, lim = p["s5_lambda_re"][0], p["s5_lambda_im"][0]
    ldt = p["s5_log_dt"][0].reshape(S5_GROUPS, 1)
    bre = p["s5_b_re"][0].transpose(0, 2, 1)
    bim = p["s5_b_im"][0].transpose(0, 2, 1)
    cre, cim = p["s5_c_re"][0], p["s5_c_im"][0]
    dskip = p["s5_d"][0].reshape(1, S5_WIDTH)
    bg, bup, gn = p["s5_glu_b"], p["gla_a_up_b"], p["gla_out_norm"]

    mix_first, mix_rest = ("w_in", "gla_a_up_w"), ("s5_glu_w", "proj_s5", "proj_gla", "w_out")
    h1, (a3_1, b3_1, n1), got = _ffn_fwd(x, g1, w["ffn1_w1"], w["ffn1_w3"], w["ffn1_w2"], "ffn1_fwd",
                                         gather(mix_first))
    gathered(mix_first, got)
    wup = w["gla_a_up_w"].astype(F32)
    (u, s5in, q, k, v, r, alow, gs5, ggla), got = _mix_pre_fwd(h1, gm, w["w_in"], gather(mix_rest))
    gathered(mix_rest, got)
    ar, ai, bbr, bbi = _s5_disc(lre, lim, ldt, bre, bim)
    bd = _s5_dense(bbr, bbi, 1.0)
    cd = _s5_dense(cre, cim, -1.0)
    bd16, cd16 = bd.astype(BF16), cd.astype(BF16)
    bdt16, ctd16 = bd16.transpose(0, 2, 1), cd16.transpose(0, 2, 1)
    ar4 = ar.reshape(S5_BLOCKS, 1, S5_BSTATE)
    ai4 = ai.reshape(S5_BLOCKS, 1, S5_BSTATE)
    (xs, y), got = _s5_fwd(s5in, bd16, ctd16, ar4, ai4, dskip, gather(GROUPS[2][:1]))
    gathered(GROUPS[2][:1], got)
    (o, ssave), got = _gla_fwd(q, k, v, alow, wup, bup, gather(GROUPS[2][1:2]))
    gathered(GROUPS[2][1:2], got)
    post_w = (w["s5_glu_w"], bg, gn, w["proj_s5"], w["proj_gla"], w["w_out"])
    h2, got = _mix_post_fwd(y, o, r, gs5, ggla, h1, *post_w, carry=gather(GROUPS[2][2:]))
    gathered(GROUPS[2][2:], got)
    loss, dh3, dgf, (a3_2, b3_2, n2) = _ffn_fwd_head(h2, g2, w["ffn2_w1"], w["ffn2_w3"], w["ffn2_w2"], gf, target,
                                                     "ffn2_fwd")

    big, small = {}, {}
    small["final_norm"] = dgf.reshape(D_MODEL)
    (dh2, dg2, da3, db3, s3, dhh2), _ = _ffn_bwd(
        h2, dh3, g2, a3_2, b3_2, w["ffn2_w1"], w["ffn2_w3"], w["ffn2_w2"], "ffn2_bwd")
    small["ffn2_norm"] = dg2
    big["ffn2_w1"] = _mm_tn(da3, n2, "ffn2_dw1")
    big["ffn2_w3"] = _mm_tn(db3, n2, "ffn2_dw3")
    big["ffn2_w2"] = _mm_tn(s3, dhh2, "ffn2_dw2")
    (dy, do, dr, dgs5, dggla, dbg, dgn, z5b, dgpb, ys5b, dm5b, yglab, dmgb, mergedb, dh2b), got = _mix_post_bwd(
        y, o, r, gs5, ggla, dh2, *post_w, carry=scatter(GROUPS[2][:1]))
    scattered(GROUPS[2][:1], got)
    small["s5_glu_b"] = dbg
    small["gla_out_norm"] = dgn
    big["s5_glu_w"] = _mm_tn(z5b, dgpb, "glu_dw")
    big["proj_s5"] = _mm_tn(dm5b, ys5b, "proj_s5_dw")
    big["proj_gla"] = _mm_tn(dmgb, yglab, "proj_gla_dw")
    big["w_out"] = _mm_tn(mergedb, dh2b, "w_out_dw")
    (dq, dk, dv, dalow, dwup, dbup), got = _gla_bwd(q, k, v, alow, wup, bup, ssave, do, scatter(GROUPS[2][1:2]))
    scattered(GROUPS[2][1:2], got)
    big["gla_a_up_w"] = dwup
    small["gla_a_up_b"] = dbup
    (ds5in, dbd, dcd, dd, dar4, dai4), got = _s5_bwd(
        dy, s5in, xs, cd16, bdt16, ar4, ai4, dskip, scatter(GROUPS[2][2:]))
    scattered(GROUPS[2][2:], got)
    dbbr, dbbi = _s5_undense(dbd)
    dcre, dcim_neg = _s5_undense(dcd)
    glre, glim, gldt, gbre, gbim = _s5_disc_bwd(
        lre, lim, ldt, bre, bim, dar4.reshape(S5_GROUPS, S5_STATE), dai4.reshape(S5_GROUPS, S5_STATE),
        dbbr, dbbi)
    small["s5_lambda_re"] = glre[None]
    small["s5_lambda_im"] = glim[None]
    small["s5_log_dt"] = gldt.reshape(1, S5_GROUPS)
    small["s5_b_re"] = gbre
    small["s5_b_im"] = gbim
    small["s5_c_re"] = dcre[None]
    small["s5_c_im"] = -dcim_neg[None]
    small["s5_d"] = dd.reshape(1, S5_GROUPS, S5_GROUP)
    dz = (ds5in, dq, dk, dv, dr, dalow, dgs5, dggla)
    (dh1, dgm), got = _mix_pre_bwd(h1, gm, w["w_in"], dh2, dz, scatter(mix_rest[:3]))
    scattered(mix_rest[:3], got)
    small["mix_norm"] = dgm
    big["w_in"] = jnp.concatenate([_mm_tn(d, u, "w_in_dw%d" % i) for i, d in enumerate(dz)], axis=0)
    (dx, dg1, da3, db3, s3, dhh1), got = _ffn_bwd(
        x, dh1, g1, a3_1, b3_1, w["ffn1_w1"], w["ffn1_w3"], w["ffn1_w2"], "ffn1_bwd",
        scatter(mix_first + mix_rest[3:]))
    scattered(mix_first + mix_rest[3:], got)
    small["ffn1_norm"] = dg1
    if rows is None:
        big["ffn1_w1"] = _mm_tn(da3, n1, "ffn1_dw1")
        big["ffn1_w3"] = _mm_tn(db3, n1, "ffn1_dw3")
        big["ffn1_w2"] = _mm_tn(s3, dhh1, "ffn1_dw2")
        return loss[0, 0], dx, big, small
    part = _pack_small(small, loss).reshape(N_DEV, SMALL_R // N_DEV, 1024)
    big["ffn1_w1"], (small_landed,) = _mm_tn(da3, n1, "ffn1_dw1", ([part], "scatter"))
    small_mine = _sum_slabs(small_landed, "sum_small")
    slabs1 = _grad_slabs("ffn1_w1", big["ffn1_w1"])
    big["ffn1_w3"], (from_pair, small_all) = _mm_tn(db3, n1, "ffn1_dw3",
                                                    [([slabs1], "pair"), ([small_mine], "gather")])
    small = small_all.reshape(SMALL_R, 1024)
    sums1 = _pair_add(slabs1, from_pair, "ffn1_w1_pair")
    slabs3 = _grad_slabs("ffn1_w3", big["ffn1_w3"])
    big["ffn1_w2"], (landed1, from_pair) = _mm_tn(s3, dhh1, "ffn1_dw2", [([sums1], "chips"), ([slabs3], "pair")])
    sums3 = _pair_add(slabs3, from_pair, "ffn1_w3_pair")
    slabs2 = _grad_slabs("ffn1_w2", big["ffn1_w2"])

    def sum_adamw(names, lands, name, carry=None):
        outs, got = _sum_adamw(lands, *([opt[n][j] for n in names] for j in range(3)), name, carry)
        for i, n in enumerate(names):
            updated[n] = outs[i::len(names)]
        return got

    updated = {}
    landed3, from_pair = sum_adamw(GROUPS[2], [landed_grads.pop(n) for n in GROUPS[2]], "adamw_ffn2",
                                   [([sums3], "chips"), ([slabs2], "pair")])
    sums2 = _pair_add(slabs2, from_pair, "ffn1_w2_pair")
    (landed2,) = _exchange(([sums2], "chips"), "scatter_ffn1_b")
    sum_adamw(GROUPS[0], [landed1, landed3, landed2], "adamw_ffn1")
    return loss[0, 0], dx, landed_grads, small, updated


NAMES = ("ffn1_norm", "ffn1_w1", "ffn1_w3", "ffn1_w2", "mix_norm", "w_in", "s5_lambda_re", "s5_lambda_im",
         "s5_log_dt", "s5_b_re", "s5_b_im", "s5_c_re", "s5_c_im", "s5_d", "s5_glu_w", "s5_glu_b", "gla_a_up_w",
         "gla_a_up_b", "gla_out_norm", "proj_s5", "proj_gla", "w_out", "ffn2_norm", "ffn2_w1", "ffn2_w3", "ffn2_w2",
         "final_norm")


def kernel(*args):
    nw = len(NAMES)
    x = args[0][0]
    wts = dict(zip(NAMES, args[1:1 + nw]))
    target = args[1 + nw][0]
    mom = dict(zip(NAMES, args[2 + nw:2 + 2 * nw]))
    var = dict(zip(NAMES, args[2 + 2 * nw:2 + 3 * nw]))

    shards = {n: wts[n][0] for n in BIG}
    rows = {n: _shard_rows(n, shards[n]).astype(BF16) for n in BIG}
    def row_layout(n, a):
        return a.T if n in ROW_ADAM else a

    opt = {n: tuple(row_layout(n, d[n][0]) for d in (wts, mom, var)) for n in GROUPS[0] + GROUPS[2]}
    _, dx, landed, small_slab, updated = _local_step(x, target, {n: wts[n] for n in SMALL}, None, rows, opt)
    loss = small_slab.reshape(-1)[SMALL_N]
    g_small = _unpack_small(small_slab)

    grad, delta, new_m, new_v = {}, {}, {}, {}
    for n, arrays in updated.items():
        grad[n], delta[n], new_m[n], new_v[n] = (row_layout(n, a)[None] for a in arrays)
    for n in GROUPS[1]:
        g_rows = _sum_slabs(landed[n], "sum_" + n)
        if n in ROW_ADAM:
            g = g_rows[:W_IN_ROWS] if n == "w_in" else g_rows
            outs = _adamw(shards[n].T, g, mom[n][0].T, var[n][0].T, "adamw_" + n)
            grad[n], delta[n], new_m[n], new_v[n] = (a.T[None] for a in (g, *outs))
        else:
            g = _unshard_rows(n, g_rows, shards[n].shape)
            outs = _adamw(shards[n], g, mom[n][0], var[n][0], "adamw_" + n)
            grad[n], delta[n], new_m[n], new_v[n] = (a[None] for a in (g, *outs))

    def flat2d(a):
        return a.reshape(-1, a.shape[-1])

    operands = ([flat2d(_working(n, d[n])) for n in SMALL] for d in (wts, mom, var))
    w2d, m2d, v2d = operands
    outs = _adamw_many(w2d, [flat2d(g_small[n]) for n in SMALL], m2d, v2d, "adamw_small")
    for out, arrays in zip((grad, delta, new_m, new_v), ([g_small[n] for n in SMALL], *outs)):
        out.update({n: _declared(n, a.reshape(g_small[n].shape)) for n, a in zip(SMALL, arrays)})
    return (loss, dx[None], *(d[n] for d in (grad, delta, new_m, new_v) for n in NAMES))
```

```python
import functools
import math

import jax
import jax.numpy as jnp
from jax import lax
from jax.experimental import pallas as pl
from jax.experimental.pallas import tpu as pltpu

F32, BF16 = jnp.float32, jnp.bfloat16
HIGHEST = lax.Precision.HIGHEST

D_MODEL = 1024
D_FF = 2816
N_DEV = 8
S5_WIDTH, S5_GROUPS, S5_GROUP, S5_STATE = 512, 32, 16, 64
S5_BLOCKS = 4
S5_BSTATE = 512
S5_SEGS = 8
GLA_HEADS, GLA_DK, GLA_DV = 4, 64, 128
GLA_KEY, GLA_VAL, GLA_RANK, GLA_CHUNK = 256, 512, 16, 64
GLA_TAU = 16.0
GLA_STEP_CHUNKS = 4
EPS = 1e-6
IN_SIZES = (512, 256, 256, 512, 512, 16, 1024, 1024)
IN_OFFS = tuple(sum(IN_SIZES[:i]) for i in range(len(IN_SIZES)))
IN_COLS = sum(IN_SIZES)
ADAM_LR, ADAM_B1, ADAM_B2, ADAM_EPS, ADAM_WD, ADAM_STEP = 0.001, 0.9, 0.999, 1e-08, 0.01, 10
GELU_C0 = math.sqrt(2.0 / math.pi)
GELU_C1 = 0.044715

FFN_FT = 256
VMEM_LIMIT_BYTES = 56 * 1024 * 1024

VMEM_FULL = pl.BlockSpec(memory_space=pltpu.VMEM)
ANY = pl.BlockSpec(memory_space=pl.ANY)


def _cparams(n_grid):
    return pltpu.CompilerParams(dimension_semantics=("arbitrary",) * n_grid, vmem_limit_bytes=VMEM_LIMIT_BYTES)


def _tile(t):
    return 512 if t >= 1024 else t // 2


def _nn(a, b):
    return jnp.dot(a, b, preferred_element_type=F32)


def _nt(a, b):
    return lax.dot_general(a, b, (((1,), (1,)), ((), ())), preferred_element_type=F32)


def _tn(a, b):
    return lax.dot_general(a, b, (((0,), (0,)), ((), ())), preferred_element_type=F32)


def _rms_parts(x):
    r = lax.rsqrt(jnp.mean(x * x, axis=-1, keepdims=True) + EPS)
    return x * r, r


def _rms_bwd(dn, g, xhat, r):
    dxh = dn * g
    dx = r * (dxh - xhat * jnp.mean(dxh * xhat, axis=-1, keepdims=True))
    return dx, jnp.sum(dn * xhat, axis=0, keepdims=True)


def _peers():
    x, y, c = lax.axis_index("x"), lax.axis_index("y"), lax.axis_index("c")
    out = []
    for k in range(1, N_DEV):
        px = 1 - x if k & 4 else x
        py = 1 - y if k & 2 else y
        pc = 1 - c if k & 1 else c
        out.append(((px, py, pc), 4 * px + 2 * py + pc))
    return 4 * x + 2 * y + c, out


def _exchange_copies(src_refs, out_refs, send_sems, recv_sems, local_sems, scatter, with_recvs):
    me, peers = _peers()
    locals_, sends, recvs = [], [], []
    for a, (src_ref, out_ref) in enumerate(zip(src_refs, out_refs)):
        def mine(idx, src_ref=src_ref):
            return src_ref.at[idx] if scatter else src_ref

        locals_.append(pltpu.make_async_copy(mine(me), out_ref.at[me], local_sems.at[a]))
        for k, (dev, idx) in enumerate(peers):
            sends.append(pltpu.make_async_remote_copy(
                src_ref=mine(idx), dst_ref=out_ref.at[me], send_sem=send_sems.at[a, k], recv_sem=recv_sems.at[a, k],
                device_id=dev, device_id_type=pl.DeviceIdType.MESH))
            if with_recvs:
                recvs.append(pltpu.make_async_remote_copy(
                    src_ref=mine(idx), dst_ref=out_ref.at[idx], send_sem=send_sems.at[a, k],
                    recv_sem=recv_sems.at[a, k], device_id=dev, device_id_type=pl.DeviceIdType.MESH))
    return locals_, sends, recvs


def _remote(src, dst, send_sems, recv_sems, a, k, dev):
    return pltpu.make_async_remote_copy(src_ref=src, dst_ref=dst, send_sem=send_sems.at[a, k],
                                        recv_sem=recv_sems.at[a, k], device_id=dev,
                                        device_id_type=pl.DeviceIdType.MESH)


def _gather_places():
    x, y, c = lax.axis_index("x"), lax.axis_index("y"), lax.axis_index("c")
    chips = [(1 - x, y), (x, 1 - y), (1 - x, 1 - y)]
    sibling = (x, y, 1 - c)
    me_idx, sib_idx = 4 * x + 2 * y + c, 4 * x + 2 * y + 1 - c
    same_core = [((cx, cy, c), 4 * cx + 2 * cy + c) for cx, cy in chips]
    other_core_idx = [4 * cx + 2 * cy + 1 - c for cx, cy in chips]
    return sibling, me_idx, sib_idx, same_core, other_core_idx


def _gather_start(src_refs, out_refs, send_sems, recv_sems, local_sems):
    sibling, me_idx, _, same_core, _ = _gather_places()
    for a, (src, out) in enumerate(zip(src_refs, out_refs)):
        pltpu.make_async_copy(src, out.at[me_idx], local_sems.at[a]).start()
        _remote(src, out.at[me_idx], send_sems, recv_sems, a, 0, sibling).start()
        for j, (dev, _) in enumerate(same_core):
            _remote(src, out.at[me_idx], send_sems, recv_sems, a, 1 + j, dev).start()


def _gather_forward(src_refs, out_refs, send_sems, recv_sems, local_sems):
    sibling, _, _, same_core, _ = _gather_places()
    for a, (src, out) in enumerate(zip(src_refs, out_refs)):
        for j, (dev, idx) in enumerate(same_core):
            _remote(src, out.at[idx], send_sems, recv_sems, a, 1 + j, dev).wait_recv()
            _remote(out.at[idx], out.at[idx], send_sems, recv_sems, a, 4 + j, sibling).start()


def _gather_finish(src_refs, out_refs, send_sems, recv_sems, local_sems):
    sibling, me_idx, sib_idx, same_core, other_core_idx = _gather_places()
    for a, (src, out) in enumerate(zip(src_refs, out_refs)):
        _remote(src, out.at[sib_idx], send_sems, recv_sems, a, 0, sibling).wait_recv()
        for j, idx in enumerate(other_core_idx):
            _remote(src, out.at[idx], send_sems, recv_sems, a, 4 + j, sibling).wait_recv()
        _remote(src, out.at[me_idx], send_sems, recv_sems, a, 0, sibling).wait_send()
        for j, (dev, idx) in enumerate(same_core):
            _remote(src, out.at[me_idx], send_sems, recv_sems, a, 1 + j, dev).wait_send()
            _remote(out.at[idx], out.at[idx], send_sems, recv_sems, a, 4 + j, sibling).wait_send()
        pltpu.make_async_copy(src, out.at[me_idx], local_sems.at[a]).wait()


def _exchange_start(*refs, scatter):
    locals_, sends, _ = _exchange_copies(*refs, scatter=scatter, with_recvs=False)
    for cp in locals_ + sends:
        cp.start()


def _exchange_wait(*refs, scatter):
    locals_, sends, recvs = _exchange_copies(*refs, scatter=scatter, with_recvs=True)
    for cp in recvs:
        cp.wait_recv()
    for cp in sends:
        cp.wait_send()
    for cp in locals_:
        cp.wait()


def _halves_places():
    x, y, c = lax.axis_index("x"), lax.axis_index("y"), lax.axis_index("c")
    flips = [(1 - x, y), (x, 1 - y), (1 - x, 1 - y)]
    return (x, y, 1 - c), c, 2 * x + y, [((fx, fy, c), 2 * fx + fy) for fx, fy in flips]


def _pair_start(src_refs, out_refs, send_sems, recv_sems, local_sems):
    sibling, c, _, _ = _halves_places()
    for a, (src, out) in enumerate(zip(src_refs, out_refs)):
        for i in range(4):
            _remote(src.at[2 * i + 1 - c], out.at[i], send_sems, recv_sems, a, i, sibling).start()


def _pair_finish(src_refs, out_refs, send_sems, recv_sems, local_sems):
    sibling, c, _, _ = _halves_places()
    for a, (src, out) in enumerate(zip(src_refs, out_refs)):
        for i in range(4):
            _remote(src.at[2 * i + 1 - c], out.at[i], send_sems, recv_sems, a, i, sibling).wait()


def _chips_start(src_refs, out_refs, send_sems, recv_sems, local_sems):
    _, _, chip, others = _halves_places()
    for a, (src, out) in enumerate(zip(src_refs, out_refs)):
        pltpu.make_async_copy(src.at[chip], out.at[chip], local_sems.at[a]).start()
        for k, (dev, their_chip) in enumerate(others):
            _remote(src.at[their_chip], out.at[chip], send_sems, recv_sems, a, k, dev).start()


def _chips_finish(src_refs, out_refs, send_sems, recv_sems, local_sems):
    _, _, chip, others = _halves_places()
    for a, (src, out) in enumerate(zip(src_refs, out_refs)):
        for k, (dev, their_chip) in enumerate(others):
            _remote(src.at[their_chip], out.at[their_chip], send_sems, recv_sems, a, k, dev).wait_recv()
        for k, (dev, their_chip) in enumerate(others):
            _remote(src.at[their_chip], out.at[chip], send_sems, recv_sems, a, k, dev).wait_send()
        pltpu.make_async_copy(src.at[chip], out.at[chip], local_sems.at[a]).wait()


EXCHANGES = {
    "gather": (_gather_start, _gather_forward, _gather_finish, N_DEV, False),
    "scatter": (functools.partial(_exchange_start, scatter=True), None, functools.partial(_exchange_wait, scatter=True),
                N_DEV, True),
    "pair": (_pair_start, None, _pair_finish, 4, True),
    "chips": (_chips_start, None, _chips_finish, 4, True),
}


def _exchange_sems(n_arrays):
    return [pltpu.SemaphoreType.DMA((n_arrays, N_DEV - 1)), pltpu.SemaphoreType.DMA((n_arrays, N_DEV - 1)),
            pltpu.SemaphoreType.DMA((n_arrays,))]


def _exchange_shapes(srcs, kind):
    lead, slabbed = EXCHANGES[kind][3:]
    return [jax.ShapeDtypeStruct((lead,) + tuple(s.shape[1:] if slabbed else s.shape), s.dtype) for s in srcs]


def _carries(carry):
    if carry is None:
        return []
    return [carry] if isinstance(carry, tuple) else list(carry)


def _call(body, *, name, grid, in_specs, out_specs, out_shape, args, scratch_shapes=(), carry=None):
    n_in, n_out, n_scr = len(in_specs), len(out_specs), len(scratch_shapes)
    groups = _carries(carry)
    sizes = [len(arrays) for arrays, _ in groups]
    nc = sum(sizes)

    def wrapped(*refs):
        ins, refs = refs[:n_in], refs[n_in:]
        csrc, refs = refs[:nc], refs[nc:]
        outs, refs = refs[:n_out], refs[n_out:]
        cland, refs = refs[:nc], refs[nc:]
        scr, sems = refs[:n_scr], refs[n_scr:]

        def run(phase):
            at = 0
            for gi, ((_, kind), size) in enumerate(zip(groups, sizes)):
                if EXCHANGES[kind][phase] is not None:
                    EXCHANGES[kind][phase](csrc[at:at + size], cland[at:at + size], *sems[3 * gi:3 * gi + 3])
                at += size

        last = pl.program_id(0) == grid[0] - 1
        if nc:
            pl.when(pl.program_id(0) == 0)(functools.partial(run, 0))
            pl.when(last)(functools.partial(run, 1))
        if body is not None:
            body(*ins, *outs, *scr)
        if nc:
            pl.when(last)(functools.partial(run, 2))

    res = pl.pallas_call(
        wrapped, name=name, grid=grid,
        in_specs=list(in_specs) + [ANY] * nc, out_specs=list(out_specs) + [ANY] * nc,
        out_shape=list(out_shape) + [s for arrays, kind in groups for s in _exchange_shapes(arrays, kind)],
        scratch_shapes=list(scratch_shapes) + [s for size in sizes for s in _exchange_sems(size)],
        compiler_params=_cparams(1),
    )(*args, *[a for arrays, _ in groups for a in arrays])
    return res[:n_out], res[n_out:]


def _row_tile(tm, d):
    return pl.BlockSpec((tm, d), lambda i: (i, 0))


def _acc_row(d):
    return pl.BlockSpec((1, d), lambda i: (0, 0))


def _ffn_body(x_ref, g_ref, w1_ref, w3_ref, w2_ref, acc_ref, a_ref, b_ref, n_ref):
    xv = x_ref[...]
    xhat, _ = _rms_parts(xv)
    n = (xhat * g_ref[...]).astype(BF16)
    n_ref[...] = n
    acc_ref[...] = xv

    def fstep(f, c):
        rows = pl.ds(pl.multiple_of(f * FFN_FT, FFN_FT), FFN_FT)
        a = _nt(n, w1_ref[rows, :])
        b = _nt(n, w3_ref[rows, :])
        a_ref[f] = a.astype(BF16)
        b_ref[f] = b.astype(BF16)
        s = (a * jax.nn.sigmoid(a) * b).astype(BF16)
        acc_ref[...] += 0.5 * _nn(s, w2_ref[rows, :])
        return c

    lax.fori_loop(0, D_FF // FFN_FT, fstep, 0, unroll=True)


def _ffn_fwd(x, g, w1t, w3t, w2, name, carry=None):
    t = x.shape[0]
    tm = _tile(t)
    nf = D_FF // FFN_FT
    blk3 = pl.BlockSpec((nf, tm, FFN_FT), lambda i: (0, i, 0))
    sh3 = jax.ShapeDtypeStruct((nf, t, FFN_FT), BF16)
    (h, a3, b3, n), landed = _call(
        functools.partial(_ffn_body), name=name, grid=(t // tm,),
        in_specs=[_row_tile(tm, D_MODEL), _acc_row(D_MODEL), VMEM_FULL, VMEM_FULL, VMEM_FULL],
        out_specs=[_row_tile(tm, D_MODEL), blk3, blk3, _row_tile(tm, D_MODEL)],
        out_shape=[jax.ShapeDtypeStruct((t, D_MODEL), F32), sh3, sh3, jax.ShapeDtypeStruct((t, D_MODEL), BF16)],
        args=(x, g, w1t, w3t, w2), carry=carry)
    return h, (a3, b3, n), landed


def _ffn_fwd_head(x, g, w1t, w3t, w2, gf, target, name):
    t = x.shape[0]
    tm = _tile(t)
    nf = D_FF // FFN_FT

    def body(x_ref, g_ref, w1_ref, w3_ref, w2_ref, gf_ref, t_ref, loss_ref, dh_ref, dgf_ref, a_ref, b_ref, n_ref, acc):
        _ffn_body(x_ref, g_ref, w1_ref, w3_ref, w2_ref, acc, a_ref, b_ref, n_ref)
        _head_math(acc[...], gf_ref[...], t_ref[...], loss_ref, dh_ref, dgf_ref)

    blk3 = pl.BlockSpec((nf, tm, FFN_FT), lambda i: (0, i, 0))
    sh3 = jax.ShapeDtypeStruct((nf, t, FFN_FT), BF16)
    (loss, dh, dgf, a3, b3, n), _ = _call(
        body, name=name, grid=(t // tm,),
        in_specs=[_row_tile(tm, D_MODEL), _acc_row(D_MODEL), VMEM_FULL, VMEM_FULL, VMEM_FULL, _acc_row(D_MODEL),
                  _row_tile(tm, D_MODEL)],
        out_specs=[pl.BlockSpec((1, 1), lambda i: (0, 0)), _row_tile(tm, D_MODEL), _acc_row(D_MODEL), blk3, blk3,
                   _row_tile(tm, D_MODEL)],
        out_shape=[jax.ShapeDtypeStruct((1, 1), F32), jax.ShapeDtypeStruct((t, D_MODEL), F32),
                   jax.ShapeDtypeStruct((1, D_MODEL), F32), sh3, sh3, jax.ShapeDtypeStruct((t, D_MODEL), BF16)],
        scratch_shapes=[pltpu.VMEM((tm, D_MODEL), F32)],
        args=(x, g, w1t, w3t, w2, gf, target))
    return loss, dh, dgf, (a3, b3, n)


def _head_math(h, gv, target, loss_ref, dh_ref, dg_ref):
    i = pl.program_id(0)
    xhat, r = _rms_parts(h)
    err = xhat * gv - target
    dx, dg = _rms_bwd(err * (1.0 / D_MODEL), gv, xhat, r)
    dh_ref[...] = dx

    @pl.when(i == 0)
    def _():
        loss_ref[...] = jnp.zeros_like(loss_ref)
        dg_ref[...] = jnp.zeros_like(dg_ref)

    loss_ref[...] += (0.5 / D_MODEL) * jnp.sum(jnp.sum(err * err, axis=1, keepdims=True), axis=0, keepdims=True)
    dg_ref[...] += dg


def _ffn_bwd(x, dh, g, a3, b3, w1t, w3t, w2, name, carry=None):
    t = x.shape[0]
    tm = _tile(t) // 2
    nf = D_FF // FFN_FT

    def body(x_ref, dh_ref, g_ref, a_ref, b_ref, w1_ref, w3_ref, w2_ref,
             dx_ref, dg_ref, da_ref, db_ref, s_ref, dhh_ref, dn_acc):
        i = pl.program_id(0)
        xv = x_ref[...]
        gv = g_ref[...]
        xhat, r = _rms_parts(xv)
        dhv = dh_ref[...]
        dhh = (0.5 * dhv).astype(BF16)
        dhh_ref[...] = dhh
        dn_acc[...] = jnp.zeros_like(dn_acc)

        def fstep(f, c):
            rows = pl.ds(pl.multiple_of(f * FFN_FT, FFN_FT), FFN_FT)
            w1c, w3c, w2c = w1_ref[rows, :], w3_ref[rows, :], w2_ref[rows, :]
            a = a_ref[f].astype(F32)
            b = b_ref[f].astype(F32)
            sg = jax.nn.sigmoid(a)
            sl = a * sg
            ds = _nt(dhh, w2c)
            da = (ds * b * sg * (1.0 + a * (1.0 - sg))).astype(BF16)
            db = (ds * sl).astype(BF16)
            s_ref[f] = (sl * b).astype(BF16)
            da_ref[f] = da
            db_ref[f] = db
            dn_acc[...] += _nn(da, w1c) + _nn(db, w3c)
            return c

        lax.fori_loop(0, nf, fstep, 0, unroll=True)
        dx, dg = _rms_bwd(dn_acc[...], gv, xhat, r)
        dx_ref[...] = dhv + dx

        @pl.when(i == 0)
        def _():
            dg_ref[...] = jnp.zeros_like(dg_ref)

        dg_ref[...] += dg

    blk3 = pl.BlockSpec((nf, tm, FFN_FT), lambda i: (0, i, 0))
    sh3 = jax.ShapeDtypeStruct((nf, t, FFN_FT), BF16)
    return _call(
        body, name=name, grid=(t // tm,),
        in_specs=[_row_tile(tm, D_MODEL), _row_tile(tm, D_MODEL), _acc_row(D_MODEL), blk3, blk3,
                  VMEM_FULL, VMEM_FULL, VMEM_FULL],
        out_specs=[_row_tile(tm, D_MODEL), _acc_row(D_MODEL), blk3, blk3, blk3, _row_tile(tm, D_MODEL)],
        out_shape=[jax.ShapeDtypeStruct((t, D_MODEL), F32), jax.ShapeDtypeStruct((1, D_MODEL), F32), sh3, sh3, sh3,
                   jax.ShapeDtypeStruct((t, D_MODEL), BF16)],
        scratch_shapes=[pltpu.VMEM((tm, D_MODEL), F32)],
        args=(x, dh, g, a3, b3, w1t, w3t, w2), carry=carry)


def _mm_tn(a, b, name, carry=None):
    t, n = b.shape
    kc = min(512, t)
    if a.ndim == 3:
        nb, _, tb = a.shape
        a_spec = pl.BlockSpec((1, t, tb), lambda i: (i, 0, 0))
    else:
        m = a.shape[1]
        tb = min(m, 256)
        nb = m // tb
        a_spec = pl.BlockSpec((t, tb), lambda i: (0, i))
    three_d = a.ndim == 3

    def body(a_ref, b_ref, o_ref, acc):
        acc[...] = jnp.zeros_like(acc)

        def kstep(k, c):
            rows = pl.ds(pl.multiple_of(k * kc, kc), kc)
            av = a_ref[0, rows, :] if three_d else a_ref[rows, :]
            acc[...] += _tn(av.astype(BF16), b_ref[rows, :])
            return c

        lax.fori_loop(0, t // kc, kstep, 0, unroll=True)
        o_ref[...] = acc[...].astype(BF16)

    (out,), landed = _call(
        body, name=name, grid=(nb,),
        in_specs=[a_spec, VMEM_FULL],
        out_specs=[pl.BlockSpec((tb, n), lambda i: (i, 0))],
        out_shape=[jax.ShapeDtypeStruct((nb * tb, n), BF16)],
        scratch_shapes=[pltpu.VMEM((tb, n), F32)],
        args=(a, b), carry=carry)
    return (out, landed) if carry is not None else out


MM_TB = 256


def _mm_tn_many(arrays, b, name):
    t, n = b.shape
    kc = min(512, t)
    counts = [a.shape[1] // MM_TB for a in arrays]
    starts = [sum(counts[:k]) for k in range(len(arrays))]

    def spec(start, count):
        return pl.BlockSpec((t, MM_TB), lambda i: (0, jnp.clip(i - start, 0, count - 1)))

    def body(*refs):
        a_refs, (b_ref, o_ref, acc) = refs[:len(arrays)], refs[len(arrays):]
        i = pl.program_id(0)
        for a_ref, start, count in zip(a_refs, starts, counts):
            @pl.when((i >= start) & (i < start + count))
            def _(a_ref=a_ref):
                acc[...] = jnp.zeros_like(acc)

                def kstep(k, c):
                    rows = pl.ds(pl.multiple_of(k * kc, kc), kc)
                    acc[...] += _tn(a_ref[rows, :].astype(BF16), b_ref[rows, :])
                    return c

                lax.fori_loop(0, t // kc, kstep, 0, unroll=True)
                o_ref[...] = acc[...].astype(BF16)

    return pl.pallas_call(
        body, name=name, grid=(sum(counts),),
        in_specs=[spec(s, c) for s, c in zip(starts, counts)] + [VMEM_FULL],
        out_specs=pl.BlockSpec((MM_TB, n), lambda i: (i, 0)),
        out_shape=jax.ShapeDtypeStruct((sum(counts) * MM_TB, n), BF16),
        scratch_shapes=[pltpu.VMEM((MM_TB, n), F32)],
        compiler_params=_cparams(1),
    )(*arrays, b)


def _mix_pre_fwd(h, g, wint, carry=None):
    t = h.shape[0]
    tm = _tile(t)

    def body(h_ref, g_ref, w_ref, u_ref, *outs):
        xhat, _ = _rms_parts(h_ref[...])
        u = (xhat * g_ref[...]).astype(BF16)
        u_ref[...] = u
        for o_ref, off, size in zip(outs, IN_OFFS, IN_SIZES):
            o_ref[...] = _nt(u, w_ref[off:off + size, :])

    return _call(
        body, name="mix_pre_fwd", grid=(t // tm,),
        in_specs=[_row_tile(tm, D_MODEL), _acc_row(D_MODEL), VMEM_FULL],
        out_specs=[_row_tile(tm, D_MODEL)] + [_row_tile(tm, s) for s in IN_SIZES],
        out_shape=[jax.ShapeDtypeStruct((t, D_MODEL), BF16)] + [jax.ShapeDtypeStruct((t, s), F32) for s in IN_SIZES],
        args=(h, g, wint), carry=carry)


def _mix_pre_bwd(h, g, wint, dh2, dz, carry=None):
    t = h.shape[0]
    tm = _tile(t)

    def body(h_ref, g_ref, w_ref, dh2_ref, *rest):
        dz_refs, (dh1_ref, dg_ref) = rest[:len(IN_SIZES)], rest[len(IN_SIZES):]
        i = pl.program_id(0)
        gv = g_ref[...]
        xhat, r = _rms_parts(h_ref[...])
        du = jnp.zeros((tm, D_MODEL), F32)
        for dz_ref, off, size in zip(dz_refs, IN_OFFS, IN_SIZES):
            du = du + _nn(dz_ref[...].astype(BF16), w_ref[off:off + size, :])
        dx, dg = _rms_bwd(du, gv, xhat, r)
        dh1_ref[...] = dh2_ref[...] + dx

        @pl.when(i == 0)
        def _():
            dg_ref[...] = jnp.zeros_like(dg_ref)

        dg_ref[...] += dg

    return _call(
        body, name="mix_pre_bwd", grid=(t // tm,),
        in_specs=[_row_tile(tm, D_MODEL), _acc_row(D_MODEL), VMEM_FULL, _row_tile(tm, D_MODEL)]
        + [_row_tile(tm, s) for s in IN_SIZES],
        out_specs=[_row_tile(tm, D_MODEL), _acc_row(D_MODEL)],
        out_shape=[jax.ShapeDtypeStruct((t, D_MODEL), F32), jax.ShapeDtypeStruct((1, D_MODEL), F32)],
        args=(h, g, wint, dh2, *dz), carry=carry)


def _disc_math(lre, lim, ldt, bre, bim):
    dt = jnp.exp(ldt)
    mag = jnp.exp(lre * dt)
    ar = mag * jnp.cos(lim * dt)
    ai = mag * jnp.sin(lim * dt)
    den = lre * lre + lim * lim
    nr = ar - 1.0
    fr = (nr * lre + ai * lim) / den
    fi = (ai * lre - nr * lim) / den
    fr, fi = fr[:, None, :], fi[:, None, :]
    return ar, ai, fr * bre - fi * bim, fr * bim + fi * bre


def _s5_disc(lre, lim, ldt, bre, bim):
    def body(lre_ref, lim_ref, ldt_ref, bre_ref, bim_ref, ar_ref, ai_ref, bbr_ref, bbi_ref):
        ar, ai, bbr, bbi = _disc_math(lre_ref[...], lim_ref[...], ldt_ref[...], bre_ref[...], bim_ref[...])
        ar_ref[...] = ar
        ai_ref[...] = ai
        bbr_ref[...] = bbr
        bbi_ref[...] = bbi

    small = jax.ShapeDtypeStruct(lre.shape, F32)
    big = jax.ShapeDtypeStruct(bre.shape, F32)
    return pl.pallas_call(body, name="s5_disc", out_shape=[small, small, big, big],
                          in_specs=[VMEM_FULL] * 5, out_specs=[VMEM_FULL] * 4)(lre, lim, ldt, bre, bim)


def _s5_disc_bwd(lre, lim, ldt, bre, bim, dar, dai, dbbr, dbbi):
    def body(lre_ref, lim_ref, ldt_ref, bre_ref, bim_ref, dar_ref, dai_ref, dbbr_ref, dbbi_ref,
             glre_ref, glim_ref, gldt_ref, gbre_ref, gbim_ref):
        _, vjp = jax.vjp(_disc_math, lre_ref[...], lim_ref[...], ldt_ref[...], bre_ref[...], bim_ref[...])
        glre, glim, gldt, gbre, gbim = vjp((dar_ref[...], dai_ref[...], dbbr_ref[...], dbbi_ref[...]))
        glre_ref[...] = glre
        glim_ref[...] = glim
        gldt_ref[...] = gldt
        gbre_ref[...] = gbre
        gbim_ref[...] = gbim

    small = jax.ShapeDtypeStruct(lre.shape, F32)
    big = jax.ShapeDtypeStruct(bre.shape, F32)
    return pl.pallas_call(body, name="s5_disc_bwd",
                          out_shape=[small, small, jax.ShapeDtypeStruct(ldt.shape, F32), big, big],
                          in_specs=[VMEM_FULL] * 9, out_specs=[VMEM_FULL] * 5,
                          )(lre, lim, ldt, bre, bim, dar, dai, dbbr, dbbi)


def _cmul(ar, ai, br, bi):
    return ar * br - ai * bi, ar * bi + ai * br


def _cpow(ar, ai, n):
    rr, ri = None, None
    pr, pi = ar, ai
    while n:
        if n & 1:
            rr, ri = (pr, pi) if rr is None else _cmul(rr, ri, pr, pi)
        n >>= 1
        if n:
            pr, pi = _cmul(pr, pi, pr, pi)
    return rr, ri


def _shift_rows(v, down):
    row = lax.broadcasted_iota(jnp.int32, v.shape, 0)
    if down:
        return jnp.where(row == 0, 0.0, pltpu.roll(v, 1, 0))
    return jnp.where(row == S5_SEGS - 1, 0.0, pltpu.roll(v, S5_SEGS - 1, 0))


def _chain_segments(er, ei, pr, pi, down):
    fr, fi = er, ei
    for _ in range(S5_SEGS - 1):
        sr, si = _shift_rows(fr, down), _shift_rows(fi, down)
        mr, mi = _cmul(pr, pi, sr, si)
        fr, fi = er + mr, ei + mi
    return _shift_rows(fr, down), _shift_rows(fi, down)


def _rows_to_scan_order(src_ref, dst_ref, t):
    ls = t // S5_SEGS

    def tile(j, c):
        dst_ref[pl.ds(pl.multiple_of(j * S5_SEGS, S5_SEGS), S5_SEGS), :] = src_ref[pl.ds(j, S5_SEGS, stride=ls), :]
        return c

    lax.fori_loop(0, ls, tile, 0, unroll=8)


def _rows_from_scan_order(src_ref, dst_ref, t):
    ls = t // S5_SEGS
    for s in range(S5_SEGS):
        def tile(jb, c, s=s):
            dst_ref[pl.ds(pl.multiple_of(s * ls + jb * 8, 8), 8), :] = (
                src_ref[pl.ds(jb * 8 * S5_SEGS + s, 8, stride=S5_SEGS), :])
            return c

        lax.fori_loop(0, ls // 8, tile, 0, unroll=8)


def _s5_fwd(ug, bd, ctd, ar4, ai4, dskip, carry=None):
    t = ug.shape[0]
    ls = t // S5_SEGS
    rc = min(512, t)
    ns = S5_BSTATE

    def body(ugn_ref, bd_ref, ct_ref, ar_ref, ai_ref, d_ref, xs_hbm, yn_ref, buf, ug_ref, y_ref, sem):
        cb = pl.program_id(0)
        bdv = bd_ref[0]
        _rows_to_scan_order(ugn_ref, ug_ref, t)

        def mm(i, c):
            rows = pl.ds(pl.multiple_of(i * rc, rc), rc)
            buf[rows, :] = _nn(ug_ref[rows, :].astype(BF16), bdv)
            return c

        lax.fori_loop(0, t // rc, mm, 0, unroll=True)
        arb = jnp.broadcast_to(ar_ref[0], (S5_SEGS, ns))
        aib = jnp.broadcast_to(ai_ref[0], (S5_SEGS, ns))

        def step(j, c, store):
            sr, si = c
            rows = pl.ds(pl.multiple_of(j * S5_SEGS, S5_SEGS), S5_SEGS)
            nr = arb * sr - aib * si + buf[rows, 0:ns]
            ni = arb * si + aib * sr + buf[rows, ns:2 * ns]
            if store:
                buf[rows, 0:ns] = nr
                buf[rows, ns:2 * ns] = ni
            return nr, ni

        zero = jnp.zeros((S5_SEGS, ns), F32)
        er, ei = lax.fori_loop(0, ls, functools.partial(step, store=False), (zero, zero))
        pr, pi = _cpow(arb, aib, ls)
        init = _chain_segments(er, ei, pr, pi, down=True)
        lax.fori_loop(0, ls, functools.partial(step, store=True), init)

        out = pltpu.make_async_copy(buf, xs_hbm.at[cb], sem)
        out.start()
        ctv = ct_ref[0]
        dv = d_ref[...]

        def ymm(i, c):
            rows = pl.ds(pl.multiple_of(i * rc, rc), rc)
            y_ref[rows, :] = _nn(buf[rows, :].astype(BF16), ctv) + dv * ug_ref[rows, :]
            return c

        lax.fori_loop(0, t // rc, ymm, 0, unroll=True)
        _rows_from_scan_order(y_ref, yn_ref, t)
        out.wait()

    return _call(
        body, name="s5_fwd", grid=(S5_BLOCKS,),
        in_specs=[pl.BlockSpec((t, 128), lambda i: (0, i)),
                  pl.BlockSpec((1, 128, 2 * ns), lambda i: (i, 0, 0)),
                  pl.BlockSpec((1, 2 * ns, 128), lambda i: (i, 0, 0)),
                  pl.BlockSpec((1, 1, ns), lambda i: (i, 0, 0)),
                  pl.BlockSpec((1, 1, ns), lambda i: (i, 0, 0)),
                  pl.BlockSpec((1, 128), lambda i: (0, i))],
        out_specs=[ANY, pl.BlockSpec((t, 128), lambda i: (0, i))],
        out_shape=[jax.ShapeDtypeStruct((S5_BLOCKS, t, 2 * ns), F32), jax.ShapeDtypeStruct((t, S5_WIDTH), F32)],
        scratch_shapes=[pltpu.VMEM((t, 2 * ns), F32), pltpu.VMEM((t, 128), F32), pltpu.VMEM((t, 128), F32),
                        pltpu.SemaphoreType.DMA(())],
        args=(ug, bd, ctd, ar4, ai4, dskip), carry=carry)


def _s5_bwd(dy, ug, xs, cd, bdt, ar4, ai4, dskip, carry=None):
    t = ug.shape[0]
    ls = t // S5_SEGS
    rc = min(512, t)
    ns = S5_BSTATE

    def body(dyn_ref, ugn_ref, xs_hbm, cd_ref, bdt_ref, ar_ref, ai_ref, d_ref,
             dugn_ref, dbd_ref, dcd_ref, dd_ref, dar_ref, dai_ref, xbuf, lam, dy_ref, ug_ref, dug_ref, sem):
        cb = pl.program_id(0)
        load = pltpu.make_async_copy(xs_hbm.at[cb], xbuf, sem)
        load.start()
        cdv = cd_ref[0]
        _rows_to_scan_order(dyn_ref, dy_ref, t)
        _rows_to_scan_order(ugn_ref, ug_ref, t)

        def mm(i, c):
            rows = pl.ds(pl.multiple_of(i * rc, rc), rc)
            lam[rows, :] = _nn(dy_ref[rows, :].astype(BF16), cdv)
            return c

        lax.fori_loop(0, t // rc, mm, 0, unroll=True)
        arb = jnp.broadcast_to(ar_ref[0], (S5_SEGS, ns))
        aib = jnp.broadcast_to(ai_ref[0], (S5_SEGS, ns))

        def lam_step(j, lr, li):
            rows = pl.ds(pl.multiple_of(j * S5_SEGS, S5_SEGS), S5_SEGS)
            nr = arb * lr + aib * li + lam[rows, 0:ns]
            ni = arb * li - aib * lr + lam[rows, ns:2 * ns]
            return rows, nr, ni

        def pass1(jj, c):
            _, nr, ni = lam_step(ls - 1 - jj, *c)
            return nr, ni

        zero = jnp.zeros((S5_SEGS, ns), F32)
        er, ei = lax.fori_loop(0, ls, pass1, (zero, zero))
        pr, pi = _cpow(arb, aib, ls)
        init = _chain_segments(er, ei, pr, -pi, down=False)
        load.wait()

        def accumulate(acc, nr, ni, xpr, xpi):
            return acc[0] + nr * xpr + ni * xpi, acc[1] + ni * xpr - nr * xpi

        def pass2(jj, c):
            lr, li, accr, acci = c
            j = ls - 1 - jj
            rows, nr, ni = lam_step(j, lr, li)
            lam[rows, 0:ns] = nr
            lam[rows, ns:2 * ns] = ni
            prev = pl.ds(pl.multiple_of((j - 1) * S5_SEGS, S5_SEGS), S5_SEGS)
            accr, acci = accumulate((accr, acci), nr, ni, xbuf[prev, 0:ns], xbuf[prev, ns:2 * ns])
            return nr, ni, accr, acci

        lr, li, accr, acci = lax.fori_loop(0, ls - 1, pass2, (init[0], init[1], zero, zero))
        rows, nr, ni = lam_step(0, lr, li)
        lam[rows, 0:ns] = nr
        lam[rows, ns:2 * ns] = ni
        last = pl.ds((ls - 1) * S5_SEGS, S5_SEGS)
        accr, acci = accumulate((accr, acci), nr, ni,
                                _shift_rows(xbuf[last, 0:ns], True), _shift_rows(xbuf[last, ns:2 * ns], True))
        dar_ref[0] = jnp.sum(accr, axis=0, keepdims=True)
        dai_ref[0] = jnp.sum(acci, axis=0, keepdims=True)

        bdtv = bdt_ref[0]
        dv = d_ref[...]
        dbd_ref[...] = jnp.zeros_like(dbd_ref)
        dcd_ref[...] = jnp.zeros_like(dcd_ref)
        dd_ref[...] = jnp.zeros_like(dd_ref)

        def tail(i, c):
            rows = pl.ds(pl.multiple_of(i * rc, rc), rc)
            dy = dy_ref[rows, :]
            ug = ug_ref[rows, :]
            lb = lam[rows, :].astype(BF16)
            dug_ref[rows, :] = _nn(lb, bdtv) + dv * dy
            dbd_ref[0] += _tn(ug.astype(BF16), lb)
            dcd_ref[0] += _tn(dy.astype(BF16), xbuf[rows, :].astype(BF16))
            dd_ref[...] += jnp.sum(dy * ug, axis=0, keepdims=True)
            return c

        lax.fori_loop(0, t // rc, tail, 0, unroll=True)
        _rows_from_scan_order(dug_ref, dugn_ref, t)

    chan = pl.BlockSpec((t, 128), lambda i: (0, i))
    dense = pl.BlockSpec((1, 128, 2 * ns), lambda i: (i, 0, 0))
    vec = pl.BlockSpec((1, 1, ns), lambda i: (i, 0, 0))
    return _call(
        body, name="s5_bwd", grid=(S5_BLOCKS,),
        in_specs=[chan, chan, ANY, dense, pl.BlockSpec((1, 2 * ns, 128), lambda i: (i, 0, 0)), vec, vec,
                  pl.BlockSpec((1, 128), lambda i: (0, i))],
        out_specs=[chan, dense, dense, pl.BlockSpec((1, 128), lambda i: (0, i)), vec, vec],
        out_shape=[jax.ShapeDtypeStruct((t, S5_WIDTH), F32),
                   jax.ShapeDtypeStruct((S5_BLOCKS, 128, 2 * ns), F32),
                   jax.ShapeDtypeStruct((S5_BLOCKS, 128, 2 * ns), F32),
                   jax.ShapeDtypeStruct((1, S5_WIDTH), F32),
                   jax.ShapeDtypeStruct((S5_BLOCKS, 1, ns), F32),
                   jax.ShapeDtypeStruct((S5_BLOCKS, 1, ns), F32)],
        scratch_shapes=[pltpu.VMEM((t, 2 * ns), F32), pltpu.VMEM((t, 2 * ns), F32)]
        + [pltpu.VMEM((t, 128), F32)] * 3 + [pltpu.SemaphoreType.DMA(())],
        args=(dy, ug, xs, cd, bdt, ar4, ai4, dskip), carry=carry)


def _cumsum_rows(x, reverse):
    c = x.shape[0]
    row = lax.broadcasted_iota(jnp.int32, x.shape, 0)
    d = 1
    while d < c:
        if reverse:
            x = x + jnp.where(row < c - d, pltpu.roll(x, c - d, 0), 0.0)
        else:
            x = x + jnp.where(row >= d, pltpu.roll(x, d, 0), 0.0)
        d *= 2
    return x


def _gla_common(q, k, alow, wup, bup):
    c = GLA_CHUNK
    pre = _nn(alow.astype(BF16), wup.astype(BF16)) + bup
    la = (jnp.minimum(pre, 0.0) - jnp.log(1.0 + jnp.exp(-jnp.abs(pre)))) * (1.0 / GLA_TAU)
    rr = lax.broadcasted_iota(jnp.int32, (c, c), 0)
    cc = lax.broadcasted_iota(jnp.int32, (c, c), 1)
    tril = (rr >= cc).astype(F32)
    bc = _cumsum_rows(la, reverse=False)
    bl = bc[c - 1:c, :]
    e_pos = jnp.exp(bc)
    e_neg = jnp.exp(-bc)
    e_end = jnp.exp(bl - bc)
    qt = q * (GLA_DK ** -0.5) * e_pos
    kt = k * e_neg
    ke = k * e_end
    lane = lax.broadcasted_iota(jnp.int32, (1, GLA_KEY), 1)
    masks = [((lane >= h * GLA_DK) & (lane < (h + 1) * GLA_DK)).astype(F32) for h in range(GLA_HEADS)]
    return dict(pre=pre, tril=tril, bc=bc, bl=bl, e_pos=e_pos, e_neg=e_neg, e_end=e_end,
                qt=qt, kt=kt, ke=ke, dec=jnp.exp(bl), masks=masks)


def _gla_fwd(q, k, v, alow, wup, bup, carry=None):
    t = q.shape[0]
    c = GLA_CHUNK
    n = t // c
    step = GLA_STEP_CHUNKS * c

    def body(q_ref, k_ref, v_ref, al_ref, wup_ref, bup_ref, o_ref, ss_ref, s_ref):
        i = pl.program_id(0)

        @pl.when(i == 0)
        def _():
            s_ref[...] = jnp.zeros_like(s_ref)

        wup_v, bup_v = wup_ref[...], bup_ref[...]
        s = s_ref[...]
        for j in range(GLA_STEP_CHUNKS):
            tok = slice(j * c, (j + 1) * c)
            m = _gla_common(q_ref[tok, :], k_ref[tok, :], al_ref[tok, :], wup_v, bup_v)
            ss_ref[j] = s
            sb = s.astype(BF16)
            ktb = m["kt"].astype(BF16)
            update = jnp.zeros_like(s)
            for h in range(GLA_HEADS):
                mask = m["masks"][h]
                qm = (m["qt"] * mask).astype(BF16)
                vh = v_ref[tok, h * GLA_DV:(h + 1) * GLA_DV].astype(BF16)
                p = (m["tril"] * _nt(qm, ktb)).astype(BF16)
                o_ref[tok, h * GLA_DV:(h + 1) * GLA_DV] = _nn(p, vh) + _nt(qm, sb)
                update = update + _tn(vh, (m["ke"] * mask).astype(BF16))
            s = m["dec"] * s + update
        s_ref[...] = s

    return _call(
        body, name="gla_fwd", grid=(t // step,),
        in_specs=[_row_tile(step, GLA_KEY), _row_tile(step, GLA_KEY), _row_tile(step, GLA_VAL),
                  _row_tile(step, GLA_RANK), VMEM_FULL, VMEM_FULL],
        out_specs=[_row_tile(step, GLA_VAL), pl.BlockSpec((GLA_STEP_CHUNKS, GLA_DV, GLA_KEY), lambda i: (i, 0, 0))],
        out_shape=[jax.ShapeDtypeStruct((t, GLA_VAL), F32), jax.ShapeDtypeStruct((n, GLA_DV, GLA_KEY), F32)],
        scratch_shapes=[pltpu.VMEM((GLA_DV, GLA_KEY), F32)],
        args=(q, k, v, alow, wup, bup), carry=carry)


def _gla_bwd(q, k, v, alow, wup, bup, ssave, do, carry=None):
    t = q.shape[0]
    c = GLA_CHUNK
    n = t // c

    def body(q_ref, k_ref, v_ref, al_ref, wup_ref, bup_ref, ss_ref, do_ref,
             dq_ref, dk_ref, dv_ref, dal_ref, dwup_ref, dbup_ref, ds_ref):
        i = pl.program_id(0)

        @pl.when(i == 0)
        def _():
            ds_ref[...] = jnp.zeros_like(ds_ref)
            dwup_ref[...] = jnp.zeros_like(dwup_ref)
            dbup_ref[...] = jnp.zeros_like(dbup_ref)

        wup_v, bup_v = wup_ref[...], bup_ref[...]
        ds_in = ds_ref[...]
        dwup = jnp.zeros((GLA_RANK, GLA_KEY), F32)
        dbup = jnp.zeros((1, GLA_KEY), F32)
        for j in reversed(range(GLA_STEP_CHUNKS)):
            tok = slice(j * c, (j + 1) * c)
            alow_v = al_ref[tok, :]
            m = _gla_common(q_ref[tok, :], k_ref[tok, :], alow_v, wup_v, bup_v)
            s = ss_ref[j]
            sb = s.astype(BF16)
            dsb = ds_in.astype(BF16)
            qt, kt, ke = m["qt"], m["kt"], m["ke"]
            ktb = kt.astype(BF16)
            dqt = jnp.zeros((c, GLA_KEY), F32)
            dkt = jnp.zeros((c, GLA_KEY), F32)
            dke = jnp.zeros((c, GLA_KEY), F32)
            update = jnp.zeros_like(ds_in)
            for h in range(GLA_HEADS):
                mask = m["masks"][h]
                qm = (qt * mask).astype(BF16)
                km = (kt * mask).astype(BF16)
                kem = (ke * mask).astype(BF16)
                cols = slice(h * GLA_DV, (h + 1) * GLA_DV)
                vh = v_ref[tok, cols].astype(BF16)
                doh = do_ref[tok, cols].astype(BF16)
                p = (m["tril"] * _nt(qm, ktb)).astype(BF16)
                dp = (m["tril"] * _nt(doh, vh)).astype(BF16)
                dv_ref[tok, cols] = (_tn(p, doh) + _nt(kem, dsb)).astype(BF16)
                dqt = dqt + _nn(dp, km) + _nn(doh, sb) * mask
                dkt = dkt + _tn(dp, qm)
                dke = dke + _nn(vh, dsb) * mask
                update = update + _tn(doh, qm)
            ddec = jnp.sum(ds_in * s, axis=0, keepdims=True)
            dq_ref[tok, :] = (dqt * m["e_pos"] * (GLA_DK ** -0.5)).astype(BF16)
            dk_ref[tok, :] = (dkt * m["e_neg"] + dke * m["e_end"]).astype(BF16)
            dkeke = dke * ke
            dbl = jnp.sum(dkeke, axis=0, keepdims=True) + ddec * m["dec"]
            last = (lax.broadcasted_iota(jnp.int32, (c, 1), 0) == c - 1).astype(F32)
            dla = _cumsum_rows(dqt * qt - dkt * kt - dkeke + last * dbl, reverse=True)
            dpre = dla * (1.0 / GLA_TAU) * jax.nn.sigmoid(-m["pre"])
            dpb = dpre.astype(BF16)
            dal_ref[tok, :] = _nt(dpb, wup_v.astype(BF16)).astype(BF16)
            dwup = dwup + _tn(alow_v.astype(BF16), dpb)
            dbup = dbup + jnp.sum(dpre, axis=0, keepdims=True)
            ds_in = m["dec"] * ds_in + update
        ds_ref[...] = ds_in
        dwup_ref[...] += dwup
        dbup_ref[...] += dbup

    step = GLA_STEP_CHUNKS * c
    nsteps = t // step

    def rev(d):
        return pl.BlockSpec((step, d), lambda i: (nsteps - 1 - i, 0))

    return _call(
        body, name="gla_bwd", grid=(nsteps,),
        in_specs=[rev(GLA_KEY), rev(GLA_KEY), rev(GLA_VAL), rev(GLA_RANK), VMEM_FULL, VMEM_FULL,
                  pl.BlockSpec((GLA_STEP_CHUNKS, GLA_DV, GLA_KEY), lambda i: (nsteps - 1 - i, 0, 0)), rev(GLA_VAL)],
        out_specs=[rev(GLA_KEY), rev(GLA_KEY), rev(GLA_VAL), rev(GLA_RANK),
                   pl.BlockSpec((GLA_RANK, GLA_KEY), lambda i: (0, 0)), _acc_row(GLA_KEY)],
        out_shape=[jax.ShapeDtypeStruct((t, GLA_KEY), BF16), jax.ShapeDtypeStruct((t, GLA_KEY), BF16),
                   jax.ShapeDtypeStruct((t, GLA_VAL), BF16), jax.ShapeDtypeStruct((t, GLA_RANK), BF16),
                   jax.ShapeDtypeStruct((GLA_RANK, GLA_KEY), F32), jax.ShapeDtypeStruct((1, GLA_KEY), F32)],
        scratch_shapes=[pltpu.VMEM((GLA_DV, GLA_KEY), F32)],
        args=(q, k, v, alow, wup, bup, ssave, do), carry=carry)


def _post_math(y, o, r, gs5, ggla, wg, bg, gn, ps5t, pglat):
    y2 = y * y
    th = jnp.tanh(GELU_C0 * (y + GELU_C1 * y * y2))
    z5 = 0.5 * y * (1.0 + th)
    z5b = z5.astype(BF16)
    gate = jax.nn.sigmoid(_nn(z5b, wg) + bg)
    ys5 = z5 * gate
    rs, on = [], []
    for h in range(GLA_HEADS):
        oh = o[:, h * GLA_DV:(h + 1) * GLA_DV]
        rh = lax.rsqrt(jnp.mean(oh * oh, axis=-1, keepdims=True) + EPS)
        rs.append(rh)
        on.append(oh * rh)
    on = jnp.concatenate(on, axis=-1)
    sr = jax.nn.sigmoid(r)
    silu_r = r * sr
    ygla = on * gn * silu_r
    ys5b, yglab = ys5.astype(BF16), ygla.astype(BF16)
    m5 = _nt(ys5b, ps5t)
    mg = _nt(yglab, pglat)
    s5g, glag = jax.nn.sigmoid(gs5), jax.nn.sigmoid(ggla)
    merged = s5g * m5 + glag * mg
    return dict(y2=y2, th=th, z5=z5, z5b=z5b, gate=gate, ys5b=ys5b, yglab=yglab, rs=rs, on=on, sr=sr,
                silu_r=silu_r, m5=m5, mg=mg, s5g=s5g, glag=glag, mergedb=merged.astype(BF16))


def _mix_post_fwd(y, o, r, gs5, ggla, h1, wg, bg, gn, ps5t, pglat, wout, carry=None):
    t = o.shape[0]
    tm = _tile(t)

    def body(y_ref, o_ref, r_ref, gs5_ref, ggla_ref, h1_ref, wg_ref, bg_ref, gn_ref, ps_ref, pg_ref, wo_ref, h2_ref):
        m = _post_math(y_ref[...], o_ref[...], r_ref[...], gs5_ref[...], ggla_ref[...],
                       wg_ref[...], bg_ref[...], gn_ref[...], ps_ref[...], pg_ref[...])
        h2_ref[...] = h1_ref[...] + _nn(m["mergedb"], wo_ref[...])

    (h2,), landed = _call(
        body, name="mix_post_fwd", grid=(t // tm,),
        in_specs=[_row_tile(tm, 512)] * 3 + [_row_tile(tm, D_MODEL)] * 3
        + [VMEM_FULL, _acc_row(512), _acc_row(512), VMEM_FULL, VMEM_FULL, VMEM_FULL],
        out_specs=[_row_tile(tm, D_MODEL)],
        out_shape=[jax.ShapeDtypeStruct((t, D_MODEL), F32)],
        args=(y, o, r, gs5, ggla, h1, wg, bg, gn, ps5t, pglat, wout), carry=carry)
    return h2, landed


def _mix_post_bwd(y, o, r, gs5, ggla, dh2, wg, bg, gn, ps5t, pglat, wout, carry=None):
    t = o.shape[0]
    tm = _tile(t) // 2

    def body(y_ref, o_ref, r_ref, gs5_ref, ggla_ref, dh2_ref, wg_ref, bg_ref, gn_ref, ps_ref, pg_ref, wo_ref,
             dy_ref, do_ref, dr_ref, dgs5_ref, dggla_ref, dbg_ref, dgn_ref,
             z5b_ref, dgp_ref, ys5b_ref, dm5b_ref, yglab_ref, dmgb_ref, mergedb_ref, dh2b_ref):
        i = pl.program_id(0)
        yv, ov, rv = y_ref[...], o_ref[...], r_ref[...]
        wg, gn, ps5t, pglat = wg_ref[...], gn_ref[...], ps_ref[...], pg_ref[...]
        m = _post_math(yv, ov, rv, gs5_ref[...], ggla_ref[...], wg, bg_ref[...], gn, ps5t, pglat)
        dh2b = dh2_ref[...].astype(BF16)
        dmerged = _nt(dh2b, wo_ref[...])
        s5g, glag = m["s5g"], m["glag"]
        dgs5_ref[...] = (dmerged * m["m5"] * s5g * (1.0 - s5g)).astype(BF16)
        dggla_ref[...] = (dmerged * m["mg"] * glag * (1.0 - glag)).astype(BF16)
        dm5b = (dmerged * s5g).astype(BF16)
        dmgb = (dmerged * glag).astype(BF16)
        dys5 = _nn(dm5b, ps5t)
        dygla = _nn(dmgb, pglat)
        gate, z5, th = m["gate"], m["z5"], m["th"]
        dgpre = dys5 * z5 * gate * (1.0 - gate)
        dgpb = dgpre.astype(BF16)
        dz5 = dys5 * gate + _nt(dgpb, wg)
        dgelu = 0.5 * (1.0 + th) + 0.5 * yv * (1.0 - th * th) * GELU_C0 * (1.0 + 3.0 * GELU_C1 * m["y2"])
        dy_ref[...] = dz5 * dgelu
        on, sr, silu_r = m["on"], m["sr"], m["silu_r"]
        dr_ref[...] = (dygla * on * gn * sr * (1.0 + rv * (1.0 - sr))).astype(BF16)
        dgn = jnp.sum(dygla * on * silu_r, axis=0, keepdims=True)
        don = dygla * gn * silu_r
        for h in range(GLA_HEADS):
            cols = slice(h * GLA_DV, (h + 1) * GLA_DV)
            donh, onh = don[:, cols], on[:, cols]
            do_ref[:, cols] = (m["rs"][h] * (donh - onh * jnp.mean(donh * onh, axis=-1, keepdims=True))).astype(BF16)

        @pl.when(i == 0)
        def _():
            dbg_ref[...] = jnp.zeros_like(dbg_ref)
            dgn_ref[...] = jnp.zeros_like(dgn_ref)

        dbg_ref[...] += jnp.sum(dgpre, axis=0, keepdims=True)
        dgn_ref[...] += dgn
        z5b_ref[...] = m["z5b"]
        dgp_ref[...] = dgpb
        ys5b_ref[...] = m["ys5b"]
        dm5b_ref[...] = dm5b
        yglab_ref[...] = m["yglab"]
        dmgb_ref[...] = dmgb
        mergedb_ref[...] = m["mergedb"]
        dh2b_ref[...] = dh2b

    def f32(d):
        return jax.ShapeDtypeStruct((t, d), F32)

    def b16(d):
        return jax.ShapeDtypeStruct((t, d), BF16)

    widths = (512, 512, 512, 1024, 512, 1024, 1024, 1024)
    return _call(
        body, name="mix_post_bwd", grid=(t // tm,),
        in_specs=[_row_tile(tm, 512)] * 3 + [_row_tile(tm, D_MODEL)] * 3
        + [VMEM_FULL, _acc_row(512), _acc_row(512), VMEM_FULL, VMEM_FULL, VMEM_FULL],
        out_specs=[_row_tile(tm, 512)] * 3 + [_row_tile(tm, D_MODEL)] * 2
        + [_acc_row(512)] * 2 + [_row_tile(tm, w) for w in widths],
        out_shape=[f32(512), b16(512), b16(512), b16(D_MODEL), b16(D_MODEL)]
        + [jax.ShapeDtypeStruct((1, 512), F32)] * 2
        + [b16(w) for w in widths],
        args=(y, o, r, gs5, ggla, dh2, wg, bg, gn, ps5t, pglat, wout), carry=carry)


ADAM_TILE_ELEMS = 256 * 1024


def _adamw(w, g, m, v, name):
    rows, cols = w.shape
    tr = rows
    while tr * cols > ADAM_TILE_ELEMS and tr % 16 == 0:
        tr //= 2

    spec = pl.BlockSpec((tr, cols), lambda i: (i, 0))
    sh = jax.ShapeDtypeStruct((rows, cols), F32)
    return pl.pallas_call(functools.partial(_adamw_body), name=name, grid=(rows // tr,), in_specs=[spec] * 4,
                          out_specs=[spec] * 3, out_shape=[sh] * 3, compiler_params=_cparams(1))(w, g, m, v)


def _adamw_math(w, g, m, v):
    nm = ADAM_B1 * m + (1.0 - ADAM_B1) * g
    nv = ADAM_B2 * v + (1.0 - ADAM_B2) * (g * g)
    m_hat = nm / (1.0 - ADAM_B1 ** ADAM_STEP)
    v_hat = nv / (1.0 - ADAM_B2 ** ADAM_STEP)
    return -ADAM_LR * (m_hat / (jnp.sqrt(v_hat) + ADAM_EPS) + ADAM_WD * w), nm, nv


def _adamw_body(w_ref, g_ref, m_ref, v_ref, d_ref, nm_ref, nv_ref):
    d_ref[...], nm_ref[...], nv_ref[...] = _adamw_math(w_ref[...], g_ref[...], m_ref[...], v_ref[...])


SUM_ADAM_ROWS = 32


def _sum_adamw(landed, ws, ms, vs, name, carry=None):
    k = len(ws)
    n = landed[0].shape[0]
    r, c = ws[0].shape
    tr = SUM_ADAM_ROWS

    def body(*refs):
        lands, (w_refs, m_refs, v_refs), outs = refs[:k], (refs[k:2 * k], refs[2 * k:3 * k], refs[3 * k:4 * k]), refs[4 * k:]
        for i in range(k):
            g = lands[i][0].astype(F32)
            for s in range(1, n):
                g = g + lands[i][s].astype(F32)
            outs[i][...] = g
            outs[k + i][...], outs[2 * k + i][...], outs[3 * k + i][...] = _adamw_math(
                w_refs[i][...], g, m_refs[i][...], v_refs[i][...])

    row = pl.BlockSpec((tr, c), lambda i: (i, 0))
    return _call(
        body, name=name, grid=(r // tr,),
        in_specs=[pl.BlockSpec((n, tr, c), lambda i: (0, i, 0))] * k + [row] * (3 * k),
        out_specs=[row] * (4 * k), out_shape=[jax.ShapeDtypeStruct((r, c), F32)] * (4 * k),
        args=(*landed, *ws, *ms, *vs), carry=carry)


def _adamw_many(ws, gs, ms, vs, name):
    n = len(ws)

    def body(*refs):
        ins, outs = refs[:4 * n], refs[4 * n:]
        for i in range(n):
            _adamw_body(*(ins[j * n + i] for j in range(4)), *(outs[j * n + i] for j in range(3)))

    shapes = [jax.ShapeDtypeStruct(w.shape, F32) for w in ws]
    res = pl.pallas_call(body, name=name, in_specs=[VMEM_FULL] * (4 * n), out_specs=[VMEM_FULL] * (3 * n),
                         out_shape=shapes * 3)(*ws, *gs, *ms, *vs)
    return res[:n], res[n:2 * n], res[2 * n:]


def _exchange(carry, name):
    return _call(None, name=name, grid=(1,), in_specs=[], out_specs=[], out_shape=[], args=(), carry=carry)[1]


def _pair_add(slabs, from_pair, name):
    _, r, cols = slabs.shape

    def body(s_ref, p_ref, o_ref):
        c = lax.axis_index("c")
        mine = jnp.where(c == 0, s_ref[0, 0].astype(F32), s_ref[0, 1].astype(F32))
        o_ref[0] = (mine + p_ref[0].astype(F32)).astype(BF16)

    return pl.pallas_call(
        body, name=name, grid=(4,),
        in_specs=[pl.BlockSpec((1, 2, r, cols), lambda i: (i, 0, 0, 0)), pl.BlockSpec((1, r, cols), lambda i: (i, 0, 0))],
        out_specs=pl.BlockSpec((1, r, cols), lambda i: (i, 0, 0)),
        out_shape=jax.ShapeDtypeStruct((4, r, cols), BF16),
        compiler_params=_cparams(1),
    )(slabs.reshape(4, 2, r, cols), from_pair)


def _sum_slabs(slabs, name):
    n = slabs.shape[0]

    def body(s_ref, o_ref):
        acc = s_ref[0].astype(F32)
        for s in range(1, n):
            acc = acc + s_ref[s].astype(F32)
        o_ref[...] = acc

    return pl.pallas_call(
        body, name=name, in_specs=[VMEM_FULL], out_specs=VMEM_FULL,
        out_shape=jax.ShapeDtypeStruct(slabs.shape[1:], F32),
        compiler_params=pltpu.CompilerParams(vmem_limit_bytes=VMEM_LIMIT_BYTES),
    )(slabs)


BIG = ("ffn1_w1", "ffn1_w3", "ffn1_w2", "w_in", "s5_glu_w", "gla_a_up_w", "proj_s5", "proj_gla", "w_out",
       "ffn2_w1", "ffn2_w3", "ffn2_w2")
GROUPS = (("ffn1_w1", "ffn1_w3", "ffn1_w2"),
          ("w_in", "s5_glu_w", "gla_a_up_w", "proj_s5", "proj_gla", "w_out"),
          ("ffn2_w1", "ffn2_w3", "ffn2_w2"))
W_IN_ROWS = 514
W_IN_PAD = 528
UP_COLS = 32
ROW_ADAM = ("ffn1_w1", "ffn1_w3", "w_in", "ffn2_w1", "ffn2_w3")
COL_SHARDED = ("ffn1_w1", "ffn1_w3", "w_in", "proj_s5", "proj_gla", "ffn2_w1", "ffn2_w3")

SMALL = ("ffn1_norm", "mix_norm", "s5_lambda_re", "s5_lambda_im", "s5_log_dt", "s5_b_re", "s5_b_im", "s5_c_re",
         "s5_c_im", "s5_d", "s5_glu_b", "gla_a_up_b", "gla_out_norm", "ffn2_norm", "final_norm")
SMALL_SHAPES = dict(ffn1_norm=(1, 1024), mix_norm=(1, 1024), s5_lambda_re=(1, 32, 64), s5_lambda_im=(1, 32, 64),
                    s5_log_dt=(1, 32), s5_b_re=(1, 32, 64, 16), s5_b_im=(1, 32, 64, 16), s5_c_re=(1, 32, 16, 64),
                    s5_c_im=(1, 32, 16, 64), s5_d=(1, 32, 16), s5_glu_b=(1, 512), gla_a_up_b=(1, 256),
                    gla_out_norm=(1, 512), ffn2_norm=(1, 1024), final_norm=(1024,))
SMALL_N = sum(math.prod(s) for s in SMALL_SHAPES.values())
SMALL_R = -(-SMALL_N // (64 * 1024)) * 64


def _shard_rows(name, a):
    if name == "gla_a_up_w":
        return jnp.pad(a, ((0, 0), (0, 128 - UP_COLS)))
    if name in COL_SHARDED:
        a = a.T
    if name == "w_in":
        return jnp.pad(a, ((0, W_IN_PAD - W_IN_ROWS), (0, 0)))
    return a.reshape(-1, 1024)


def _unshard_rows(name, rows, shape):
    if name == "gla_a_up_w":
        return rows[:, :UP_COLS]
    if name == "w_in":
        rows = rows[:W_IN_ROWS]
    if name in COL_SHARDED:
        return rows.reshape(shape[1], shape[0]).T
    return rows.reshape(shape)


def _pack_small(vals, loss):
    flat = jnp.concatenate([vals[n].reshape(-1).astype(F32) for n in SMALL] + [loss.reshape(1)])
    return jnp.pad(flat, (0, SMALL_R * 1024 - SMALL_N - 1)).reshape(SMALL_R, 1024)


S5_B = ("s5_b_re", "s5_b_im")


def _working(name, a):
    return a[0].transpose(0, 2, 1) if name in S5_B else a


def _declared(name, a):
    return a.transpose(0, 2, 1)[None] if name in S5_B else a.reshape(SMALL_SHAPES[name])


def _unpack_small(slab):
    flat = slab.reshape(-1)
    out, off = {}, 0
    for n in SMALL:
        size = math.prod(SMALL_SHAPES[n])
        shape = (S5_GROUPS, S5_GROUP, S5_STATE) if n in S5_B else SMALL_SHAPES[n]
        out[n] = flat[off:off + size].reshape(shape)
        off += size
    return out


FULL_SHAPES = dict(w_in=(IN_COLS, D_MODEL), s5_glu_w=(S5_WIDTH, S5_WIDTH), gla_a_up_w=(GLA_RANK, GLA_KEY),
                   proj_s5=(D_MODEL, S5_WIDTH), proj_gla=(D_MODEL, GLA_VAL), w_out=(D_MODEL, D_MODEL))


def _full_weight(name, gathered):
    if name == "gla_a_up_w":
        return gathered[:, :, :UP_COLS].transpose(1, 0, 2).reshape(GLA_RANK, GLA_KEY)
    if name == "w_in":
        gathered = gathered[:, :W_IN_ROWS]
    return gathered.reshape(FULL_SHAPES.get(name, (D_FF, D_MODEL)))


def _grad_slabs(name, g):
    if name == "gla_a_up_w":
        g = g.reshape(GLA_RANK, N_DEV, UP_COLS).transpose(1, 0, 2)
        return jnp.pad(g, ((0, 0), (0, 0), (0, 128 - UP_COLS))).astype(BF16)
    if name == "w_in":
        return jnp.pad(g.reshape(N_DEV, W_IN_ROWS, D_MODEL), ((0, 0), (0, W_IN_PAD - W_IN_ROWS), (0, 0)))
    return g.reshape(N_DEV, -1, 1024)


def _s5_dense(re, im, sign_im):
    eye = jnp.eye(8, dtype=F32)

    def one(a):
        a = a.reshape(S5_BLOCKS, 8, S5_GROUP, S5_STATE)
        return jnp.einsum("cghp,gk->cghkp", a, eye).reshape(S5_BLOCKS, 128, S5_BSTATE)

    return jnp.concatenate([one(re), sign_im * one(im)], axis=-1)


def _s5_undense(d):
    eye = jnp.eye(8, dtype=F32)

    def one(a):
        a = a.reshape(S5_BLOCKS, 8, S5_GROUP, 8, S5_STATE)
        return jnp.einsum("cghkp,gk->cghp", a, eye).reshape(S5_GROUPS, S5_GROUP, S5_STATE)

    return one(d[..., :S5_BSTATE]), one(d[..., S5_BSTATE:])


def _local_step(x, target, p, w, rows=None, opt=None):
    w = dict(w or {})
    landed_grads = {}

    def gather(names):
        return None if rows is None else ([rows[n] for n in names], "gather")

    def gathered(names, landed):
        w.update({n: _full_weight(n, g) for n, g in zip(names, landed)})

    def scatter(names):
        return None if rows is None else ([_grad_slabs(n, big[n]) for n in names], "scatter")

    def scattered(names, landed):
        landed_grads.update(zip(names, landed))

    if rows is not None:
        gathered(GROUPS[0], _exchange(gather(GROUPS[0]), "gather_ffn1"))
    g1, gm, g2 = p["ffn1_norm"], p["mix_norm"], p["ffn2_norm"]
    gf = p["final_norm"].reshape(1, D_MODEL)
    lre, lim = p["s5_lambda_re"][0], p["s5_lambda_im"][0]
    ldt = p["s5_log_dt"][0].reshape(S5_GROUPS, 1)
    bre = p["s5_b_re"][0].transpose(0, 2, 1)
    bim = p["s5_b_im"][0].transpose(0, 2, 1)
    cre, cim = p["s5_c_re"][0], p["s5_c_im"][0]
    dskip = p["s5_d"][0].reshape(1, S5_WIDTH)
    bg, bup, gn = p["s5_glu_b"], p["gla_a_up_b"], p["gla_out_norm"]

    mix_first, mix_rest = ("w_in", "gla_a_up_w"), ("s5_glu_w", "proj_s5", "proj_gla", "w_out")
    h1, (a3_1, b3_1, n1), got = _ffn_fwd(x, g1, w["ffn1_w1"], w["ffn1_w3"], w["ffn1_w2"], "ffn1_fwd",
                                         gather(mix_first))
    gathered(mix_first, got)
    wup = w["gla_a_up_w"].astype(F32)
    (u, s5in, q, k, v, r, alow, gs5, ggla), got = _mix_pre_fwd(h1, gm, w["w_in"], gather(mix_rest))
    gathered(mix_rest, got)
    ar, ai, bbr, bbi = _s5_disc(lre, lim, ldt, bre, bim)
    bd = _s5_dense(bbr, bbi, 1.0)
    cd = _s5_dense(cre, cim, -1.0)
    bd16, cd16 = bd.astype(BF16), cd.astype(BF16)
    bdt16, ctd16 = bd16.transpose(0, 2, 1), cd16.transpose(0, 2, 1)
    ar4 = ar.reshape(S5_BLOCKS, 1, S5_BSTATE)
    ai4 = ai.reshape(S5_BLOCKS, 1, S5_BSTATE)
    (xs, y), got = _s5_fwd(s5in, bd16, ctd16, ar4, ai4, dskip, gather(GROUPS[2][:1]))
    gathered(GROUPS[2][:1], got)
    (o, ssave), got = _gla_fwd(q, k, v, alow, wup, bup, gather(GROUPS[2][1:2]))
    gathered(GROUPS[2][1:2], got)
    post_w = (w["s5_glu_w"], bg, gn, w["proj_s5"], w["proj_gla"], w["w_out"])
    h2, got = _mix_post_fwd(y, o, r, gs5, ggla, h1, *post_w, carry=gather(GROUPS[2][2:]))
    gathered(GROUPS[2][2:], got)
    loss, dh3, dgf, (a3_2, b3_2, n2) = _ffn_fwd_head(h2, g2, w["ffn2_w1"], w["ffn2_w3"], w["ffn2_w2"], gf, target,
                                                     "ffn2_fwd")

    big, small = {}, {}
    small["final_norm"] = dgf.reshape(D_MODEL)
    (dh2, dg2, da3, db3, s3, dhh2), _ = _ffn_bwd(
        h2, dh3, g2, a3_2, b3_2, w["ffn2_w1"], w["ffn2_w3"], w["ffn2_w2"], "ffn2_bwd")
    small["ffn2_norm"] = dg2
    big["ffn2_w1"] = _mm_tn(da3, n2, "ffn2_dw1")
    big["ffn2_w3"] = _mm_tn(db3, n2, "ffn2_dw3")
    big["ffn2_w2"] = _mm_tn(s3, dhh2, "ffn2_dw2")
    (dy, do, dr, dgs5, dggla, dbg, dgn, z5b, dgpb, ys5b, dm5b, yglab, dmgb, mergedb, dh2b), got = _mix_post_bwd(
        y, o, r, gs5, ggla, dh2, *post_w, carry=scatter(GROUPS[2][:1]))
    scattered(GROUPS[2][:1], got)
    small["s5_glu_b"] = dbg
    small["gla_out_norm"] = dgn
    big["s5_glu_w"] = _mm_tn(z5b, dgpb, "glu_dw")
    big["proj_s5"] = _mm_tn(dm5b, ys5b, "proj_s5_dw")
    big["proj_gla"] = _mm_tn(dmgb, yglab, "proj_gla_dw")
    big["w_out"] = _mm_tn(mergedb, dh2b, "w_out_dw")
    (dq, dk, dv, dalow, dwup, dbup), got = _gla_bwd(q, k, v, alow, wup, bup, ssave, do, scatter(GROUPS[2][1:2]))
    scattered(GROUPS[2][1:2], got)
    big["gla_a_up_w"] = dwup
    small["gla_a_up_b"] = dbup
    (ds5in, dbd, dcd, dd, dar4, dai4), got = _s5_bwd(
        dy, s5in, xs, cd16, bdt16, ar4, ai4, dskip, scatter(GROUPS[2][2:]))
    scattered(GROUPS[2][2:], got)
    dbbr, dbbi = _s5_undense(dbd)
    dcre, dcim_neg = _s5_undense(dcd)
    glre, glim, gldt, gbre, gbim = _s5_disc_bwd(
        lre, lim, ldt, bre, bim, dar4.reshape(S5_GROUPS, S5_STATE), dai4.reshape(S5_GROUPS, S5_STATE),
        dbbr, dbbi)
    small["s5_lambda_re"] = glre[None]
    small["s5_lambda_im"] = glim[None]
    small["s5_log_dt"] = gldt.reshape(1, S5_GROUPS)
    small["s5_b_re"] = gbre
    small["s5_b_im"] = gbim
    small["s5_c_re"] = dcre[None]
    small["s5_c_im"] = -dcim_neg[None]
    small["s5_d"] = dd.reshape(1, S5_GROUPS, S5_GROUP)
    dz = (ds5in, dq, dk, dv, dr, dalow, dgs5, dggla)
    (dh1, dgm), got = _mix_pre_bwd(h1, gm, w["w_in"], dh2, dz, scatter(mix_rest[:3]))
    scattered(mix_rest[:3], got)
    small["mix_norm"] = dgm
    wide = _mm_tn_many(dz[:5] + dz[6:], u, "w_in_dw")
    low_at = IN_OFFS[5]
    big["w_in"] = jnp.concatenate([wide[:low_at], _mm_tn(dalow, u, "w_in_dw_low"), wide[low_at:]], axis=0)
    (dx, dg1, da3, db3, s3, dhh1), got = _ffn_bwd(
        x, dh1, g1, a3_1, b3_1, w["ffn1_w1"], w["ffn1_w3"], w["ffn1_w2"], "ffn1_bwd",
        scatter(mix_first + mix_rest[3:]))
    scattered(mix_first + mix_rest[3:], got)
    small["ffn1_norm"] = dg1
    if rows is None:
        big["ffn1_w1"] = _mm_tn(da3, n1, "ffn1_dw1")
        big["ffn1_w3"] = _mm_tn(db3, n1, "ffn1_dw3")
        big["ffn1_w2"] = _mm_tn(s3, dhh1, "ffn1_dw2")
        return loss[0, 0], dx, big, small
    part = _pack_small(small, loss).reshape(N_DEV, SMALL_R // N_DEV, 1024)
    big["ffn1_w1"], (small_landed,) = _mm_tn(da3, n1, "ffn1_dw1", ([part], "scatter"))
    small_mine = _sum_slabs(small_landed, "sum_small")
    slabs1 = _grad_slabs("ffn1_w1", big["ffn1_w1"])
    big["ffn1_w3"], (from_pair, small_all) = _mm_tn(db3, n1, "ffn1_dw3",
                                                    [([slabs1], "pair"), ([small_mine], "gather")])
    small = small_all.reshape(SMALL_R, 1024)
    sums1 = _pair_add(slabs1, from_pair, "ffn1_w1_pair")
    slabs3 = _grad_slabs("ffn1_w3", big["ffn1_w3"])
    big["ffn1_w2"], (landed1, from_pair) = _mm_tn(s3, dhh1, "ffn1_dw2", [([sums1], "chips"), ([slabs3], "pair")])
    sums3 = _pair_add(slabs3, from_pair, "ffn1_w3_pair")
    slabs2 = _grad_slabs("ffn1_w2", big["ffn1_w2"])

    def sum_adamw(names, lands, name, carry=None):
        outs, got = _sum_adamw(lands, *([opt[n][j] for n in names] for j in range(3)), name, carry)
        for i, n in enumerate(names):
            updated[n] = outs[i::len(names)]
        return got

    updated = {}
    landed3, from_pair = sum_adamw(GROUPS[2], [landed_grads.pop(n) for n in GROUPS[2]], "adamw_ffn2",
                                   [([sums3], "chips"), ([slabs2], "pair")])
    sums2 = _pair_add(slabs2, from_pair, "ffn1_w2_pair")
    (landed2,) = _exchange(([sums2], "chips"), "scatter_ffn1_b")
    sum_adamw(GROUPS[0], [landed1, landed3, landed2], "adamw_ffn1")
    return loss[0, 0], dx, landed_grads, small, updated


NAMES = ("ffn1_norm", "ffn1_w1", "ffn1_w3", "ffn1_w2", "mix_norm", "w_in", "s5_lambda_re", "s5_lambda_im",
         "s5_log_dt", "s5_b_re", "s5_b_im", "s5_c_re", "s5_c_im", "s5_d", "s5_glu_w", "s5_glu_b", "gla_a_up_w",
         "gla_a_up_b", "gla_out_norm", "proj_s5", "proj_gla", "w_out", "ffn2_norm", "ffn2_w1", "ffn2_w3", "ffn2_w2",
         "final_norm")


def kernel(*args):
    nw = len(NAMES)
    x = args[0][0]
    wts = dict(zip(NAMES, args[1:1 + nw]))
    target = args[1 + nw][0]
    mom = dict(zip(NAMES, args[2 + nw:2 + 2 * nw]))
    var = dict(zip(NAMES, args[2 + 2 * nw:2 + 3 * nw]))

    shards = {n: wts[n][0] for n in BIG}
    rows = {n: _shard_rows(n, shards[n]).astype(BF16) for n in BIG}
    def row_layout(n, a):
        return a.T if n in ROW_ADAM else a

    opt = {n: tuple(row_layout(n, d[n][0]) for d in (wts, mom, var)) for n in GROUPS[0] + GROUPS[2]}
    _, dx, landed, small_slab, updated = _local_step(x, target, {n: wts[n] for n in SMALL}, None, rows, opt)
    loss = small_slab.reshape(-1)[SMALL_N]
    g_small = _unpack_small(small_slab)

    grad, delta, new_m, new_v = {}, {}, {}, {}
    for n, arrays in updated.items():
        grad[n], delta[n], new_m[n], new_v[n] = (row_layout(n, a)[None] for a in arrays)
    for n in GROUPS[1]:
        g_rows = _sum_slabs(landed[n], "sum_" + n)
        if n in ROW_ADAM:
            g = g_rows[:W_IN_ROWS] if n == "w_in" else g_rows
            outs = _adamw(shards[n].T, g, mom[n][0].T, var[n][0].T, "adamw_" + n)
            grad[n], delta[n], new_m[n], new_v[n] = (a.T[None] for a in (g, *outs))
        else:
            g = _unshard_rows(n, g_rows, shards[n].shape)
            outs = _adamw(shards[n], g, mom[n][0], var[n][0], "adamw_" + n)
            grad[n], delta[n], new_m[n], new_v[n] = (a[None] for a in (g, *outs))

    def flat2d(a):
        return a.reshape(-1, a.shape[-1])

    operands = ([flat2d(_working(n, d[n])) for n in SMALL] for d in (wts, mom, var))
    w2d, m2d, v2d = operands
    outs = _adamw_many(w2d, [flat2d(g_small[n]) for n in SMALL], m2d, v2d, "adamw_small")
    for out, arrays in zip((grad, delta, new_m, new_v), ([g_small[n] for n in SMALL], *outs)):
        out.update({n: _declared(n, a.reshape(g_small[n].shape)) for n, a in zip(SMALL, arrays)})
    return (loss, dx[None], *(d[n] for d in (grad, delta, new_m, new_v) for n in NAMES))
```

```python
import functools
import math

import jax
import jax.numpy as jnp
from jax import lax
from jax.experimental import pallas as pl
from jax.experimental.pallas import tpu as pltpu

F32, BF16 = jnp.float32, jnp.bfloat16
HIGHEST = lax.Precision.HIGHEST

D_MODEL = 1024
D_FF = 2816
N_DEV = 8
S5_WIDTH, S5_GROUPS, S5_GROUP, S5_STATE = 512, 32, 16, 64
S5_BLOCKS = 4
S5_BSTATE = 512
S5_SEGS = 8
GLA_HEADS, GLA_DK, GLA_DV = 4, 64, 128
GLA_KEY, GLA_VAL, GLA_RANK, GLA_CHUNK = 256, 512, 16, 64
GLA_TAU = 16.0
GLA_STEP_CHUNKS = 8
EPS = 1e-6
IN_SIZES = (512, 256, 256, 512, 512, 16, 1024, 1024)
IN_OFFS = tuple(sum(IN_SIZES[:i]) for i in range(len(IN_SIZES)))
IN_COLS = sum(IN_SIZES)
ADAM_LR, ADAM_B1, ADAM_B2, ADAM_EPS, ADAM_WD, ADAM_STEP = 0.001, 0.9, 0.999, 1e-08, 0.01, 10
GELU_C0 = math.sqrt(2.0 / math.pi)
GELU_C1 = 0.044715

FFN_FT = 256
VMEM_LIMIT_BYTES = 56 * 1024 * 1024

VMEM_FULL = pl.BlockSpec(memory_space=pltpu.VMEM)
ANY = pl.BlockSpec(memory_space=pl.ANY)


def _cparams(n_grid):
    return pltpu.CompilerParams(dimension_semantics=("arbitrary",) * n_grid, vmem_limit_bytes=VMEM_LIMIT_BYTES)


def _tile(t):
    return 512 if t >= 1024 else t // 2


def _nn(a, b):
    return jnp.dot(a, b, preferred_element_type=F32)


def _nt(a, b):
    return lax.dot_general(a, b, (((1,), (1,)), ((), ())), preferred_element_type=F32)


def _tn(a, b):
    return lax.dot_general(a, b, (((0,), (0,)), ((), ())), preferred_element_type=F32)


def _rms_parts(x):
    r = lax.rsqrt(jnp.mean(x * x, axis=-1, keepdims=True) + EPS)
    return x * r, r


def _rms_bwd(dn, g, xhat, r):
    dxh = dn * g
    dx = r * (dxh - xhat * jnp.mean(dxh * xhat, axis=-1, keepdims=True))
    return dx, jnp.sum(dn * xhat, axis=0, keepdims=True)


def _peers():
    x, y, c = lax.axis_index("x"), lax.axis_index("y"), lax.axis_index("c")
    out = []
    for k in range(1, N_DEV):
        px = 1 - x if k & 4 else x
        py = 1 - y if k & 2 else y
        pc = 1 - c if k & 1 else c
        out.append(((px, py, pc), 4 * px + 2 * py + pc))
    return 4 * x + 2 * y + c, out


def _exchange_copies(src_refs, out_refs, send_sems, recv_sems, local_sems, scatter, with_recvs):
    me, peers = _peers()
    locals_, sends, recvs = [], [], []
    for a, (src_ref, out_ref) in enumerate(zip(src_refs, out_refs)):
        def mine(idx, src_ref=src_ref):
            return src_ref.at[idx] if scatter else src_ref

        locals_.append(pltpu.make_async_copy(mine(me), out_ref.at[me], local_sems.at[a]))
        for k, (dev, idx) in enumerate(peers):
            sends.append(pltpu.make_async_remote_copy(
                src_ref=mine(idx), dst_ref=out_ref.at[me], send_sem=send_sems.at[a, k], recv_sem=recv_sems.at[a, k],
                device_id=dev, device_id_type=pl.DeviceIdType.MESH))
            if with_recvs:
                recvs.append(pltpu.make_async_remote_copy(
                    src_ref=mine(idx), dst_ref=out_ref.at[idx], send_sem=send_sems.at[a, k],
                    recv_sem=recv_sems.at[a, k], device_id=dev, device_id_type=pl.DeviceIdType.MESH))
    return locals_, sends, recvs


def _remote(src, dst, send_sems, recv_sems, a, k, dev):
    return pltpu.make_async_remote_copy(src_ref=src, dst_ref=dst, send_sem=send_sems.at[a, k],
                                        recv_sem=recv_sems.at[a, k], device_id=dev,
                                        device_id_type=pl.DeviceIdType.MESH)


def _gather_places():
    x, y, c = lax.axis_index("x"), lax.axis_index("y"), lax.axis_index("c")
    chips = [(1 - x, y), (x, 1 - y), (1 - x, 1 - y)]
    sibling = (x, y, 1 - c)
    me_idx, sib_idx = 4 * x + 2 * y + c, 4 * x + 2 * y + 1 - c
    same_core = [((cx, cy, c), 4 * cx + 2 * cy + c) for cx, cy in chips]
    other_core_idx = [4 * cx + 2 * cy + 1 - c for cx, cy in chips]
    return sibling, me_idx, sib_idx, same_core, other_core_idx


def _gather_start(src_refs, out_refs, send_sems, recv_sems, local_sems):
    sibling, me_idx, _, same_core, _ = _gather_places()
    for a, (src, out) in enumerate(zip(src_refs, out_refs)):
        pltpu.make_async_copy(src, out.at[me_idx], local_sems.at[a]).start()
        _remote(src, out.at[me_idx], send_sems, recv_sems, a, 0, sibling).start()
        for j, (dev, _) in enumerate(same_core):
            _remote(src, out.at[me_idx], send_sems, recv_sems, a, 1 + j, dev).start()


def _gather_forward(src_refs, out_refs, send_sems, recv_sems, local_sems):
    sibling, _, _, same_core, _ = _gather_places()
    for a, (src, out) in enumerate(zip(src_refs, out_refs)):
        for j, (dev, idx) in enumerate(same_core):
            _remote(src, out.at[idx], send_sems, recv_sems, a, 1 + j, dev).wait_recv()
            _remote(out.at[idx], out.at[idx], send_sems, recv_sems, a, 4 + j, sibling).start()


def _gather_finish(src_refs, out_refs, send_sems, recv_sems, local_sems):
    sibling, me_idx, sib_idx, same_core, other_core_idx = _gather_places()
    for a, (src, out) in enumerate(zip(src_refs, out_refs)):
        _remote(src, out.at[sib_idx], send_sems, recv_sems, a, 0, sibling).wait_recv()
        for j, idx in enumerate(other_core_idx):
            _remote(src, out.at[idx], send_sems, recv_sems, a, 4 + j, sibling).wait_recv()
        _remote(src, out.at[me_idx], send_sems, recv_sems, a, 0, sibling).wait_send()
        for j, (dev, idx) in enumerate(same_core):
            _remote(src, out.at[me_idx], send_sems, recv_sems, a, 1 + j, dev).wait_send()
            _remote(out.at[idx], out.at[idx], send_sems, recv_sems, a, 4 + j, sibling).wait_send()
        pltpu.make_async_copy(src, out.at[me_idx], local_sems.at[a]).wait()


def _exchange_start(*refs, scatter):
    locals_, sends, _ = _exchange_copies(*refs, scatter=scatter, with_recvs=False)
    for cp in locals_ + sends:
        cp.start()


def _exchange_wait(*refs, scatter):
    locals_, sends, recvs = _exchange_copies(*refs, scatter=scatter, with_recvs=True)
    for cp in recvs:
        cp.wait_recv()
    for cp in sends:
        cp.wait_send()
    for cp in locals_:
        cp.wait()


def _halves_places():
    x, y, c = lax.axis_index("x"), lax.axis_index("y"), lax.axis_index("c")
    flips = [(1 - x, y), (x, 1 - y), (1 - x, 1 - y)]
    return (x, y, 1 - c), c, 2 * x + y, [((fx, fy, c), 2 * fx + fy) for fx, fy in flips]


def _pair_start(src_refs, out_refs, send_sems, recv_sems, local_sems):
    sibling, c, _, _ = _halves_places()
    for a, (src, out) in enumerate(zip(src_refs, out_refs)):
        for i in range(4):
            _remote(src.at[2 * i + 1 - c], out.at[i], send_sems, recv_sems, a, i, sibling).start()


def _pair_finish(src_refs, out_refs, send_sems, recv_sems, local_sems):
    sibling, c, _, _ = _halves_places()
    for a, (src, out) in enumerate(zip(src_refs, out_refs)):
        for i in range(4):
            _remote(src.at[2 * i + 1 - c], out.at[i], send_sems, recv_sems, a, i, sibling).wait()


def _chips_start(src_refs, out_refs, send_sems, recv_sems, local_sems):
    _, _, chip, others = _halves_places()
    for a, (src, out) in enumerate(zip(src_refs, out_refs)):
        pltpu.make_async_copy(src.at[chip], out.at[chip], local_sems.at[a]).start()
        for k, (dev, their_chip) in enumerate(others):
            _remote(src.at[their_chip], out.at[chip], send_sems, recv_sems, a, k, dev).start()


def _chips_finish(src_refs, out_refs, send_sems, recv_sems, local_sems):
    _, _, chip, others = _halves_places()
    for a, (src, out) in enumerate(zip(src_refs, out_refs)):
        for k, (dev, their_chip) in enumerate(others):
            _remote(src.at[their_chip], out.at[their_chip], send_sems, recv_sems, a, k, dev).wait_recv()
        for k, (dev, their_chip) in enumerate(others):
            _remote(src.at[their_chip], out.at[chip], send_sems, recv_sems, a, k, dev).wait_send()
        pltpu.make_async_copy(src.at[chip], out.at[chip], local_sems.at[a]).wait()


EXCHANGES = {
    "gather": (_gather_start, _gather_forward, _gather_finish, N_DEV, False),
    "scatter": (functools.partial(_exchange_start, scatter=True), None, functools.partial(_exchange_wait, scatter=True),
                N_DEV, True),
    "pair": (_pair_start, None, _pair_finish, 4, True),
    "chips": (_chips_start, None, _chips_finish, 4, True),
}


def _exchange_sems(n_arrays):
    return [pltpu.SemaphoreType.DMA((n_arrays, N_DEV - 1)), pltpu.SemaphoreType.DMA((n_arrays, N_DEV - 1)),
            pltpu.SemaphoreType.DMA((n_arrays,))]


def _exchange_shapes(srcs, kind):
    lead, slabbed = EXCHANGES[kind][3:]
    return [jax.ShapeDtypeStruct((lead,) + tuple(s.shape[1:] if slabbed else s.shape), s.dtype) for s in srcs]


def _carries(carry):
    if carry is None:
        return []
    return [carry] if isinstance(carry, tuple) else list(carry)


def _call(body, *, name, grid, in_specs, out_specs, out_shape, args, scratch_shapes=(), carry=None):
    n_in, n_out, n_scr = len(in_specs), len(out_specs), len(scratch_shapes)
    groups = _carries(carry)
    sizes = [len(arrays) for arrays, _ in groups]
    nc = sum(sizes)

    def wrapped(*refs):
        ins, refs = refs[:n_in], refs[n_in:]
        csrc, refs = refs[:nc], refs[nc:]
        outs, refs = refs[:n_out], refs[n_out:]
        cland, refs = refs[:nc], refs[nc:]
        scr, sems = refs[:n_scr], refs[n_scr:]

        def run(phase):
            at = 0
            for gi, ((_, kind), size) in enumerate(zip(groups, sizes)):
                if EXCHANGES[kind][phase] is not None:
                    EXCHANGES[kind][phase](csrc[at:at + size], cland[at:at + size], *sems[3 * gi:3 * gi + 3])
                at += size

        last = pl.program_id(0) == grid[0] - 1
        if nc:
            pl.when(pl.program_id(0) == 0)(functools.partial(run, 0))
            pl.when(last)(functools.partial(run, 1))
        if body is not None:
            body(*ins, *outs, *scr)
        if nc:
            pl.when(last)(functools.partial(run, 2))

    res = pl.pallas_call(
        wrapped, name=name, grid=grid,
        in_specs=list(in_specs) + [ANY] * nc, out_specs=list(out_specs) + [ANY] * nc,
        out_shape=list(out_shape) + [s for arrays, kind in groups for s in _exchange_shapes(arrays, kind)],
        scratch_shapes=list(scratch_shapes) + [s for size in sizes for s in _exchange_sems(size)],
        compiler_params=_cparams(1),
    )(*args, *[a for arrays, _ in groups for a in arrays])
    return res[:n_out], res[n_out:]


def _row_tile(tm, d):
    return pl.BlockSpec((tm, d), lambda i: (i, 0))


def _acc_row(d):
    return pl.BlockSpec((1, d), lambda i: (0, 0))


def _ffn_body(x_ref, g_ref, w1_ref, w3_ref, w2_ref, acc_ref, a_ref, b_ref, n_ref):
    xv = x_ref[...]
    xhat, _ = _rms_parts(xv)
    n = (xhat * g_ref[...]).astype(BF16)
    n_ref[...] = n
    acc_ref[...] = xv

    def fstep(f, c):
        rows = pl.ds(pl.multiple_of(f * FFN_FT, FFN_FT), FFN_FT)
        a = _nt(n, w1_ref[rows, :])
        b = _nt(n, w3_ref[rows, :])
        a_ref[f] = a.astype(BF16)
        b_ref[f] = b.astype(BF16)
        s = (a * jax.nn.sigmoid(a) * b).astype(BF16)
        acc_ref[...] += 0.5 * _nn(s, w2_ref[rows, :])
        return c

    lax.fori_loop(0, D_FF // FFN_FT, fstep, 0, unroll=True)


def _ffn_fwd(x, g, w1t, w3t, w2, name, carry=None):
    t = x.shape[0]
    tm = _tile(t)
    nf = D_FF // FFN_FT
    blk3 = pl.BlockSpec((nf, tm, FFN_FT), lambda i: (0, i, 0))
    sh3 = jax.ShapeDtypeStruct((nf, t, FFN_FT), BF16)
    (h, a3, b3, n), landed = _call(
        functools.partial(_ffn_body), name=name, grid=(t // tm,),
        in_specs=[_row_tile(tm, D_MODEL), _acc_row(D_MODEL), VMEM_FULL, VMEM_FULL, VMEM_FULL],
        out_specs=[_row_tile(tm, D_MODEL), blk3, blk3, _row_tile(tm, D_MODEL)],
        out_shape=[jax.ShapeDtypeStruct((t, D_MODEL), F32), sh3, sh3, jax.ShapeDtypeStruct((t, D_MODEL), BF16)],
        args=(x, g, w1t, w3t, w2), carry=carry)
    return h, (a3, b3, n), landed


def _ffn_fwd_head(x, g, w1t, w3t, w2, gf, target, name):
    t = x.shape[0]
    tm = _tile(t)
    nf = D_FF // FFN_FT

    def body(x_ref, g_ref, w1_ref, w3_ref, w2_ref, gf_ref, t_ref, loss_ref, dh_ref, dgf_ref, a_ref, b_ref, n_ref, acc):
        _ffn_body(x_ref, g_ref, w1_ref, w3_ref, w2_ref, acc, a_ref, b_ref, n_ref)
        _head_math(acc[...], gf_ref[...], t_ref[...], loss_ref, dh_ref, dgf_ref)

    blk3 = pl.BlockSpec((nf, tm, FFN_FT), lambda i: (0, i, 0))
    sh3 = jax.ShapeDtypeStruct((nf, t, FFN_FT), BF16)
    (loss, dh, dgf, a3, b3, n), _ = _call(
        body, name=name, grid=(t // tm,),
        in_specs=[_row_tile(tm, D_MODEL), _acc_row(D_MODEL), VMEM_FULL, VMEM_FULL, VMEM_FULL, _acc_row(D_MODEL),
                  _row_tile(tm, D_MODEL)],
        out_specs=[pl.BlockSpec((1, 1), lambda i: (0, 0)), _row_tile(tm, D_MODEL), _acc_row(D_MODEL), blk3, blk3,
                   _row_tile(tm, D_MODEL)],
        out_shape=[jax.ShapeDtypeStruct((1, 1), F32), jax.ShapeDtypeStruct((t, D_MODEL), F32),
                   jax.ShapeDtypeStruct((1, D_MODEL), F32), sh3, sh3, jax.ShapeDtypeStruct((t, D_MODEL), BF16)],
        scratch_shapes=[pltpu.VMEM((tm, D_MODEL), F32)],
        args=(x, g, w1t, w3t, w2, gf, target))
    return loss, dh, dgf, (a3, b3, n)


def _head_math(h, gv, target, loss_ref, dh_ref, dg_ref):
    i = pl.program_id(0)
    xhat, r = _rms_parts(h)
    err = xhat * gv - target
    dx, dg = _rms_bwd(err * (1.0 / D_MODEL), gv, xhat, r)
    dh_ref[...] = dx

    @pl.when(i == 0)
    def _():
        loss_ref[...] = jnp.zeros_like(loss_ref)
        dg_ref[...] = jnp.zeros_like(dg_ref)

    loss_ref[...] += (0.5 / D_MODEL) * jnp.sum(jnp.sum(err * err, axis=1, keepdims=True), axis=0, keepdims=True)
    dg_ref[...] += dg


def _ffn_bwd(x, dh, g, a3, b3, w1t, w3t, w2, name, carry=None):
    t = x.shape[0]
    tm = _tile(t) // 2
    nf = D_FF // FFN_FT

    def body(x_ref, dh_ref, g_ref, a_ref, b_ref, w1_ref, w3_ref, w2_ref,
             dx_ref, dg_ref, da_ref, db_ref, s_ref, dhh_ref, dn_acc):
        i = pl.program_id(0)
        xv = x_ref[...]
        gv = g_ref[...]
        xhat, r = _rms_parts(xv)
        dhv = dh_ref[...]
        dhh = (0.5 * dhv).astype(BF16)
        dhh_ref[...] = dhh
        dn_acc[...] = jnp.zeros_like(dn_acc)

        def fstep(f, c):
            rows = pl.ds(pl.multiple_of(f * FFN_FT, FFN_FT), FFN_FT)
            w1c, w3c, w2c = w1_ref[rows, :], w3_ref[rows, :], w2_ref[rows, :]
            a = a_ref[f].astype(F32)
            b = b_ref[f].astype(F32)
            sg = jax.nn.sigmoid(a)
            sl = a * sg
            ds = _nt(dhh, w2c)
            da = (ds * b * sg * (1.0 + a * (1.0 - sg))).astype(BF16)
            db = (ds * sl).astype(BF16)
            s_ref[f] = (sl * b).astype(BF16)
            da_ref[f] = da
            db_ref[f] = db
            dn_acc[...] += _nn(da, w1c) + _nn(db, w3c)
            return c

        lax.fori_loop(0, nf, fstep, 0, unroll=True)
        dx, dg = _rms_bwd(dn_acc[...], gv, xhat, r)
        dx_ref[...] = dhv + dx

        @pl.when(i == 0)
        def _():
            dg_ref[...] = jnp.zeros_like(dg_ref)

        dg_ref[...] += dg

    blk3 = pl.BlockSpec((nf, tm, FFN_FT), lambda i: (0, i, 0))
    sh3 = jax.ShapeDtypeStruct((nf, t, FFN_FT), BF16)
    return _call(
        body, name=name, grid=(t // tm,),
        in_specs=[_row_tile(tm, D_MODEL), _row_tile(tm, D_MODEL), _acc_row(D_MODEL), blk3, blk3,
                  VMEM_FULL, VMEM_FULL, VMEM_FULL],
        out_specs=[_row_tile(tm, D_MODEL), _acc_row(D_MODEL), blk3, blk3, blk3, _row_tile(tm, D_MODEL)],
        out_shape=[jax.ShapeDtypeStruct((t, D_MODEL), F32), jax.ShapeDtypeStruct((1, D_MODEL), F32), sh3, sh3, sh3,
                   jax.ShapeDtypeStruct((t, D_MODEL), BF16)],
        scratch_shapes=[pltpu.VMEM((tm, D_MODEL), F32)],
        args=(x, dh, g, a3, b3, w1t, w3t, w2), carry=carry)


def _mm_tn(a, b, name, carry=None):
    t, n = b.shape
    kc = min(512, t)
    if a.ndim == 3:
        nb, _, tb = a.shape
        a_spec = pl.BlockSpec((1, t, tb), lambda i: (i, 0, 0))
    else:
        m = a.shape[1]
        tb = min(m, 256)
        nb = m // tb
        a_spec = pl.BlockSpec((t, tb), lambda i: (0, i))
    three_d = a.ndim == 3

    def body(a_ref, b_ref, o_ref, acc):
        acc[...] = jnp.zeros_like(acc)

        def kstep(k, c):
            rows = pl.ds(pl.multiple_of(k * kc, kc), kc)
            av = a_ref[0, rows, :] if three_d else a_ref[rows, :]
            acc[...] += _tn(av.astype(BF16), b_ref[rows, :])
            return c

        lax.fori_loop(0, t // kc, kstep, 0, unroll=True)
        o_ref[...] = acc[...].astype(BF16)

    (out,), landed = _call(
        body, name=name, grid=(nb,),
        in_specs=[a_spec, VMEM_FULL],
        out_specs=[pl.BlockSpec((tb, n), lambda i: (i, 0))],
        out_shape=[jax.ShapeDtypeStruct((nb * tb, n), BF16)],
        scratch_shapes=[pltpu.VMEM((tb, n), F32)],
        args=(a, b), carry=carry)
    return (out, landed) if carry is not None else out


MM_TB = 256


def _mm_tn_many(arrays, b, name):
    t, n = b.shape
    kc = min(512, t)
    counts = [a.shape[1] // MM_TB for a in arrays]
    starts = [sum(counts[:k]) for k in range(len(arrays))]

    def spec(start, count):
        return pl.BlockSpec((t, MM_TB), lambda i: (0, jnp.clip(i - start, 0, count - 1)))

    def body(*refs):
        a_refs, (b_ref, o_ref, acc) = refs[:len(arrays)], refs[len(arrays):]
        i = pl.program_id(0)
        for a_ref, start, count in zip(a_refs, starts, counts):
            @pl.when((i >= start) & (i < start + count))
            def _(a_ref=a_ref):
                acc[...] = jnp.zeros_like(acc)

                def kstep(k, c):
                    rows = pl.ds(pl.multiple_of(k * kc, kc), kc)
                    acc[...] += _tn(a_ref[rows, :].astype(BF16), b_ref[rows, :])
                    return c

                lax.fori_loop(0, t // kc, kstep, 0, unroll=True)
                o_ref[...] = acc[...].astype(BF16)

    return pl.pallas_call(
        body, name=name, grid=(sum(counts),),
        in_specs=[spec(s, c) for s, c in zip(starts, counts)] + [VMEM_FULL],
        out_specs=pl.BlockSpec((MM_TB, n), lambda i: (i, 0)),
        out_shape=jax.ShapeDtypeStruct((sum(counts) * MM_TB, n), BF16),
        scratch_shapes=[pltpu.VMEM((MM_TB, n), F32)],
        compiler_params=_cparams(1),
    )(*arrays, b)


def _mix_pre_fwd(h, g, wint, carry=None):
    t = h.shape[0]
    tm = _tile(t)

    def body(h_ref, g_ref, w_ref, u_ref, *outs):
        xhat, _ = _rms_parts(h_ref[...])
        u = (xhat * g_ref[...]).astype(BF16)
        u_ref[...] = u
        for o_ref, off, size in zip(outs, IN_OFFS, IN_SIZES):
            o_ref[...] = _nt(u, w_ref[off:off + size, :])

    return _call(
        body, name="mix_pre_fwd", grid=(t // tm,),
        in_specs=[_row_tile(tm, D_MODEL), _acc_row(D_MODEL), VMEM_FULL],
        out_specs=[_row_tile(tm, D_MODEL)] + [_row_tile(tm, s) for s in IN_SIZES],
        out_shape=[jax.ShapeDtypeStruct((t, D_MODEL), BF16)] + [jax.ShapeDtypeStruct((t, s), F32) for s in IN_SIZES],
        args=(h, g, wint), carry=carry)


def _mix_pre_bwd(h, g, wint, dh2, dz, carry=None):
    t = h.shape[0]
    tm = _tile(t)

    def body(h_ref, g_ref, w_ref, dh2_ref, *rest):
        dz_refs, (dh1_ref, dg_ref) = rest[:len(IN_SIZES)], rest[len(IN_SIZES):]
        i = pl.program_id(0)
        gv = g_ref[...]
        xhat, r = _rms_parts(h_ref[...])
        du = jnp.zeros((tm, D_MODEL), F32)
        for dz_ref, off, size in zip(dz_refs, IN_OFFS, IN_SIZES):
            du = du + _nn(dz_ref[...].astype(BF16), w_ref[off:off + size, :])
        dx, dg = _rms_bwd(du, gv, xhat, r)
        dh1_ref[...] = dh2_ref[...] + dx

        @pl.when(i == 0)
        def _():
            dg_ref[...] = jnp.zeros_like(dg_ref)

        dg_ref[...] += dg

    return _call(
        body, name="mix_pre_bwd", grid=(t // tm,),
        in_specs=[_row_tile(tm, D_MODEL), _acc_row(D_MODEL), VMEM_FULL, _row_tile(tm, D_MODEL)]
        + [_row_tile(tm, s) for s in IN_SIZES],
        out_specs=[_row_tile(tm, D_MODEL), _acc_row(D_MODEL)],
        out_shape=[jax.ShapeDtypeStruct((t, D_MODEL), F32), jax.ShapeDtypeStruct((1, D_MODEL), F32)],
        args=(h, g, wint, dh2, *dz), carry=carry)


def _disc_math(lre, lim, ldt, bre, bim):
    dt = jnp.exp(ldt)
    mag = jnp.exp(lre * dt)
    ar = mag * jnp.cos(lim * dt)
    ai = mag * jnp.sin(lim * dt)
    den = lre * lre + lim * lim
    nr = ar - 1.0
    fr = (nr * lre + ai * lim) / den
    fi = (ai * lre - nr * lim) / den
    fr, fi = fr[:, None, :], fi[:, None, :]
    return ar, ai, fr * bre - fi * bim, fr * bim + fi * bre


def _s5_disc(lre, lim, ldt, bre, bim):
    def body(lre_ref, lim_ref, ldt_ref, bre_ref, bim_ref, ar_ref, ai_ref, bbr_ref, bbi_ref):
        ar, ai, bbr, bbi = _disc_math(lre_ref[...], lim_ref[...], ldt_ref[...], bre_ref[...], bim_ref[...])
        ar_ref[...] = ar
        ai_ref[...] = ai
        bbr_ref[...] = bbr
        bbi_ref[...] = bbi

    small = jax.ShapeDtypeStruct(lre.shape, F32)
    big = jax.ShapeDtypeStruct(bre.shape, F32)
    return pl.pallas_call(body, name="s5_disc", out_shape=[small, small, big, big],
                          in_specs=[VMEM_FULL] * 5, out_specs=[VMEM_FULL] * 4)(lre, lim, ldt, bre, bim)


def _s5_disc_bwd(lre, lim, ldt, bre, bim, dar, dai, dbbr, dbbi):
    def body(lre_ref, lim_ref, ldt_ref, bre_ref, bim_ref, dar_ref, dai_ref, dbbr_ref, dbbi_ref,
             glre_ref, glim_ref, gldt_ref, gbre_ref, gbim_ref):
        _, vjp = jax.vjp(_disc_math, lre_ref[...], lim_ref[...], ldt_ref[...], bre_ref[...], bim_ref[...])
        glre, glim, gldt, gbre, gbim = vjp((dar_ref[...], dai_ref[...], dbbr_ref[...], dbbi_ref[...]))
        glre_ref[...] = glre
        glim_ref[...] = glim
        gldt_ref[...] = gldt
        gbre_ref[...] = gbre
        gbim_ref[...] = gbim

    small = jax.ShapeDtypeStruct(lre.shape, F32)
    big = jax.ShapeDtypeStruct(bre.shape, F32)
    return pl.pallas_call(body, name="s5_disc_bwd",
                          out_shape=[small, small, jax.ShapeDtypeStruct(ldt.shape, F32), big, big],
                          in_specs=[VMEM_FULL] * 9, out_specs=[VMEM_FULL] * 5,
                          )(lre, lim, ldt, bre, bim, dar, dai, dbbr, dbbi)


def _cmul(ar, ai, br, bi):
    return ar * br - ai * bi, ar * bi + ai * br


def _cpow(ar, ai, n):
    rr, ri = None, None
    pr, pi = ar, ai
    while n:
        if n & 1:
            rr, ri = (pr, pi) if rr is None else _cmul(rr, ri, pr, pi)
        n >>= 1
        if n:
            pr, pi = _cmul(pr, pi, pr, pi)
    return rr, ri


def _shift_rows(v, down):
    row = lax.broadcasted_iota(jnp.int32, v.shape, 0)
    if down:
        return jnp.where(row == 0, 0.0, pltpu.roll(v, 1, 0))
    return jnp.where(row == S5_SEGS - 1, 0.0, pltpu.roll(v, S5_SEGS - 1, 0))


def _chain_segments(er, ei, pr, pi, down):
    fr, fi = er, ei
    for _ in range(S5_SEGS - 1):
        sr, si = _shift_rows(fr, down), _shift_rows(fi, down)
        mr, mi = _cmul(pr, pi, sr, si)
        fr, fi = er + mr, ei + mi
    return _shift_rows(fr, down), _shift_rows(fi, down)


def _rows_to_scan_order(src_ref, dst_ref, t):
    ls = t // S5_SEGS

    def tile(j, c):
        dst_ref[pl.ds(pl.multiple_of(j * S5_SEGS, S5_SEGS), S5_SEGS), :] = src_ref[pl.ds(j, S5_SEGS, stride=ls), :]
        return c

    lax.fori_loop(0, ls, tile, 0, unroll=8)


def _rows_from_scan_order(src_ref, dst_ref, t):
    ls = t // S5_SEGS
    for s in range(S5_SEGS):
        def tile(jb, c, s=s):
            dst_ref[pl.ds(pl.multiple_of(s * ls + jb * 8, 8), 8), :] = (
                src_ref[pl.ds(jb * 8 * S5_SEGS + s, 8, stride=S5_SEGS), :])
            return c

        lax.fori_loop(0, ls // 8, tile, 0, unroll=8)


def _s5_fwd(ug, bd, ctd, ar4, ai4, dskip, carry=None):
    t = ug.shape[0]
    ls = t // S5_SEGS
    rc = min(512, t)
    ns = S5_BSTATE

    def body(ugn_ref, bd_ref, ct_ref, ar_ref, ai_ref, d_ref, xs_hbm, yn_ref, buf, ug_ref, y_ref, sem):
        cb = pl.program_id(0)
        bdv = bd_ref[0]
        _rows_to_scan_order(ugn_ref, ug_ref, t)

        def mm(i, c):
            rows = pl.ds(pl.multiple_of(i * rc, rc), rc)
            buf[rows, :] = _nn(ug_ref[rows, :].astype(BF16), bdv)
            return c

        lax.fori_loop(0, t // rc, mm, 0, unroll=True)
        arb = jnp.broadcast_to(ar_ref[0], (S5_SEGS, ns))
        aib = jnp.broadcast_to(ai_ref[0], (S5_SEGS, ns))

        def step(j, c, store):
            sr, si = c
            rows = pl.ds(pl.multiple_of(j * S5_SEGS, S5_SEGS), S5_SEGS)
            nr = arb * sr - aib * si + buf[rows, 0:ns]
            ni = arb * si + aib * sr + buf[rows, ns:2 * ns]
            if store:
                buf[rows, 0:ns] = nr
                buf[rows, ns:2 * ns] = ni
            return nr, ni

        zero = jnp.zeros((S5_SEGS, ns), F32)
        er, ei = lax.fori_loop(0, ls, functools.partial(step, store=False), (zero, zero))
        pr, pi = _cpow(arb, aib, ls)
        init = _chain_segments(er, ei, pr, pi, down=True)
        lax.fori_loop(0, ls, functools.partial(step, store=True), init)

        out = pltpu.make_async_copy(buf, xs_hbm.at[cb], sem)
        out.start()
        ctv = ct_ref[0]
        dv = d_ref[...]

        def ymm(i, c):
            rows = pl.ds(pl.multiple_of(i * rc, rc), rc)
            y_ref[rows, :] = _nn(buf[rows, :].astype(BF16), ctv) + dv * ug_ref[rows, :]
            return c

        lax.fori_loop(0, t // rc, ymm, 0, unroll=True)
        _rows_from_scan_order(y_ref, yn_ref, t)
        out.wait()

    return _call(
        body, name="s5_fwd", grid=(S5_BLOCKS,),
        in_specs=[pl.BlockSpec((t, 128), lambda i: (0, i)),
                  pl.BlockSpec((1, 128, 2 * ns), lambda i: (i, 0, 0)),
                  pl.BlockSpec((1, 2 * ns, 128), lambda i: (i, 0, 0)),
                  pl.BlockSpec((1, 1, ns), lambda i: (i, 0, 0)),
                  pl.BlockSpec((1, 1, ns), lambda i: (i, 0, 0)),
                  pl.BlockSpec((1, 128), lambda i: (0, i))],
        out_specs=[ANY, pl.BlockSpec((t, 128), lambda i: (0, i))],
        out_shape=[jax.ShapeDtypeStruct((S5_BLOCKS, t, 2 * ns), F32), jax.ShapeDtypeStruct((t, S5_WIDTH), F32)],
        scratch_shapes=[pltpu.VMEM((t, 2 * ns), F32), pltpu.VMEM((t, 128), F32), pltpu.VMEM((t, 128), F32),
                        pltpu.SemaphoreType.DMA(())],
        args=(ug, bd, ctd, ar4, ai4, dskip), carry=carry)


def _s5_bwd(dy, ug, xs, cd, bdt, ar4, ai4, dskip, carry=None):
    t = ug.shape[0]
    ls = t // S5_SEGS
    rc = min(512, t)
    ns = S5_BSTATE

    def body(dyn_ref, ugn_ref, xs_hbm, cd_ref, bdt_ref, ar_ref, ai_ref, d_ref,
             dugn_ref, dbd_ref, dcd_ref, dd_ref, dar_ref, dai_ref, xbuf, lam, dy_ref, ug_ref, dug_ref, sem):
        cb = pl.program_id(0)
        load = pltpu.make_async_copy(xs_hbm.at[cb], xbuf, sem)
        load.start()
        cdv = cd_ref[0]
        _rows_to_scan_order(dyn_ref, dy_ref, t)
        _rows_to_scan_order(ugn_ref, ug_ref, t)

        def mm(i, c):
            rows = pl.ds(pl.multiple_of(i * rc, rc), rc)
            lam[rows, :] = _nn(dy_ref[rows, :].astype(BF16), cdv)
            return c

        lax.fori_loop(0, t // rc, mm, 0, unroll=True)
        arb = jnp.broadcast_to(ar_ref[0], (S5_SEGS, ns))
        aib = jnp.broadcast_to(ai_ref[0], (S5_SEGS, ns))

        def lam_step(j, lr, li):
            rows = pl.ds(pl.multiple_of(j * S5_SEGS, S5_SEGS), S5_SEGS)
            nr = arb * lr + aib * li + lam[rows, 0:ns]
            ni = arb * li - aib * lr + lam[rows, ns:2 * ns]
            return rows, nr, ni

        def pass1(jj, c):
            _, nr, ni = lam_step(ls - 1 - jj, *c)
            return nr, ni

        zero = jnp.zeros((S5_SEGS, ns), F32)
        er, ei = lax.fori_loop(0, ls, pass1, (zero, zero))
        pr, pi = _cpow(arb, aib, ls)
        init = _chain_segments(er, ei, pr, -pi, down=False)
        load.wait()

        def accumulate(acc, nr, ni, xpr, xpi):
            return acc[0] + nr * xpr + ni * xpi, acc[1] + ni * xpr - nr * xpi

        def pass2(jj, c):
            lr, li, accr, acci = c
            j = ls - 1 - jj
            rows, nr, ni = lam_step(j, lr, li)
            lam[rows, 0:ns] = nr
            lam[rows, ns:2 * ns] = ni
            prev = pl.ds(pl.multiple_of((j - 1) * S5_SEGS, S5_SEGS), S5_SEGS)
            accr, acci = accumulate((accr, acci), nr, ni, xbuf[prev, 0:ns], xbuf[prev, ns:2 * ns])
            return nr, ni, accr, acci

        lr, li, accr, acci = lax.fori_loop(0, ls - 1, pass2, (init[0], init[1], zero, zero))
        rows, nr, ni = lam_step(0, lr, li)
        lam[rows, 0:ns] = nr
        lam[rows, ns:2 * ns] = ni
        last = pl.ds((ls - 1) * S5_SEGS, S5_SEGS)
        accr, acci = accumulate((accr, acci), nr, ni,
                                _shift_rows(xbuf[last, 0:ns], True), _shift_rows(xbuf[last, ns:2 * ns], True))
        dar_ref[0] = jnp.sum(accr, axis=0, keepdims=True)
        dai_ref[0] = jnp.sum(acci, axis=0, keepdims=True)

        bdtv = bdt_ref[0]
        dv = d_ref[...]
        dbd_ref[...] = jnp.zeros_like(dbd_ref)
        dcd_ref[...] = jnp.zeros_like(dcd_ref)
        dd_ref[...] = jnp.zeros_like(dd_ref)

        def tail(i, c):
            rows = pl.ds(pl.multiple_of(i * rc, rc), rc)
            dy = dy_ref[rows, :]
            ug = ug_ref[rows, :]
            lb = lam[rows, :].astype(BF16)
            dug_ref[rows, :] = _nn(lb, bdtv) + dv * dy
            dbd_ref[0] += _tn(ug.astype(BF16), lb)
            dcd_ref[0] += _tn(dy.astype(BF16), xbuf[rows, :].astype(BF16))
            dd_ref[...] += jnp.sum(dy * ug, axis=0, keepdims=True)
            return c

        lax.fori_loop(0, t // rc, tail, 0, unroll=True)
        _rows_from_scan_order(dug_ref, dugn_ref, t)

    chan = pl.BlockSpec((t, 128), lambda i: (0, i))
    dense = pl.BlockSpec((1, 128, 2 * ns), lambda i: (i, 0, 0))
    vec = pl.BlockSpec((1, 1, ns), lambda i: (i, 0, 0))
    return _call(
        body, name="s5_bwd", grid=(S5_BLOCKS,),
        in_specs=[chan, chan, ANY, dense, pl.BlockSpec((1, 2 * ns, 128), lambda i: (i, 0, 0)), vec, vec,
                  pl.BlockSpec((1, 128), lambda i: (0, i))],
        out_specs=[chan, dense, dense, pl.BlockSpec((1, 128), lambda i: (0, i)), vec, vec],
        out_shape=[jax.ShapeDtypeStruct((t, S5_WIDTH), F32),
                   jax.ShapeDtypeStruct((S5_BLOCKS, 128, 2 * ns), F32),
                   jax.ShapeDtypeStruct((S5_BLOCKS, 128, 2 * ns), F32),
                   jax.ShapeDtypeStruct((1, S5_WIDTH), F32),
                   jax.ShapeDtypeStruct((S5_BLOCKS, 1, ns), F32),
                   jax.ShapeDtypeStruct((S5_BLOCKS, 1, ns), F32)],
        scratch_shapes=[pltpu.VMEM((t, 2 * ns), F32), pltpu.VMEM((t, 2 * ns), F32)]
        + [pltpu.VMEM((t, 128), F32)] * 3 + [pltpu.SemaphoreType.DMA(())],
        args=(dy, ug, xs, cd, bdt, ar4, ai4, dskip), carry=carry)


def _cumsum_rows(x, reverse):
    c = x.shape[0]
    row = lax.broadcasted_iota(jnp.int32, x.shape, 0)
    d = 1
    while d < c:
        if reverse:
            x = x + jnp.where(row < c - d, pltpu.roll(x, c - d, 0), 0.0)
        else:
            x = x + jnp.where(row >= d, pltpu.roll(x, d, 0), 0.0)
        d *= 2
    return x


def _gla_common(q, k, alow, wup, bup):
    c = GLA_CHUNK
    pre = _nn(alow.astype(BF16), wup.astype(BF16)) + bup
    la = (jnp.minimum(pre, 0.0) - jnp.log(1.0 + jnp.exp(-jnp.abs(pre)))) * (1.0 / GLA_TAU)
    rr = lax.broadcasted_iota(jnp.int32, (c, c), 0)
    cc = lax.broadcasted_iota(jnp.int32, (c, c), 1)
    tril = (rr >= cc).astype(F32)
    bc = _cumsum_rows(la, reverse=False)
    bl = bc[c - 1:c, :]
    e_pos = jnp.exp(bc)
    e_neg = jnp.exp(-bc)
    e_end = jnp.exp(bl - bc)
    qt = q * (GLA_DK ** -0.5) * e_pos
    kt = k * e_neg
    ke = k * e_end
    lane = lax.broadcasted_iota(jnp.int32, (1, GLA_KEY), 1)
    masks = [((lane >= h * GLA_DK) & (lane < (h + 1) * GLA_DK)).astype(F32) for h in range(GLA_HEADS)]
    return dict(pre=pre, tril=tril, bc=bc, bl=bl, e_pos=e_pos, e_neg=e_neg, e_end=e_end,
                qt=qt, kt=kt, ke=ke, dec=jnp.exp(bl), masks=masks)


def _gla_fwd(q, k, v, alow, wup, bup, carry=None):
    t = q.shape[0]
    c = GLA_CHUNK
    n = t // c
    step = GLA_STEP_CHUNKS * c

    def body(q_ref, k_ref, v_ref, al_ref, wup_ref, bup_ref, o_ref, ss_ref, s_ref):
        i = pl.program_id(0)

        @pl.when(i == 0)
        def _():
            s_ref[...] = jnp.zeros_like(s_ref)

        wup_v, bup_v = wup_ref[...], bup_ref[...]
        s = s_ref[...]
        for j in range(GLA_STEP_CHUNKS):
            tok = slice(j * c, (j + 1) * c)
            m = _gla_common(q_ref[tok, :], k_ref[tok, :], al_ref[tok, :], wup_v, bup_v)
            ss_ref[j] = s
            sb = s.astype(BF16)
            ktb = m["kt"].astype(BF16)
            update = jnp.zeros_like(s)
            for h in range(GLA_HEADS):
                mask = m["masks"][h]
                qm = (m["qt"] * mask).astype(BF16)
                vh = v_ref[tok, h * GLA_DV:(h + 1) * GLA_DV].astype(BF16)
                p = (m["tril"] * _nt(qm, ktb)).astype(BF16)
                o_ref[tok, h * GLA_DV:(h + 1) * GLA_DV] = _nn(p, vh) + _nt(qm, sb)
                update = update + _tn(vh, (m["ke"] * mask).astype(BF16))
            s = m["dec"] * s + update
        s_ref[...] = s

    return _call(
        body, name="gla_fwd", grid=(t // step,),
        in_specs=[_row_tile(step, GLA_KEY), _row_tile(step, GLA_KEY), _row_tile(step, GLA_VAL),
                  _row_tile(step, GLA_RANK), VMEM_FULL, VMEM_FULL],
        out_specs=[_row_tile(step, GLA_VAL), pl.BlockSpec((GLA_STEP_CHUNKS, GLA_DV, GLA_KEY), lambda i: (i, 0, 0))],
        out_shape=[jax.ShapeDtypeStruct((t, GLA_VAL), F32), jax.ShapeDtypeStruct((n, GLA_DV, GLA_KEY), F32)],
        scratch_shapes=[pltpu.VMEM((GLA_DV, GLA_KEY), F32)],
        args=(q, k, v, alow, wup, bup), carry=carry)


def _gla_bwd(q, k, v, alow, wup, bup, ssave, do, carry=None):
    t = q.shape[0]
    c = GLA_CHUNK
    n = t // c

    def body(q_ref, k_ref, v_ref, al_ref, wup_ref, bup_ref, ss_ref, do_ref,
             dq_ref, dk_ref, dv_ref, dal_ref, dwup_ref, dbup_ref, ds_ref):
        i = pl.program_id(0)

        @pl.when(i == 0)
        def _():
            ds_ref[...] = jnp.zeros_like(ds_ref)
            dwup_ref[...] = jnp.zeros_like(dwup_ref)
            dbup_ref[...] = jnp.zeros_like(dbup_ref)

        wup_v, bup_v = wup_ref[...], bup_ref[...]
        ds_in = ds_ref[...]
        dwup = jnp.zeros((GLA_RANK, GLA_KEY), F32)
        dbup = jnp.zeros((1, GLA_KEY), F32)
        for j in reversed(range(GLA_STEP_CHUNKS)):
            tok = slice(j * c, (j + 1) * c)
            alow_v = al_ref[tok, :]
            m = _gla_common(q_ref[tok, :], k_ref[tok, :], alow_v, wup_v, bup_v)
            s = ss_ref[j]
            sb = s.astype(BF16)
            dsb = ds_in.astype(BF16)
            qt, kt, ke = m["qt"], m["kt"], m["ke"]
            ktb = kt.astype(BF16)
            dqt = jnp.zeros((c, GLA_KEY), F32)
            dkt = jnp.zeros((c, GLA_KEY), F32)
            dke = jnp.zeros((c, GLA_KEY), F32)
            update = jnp.zeros_like(ds_in)
            for h in range(GLA_HEADS):
                mask = m["masks"][h]
                qm = (qt * mask).astype(BF16)
                km = (kt * mask).astype(BF16)
                kem = (ke * mask).astype(BF16)
                cols = slice(h * GLA_DV, (h + 1) * GLA_DV)
                vh = v_ref[tok, cols].astype(BF16)
                doh = do_ref[tok, cols].astype(BF16)
                p = (m["tril"] * _nt(qm, ktb)).astype(BF16)
                dp = (m["tril"] * _nt(doh, vh)).astype(BF16)
                dv_ref[tok, cols] = (_tn(p, doh) + _nt(kem, dsb)).astype(BF16)
                dqt = dqt + _nn(dp, km) + _nn(doh, sb) * mask
                dkt = dkt + _tn(dp, qm)
                dke = dke + _nn(vh, dsb) * mask
                update = update + _tn(doh, qm)
            ddec = jnp.sum(ds_in * s, axis=0, keepdims=True)
            dq_ref[tok, :] = (dqt * m["e_pos"] * (GLA_DK ** -0.5)).astype(BF16)
            dk_ref[tok, :] = (dkt * m["e_neg"] + dke * m["e_end"]).astype(BF16)
            dkeke = dke * ke
            dbl = jnp.sum(dkeke, axis=0, keepdims=True) + ddec * m["dec"]
            last = (lax.broadcasted_iota(jnp.int32, (c, 1), 0) == c - 1).astype(F32)
            dla = _cumsum_rows(dqt * qt - dkt * kt - dkeke + last * dbl, reverse=True)
            dpre = dla * (1.0 / GLA_TAU) * jax.nn.sigmoid(-m["pre"])
            dpb = dpre.astype(BF16)
            dal_ref[tok, :] = _nt(dpb, wup_v.astype(BF16)).astype(BF16)
            dwup = dwup + _tn(alow_v.astype(BF16), dpb)
            dbup = dbup + jnp.sum(dpre, axis=0, keepdims=True)
            ds_in = m["dec"] * ds_in + update
        ds_ref[...] = ds_in
        dwup_ref[...] += dwup
        dbup_ref[...] += dbup

    step = GLA_STEP_CHUNKS * c
    nsteps = t // step

    def rev(d):
        return pl.BlockSpec((step, d), lambda i: (nsteps - 1 - i, 0))

    return _call(
        body, name="gla_bwd", grid=(nsteps,),
        in_specs=[rev(GLA_KEY), rev(GLA_KEY), rev(GLA_VAL), rev(GLA_RANK), VMEM_FULL, VMEM_FULL,
                  pl.BlockSpec((GLA_STEP_CHUNKS, GLA_DV, GLA_KEY), lambda i: (nsteps - 1 - i, 0, 0)), rev(GLA_VAL)],
        out_specs=[rev(GLA_KEY), rev(GLA_KEY), rev(GLA_VAL), rev(GLA_RANK),
                   pl.BlockSpec((GLA_RANK, GLA_KEY), lambda i: (0, 0)), _acc_row(GLA_KEY)],
        out_shape=[jax.ShapeDtypeStruct((t, GLA_KEY), BF16), jax.ShapeDtypeStruct((t, GLA_KEY), BF16),
                   jax.ShapeDtypeStruct((t, GLA_VAL), BF16), jax.ShapeDtypeStruct((t, GLA_RANK), BF16),
                   jax.ShapeDtypeStruct((GLA_RANK, GLA_KEY), F32), jax.ShapeDtypeStruct((1, GLA_KEY), F32)],
        scratch_shapes=[pltpu.VMEM((GLA_DV, GLA_KEY), F32)],
        args=(q, k, v, alow, wup, bup, ssave, do), carry=carry)


def _post_math(y, o, r, gs5, ggla, wg, bg, gn, ps5t, pglat):
    y2 = y * y
    th = jnp.tanh(GELU_C0 * (y + GELU_C1 * y * y2))
    z5 = 0.5 * y * (1.0 + th)
    z5b = z5.astype(BF16)
    gate = jax.nn.sigmoid(_nn(z5b, wg) + bg)
    ys5 = z5 * gate
    rs, on = [], []
    for h in range(GLA_HEADS):
        oh = o[:, h * GLA_DV:(h + 1) * GLA_DV]
        rh = lax.rsqrt(jnp.mean(oh * oh, axis=-1, keepdims=True) + EPS)
        rs.append(rh)
        on.append(oh * rh)
    on = jnp.concatenate(on, axis=-1)
    sr = jax.nn.sigmoid(r)
    silu_r = r * sr
    ygla = on * gn * silu_r
    ys5b, yglab = ys5.astype(BF16), ygla.astype(BF16)
    m5 = _nt(ys5b, ps5t)
    mg = _nt(yglab, pglat)
    s5g, glag = jax.nn.sigmoid(gs5), jax.nn.sigmoid(ggla)
    merged = s5g * m5 + glag * mg
    return dict(y2=y2, th=th, z5=z5, z5b=z5b, gate=gate, ys5b=ys5b, yglab=yglab, rs=rs, on=on, sr=sr,
                silu_r=silu_r, m5=m5, mg=mg, s5g=s5g, glag=glag, mergedb=merged.astype(BF16))


def _mix_post_fwd(y, o, r, gs5, ggla, h1, wg, bg, gn, ps5t, pglat, wout, carry=None):
    t = o.shape[0]
    tm = _tile(t)

    def body(y_ref, o_ref, r_ref, gs5_ref, ggla_ref, h1_ref, wg_ref, bg_ref, gn_ref, ps_ref, pg_ref, wo_ref, h2_ref):
        m = _post_math(y_ref[...], o_ref[...], r_ref[...], gs5_ref[...], ggla_ref[...],
                       wg_ref[...], bg_ref[...], gn_ref[...], ps_ref[...], pg_ref[...])
        h2_ref[...] = h1_ref[...] + _nn(m["mergedb"], wo_ref[...])

    (h2,), landed = _call(
        body, name="mix_post_fwd", grid=(t // tm,),
        in_specs=[_row_tile(tm, 512)] * 3 + [_row_tile(tm, D_MODEL)] * 3
        + [VMEM_FULL, _acc_row(512), _acc_row(512), VMEM_FULL, VMEM_FULL, VMEM_FULL],
        out_specs=[_row_tile(tm, D_MODEL)],
        out_shape=[jax.ShapeDtypeStruct((t, D_MODEL), F32)],
        args=(y, o, r, gs5, ggla, h1, wg, bg, gn, ps5t, pglat, wout), carry=carry)
    return h2, landed


def _mix_post_bwd(y, o, r, gs5, ggla, dh2, wg, bg, gn, ps5t, pglat, wout, carry=None):
    t = o.shape[0]
    tm = _tile(t) // 2

    def body(y_ref, o_ref, r_ref, gs5_ref, ggla_ref, dh2_ref, wg_ref, bg_ref, gn_ref, ps_ref, pg_ref, wo_ref,
             dy_ref, do_ref, dr_ref, dgs5_ref, dggla_ref, dbg_ref, dgn_ref,
             z5b_ref, dgp_ref, ys5b_ref, dm5b_ref, yglab_ref, dmgb_ref, mergedb_ref, dh2b_ref):
        i = pl.program_id(0)
        yv, ov, rv = y_ref[...], o_ref[...], r_ref[...]
        wg, gn, ps5t, pglat = wg_ref[...], gn_ref[...], ps_ref[...], pg_ref[...]
        m = _post_math(yv, ov, rv, gs5_ref[...], ggla_ref[...], wg, bg_ref[...], gn, ps5t, pglat)
        dh2b = dh2_ref[...].astype(BF16)
        dmerged = _nt(dh2b, wo_ref[...])
        s5g, glag = m["s5g"], m["glag"]
        dgs5_ref[...] = (dmerged * m["m5"] * s5g * (1.0 - s5g)).astype(BF16)
        dggla_ref[...] = (dmerged * m["mg"] * glag * (1.0 - glag)).astype(BF16)
        dm5b = (dmerged * s5g).astype(BF16)
        dmgb = (dmerged * glag).astype(BF16)
        dys5 = _nn(dm5b, ps5t)
        dygla = _nn(dmgb, pglat)
        gate, z5, th = m["gate"], m["z5"], m["th"]
        dgpre = dys5 * z5 * gate * (1.0 - gate)
        dgpb = dgpre.astype(BF16)
        dz5 = dys5 * gate + _nt(dgpb, wg)
        dgelu = 0.5 * (1.0 + th) + 0.5 * yv * (1.0 - th * th) * GELU_C0 * (1.0 + 3.0 * GELU_C1 * m["y2"])
        dy_ref[...] = dz5 * dgelu
        on, sr, silu_r = m["on"], m["sr"], m["silu_r"]
        dr_ref[...] = (dygla * on * gn * sr * (1.0 + rv * (1.0 - sr))).astype(BF16)
        dgn = jnp.sum(dygla * on * silu_r, axis=0, keepdims=True)
        don = dygla * gn * silu_r
        for h in range(GLA_HEADS):
            cols = slice(h * GLA_DV, (h + 1) * GLA_DV)
            donh, onh = don[:, cols], on[:, cols]
            do_ref[:, cols] = (m["rs"][h] * (donh - onh * jnp.mean(donh * onh, axis=-1, keepdims=True))).astype(BF16)

        @pl.when(i == 0)
        def _():
            dbg_ref[...] = jnp.zeros_like(dbg_ref)
            dgn_ref[...] = jnp.zeros_like(dgn_ref)

        dbg_ref[...] += jnp.sum(dgpre, axis=0, keepdims=True)
        dgn_ref[...] += dgn
        z5b_ref[...] = m["z5b"]
        dgp_ref[...] = dgpb
        ys5b_ref[...] = m["ys5b"]
        dm5b_ref[...] = dm5b
        yglab_ref[...] = m["yglab"]
        dmgb_ref[...] = dmgb
        mergedb_ref[...] = m["mergedb"]
        dh2b_ref[...] = dh2b

    def f32(d):
        return jax.ShapeDtypeStruct((t, d), F32)

    def b16(d):
        return jax.ShapeDtypeStruct((t, d), BF16)

    widths = (512, 512, 512, 1024, 512, 1024, 1024, 1024)
    return _call(
        body, name="mix_post_bwd", grid=(t // tm,),
        in_specs=[_row_tile(tm, 512)] * 3 + [_row_tile(tm, D_MODEL)] * 3
        + [VMEM_FULL, _acc_row(512), _acc_row(512), VMEM_FULL, VMEM_FULL, VMEM_FULL],
        out_specs=[_row_tile(tm, 512)] * 3 + [_row_tile(tm, D_MODEL)] * 2
        + [_acc_row(512)] * 2 + [_row_tile(tm, w) for w in widths],
        out_shape=[f32(512), b16(512), b16(512), b16(D_MODEL), b16(D_MODEL)]
        + [jax.ShapeDtypeStruct((1, 512), F32)] * 2
        + [b16(w) for w in widths],
        args=(y, o, r, gs5, ggla, dh2, wg, bg, gn, ps5t, pglat, wout), carry=carry)


ADAM_TILE_ELEMS = 256 * 1024


def _adamw(w, g, m, v, name):
    rows, cols = w.shape
    tr = rows
    while tr * cols > ADAM_TILE_ELEMS and tr % 16 == 0:
        tr //= 2

    spec = pl.BlockSpec((tr, cols), lambda i: (i, 0))
    sh = jax.ShapeDtypeStruct((rows, cols), F32)
    return pl.pallas_call(functools.partial(_adamw_body), name=name, grid=(rows // tr,), in_specs=[spec] * 4,
                          out_specs=[spec] * 3, out_shape=[sh] * 3, compiler_params=_cparams(1))(w, g, m, v)


def _adamw_math(w, g, m, v):
    nm = ADAM_B1 * m + (1.0 - ADAM_B1) * g
    nv = ADAM_B2 * v + (1.0 - ADAM_B2) * (g * g)
    m_hat = nm / (1.0 - ADAM_B1 ** ADAM_STEP)
    v_hat = nv / (1.0 - ADAM_B2 ** ADAM_STEP)
    return -ADAM_LR * (m_hat / (jnp.sqrt(v_hat) + ADAM_EPS) + ADAM_WD * w), nm, nv


def _adamw_body(w_ref, g_ref, m_ref, v_ref, d_ref, nm_ref, nv_ref):
    d_ref[...], nm_ref[...], nv_ref[...] = _adamw_math(w_ref[...], g_ref[...], m_ref[...], v_ref[...])


SUM_ADAM_ROWS = 32


def _sum_adamw(landed, ws, ms, vs, name, carry=None):
    k = len(ws)
    n = landed[0].shape[0]
    r, c = ws[0].shape
    tr = SUM_ADAM_ROWS

    def body(*refs):
        lands, (w_refs, m_refs, v_refs), outs = refs[:k], (refs[k:2 * k], refs[2 * k:3 * k], refs[3 * k:4 * k]), refs[4 * k:]
        for i in range(k):
            g = lands[i][0].astype(F32)
            for s in range(1, n):
                g = g + lands[i][s].astype(F32)
            outs[i][...] = g
            outs[k + i][...], outs[2 * k + i][...], outs[3 * k + i][...] = _adamw_math(
                w_refs[i][...], g, m_refs[i][...], v_refs[i][...])

    row = pl.BlockSpec((tr, c), lambda i: (i, 0))
    return _call(
        body, name=name, grid=(r // tr,),
        in_specs=[pl.BlockSpec((n, tr, c), lambda i: (0, i, 0))] * k + [row] * (3 * k),
        out_specs=[row] * (4 * k), out_shape=[jax.ShapeDtypeStruct((r, c), F32)] * (4 * k),
        args=(*landed, *ws, *ms, *vs), carry=carry)


def _adamw_many(ws, gs, ms, vs, name):
    n = len(ws)

    def body(*refs):
        ins, outs = refs[:4 * n], refs[4 * n:]
        for i in range(n):
            _adamw_body(*(ins[j * n + i] for j in range(4)), *(outs[j * n + i] for j in range(3)))

    shapes = [jax.ShapeDtypeStruct(w.shape, F32) for w in ws]
    res = pl.pallas_call(body, name=name, in_specs=[VMEM_FULL] * (4 * n), out_specs=[VMEM_FULL] * (3 * n),
                         out_shape=shapes * 3)(*ws, *gs, *ms, *vs)
    return res[:n], res[n:2 * n], res[2 * n:]


def _exchange(carry, name):
    return _call(None, name=name, grid=(1,), in_specs=[], out_specs=[], out_shape=[], args=(), carry=carry)[1]


def _pair_add(slabs, from_pair, name):
    _, r, cols = slabs.shape

    def body(s_ref, p_ref, o_ref):
        c = lax.axis_index("c")
        mine = jnp.where(c == 0, s_ref[0, 0].astype(F32), s_ref[0, 1].astype(F32))
        o_ref[0] = (mine + p_ref[0].astype(F32)).astype(BF16)

    return pl.pallas_call(
        body, name=name, grid=(4,),
        in_specs=[pl.BlockSpec((1, 2, r, cols), lambda i: (i, 0, 0, 0)), pl.BlockSpec((1, r, cols), lambda i: (i, 0, 0))],
        out_specs=pl.BlockSpec((1, r, cols), lambda i: (i, 0, 0)),
        out_shape=jax.ShapeDtypeStruct((4, r, cols), BF16),
        compiler_params=_cparams(1),
    )(slabs.reshape(4, 2, r, cols), from_pair)


def _sum_slabs(slabs, name):
    n = slabs.shape[0]

    def body(s_ref, o_ref):
        acc = s_ref[0].astype(F32)
        for s in range(1, n):
            acc = acc + s_ref[s].astype(F32)
        o_ref[...] = acc

    return pl.pallas_call(
        body, name=name, in_specs=[VMEM_FULL], out_specs=VMEM_FULL,
        out_shape=jax.ShapeDtypeStruct(slabs.shape[1:], F32),
        compiler_params=pltpu.CompilerParams(vmem_limit_bytes=VMEM_LIMIT_BYTES),
    )(slabs)


BIG = ("ffn1_w1", "ffn1_w3", "ffn1_w2", "w_in", "s5_glu_w", "gla_a_up_w", "proj_s5", "proj_gla", "w_out",
       "ffn2_w1", "ffn2_w3", "ffn2_w2")
GROUPS = (("ffn1_w1", "ffn1_w3", "ffn1_w2"),
          ("w_in", "s5_glu_w", "gla_a_up_w", "proj_s5", "proj_gla", "w_out"),
          ("ffn2_w1", "ffn2_w3", "ffn2_w2"))
W_IN_ROWS = 514
W_IN_PAD = 528
UP_COLS = 32
ROW_ADAM = ("ffn1_w1", "ffn1_w3", "w_in", "ffn2_w1", "ffn2_w3")
COL_SHARDED = ("ffn1_w1", "ffn1_w3", "w_in", "proj_s5", "proj_gla", "ffn2_w1", "ffn2_w3")

SMALL = ("ffn1_norm", "mix_norm", "s5_lambda_re", "s5_lambda_im", "s5_log_dt", "s5_b_re", "s5_b_im", "s5_c_re",
         "s5_c_im", "s5_d", "s5_glu_b", "gla_a_up_b", "gla_out_norm", "ffn2_norm", "final_norm")
SMALL_SHAPES = dict(ffn1_norm=(1, 1024), mix_norm=(1, 1024), s5_lambda_re=(1, 32, 64), s5_lambda_im=(1, 32, 64),
                    s5_log_dt=(1, 32), s5_b_re=(1, 32, 64, 16), s5_b_im=(1, 32, 64, 16), s5_c_re=(1, 32, 16, 64),
                    s5_c_im=(1, 32, 16, 64), s5_d=(1, 32, 16), s5_glu_b=(1, 512), gla_a_up_b=(1, 256),
                    gla_out_norm=(1, 512), ffn2_norm=(1, 1024), final_norm=(1024,))
SMALL_N = sum(math.prod(s) for s in SMALL_SHAPES.values())
SMALL_R = -(-SMALL_N // (64 * 1024)) * 64


def _shard_rows(name, a):
    if name == "gla_a_up_w":
        return jnp.pad(a, ((0, 0), (0, 128 - UP_COLS)))
    if name in COL_SHARDED:
        a = a.T
    if name == "w_in":
        return jnp.pad(a, ((0, W_IN_PAD - W_IN_ROWS), (0, 0)))
    return a.reshape(-1, 1024)


def _unshard_rows(name, rows, shape):
    if name == "gla_a_up_w":
        return rows[:, :UP_COLS]
    if name == "w_in":
        rows = rows[:W_IN_ROWS]
    if name in COL_SHARDED:
        return rows.reshape(shape[1], shape[0]).T
    return rows.reshape(shape)


def _pack_small(vals, loss):
    flat = jnp.concatenate([vals[n].reshape(-1).astype(F32) for n in SMALL] + [loss.reshape(1)])
    return jnp.pad(flat, (0, SMALL_R * 1024 - SMALL_N - 1)).reshape(SMALL_R, 1024)


S5_B = ("s5_b_re", "s5_b_im")


def _working(name, a):
    return a[0].transpose(0, 2, 1) if name in S5_B else a


def _declared(name, a):
    return a.transpose(0, 2, 1)[None] if name in S5_B else a.reshape(SMALL_SHAPES[name])


def _unpack_small(slab):
    flat = slab.reshape(-1)
    out, off = {}, 0
    for n in SMALL:
        size = math.prod(SMALL_SHAPES[n])
        shape = (S5_GROUPS, S5_GROUP, S5_STATE) if n in S5_B else SMALL_SHAPES[n]
        out[n] = flat[off:off + size].reshape(shape)
        off += size
    return out


FULL_SHAPES = dict(w_in=(IN_COLS, D_MODEL), s5_glu_w=(S5_WIDTH, S5_WIDTH), gla_a_up_w=(GLA_RANK, GLA_KEY),
                   proj_s5=(D_MODEL, S5_WIDTH), proj_gla=(D_MODEL, GLA_VAL), w_out=(D_MODEL, D_MODEL))


def _full_weight(name, gathered):
    if name == "gla_a_up_w":
        return gathered[:, :, :UP_COLS].transpose(1, 0, 2).reshape(GLA_RANK, GLA_KEY)
    if name == "w_in":
        gathered = gathered[:, :W_IN_ROWS]
    return gathered.reshape(FULL_SHAPES.get(name, (D_FF, D_MODEL)))


def _grad_slabs(name, g):
    if name == "gla_a_up_w":
        g = g.reshape(GLA_RANK, N_DEV, UP_COLS).transpose(1, 0, 2)
        return jnp.pad(g, ((0, 0), (0, 0), (0, 128 - UP_COLS))).astype(BF16)
    if name == "w_in":
        return jnp.pad(g.reshape(N_DEV, W_IN_ROWS, D_MODEL), ((0, 0), (0, W_IN_PAD - W_IN_ROWS), (0, 0)))
    return g.reshape(N_DEV, -1, 1024)


def _s5_dense(re, im, sign_im):
    eye = jnp.eye(8, dtype=F32)

    def one(a):
        a = a.reshape(S5_BLOCKS, 8, S5_GROUP, S5_STATE)
        return jnp.einsum("cghp,gk->cghkp", a, eye).reshape(S5_BLOCKS, 128, S5_BSTATE)

    return jnp.concatenate([one(re), sign_im * one(im)], axis=-1)


def _s5_undense(d):
    eye = jnp.eye(8, dtype=F32)

    def one(a):
        a = a.reshape(S5_BLOCKS, 8, S5_GROUP, 8, S5_STATE)
        return jnp.einsum("cghkp,gk->cghp", a, eye).reshape(S5_GROUPS, S5_GROUP, S5_STATE)

    return one(d[..., :S5_BSTATE]), one(d[..., S5_BSTATE:])


def _local_step(x, target, p, w, rows=None, opt=None):
    w = dict(w or {})
    landed_grads = {}

    def gather(names):
        return None if rows is None else ([rows[n] for n in names], "gather")

    def gathered(names, landed):
        w.update({n: _full_weight(n, g) for n, g in zip(names, landed)})

    def scatter(names):
        return None if rows is None else ([_grad_slabs(n, big[n]) for n in names], "scatter")

    def scattered(names, landed):
        landed_grads.update(zip(names, landed))

    if rows is not None:
        gathered(GROUPS[0], _exchange(gather(GROUPS[0]), "gather_ffn1"))
    g1, gm, g2 = p["ffn1_norm"], p["mix_norm"], p["ffn2_norm"]
    gf = p["final_norm"].reshape(1, D_MODEL)
    lre, lim = p["s5_lambda_re"][0], p["s5_lambda_im"][0]
    ldt = p["s5_log_dt"][0].reshape(S5_GROUPS, 1)
    bre = p["s5_b_re"][0].transpose(0, 2, 1)
    bim = p["s5_b_im"][0].transpose(0, 2, 1)
    cre, cim = p["s5_c_re"][0], p["s5_c_im"][0]
    dskip = p["s5_d"][0].reshape(1, S5_WIDTH)
    bg, bup, gn = p["s5_glu_b"], p["gla_a_up_b"], p["gla_out_norm"]

    mix_first, mix_rest = ("w_in", "gla_a_up_w"), ("s5_glu_w", "proj_s5", "proj_gla", "w_out")
    h1, (a3_1, b3_1, n1), got = _ffn_fwd(x, g1, w["ffn1_w1"], w["ffn1_w3"], w["ffn1_w2"], "ffn1_fwd",
                                         gather(mix_first))
    gathered(mix_first, got)
    wup = w["gla_a_up_w"].astype(F32)
    (u, s5in, q, k, v, r, alow, gs5, ggla), got = _mix_pre_fwd(h1, gm, w["w_in"], gather(mix_rest))
    gathered(mix_rest, got)
    ar, ai, bbr, bbi = _s5_disc(lre, lim, ldt, bre, bim)
    bd = _s5_dense(bbr, bbi, 1.0)
    cd = _s5_dense(cre, cim, -1.0)
    bd16, cd16 = bd.astype(BF16), cd.astype(BF16)
    bdt16, ctd16 = bd16.transpose(0, 2, 1), cd16.transpose(0, 2, 1)
    ar4 = ar.reshape(S5_BLOCKS, 1, S5_BSTATE)
    ai4 = ai.reshape(S5_BLOCKS, 1, S5_BSTATE)
    (xs, y), got = _s5_fwd(s5in, bd16, ctd16, ar4, ai4, dskip, gather(GROUPS[2][:1]))
    gathered(GROUPS[2][:1], got)
    (o, ssave), got = _gla_fwd(q, k, v, alow, wup, bup, gather(GROUPS[2][1:2]))
    gathered(GROUPS[2][1:2], got)
    post_w = (w["s5_glu_w"], bg, gn, w["proj_s5"], w["proj_gla"], w["w_out"])
    h2, got = _mix_post_fwd(y, o, r, gs5, ggla, h1, *post_w, carry=gather(GROUPS[2][2:]))
    gathered(GROUPS[2][2:], got)
    loss, dh3, dgf, (a3_2, b3_2, n2) = _ffn_fwd_head(h2, g2, w["ffn2_w1"], w["ffn2_w3"], w["ffn2_w2"], gf, target,
                                                     "ffn2_fwd")

    big, small = {}, {}
    small["final_norm"] = dgf.reshape(D_MODEL)
    (dh2, dg2, da3, db3, s3, dhh2), _ = _ffn_bwd(
        h2, dh3, g2, a3_2, b3_2, w["ffn2_w1"], w["ffn2_w3"], w["ffn2_w2"], "ffn2_bwd")
    small["ffn2_norm"] = dg2
    big["ffn2_w1"] = _mm_tn(da3, n2, "ffn2_dw1")
    big["ffn2_w3"] = _mm_tn(db3, n2, "ffn2_dw3")
    big["ffn2_w2"] = _mm_tn(s3, dhh2, "ffn2_dw2")
    (dy, do, dr, dgs5, dggla, dbg, dgn, z5b, dgpb, ys5b, dm5b, yglab, dmgb, mergedb, dh2b), got = _mix_post_bwd(
        y, o, r, gs5, ggla, dh2, *post_w, carry=scatter(GROUPS[2][:1]))
    scattered(GROUPS[2][:1], got)
    small["s5_glu_b"] = dbg
    small["gla_out_norm"] = dgn
    big["s5_glu_w"] = _mm_tn(z5b, dgpb, "glu_dw")
    big["proj_s5"] = _mm_tn(dm5b, ys5b, "proj_s5_dw")
    big["proj_gla"] = _mm_tn(dmgb, yglab, "proj_gla_dw")
    big["w_out"] = _mm_tn(mergedb, dh2b, "w_out_dw")
    (dq, dk, dv, dalow, dwup, dbup), got = _gla_bwd(q, k, v, alow, wup, bup, ssave, do, scatter(GROUPS[2][1:2]))
    scattered(GROUPS[2][1:2], got)
    big["gla_a_up_w"] = dwup
    small["gla_a_up_b"] = dbup
    (ds5in, dbd, dcd, dd, dar4, dai4), got = _s5_bwd(
        dy, s5in, xs, cd16, bdt16, ar4, ai4, dskip, scatter(GROUPS[2][2:]))
    scattered(GROUPS[2][2:], got)
    dbbr, dbbi = _s5_undense(dbd)
    dcre, dcim_neg = _s5_undense(dcd)
    glre, glim, gldt, gbre, gbim = _s5_disc_bwd(
        lre, lim, ldt, bre, bim, dar4.reshape(S5_GROUPS, S5_STATE), dai4.reshape(S5_GROUPS, S5_STATE),
        dbbr, dbbi)
    small["s5_lambda_re"] = glre[None]
    small["s5_lambda_im"] = glim[None]
    small["s5_log_dt"] = gldt.reshape(1, S5_GROUPS)
    small["s5_b_re"] = gbre
    small["s5_b_im"] = gbim
    small["s5_c_re"] = dcre[None]
    small["s5_c_im"] = -dcim_neg[None]
    small["s5_d"] = dd.reshape(1, S5_GROUPS, S5_GROUP)
    dz = (ds5in, dq, dk, dv, dr, dalow, dgs5, dggla)
    (dh1, dgm), got = _mix_pre_bwd(h1, gm, w["w_in"], dh2, dz, scatter(mix_rest[:3]))
    scattered(mix_rest[:3], got)
    small["mix_norm"] = dgm
    wide = _mm_tn_many(dz[:5] + dz[6:], u, "w_in_dw")
    low_at = IN_OFFS[5]
    big["w_in"] = jnp.concatenate([wide[:low_at], _mm_tn(dalow, u, "w_in_dw_low"), wide[low_at:]], axis=0)
    (dx, dg1, da3, db3, s3, dhh1), got = _ffn_bwd(
        x, dh1, g1, a3_1, b3_1, w["ffn1_w1"], w["ffn1_w3"], w["ffn1_w2"], "ffn1_bwd",
        scatter(mix_first + mix_rest[3:]))
    scattered(mix_first + mix_rest[3:], got)
    small["ffn1_norm"] = dg1
    if rows is None:
        big["ffn1_w1"] = _mm_tn(da3, n1, "ffn1_dw1")
        big["ffn1_w3"] = _mm_tn(db3, n1, "ffn1_dw3")
        big["ffn1_w2"] = _mm_tn(s3, dhh1, "ffn1_dw2")
        return loss[0, 0], dx, big, small
    part = _pack_small(small, loss).reshape(N_DEV, SMALL_R // N_DEV, 1024)
    big["ffn1_w1"], (small_landed,) = _mm_tn(da3, n1, "ffn1_dw1", ([part], "scatter"))
    small_mine = _sum_slabs(small_landed, "sum_small")
    slabs1 = _grad_slabs("ffn1_w1", big["ffn1_w1"])
    big["ffn1_w3"], (from_pair, small_all) = _mm_tn(db3, n1, "ffn1_dw3",
                                                    [([slabs1], "pair"), ([small_mine], "gather")])
    small = small_all.reshape(SMALL_R, 1024)
    sums1 = _pair_add(slabs1, from_pair, "ffn1_w1_pair")
    slabs3 = _grad_slabs("ffn1_w3", big["ffn1_w3"])
    big["ffn1_w2"], (landed1, from_pair) = _mm_tn(s3, dhh1, "ffn1_dw2", [([sums1], "chips"), ([slabs3], "pair")])
    sums3 = _pair_add(slabs3, from_pair, "ffn1_w3_pair")
    slabs2 = _grad_slabs("ffn1_w2", big["ffn1_w2"])

    def sum_adamw(names, lands, name, carry=None):
        outs, got = _sum_adamw(lands, *([opt[n][j] for n in names] for j in range(3)), name, carry)
        for i, n in enumerate(names):
            updated[n] = outs[i::len(names)]
        return got

    updated = {}
    landed3, from_pair = sum_adamw(GROUPS[2], [landed_grads.pop(n) for n in GROUPS[2]], "adamw_ffn2",
                                   [([sums3], "chips"), ([slabs2], "pair")])
    sums2 = _pair_add(slabs2, from_pair, "ffn1_w2_pair")
    (landed2,) = sum_adamw(GROUPS[0][:2], [landed1, landed3], "adamw_ffn1_w13", ([sums2], "chips"))
    sum_adamw(GROUPS[0][2:], [landed2], "adamw_ffn1_w2")
    return loss[0, 0], dx, landed_grads, small, updated


NAMES = ("ffn1_norm", "ffn1_w1", "ffn1_w3", "ffn1_w2", "mix_norm", "w_in", "s5_lambda_re", "s5_lambda_im",
         "s5_log_dt", "s5_b_re", "s5_b_im", "s5_c_re", "s5_c_im", "s5_d", "s5_glu_w", "s5_glu_b", "gla_a_up_w",
         "gla_a_up_b", "gla_out_norm", "proj_s5", "proj_gla", "w_out", "ffn2_norm", "ffn2_w1", "ffn2_w3", "ffn2_w2",
         "final_norm")


def kernel(*args):
    nw = len(NAMES)
    x = args[0][0]
    wts = dict(zip(NAMES, args[1:1 + nw]))
    target = args[1 + nw][0]
    mom = dict(zip(NAMES, args[2 + nw:2 + 2 * nw]))
    var = dict(zip(NAMES, args[2 + 2 * nw:2 + 3 * nw]))

    shards = {n: wts[n][0] for n in BIG}
    rows = {n: _shard_rows(n, shards[n]).astype(BF16) for n in BIG}
    def row_layout(n, a):
        return a.T if n in ROW_ADAM else a

    opt = {n: tuple(row_layout(n, d[n][0]) for d in (wts, mom, var)) for n in GROUPS[0] + GROUPS[2]}
    _, dx, landed, small_slab, updated = _local_step(x, target, {n: wts[n] for n in SMALL}, None, rows, opt)
    loss = small_slab.reshape(-1)[SMALL_N]
    g_small = _unpack_small(small_slab)

    grad, delta, new_m, new_v = {}, {}, {}, {}
    for n, arrays in updated.items():
        grad[n], delta[n], new_m[n], new_v[n] = (row_layout(n, a)[None] for a in arrays)
    for n in GROUPS[1]:
        g_rows = _sum_slabs(landed[n], "sum_" + n)
        if n in ROW_ADAM:
            g = g_rows[:W_IN_ROWS] if n == "w_in" else g_rows
            outs = _adamw(shards[n].T, g, mom[n][0].T, var[n][0].T, "adamw_" + n)
            grad[n], delta[n], new_m[n], new_v[n] = (a.T[None] for a in (g, *outs))
        else:
            g = _unshard_rows(n, g_rows, shards[n].shape)
            outs = _adamw(shards[n], g, mom[n][0], var[n][0], "adamw_" + n)
            grad[n], delta[n], new_m[n], new_v[n] = (a[None] for a in (g, *outs))

    def flat2d(a):
        return a.reshape(-1, a.shape[-1])

    operands = ([flat2d(_working(n, d[n])) for n in SMALL] for d in (wts, mom, var))
    w2d, m2d, v2d = operands
    outs = _adamw_many(w2d, [flat2d(g_small[n]) for n in SMALL], m2d, v2d, "adamw_small")
    for out, arrays in zip((grad, delta, new_m, new_v), ([g_small[n] for n in SMALL], *outs)):
        out.update({n: _declared(n, a.reshape(g_small[n].shape)) for n, a in zip(SMALL, arrays)})
    return (loss, dx[None], *(d[n] for d in (grad, delta, new_m, new_v) for n in NAMES))
```

```python
import functools
import math

import jax
import jax.numpy as jnp
from jax import lax
from jax.experimental import pallas as pl
from jax.experimental.pallas import tpu as pltpu

F32, BF16 = jnp.float32, jnp.bfloat16
HIGHEST = lax.Precision.HIGHEST

D_MODEL = 1024
D_FF = 2816
N_DEV = 8
S5_WIDTH, S5_GROUPS, S5_GROUP, S5_STATE = 512, 32, 16, 64
S5_BLOCKS = 4
S5_BSTATE = 512
S5_SEGS = 8
GLA_HEADS, GLA_DK, GLA_DV = 4, 64, 128
GLA_KEY, GLA_VAL, GLA_RANK, GLA_CHUNK = 256, 512, 16, 64
GLA_TAU = 16.0
GLA_STEP_CHUNKS = 4
EPS = 1e-6
IN_SIZES = (512, 256, 256, 512, 512, 16, 1024, 1024)
IN_OFFS = tuple(sum(IN_SIZES[:i]) for i in range(len(IN_SIZES)))
IN_COLS = sum(IN_SIZES)
ADAM_LR, ADAM_B1, ADAM_B2, ADAM_EPS, ADAM_WD, ADAM_STEP = 0.001, 0.9, 0.999, 1e-08, 0.01, 10
GELU_C0 = math.sqrt(2.0 / math.pi)
GELU_C1 = 0.044715

FFN_FT = 256
VMEM_LIMIT_BYTES = 56 * 1024 * 1024

VMEM_FULL = pl.BlockSpec(memory_space=pltpu.VMEM)
ANY = pl.BlockSpec(memory_space=pl.ANY)


def _cparams(n_grid):
    return pltpu.CompilerParams(dimension_semantics=("arbitrary",) * n_grid, vmem_limit_bytes=VMEM_LIMIT_BYTES)


def _tile(t):
    return 512 if t >= 1024 else t // 2


def _nn(a, b):
    return jnp.dot(a, b, preferred_element_type=F32)


def _nt(a, b):
    return lax.dot_general(a, b, (((1,), (1,)), ((), ())), preferred_element_type=F32)


def _tn(a, b):
    return lax.dot_general(a, b, (((0,), (0,)), ((), ())), preferred_element_type=F32)


def _rms_parts(x):
    r = lax.rsqrt(jnp.mean(x * x, axis=-1, keepdims=True) + EPS)
    return x * r, r


def _rms_bwd(dn, g, xhat, r):
    dxh = dn * g
    dx = r * (dxh - xhat * jnp.mean(dxh * xhat, axis=-1, keepdims=True))
    return dx, jnp.sum(dn * xhat, axis=0, keepdims=True)


def _peers():
    x, y, c = lax.axis_index("x"), lax.axis_index("y"), lax.axis_index("c")
    out = []
    for k in range(1, N_DEV):
        px = 1 - x if k & 4 else x
        py = 1 - y if k & 2 else y
        pc = 1 - c if k & 1 else c
        out.append(((px, py, pc), 4 * px + 2 * py + pc))
    return 4 * x + 2 * y + c, out


def _exchange_copies(src_refs, out_refs, send_sems, recv_sems, local_sems, scatter, with_recvs):
    me, peers = _peers()
    locals_, sends, recvs = [], [], []
    for a, (src_ref, out_ref) in enumerate(zip(src_refs, out_refs)):
        def mine(idx, src_ref=src_ref):
            return src_ref.at[idx] if scatter else src_ref

        locals_.append(pltpu.make_async_copy(mine(me), out_ref.at[me], local_sems.at[a]))
        for k, (dev, idx) in enumerate(peers):
            sends.append(pltpu.make_async_remote_copy(
                src_ref=mine(idx), dst_ref=out_ref.at[me], send_sem=send_sems.at[a, k], recv_sem=recv_sems.at[a, k],
                device_id=dev, device_id_type=pl.DeviceIdType.MESH))
            if with_recvs:
                recvs.append(pltpu.make_async_remote_copy(
                    src_ref=mine(idx), dst_ref=out_ref.at[idx], send_sem=send_sems.at[a, k],
                    recv_sem=recv_sems.at[a, k], device_id=dev, device_id_type=pl.DeviceIdType.MESH))
    return locals_, sends, recvs


def _remote(src, dst, send_sems, recv_sems, a, k, dev):
    return pltpu.make_async_remote_copy(src_ref=src, dst_ref=dst, send_sem=send_sems.at[a, k],
                                        recv_sem=recv_sems.at[a, k], device_id=dev,
                                        device_id_type=pl.DeviceIdType.MESH)


def _gather_places():
    x, y, c = lax.axis_index("x"), lax.axis_index("y"), lax.axis_index("c")
    chips = [(1 - x, y), (x, 1 - y), (1 - x, 1 - y)]
    sibling = (x, y, 1 - c)
    me_idx, sib_idx = 4 * x + 2 * y + c, 4 * x + 2 * y + 1 - c
    same_core = [((cx, cy, c), 4 * cx + 2 * cy + c) for cx, cy in chips]
    other_core_idx = [4 * cx + 2 * cy + 1 - c for cx, cy in chips]
    return sibling, me_idx, sib_idx, same_core, other_core_idx


def _gather_start(src_refs, out_refs, send_sems, recv_sems, local_sems):
    sibling, me_idx, _, same_core, _ = _gather_places()
    for a, (src, out) in enumerate(zip(src_refs, out_refs)):
        pltpu.make_async_copy(src, out.at[me_idx], local_sems.at[a]).start()
        _remote(src, out.at[me_idx], send_sems, recv_sems, a, 0, sibling).start()
        for j, (dev, _) in enumerate(same_core):
            _remote(src, out.at[me_idx], send_sems, recv_sems, a, 1 + j, dev).start()


def _gather_forward(src_refs, out_refs, send_sems, recv_sems, local_sems):
    sibling, _, _, same_core, _ = _gather_places()
    for a, (src, out) in enumerate(zip(src_refs, out_refs)):
        for j, (dev, idx) in enumerate(same_core):
            _remote(src, out.at[idx], send_sems, recv_sems, a, 1 + j, dev).wait_recv()
            _remote(out.at[idx], out.at[idx], send_sems, recv_sems, a, 4 + j, sibling).start()


def _gather_finish(src_refs, out_refs, send_sems, recv_sems, local_sems):
    sibling, me_idx, sib_idx, same_core, other_core_idx = _gather_places()
    for a, (src, out) in enumerate(zip(src_refs, out_refs)):
        _remote(src, out.at[sib_idx], send_sems, recv_sems, a, 0, sibling).wait_recv()
        for j, idx in enumerate(other_core_idx):
            _remote(src, out.at[idx], send_sems, recv_sems, a, 4 + j, sibling).wait_recv()
        _remote(src, out.at[me_idx], send_sems, recv_sems, a, 0, sibling).wait_send()
        for j, (dev, idx) in enumerate(same_core):
            _remote(src, out.at[me_idx], send_sems, recv_sems, a, 1 + j, dev).wait_send()
            _remote(out.at[idx], out.at[idx], send_sems, recv_sems, a, 4 + j, sibling).wait_send()
        pltpu.make_async_copy(src, out.at[me_idx], local_sems.at[a]).wait()


def _exchange_start(*refs, scatter):
    locals_, sends, _ = _exchange_copies(*refs, scatter=scatter, with_recvs=False)
    for cp in locals_ + sends:
        cp.start()


def _exchange_wait(*refs, scatter):
    locals_, sends, recvs = _exchange_copies(*refs, scatter=scatter, with_recvs=True)
    for cp in recvs:
        cp.wait_recv()
    for cp in sends:
        cp.wait_send()
    for cp in locals_:
        cp.wait()


def _halves_places():
    x, y, c = lax.axis_index("x"), lax.axis_index("y"), lax.axis_index("c")
    flips = [(1 - x, y), (x, 1 - y), (1 - x, 1 - y)]
    return (x, y, 1 - c), c, 2 * x + y, [((fx, fy, c), 2 * fx + fy) for fx, fy in flips]


def _pair_start(src_refs, out_refs, send_sems, recv_sems, local_sems):
    sibling, c, _, _ = _halves_places()
    for a, (src, out) in enumerate(zip(src_refs, out_refs)):
        for i in range(4):
            _remote(src.at[2 * i + 1 - c], out.at[i], send_sems, recv_sems, a, i, sibling).start()


def _pair_finish(src_refs, out_refs, send_sems, recv_sems, local_sems):
    sibling, c, _, _ = _halves_places()
    for a, (src, out) in enumerate(zip(src_refs, out_refs)):
        for i in range(4):
            _remote(src.at[2 * i + 1 - c], out.at[i], send_sems, recv_sems, a, i, sibling).wait()


def _chips_start(src_refs, out_refs, send_sems, recv_sems, local_sems):
    _, _, chip, others = _halves_places()
    for a, (src, out) in enumerate(zip(src_refs, out_refs)):
        pltpu.make_async_copy(src.at[chip], out.at[chip], local_sems.at[a]).start()
        for k, (dev, their_chip) in enumerate(others):
            _remote(src.at[their_chip], out.at[chip], send_sems, recv_sems, a, k, dev).start()


def _chips_finish(src_refs, out_refs, send_sems, recv_sems, local_sems):
    _, _, chip, others = _halves_places()
    for a, (src, out) in enumerate(zip(src_refs, out_refs)):
        for k, (dev, their_chip) in enumerate(others):
            _remote(src.at[their_chip], out.at[their_chip], send_sems, recv_sems, a, k, dev).wait_recv()
        for k, (dev, their_chip) in enumerate(others):
            _remote(src.at[their_chip], out.at[chip], send_sems, recv_sems, a, k, dev).wait_send()
        pltpu.make_async_copy(src.at[chip], out.at[chip], local_sems.at[a]).wait()


EXCHANGES = {
    "gather": (_gather_start, _gather_forward, _gather_finish, N_DEV, False),
    "scatter": (functools.partial(_exchange_start, scatter=True), None, functools.partial(_exchange_wait, scatter=True),
                N_DEV, True),
    "pair": (_pair_start, None, _pair_finish, 4, True),
    "chips": (_chips_start, None, _chips_finish, 4, True),
}


def _exchange_sems(n_arrays):
    return [pltpu.SemaphoreType.DMA((n_arrays, N_DEV - 1)), pltpu.SemaphoreType.DMA((n_arrays, N_DEV - 1)),
            pltpu.SemaphoreType.DMA((n_arrays,))]


def _exchange_shapes(srcs, kind):
    lead, slabbed = EXCHANGES[kind][3:]
    return [jax.ShapeDtypeStruct((lead,) + tuple(s.shape[1:] if slabbed else s.shape), s.dtype) for s in srcs]


def _carries(carry):
    if carry is None:
        return []
    return [carry] if isinstance(carry, tuple) else list(carry)


def _call(body, *, name, grid, in_specs, out_specs, out_shape, args, scratch_shapes=(), carry=None):
    n_in, n_out, n_scr = len(in_specs), len(out_specs), len(scratch_shapes)
    groups = _carries(carry)
    sizes = [len(arrays) for arrays, _ in groups]
    nc = sum(sizes)

    def wrapped(*refs):
        ins, refs = refs[:n_in], refs[n_in:]
        csrc, refs = refs[:nc], refs[nc:]
        outs, refs = refs[:n_out], refs[n_out:]
        cland, refs = refs[:nc], refs[nc:]
        scr, sems = refs[:n_scr], refs[n_scr:]

        def run(phase):
            at = 0
            for gi, ((_, kind), size) in enumerate(zip(groups, sizes)):
                if EXCHANGES[kind][phase] is not None:
                    EXCHANGES[kind][phase](csrc[at:at + size], cland[at:at + size], *sems[3 * gi:3 * gi + 3])
                at += size

        last = pl.program_id(0) == grid[0] - 1
        if nc:
            pl.when(pl.program_id(0) == 0)(functools.partial(run, 0))
            pl.when(last)(functools.partial(run, 1))
        if body is not None:
            body(*ins, *outs, *scr)
        if nc:
            pl.when(last)(functools.partial(run, 2))

    res = pl.pallas_call(
        wrapped, name=name, grid=grid,
        in_specs=list(in_specs) + [ANY] * nc, out_specs=list(out_specs) + [ANY] * nc,
        out_shape=list(out_shape) + [s for arrays, kind in groups for s in _exchange_shapes(arrays, kind)],
        scratch_shapes=list(scratch_shapes) + [s for size in sizes for s in _exchange_sems(size)],
        compiler_params=_cparams(1),
    )(*args, *[a for arrays, _ in groups for a in arrays])
    return res[:n_out], res[n_out:]


def _row_tile(tm, d):
    return pl.BlockSpec((tm, d), lambda i: (i, 0))


def _acc_row(d):
    return pl.BlockSpec((1, d), lambda i: (0, 0))


def _ffn_body(x_ref, g_ref, w1_ref, w3_ref, w2_ref, acc_ref, a_ref, b_ref, n_ref):
    xv = x_ref[...]
    xhat, _ = _rms_parts(xv)
    n = (xhat * g_ref[...]).astype(BF16)
    n_ref[...] = n
    acc_ref[...] = xv

    def fstep(f, c):
        rows = pl.ds(pl.multiple_of(f * FFN_FT, FFN_FT), FFN_FT)
        a = _nt(n, w1_ref[rows, :])
        b = _nt(n, w3_ref[rows, :])
        a_ref[f] = a.astype(BF16)
        b_ref[f] = b.astype(BF16)
        s = (a * jax.nn.sigmoid(a) * b).astype(BF16)
        acc_ref[...] += 0.5 * _nn(s, w2_ref[rows, :])
        return c

    lax.fori_loop(0, D_FF // FFN_FT, fstep, 0, unroll=True)


def _ffn_fwd(x, g, w1t, w3t, w2, name, carry=None):
    t = x.shape[0]
    tm = _tile(t)
    nf = D_FF // FFN_FT
    blk3 = pl.BlockSpec((nf, tm, FFN_FT), lambda i: (0, i, 0))
    sh3 = jax.ShapeDtypeStruct((nf, t, FFN_FT), BF16)
    (h, a3, b3, n), landed = _call(
        functools.partial(_ffn_body), name=name, grid=(t // tm,),
        in_specs=[_row_tile(tm, D_MODEL), _acc_row(D_MODEL), VMEM_FULL, VMEM_FULL, VMEM_FULL],
        out_specs=[_row_tile(tm, D_MODEL), blk3, blk3, _row_tile(tm, D_MODEL)],
        out_shape=[jax.ShapeDtypeStruct((t, D_MODEL), F32), sh3, sh3, jax.ShapeDtypeStruct((t, D_MODEL), BF16)],
        args=(x, g, w1t, w3t, w2), carry=carry)
    return h, (a3, b3, n), landed


def _ffn_fwd_head(x, g, w1t, w3t, w2, gf, target, name):
    t = x.shape[0]
    tm = _tile(t)
    nf = D_FF // FFN_FT

    def body(x_ref, g_ref, w1_ref, w3_ref, w2_ref, gf_ref, t_ref, loss_ref, dh_ref, dgf_ref, a_ref, b_ref, n_ref, acc):
        _ffn_body(x_ref, g_ref, w1_ref, w3_ref, w2_ref, acc, a_ref, b_ref, n_ref)
        _head_math(acc[...], gf_ref[...], t_ref[...], loss_ref, dh_ref, dgf_ref)

    blk3 = pl.BlockSpec((nf, tm, FFN_FT), lambda i: (0, i, 0))
    sh3 = jax.ShapeDtypeStruct((nf, t, FFN_FT), BF16)
    (loss, dh, dgf, a3, b3, n), _ = _call(
        body, name=name, grid=(t // tm,),
        in_specs=[_row_tile(tm, D_MODEL), _acc_row(D_MODEL), VMEM_FULL, VMEM_FULL, VMEM_FULL, _acc_row(D_MODEL),
                  _row_tile(tm, D_MODEL)],
        out_specs=[pl.BlockSpec((1, 1), lambda i: (0, 0)), _row_tile(tm, D_MODEL), _acc_row(D_MODEL), blk3, blk3,
                   _row_tile(tm, D_MODEL)],
        out_shape=[jax.ShapeDtypeStruct((1, 1), F32), jax.ShapeDtypeStruct((t, D_MODEL), F32),
                   jax.ShapeDtypeStruct((1, D_MODEL), F32), sh3, sh3, jax.ShapeDtypeStruct((t, D_MODEL), BF16)],
        scratch_shapes=[pltpu.VMEM((tm, D_MODEL), F32)],
        args=(x, g, w1t, w3t, w2, gf, target))
    return loss, dh, dgf, (a3, b3, n)


def _head_math(h, gv, target, loss_ref, dh_ref, dg_ref):
    i = pl.program_id(0)
    xhat, r = _rms_parts(h)
    err = xhat * gv - target
    dx, dg = _rms_bwd(err * (1.0 / D_MODEL), gv, xhat, r)
    dh_ref[...] = dx

    @pl.when(i == 0)
    def _():
        loss_ref[...] = jnp.zeros_like(loss_ref)
        dg_ref[...] = jnp.zeros_like(dg_ref)

    loss_ref[...] += (0.5 / D_MODEL) * jnp.sum(jnp.sum(err * err, axis=1, keepdims=True), axis=0, keepdims=True)
    dg_ref[...] += dg


def _ffn_bwd(x, dh, g, a3, b3, w1t, w3t, w2, name, carry=None):
    t = x.shape[0]
    tm = _tile(t) // 2
    nf = D_FF // FFN_FT

    def body(x_ref, dh_ref, g_ref, a_ref, b_ref, w1_ref, w3_ref, w2_ref,
             dx_ref, dg_ref, da_ref, db_ref, s_ref, dhh_ref, dn_acc):
        i = pl.program_id(0)
        xv = x_ref[...]
        gv = g_ref[...]
        xhat, r = _rms_parts(xv)
        dhv = dh_ref[...]
        dhh = (0.5 * dhv).astype(BF16)
        dhh_ref[...] = dhh
        dn_acc[...] = jnp.zeros_like(dn_acc)

        def fstep(f, c):
            rows = pl.ds(pl.multiple_of(f * FFN_FT, FFN_FT), FFN_FT)
            w1c, w3c, w2c = w1_ref[rows, :], w3_ref[rows, :], w2_ref[rows, :]
            a = a_ref[f].astype(F32)
            b = b_ref[f].astype(F32)
            sg = jax.nn.sigmoid(a)
            sl = a * sg
            ds = _nt(dhh, w2c)
            da = (ds * b * sg * (1.0 + a * (1.0 - sg))).astype(BF16)
            db = (ds * sl).astype(BF16)
            s_ref[f] = (sl * b).astype(BF16)
            da_ref[f] = da
            db_ref[f] = db
            return c

        def nstep(f, c):
            rows = pl.ds(pl.multiple_of(f * FFN_FT, FFN_FT), FFN_FT)
            dn_acc[...] += _nn(da_ref[f], w1_ref[rows, :]) + _nn(db_ref[f], w3_ref[rows, :])
            return c

        lax.fori_loop(0, nf, fstep, 0, unroll=True)
        lax.fori_loop(0, nf, nstep, 0, unroll=True)
        dx, dg = _rms_bwd(dn_acc[...], gv, xhat, r)
        dx_ref[...] = dhv + dx

        @pl.when(i == 0)
        def _():
            dg_ref[...] = jnp.zeros_like(dg_ref)

        dg_ref[...] += dg

    blk3 = pl.BlockSpec((nf, tm, FFN_FT), lambda i: (0, i, 0))
    sh3 = jax.ShapeDtypeStruct((nf, t, FFN_FT), BF16)
    return _call(
        body, name=name, grid=(t // tm,),
        in_specs=[_row_tile(tm, D_MODEL), _row_tile(tm, D_MODEL), _acc_row(D_MODEL), blk3, blk3,
                  VMEM_FULL, VMEM_FULL, VMEM_FULL],
        out_specs=[_row_tile(tm, D_MODEL), _acc_row(D_MODEL), blk3, blk3, blk3, _row_tile(tm, D_MODEL)],
        out_shape=[jax.ShapeDtypeStruct((t, D_MODEL), F32), jax.ShapeDtypeStruct((1, D_MODEL), F32), sh3, sh3, sh3,
                   jax.ShapeDtypeStruct((t, D_MODEL), BF16)],
        scratch_shapes=[pltpu.VMEM((tm, D_MODEL), F32)],
        args=(x, dh, g, a3, b3, w1t, w3t, w2), carry=carry)


def _mm_tn(a, b, name, carry=None):
    t, n = b.shape
    kc = min(512, t)
    if a.ndim == 3:
        nb, _, tb = a.shape
        a_spec = pl.BlockSpec((1, t, tb), lambda i: (i, 0, 0))
    else:
        m = a.shape[1]
        tb = min(m, 256)
        nb = m // tb
        a_spec = pl.BlockSpec((t, tb), lambda i: (0, i))
    three_d = a.ndim == 3

    def body(a_ref, b_ref, o_ref, acc):
        acc[...] = jnp.zeros_like(acc)

        def kstep(k, c):
            rows = pl.ds(pl.multiple_of(k * kc, kc), kc)
            av = a_ref[0, rows, :] if three_d else a_ref[rows, :]
            acc[...] += _tn(av.astype(BF16), b_ref[rows, :])
            return c

        lax.fori_loop(0, t // kc, kstep, 0, unroll=True)
        o_ref[...] = acc[...].astype(BF16)

    (out,), landed = _call(
        body, name=name, grid=(nb,),
        in_specs=[a_spec, VMEM_FULL],
        out_specs=[pl.BlockSpec((tb, n), lambda i: (i, 0))],
        out_shape=[jax.ShapeDtypeStruct((nb * tb, n), BF16)],
        scratch_shapes=[pltpu.VMEM((tb, n), F32)],
        args=(a, b), carry=carry)
    return (out, landed) if carry is not None else out


MM_TB = 256


def _mm_tn_many(arrays, b, name):
    t, n = b.shape
    kc = min(512, t)
    counts = [a.shape[1] // MM_TB for a in arrays]
    starts = [sum(counts[:k]) for k in range(len(arrays))]

    def spec(start, count):
        return pl.BlockSpec((t, MM_TB), lambda i: (0, jnp.clip(i - start, 0, count - 1)))

    def body(*refs):
        a_refs, (b_ref, o_ref, acc) = refs[:len(arrays)], refs[len(arrays):]
        i = pl.program_id(0)
        for a_ref, start, count in zip(a_refs, starts, counts):
            @pl.when((i >= start) & (i < start + count))
            def _(a_ref=a_ref):
                acc[...] = jnp.zeros_like(acc)

                def kstep(k, c):
                    rows = pl.ds(pl.multiple_of(k * kc, kc), kc)
                    acc[...] += _tn(a_ref[rows, :].astype(BF16), b_ref[rows, :])
                    return c

                lax.fori_loop(0, t // kc, kstep, 0, unroll=True)
                o_ref[...] = acc[...].astype(BF16)

    return pl.pallas_call(
        body, name=name, grid=(sum(counts),),
        in_specs=[spec(s, c) for s, c in zip(starts, counts)] + [VMEM_FULL],
        out_specs=pl.BlockSpec((MM_TB, n), lambda i: (i, 0)),
        out_shape=jax.ShapeDtypeStruct((sum(counts) * MM_TB, n), BF16),
        scratch_shapes=[pltpu.VMEM((MM_TB, n), F32)],
        compiler_params=_cparams(1),
    )(*arrays, b)


def _mix_pre_fwd(h, g, wint, carry=None):
    t = h.shape[0]
    tm = _tile(t)

    def body(h_ref, g_ref, w_ref, u_ref, *outs):
        xhat, _ = _rms_parts(h_ref[...])
        u = (xhat * g_ref[...]).astype(BF16)
        u_ref[...] = u
        for o_ref, off, size in zip(outs, IN_OFFS, IN_SIZES):
            o_ref[...] = _nt(u, w_ref[off:off + size, :])

    return _call(
        body, name="mix_pre_fwd", grid=(t // tm,),
        in_specs=[_row_tile(tm, D_MODEL), _acc_row(D_MODEL), VMEM_FULL],
        out_specs=[_row_tile(tm, D_MODEL)] + [_row_tile(tm, s) for s in IN_SIZES],
        out_shape=[jax.ShapeDtypeStruct((t, D_MODEL), BF16)] + [jax.ShapeDtypeStruct((t, s), F32) for s in IN_SIZES],
        args=(h, g, wint), carry=carry)


def _mix_pre_bwd(h, g, wint, dh2, dz, carry=None):
    t = h.shape[0]
    tm = _tile(t)

    def body(h_ref, g_ref, w_ref, dh2_ref, *rest):
        dz_refs, (dh1_ref, dg_ref) = rest[:len(IN_SIZES)], rest[len(IN_SIZES):]
        i = pl.program_id(0)
        gv = g_ref[...]
        xhat, r = _rms_parts(h_ref[...])
        du = jnp.zeros((tm, D_MODEL), F32)
        for dz_ref, off, size in zip(dz_refs, IN_OFFS, IN_SIZES):
            du = du + _nn(dz_ref[...].astype(BF16), w_ref[off:off + size, :])
        dx, dg = _rms_bwd(du, gv, xhat, r)
        dh1_ref[...] = dh2_ref[...] + dx

        @pl.when(i == 0)
        def _():
            dg_ref[...] = jnp.zeros_like(dg_ref)

        dg_ref[...] += dg

    return _call(
        body, name="mix_pre_bwd", grid=(t // tm,),
        in_specs=[_row_tile(tm, D_MODEL), _acc_row(D_MODEL), VMEM_FULL, _row_tile(tm, D_MODEL)]
        + [_row_tile(tm, s) for s in IN_SIZES],
        out_specs=[_row_tile(tm, D_MODEL), _acc_row(D_MODEL)],
        out_shape=[jax.ShapeDtypeStruct((t, D_MODEL), F32), jax.ShapeDtypeStruct((1, D_MODEL), F32)],
        args=(h, g, wint, dh2, *dz), carry=carry)


def _disc_math(lre, lim, ldt, bre, bim):
    dt = jnp.exp(ldt)
    mag = jnp.exp(lre * dt)
    ar = mag * jnp.cos(lim * dt)
    ai = mag * jnp.sin(lim * dt)
    den = lre * lre + lim * lim
    nr = ar - 1.0
    fr = (nr * lre + ai * lim) / den
    fi = (ai * lre - nr * lim) / den
    fr, fi = fr[:, None, :], fi[:, None, :]
    return ar, ai, fr * bre - fi * bim, fr * bim + fi * bre


def _s5_disc(lre, lim, ldt, bre, bim):
    def body(lre_ref, lim_ref, ldt_ref, bre_ref, bim_ref, ar_ref, ai_ref, bbr_ref, bbi_ref):
        ar, ai, bbr, bbi = _disc_math(lre_ref[...], lim_ref[...], ldt_ref[...], bre_ref[...], bim_ref[...])
        ar_ref[...] = ar
        ai_ref[...] = ai
        bbr_ref[...] = bbr
        bbi_ref[...] = bbi

    small = jax.ShapeDtypeStruct(lre.shape, F32)
    big = jax.ShapeDtypeStruct(bre.shape, F32)
    return pl.pallas_call(body, name="s5_disc", out_shape=[small, small, big, big],
                          in_specs=[VMEM_FULL] * 5, out_specs=[VMEM_FULL] * 4)(lre, lim, ldt, bre, bim)


def _s5_disc_bwd(lre, lim, ldt, bre, bim, dar, dai, dbbr, dbbi):
    def body(lre_ref, lim_ref, ldt_ref, bre_ref, bim_ref, dar_ref, dai_ref, dbbr_ref, dbbi_ref,
             glre_ref, glim_ref, gldt_ref, gbre_ref, gbim_ref):
        _, vjp = jax.vjp(_disc_math, lre_ref[...], lim_ref[...], ldt_ref[...], bre_ref[...], bim_ref[...])
        glre, glim, gldt, gbre, gbim = vjp((dar_ref[...], dai_ref[...], dbbr_ref[...], dbbi_ref[...]))
        glre_ref[...] = glre
        glim_ref[...] = glim
        gldt_ref[...] = gldt
        gbre_ref[...] = gbre
        gbim_ref[...] = gbim

    small = jax.ShapeDtypeStruct(lre.shape, F32)
    big = jax.ShapeDtypeStruct(bre.shape, F32)
    return pl.pallas_call(body, name="s5_disc_bwd",
                          out_shape=[small, small, jax.ShapeDtypeStruct(ldt.shape, F32), big, big],
                          in_specs=[VMEM_FULL] * 9, out_specs=[VMEM_FULL] * 5,
                          )(lre, lim, ldt, bre, bim, dar, dai, dbbr, dbbi)


def _cmul(ar, ai, br, bi):
    return ar * br - ai * bi, ar * bi + ai * br


def _cpow(ar, ai, n):
    rr, ri = None, None
    pr, pi = ar, ai
    while n:
        if n & 1:
            rr, ri = (pr, pi) if rr is None else _cmul(rr, ri, pr, pi)
        n >>= 1
        if n:
            pr, pi = _cmul(pr, pi, pr, pi)
    return rr, ri


def _shift_rows(v, down):
    row = lax.broadcasted_iota(jnp.int32, v.shape, 0)
    if down:
        return jnp.where(row == 0, 0.0, pltpu.roll(v, 1, 0))
    return jnp.where(row == S5_SEGS - 1, 0.0, pltpu.roll(v, S5_SEGS - 1, 0))


def _chain_segments(er, ei, pr, pi, down):
    fr, fi = er, ei
    for _ in range(S5_SEGS - 1):
        sr, si = _shift_rows(fr, down), _shift_rows(fi, down)
        mr, mi = _cmul(pr, pi, sr, si)
        fr, fi = er + mr, ei + mi
    return _shift_rows(fr, down), _shift_rows(fi, down)


def _rows_to_scan_order(src_ref, dst_ref, t):
    ls = t // S5_SEGS

    def tile(j, c):
        dst_ref[pl.ds(pl.multiple_of(j * S5_SEGS, S5_SEGS), S5_SEGS), :] = src_ref[pl.ds(j, S5_SEGS, stride=ls), :]
        return c

    lax.fori_loop(0, ls, tile, 0, unroll=8)


def _rows_from_scan_order(src_ref, dst_ref, t):
    ls = t // S5_SEGS
    for s in range(S5_SEGS):
        def tile(jb, c, s=s):
            dst_ref[pl.ds(pl.multiple_of(s * ls + jb * 8, 8), 8), :] = (
                src_ref[pl.ds(jb * 8 * S5_SEGS + s, 8, stride=S5_SEGS), :])
            return c

        lax.fori_loop(0, ls // 8, tile, 0, unroll=8)


def _s5_fwd(ug, bd, ctd, ar4, ai4, dskip, carry=None):
    t = ug.shape[0]
    ls = t // S5_SEGS
    rc = min(512, t)
    ns = S5_BSTATE

    def body(ugn_ref, bd_ref, ct_ref, ar_ref, ai_ref, d_ref, xs_hbm, yn_ref, buf, ug_ref, y_ref, sem):
        cb = pl.program_id(0)
        bdv = bd_ref[0]
        _rows_to_scan_order(ugn_ref, ug_ref, t)

        def mm(i, c):
            rows = pl.ds(pl.multiple_of(i * rc, rc), rc)
            buf[rows, :] = _nn(ug_ref[rows, :].astype(BF16), bdv)
            return c

        lax.fori_loop(0, t // rc, mm, 0, unroll=True)
        arb = jnp.broadcast_to(ar_ref[0], (S5_SEGS, ns))
        aib = jnp.broadcast_to(ai_ref[0], (S5_SEGS, ns))

        def step(j, c, store):
            sr, si = c
            rows = pl.ds(pl.multiple_of(j * S5_SEGS, S5_SEGS), S5_SEGS)
            nr = arb * sr - aib * si + buf[rows, 0:ns]
            ni = arb * si + aib * sr + buf[rows, ns:2 * ns]
            if store:
                buf[rows, 0:ns] = nr
                buf[rows, ns:2 * ns] = ni
            return nr, ni

        zero = jnp.zeros((S5_SEGS, ns), F32)
        er, ei = lax.fori_loop(0, ls, functools.partial(step, store=False), (zero, zero))
        pr, pi = _cpow(arb, aib, ls)
        init = _chain_segments(er, ei, pr, pi, down=True)
        lax.fori_loop(0, ls, functools.partial(step, store=True), init)

        out = pltpu.make_async_copy(buf, xs_hbm.at[cb], sem)
        out.start()
        ctv = ct_ref[0]
        dv = d_ref[...]

        def ymm(i, c):
            rows = pl.ds(pl.multiple_of(i * rc, rc), rc)
            y_ref[rows, :] = _nn(buf[rows, :].astype(BF16), ctv) + dv * ug_ref[rows, :]
            return c

        lax.fori_loop(0, t // rc, ymm, 0, unroll=True)
        _rows_from_scan_order(y_ref, yn_ref, t)
        out.wait()

    return _call(
        body, name="s5_fwd", grid=(S5_BLOCKS,),
        in_specs=[pl.BlockSpec((t, 128), lambda i: (0, i)),
                  pl.BlockSpec((1, 128, 2 * ns), lambda i: (i, 0, 0)),
                  pl.BlockSpec((1, 2 * ns, 128), lambda i: (i, 0, 0)),
                  pl.BlockSpec((1, 1, ns), lambda i: (i, 0, 0)),
                  pl.BlockSpec((1, 1, ns), lambda i: (i, 0, 0)),
                  pl.BlockSpec((1, 128), lambda i: (0, i))],
        out_specs=[ANY, pl.BlockSpec((t, 128), lambda i: (0, i))],
        out_shape=[jax.ShapeDtypeStruct((S5_BLOCKS, t, 2 * ns), F32), jax.ShapeDtypeStruct((t, S5_WIDTH), F32)],
        scratch_shapes=[pltpu.VMEM((t, 2 * ns), F32), pltpu.VMEM((t, 128), F32), pltpu.VMEM((t, 128), F32),
                        pltpu.SemaphoreType.DMA(())],
        args=(ug, bd, ctd, ar4, ai4, dskip), carry=carry)


def _s5_bwd(dy, ug, xs, cd, bdt, ar4, ai4, dskip, carry=None):
    t = ug.shape[0]
    ls = t // S5_SEGS
    rc = min(512, t)
    ns = S5_BSTATE

    def body(dyn_ref, ugn_ref, xs_hbm, cd_ref, bdt_ref, ar_ref, ai_ref, d_ref,
             dugn_ref, dbd_ref, dcd_ref, dd_ref, dar_ref, dai_ref, xbuf, lam, dy_ref, ug_ref, dug_ref, sem):
        cb = pl.program_id(0)
        load = pltpu.make_async_copy(xs_hbm.at[cb], xbuf, sem)
        load.start()
        cdv = cd_ref[0]
        _rows_to_scan_order(dyn_ref, dy_ref, t)
        _rows_to_scan_order(ugn_ref, ug_ref, t)

        def mm(i, c):
            rows = pl.ds(pl.multiple_of(i * rc, rc), rc)
            lam[rows, :] = _nn(dy_ref[rows, :].astype(BF16), cdv)
            return c

        lax.fori_loop(0, t // rc, mm, 0, unroll=True)
        arb = jnp.broadcast_to(ar_ref[0], (S5_SEGS, ns))
        aib = jnp.broadcast_to(ai_ref[0], (S5_SEGS, ns))

        def lam_step(j, lr, li):
            rows = pl.ds(pl.multiple_of(j * S5_SEGS, S5_SEGS), S5_SEGS)
            nr = arb * lr + aib * li + lam[rows, 0:ns]
            ni = arb * li - aib * lr + lam[rows, ns:2 * ns]
            return rows, nr, ni

        def pass1(jj, c):
            _, nr, ni = lam_step(ls - 1 - jj, *c)
            return nr, ni

        zero = jnp.zeros((S5_SEGS, ns), F32)
        er, ei = lax.fori_loop(0, ls, pass1, (zero, zero))
        pr, pi = _cpow(arb, aib, ls)
        init = _chain_segments(er, ei, pr, -pi, down=False)
        load.wait()

        def accumulate(acc, nr, ni, xpr, xpi):
            return acc[0] + nr * xpr + ni * xpi, acc[1] + ni * xpr - nr * xpi

        def pass2(jj, c):
            lr, li, accr, acci = c
            j = ls - 1 - jj
            rows, nr, ni = lam_step(j, lr, li)
            lam[rows, 0:ns] = nr
            lam[rows, ns:2 * ns] = ni
            prev = pl.ds(pl.multiple_of((j - 1) * S5_SEGS, S5_SEGS), S5_SEGS)
            accr, acci = accumulate((accr, acci), nr, ni, xbuf[prev, 0:ns], xbuf[prev, ns:2 * ns])
            return nr, ni, accr, acci

        lr, li, accr, acci = lax.fori_loop(0, ls - 1, pass2, (init[0], init[1], zero, zero))
        rows, nr, ni = lam_step(0, lr, li)
        lam[rows, 0:ns] = nr
        lam[rows, ns:2 * ns] = ni
        last = pl.ds((ls - 1) * S5_SEGS, S5_SEGS)
        accr, acci = accumulate((accr, acci), nr, ni,
                                _shift_rows(xbuf[last, 0:ns], True), _shift_rows(xbuf[last, ns:2 * ns], True))
        dar_ref[0] = jnp.sum(accr, axis=0, keepdims=True)
        dai_ref[0] = jnp.sum(acci, axis=0, keepdims=True)

        bdtv = bdt_ref[0]
        dv = d_ref[...]
        dbd_ref[...] = jnp.zeros_like(dbd_ref)
        dcd_ref[...] = jnp.zeros_like(dcd_ref)
        dd_ref[...] = jnp.zeros_like(dd_ref)

        def tail(i, c):
            rows = pl.ds(pl.multiple_of(i * rc, rc), rc)
            dy = dy_ref[rows, :]
            ug = ug_ref[rows, :]
            lb = lam[rows, :].astype(BF16)
            dug_ref[rows, :] = _nn(lb, bdtv) + dv * dy
            dbd_ref[0] += _tn(ug.astype(BF16), lb)
            dcd_ref[0] += _tn(dy.astype(BF16), xbuf[rows, :].astype(BF16))
            dd_ref[...] += jnp.sum(dy * ug, axis=0, keepdims=True)
            return c

        lax.fori_loop(0, t // rc, tail, 0, unroll=True)
        _rows_from_scan_order(dug_ref, dugn_ref, t)

    chan = pl.BlockSpec((t, 128), lambda i: (0, i))
    dense = pl.BlockSpec((1, 128, 2 * ns), lambda i: (i, 0, 0))
    vec = pl.BlockSpec((1, 1, ns), lambda i: (i, 0, 0))
    return _call(
        body, name="s5_bwd", grid=(S5_BLOCKS,),
        in_specs=[chan, chan, ANY, dense, pl.BlockSpec((1, 2 * ns, 128), lambda i: (i, 0, 0)), vec, vec,
                  pl.BlockSpec((1, 128), lambda i: (0, i))],
        out_specs=[chan, dense, dense, pl.BlockSpec((1, 128), lambda i: (0, i)), vec, vec],
        out_shape=[jax.ShapeDtypeStruct((t, S5_WIDTH), F32),
                   jax.ShapeDtypeStruct((S5_BLOCKS, 128, 2 * ns), F32),
                   jax.ShapeDtypeStruct((S5_BLOCKS, 128, 2 * ns), F32),
                   jax.ShapeDtypeStruct((1, S5_WIDTH), F32),
                   jax.ShapeDtypeStruct((S5_BLOCKS, 1, ns), F32),
                   jax.ShapeDtypeStruct((S5_BLOCKS, 1, ns), F32)],
        scratch_shapes=[pltpu.VMEM((t, 2 * ns), F32), pltpu.VMEM((t, 2 * ns), F32)]
        + [pltpu.VMEM((t, 128), F32)] * 3 + [pltpu.SemaphoreType.DMA(())],
        args=(dy, ug, xs, cd, bdt, ar4, ai4, dskip), carry=carry)


def _cumsum_rows(x, reverse):
    c = x.shape[0]
    row = lax.broadcasted_iota(jnp.int32, x.shape, 0)
    d = 1
    while d < c:
        if reverse:
            x = x + jnp.where(row < c - d, pltpu.roll(x, c - d, 0), 0.0)
        else:
            x = x + jnp.where(row >= d, pltpu.roll(x, d, 0), 0.0)
        d *= 2
    return x


def _gla_common(q, k, alow, wup, bup):
    c = GLA_CHUNK
    pre = _nn(alow.astype(BF16), wup.astype(BF16)) + bup
    la = (jnp.minimum(pre, 0.0) - jnp.log(1.0 + jnp.exp(-jnp.abs(pre)))) * (1.0 / GLA_TAU)
    rr = lax.broadcasted_iota(jnp.int32, (c, c), 0)
    cc = lax.broadcasted_iota(jnp.int32, (c, c), 1)
    tril = (rr >= cc).astype(F32)
    bc = _cumsum_rows(la, reverse=False)
    bl = bc[c - 1:c, :]
    e_pos = jnp.exp(bc)
    e_neg = jnp.exp(-bc)
    e_end = jnp.exp(bl - bc)
    qt = q * (GLA_DK ** -0.5) * e_pos
    kt = k * e_neg
    ke = k * e_end
    lane = lax.broadcasted_iota(jnp.int32, (1, GLA_KEY), 1)
    masks = [((lane >= h * GLA_DK) & (lane < (h + 1) * GLA_DK)).astype(F32) for h in range(GLA_HEADS)]
    return dict(pre=pre, tril=tril, bc=bc, bl=bl, e_pos=e_pos, e_neg=e_neg, e_end=e_end,
                qt=qt, kt=kt, ke=ke, dec=jnp.exp(bl), masks=masks)


def _gla_fwd(q, k, v, alow, wup, bup, carry=None):
    t = q.shape[0]
    c = GLA_CHUNK
    n = t // c
    step = GLA_STEP_CHUNKS * c

    def body(q_ref, k_ref, v_ref, al_ref, wup_ref, bup_ref, o_ref, ss_ref, s_ref):
        i = pl.program_id(0)

        @pl.when(i == 0)
        def _():
            s_ref[...] = jnp.zeros_like(s_ref)

        wup_v, bup_v = wup_ref[...], bup_ref[...]
        s = s_ref[...]
        for j in range(GLA_STEP_CHUNKS):
            tok = slice(j * c, (j + 1) * c)
            m = _gla_common(q_ref[tok, :], k_ref[tok, :], al_ref[tok, :], wup_v, bup_v)
            ss_ref[j] = s
            sb = s.astype(BF16)
            ktb = m["kt"].astype(BF16)
            update = jnp.zeros_like(s)
            for h in range(GLA_HEADS):
                mask = m["masks"][h]
                qm = (m["qt"] * mask).astype(BF16)
                vh = v_ref[tok, h * GLA_DV:(h + 1) * GLA_DV].astype(BF16)
                p = (m["tril"] * _nt(qm, ktb)).astype(BF16)
                o_ref[tok, h * GLA_DV:(h + 1) * GLA_DV] = _nn(p, vh) + _nt(qm, sb)
                update = update + _tn(vh, (m["ke"] * mask).astype(BF16))
            s = m["dec"] * s + update
        s_ref[...] = s

    return _call(
        body, name="gla_fwd", grid=(t // step,),
        in_specs=[_row_tile(step, GLA_KEY), _row_tile(step, GLA_KEY), _row_tile(step, GLA_VAL),
                  _row_tile(step, GLA_RANK), VMEM_FULL, VMEM_FULL],
        out_specs=[_row_tile(step, GLA_VAL), pl.BlockSpec((GLA_STEP_CHUNKS, GLA_DV, GLA_KEY), lambda i: (i, 0, 0))],
        out_shape=[jax.ShapeDtypeStruct((t, GLA_VAL), F32), jax.ShapeDtypeStruct((n, GLA_DV, GLA_KEY), F32)],
        scratch_shapes=[pltpu.VMEM((GLA_DV, GLA_KEY), F32)],
        args=(q, k, v, alow, wup, bup), carry=carry)


def _gla_bwd(q, k, v, alow, wup, bup, ssave, do, carry=None):
    t = q.shape[0]
    c = GLA_CHUNK
    n = t // c

    def body(q_ref, k_ref, v_ref, al_ref, wup_ref, bup_ref, ss_ref, do_ref,
             dq_ref, dk_ref, dv_ref, dal_ref, dwup_ref, dbup_ref, ds_ref):
        i = pl.program_id(0)

        @pl.when(i == 0)
        def _():
            ds_ref[...] = jnp.zeros_like(ds_ref)
            dwup_ref[...] = jnp.zeros_like(dwup_ref)
            dbup_ref[...] = jnp.zeros_like(dbup_ref)

        wup_v, bup_v = wup_ref[...], bup_ref[...]
        ds_in = ds_ref[...]
        dwup = jnp.zeros((GLA_RANK, GLA_KEY), F32)
        dbup = jnp.zeros((1, GLA_KEY), F32)
        for j in reversed(range(GLA_STEP_CHUNKS)):
            tok = slice(j * c, (j + 1) * c)
            alow_v = al_ref[tok, :]
            m = _gla_common(q_ref[tok, :], k_ref[tok, :], alow_v, wup_v, bup_v)
            s = ss_ref[j]
            sb = s.astype(BF16)
            dsb = ds_in.astype(BF16)
            qt, kt, ke = m["qt"], m["kt"], m["ke"]
            ktb = kt.astype(BF16)
            dqt = jnp.zeros((c, GLA_KEY), F32)
            dkt = jnp.zeros((c, GLA_KEY), F32)
            dke = jnp.zeros((c, GLA_KEY), F32)
            update = jnp.zeros_like(ds_in)
            for h in range(GLA_HEADS):
                mask = m["masks"][h]
                qm = (qt * mask).astype(BF16)
                km = (kt * mask).astype(BF16)
                kem = (ke * mask).astype(BF16)
                cols = slice(h * GLA_DV, (h + 1) * GLA_DV)
                vh = v_ref[tok, cols].astype(BF16)
                doh = do_ref[tok, cols].astype(BF16)
                p = (m["tril"] * _nt(qm, ktb)).astype(BF16)
                dp = (m["tril"] * _nt(doh, vh)).astype(BF16)
                dv_ref[tok, cols] = (_tn(p, doh) + _nt(kem, dsb)).astype(BF16)
                dqt = dqt + _nn(dp, km) + _nn(doh, sb) * mask
                dkt = dkt + _tn(dp, qm)
                dke = dke + _nn(vh, dsb) * mask
                update = update + _tn(doh, qm)
            ddec = jnp.sum(ds_in * s, axis=0, keepdims=True)
            dq_ref[tok, :] = (dqt * m["e_pos"] * (GLA_DK ** -0.5)).astype(BF16)
            dk_ref[tok, :] = (dkt * m["e_neg"] + dke * m["e_end"]).astype(BF16)
            dkeke = dke * ke
            dbl = jnp.sum(dkeke, axis=0, keepdims=True) + ddec * m["dec"]
            last = (lax.broadcasted_iota(jnp.int32, (c, 1), 0) == c - 1).astype(F32)
            dla = _cumsum_rows(dqt * qt - dkt * kt - dkeke + last * dbl, reverse=True)
            dpre = dla * (1.0 / GLA_TAU) * jax.nn.sigmoid(-m["pre"])
            dpb = dpre.astype(BF16)
            dal_ref[tok, :] = _nt(dpb, wup_v.astype(BF16)).astype(BF16)
            dwup = dwup + _tn(alow_v.astype(BF16), dpb)
            dbup = dbup + jnp.sum(dpre, axis=0, keepdims=True)
            ds_in = m["dec"] * ds_in + update
        ds_ref[...] = ds_in
        dwup_ref[...] += dwup
        dbup_ref[...] += dbup

    step = GLA_STEP_CHUNKS * c
    nsteps = t // step

    def rev(d):
        return pl.BlockSpec((step, d), lambda i: (nsteps - 1 - i, 0))

    return _call(
        body, name="gla_bwd", grid=(nsteps,),
        in_specs=[rev(GLA_KEY), rev(GLA_KEY), rev(GLA_VAL), rev(GLA_RANK), VMEM_FULL, VMEM_FULL,
                  pl.BlockSpec((GLA_STEP_CHUNKS, GLA_DV, GLA_KEY), lambda i: (nsteps - 1 - i, 0, 0)), rev(GLA_VAL)],
        out_specs=[rev(GLA_KEY), rev(GLA_KEY), rev(GLA_VAL), rev(GLA_RANK),
                   pl.BlockSpec((GLA_RANK, GLA_KEY), lambda i: (0, 0)), _acc_row(GLA_KEY)],
        out_shape=[jax.ShapeDtypeStruct((t, GLA_KEY), BF16), jax.ShapeDtypeStruct((t, GLA_KEY), BF16),
                   jax.ShapeDtypeStruct((t, GLA_VAL), BF16), jax.ShapeDtypeStruct((t, GLA_RANK), BF16),
                   jax.ShapeDtypeStruct((GLA_RANK, GLA_KEY), F32), jax.ShapeDtypeStruct((1, GLA_KEY), F32)],
        scratch_shapes=[pltpu.VMEM((GLA_DV, GLA_KEY), F32)],
        args=(q, k, v, alow, wup, bup, ssave, do), carry=carry)


def _post_math(y, o, r, gs5, ggla, wg, bg, gn, ps5t, pglat):
    y2 = y * y
    th = jnp.tanh(GELU_C0 * (y + GELU_C1 * y * y2))
    z5 = 0.5 * y * (1.0 + th)
    z5b = z5.astype(BF16)
    gate = jax.nn.sigmoid(_nn(z5b, wg) + bg)
    ys5 = z5 * gate
    rs, on = [], []
    for h in range(GLA_HEADS):
        oh = o[:, h * GLA_DV:(h + 1) * GLA_DV]
        rh = lax.rsqrt(jnp.mean(oh * oh, axis=-1, keepdims=True) + EPS)
        rs.append(rh)
        on.append(oh * rh)
    on = jnp.concatenate(on, axis=-1)
    sr = jax.nn.sigmoid(r)
    silu_r = r * sr
    ygla = on * gn * silu_r
    ys5b, yglab = ys5.astype(BF16), ygla.astype(BF16)
    m5 = _nt(ys5b, ps5t)
    mg = _nt(yglab, pglat)
    s5g, glag = jax.nn.sigmoid(gs5), jax.nn.sigmoid(ggla)
    merged = s5g * m5 + glag * mg
    return dict(y2=y2, th=th, z5=z5, z5b=z5b, gate=gate, ys5b=ys5b, yglab=yglab, rs=rs, on=on, sr=sr,
                silu_r=silu_r, m5=m5, mg=mg, s5g=s5g, glag=glag, mergedb=merged.astype(BF16))


def _mix_post_fwd(y, o, r, gs5, ggla, h1, wg, bg, gn, ps5t, pglat, wout, carry=None):
    t = o.shape[0]
    tm = _tile(t)

    def body(y_ref, o_ref, r_ref, gs5_ref, ggla_ref, h1_ref, wg_ref, bg_ref, gn_ref, ps_ref, pg_ref, wo_ref, h2_ref):
        m = _post_math(y_ref[...], o_ref[...], r_ref[...], gs5_ref[...], ggla_ref[...],
                       wg_ref[...], bg_ref[...], gn_ref[...], ps_ref[...], pg_ref[...])
        h2_ref[...] = h1_ref[...] + _nn(m["mergedb"], wo_ref[...])

    (h2,), landed = _call(
        body, name="mix_post_fwd", grid=(t // tm,),
        in_specs=[_row_tile(tm, 512)] * 3 + [_row_tile(tm, D_MODEL)] * 3
        + [VMEM_FULL, _acc_row(512), _acc_row(512), VMEM_FULL, VMEM_FULL, VMEM_FULL],
        out_specs=[_row_tile(tm, D_MODEL)],
        out_shape=[jax.ShapeDtypeStruct((t, D_MODEL), F32)],
        args=(y, o, r, gs5, ggla, h1, wg, bg, gn, ps5t, pglat, wout), carry=carry)
    return h2, landed


def _mix_post_bwd(y, o, r, gs5, ggla, dh2, wg, bg, gn, ps5t, pglat, wout, carry=None):
    t = o.shape[0]
    tm = _tile(t) // 2

    def body(y_ref, o_ref, r_ref, gs5_ref, ggla_ref, dh2_ref, wg_ref, bg_ref, gn_ref, ps_ref, pg_ref, wo_ref,
             dy_ref, do_ref, dr_ref, dgs5_ref, dggla_ref, dbg_ref, dgn_ref,
             z5b_ref, dgp_ref, ys5b_ref, dm5b_ref, yglab_ref, dmgb_ref, mergedb_ref, dh2b_ref):
        i = pl.program_id(0)
        yv, ov, rv = y_ref[...], o_ref[...], r_ref[...]
        wg, gn, ps5t, pglat = wg_ref[...], gn_ref[...], ps_ref[...], pg_ref[...]
        m = _post_math(yv, ov, rv, gs5_ref[...], ggla_ref[...], wg, bg_ref[...], gn, ps5t, pglat)
        dh2b = dh2_ref[...].astype(BF16)
        dmerged = _nt(dh2b, wo_ref[...])
        s5g, glag = m["s5g"], m["glag"]
        dgs5_ref[...] = (dmerged * m["m5"] * s5g * (1.0 - s5g)).astype(BF16)
        dggla_ref[...] = (dmerged * m["mg"] * glag * (1.0 - glag)).astype(BF16)
        dm5b = (dmerged * s5g).astype(BF16)
        dmgb = (dmerged * glag).astype(BF16)
        dys5 = _nn(dm5b, ps5t)
        dygla = _nn(dmgb, pglat)
        gate, z5, th = m["gate"], m["z5"], m["th"]
        dgpre = dys5 * z5 * gate * (1.0 - gate)
        dgpb = dgpre.astype(BF16)
        dz5 = dys5 * gate + _nt(dgpb, wg)
        dgelu = 0.5 * (1.0 + th) + 0.5 * yv * (1.0 - th * th) * GELU_C0 * (1.0 + 3.0 * GELU_C1 * m["y2"])
        dy_ref[...] = dz5 * dgelu
        on, sr, silu_r = m["on"], m["sr"], m["silu_r"]
        dr_ref[...] = (dygla * on * gn * sr * (1.0 + rv * (1.0 - sr))).astype(BF16)
        dgn = jnp.sum(dygla * on * silu_r, axis=0, keepdims=True)
        don = dygla * gn * silu_r
        for h in range(GLA_HEADS):
            cols = slice(h * GLA_DV, (h + 1) * GLA_DV)
            donh, onh = don[:, cols], on[:, cols]
            do_ref[:, cols] = (m["rs"][h] * (donh - onh * jnp.mean(donh * onh, axis=-1, keepdims=True))).astype(BF16)

        @pl.when(i == 0)
        def _():
            dbg_ref[...] = jnp.zeros_like(dbg_ref)
            dgn_ref[...] = jnp.zeros_like(dgn_ref)

        dbg_ref[...] += jnp.sum(dgpre, axis=0, keepdims=True)
        dgn_ref[...] += dgn
        z5b_ref[...] = m["z5b"]
        dgp_ref[...] = dgpb
        ys5b_ref[...] = m["ys5b"]
        dm5b_ref[...] = dm5b
        yglab_ref[...] = m["yglab"]
        dmgb_ref[...] = dmgb
        mergedb_ref[...] = m["mergedb"]
        dh2b_ref[...] = dh2b

    def f32(d):
        return jax.ShapeDtypeStruct((t, d), F32)

    def b16(d):
        return jax.ShapeDtypeStruct((t, d), BF16)

    widths = (512, 512, 512, 1024, 512, 1024, 1024, 1024)
    return _call(
        body, name="mix_post_bwd", grid=(t // tm,),
        in_specs=[_row_tile(tm, 512)] * 3 + [_row_tile(tm, D_MODEL)] * 3
        + [VMEM_FULL, _acc_row(512), _acc_row(512), VMEM_FULL, VMEM_FULL, VMEM_FULL],
        out_specs=[_row_tile(tm, 512)] * 3 + [_row_tile(tm, D_MODEL)] * 2
        + [_acc_row(512)] * 2 + [_row_tile(tm, w) for w in widths],
        out_shape=[f32(512), b16(512), b16(512), b16(D_MODEL), b16(D_MODEL)]
        + [jax.ShapeDtypeStruct((1, 512), F32)] * 2
        + [b16(w) for w in widths],
        args=(y, o, r, gs5, ggla, dh2, wg, bg, gn, ps5t, pglat, wout), carry=carry)


ADAM_TILE_ELEMS = 256 * 1024


def _adamw(w, g, m, v, name):
    rows, cols = w.shape
    tr = rows
    while tr * cols > ADAM_TILE_ELEMS and tr % 16 == 0:
        tr //= 2

    spec = pl.BlockSpec((tr, cols), lambda i: (i, 0))
    sh = jax.ShapeDtypeStruct((rows, cols), F32)
    return pl.pallas_call(functools.partial(_adamw_body), name=name, grid=(rows // tr,), in_specs=[spec] * 4,
                          out_specs=[spec] * 3, out_shape=[sh] * 3, compiler_params=_cparams(1))(w, g, m, v)


def _adamw_math(w, g, m, v):
    nm = ADAM_B1 * m + (1.0 - ADAM_B1) * g
    nv = ADAM_B2 * v + (1.0 - ADAM_B2) * (g * g)
    m_hat = nm / (1.0 - ADAM_B1 ** ADAM_STEP)
    v_hat = nv / (1.0 - ADAM_B2 ** ADAM_STEP)
    return -ADAM_LR * (m_hat / (jnp.sqrt(v_hat) + ADAM_EPS) + ADAM_WD * w), nm, nv


def _adamw_body(w_ref, g_ref, m_ref, v_ref, d_ref, nm_ref, nv_ref):
    d_ref[...], nm_ref[...], nv_ref[...] = _adamw_math(w_ref[...], g_ref[...], m_ref[...], v_ref[...])


SUM_ADAM_ROWS = 32


def _sum_adamw(landed, ws, ms, vs, name, carry=None):
    k = len(ws)
    n = landed[0].shape[0]
    r, c = ws[0].shape
    tr = SUM_ADAM_ROWS

    def body(*refs):
        lands, (w_refs, m_refs, v_refs), outs = refs[:k], (refs[k:2 * k], refs[2 * k:3 * k], refs[3 * k:4 * k]), refs[4 * k:]
        for i in range(k):
            g = lands[i][0].astype(F32)
            for s in range(1, n):
                g = g + lands[i][s].astype(F32)
            outs[i][...] = g
            outs[k + i][...], outs[2 * k + i][...], outs[3 * k + i][...] = _adamw_math(
                w_refs[i][...], g, m_refs[i][...], v_refs[i][...])

    row = pl.BlockSpec((tr, c), lambda i: (i, 0))
    return _call(
        body, name=name, grid=(r // tr,),
        in_specs=[pl.BlockSpec((n, tr, c), lambda i: (0, i, 0))] * k + [row] * (3 * k),
        out_specs=[row] * (4 * k), out_shape=[jax.ShapeDtypeStruct((r, c), F32)] * (4 * k),
        args=(*landed, *ws, *ms, *vs), carry=carry)


def _adamw_many(ws, gs, ms, vs, name):
    n = len(ws)

    def body(*refs):
        ins, outs = refs[:4 * n], refs[4 * n:]
        for i in range(n):
            _adamw_body(*(ins[j * n + i] for j in range(4)), *(outs[j * n + i] for j in range(3)))

    shapes = [jax.ShapeDtypeStruct(w.shape, F32) for w in ws]
    res = pl.pallas_call(body, name=name, in_specs=[VMEM_FULL] * (4 * n), out_specs=[VMEM_FULL] * (3 * n),
                         out_shape=shapes * 3)(*ws, *gs, *ms, *vs)
    return res[:n], res[n:2 * n], res[2 * n:]


def _exchange(carry, name):
    return _call(None, name=name, grid=(1,), in_specs=[], out_specs=[], out_shape=[], args=(), carry=carry)[1]


def _pair_add(slabs, from_pair, name):
    _, r, cols = slabs.shape

    def body(s_ref, p_ref, o_ref):
        c = lax.axis_index("c")
        mine = jnp.where(c == 0, s_ref[0, 0].astype(F32), s_ref[0, 1].astype(F32))
        o_ref[0] = (mine + p_ref[0].astype(F32)).astype(BF16)

    return pl.pallas_call(
        body, name=name, grid=(4,),
        in_specs=[pl.BlockSpec((1, 2, r, cols), lambda i: (i, 0, 0, 0)), pl.BlockSpec((1, r, cols), lambda i: (i, 0, 0))],
        out_specs=pl.BlockSpec((1, r, cols), lambda i: (i, 0, 0)),
        out_shape=jax.ShapeDtypeStruct((4, r, cols), BF16),
        compiler_params=_cparams(1),
    )(slabs.reshape(4, 2, r, cols), from_pair)


def _sum_slabs(slabs, name):
    n = slabs.shape[0]

    def body(s_ref, o_ref):
        acc = s_ref[0].astype(F32)
        for s in range(1, n):
            acc = acc + s_ref[s].astype(F32)
        o_ref[...] = acc

    return pl.pallas_call(
        body, name=name, in_specs=[VMEM_FULL], out_specs=VMEM_FULL,
        out_shape=jax.ShapeDtypeStruct(slabs.shape[1:], F32),
        compiler_params=pltpu.CompilerParams(vmem_limit_bytes=VMEM_LIMIT_BYTES),
    )(slabs)


BIG = ("ffn1_w1", "ffn1_w3", "ffn1_w2", "w_in", "s5_glu_w", "gla_a_up_w", "proj_s5", "proj_gla", "w_out",
       "ffn2_w1", "ffn2_w3", "ffn2_w2")
GROUPS = (("ffn1_w1", "ffn1_w3", "ffn1_w2"),
          ("w_in", "s5_glu_w", "gla_a_up_w", "proj_s5", "proj_gla", "w_out"),
          ("ffn2_w1", "ffn2_w3", "ffn2_w2"))
W_IN_ROWS = 514
W_IN_PAD = 528
UP_COLS = 32
ROW_ADAM = ("ffn1_w1", "ffn1_w3", "w_in", "ffn2_w1", "ffn2_w3")
COL_SHARDED = ("ffn1_w1", "ffn1_w3", "w_in", "proj_s5", "proj_gla", "ffn2_w1", "ffn2_w3")

SMALL = ("ffn1_norm", "mix_norm", "s5_lambda_re", "s5_lambda_im", "s5_log_dt", "s5_b_re", "s5_b_im", "s5_c_re",
         "s5_c_im", "s5_d", "s5_glu_b", "gla_a_up_b", "gla_out_norm", "ffn2_norm", "final_norm")
SMALL_SHAPES = dict(ffn1_norm=(1, 1024), mix_norm=(1, 1024), s5_lambda_re=(1, 32, 64), s5_lambda_im=(1, 32, 64),
                    s5_log_dt=(1, 32), s5_b_re=(1, 32, 64, 16), s5_b_im=(1, 32, 64, 16), s5_c_re=(1, 32, 16, 64),
                    s5_c_im=(1, 32, 16, 64), s5_d=(1, 32, 16), s5_glu_b=(1, 512), gla_a_up_b=(1, 256),
                    gla_out_norm=(1, 512), ffn2_norm=(1, 1024), final_norm=(1024,))
SMALL_N = sum(math.prod(s) for s in SMALL_SHAPES.values())
SMALL_R = -(-SMALL_N // (64 * 1024)) * 64


def _shard_rows(name, a):
    if name == "gla_a_up_w":
        return jnp.pad(a, ((0, 0), (0, 128 - UP_COLS)))
    if name in COL_SHARDED:
        a = a.T
    if name == "w_in":
        return jnp.pad(a, ((0, W_IN_PAD - W_IN_ROWS), (0, 0)))
    return a.reshape(-1, 1024)


def _unshard_rows(name, rows, shape):
    if name == "gla_a_up_w":
        return rows[:, :UP_COLS]
    if name == "w_in":
        rows = rows[:W_IN_ROWS]
    if name in COL_SHARDED:
        return rows.reshape(shape[1], shape[0]).T
    return rows.reshape(shape)


def _pack_small(vals, loss):
    flat = jnp.concatenate([vals[n].reshape(-1).astype(F32) for n in SMALL] + [loss.reshape(1)])
    return jnp.pad(flat, (0, SMALL_R * 1024 - SMALL_N - 1)).reshape(SMALL_R, 1024)


S5_B = ("s5_b_re", "s5_b_im")


def _working(name, a):
    return a[0].transpose(0, 2, 1) if name in S5_B else a


def _declared(name, a):
    return a.transpose(0, 2, 1)[None] if name in S5_B else a.reshape(SMALL_SHAPES[name])


def _unpack_small(slab):
    flat = slab.reshape(-1)
    out, off = {}, 0
    for n in SMALL:
        size = math.prod(SMALL_SHAPES[n])
        shape = (S5_GROUPS, S5_GROUP, S5_STATE) if n in S5_B else SMALL_SHAPES[n]
        out[n] = flat[off:off + size].reshape(shape)
        off += size
    return out


FULL_SHAPES = dict(w_in=(IN_COLS, D_MODEL), s5_glu_w=(S5_WIDTH, S5_WIDTH), gla_a_up_w=(GLA_RANK, GLA_KEY),
                   proj_s5=(D_MODEL, S5_WIDTH), proj_gla=(D_MODEL, GLA_VAL), w_out=(D_MODEL, D_MODEL))


def _full_weight(name, gathered):
    if name == "gla_a_up_w":
        return gathered[:, :, :UP_COLS].transpose(1, 0, 2).reshape(GLA_RANK, GLA_KEY)
    if name == "w_in":
        gathered = gathered[:, :W_IN_ROWS]
    return gathered.reshape(FULL_SHAPES.get(name, (D_FF, D_MODEL)))


def _grad_slabs(name, g):
    if name == "gla_a_up_w":
        g = g.reshape(GLA_RANK, N_DEV, UP_COLS).transpose(1, 0, 2)
        return jnp.pad(g, ((0, 0), (0, 0), (0, 128 - UP_COLS))).astype(BF16)
    if name == "w_in":
        return jnp.pad(g.reshape(N_DEV, W_IN_ROWS, D_MODEL), ((0, 0), (0, W_IN_PAD - W_IN_ROWS), (0, 0)))
    return g.reshape(N_DEV, -1, 1024)


def _s5_dense(re, im, sign_im):
    eye = jnp.eye(8, dtype=F32)

    def one(a):
        a = a.reshape(S5_BLOCKS, 8, S5_GROUP, S5_STATE)
        return jnp.einsum("cghp,gk->cghkp", a, eye).reshape(S5_BLOCKS, 128, S5_BSTATE)

    return jnp.concatenate([one(re), sign_im * one(im)], axis=-1)


def _s5_undense(d):
    eye = jnp.eye(8, dtype=F32)

    def one(a):
        a = a.reshape(S5_BLOCKS, 8, S5_GROUP, 8, S5_STATE)
        return jnp.einsum("cghkp,gk->cghp", a, eye).reshape(S5_GROUPS, S5_GROUP, S5_STATE)

    return one(d[..., :S5_BSTATE]), one(d[..., S5_BSTATE:])


def _local_step(x, target, p, w, rows=None, opt=None):
    w = dict(w or {})
    landed_grads = {}

    def gather(names):
        return None if rows is None else ([rows[n] for n in names], "gather")

    def gathered(names, landed):
        w.update({n: _full_weight(n, g) for n, g in zip(names, landed)})

    def scatter(names):
        return None if rows is None else ([_grad_slabs(n, big[n]) for n in names], "scatter")

    def scattered(names, landed):
        landed_grads.update(zip(names, landed))

    if rows is not None:
        gathered(GROUPS[0], _exchange(gather(GROUPS[0]), "gather_ffn1"))
    g1, gm, g2 = p["ffn1_norm"], p["mix_norm"], p["ffn2_norm"]
    gf = p["final_norm"].reshape(1, D_MODEL)
    lre, lim = p["s5_lambda_re"][0], p["s5_lambda_im"][0]
    ldt = p["s5_log_dt"][0].reshape(S5_GROUPS, 1)
    bre = p["s5_b_re"][0].transpose(0, 2, 1)
    bim = p["s5_b_im"][0].transpose(0, 2, 1)
    cre, cim = p["s5_c_re"][0], p["s5_c_im"][0]
    dskip = p["s5_d"][0].reshape(1, S5_WIDTH)
    bg, bup, gn = p["s5_glu_b"], p["gla_a_up_b"], p["gla_out_norm"]

    mix_first, mix_rest = ("w_in", "gla_a_up_w"), ("s5_glu_w", "proj_s5", "proj_gla", "w_out")
    h1, (a3_1, b3_1, n1), got = _ffn_fwd(x, g1, w["ffn1_w1"], w["ffn1_w3"], w["ffn1_w2"], "ffn1_fwd",
                                         gather(mix_first))
    gathered(mix_first, got)
    wup = w["gla_a_up_w"].astype(F32)
    (u, s5in, q, k, v, r, alow, gs5, ggla), got = _mix_pre_fwd(h1, gm, w["w_in"], gather(mix_rest))
    gathered(mix_rest, got)
    ar, ai, bbr, bbi = _s5_disc(lre, lim, ldt, bre, bim)
    bd = _s5_dense(bbr, bbi, 1.0)
    cd = _s5_dense(cre, cim, -1.0)
    bd16, cd16 = bd.astype(BF16), cd.astype(BF16)
    bdt16, ctd16 = bd16.transpose(0, 2, 1), cd16.transpose(0, 2, 1)
    ar4 = ar.reshape(S5_BLOCKS, 1, S5_BSTATE)
    ai4 = ai.reshape(S5_BLOCKS, 1, S5_BSTATE)
    (xs, y), got = _s5_fwd(s5in, bd16, ctd16, ar4, ai4, dskip, gather(GROUPS[2][:1]))
    gathered(GROUPS[2][:1], got)
    (o, ssave), got = _gla_fwd(q, k, v, alow, wup, bup, gather(GROUPS[2][1:2]))
    gathered(GROUPS[2][1:2], got)
    post_w = (w["s5_glu_w"], bg, gn, w["proj_s5"], w["proj_gla"], w["w_out"])
    h2, got = _mix_post_fwd(y, o, r, gs5, ggla, h1, *post_w, carry=gather(GROUPS[2][2:]))
    gathered(GROUPS[2][2:], got)
    loss, dh3, dgf, (a3_2, b3_2, n2) = _ffn_fwd_head(h2, g2, w["ffn2_w1"], w["ffn2_w3"], w["ffn2_w2"], gf, target,
                                                     "ffn2_fwd")

    big, small = {}, {}
    small["final_norm"] = dgf.reshape(D_MODEL)
    (dh2, dg2, da3, db3, s3, dhh2), _ = _ffn_bwd(
        h2, dh3, g2, a3_2, b3_2, w["ffn2_w1"], w["ffn2_w3"], w["ffn2_w2"], "ffn2_bwd")
    small["ffn2_norm"] = dg2
    big["ffn2_w1"] = _mm_tn(da3, n2, "ffn2_dw1")
    big["ffn2_w3"] = _mm_tn(db3, n2, "ffn2_dw3")
    big["ffn2_w2"] = _mm_tn(s3, dhh2, "ffn2_dw2")
    (dy, do, dr, dgs5, dggla, dbg, dgn, z5b, dgpb, ys5b, dm5b, yglab, dmgb, mergedb, dh2b), got = _mix_post_bwd(
        y, o, r, gs5, ggla, dh2, *post_w, carry=scatter(GROUPS[2][:1]))
    scattered(GROUPS[2][:1], got)
    small["s5_glu_b"] = dbg
    small["gla_out_norm"] = dgn
    big["s5_glu_w"] = _mm_tn(z5b, dgpb, "glu_dw")
    big["proj_s5"] = _mm_tn(dm5b, ys5b, "proj_s5_dw")
    big["proj_gla"] = _mm_tn(dmgb, yglab, "proj_gla_dw")
    big["w_out"] = _mm_tn(mergedb, dh2b, "w_out_dw")
    (dq, dk, dv, dalow, dwup, dbup), got = _gla_bwd(q, k, v, alow, wup, bup, ssave, do, scatter(GROUPS[2][1:2]))
    scattered(GROUPS[2][1:2], got)
    big["gla_a_up_w"] = dwup
    small["gla_a_up_b"] = dbup
    (ds5in, dbd, dcd, dd, dar4, dai4), got = _s5_bwd(
        dy, s5in, xs, cd16, bdt16, ar4, ai4, dskip, scatter(GROUPS[2][2:]))
    scattered(GROUPS[2][2:], got)
    dbbr, dbbi = _s5_undense(dbd)
    dcre, dcim_neg = _s5_undense(dcd)
    glre, glim, gldt, gbre, gbim = _s5_disc_bwd(
        lre, lim, ldt, bre, bim, dar4.reshape(S5_GROUPS, S5_STATE), dai4.reshape(S5_GROUPS, S5_STATE),
        dbbr, dbbi)
    small["s5_lambda_re"] = glre[None]
    small["s5_lambda_im"] = glim[None]
    small["s5_log_dt"] = gldt.reshape(1, S5_GROUPS)
    small["s5_b_re"] = gbre
    small["s5_b_im"] = gbim
    small["s5_c_re"] = dcre[None]
    small["s5_c_im"] = -dcim_neg[None]
    small["s5_d"] = dd.reshape(1, S5_GROUPS, S5_GROUP)
    dz = (ds5in, dq, dk, dv, dr, dalow, dgs5, dggla)
    (dh1, dgm), got = _mix_pre_bwd(h1, gm, w["w_in"], dh2, dz, scatter(mix_rest[:3]))
    scattered(mix_rest[:3], got)
    small["mix_norm"] = dgm
    wide = _mm_tn_many(dz[:5] + dz[6:], u, "w_in_dw")
    low_at = IN_OFFS[5]
    big["w_in"] = jnp.concatenate([wide[:low_at], _mm_tn(dalow, u, "w_in_dw_low"), wide[low_at:]], axis=0)
    (dx, dg1, da3, db3, s3, dhh1), got = _ffn_bwd(
        x, dh1, g1, a3_1, b3_1, w["ffn1_w1"], w["ffn1_w3"], w["ffn1_w2"], "ffn1_bwd",
        scatter(mix_first + mix_rest[3:]))
    scattered(mix_first + mix_rest[3:], got)
    small["ffn1_norm"] = dg1
    if rows is None:
        big["ffn1_w1"] = _mm_tn(da3, n1, "ffn1_dw1")
        big["ffn1_w3"] = _mm_tn(db3, n1, "ffn1_dw3")
        big["ffn1_w2"] = _mm_tn(s3, dhh1, "ffn1_dw2")
        return loss[0, 0], dx, big, small
    part = _pack_small(small, loss).reshape(N_DEV, SMALL_R // N_DEV, 1024)
    big["ffn1_w1"], (small_landed,) = _mm_tn(da3, n1, "ffn1_dw1", ([part], "scatter"))
    small_mine = _sum_slabs(small_landed, "sum_small")
    slabs1 = _grad_slabs("ffn1_w1", big["ffn1_w1"])
    big["ffn1_w3"], (from_pair, small_all) = _mm_tn(db3, n1, "ffn1_dw3",
                                                    [([slabs1], "pair"), ([small_mine], "gather")])
    small = small_all.reshape(SMALL_R, 1024)
    sums1 = _pair_add(slabs1, from_pair, "ffn1_w1_pair")
    slabs3 = _grad_slabs("ffn1_w3", big["ffn1_w3"])
    big["ffn1_w2"], (landed1, from_pair) = _mm_tn(s3, dhh1, "ffn1_dw2", [([sums1], "chips"), ([slabs3], "pair")])
    sums3 = _pair_add(slabs3, from_pair, "ffn1_w3_pair")
    slabs2 = _grad_slabs("ffn1_w2", big["ffn1_w2"])

    def sum_adamw(names, lands, name, carry=None):
        outs, got = _sum_adamw(lands, *([opt[n][j] for n in names] for j in range(3)), name, carry)
        for i, n in enumerate(names):
            updated[n] = outs[i::len(names)]
        return got

    updated = {}
    landed3, from_pair = sum_adamw(GROUPS[2], [landed_grads.pop(n) for n in GROUPS[2]], "adamw_ffn2",
                                   [([sums3], "chips"), ([slabs2], "pair")])
    sums2 = _pair_add(slabs2, from_pair, "ffn1_w2_pair")
    (landed2,) = _exchange(([sums2], "chips"), "scatter_ffn1_b")
    sum_adamw(GROUPS[0], [landed1, landed3, landed2], "adamw_ffn1")
    return loss[0, 0], dx, landed_grads, small, updated


NAMES = ("ffn1_norm", "ffn1_w1", "ffn1_w3", "ffn1_w2", "mix_norm", "w_in", "s5_lambda_re", "s5_lambda_im",
         "s5_log_dt", "s5_b_re", "s5_b_im", "s5_c_re", "s5_c_im", "s5_d", "s5_glu_w", "s5_glu_b", "gla_a_up_w",
         "gla_a_up_b", "gla_out_norm", "proj_s5", "proj_gla", "w_out", "ffn2_norm", "ffn2_w1", "ffn2_w3", "ffn2_w2",
         "final_norm")


def kernel(*args):
    nw = len(NAMES)
    x = args[0][0]
    wts = dict(zip(NAMES, args[1:1 + nw]))
    target = args[1 + nw][0]
    mom = dict(zip(NAMES, args[2 + nw:2 + 2 * nw]))
    var = dict(zip(NAMES, args[2 + 2 * nw:2 + 3 * nw]))

    shards = {n: wts[n][0] for n in BIG}
    rows = {n: _shard_rows(n, shards[n]).astype(BF16) for n in BIG}
    def row_layout(n, a):
        return a.T if n in ROW_ADAM else a

    opt = {n: tuple(row_layout(n, d[n][0]) for d in (wts, mom, var)) for n in GROUPS[0] + GROUPS[2]}
    _, dx, landed, small_slab, updated = _local_step(x, target, {n: wts[n] for n in SMALL}, None, rows, opt)
    loss = small_slab.reshape(-1)[SMALL_N]
    g_small = _unpack_small(small_slab)

    grad, delta, new_m, new_v = {}, {}, {}, {}
    for n, arrays in updated.items():
        grad[n], delta[n], new_m[n], new_v[n] = (row_layout(n, a)[None] for a in arrays)
    for n in GROUPS[1]:
        g_rows = _sum_slabs(landed[n], "sum_" + n)
        if n in ROW_ADAM:
            g = g_rows[:W_IN_ROWS] if n == "w_in" else g_rows
            outs = _adamw(shards[n].T, g, mom[n][0].T, var[n][0].T, "adamw_" + n)
            grad[n], delta[n], new_m[n], new_v[n] = (a.T[None] for a in (g, *outs))
        else:
            g = _unshard_rows(n, g_rows, shards[n].shape)
            outs = _adamw(shards[n], g, mom[n][0], var[n][0], "adamw_" + n)
            grad[n], delta[n], new_m[n], new_v[n] = (a[None] for a in (g, *outs))

    def flat2d(a):
        return a.reshape(-1, a.shape[-1])

    operands = ([flat2d(_working(n, d[n])) for n in SMALL] for d in (wts, mom, var))
    w2d, m2d, v2d = operands
    outs = _adamw_many(w2d, [flat2d(g_small[n]) for n in SMALL], m2d, v2d, "adamw_small")
    for out, arrays in zip((grad, delta, new_m, new_v), ([g_small[n] for n in SMALL], *outs)):
        out.update({n: _declared(n, a.reshape(g_small[n].shape)) for n, a in zip(SMALL, arrays)})
    return (loss, dx[None], *(d[n] for d in (grad, delta, new_m, new_v) for n in NAMES))
```

```python
import functools
import math

import jax
import jax.numpy as jnp
from jax import lax
from jax.experimental import pallas as pl
from jax.experimental.pallas import tpu as pltpu

F32, BF16 = jnp.float32, jnp.bfloat16
HIGHEST = lax.Precision.HIGHEST

D_MODEL = 1024
D_FF = 2816
N_DEV = 8
S5_WIDTH, S5_GROUPS, S5_GROUP, S5_STATE = 512, 32, 16, 64
S5_BLOCKS = 4
S5_BSTATE = 512
S5_SEGS = 8
GLA_HEADS, GLA_DK, GLA_DV = 4, 64, 128
GLA_KEY, GLA_VAL, GLA_RANK, GLA_CHUNK = 256, 512, 16, 64
GLA_TAU = 16.0
GLA_STEP_CHUNKS = 4
EPS = 1e-6
IN_SIZES = (512, 256, 256, 512, 512, 16, 1024, 1024)
IN_OFFS = tuple(sum(IN_SIZES[:i]) for i in range(len(IN_SIZES)))
IN_COLS = sum(IN_SIZES)
ADAM_LR, ADAM_B1, ADAM_B2, ADAM_EPS, ADAM_WD, ADAM_STEP = 0.001, 0.9, 0.999, 1e-08, 0.01, 10
GELU_C0 = math.sqrt(2.0 / math.pi)
GELU_C1 = 0.044715

FFN_FT = 256
VMEM_LIMIT_BYTES = 56 * 1024 * 1024

VMEM_FULL = pl.BlockSpec(memory_space=pltpu.VMEM)
ANY = pl.BlockSpec(memory_space=pl.ANY)


def _cparams(n_grid):
    return pltpu.CompilerParams(dimension_semantics=("arbitrary",) * n_grid, vmem_limit_bytes=VMEM_LIMIT_BYTES)


def _tile(t):
    return 512 if t >= 1024 else t // 2


def _nn(a, b):
    return jnp.dot(a, b, preferred_element_type=F32)


def _nt(a, b):
    return lax.dot_general(a, b, (((1,), (1,)), ((), ())), preferred_element_type=F32)


def _tn(a, b):
    return lax.dot_general(a, b, (((0,), (0,)), ((), ())), preferred_element_type=F32)


def _rms_parts(x):
    r = lax.rsqrt(jnp.mean(x * x, axis=-1, keepdims=True) + EPS)
    return x * r, r


def _rms_bwd(dn, g, xhat, r):
    dxh = dn * g
    dx = r * (dxh - xhat * jnp.mean(dxh * xhat, axis=-1, keepdims=True))
    return dx, jnp.sum(dn * xhat, axis=0, keepdims=True)


def _peers():
    x, y, c = lax.axis_index("x"), lax.axis_index("y"), lax.axis_index("c")
    out = []
    for k in range(1, N_DEV):
        px = 1 - x if k & 4 else x
        py = 1 - y if k & 2 else y
        pc = 1 - c if k & 1 else c
        out.append(((px, py, pc), 4 * px + 2 * py + pc))
    return 4 * x + 2 * y + c, out


def _exchange_copies(src_refs, out_refs, send_sems, recv_sems, local_sems, scatter, with_recvs):
    me, peers = _peers()
    locals_, sends, recvs = [], [], []
    for a, (src_ref, out_ref) in enumerate(zip(src_refs, out_refs)):
        def mine(idx, src_ref=src_ref):
            return src_ref.at[idx] if scatter else src_ref

        locals_.append(pltpu.make_async_copy(mine(me), out_ref.at[me], local_sems.at[a]))
        for k, (dev, idx) in enumerate(peers):
            sends.append(pltpu.make_async_remote_copy(
                src_ref=mine(idx), dst_ref=out_ref.at[me], send_sem=send_sems.at[a, k], recv_sem=recv_sems.at[a, k],
                device_id=dev, device_id_type=pl.DeviceIdType.MESH))
            if with_recvs:
                recvs.append(pltpu.make_async_remote_copy(
                    src_ref=mine(idx), dst_ref=out_ref.at[idx], send_sem=send_sems.at[a, k],
                    recv_sem=recv_sems.at[a, k], device_id=dev, device_id_type=pl.DeviceIdType.MESH))
    return locals_, sends, recvs


def _remote(src, dst, send_sems, recv_sems, a, k, dev):
    return pltpu.make_async_remote_copy(src_ref=src, dst_ref=dst, send_sem=send_sems.at[a, k],
                                        recv_sem=recv_sems.at[a, k], device_id=dev,
                                        device_id_type=pl.DeviceIdType.MESH)


def _gather_places():
    x, y, c = lax.axis_index("x"), lax.axis_index("y"), lax.axis_index("c")
    chips = [(1 - x, y), (x, 1 - y), (1 - x, 1 - y)]
    sibling = (x, y, 1 - c)
    me_idx, sib_idx = 4 * x + 2 * y + c, 4 * x + 2 * y + 1 - c
    same_core = [((cx, cy, c), 4 * cx + 2 * cy + c) for cx, cy in chips]
    other_core_idx = [4 * cx + 2 * cy + 1 - c for cx, cy in chips]
    return sibling, me_idx, sib_idx, same_core, other_core_idx


def _gather_start(src_refs, out_refs, send_sems, recv_sems, local_sems):
    sibling, me_idx, _, same_core, _ = _gather_places()
    for a, (src, out) in enumerate(zip(src_refs, out_refs)):
        pltpu.make_async_copy(src, out.at[me_idx], local_sems.at[a]).start()
        _remote(src, out.at[me_idx], send_sems, recv_sems, a, 0, sibling).start()
        for j, (dev, _) in enumerate(same_core):
            _remote(src, out.at[me_idx], send_sems, recv_sems, a, 1 + j, dev).start()


def _gather_forward(src_refs, out_refs, send_sems, recv_sems, local_sems):
    sibling, _, _, same_core, _ = _gather_places()
    for a, (src, out) in enumerate(zip(src_refs, out_refs)):
        for j, (dev, idx) in enumerate(same_core):
            _remote(src, out.at[idx], send_sems, recv_sems, a, 1 + j, dev).wait_recv()
            _remote(out.at[idx], out.at[idx], send_sems, recv_sems, a, 4 + j, sibling).start()


def _gather_finish(src_refs, out_refs, send_sems, recv_sems, local_sems):
    sibling, me_idx, sib_idx, same_core, other_core_idx = _gather_places()
    for a, (src, out) in enumerate(zip(src_refs, out_refs)):
        _remote(src, out.at[sib_idx], send_sems, recv_sems, a, 0, sibling).wait_recv()
        for j, idx in enumerate(other_core_idx):
            _remote(src, out.at[idx], send_sems, recv_sems, a, 4 + j, sibling).wait_recv()
        _remote(src, out.at[me_idx], send_sems, recv_sems, a, 0, sibling).wait_send()
        for j, (dev, idx) in enumerate(same_core):
            _remote(src, out.at[me_idx], send_sems, recv_sems, a, 1 + j, dev).wait_send()
            _remote(out.at[idx], out.at[idx], send_sems, recv_sems, a, 4 + j, sibling).wait_send()
        pltpu.make_async_copy(src, out.at[me_idx], local_sems.at[a]).wait()


def _exchange_start(*refs, scatter):
    locals_, sends, _ = _exchange_copies(*refs, scatter=scatter, with_recvs=False)
    for cp in locals_ + sends:
        cp.start()


def _exchange_wait(*refs, scatter):
    locals_, sends, recvs = _exchange_copies(*refs, scatter=scatter, with_recvs=True)
    for cp in recvs:
        cp.wait_recv()
    for cp in sends:
        cp.wait_send()
    for cp in locals_:
        cp.wait()


def _halves_places():
    x, y, c = lax.axis_index("x"), lax.axis_index("y"), lax.axis_index("c")
    flips = [(1 - x, y), (x, 1 - y), (1 - x, 1 - y)]
    return (x, y, 1 - c), c, 2 * x + y, [((fx, fy, c), 2 * fx + fy) for fx, fy in flips]


def _pair_start(src_refs, out_refs, send_sems, recv_sems, local_sems):
    sibling, c, _, _ = _halves_places()
    for a, (src, out) in enumerate(zip(src_refs, out_refs)):
        for i in range(4):
            _remote(src.at[2 * i + 1 - c], out.at[i], send_sems, recv_sems, a, i, sibling).start()


def _pair_finish(src_refs, out_refs, send_sems, recv_sems, local_sems):
    sibling, c, _, _ = _halves_places()
    for a, (src, out) in enumerate(zip(src_refs, out_refs)):
        for i in range(4):
            _remote(src.at[2 * i + 1 - c], out.at[i], send_sems, recv_sems, a, i, sibling).wait()


def _chips_start(src_refs, out_refs, send_sems, recv_sems, local_sems):
    _, _, chip, others = _halves_places()
    for a, (src, out) in enumerate(zip(src_refs, out_refs)):
        pltpu.make_async_copy(src.at[chip], out.at[chip], local_sems.at[a]).start()
        for k, (dev, their_chip) in enumerate(others):
            _remote(src.at[their_chip], out.at[chip], send_sems, recv_sems, a, k, dev).start()


def _chips_finish(src_refs, out_refs, send_sems, recv_sems, local_sems):
    _, _, chip, others = _halves_places()
    for a, (src, out) in enumerate(zip(src_refs, out_refs)):
        for k, (dev, their_chip) in enumerate(others):
            _remote(src.at[their_chip], out.at[their_chip], send_sems, recv_sems, a, k, dev).wait_recv()
        for k, (dev, their_chip) in enumerate(others):
            _remote(src.at[their_chip], out.at[chip], send_sems, recv_sems, a, k, dev).wait_send()
        pltpu.make_async_copy(src.at[chip], out.at[chip], local_sems.at[a]).wait()


EXCHANGES = {
    "gather": (_gather_start, _gather_forward, _gather_finish, N_DEV, False),
    "scatter": (functools.partial(_exchange_start, scatter=True), None, functools.partial(_exchange_wait, scatter=True),
                N_DEV, True),
    "pair": (_pair_start, None, _pair_finish, 4, True),
    "chips": (_chips_start, None, _chips_finish, 4, True),
}


def _exchange_sems(n_arrays):
    return [pltpu.SemaphoreType.DMA((n_arrays, N_DEV - 1)), pltpu.SemaphoreType.DMA((n_arrays, N_DEV - 1)),
            pltpu.SemaphoreType.DMA((n_arrays,))]


def _exchange_shapes(srcs, kind):
    lead, slabbed = EXCHANGES[kind][3:]
    return [jax.ShapeDtypeStruct((lead,) + tuple(s.shape[1:] if slabbed else s.shape), s.dtype) for s in srcs]


def _carries(carry):
    if carry is None:
        return []
    return [carry] if isinstance(carry, tuple) else list(carry)


def _call(body, *, name, grid, in_specs, out_specs, out_shape, args, scratch_shapes=(), carry=None):
    n_in, n_out, n_scr = len(in_specs), len(out_specs), len(scratch_shapes)
    groups = _carries(carry)
    sizes = [len(arrays) for arrays, _ in groups]
    nc = sum(sizes)

    def wrapped(*refs):
        ins, refs = refs[:n_in], refs[n_in:]
        csrc, refs = refs[:nc], refs[nc:]
        outs, refs = refs[:n_out], refs[n_out:]
        cland, refs = refs[:nc], refs[nc:]
        scr, sems = refs[:n_scr], refs[n_scr:]

        def run(phase):
            at = 0
            for gi, ((_, kind), size) in enumerate(zip(groups, sizes)):
                if EXCHANGES[kind][phase] is not None:
                    EXCHANGES[kind][phase](csrc[at:at + size], cland[at:at + size], *sems[3 * gi:3 * gi + 3])
                at += size

        last = pl.program_id(0) == grid[0] - 1
        if nc:
            pl.when(pl.program_id(0) == 0)(functools.partial(run, 0))
            pl.when(last)(functools.partial(run, 1))
        if body is not None:
            body(*ins, *outs, *scr)
        if nc:
            pl.when(last)(functools.partial(run, 2))

    res = pl.pallas_call(
        wrapped, name=name, grid=grid,
        in_specs=list(in_specs) + [ANY] * nc, out_specs=list(out_specs) + [ANY] * nc,
        out_shape=list(out_shape) + [s for arrays, kind in groups for s in _exchange_shapes(arrays, kind)],
        scratch_shapes=list(scratch_shapes) + [s for size in sizes for s in _exchange_sems(size)],
        compiler_params=_cparams(1),
    )(*args, *[a for arrays, _ in groups for a in arrays])
    return res[:n_out], res[n_out:]


def _row_tile(tm, d):
    return pl.BlockSpec((tm, d), lambda i: (i, 0))


def _acc_row(d):
    return pl.BlockSpec((1, d), lambda i: (0, 0))


def _ffn_body(x_ref, g_ref, w1_ref, w3_ref, w2_ref, acc_ref, a_ref, b_ref, n_ref):
    xv = x_ref[...]
    xhat, _ = _rms_parts(xv)
    n = (xhat * g_ref[...]).astype(BF16)
    n_ref[...] = n
    acc_ref[...] = xv

    def fstep(f, c):
        rows = pl.ds(pl.multiple_of(f * FFN_FT, FFN_FT), FFN_FT)
        a = _nt(n, w1_ref[rows, :])
        b = _nt(n, w3_ref[rows, :])
        a_ref[f] = a.astype(BF16)
        b_ref[f] = b.astype(BF16)
        s = (a * jax.nn.sigmoid(a) * b).astype(BF16)
        acc_ref[...] += 0.5 * _nn(s, w2_ref[rows, :])
        return c

    lax.fori_loop(0, D_FF // FFN_FT, fstep, 0, unroll=True)


def _ffn_fwd(x, g, w1t, w3t, w2, name, carry=None):
    t = x.shape[0]
    tm = _tile(t)
    nf = D_FF // FFN_FT
    blk3 = pl.BlockSpec((nf, tm, FFN_FT), lambda i: (0, i, 0))
    sh3 = jax.ShapeDtypeStruct((nf, t, FFN_FT), BF16)
    (h, a3, b3, n), landed = _call(
        functools.partial(_ffn_body), name=name, grid=(t // tm,),
        in_specs=[_row_tile(tm, D_MODEL), _acc_row(D_MODEL), VMEM_FULL, VMEM_FULL, VMEM_FULL],
        out_specs=[_row_tile(tm, D_MODEL), blk3, blk3, _row_tile(tm, D_MODEL)],
        out_shape=[jax.ShapeDtypeStruct((t, D_MODEL), F32), sh3, sh3, jax.ShapeDtypeStruct((t, D_MODEL), BF16)],
        args=(x, g, w1t, w3t, w2), carry=carry)
    return h, (a3, b3, n), landed


def _ffn_fwd_head(x, g, w1t, w3t, w2, gf, target, name):
    t = x.shape[0]
    tm = _tile(t)
    nf = D_FF // FFN_FT

    def body(x_ref, g_ref, w1_ref, w3_ref, w2_ref, gf_ref, t_ref, loss_ref, dh_ref, dgf_ref, a_ref, b_ref, n_ref, acc):
        _ffn_body(x_ref, g_ref, w1_ref, w3_ref, w2_ref, acc, a_ref, b_ref, n_ref)
        _head_math(acc[...], gf_ref[...], t_ref[...], loss_ref, dh_ref, dgf_ref)

    blk3 = pl.BlockSpec((nf, tm, FFN_FT), lambda i: (0, i, 0))
    sh3 = jax.ShapeDtypeStruct((nf, t, FFN_FT), BF16)
    (loss, dh, dgf, a3, b3, n), _ = _call(
        body, name=name, grid=(t // tm,),
        in_specs=[_row_tile(tm, D_MODEL), _acc_row(D_MODEL), VMEM_FULL, VMEM_FULL, VMEM_FULL, _acc_row(D_MODEL),
                  _row_tile(tm, D_MODEL)],
        out_specs=[pl.BlockSpec((1, 1), lambda i: (0, 0)), _row_tile(tm, D_MODEL), _acc_row(D_MODEL), blk3, blk3,
                   _row_tile(tm, D_MODEL)],
        out_shape=[jax.ShapeDtypeStruct((1, 1), F32), jax.ShapeDtypeStruct((t, D_MODEL), F32),
                   jax.ShapeDtypeStruct((1, D_MODEL), F32), sh3, sh3, jax.ShapeDtypeStruct((t, D_MODEL), BF16)],
        scratch_shapes=[pltpu.VMEM((tm, D_MODEL), F32)],
        args=(x, g, w1t, w3t, w2, gf, target))
    return loss, dh, dgf, (a3, b3, n)


def _head_math(h, gv, target, loss_ref, dh_ref, dg_ref):
    i = pl.program_id(0)
    xhat, r = _rms_parts(h)
    err = xhat * gv - target
    dx, dg = _rms_bwd(err * (1.0 / D_MODEL), gv, xhat, r)
    dh_ref[...] = dx

    @pl.when(i == 0)
    def _():
        loss_ref[...] = jnp.zeros_like(loss_ref)
        dg_ref[...] = jnp.zeros_like(dg_ref)

    loss_ref[...] += (0.5 / D_MODEL) * jnp.sum(jnp.sum(err * err, axis=1, keepdims=True), axis=0, keepdims=True)
    dg_ref[...] += dg


def _ffn_bwd(x, dh, g, a3, b3, w1t, w3t, w2, name, carry=None):
    t = x.shape[0]
    tm = _tile(t) // 2
    nf = D_FF // FFN_FT

    def body(x_ref, dh_ref, g_ref, a_ref, b_ref, w1_ref, w3_ref, w2_ref,
             dx_ref, dg_ref, da_ref, db_ref, s_ref, dhh_ref, dn_acc):
        i = pl.program_id(0)
        xv = x_ref[...]
        gv = g_ref[...]
        xhat, r = _rms_parts(xv)
        dhv = dh_ref[...]
        dhh = (0.5 * dhv).astype(BF16)
        dhh_ref[...] = dhh
        dn_acc[...] = jnp.zeros_like(dn_acc)

        def fstep(f, c):
            rows = pl.ds(f * FFN_FT, FFN_FT)
            w1c, w3c, w2c = w1_ref[rows, :], w3_ref[rows, :], w2_ref[rows, :]
            a = a_ref[f].astype(F32)
            b = b_ref[f].astype(F32)
            sg = jax.nn.sigmoid(a)
            sl = a * sg
            ds = _nt(dhh, w2c)
            da = (ds * b * sg * (1.0 + a * (1.0 - sg))).astype(BF16)
            db = (ds * sl).astype(BF16)
            s_ref[f] = (sl * b).astype(BF16)
            da_ref[f] = da
            db_ref[f] = db
            return c

        def nstep(f, c):
            rows = pl.ds(f * FFN_FT, FFN_FT)
            dn_acc[...] += _nn(da_ref[f], w1_ref[rows, :]) + _nn(db_ref[f], w3_ref[rows, :])
            return c

        for f in range(nf + 1):
            if f < nf:
                fstep(f, 0)
            if f:
                nstep(f - 1, 0)
        dx, dg = _rms_bwd(dn_acc[...], gv, xhat, r)
        dx_ref[...] = dhv + dx

        @pl.when(i == 0)
        def _():
            dg_ref[...] = jnp.zeros_like(dg_ref)

        dg_ref[...] += dg

    blk3 = pl.BlockSpec((nf, tm, FFN_FT), lambda i: (0, i, 0))
    sh3 = jax.ShapeDtypeStruct((nf, t, FFN_FT), BF16)
    return _call(
        body, name=name, grid=(t // tm,),
        in_specs=[_row_tile(tm, D_MODEL), _row_tile(tm, D_MODEL), _acc_row(D_MODEL), blk3, blk3,
                  VMEM_FULL, VMEM_FULL, VMEM_FULL],
        out_specs=[_row_tile(tm, D_MODEL), _acc_row(D_MODEL), blk3, blk3, blk3, _row_tile(tm, D_MODEL)],
        out_shape=[jax.ShapeDtypeStruct((t, D_MODEL), F32), jax.ShapeDtypeStruct((1, D_MODEL), F32), sh3, sh3, sh3,
                   jax.ShapeDtypeStruct((t, D_MODEL), BF16)],
        scratch_shapes=[pltpu.VMEM((tm, D_MODEL), F32)],
        args=(x, dh, g, a3, b3, w1t, w3t, w2), carry=carry)


def _mm_tn(a, b, name, carry=None):
    t, n = b.shape
    kc = min(512, t)
    if a.ndim == 3:
        nb, _, tb = a.shape
        a_spec = pl.BlockSpec((1, t, tb), lambda i: (i, 0, 0))
    else:
        m = a.shape[1]
        tb = min(m, 256)
        nb = m // tb
        a_spec = pl.BlockSpec((t, tb), lambda i: (0, i))
    three_d = a.ndim == 3

    def body(a_ref, b_ref, o_ref, acc):
        acc[...] = jnp.zeros_like(acc)

        def kstep(k, c):
            rows = pl.ds(pl.multiple_of(k * kc, kc), kc)
            av = a_ref[0, rows, :] if three_d else a_ref[rows, :]
            acc[...] += _tn(av.astype(BF16), b_ref[rows, :])
            return c

        lax.fori_loop(0, t // kc, kstep, 0, unroll=True)
        o_ref[...] = acc[...].astype(BF16)

    (out,), landed = _call(
        body, name=name, grid=(nb,),
        in_specs=[a_spec, VMEM_FULL],
        out_specs=[pl.BlockSpec((tb, n), lambda i: (i, 0))],
        out_shape=[jax.ShapeDtypeStruct((nb * tb, n), BF16)],
        scratch_shapes=[pltpu.VMEM((tb, n), F32)],
        args=(a, b), carry=carry)
    return (out, landed) if carry is not None else out


MM_TB = 256


def _mm_tn_many(arrays, b, name):
    t, n = b.shape
    kc = min(512, t)
    counts = [a.shape[1] // MM_TB for a in arrays]
    starts = [sum(counts[:k]) for k in range(len(arrays))]

    def spec(start, count):
        return pl.BlockSpec((t, MM_TB), lambda i: (0, jnp.clip(i - start, 0, count - 1)))

    def body(*refs):
        a_refs, (b_ref, o_ref, acc) = refs[:len(arrays)], refs[len(arrays):]
        i = pl.program_id(0)
        for a_ref, start, count in zip(a_refs, starts, counts):
            @pl.when((i >= start) & (i < start + count))
            def _(a_ref=a_ref):
                acc[...] = jnp.zeros_like(acc)

                def kstep(k, c):
                    rows = pl.ds(pl.multiple_of(k * kc, kc), kc)
                    acc[...] += _tn(a_ref[rows, :].astype(BF16), b_ref[rows, :])
                    return c

                lax.fori_loop(0, t // kc, kstep, 0, unroll=True)
                o_ref[...] = acc[...].astype(BF16)

    return pl.pallas_call(
        body, name=name, grid=(sum(counts),),
        in_specs=[spec(s, c) for s, c in zip(starts, counts)] + [VMEM_FULL],
        out_specs=pl.BlockSpec((MM_TB, n), lambda i: (i, 0)),
        out_shape=jax.ShapeDtypeStruct((sum(counts) * MM_TB, n), BF16),
        scratch_shapes=[pltpu.VMEM((MM_TB, n), F32)],
        compiler_params=_cparams(1),
    )(*arrays, b)


def _mix_pre_fwd(h, g, wint, carry=None):
    t = h.shape[0]
    tm = _tile(t)

    def body(h_ref, g_ref, w_ref, u_ref, *outs):
        xhat, _ = _rms_parts(h_ref[...])
        u = (xhat * g_ref[...]).astype(BF16)
        u_ref[...] = u
        for o_ref, off, size in zip(outs, IN_OFFS, IN_SIZES):
            o_ref[...] = _nt(u, w_ref[off:off + size, :])

    return _call(
        body, name="mix_pre_fwd", grid=(t // tm,),
        in_specs=[_row_tile(tm, D_MODEL), _acc_row(D_MODEL), VMEM_FULL],
        out_specs=[_row_tile(tm, D_MODEL)] + [_row_tile(tm, s) for s in IN_SIZES],
        out_shape=[jax.ShapeDtypeStruct((t, D_MODEL), BF16)] + [jax.ShapeDtypeStruct((t, s), F32) for s in IN_SIZES],
        args=(h, g, wint), carry=carry)


def _mix_pre_bwd(h, g, wint, dh2, dz, carry=None):
    t = h.shape[0]
    tm = _tile(t)

    def body(h_ref, g_ref, w_ref, dh2_ref, *rest):
        dz_refs, (dh1_ref, dg_ref) = rest[:len(IN_SIZES)], rest[len(IN_SIZES):]
        i = pl.program_id(0)
        gv = g_ref[...]
        xhat, r = _rms_parts(h_ref[...])
        du = jnp.zeros((tm, D_MODEL), F32)
        for dz_ref, off, size in zip(dz_refs, IN_OFFS, IN_SIZES):
            du = du + _nn(dz_ref[...].astype(BF16), w_ref[off:off + size, :])
        dx, dg = _rms_bwd(du, gv, xhat, r)
        dh1_ref[...] = dh2_ref[...] + dx

        @pl.when(i == 0)
        def _():
            dg_ref[...] = jnp.zeros_like(dg_ref)

        dg_ref[...] += dg

    return _call(
        body, name="mix_pre_bwd", grid=(t // tm,),
        in_specs=[_row_tile(tm, D_MODEL), _acc_row(D_MODEL), VMEM_FULL, _row_tile(tm, D_MODEL)]
        + [_row_tile(tm, s) for s in IN_SIZES],
        out_specs=[_row_tile(tm, D_MODEL), _acc_row(D_MODEL)],
        out_shape=[jax.ShapeDtypeStruct((t, D_MODEL), F32), jax.ShapeDtypeStruct((1, D_MODEL), F32)],
        args=(h, g, wint, dh2, *dz), carry=carry)


def _disc_math(lre, lim, ldt, bre, bim):
    dt = jnp.exp(ldt)
    mag = jnp.exp(lre * dt)
    ar = mag * jnp.cos(lim * dt)
    ai = mag * jnp.sin(lim * dt)
    den = lre * lre + lim * lim
    nr = ar - 1.0
    fr = (nr * lre + ai * lim) / den
    fi = (ai * lre - nr * lim) / den
    fr, fi = fr[:, None, :], fi[:, None, :]
    return ar, ai, fr * bre - fi * bim, fr * bim + fi * bre


def _s5_disc(lre, lim, ldt, bre, bim):
    def body(lre_ref, lim_ref, ldt_ref, bre_ref, bim_ref, ar_ref, ai_ref, bbr_ref, bbi_ref):
        ar, ai, bbr, bbi = _disc_math(lre_ref[...], lim_ref[...], ldt_ref[...], bre_ref[...], bim_ref[...])
        ar_ref[...] = ar
        ai_ref[...] = ai
        bbr_ref[...] = bbr
        bbi_ref[...] = bbi

    small = jax.ShapeDtypeStruct(lre.shape, F32)
    big = jax.ShapeDtypeStruct(bre.shape, F32)
    return pl.pallas_call(body, name="s5_disc", out_shape=[small, small, big, big],
                          in_specs=[VMEM_FULL] * 5, out_specs=[VMEM_FULL] * 4)(lre, lim, ldt, bre, bim)


def _s5_disc_bwd(lre, lim, ldt, bre, bim, dar, dai, dbbr, dbbi):
    def body(lre_ref, lim_ref, ldt_ref, bre_ref, bim_ref, dar_ref, dai_ref, dbbr_ref, dbbi_ref,
             glre_ref, glim_ref, gldt_ref, gbre_ref, gbim_ref):
        _, vjp = jax.vjp(_disc_math, lre_ref[...], lim_ref[...], ldt_ref[...], bre_ref[...], bim_ref[...])
        glre, glim, gldt, gbre, gbim = vjp((dar_ref[...], dai_ref[...], dbbr_ref[...], dbbi_ref[...]))
        glre_ref[...] = glre
        glim_ref[...] = glim
        gldt_ref[...] = gldt
        gbre_ref[...] = gbre
        gbim_ref[...] = gbim

    small = jax.ShapeDtypeStruct(lre.shape, F32)
    big = jax.ShapeDtypeStruct(bre.shape, F32)
    return pl.pallas_call(body, name="s5_disc_bwd",
                          out_shape=[small, small, jax.ShapeDtypeStruct(ldt.shape, F32), big, big],
                          in_specs=[VMEM_FULL] * 9, out_specs=[VMEM_FULL] * 5,
                          )(lre, lim, ldt, bre, bim, dar, dai, dbbr, dbbi)


def _cmul(ar, ai, br, bi):
    return ar * br - ai * bi, ar * bi + ai * br


def _cpow(ar, ai, n):
    rr, ri = None, None
    pr, pi = ar, ai
    while n:
        if n & 1:
            rr, ri = (pr, pi) if rr is None else _cmul(rr, ri, pr, pi)
        n >>= 1
        if n:
            pr, pi = _cmul(pr, pi, pr, pi)
    return rr, ri


def _shift_rows(v, down):
    row = lax.broadcasted_iota(jnp.int32, v.shape, 0)
    if down:
        return jnp.where(row == 0, 0.0, pltpu.roll(v, 1, 0))
    return jnp.where(row == S5_SEGS - 1, 0.0, pltpu.roll(v, S5_SEGS - 1, 0))


def _chain_segments(er, ei, pr, pi, down):
    fr, fi = er, ei
    for _ in range(S5_SEGS - 1):
        sr, si = _shift_rows(fr, down), _shift_rows(fi, down)
        mr, mi = _cmul(pr, pi, sr, si)
        fr, fi = er + mr, ei + mi
    return _shift_rows(fr, down), _shift_rows(fi, down)


def _rows_to_scan_order(src_ref, dst_ref, t):
    ls = t // S5_SEGS

    def tile(j, c):
        dst_ref[pl.ds(pl.multiple_of(j * S5_SEGS, S5_SEGS), S5_SEGS), :] = src_ref[pl.ds(j, S5_SEGS, stride=ls), :]
        return c

    lax.fori_loop(0, ls, tile, 0, unroll=8)


def _rows_from_scan_order(src_ref, dst_ref, t):
    ls = t // S5_SEGS
    for s in range(S5_SEGS):
        def tile(jb, c, s=s):
            dst_ref[pl.ds(pl.multiple_of(s * ls + jb * 8, 8), 8), :] = (
                src_ref[pl.ds(jb * 8 * S5_SEGS + s, 8, stride=S5_SEGS), :])
            return c

        lax.fori_loop(0, ls // 8, tile, 0, unroll=8)


def _s5_fwd(ug, bd, ctd, ar4, ai4, dskip, carry=None):
    t = ug.shape[0]
    ls = t // S5_SEGS
    rc = min(512, t)
    ns = S5_BSTATE

    def body(ugn_ref, bd_ref, ct_ref, ar_ref, ai_ref, d_ref, xs_hbm, yn_ref, buf, ug_ref, y_ref, sem):
        cb = pl.program_id(0)
        bdv = bd_ref[0]
        _rows_to_scan_order(ugn_ref, ug_ref, t)

        def mm(i, c):
            rows = pl.ds(pl.multiple_of(i * rc, rc), rc)
            buf[rows, :] = _nn(ug_ref[rows, :].astype(BF16), bdv)
            return c

        lax.fori_loop(0, t // rc, mm, 0, unroll=True)
        arb = jnp.broadcast_to(ar_ref[0], (S5_SEGS, ns))
        aib = jnp.broadcast_to(ai_ref[0], (S5_SEGS, ns))

        def step(j, c, store):
            sr, si = c
            rows = pl.ds(pl.multiple_of(j * S5_SEGS, S5_SEGS), S5_SEGS)
            nr = arb * sr - aib * si + buf[rows, 0:ns]
            ni = arb * si + aib * sr + buf[rows, ns:2 * ns]
            if store:
                buf[rows, 0:ns] = nr
                buf[rows, ns:2 * ns] = ni
            return nr, ni

        zero = jnp.zeros((S5_SEGS, ns), F32)
        er, ei = lax.fori_loop(0, ls, functools.partial(step, store=False), (zero, zero))
        pr, pi = _cpow(arb, aib, ls)
        init = _chain_segments(er, ei, pr, pi, down=True)
        lax.fori_loop(0, ls, functools.partial(step, store=True), init)

        out = pltpu.make_async_copy(buf, xs_hbm.at[cb], sem)
        out.start()
        ctv = ct_ref[0]
        dv = d_ref[...]

        def ymm(i, c):
            rows = pl.ds(pl.multiple_of(i * rc, rc), rc)
            y_ref[rows, :] = _nn(buf[rows, :].astype(BF16), ctv) + dv * ug_ref[rows, :]
            return c

        lax.fori_loop(0, t // rc, ymm, 0, unroll=True)
        _rows_from_scan_order(y_ref, yn_ref, t)
        out.wait()

    return _call(
        body, name="s5_fwd", grid=(S5_BLOCKS,),
        in_specs=[pl.BlockSpec((t, 128), lambda i: (0, i)),
                  pl.BlockSpec((1, 128, 2 * ns), lambda i: (i, 0, 0)),
                  pl.BlockSpec((1, 2 * ns, 128), lambda i: (i, 0, 0)),
                  pl.BlockSpec((1, 1, ns), lambda i: (i, 0, 0)),
                  pl.BlockSpec((1, 1, ns), lambda i: (i, 0, 0)),
                  pl.BlockSpec((1, 128), lambda i: (0, i))],
        out_specs=[ANY, pl.BlockSpec((t, 128), lambda i: (0, i))],
        out_shape=[jax.ShapeDtypeStruct((S5_BLOCKS, t, 2 * ns), F32), jax.ShapeDtypeStruct((t, S5_WIDTH), F32)],
        scratch_shapes=[pltpu.VMEM((t, 2 * ns), F32), pltpu.VMEM((t, 128), F32), pltpu.VMEM((t, 128), F32),
                        pltpu.SemaphoreType.DMA(())],
        args=(ug, bd, ctd, ar4, ai4, dskip), carry=carry)


def _s5_bwd(dy, ug, xs, cd, bdt, ar4, ai4, dskip, carry=None):
    t = ug.shape[0]
    ls = t // S5_SEGS
    rc = min(512, t)
    ns = S5_BSTATE

    def body(dyn_ref, ugn_ref, xs_hbm, cd_ref, bdt_ref, ar_ref, ai_ref, d_ref,
             dugn_ref, dbd_ref, dcd_ref, dd_ref, dar_ref, dai_ref, xbuf, lam, dy_ref, ug_ref, dug_ref, sem):
        cb = pl.program_id(0)
        load = pltpu.make_async_copy(xs_hbm.at[cb], xbuf, sem)
        load.start()
        cdv = cd_ref[0]
        _rows_to_scan_order(dyn_ref, dy_ref, t)
        _rows_to_scan_order(ugn_ref, ug_ref, t)

        def mm(i, c):
            rows = pl.ds(pl.multiple_of(i * rc, rc), rc)
            lam[rows, :] = _nn(dy_ref[rows, :].astype(BF16), cdv)
            return c

        lax.fori_loop(0, t // rc, mm, 0, unroll=True)
        arb = jnp.broadcast_to(ar_ref[0], (S5_SEGS, ns))
        aib = jnp.broadcast_to(ai_ref[0], (S5_SEGS, ns))

        def lam_step(j, lr, li):
            rows = pl.ds(pl.multiple_of(j * S5_SEGS, S5_SEGS), S5_SEGS)
            nr = arb * lr + aib * li + lam[rows, 0:ns]
            ni = arb * li - aib * lr + lam[rows, ns:2 * ns]
            return rows, nr, ni

        def pass1(jj, c):
            _, nr, ni = lam_step(ls - 1 - jj, *c)
            return nr, ni

        zero = jnp.zeros((S5_SEGS, ns), F32)
        er, ei = lax.fori_loop(0, ls, pass1, (zero, zero))
        pr, pi = _cpow(arb, aib, ls)
        init = _chain_segments(er, ei, pr, -pi, down=False)
        load.wait()

        def accumulate(acc, nr, ni, xpr, xpi):
            return acc[0] + nr * xpr + ni * xpi, acc[1] + ni * xpr - nr * xpi

        def pass2(jj, c):
            lr, li, accr, acci = c
            j = ls - 1 - jj
            rows, nr, ni = lam_step(j, lr, li)
            lam[rows, 0:ns] = nr
            lam[rows, ns:2 * ns] = ni
            prev = pl.ds(pl.multiple_of((j - 1) * S5_SEGS, S5_SEGS), S5_SEGS)
            accr, acci = accumulate((accr, acci), nr, ni, xbuf[prev, 0:ns], xbuf[prev, ns:2 * ns])
            return nr, ni, accr, acci

        lr, li, accr, acci = lax.fori_loop(0, ls - 1, pass2, (init[0], init[1], zero, zero))
        rows, nr, ni = lam_step(0, lr, li)
        lam[rows, 0:ns] = nr
        lam[rows, ns:2 * ns] = ni
        last = pl.ds((ls - 1) * S5_SEGS, S5_SEGS)
        accr, acci = accumulate((accr, acci), nr, ni,
                                _shift_rows(xbuf[last, 0:ns], True), _shift_rows(xbuf[last, ns:2 * ns], True))
        dar_ref[0] = jnp.sum(accr, axis=0, keepdims=True)
        dai_ref[0] = jnp.sum(acci, axis=0, keepdims=True)

        bdtv = bdt_ref[0]
        dv = d_ref[...]
        dbd_ref[...] = jnp.zeros_like(dbd_ref)
        dcd_ref[...] = jnp.zeros_like(dcd_ref)
        dd_ref[...] = jnp.zeros_like(dd_ref)

        def tail(i, c):
            rows = pl.ds(pl.multiple_of(i * rc, rc), rc)
            dy = dy_ref[rows, :]
            ug = ug_ref[rows, :]
            lb = lam[rows, :].astype(BF16)
            dug_ref[rows, :] = _nn(lb, bdtv) + dv * dy
            dbd_ref[0] += _tn(ug.astype(BF16), lb)
            dcd_ref[0] += _tn(dy.astype(BF16), xbuf[rows, :].astype(BF16))
            dd_ref[...] += jnp.sum(dy * ug, axis=0, keepdims=True)
            return c

        lax.fori_loop(0, t // rc, tail, 0, unroll=True)
        _rows_from_scan_order(dug_ref, dugn_ref, t)

    chan = pl.BlockSpec((t, 128), lambda i: (0, i))
    dense = pl.BlockSpec((1, 128, 2 * ns), lambda i: (i, 0, 0))
    vec = pl.BlockSpec((1, 1, ns), lambda i: (i, 0, 0))
    return _call(
        body, name="s5_bwd", grid=(S5_BLOCKS,),
        in_specs=[chan, chan, ANY, dense, pl.BlockSpec((1, 2 * ns, 128), lambda i: (i, 0, 0)), vec, vec,
                  pl.BlockSpec((1, 128), lambda i: (0, i))],
        out_specs=[chan, dense, dense, pl.BlockSpec((1, 128), lambda i: (0, i)), vec, vec],
        out_shape=[jax.ShapeDtypeStruct((t, S5_WIDTH), F32),
                   jax.ShapeDtypeStruct((S5_BLOCKS, 128, 2 * ns), F32),
                   jax.ShapeDtypeStruct((S5_BLOCKS, 128, 2 * ns), F32),
                   jax.ShapeDtypeStruct((1, S5_WIDTH), F32),
                   jax.ShapeDtypeStruct((S5_BLOCKS, 1, ns), F32),
                   jax.ShapeDtypeStruct((S5_BLOCKS, 1, ns), F32)],
        scratch_shapes=[pltpu.VMEM((t, 2 * ns), F32), pltpu.VMEM((t, 2 * ns), F32)]
        + [pltpu.VMEM((t, 128), F32)] * 3 + [pltpu.SemaphoreType.DMA(())],
        args=(dy, ug, xs, cd, bdt, ar4, ai4, dskip), carry=carry)


def _cumsum_rows(x, reverse):
    c = x.shape[0]
    row = lax.broadcasted_iota(jnp.int32, x.shape, 0)
    d = 1
    while d < c:
        if reverse:
            x = x + jnp.where(row < c - d, pltpu.roll(x, c - d, 0), 0.0)
        else:
            x = x + jnp.where(row >= d, pltpu.roll(x, d, 0), 0.0)
        d *= 2
    return x


def _gla_common(q, k, alow, wup, bup):
    c = GLA_CHUNK
    pre = _nn(alow.astype(BF16), wup.astype(BF16)) + bup
    la = (jnp.minimum(pre, 0.0) - jnp.log(1.0 + jnp.exp(-jnp.abs(pre)))) * (1.0 / GLA_TAU)
    rr = lax.broadcasted_iota(jnp.int32, (c, c), 0)
    cc = lax.broadcasted_iota(jnp.int32, (c, c), 1)
    tril = (rr >= cc).astype(F32)
    bc = _cumsum_rows(la, reverse=False)
    bl = bc[c - 1:c, :]
    e_pos = jnp.exp(bc)
    e_neg = jnp.exp(-bc)
    e_end = jnp.exp(bl - bc)
    qt = q * (GLA_DK ** -0.5) * e_pos
    kt = k * e_neg
    ke = k * e_end
    lane = lax.broadcasted_iota(jnp.int32, (1, GLA_KEY), 1)
    masks = [((lane >= h * GLA_DK) & (lane < (h + 1) * GLA_DK)).astype(F32) for h in range(GLA_HEADS)]
    return dict(pre=pre, tril=tril, bc=bc, bl=bl, e_pos=e_pos, e_neg=e_neg, e_end=e_end,
                qt=qt, kt=kt, ke=ke, dec=jnp.exp(bl), masks=masks)


def _gla_fwd(q, k, v, alow, wup, bup, carry=None):
    t = q.shape[0]
    c = GLA_CHUNK
    n = t // c
    step = GLA_STEP_CHUNKS * c

    def body(q_ref, k_ref, v_ref, al_ref, wup_ref, bup_ref, o_ref, ss_ref, s_ref):
        i = pl.program_id(0)

        @pl.when(i == 0)
        def _():
            s_ref[...] = jnp.zeros_like(s_ref)

        wup_v, bup_v = wup_ref[...], bup_ref[...]
        s = s_ref[...]
        for j in range(GLA_STEP_CHUNKS):
            tok = slice(j * c, (j + 1) * c)
            m = _gla_common(q_ref[tok, :], k_ref[tok, :], al_ref[tok, :], wup_v, bup_v)
            ss_ref[j] = s
            sb = s.astype(BF16)
            ktb = m["kt"].astype(BF16)
            update = jnp.zeros_like(s)
            for h in range(GLA_HEADS):
                mask = m["masks"][h]
                qm = (m["qt"] * mask).astype(BF16)
                vh = v_ref[tok, h * GLA_DV:(h + 1) * GLA_DV].astype(BF16)
                p = (m["tril"] * _nt(qm, ktb)).astype(BF16)
                o_ref[tok, h * GLA_DV:(h + 1) * GLA_DV] = _nn(p, vh) + _nt(qm, sb)
                update = update + _tn(vh, (m["ke"] * mask).astype(BF16))
            s = m["dec"] * s + update
        s_ref[...] = s

    return _call(
        body, name="gla_fwd", grid=(t // step,),
        in_specs=[_row_tile(step, GLA_KEY), _row_tile(step, GLA_KEY), _row_tile(step, GLA_VAL),
                  _row_tile(step, GLA_RANK), VMEM_FULL, VMEM_FULL],
        out_specs=[_row_tile(step, GLA_VAL), pl.BlockSpec((GLA_STEP_CHUNKS, GLA_DV, GLA_KEY), lambda i: (i, 0, 0))],
        out_shape=[jax.ShapeDtypeStruct((t, GLA_VAL), F32), jax.ShapeDtypeStruct((n, GLA_DV, GLA_KEY), F32)],
        scratch_shapes=[pltpu.VMEM((GLA_DV, GLA_KEY), F32)],
        args=(q, k, v, alow, wup, bup), carry=carry)


def _gla_bwd(q, k, v, alow, wup, bup, ssave, do, carry=None):
    t = q.shape[0]
    c = GLA_CHUNK
    n = t // c

    def body(q_ref, k_ref, v_ref, al_ref, wup_ref, bup_ref, ss_ref, do_ref,
             dq_ref, dk_ref, dv_ref, dal_ref, dwup_ref, dbup_ref, ds_ref):
        i = pl.program_id(0)

        @pl.when(i == 0)
        def _():
            ds_ref[...] = jnp.zeros_like(ds_ref)
            dwup_ref[...] = jnp.zeros_like(dwup_ref)
            dbup_ref[...] = jnp.zeros_like(dbup_ref)

        wup_v, bup_v = wup_ref[...], bup_ref[...]
        ds_in = ds_ref[...]
        dwup = jnp.zeros((GLA_RANK, GLA_KEY), F32)
        dbup = jnp.zeros((1, GLA_KEY), F32)
        for j in reversed(range(GLA_STEP_CHUNKS)):
            tok = slice(j * c, (j + 1) * c)
            alow_v = al_ref[tok, :]
            m = _gla_common(q_ref[tok, :], k_ref[tok, :], alow_v, wup_v, bup_v)
            s = ss_ref[j]
            sb = s.astype(BF16)
            dsb = ds_in.astype(BF16)
            qt, kt, ke = m["qt"], m["kt"], m["ke"]
            ktb = kt.astype(BF16)
            dqt = jnp.zeros((c, GLA_KEY), F32)
            dkt = jnp.zeros((c, GLA_KEY), F32)
            dke = jnp.zeros((c, GLA_KEY), F32)
            update = jnp.zeros_like(ds_in)
            for h in range(GLA_HEADS):
                mask = m["masks"][h]
                qm = (qt * mask).astype(BF16)
                km = (kt * mask).astype(BF16)
                kem = (ke * mask).astype(BF16)
                cols = slice(h * GLA_DV, (h + 1) * GLA_DV)
                vh = v_ref[tok, cols].astype(BF16)
                doh = do_ref[tok, cols].astype(BF16)
                p = (m["tril"] * _nt(qm, ktb)).astype(BF16)
                dp = (m["tril"] * _nt(doh, vh)).astype(BF16)
                dv_ref[tok, cols] = (_tn(p, doh) + _nt(kem, dsb)).astype(BF16)
                dqt = dqt + _nn(dp, km) + _nn(doh, sb) * mask
                dkt = dkt + _tn(dp, qm)
                dke = dke + _nn(vh, dsb) * mask
                update = update + _tn(doh, qm)
            ddec = jnp.sum(ds_in * s, axis=0, keepdims=True)
            dq_ref[tok, :] = (dqt * m["e_pos"] * (GLA_DK ** -0.5)).astype(BF16)
            dk_ref[tok, :] = (dkt * m["e_neg"] + dke * m["e_end"]).astype(BF16)
            dkeke = dke * ke
            dbl = jnp.sum(dkeke, axis=0, keepdims=True) + ddec * m["dec"]
            last = (lax.broadcasted_iota(jnp.int32, (c, 1), 0) == c - 1).astype(F32)
            dla = _cumsum_rows(dqt * qt - dkt * kt - dkeke + last * dbl, reverse=True)
            dpre = dla * (1.0 / GLA_TAU) * jax.nn.sigmoid(-m["pre"])
            dpb = dpre.astype(BF16)
            dal_ref[tok, :] = _nt(dpb, wup_v.astype(BF16)).astype(BF16)
            dwup = dwup + _tn(alow_v.astype(BF16), dpb)
            dbup = dbup + jnp.sum(dpre, axis=0, keepdims=True)
            ds_in = m["dec"] * ds_in + update
        ds_ref[...] = ds_in
        dwup_ref[...] += dwup
        dbup_ref[...] += dbup

    step = GLA_STEP_CHUNKS * c
    nsteps = t // step

    def rev(d):
        return pl.BlockSpec((step, d), lambda i: (nsteps - 1 - i, 0))

    return _call(
        body, name="gla_bwd", grid=(nsteps,),
        in_specs=[rev(GLA_KEY), rev(GLA_KEY), rev(GLA_VAL), rev(GLA_RANK), VMEM_FULL, VMEM_FULL,
                  pl.BlockSpec((GLA_STEP_CHUNKS, GLA_DV, GLA_KEY), lambda i: (nsteps - 1 - i, 0, 0)), rev(GLA_VAL)],
        out_specs=[rev(GLA_KEY), rev(GLA_KEY), rev(GLA_VAL), rev(GLA_RANK),
                   pl.BlockSpec((GLA_RANK, GLA_KEY), lambda i: (0, 0)), _acc_row(GLA_KEY)],
        out_shape=[jax.ShapeDtypeStruct((t, GLA_KEY), BF16), jax.ShapeDtypeStruct((t, GLA_KEY), BF16),
                   jax.ShapeDtypeStruct((t, GLA_VAL), BF16), jax.ShapeDtypeStruct((t, GLA_RANK), BF16),
                   jax.ShapeDtypeStruct((GLA_RANK, GLA_KEY), F32), jax.ShapeDtypeStruct((1, GLA_KEY), F32)],
        scratch_shapes=[pltpu.VMEM((GLA_DV, GLA_KEY), F32)],
        args=(q, k, v, alow, wup, bup, ssave, do), carry=carry)


def _post_math(y, o, r, gs5, ggla, wg, bg, gn, ps5t, pglat):
    y2 = y * y
    th = jnp.tanh(GELU_C0 * (y + GELU_C1 * y * y2))
    z5 = 0.5 * y * (1.0 + th)
    z5b = z5.astype(BF16)
    gate = jax.nn.sigmoid(_nn(z5b, wg) + bg)
    ys5 = z5 * gate
    rs, on = [], []
    for h in range(GLA_HEADS):
        oh = o[:, h * GLA_DV:(h + 1) * GLA_DV]
        rh = lax.rsqrt(jnp.mean(oh * oh, axis=-1, keepdims=True) + EPS)
        rs.append(rh)
        on.append(oh * rh)
    on = jnp.concatenate(on, axis=-1)
    sr = jax.nn.sigmoid(r)
    silu_r = r * sr
    ygla = on * gn * silu_r
    ys5b, yglab = ys5.astype(BF16), ygla.astype(BF16)
    m5 = _nt(ys5b, ps5t)
    mg = _nt(yglab, pglat)
    s5g, glag = jax.nn.sigmoid(gs5), jax.nn.sigmoid(ggla)
    merged = s5g * m5 + glag * mg
    return dict(y2=y2, th=th, z5=z5, z5b=z5b, gate=gate, ys5b=ys5b, yglab=yglab, rs=rs, on=on, sr=sr,
                silu_r=silu_r, m5=m5, mg=mg, s5g=s5g, glag=glag, mergedb=merged.astype(BF16))


def _mix_post_fwd(y, o, r, gs5, ggla, h1, wg, bg, gn, ps5t, pglat, wout, carry=None):
    t = o.shape[0]
    tm = _tile(t)

    def body(y_ref, o_ref, r_ref, gs5_ref, ggla_ref, h1_ref, wg_ref, bg_ref, gn_ref, ps_ref, pg_ref, wo_ref, h2_ref):
        m = _post_math(y_ref[...], o_ref[...], r_ref[...], gs5_ref[...], ggla_ref[...],
                       wg_ref[...], bg_ref[...], gn_ref[...], ps_ref[...], pg_ref[...])
        h2_ref[...] = h1_ref[...] + _nn(m["mergedb"], wo_ref[...])

    (h2,), landed = _call(
        body, name="mix_post_fwd", grid=(t // tm,),
        in_specs=[_row_tile(tm, 512)] * 3 + [_row_tile(tm, D_MODEL)] * 3
        + [VMEM_FULL, _acc_row(512), _acc_row(512), VMEM_FULL, VMEM_FULL, VMEM_FULL],
        out_specs=[_row_tile(tm, D_MODEL)],
        out_shape=[jax.ShapeDtypeStruct((t, D_MODEL), F32)],
        args=(y, o, r, gs5, ggla, h1, wg, bg, gn, ps5t, pglat, wout), carry=carry)
    return h2, landed


def _mix_post_bwd(y, o, r, gs5, ggla, dh2, wg, bg, gn, ps5t, pglat, wout, carry=None):
    t = o.shape[0]
    tm = _tile(t) // 2

    def body(y_ref, o_ref, r_ref, gs5_ref, ggla_ref, dh2_ref, wg_ref, bg_ref, gn_ref, ps_ref, pg_ref, wo_ref,
             dy_ref, do_ref, dr_ref, dgs5_ref, dggla_ref, dbg_ref, dgn_ref,
             z5b_ref, dgp_ref, ys5b_ref, dm5b_ref, yglab_ref, dmgb_ref, mergedb_ref, dh2b_ref):
        i = pl.program_id(0)
        yv, ov, rv = y_ref[...], o_ref[...], r_ref[...]
        wg, gn, ps5t, pglat = wg_ref[...], gn_ref[...], ps_ref[...], pg_ref[...]
        m = _post_math(yv, ov, rv, gs5_ref[...], ggla_ref[...], wg, bg_ref[...], gn, ps5t, pglat)
        dh2b = dh2_ref[...].astype(BF16)
        dmerged = _nt(dh2b, wo_ref[...])
        s5g, glag = m["s5g"], m["glag"]
        dgs5_ref[...] = (dmerged * m["m5"] * s5g * (1.0 - s5g)).astype(BF16)
        dggla_ref[...] = (dmerged * m["mg"] * glag * (1.0 - glag)).astype(BF16)
        dm5b = (dmerged * s5g).astype(BF16)
        dmgb = (dmerged * glag).astype(BF16)
        dys5 = _nn(dm5b, ps5t)
        dygla = _nn(dmgb, pglat)
        gate, z5, th = m["gate"], m["z5"], m["th"]
        dgpre = dys5 * z5 * gate * (1.0 - gate)
        dgpb = dgpre.astype(BF16)
        dz5 = dys5 * gate + _nt(dgpb, wg)
        dgelu = 0.5 * (1.0 + th) + 0.5 * yv * (1.0 - th * th) * GELU_C0 * (1.0 + 3.0 * GELU_C1 * m["y2"])
        dy_ref[...] = dz5 * dgelu
        on, sr, silu_r = m["on"], m["sr"], m["silu_r"]
        dr_ref[...] = (dygla * on * gn * sr * (1.0 + rv * (1.0 - sr))).astype(BF16)
        dgn = jnp.sum(dygla * on * silu_r, axis=0, keepdims=True)
        don = dygla * gn * silu_r
        for h in range(GLA_HEADS):
            cols = slice(h * GLA_DV, (h + 1) * GLA_DV)
            donh, onh = don[:, cols], on[:, cols]
            do_ref[:, cols] = (m["rs"][h] * (donh - onh * jnp.mean(donh * onh, axis=-1, keepdims=True))).astype(BF16)

        @pl.when(i == 0)
        def _():
            dbg_ref[...] = jnp.zeros_like(dbg_ref)
            dgn_ref[...] = jnp.zeros_like(dgn_ref)

        dbg_ref[...] += jnp.sum(dgpre, axis=0, keepdims=True)
        dgn_ref[...] += dgn
        z5b_ref[...] = m["z5b"]
        dgp_ref[...] = dgpb
        ys5b_ref[...] = m["ys5b"]
        dm5b_ref[...] = dm5b
        yglab_ref[...] = m["yglab"]
        dmgb_ref[...] = dmgb
        mergedb_ref[...] = m["mergedb"]
        dh2b_ref[...] = dh2b

    def f32(d):
        return jax.ShapeDtypeStruct((t, d), F32)

    def b16(d):
        return jax.ShapeDtypeStruct((t, d), BF16)

    widths = (512, 512, 512, 1024, 512, 1024, 1024, 1024)
    return _call(
        body, name="mix_post_bwd", grid=(t // tm,),
        in_specs=[_row_tile(tm, 512)] * 3 + [_row_tile(tm, D_MODEL)] * 3
        + [VMEM_FULL, _acc_row(512), _acc_row(512), VMEM_FULL, VMEM_FULL, VMEM_FULL],
        out_specs=[_row_tile(tm, 512)] * 3 + [_row_tile(tm, D_MODEL)] * 2
        + [_acc_row(512)] * 2 + [_row_tile(tm, w) for w in widths],
        out_shape=[f32(512), b16(512), b16(512), b16(D_MODEL), b16(D_MODEL)]
        + [jax.ShapeDtypeStruct((1, 512), F32)] * 2
        + [b16(w) for w in widths],
        args=(y, o, r, gs5, ggla, dh2, wg, bg, gn, ps5t, pglat, wout), carry=carry)


ADAM_TILE_ELEMS = 256 * 1024


def _adamw(w, g, m, v, name):
    rows, cols = w.shape
    tr = rows
    while tr * cols > ADAM_TILE_ELEMS and tr % 16 == 0:
        tr //= 2

    spec = pl.BlockSpec((tr, cols), lambda i: (i, 0))
    sh = jax.ShapeDtypeStruct((rows, cols), F32)
    return pl.pallas_call(functools.partial(_adamw_body), name=name, grid=(rows // tr,), in_specs=[spec] * 4,
                          out_specs=[spec] * 3, out_shape=[sh] * 3, compiler_params=_cparams(1))(w, g, m, v)


def _adamw_math(w, g, m, v):
    nm = ADAM_B1 * m + (1.0 - ADAM_B1) * g
    nv = ADAM_B2 * v + (1.0 - ADAM_B2) * (g * g)
    m_hat = nm / (1.0 - ADAM_B1 ** ADAM_STEP)
    v_hat = nv / (1.0 - ADAM_B2 ** ADAM_STEP)
    return -ADAM_LR * (m_hat / (jnp.sqrt(v_hat) + ADAM_EPS) + ADAM_WD * w), nm, nv


def _adamw_body(w_ref, g_ref, m_ref, v_ref, d_ref, nm_ref, nv_ref):
    d_ref[...], nm_ref[...], nv_ref[...] = _adamw_math(w_ref[...], g_ref[...], m_ref[...], v_ref[...])


SUM_ADAM_ROWS = 32


def _sum_adamw(landed, ws, ms, vs, name, carry=None):
    k = len(ws)
    n = landed[0].shape[0]
    r, c = ws[0].shape
    tr = SUM_ADAM_ROWS

    def body(*refs):
        lands, (w_refs, m_refs, v_refs), outs = refs[:k], (refs[k:2 * k], refs[2 * k:3 * k], refs[3 * k:4 * k]), refs[4 * k:]
        for i in range(k):
            g = lands[i][0].astype(F32)
            for s in range(1, n):
                g = g + lands[i][s].astype(F32)
            outs[i][...] = g
            outs[k + i][...], outs[2 * k + i][...], outs[3 * k + i][...] = _adamw_math(
                w_refs[i][...], g, m_refs[i][...], v_refs[i][...])

    row = pl.BlockSpec((tr, c), lambda i: (i, 0))
    return _call(
        body, name=name, grid=(r // tr,),
        in_specs=[pl.BlockSpec((n, tr, c), lambda i: (0, i, 0))] * k + [row] * (3 * k),
        out_specs=[row] * (4 * k), out_shape=[jax.ShapeDtypeStruct((r, c), F32)] * (4 * k),
        args=(*landed, *ws, *ms, *vs), carry=carry)


def _adamw_many(ws, gs, ms, vs, name):
    n = len(ws)

    def body(*refs):
        ins, outs = refs[:4 * n], refs[4 * n:]
        for i in range(n):
            _adamw_body(*(ins[j * n + i] for j in range(4)), *(outs[j * n + i] for j in range(3)))

    shapes = [jax.ShapeDtypeStruct(w.shape, F32) for w in ws]
    res = pl.pallas_call(body, name=name, in_specs=[VMEM_FULL] * (4 * n), out_specs=[VMEM_FULL] * (3 * n),
                         out_shape=shapes * 3)(*ws, *gs, *ms, *vs)
    return res[:n], res[n:2 * n], res[2 * n:]


def _exchange(carry, name):
    return _call(None, name=name, grid=(1,), in_specs=[], out_specs=[], out_shape=[], args=(), carry=carry)[1]


def _pair_add(slabs, from_pair, name):
    _, r, cols = slabs.shape

    def body(s_ref, p_ref, o_ref):
        c = lax.axis_index("c")
        mine = jnp.where(c == 0, s_ref[0, 0].astype(F32), s_ref[0, 1].astype(F32))
        o_ref[0] = (mine + p_ref[0].astype(F32)).astype(BF16)

    return pl.pallas_call(
        body, name=name, grid=(4,),
        in_specs=[pl.BlockSpec((1, 2, r, cols), lambda i: (i, 0, 0, 0)), pl.BlockSpec((1, r, cols), lambda i: (i, 0, 0))],
        out_specs=pl.BlockSpec((1, r, cols), lambda i: (i, 0, 0)),
        out_shape=jax.ShapeDtypeStruct((4, r, cols), BF16),
        compiler_params=_cparams(1),
    )(slabs.reshape(4, 2, r, cols), from_pair)


def _sum_slabs(slabs, name):
    n = slabs.shape[0]

    def body(s_ref, o_ref):
        acc = s_ref[0].astype(F32)
        for s in range(1, n):
            acc = acc + s_ref[s].astype(F32)
        o_ref[...] = acc

    return pl.pallas_call(
        body, name=name, in_specs=[VMEM_FULL], out_specs=VMEM_FULL,
        out_shape=jax.ShapeDtypeStruct(slabs.shape[1:], F32),
        compiler_params=pltpu.CompilerParams(vmem_limit_bytes=VMEM_LIMIT_BYTES),
    )(slabs)


BIG = ("ffn1_w1", "ffn1_w3", "ffn1_w2", "w_in", "s5_glu_w", "gla_a_up_w", "proj_s5", "proj_gla", "w_out",
       "ffn2_w1", "ffn2_w3", "ffn2_w2")
GROUPS = (("ffn1_w1", "ffn1_w3", "ffn1_w2"),
          ("w_in", "s5_glu_w", "gla_a_up_w", "proj_s5", "proj_gla", "w_out"),
          ("ffn2_w1", "ffn2_w3", "ffn2_w2"))
W_IN_ROWS = 514
W_IN_PAD = 528
UP_COLS = 32
ROW_ADAM = ("ffn1_w1", "ffn1_w3", "w_in", "ffn2_w1", "ffn2_w3")
COL_SHARDED = ("ffn1_w1", "ffn1_w3", "w_in", "proj_s5", "proj_gla", "ffn2_w1", "ffn2_w3")

SMALL = ("ffn1_norm", "mix_norm", "s5_lambda_re", "s5_lambda_im", "s5_log_dt", "s5_b_re", "s5_b_im", "s5_c_re",
         "s5_c_im", "s5_d", "s5_glu_b", "gla_a_up_b", "gla_out_norm", "ffn2_norm", "final_norm")
SMALL_SHAPES = dict(ffn1_norm=(1, 1024), mix_norm=(1, 1024), s5_lambda_re=(1, 32, 64), s5_lambda_im=(1, 32, 64),
                    s5_log_dt=(1, 32), s5_b_re=(1, 32, 64, 16), s5_b_im=(1, 32, 64, 16), s5_c_re=(1, 32, 16, 64),
                    s5_c_im=(1, 32, 16, 64), s5_d=(1, 32, 16), s5_glu_b=(1, 512), gla_a_up_b=(1, 256),
                    gla_out_norm=(1, 512), ffn2_norm=(1, 1024), final_norm=(1024,))
SMALL_N = sum(math.prod(s) for s in SMALL_SHAPES.values())
SMALL_R = -(-SMALL_N // (64 * 1024)) * 64


def _shard_rows(name, a):
    if name == "gla_a_up_w":
        return jnp.pad(a, ((0, 0), (0, 128 - UP_COLS)))
    if name in COL_SHARDED:
        a = a.T
    if name == "w_in":
        return jnp.pad(a, ((0, W_IN_PAD - W_IN_ROWS), (0, 0)))
    return a.reshape(-1, 1024)


def _unshard_rows(name, rows, shape):
    if name == "gla_a_up_w":
        return rows[:, :UP_COLS]
    if name == "w_in":
        rows = rows[:W_IN_ROWS]
    if name in COL_SHARDED:
        return rows.reshape(shape[1], shape[0]).T
    return rows.reshape(shape)


def _pack_small(vals, loss):
    flat = jnp.concatenate([vals[n].reshape(-1).astype(F32) for n in SMALL] + [loss.reshape(1)])
    return jnp.pad(flat, (0, SMALL_R * 1024 - SMALL_N - 1)).reshape(SMALL_R, 1024)


S5_B = ("s5_b_re", "s5_b_im")


def _working(name, a):
    return a[0].transpose(0, 2, 1) if name in S5_B else a


def _declared(name, a):
    return a.transpose(0, 2, 1)[None] if name in S5_B else a.reshape(SMALL_SHAPES[name])


def _unpack_small(slab):
    flat = slab.reshape(-1)
    out, off = {}, 0
    for n in SMALL:
        size = math.prod(SMALL_SHAPES[n])
        shape = (S5_GROUPS, S5_GROUP, S5_STATE) if n in S5_B else SMALL_SHAPES[n]
        out[n] = flat[off:off + size].reshape(shape)
        off += size
    return out


FULL_SHAPES = dict(w_in=(IN_COLS, D_MODEL), s5_glu_w=(S5_WIDTH, S5_WIDTH), gla_a_up_w=(GLA_RANK, GLA_KEY),
                   proj_s5=(D_MODEL, S5_WIDTH), proj_gla=(D_MODEL, GLA_VAL), w_out=(D_MODEL, D_MODEL))


def _full_weight(name, gathered):
    if name == "gla_a_up_w":
        return gathered[:, :, :UP_COLS].transpose(1, 0, 2).reshape(GLA_RANK, GLA_KEY)
    if name == "w_in":
        gathered = gathered[:, :W_IN_ROWS]
    return gathered.reshape(FULL_SHAPES.get(name, (D_FF, D_MODEL)))


def _grad_slabs(name, g):
    if name == "gla_a_up_w":
        g = g.reshape(GLA_RANK, N_DEV, UP_COLS).transpose(1, 0, 2)
        return jnp.pad(g, ((0, 0), (0, 0), (0, 128 - UP_COLS))).astype(BF16)
    if name == "w_in":
        return jnp.pad(g.reshape(N_DEV, W_IN_ROWS, D_MODEL), ((0, 0), (0, W_IN_PAD - W_IN_ROWS), (0, 0)))
    return g.reshape(N_DEV, -1, 1024)


def _s5_dense(re, im, sign_im):
    eye = jnp.eye(8, dtype=F32)

    def one(a):
        a = a.reshape(S5_BLOCKS, 8, S5_GROUP, S5_STATE)
        return jnp.einsum("cghp,gk->cghkp", a, eye).reshape(S5_BLOCKS, 128, S5_BSTATE)

    return jnp.concatenate([one(re), sign_im * one(im)], axis=-1)


def _s5_undense(d):
    eye = jnp.eye(8, dtype=F32)

    def one(a):
        a = a.reshape(S5_BLOCKS, 8, S5_GROUP, 8, S5_STATE)
        return jnp.einsum("cghkp,gk->cghp", a, eye).reshape(S5_GROUPS, S5_GROUP, S5_STATE)

    return one(d[..., :S5_BSTATE]), one(d[..., S5_BSTATE:])


def _local_step(x, target, p, w, rows=None, opt=None):
    w = dict(w or {})
    landed_grads = {}

    def gather(names):
        return None if rows is None else ([rows[n] for n in names], "gather")

    def gathered(names, landed):
        w.update({n: _full_weight(n, g) for n, g in zip(names, landed)})

    def scatter(names):
        return None if rows is None else ([_grad_slabs(n, big[n]) for n in names], "scatter")

    def scattered(names, landed):
        landed_grads.update(zip(names, landed))

    if rows is not None:
        gathered(GROUPS[0], _exchange(gather(GROUPS[0]), "gather_ffn1"))
    g1, gm, g2 = p["ffn1_norm"], p["mix_norm"], p["ffn2_norm"]
    gf = p["final_norm"].reshape(1, D_MODEL)
    lre, lim = p["s5_lambda_re"][0], p["s5_lambda_im"][0]
    ldt = p["s5_log_dt"][0].reshape(S5_GROUPS, 1)
    bre = p["s5_b_re"][0].transpose(0, 2, 1)
    bim = p["s5_b_im"][0].transpose(0, 2, 1)
    cre, cim = p["s5_c_re"][0], p["s5_c_im"][0]
    dskip = p["s5_d"][0].reshape(1, S5_WIDTH)
    bg, bup, gn = p["s5_glu_b"], p["gla_a_up_b"], p["gla_out_norm"]

    mix_first, mix_rest = ("w_in", "gla_a_up_w"), ("s5_glu_w", "proj_s5", "proj_gla", "w_out")
    h1, (a3_1, b3_1, n1), got = _ffn_fwd(x, g1, w["ffn1_w1"], w["ffn1_w3"], w["ffn1_w2"], "ffn1_fwd",
                                         gather(mix_first))
    gathered(mix_first, got)
    wup = w["gla_a_up_w"].astype(F32)
    (u, s5in, q, k, v, r, alow, gs5, ggla), got = _mix_pre_fwd(h1, gm, w["w_in"], gather(mix_rest))
    gathered(mix_rest, got)
    ar, ai, bbr, bbi = _s5_disc(lre, lim, ldt, bre, bim)
    bd = _s5_dense(bbr, bbi, 1.0)
    cd = _s5_dense(cre, cim, -1.0)
    bd16, cd16 = bd.astype(BF16), cd.astype(BF16)
    bdt16, ctd16 = bd16.transpose(0, 2, 1), cd16.transpose(0, 2, 1)
    ar4 = ar.reshape(S5_BLOCKS, 1, S5_BSTATE)
    ai4 = ai.reshape(S5_BLOCKS, 1, S5_BSTATE)
    (xs, y), got = _s5_fwd(s5in, bd16, ctd16, ar4, ai4, dskip, gather(GROUPS[2][:1]))
    gathered(GROUPS[2][:1], got)
    (o, ssave), got = _gla_fwd(q, k, v, alow, wup, bup, gather(GROUPS[2][1:2]))
    gathered(GROUPS[2][1:2], got)
    post_w = (w["s5_glu_w"], bg, gn, w["proj_s5"], w["proj_gla"], w["w_out"])
    h2, got = _mix_post_fwd(y, o, r, gs5, ggla, h1, *post_w, carry=gather(GROUPS[2][2:]))
    gathered(GROUPS[2][2:], got)
    loss, dh3, dgf, (a3_2, b3_2, n2) = _ffn_fwd_head(h2, g2, w["ffn2_w1"], w["ffn2_w3"], w["ffn2_w2"], gf, target,
                                                     "ffn2_fwd")

    big, small = {}, {}
    small["final_norm"] = dgf.reshape(D_MODEL)
    (dh2, dg2, da3, db3, s3, dhh2), _ = _ffn_bwd(
        h2, dh3, g2, a3_2, b3_2, w["ffn2_w1"], w["ffn2_w3"], w["ffn2_w2"], "ffn2_bwd")
    small["ffn2_norm"] = dg2
    big["ffn2_w1"] = _mm_tn(da3, n2, "ffn2_dw1")
    big["ffn2_w3"] = _mm_tn(db3, n2, "ffn2_dw3")
    big["ffn2_w2"] = _mm_tn(s3, dhh2, "ffn2_dw2")
    (dy, do, dr, dgs5, dggla, dbg, dgn, z5b, dgpb, ys5b, dm5b, yglab, dmgb, mergedb, dh2b), got = _mix_post_bwd(
        y, o, r, gs5, ggla, dh2, *post_w, carry=scatter(GROUPS[2][:1]))
    scattered(GROUPS[2][:1], got)
    small["s5_glu_b"] = dbg
    small["gla_out_norm"] = dgn
    big["s5_glu_w"] = _mm_tn(z5b, dgpb, "glu_dw")
    big["proj_s5"] = _mm_tn(dm5b, ys5b, "proj_s5_dw")
    big["proj_gla"] = _mm_tn(dmgb, yglab, "proj_gla_dw")
    big["w_out"] = _mm_tn(mergedb, dh2b, "w_out_dw")
    (dq, dk, dv, dalow, dwup, dbup), got = _gla_bwd(q, k, v, alow, wup, bup, ssave, do, scatter(GROUPS[2][1:2]))
    scattered(GROUPS[2][1:2], got)
    big["gla_a_up_w"] = dwup
    small["gla_a_up_b"] = dbup
    (ds5in, dbd, dcd, dd, dar4, dai4), got = _s5_bwd(
        dy, s5in, xs, cd16, bdt16, ar4, ai4, dskip, scatter(GROUPS[2][2:]))
    scattered(GROUPS[2][2:], got)
    dbbr, dbbi = _s5_undense(dbd)
    dcre, dcim_neg = _s5_undense(dcd)
    glre, glim, gldt, gbre, gbim = _s5_disc_bwd(
        lre, lim, ldt, bre, bim, dar4.reshape(S5_GROUPS, S5_STATE), dai4.reshape(S5_GROUPS, S5_STATE),
        dbbr, dbbi)
    small["s5_lambda_re"] = glre[None]
    small["s5_lambda_im"] = glim[None]
    small["s5_log_dt"] = gldt.reshape(1, S5_GROUPS)
    small["s5_b_re"] = gbre
    small["s5_b_im"] = gbim
    small["s5_c_re"] = dcre[None]
    small["s5_c_im"] = -dcim_neg[None]
    small["s5_d"] = dd.reshape(1, S5_GROUPS, S5_GROUP)
    dz = (ds5in, dq, dk, dv, dr, dalow, dgs5, dggla)
    (dh1, dgm), got = _mix_pre_bwd(h1, gm, w["w_in"], dh2, dz, scatter(mix_rest))
    scattered(mix_rest, got)
    small["mix_norm"] = dgm
    wide = _mm_tn_many(dz[:5] + dz[6:], u, "w_in_dw")
    low_at = IN_OFFS[5]
    big["w_in"] = jnp.concatenate([wide[:low_at], _mm_tn(dalow, u, "w_in_dw_low"), wide[low_at:]], axis=0)
    (dx, dg1, da3, db3, s3, dhh1), got = _ffn_bwd(
        x, dh1, g1, a3_1, b3_1, w["ffn1_w1"], w["ffn1_w3"], w["ffn1_w2"], "ffn1_bwd",
        scatter(mix_first))
    scattered(mix_first, got)
    small["ffn1_norm"] = dg1
    if rows is None:
        big["ffn1_w1"] = _mm_tn(da3, n1, "ffn1_dw1")
        big["ffn1_w3"] = _mm_tn(db3, n1, "ffn1_dw3")
        big["ffn1_w2"] = _mm_tn(s3, dhh1, "ffn1_dw2")
        return loss[0, 0], dx, big, small
    part = _pack_small(small, loss).reshape(N_DEV, SMALL_R // N_DEV, 1024)
    big["ffn1_w1"], (small_landed,) = _mm_tn(da3, n1, "ffn1_dw1", ([part], "scatter"))
    small_mine = _sum_slabs(small_landed, "sum_small")
    slabs1 = _grad_slabs("ffn1_w1", big["ffn1_w1"])
    big["ffn1_w3"], (from_pair, small_all) = _mm_tn(db3, n1, "ffn1_dw3",
                                                    [([slabs1], "pair"), ([small_mine], "gather")])
    small = small_all.reshape(SMALL_R, 1024)
    sums1 = _pair_add(slabs1, from_pair, "ffn1_w1_pair")
    slabs3 = _grad_slabs("ffn1_w3", big["ffn1_w3"])
    big["ffn1_w2"], (landed1, from_pair) = _mm_tn(s3, dhh1, "ffn1_dw2", [([sums1], "chips"), ([slabs3], "pair")])
    sums3 = _pair_add(slabs3, from_pair, "ffn1_w3_pair")
    slabs2 = _grad_slabs("ffn1_w2", big["ffn1_w2"])

    def sum_adamw(names, lands, name, carry=None):
        outs, got = _sum_adamw(lands, *([opt[n][j] for n in names] for j in range(3)), name, carry)
        for i, n in enumerate(names):
            updated[n] = outs[i::len(names)]
        return got

    updated = {}
    landed3, from_pair = sum_adamw(GROUPS[2], [landed_grads.pop(n) for n in GROUPS[2]], "adamw_ffn2",
                                   [([sums3], "chips"), ([slabs2], "pair")])
    sums2 = _pair_add(slabs2, from_pair, "ffn1_w2_pair")
    (landed2,) = _exchange(([sums2], "chips"), "scatter_ffn1_b")
    sum_adamw(GROUPS[0], [landed1, landed3, landed2], "adamw_ffn1")
    return loss[0, 0], dx, landed_grads, small, updated


NAMES = ("ffn1_norm", "ffn1_w1", "ffn1_w3", "ffn1_w2", "mix_norm", "w_in", "s5_lambda_re", "s5_lambda_im",
         "s5_log_dt", "s5_b_re", "s5_b_im", "s5_c_re", "s5_c_im", "s5_d", "s5_glu_w", "s5_glu_b", "gla_a_up_w",
         "gla_a_up_b", "gla_out_norm", "proj_s5", "proj_gla", "w_out", "ffn2_norm", "ffn2_w1", "ffn2_w3", "ffn2_w2",
         "final_norm")


def kernel(*args):
    nw = len(NAMES)
    x = args[0][0]
    wts = dict(zip(NAMES, args[1:1 + nw]))
    target = args[1 + nw][0]
    mom = dict(zip(NAMES, args[2 + nw:2 + 2 * nw]))
    var = dict(zip(NAMES, args[2 + 2 * nw:2 + 3 * nw]))

    shards = {n: wts[n][0] for n in BIG}
    rows = {n: _shard_rows(n, shards[n]).astype(BF16) for n in BIG}
    def row_layout(n, a):
        return a.T if n in ROW_ADAM else a

    opt = {n: tuple(row_layout(n, d[n][0]) for d in (wts, mom, var)) for n in GROUPS[0] + GROUPS[2]}
    _, dx, landed, small_slab, updated = _local_step(x, target, {n: wts[n] for n in SMALL}, None, rows, opt)
    loss = small_slab.reshape(-1)[SMALL_N]
    g_small = _unpack_small(small_slab)

    grad, delta, new_m, new_v = {}, {}, {}, {}
    for n, arrays in updated.items():
        grad[n], delta[n], new_m[n], new_v[n] = (row_layout(n, a)[None] for a in arrays)
    for n in GROUPS[1]:
        g_rows = _sum_slabs(landed[n], "sum_" + n)
        if n in ROW_ADAM:
            g = g_rows[:W_IN_ROWS] if n == "w_in" else g_rows
            outs = _adamw(shards[n].T, g, mom[n][0].T, var[n][0].T, "adamw_" + n)
            grad[n], delta[n], new_m[n], new_v[n] = (a.T[None] for a in (g, *outs))
        else:
            g = _unshard_rows(n, g_rows, shards[n].shape)
            outs = _adamw(shards[n], g, mom[n][0], var[n][0], "adamw_" + n)
            grad[n], delta[n], new_m[n], new_v[n] = (a[None] for a in (g, *outs))

    def flat2d(a):
        return a.reshape(-1, a.shape[-1])

    operands = ([flat2d(_working(n, d[n])) for n in SMALL] for d in (wts, mom, var))
    w2d, m2d, v2d = operands
    outs = _adamw_many(w2d, [flat2d(g_small[n]) for n in SMALL], m2d, v2d, "adamw_small")
    for out, arrays in zip((grad, delta, new_m, new_v), ([g_small[n] for n in SMALL], *outs)):
        out.update({n: _declared(n, a.reshape(g_small[n].shape)) for n, a in zip(SMALL, arrays)})
    return (loss, dx[None], *(d[n] for d in (grad, delta, new_m, new_v) for n in NAMES))
```

```python
import functools
import math

import jax
import jax.numpy as jnp
from jax import lax
from jax.experimental import pallas as pl
from jax.experimental.pallas import tpu as pltpu

F32, BF16 = jnp.float32, jnp.bfloat16
HIGHEST = lax.Precision.HIGHEST

D_MODEL = 1024
D_FF = 2816
N_DEV = 8
S5_WIDTH, S5_GROUPS, S5_GROUP, S5_STATE = 512, 32, 16, 64
S5_BLOCKS = 4
S5_BSTATE = 512
S5_SEGS = 8
GLA_HEADS, GLA_DK, GLA_DV = 4, 64, 128
GLA_KEY, GLA_VAL, GLA_RANK, GLA_CHUNK = 256, 512, 16, 64
GLA_TAU = 16.0
GLA_STEP_CHUNKS = 4
EPS = 1e-6
IN_SIZES = (512, 256, 256, 512, 512, 16, 1024, 1024)
IN_OFFS = tuple(sum(IN_SIZES[:i]) for i in range(len(IN_SIZES)))
IN_COLS = sum(IN_SIZES)
ADAM_LR, ADAM_B1, ADAM_B2, ADAM_EPS, ADAM_WD, ADAM_STEP = 0.001, 0.9, 0.999, 1e-08, 0.01, 10
GELU_C0 = math.sqrt(2.0 / math.pi)
GELU_C1 = 0.044715

FFN_FT = 256
VMEM_LIMIT_BYTES = 56 * 1024 * 1024

VMEM_FULL = pl.BlockSpec(memory_space=pltpu.VMEM)
ANY = pl.BlockSpec(memory_space=pl.ANY)


def _cparams(n_grid):
    return pltpu.CompilerParams(dimension_semantics=("arbitrary",) * n_grid, vmem_limit_bytes=VMEM_LIMIT_BYTES)


def _tile(t):
    return 512 if t >= 1024 else t // 2


def _nn(a, b):
    return jnp.dot(a, b, preferred_element_type=F32)


def _nt(a, b):
    return lax.dot_general(a, b, (((1,), (1,)), ((), ())), preferred_element_type=F32)


def _tn(a, b):
    return lax.dot_general(a, b, (((0,), (0,)), ((), ())), preferred_element_type=F32)


def _rms_parts(x):
    r = lax.rsqrt(jnp.mean(x * x, axis=-1, keepdims=True) + EPS)
    return x * r, r


def _rms_bwd(dn, g, xhat, r):
    dxh = dn * g
    dx = r * (dxh - xhat * jnp.mean(dxh * xhat, axis=-1, keepdims=True))
    return dx, jnp.sum(dn * xhat, axis=0, keepdims=True)


def _peers():
    x, y, c = lax.axis_index("x"), lax.axis_index("y"), lax.axis_index("c")
    out = []
    for k in range(1, N_DEV):
        px = 1 - x if k & 4 else x
        py = 1 - y if k & 2 else y
        pc = 1 - c if k & 1 else c
        out.append(((px, py, pc), 4 * px + 2 * py + pc))
    return 4 * x + 2 * y + c, out


def _exchange_copies(src_refs, out_refs, send_sems, recv_sems, local_sems, scatter, with_recvs):
    me, peers = _peers()
    locals_, sends, recvs = [], [], []
    for a, (src_ref, out_ref) in enumerate(zip(src_refs, out_refs)):
        def mine(idx, src_ref=src_ref):
            return src_ref.at[idx] if scatter else src_ref

        locals_.append(pltpu.make_async_copy(mine(me), out_ref.at[me], local_sems.at[a]))
        for k, (dev, idx) in enumerate(peers):
            sends.append(pltpu.make_async_remote_copy(
                src_ref=mine(idx), dst_ref=out_ref.at[me], send_sem=send_sems.at[a, k], recv_sem=recv_sems.at[a, k],
                device_id=dev, device_id_type=pl.DeviceIdType.MESH))
            if with_recvs:
                recvs.append(pltpu.make_async_remote_copy(
                    src_ref=mine(idx), dst_ref=out_ref.at[idx], send_sem=send_sems.at[a, k],
                    recv_sem=recv_sems.at[a, k], device_id=dev, device_id_type=pl.DeviceIdType.MESH))
    return locals_, sends, recvs


def _remote(src, dst, send_sems, recv_sems, a, k, dev):
    return pltpu.make_async_remote_copy(src_ref=src, dst_ref=dst, send_sem=send_sems.at[a, k],
                                        recv_sem=recv_sems.at[a, k], device_id=dev,
                                        device_id_type=pl.DeviceIdType.MESH)


def _gather_places():
    x, y, c = lax.axis_index("x"), lax.axis_index("y"), lax.axis_index("c")
    chips = [(1 - x, y), (x, 1 - y), (1 - x, 1 - y)]
    sibling = (x, y, 1 - c)
    me_idx, sib_idx = 4 * x + 2 * y + c, 4 * x + 2 * y + 1 - c
    same_core = [((cx, cy, c), 4 * cx + 2 * cy + c) for cx, cy in chips]
    other_core_idx = [4 * cx + 2 * cy + 1 - c for cx, cy in chips]
    return sibling, me_idx, sib_idx, same_core, other_core_idx


def _gather_start(src_refs, out_refs, send_sems, recv_sems, local_sems):
    sibling, me_idx, _, same_core, _ = _gather_places()
    for a, (src, out) in enumerate(zip(src_refs, out_refs)):
        pltpu.make_async_copy(src, out.at[me_idx], local_sems.at[a]).start()
        _remote(src, out.at[me_idx], send_sems, recv_sems, a, 0, sibling).start()
        for j, (dev, _) in enumerate(same_core):
            _remote(src, out.at[me_idx], send_sems, recv_sems, a, 1 + j, dev).start()


def _gather_forward(src_refs, out_refs, send_sems, recv_sems, local_sems):
    sibling, _, _, same_core, _ = _gather_places()
    for a, (src, out) in enumerate(zip(src_refs, out_refs)):
        for j, (dev, idx) in enumerate(same_core):
            _remote(src, out.at[idx], send_sems, recv_sems, a, 1 + j, dev).wait_recv()
            _remote(out.at[idx], out.at[idx], send_sems, recv_sems, a, 4 + j, sibling).start()


def _gather_finish(src_refs, out_refs, send_sems, recv_sems, local_sems):
    sibling, me_idx, sib_idx, same_core, other_core_idx = _gather_places()
    for a, (src, out) in enumerate(zip(src_refs, out_refs)):
        _remote(src, out.at[sib_idx], send_sems, recv_sems, a, 0, sibling).wait_recv()
        for j, idx in enumerate(other_core_idx):
            _remote(src, out.at[idx], send_sems, recv_sems, a, 4 + j, sibling).wait_recv()
        _remote(src, out.at[me_idx], send_sems, recv_sems, a, 0, sibling).wait_send()
        for j, (dev, idx) in enumerate(same_core):
            _remote(src, out.at[me_idx], send_sems, recv_sems, a, 1 + j, dev).wait_send()
            _remote(out.at[idx], out.at[idx], send_sems, recv_sems, a, 4 + j, sibling).wait_send()
        pltpu.make_async_copy(src, out.at[me_idx], local_sems.at[a]).wait()


def _exchange_start(*refs, scatter):
    locals_, sends, _ = _exchange_copies(*refs, scatter=scatter, with_recvs=False)
    for cp in locals_ + sends:
        cp.start()


def _exchange_wait(*refs, scatter):
    locals_, sends, recvs = _exchange_copies(*refs, scatter=scatter, with_recvs=True)
    for cp in recvs:
        cp.wait_recv()
    for cp in sends:
        cp.wait_send()
    for cp in locals_:
        cp.wait()


def _halves_places():
    x, y, c = lax.axis_index("x"), lax.axis_index("y"), lax.axis_index("c")
    flips = [(1 - x, y), (x, 1 - y), (1 - x, 1 - y)]
    return (x, y, 1 - c), c, 2 * x + y, [((fx, fy, c), 2 * fx + fy) for fx, fy in flips]


def _pair_start(src_refs, out_refs, send_sems, recv_sems, local_sems):
    sibling, c, _, _ = _halves_places()
    for a, (src, out) in enumerate(zip(src_refs, out_refs)):
        for i in range(4):
            _remote(src.at[2 * i + 1 - c], out.at[i], send_sems, recv_sems, a, i, sibling).start()


def _pair_finish(src_refs, out_refs, send_sems, recv_sems, local_sems):
    sibling, c, _, _ = _halves_places()
    for a, (src, out) in enumerate(zip(src_refs, out_refs)):
        for i in range(4):
            _remote(src.at[2 * i + 1 - c], out.at[i], send_sems, recv_sems, a, i, sibling).wait()


def _chips_start(src_refs, out_refs, send_sems, recv_sems, local_sems):
    _, _, chip, others = _halves_places()
    for a, (src, out) in enumerate(zip(src_refs, out_refs)):
        pltpu.make_async_copy(src.at[chip], out.at[chip], local_sems.at[a]).start()
        for k, (dev, their_chip) in enumerate(others):
            _remote(src.at[their_chip], out.at[chip], send_sems, recv_sems, a, k, dev).start()


def _chips_finish(src_refs, out_refs, send_sems, recv_sems, local_sems):
    _, _, chip, others = _halves_places()
    for a, (src, out) in enumerate(zip(src_refs, out_refs)):
        for k, (dev, their_chip) in enumerate(others):
            _remote(src.at[their_chip], out.at[their_chip], send_sems, recv_sems, a, k, dev).wait_recv()
        for k, (dev, their_chip) in enumerate(others):
            _remote(src.at[their_chip], out.at[chip], send_sems, recv_sems, a, k, dev).wait_send()
        pltpu.make_async_copy(src.at[chip], out.at[chip], local_sems.at[a]).wait()


EXCHANGES = {
    "gather": (_gather_start, _gather_forward, _gather_finish, N_DEV, False),
    "scatter": (functools.partial(_exchange_start, scatter=True), None, functools.partial(_exchange_wait, scatter=True),
                N_DEV, True),
    "pair": (_pair_start, None, _pair_finish, 4, True),
    "chips": (_chips_start, None, _chips_finish, 4, True),
}


def _exchange_sems(n_arrays):
    return [pltpu.SemaphoreType.DMA((n_arrays, N_DEV - 1)), pltpu.SemaphoreType.DMA((n_arrays, N_DEV - 1)),
            pltpu.SemaphoreType.DMA((n_arrays,))]


def _exchange_shapes(srcs, kind):
    lead, slabbed = EXCHANGES[kind][3:]
    return [jax.ShapeDtypeStruct((lead,) + tuple(s.shape[1:] if slabbed else s.shape), s.dtype) for s in srcs]


def _carries(carry):
    if carry is None:
        return []
    return [carry] if isinstance(carry, tuple) else list(carry)


def _call(body, *, name, grid, in_specs, out_specs, out_shape, args, scratch_shapes=(), carry=None):
    n_in, n_out, n_scr = len(in_specs), len(out_specs), len(scratch_shapes)
    groups = _carries(carry)
    sizes = [len(arrays) for arrays, _ in groups]
    nc = sum(sizes)

    def wrapped(*refs):
        ins, refs = refs[:n_in], refs[n_in:]
        csrc, refs = refs[:nc], refs[nc:]
        outs, refs = refs[:n_out], refs[n_out:]
        cland, refs = refs[:nc], refs[nc:]
        scr, sems = refs[:n_scr], refs[n_scr:]

        def run(phase):
            at = 0
            for gi, ((_, kind), size) in enumerate(zip(groups, sizes)):
                if EXCHANGES[kind][phase] is not None:
                    EXCHANGES[kind][phase](csrc[at:at + size], cland[at:at + size], *sems[3 * gi:3 * gi + 3])
                at += size

        last = pl.program_id(0) == grid[0] - 1
        if nc:
            pl.when(pl.program_id(0) == 0)(functools.partial(run, 0))
            pl.when(last)(functools.partial(run, 1))
        if body is not None:
            body(*ins, *outs, *scr)
        if nc:
            pl.when(last)(functools.partial(run, 2))

    res = pl.pallas_call(
        wrapped, name=name, grid=grid,
        in_specs=list(in_specs) + [ANY] * nc, out_specs=list(out_specs) + [ANY] * nc,
        out_shape=list(out_shape) + [s for arrays, kind in groups for s in _exchange_shapes(arrays, kind)],
        scratch_shapes=list(scratch_shapes) + [s for size in sizes for s in _exchange_sems(size)],
        compiler_params=_cparams(1),
    )(*args, *[a for arrays, _ in groups for a in arrays])
    return res[:n_out], res[n_out:]


def _row_tile(tm, d):
    return pl.BlockSpec((tm, d), lambda i: (i, 0))


def _acc_row(d):
    return pl.BlockSpec((1, d), lambda i: (0, 0))


def _ffn_body(x_ref, g_ref, w1_ref, w3_ref, w2_ref, acc_ref, a_ref, b_ref, n_ref):
    xv = x_ref[...]
    xhat, _ = _rms_parts(xv)
    n = (xhat * g_ref[...]).astype(BF16)
    n_ref[...] = n
    acc_ref[...] = xv

    def fstep(f, c):
        rows = pl.ds(pl.multiple_of(f * FFN_FT, FFN_FT), FFN_FT)
        a = _nt(n, w1_ref[rows, :])
        b = _nt(n, w3_ref[rows, :])
        a_ref[f] = a.astype(BF16)
        b_ref[f] = b.astype(BF16)
        s = (a * jax.nn.sigmoid(a) * b).astype(BF16)
        acc_ref[...] += 0.5 * _nn(s, w2_ref[rows, :])
        return c

    lax.fori_loop(0, D_FF // FFN_FT, fstep, 0, unroll=True)


def _ffn_fwd(x, g, w1t, w3t, w2, name, carry=None):
    t = x.shape[0]
    tm = _tile(t)
    nf = D_FF // FFN_FT
    blk3 = pl.BlockSpec((nf, tm, FFN_FT), lambda i: (0, i, 0))
    sh3 = jax.ShapeDtypeStruct((nf, t, FFN_FT), BF16)
    (h, a3, b3, n), landed = _call(
        functools.partial(_ffn_body), name=name, grid=(t // tm,),
        in_specs=[_row_tile(tm, D_MODEL), _acc_row(D_MODEL), VMEM_FULL, VMEM_FULL, VMEM_FULL],
        out_specs=[_row_tile(tm, D_MODEL), blk3, blk3, _row_tile(tm, D_MODEL)],
        out_shape=[jax.ShapeDtypeStruct((t, D_MODEL), F32), sh3, sh3, jax.ShapeDtypeStruct((t, D_MODEL), BF16)],
        args=(x, g, w1t, w3t, w2), carry=carry)
    return h, (a3, b3, n), landed


def _ffn_fwd_head(x, g, w1t, w3t, w2, gf, target, name):
    t = x.shape[0]
    tm = _tile(t)
    nf = D_FF // FFN_FT

    def body(x_ref, g_ref, w1_ref, w3_ref, w2_ref, gf_ref, t_ref, loss_ref, dh_ref, dgf_ref, a_ref, b_ref, n_ref, acc):
        _ffn_body(x_ref, g_ref, w1_ref, w3_ref, w2_ref, acc, a_ref, b_ref, n_ref)
        _head_math(acc[...], gf_ref[...], t_ref[...], loss_ref, dh_ref, dgf_ref)

    blk3 = pl.BlockSpec((nf, tm, FFN_FT), lambda i: (0, i, 0))
    sh3 = jax.ShapeDtypeStruct((nf, t, FFN_FT), BF16)
    (loss, dh, dgf, a3, b3, n), _ = _call(
        body, name=name, grid=(t // tm,),
        in_specs=[_row_tile(tm, D_MODEL), _acc_row(D_MODEL), VMEM_FULL, VMEM_FULL, VMEM_FULL, _acc_row(D_MODEL),
                  _row_tile(tm, D_MODEL)],
        out_specs=[pl.BlockSpec((1, 1), lambda i: (0, 0)), _row_tile(tm, D_MODEL), _acc_row(D_MODEL), blk3, blk3,
                   _row_tile(tm, D_MODEL)],
        out_shape=[jax.ShapeDtypeStruct((1, 1), F32), jax.ShapeDtypeStruct((t, D_MODEL), F32),
                   jax.ShapeDtypeStruct((1, D_MODEL), F32), sh3, sh3, jax.ShapeDtypeStruct((t, D_MODEL), BF16)],
        scratch_shapes=[pltpu.VMEM((tm, D_MODEL), F32)],
        args=(x, g, w1t, w3t, w2, gf, target))
    return loss, dh, dgf, (a3, b3, n)


def _head_math(h, gv, target, loss_ref, dh_ref, dg_ref):
    i = pl.program_id(0)
    xhat, r = _rms_parts(h)
    err = xhat * gv - target
    dx, dg = _rms_bwd(err * (1.0 / D_MODEL), gv, xhat, r)
    dh_ref[...] = dx

    @pl.when(i == 0)
    def _():
        loss_ref[...] = jnp.zeros_like(loss_ref)
        dg_ref[...] = jnp.zeros_like(dg_ref)

    loss_ref[...] += (0.5 / D_MODEL) * jnp.sum(jnp.sum(err * err, axis=1, keepdims=True), axis=0, keepdims=True)
    dg_ref[...] += dg


def _ffn_bwd(x, dh, g, a3, b3, w1t, w3t, w2, name, carry=None):
    t = x.shape[0]
    tm = _tile(t) // 2
    nf = D_FF // FFN_FT

    def body(x_ref, dh_ref, g_ref, a_ref, b_ref, w1_ref, w3_ref, w2_ref,
             dx_ref, dg_ref, da_ref, db_ref, s_ref, dhh_ref, dn_acc):
        i = pl.program_id(0)
        xv = x_ref[...]
        gv = g_ref[...]
        xhat, r = _rms_parts(xv)
        dhv = dh_ref[...]
        dhh = (0.5 * dhv).astype(BF16)
        dhh_ref[...] = dhh
        dn_acc[...] = jnp.zeros_like(dn_acc)

        def fstep(f, c):
            rows = pl.ds(f * FFN_FT, FFN_FT)
            w1c, w3c, w2c = w1_ref[rows, :], w3_ref[rows, :], w2_ref[rows, :]
            a = a_ref[f].astype(F32)
            b = b_ref[f].astype(F32)
            sg = jax.nn.sigmoid(a)
            sl = a * sg
            ds = _nt(dhh, w2c)
            da = (ds * b * sg * (1.0 + a * (1.0 - sg))).astype(BF16)
            db = (ds * sl).astype(BF16)
            s_ref[f] = (sl * b).astype(BF16)
            da_ref[f] = da
            db_ref[f] = db
            return c

        def nstep(f, c):
            rows = pl.ds(f * FFN_FT, FFN_FT)
            dn_acc[...] += _nn(da_ref[f], w1_ref[rows, :]) + _nn(db_ref[f], w3_ref[rows, :])
            return c

        for f in range(nf + 1):
            if f < nf:
                fstep(f, 0)
            if f:
                nstep(f - 1, 0)
        dx, dg = _rms_bwd(dn_acc[...], gv, xhat, r)
        dx_ref[...] = dhv + dx

        @pl.when(i == 0)
        def _():
            dg_ref[...] = jnp.zeros_like(dg_ref)

        dg_ref[...] += dg

    blk3 = pl.BlockSpec((nf, tm, FFN_FT), lambda i: (0, i, 0))
    sh3 = jax.ShapeDtypeStruct((nf, t, FFN_FT), BF16)
    return _call(
        body, name=name, grid=(t // tm,),
        in_specs=[_row_tile(tm, D_MODEL), _row_tile(tm, D_MODEL), _acc_row(D_MODEL), blk3, blk3,
                  VMEM_FULL, VMEM_FULL, VMEM_FULL],
        out_specs=[_row_tile(tm, D_MODEL), _acc_row(D_MODEL), blk3, blk3, blk3, _row_tile(tm, D_MODEL)],
        out_shape=[jax.ShapeDtypeStruct((t, D_MODEL), F32), jax.ShapeDtypeStruct((1, D_MODEL), F32), sh3, sh3, sh3,
                   jax.ShapeDtypeStruct((t, D_MODEL), BF16)],
        scratch_shapes=[pltpu.VMEM((tm, D_MODEL), F32)],
        args=(x, dh, g, a3, b3, w1t, w3t, w2), carry=carry)


def _mm_tn(a, b, name, carry=None):
    t, n = b.shape
    kc = min(512, t)
    if a.ndim == 3:
        nb, _, tb = a.shape
        a_spec = pl.BlockSpec((1, t, tb), lambda i: (i, 0, 0))
    else:
        m = a.shape[1]
        tb = min(m, 256)
        nb = m // tb
        a_spec = pl.BlockSpec((t, tb), lambda i: (0, i))
    three_d = a.ndim == 3

    def body(a_ref, b_ref, o_ref, acc):
        acc[...] = jnp.zeros_like(acc)

        def kstep(k, c):
            rows = pl.ds(pl.multiple_of(k * kc, kc), kc)
            av = a_ref[0, rows, :] if three_d else a_ref[rows, :]
            acc[...] += _tn(av.astype(BF16), b_ref[rows, :])
            return c

        lax.fori_loop(0, t // kc, kstep, 0, unroll=True)
        o_ref[...] = acc[...].astype(BF16)

    (out,), landed = _call(
        body, name=name, grid=(nb,),
        in_specs=[a_spec, VMEM_FULL],
        out_specs=[pl.BlockSpec((tb, n), lambda i: (i, 0))],
        out_shape=[jax.ShapeDtypeStruct((nb * tb, n), BF16)],
        scratch_shapes=[pltpu.VMEM((tb, n), F32)],
        args=(a, b), carry=carry)
    return (out, landed) if carry is not None else out


MM_TB = 256


def _mm_tn_many(arrays, b, name):
    t, n = b.shape
    kc = min(512, t)
    counts = [a.shape[1] // MM_TB for a in arrays]
    starts = [sum(counts[:k]) for k in range(len(arrays))]

    def spec(start, count):
        return pl.BlockSpec((t, MM_TB), lambda i: (0, jnp.clip(i - start, 0, count - 1)))

    def body(*refs):
        a_refs, (b_ref, o_ref, acc) = refs[:len(arrays)], refs[len(arrays):]
        i = pl.program_id(0)
        for a_ref, start, count in zip(a_refs, starts, counts):
            @pl.when((i >= start) & (i < start + count))
            def _(a_ref=a_ref):
                acc[...] = jnp.zeros_like(acc)

                def kstep(k, c):
                    rows = pl.ds(pl.multiple_of(k * kc, kc), kc)
                    acc[...] += _tn(a_ref[rows, :].astype(BF16), b_ref[rows, :])
                    return c

                lax.fori_loop(0, t // kc, kstep, 0, unroll=True)
                o_ref[...] = acc[...].astype(BF16)

    return pl.pallas_call(
        body, name=name, grid=(sum(counts),),
        in_specs=[spec(s, c) for s, c in zip(starts, counts)] + [VMEM_FULL],
        out_specs=pl.BlockSpec((MM_TB, n), lambda i: (i, 0)),
        out_shape=jax.ShapeDtypeStruct((sum(counts) * MM_TB, n), BF16),
        scratch_shapes=[pltpu.VMEM((MM_TB, n), F32)],
        compiler_params=_cparams(1),
    )(*arrays, b)


def _mix_pre_fwd(h, g, wint, carry=None):
    t = h.shape[0]
    tm = _tile(t)

    def body(h_ref, g_ref, w_ref, u_ref, *outs):
        xhat, _ = _rms_parts(h_ref[...])
        u = (xhat * g_ref[...]).astype(BF16)
        u_ref[...] = u
        for o_ref, off, size in zip(outs, IN_OFFS, IN_SIZES):
            o_ref[...] = _nt(u, w_ref[off:off + size, :])

    return _call(
        body, name="mix_pre_fwd", grid=(t // tm,),
        in_specs=[_row_tile(tm, D_MODEL), _acc_row(D_MODEL), VMEM_FULL],
        out_specs=[_row_tile(tm, D_MODEL)] + [_row_tile(tm, s) for s in IN_SIZES],
        out_shape=[jax.ShapeDtypeStruct((t, D_MODEL), BF16)] + [jax.ShapeDtypeStruct((t, s), F32) for s in IN_SIZES],
        args=(h, g, wint), carry=carry)


def _mix_pre_bwd(h, g, wint, dh2, dz, carry=None):
    t = h.shape[0]
    tm = _tile(t)

    def body(h_ref, g_ref, w_ref, dh2_ref, *rest):
        dz_refs, (dh1_ref, dg_ref) = rest[:len(IN_SIZES)], rest[len(IN_SIZES):]
        i = pl.program_id(0)
        gv = g_ref[...]
        xhat, r = _rms_parts(h_ref[...])
        du = jnp.zeros((tm, D_MODEL), F32)
        for dz_ref, off, size in zip(dz_refs, IN_OFFS, IN_SIZES):
            du = du + _nn(dz_ref[...].astype(BF16), w_ref[off:off + size, :])
        dx, dg = _rms_bwd(du, gv, xhat, r)
        dh1_ref[...] = dh2_ref[...] + dx

        @pl.when(i == 0)
        def _():
            dg_ref[...] = jnp.zeros_like(dg_ref)

        dg_ref[...] += dg

    return _call(
        body, name="mix_pre_bwd", grid=(t // tm,),
        in_specs=[_row_tile(tm, D_MODEL), _acc_row(D_MODEL), VMEM_FULL, _row_tile(tm, D_MODEL)]
        + [_row_tile(tm, s) for s in IN_SIZES],
        out_specs=[_row_tile(tm, D_MODEL), _acc_row(D_MODEL)],
        out_shape=[jax.ShapeDtypeStruct((t, D_MODEL), F32), jax.ShapeDtypeStruct((1, D_MODEL), F32)],
        args=(h, g, wint, dh2, *dz), carry=carry)


def _disc_math(lre, lim, ldt, bre, bim):
    dt = jnp.exp(ldt)
    mag = jnp.exp(lre * dt)
    ar = mag * jnp.cos(lim * dt)
    ai = mag * jnp.sin(lim * dt)
    den = lre * lre + lim * lim
    nr = ar - 1.0
    fr = (nr * lre + ai * lim) / den
    fi = (ai * lre - nr * lim) / den
    fr, fi = fr[:, None, :], fi[:, None, :]
    return ar, ai, fr * bre - fi * bim, fr * bim + fi * bre


def _s5_disc(lre, lim, ldt, bre, bim):
    def body(lre_ref, lim_ref, ldt_ref, bre_ref, bim_ref, ar_ref, ai_ref, bbr_ref, bbi_ref):
        ar, ai, bbr, bbi = _disc_math(lre_ref[...], lim_ref[...], ldt_ref[...], bre_ref[...], bim_ref[...])
        ar_ref[...] = ar
        ai_ref[...] = ai
        bbr_ref[...] = bbr
        bbi_ref[...] = bbi

    small = jax.ShapeDtypeStruct(lre.shape, F32)
    big = jax.ShapeDtypeStruct(bre.shape, F32)
    return pl.pallas_call(body, name="s5_disc", out_shape=[small, small, big, big],
                          in_specs=[VMEM_FULL] * 5, out_specs=[VMEM_FULL] * 4)(lre, lim, ldt, bre, bim)


def _s5_disc_bwd(lre, lim, ldt, bre, bim, dar, dai, dbbr, dbbi):
    def body(lre_ref, lim_ref, ldt_ref, bre_ref, bim_ref, dar_ref, dai_ref, dbbr_ref, dbbi_ref,
             glre_ref, glim_ref, gldt_ref, gbre_ref, gbim_ref):
        _, vjp = jax.vjp(_disc_math, lre_ref[...], lim_ref[...], ldt_ref[...], bre_ref[...], bim_ref[...])
        glre, glim, gldt, gbre, gbim = vjp((dar_ref[...], dai_ref[...], dbbr_ref[...], dbbi_ref[...]))
        glre_ref[...] = glre
        glim_ref[...] = glim
        gldt_ref[...] = gldt
        gbre_ref[...] = gbre
        gbim_ref[...] = gbim

    small = jax.ShapeDtypeStruct(lre.shape, F32)
    big = jax.ShapeDtypeStruct(bre.shape, F32)
    return pl.pallas_call(body, name="s5_disc_bwd",
                          out_shape=[small, small, jax.ShapeDtypeStruct(ldt.shape, F32), big, big],
                          in_specs=[VMEM_FULL] * 9, out_specs=[VMEM_FULL] * 5,
                          )(lre, lim, ldt, bre, bim, dar, dai, dbbr, dbbi)


def _cmul(ar, ai, br, bi):
    return ar * br - ai * bi, ar * bi + ai * br


def _cpow(ar, ai, n):
    rr, ri = None, None
    pr, pi = ar, ai
    while n:
        if n & 1:
            rr, ri = (pr, pi) if rr is None else _cmul(rr, ri, pr, pi)
        n >>= 1
        if n:
            pr, pi = _cmul(pr, pi, pr, pi)
    return rr, ri


def _shift_rows(v, down):
    row = lax.broadcasted_iota(jnp.int32, v.shape, 0)
    if down:
        return jnp.where(row == 0, 0.0, pltpu.roll(v, 1, 0))
    return jnp.where(row == S5_SEGS - 1, 0.0, pltpu.roll(v, S5_SEGS - 1, 0))


def _chain_segments(er, ei, pr, pi, down):
    fr, fi = er, ei
    for _ in range(S5_SEGS - 1):
        sr, si = _shift_rows(fr, down), _shift_rows(fi, down)
        mr, mi = _cmul(pr, pi, sr, si)
        fr, fi = er + mr, ei + mi
    return _shift_rows(fr, down), _shift_rows(fi, down)


def _rows_to_scan_order(src_ref, dst_ref, t):
    ls = t // S5_SEGS

    def tile(j, c):
        dst_ref[pl.ds(pl.multiple_of(j * S5_SEGS, S5_SEGS), S5_SEGS), :] = src_ref[pl.ds(j, S5_SEGS, stride=ls), :]
        return c

    lax.fori_loop(0, ls, tile, 0, unroll=8)


def _rows_from_scan_order(src_ref, dst_ref, t):
    ls = t // S5_SEGS
    for s in range(S5_SEGS):
        def tile(jb, c, s=s):
            dst_ref[pl.ds(pl.multiple_of(s * ls + jb * 8, 8), 8), :] = (
                src_ref[pl.ds(jb * 8 * S5_SEGS + s, 8, stride=S5_SEGS), :])
            return c

        lax.fori_loop(0, ls // 8, tile, 0, unroll=8)


def _s5_fwd(ug, bd, ctd, ar4, ai4, dskip, carry=None):
    t = ug.shape[0]
    ls = t // S5_SEGS
    rc = min(512, t)
    ns = S5_BSTATE

    def body(ugn_ref, bd_ref, ct_ref, ar_ref, ai_ref, d_ref, xs_hbm, yn_ref, buf, ug_ref, y_ref, sem):
        cb = pl.program_id(0)
        bdv = bd_ref[0]
        _rows_to_scan_order(ugn_ref, ug_ref, t)

        def mm(i, c):
            rows = pl.ds(pl.multiple_of(i * rc, rc), rc)
            buf[rows, :] = _nn(ug_ref[rows, :].astype(BF16), bdv)
            return c

        lax.fori_loop(0, t // rc, mm, 0, unroll=True)
        arb = jnp.broadcast_to(ar_ref[0], (S5_SEGS, ns))
        aib = jnp.broadcast_to(ai_ref[0], (S5_SEGS, ns))

        def step(j, c, store):
            sr, si = c
            rows = pl.ds(pl.multiple_of(j * S5_SEGS, S5_SEGS), S5_SEGS)
            nr = arb * sr - aib * si + buf[rows, 0:ns]
            ni = arb * si + aib * sr + buf[rows, ns:2 * ns]
            if store:
                buf[rows, 0:ns] = nr
                buf[rows, ns:2 * ns] = ni
            return nr, ni

        zero = jnp.zeros((S5_SEGS, ns), F32)
        er, ei = lax.fori_loop(0, ls, functools.partial(step, store=False), (zero, zero))
        pr, pi = _cpow(arb, aib, ls)
        init = _chain_segments(er, ei, pr, pi, down=True)
        lax.fori_loop(0, ls, functools.partial(step, store=True), init)

        out = pltpu.make_async_copy(buf, xs_hbm.at[cb], sem)
        out.start()
        ctv = ct_ref[0]
        dv = d_ref[...]

        def ymm(i, c):
            rows = pl.ds(pl.multiple_of(i * rc, rc), rc)
            y_ref[rows, :] = _nn(buf[rows, :].astype(BF16), ctv) + dv * ug_ref[rows, :]
            return c

        lax.fori_loop(0, t // rc, ymm, 0, unroll=True)
        _rows_from_scan_order(y_ref, yn_ref, t)
        out.wait()

    return _call(
        body, name="s5_fwd", grid=(S5_BLOCKS,),
        in_specs=[pl.BlockSpec((t, 128), lambda i: (0, i)),
                  pl.BlockSpec((1, 128, 2 * ns), lambda i: (i, 0, 0)),
                  pl.BlockSpec((1, 2 * ns, 128), lambda i: (i, 0, 0)),
                  pl.BlockSpec((1, 1, ns), lambda i: (i, 0, 0)),
                  pl.BlockSpec((1, 1, ns), lambda i: (i, 0, 0)),
                  pl.BlockSpec((1, 128), lambda i: (0, i))],
        out_specs=[ANY, pl.BlockSpec((t, 128), lambda i: (0, i))],
        out_shape=[jax.ShapeDtypeStruct((S5_BLOCKS, t, 2 * ns), F32), jax.ShapeDtypeStruct((t, S5_WIDTH), F32)],
        scratch_shapes=[pltpu.VMEM((t, 2 * ns), F32), pltpu.VMEM((t, 128), F32), pltpu.VMEM((t, 128), F32),
                        pltpu.SemaphoreType.DMA(())],
        args=(ug, bd, ctd, ar4, ai4, dskip), carry=carry)


def _s5_bwd(dy, ug, xs, cd, bdt, ar4, ai4, dskip, carry=None):
    t = ug.shape[0]
    ls = t // S5_SEGS
    rc = min(512, t)
    ns = S5_BSTATE

    def body(dyn_ref, ugn_ref, xs_hbm, cd_ref, bdt_ref, ar_ref, ai_ref, d_ref,
             dugn_ref, dbd_ref, dcd_ref, dd_ref, dar_ref, dai_ref, xbuf, lam, dy_ref, ug_ref, dug_ref, sem):
        cb = pl.program_id(0)
        load = pltpu.make_async_copy(xs_hbm.at[cb], xbuf, sem)
        load.start()
        cdv = cd_ref[0]
        _rows_to_scan_order(dyn_ref, dy_ref, t)
        _rows_to_scan_order(ugn_ref, ug_ref, t)

        def mm(i, c):
            rows = pl.ds(pl.multiple_of(i * rc, rc), rc)
            lam[rows, :] = _nn(dy_ref[rows, :].astype(BF16), cdv)
            return c

        lax.fori_loop(0, t // rc, mm, 0, unroll=True)
        arb = jnp.broadcast_to(ar_ref[0], (S5_SEGS, ns))
        aib = jnp.broadcast_to(ai_ref[0], (S5_SEGS, ns))

        def lam_step(j, lr, li):
            rows = pl.ds(pl.multiple_of(j * S5_SEGS, S5_SEGS), S5_SEGS)
            nr = arb * lr + aib * li + lam[rows, 0:ns]
            ni = arb * li - aib * lr + lam[rows, ns:2 * ns]
            return rows, nr, ni

        def pass1(jj, c):
            _, nr, ni = lam_step(ls - 1 - jj, *c)
            return nr, ni

        zero = jnp.zeros((S5_SEGS, ns), F32)
        er, ei = lax.fori_loop(0, ls, pass1, (zero, zero))
        pr, pi = _cpow(arb, aib, ls)
        init = _chain_segments(er, ei, pr, -pi, down=False)
        load.wait()

        def accumulate(acc, nr, ni, xpr, xpi):
            return acc[0] + nr * xpr + ni * xpi, acc[1] + ni * xpr - nr * xpi

        def pass2(jj, c):
            lr, li, accr, acci = c
            j = ls - 1 - jj
            rows, nr, ni = lam_step(j, lr, li)
            lam[rows, 0:ns] = nr
            lam[rows, ns:2 * ns] = ni
            prev = pl.ds(pl.multiple_of((j - 1) * S5_SEGS, S5_SEGS), S5_SEGS)
            accr, acci = accumulate((accr, acci), nr, ni, xbuf[prev, 0:ns], xbuf[prev, ns:2 * ns])
            return nr, ni, accr, acci

        lr, li, accr, acci = lax.fori_loop(0, ls - 1, pass2, (init[0], init[1], zero, zero))
        rows, nr, ni = lam_step(0, lr, li)
        lam[rows, 0:ns] = nr
        lam[rows, ns:2 * ns] = ni
        last = pl.ds((ls - 1) * S5_SEGS, S5_SEGS)
        accr, acci = accumulate((accr, acci), nr, ni,
                                _shift_rows(xbuf[last, 0:ns], True), _shift_rows(xbuf[last, ns:2 * ns], True))
        dar_ref[0] = jnp.sum(accr, axis=0, keepdims=True)
        dai_ref[0] = jnp.sum(acci, axis=0, keepdims=True)

        bdtv = bdt_ref[0]
        dv = d_ref[...]
        dbd_ref[...] = jnp.zeros_like(dbd_ref)
        dcd_ref[...] = jnp.zeros_like(dcd_ref)
        dd_ref[...] = jnp.zeros_like(dd_ref)

        def tail(i, c):
            rows = pl.ds(pl.multiple_of(i * rc, rc), rc)
            dy = dy_ref[rows, :]
            ug = ug_ref[rows, :]
            lb = lam[rows, :].astype(BF16)
            dug_ref[rows, :] = _nn(lb, bdtv) + dv * dy
            dbd_ref[0] += _tn(ug.astype(BF16), lb)
            dcd_ref[0] += _tn(dy.astype(BF16), xbuf[rows, :].astype(BF16))
            dd_ref[...] += jnp.sum(dy * ug, axis=0, keepdims=True)
            return c

        lax.fori_loop(0, t // rc, tail, 0, unroll=True)
        _rows_from_scan_order(dug_ref, dugn_ref, t)

    chan = pl.BlockSpec((t, 128), lambda i: (0, i))
    dense = pl.BlockSpec((1, 128, 2 * ns), lambda i: (i, 0, 0))
    vec = pl.BlockSpec((1, 1, ns), lambda i: (i, 0, 0))
    return _call(
        body, name="s5_bwd", grid=(S5_BLOCKS,),
        in_specs=[chan, chan, ANY, dense, pl.BlockSpec((1, 2 * ns, 128), lambda i: (i, 0, 0)), vec, vec,
                  pl.BlockSpec((1, 128), lambda i: (0, i))],
        out_specs=[chan, dense, dense, pl.BlockSpec((1, 128), lambda i: (0, i)), vec, vec],
        out_shape=[jax.ShapeDtypeStruct((t, S5_WIDTH), F32),
                   jax.ShapeDtypeStruct((S5_BLOCKS, 128, 2 * ns), F32),
                   jax.ShapeDtypeStruct((S5_BLOCKS, 128, 2 * ns), F32),
                   jax.ShapeDtypeStruct((1, S5_WIDTH), F32),
                   jax.ShapeDtypeStruct((S5_BLOCKS, 1, ns), F32),
                   jax.ShapeDtypeStruct((S5_BLOCKS, 1, ns), F32)],
        scratch_shapes=[pltpu.VMEM((t, 2 * ns), F32), pltpu.VMEM((t, 2 * ns), F32)]
        + [pltpu.VMEM((t, 128), F32)] * 3 + [pltpu.SemaphoreType.DMA(())],
        args=(dy, ug, xs, cd, bdt, ar4, ai4, dskip), carry=carry)


def _cumsum_rows(x, reverse):
    c = x.shape[0]
    row = lax.broadcasted_iota(jnp.int32, x.shape, 0)
    d = 1
    while d < c:
        if reverse:
            x = x + jnp.where(row < c - d, pltpu.roll(x, c - d, 0), 0.0)
        else:
            x = x + jnp.where(row >= d, pltpu.roll(x, d, 0), 0.0)
        d *= 2
    return x


def _gla_common(q, k, alow, wup, bup):
    c = GLA_CHUNK
    pre = _nn(alow.astype(BF16), wup.astype(BF16)) + bup
    la = (jnp.minimum(pre, 0.0) - jnp.log(1.0 + jnp.exp(-jnp.abs(pre)))) * (1.0 / GLA_TAU)
    rr = lax.broadcasted_iota(jnp.int32, (c, c), 0)
    cc = lax.broadcasted_iota(jnp.int32, (c, c), 1)
    tril = (rr >= cc).astype(F32)
    bc = _cumsum_rows(la, reverse=False)
    bl = bc[c - 1:c, :]
    e_pos = jnp.exp(bc)
    e_neg = jnp.exp(-bc)
    e_end = jnp.exp(bl - bc)
    qt = q * (GLA_DK ** -0.5) * e_pos
    kt = k * e_neg
    ke = k * e_end
    lane = lax.broadcasted_iota(jnp.int32, (1, GLA_KEY), 1)
    masks = [((lane >= h * GLA_DK) & (lane < (h + 1) * GLA_DK)).astype(F32) for h in range(GLA_HEADS)]
    return dict(pre=pre, tril=tril, bc=bc, bl=bl, e_pos=e_pos, e_neg=e_neg, e_end=e_end,
                qt=qt, kt=kt, ke=ke, dec=jnp.exp(bl), masks=masks)


def _gla_fwd(q, k, v, alow, wup, bup, carry=None):
    t = q.shape[0]
    c = GLA_CHUNK
    n = t // c
    step = GLA_STEP_CHUNKS * c

    def body(q_ref, k_ref, v_ref, al_ref, wup_ref, bup_ref, o_ref, ss_ref, s_ref):
        i = pl.program_id(0)

        @pl.when(i == 0)
        def _():
            s_ref[...] = jnp.zeros_like(s_ref)

        wup_v, bup_v = wup_ref[...], bup_ref[...]
        s = s_ref[...]
        for j in range(GLA_STEP_CHUNKS):
            tok = slice(j * c, (j + 1) * c)
            m = _gla_common(q_ref[tok, :], k_ref[tok, :], al_ref[tok, :], wup_v, bup_v)
            ss_ref[j] = s
            sb = s.astype(BF16)
            ktb = m["kt"].astype(BF16)
            update = jnp.zeros_like(s)
            for h in range(GLA_HEADS):
                mask = m["masks"][h]
                qm = (m["qt"] * mask).astype(BF16)
                vh = v_ref[tok, h * GLA_DV:(h + 1) * GLA_DV].astype(BF16)
                p = (m["tril"] * _nt(qm, ktb)).astype(BF16)
                o_ref[tok, h * GLA_DV:(h + 1) * GLA_DV] = _nn(p, vh) + _nt(qm, sb)
                update = update + _tn(vh, (m["ke"] * mask).astype(BF16))
            s = m["dec"] * s + update
        s_ref[...] = s

    return _call(
        body, name="gla_fwd", grid=(t // step,),
        in_specs=[_row_tile(step, GLA_KEY), _row_tile(step, GLA_KEY), _row_tile(step, GLA_VAL),
                  _row_tile(step, GLA_RANK), VMEM_FULL, VMEM_FULL],
        out_specs=[_row_tile(step, GLA_VAL), pl.BlockSpec((GLA_STEP_CHUNKS, GLA_DV, GLA_KEY), lambda i: (i, 0, 0))],
        out_shape=[jax.ShapeDtypeStruct((t, GLA_VAL), F32), jax.ShapeDtypeStruct((n, GLA_DV, GLA_KEY), F32)],
        scratch_shapes=[pltpu.VMEM((GLA_DV, GLA_KEY), F32)],
        args=(q, k, v, alow, wup, bup), carry=carry)


def _gla_bwd(q, k, v, alow, wup, bup, ssave, do, carry=None):
    t = q.shape[0]
    c = GLA_CHUNK
    n = t // c

    def body(q_ref, k_ref, v_ref, al_ref, wup_ref, bup_ref, ss_ref, do_ref,
             dq_ref, dk_ref, dv_ref, dal_ref, dwup_ref, dbup_ref, ds_ref):
        i = pl.program_id(0)

        @pl.when(i == 0)
        def _():
            ds_ref[...] = jnp.zeros_like(ds_ref)
            dwup_ref[...] = jnp.zeros_like(dwup_ref)
            dbup_ref[...] = jnp.zeros_like(dbup_ref)

        wup_v, bup_v = wup_ref[...], bup_ref[...]
        ds_in = ds_ref[...]
        dwup = jnp.zeros((GLA_RANK, GLA_KEY), F32)
        dbup = jnp.zeros((1, GLA_KEY), F32)
        for j in reversed(range(GLA_STEP_CHUNKS)):
            tok = slice(j * c, (j + 1) * c)
            alow_v = al_ref[tok, :]
            m = _gla_common(q_ref[tok, :], k_ref[tok, :], alow_v, wup_v, bup_v)
            s = ss_ref[j]
            sb = s.astype(BF16)
            dsb = ds_in.astype(BF16)
            qt, kt, ke = m["qt"], m["kt"], m["ke"]
            ktb = kt.astype(BF16)
            dqt = jnp.zeros((c, GLA_KEY), F32)
            dkt = jnp.zeros((c, GLA_KEY), F32)
            dke = jnp.zeros((c, GLA_KEY), F32)
            update = jnp.zeros_like(ds_in)
            for h in range(GLA_HEADS):
                mask = m["masks"][h]
                qm = (qt * mask).astype(BF16)
                km = (kt * mask).astype(BF16)
                kem = (ke * mask).astype(BF16)
                cols = slice(h * GLA_DV, (h + 1) * GLA_DV)
                vh = v_ref[tok, cols].astype(BF16)
                doh = do_ref[tok, cols].astype(BF16)
                p = (m["tril"] * _nt(qm, ktb)).astype(BF16)
                dp = (m["tril"] * _nt(doh, vh)).astype(BF16)
                dv_ref[tok, cols] = (_tn(p, doh) + _nt(kem, dsb)).astype(BF16)
                dqt = dqt + _nn(dp, km) + _nn(doh, sb) * mask
                dkt = dkt + _tn(dp, qm)
                dke = dke + _nn(vh, dsb) * mask
                update = update + _tn(doh, qm)
            ddec = jnp.sum(ds_in * s, axis=0, keepdims=True)
            dq_ref[tok, :] = (dqt * m["e_pos"] * (GLA_DK ** -0.5)).astype(BF16)
            dk_ref[tok, :] = (dkt * m["e_neg"] + dke * m["e_end"]).astype(BF16)
            dkeke = dke * ke
            dbl = jnp.sum(dkeke, axis=0, keepdims=True) + ddec * m["dec"]
            last = (lax.broadcasted_iota(jnp.int32, (c, 1), 0) == c - 1).astype(F32)
            dla = _cumsum_rows(dqt * qt - dkt * kt - dkeke + last * dbl, reverse=True)
            dpre = dla * (1.0 / GLA_TAU) * jax.nn.sigmoid(-m["pre"])
            dpb = dpre.astype(BF16)
            dal_ref[tok, :] = _nt(dpb, wup_v.astype(BF16)).astype(BF16)
            dwup = dwup + _tn(alow_v.astype(BF16), dpb)
            dbup = dbup + jnp.sum(dpre, axis=0, keepdims=True)
            ds_in = m["dec"] * ds_in + update
        ds_ref[...] = ds_in
        dwup_ref[...] += dwup
        dbup_ref[...] += dbup

    step = GLA_STEP_CHUNKS * c
    nsteps = t // step

    def rev(d):
        return pl.BlockSpec((step, d), lambda i: (nsteps - 1 - i, 0))

    return _call(
        body, name="gla_bwd", grid=(nsteps,),
        in_specs=[rev(GLA_KEY), rev(GLA_KEY), rev(GLA_VAL), rev(GLA_RANK), VMEM_FULL, VMEM_FULL,
                  pl.BlockSpec((GLA_STEP_CHUNKS, GLA_DV, GLA_KEY), lambda i: (nsteps - 1 - i, 0, 0)), rev(GLA_VAL)],
        out_specs=[rev(GLA_KEY), rev(GLA_KEY), rev(GLA_VAL), rev(GLA_RANK),
                   pl.BlockSpec((GLA_RANK, GLA_KEY), lambda i: (0, 0)), _acc_row(GLA_KEY)],
        out_shape=[jax.ShapeDtypeStruct((t, GLA_KEY), BF16), jax.ShapeDtypeStruct((t, GLA_KEY), BF16),
                   jax.ShapeDtypeStruct((t, GLA_VAL), BF16), jax.ShapeDtypeStruct((t, GLA_RANK), BF16),
                   jax.ShapeDtypeStruct((GLA_RANK, GLA_KEY), F32), jax.ShapeDtypeStruct((1, GLA_KEY), F32)],
        scratch_shapes=[pltpu.VMEM((GLA_DV, GLA_KEY), F32)],
        args=(q, k, v, alow, wup, bup, ssave, do), carry=carry)


def _post_math(y, o, r, gs5, ggla, wg, bg, gn, ps5t, pglat):
    y2 = y * y
    th = jnp.tanh(GELU_C0 * (y + GELU_C1 * y * y2))
    z5 = 0.5 * y * (1.0 + th)
    z5b = z5.astype(BF16)
    gate = jax.nn.sigmoid(_nn(z5b, wg) + bg)
    ys5 = z5 * gate
    rs, on = [], []
    for h in range(GLA_HEADS):
        oh = o[:, h * GLA_DV:(h + 1) * GLA_DV]
        rh = lax.rsqrt(jnp.mean(oh * oh, axis=-1, keepdims=True) + EPS)
        rs.append(rh)
        on.append(oh * rh)
    on = jnp.concatenate(on, axis=-1)
    sr = jax.nn.sigmoid(r)
    silu_r = r * sr
    ygla = on * gn * silu_r
    ys5b, yglab = ys5.astype(BF16), ygla.astype(BF16)
    m5 = _nt(ys5b, ps5t)
    mg = _nt(yglab, pglat)
    s5g, glag = jax.nn.sigmoid(gs5), jax.nn.sigmoid(ggla)
    merged = s5g * m5 + glag * mg
    return dict(y2=y2, th=th, z5=z5, z5b=z5b, gate=gate, ys5b=ys5b, yglab=yglab, rs=rs, on=on, sr=sr,
                silu_r=silu_r, m5=m5, mg=mg, s5g=s5g, glag=glag, mergedb=merged.astype(BF16))


def _mix_post_fwd(y, o, r, gs5, ggla, h1, wg, bg, gn, ps5t, pglat, wout, carry=None):
    t = o.shape[0]
    tm = _tile(t)

    def body(y_ref, o_ref, r_ref, gs5_ref, ggla_ref, h1_ref, wg_ref, bg_ref, gn_ref, ps_ref, pg_ref, wo_ref, h2_ref):
        m = _post_math(y_ref[...], o_ref[...], r_ref[...], gs5_ref[...], ggla_ref[...],
                       wg_ref[...], bg_ref[...], gn_ref[...], ps_ref[...], pg_ref[...])
        h2_ref[...] = h1_ref[...] + _nn(m["mergedb"], wo_ref[...])

    (h2,), landed = _call(
        body, name="mix_post_fwd", grid=(t // tm,),
        in_specs=[_row_tile(tm, 512)] * 3 + [_row_tile(tm, D_MODEL)] * 3
        + [VMEM_FULL, _acc_row(512), _acc_row(512), VMEM_FULL, VMEM_FULL, VMEM_FULL],
        out_specs=[_row_tile(tm, D_MODEL)],
        out_shape=[jax.ShapeDtypeStruct((t, D_MODEL), F32)],
        args=(y, o, r, gs5, ggla, h1, wg, bg, gn, ps5t, pglat, wout), carry=carry)
    return h2, landed


def _mix_post_bwd(y, o, r, gs5, ggla, dh2, wg, bg, gn, ps5t, pglat, wout, carry=None):
    t = o.shape[0]
    tm = _tile(t) // 2

    def body(y_ref, o_ref, r_ref, gs5_ref, ggla_ref, dh2_ref, wg_ref, bg_ref, gn_ref, ps_ref, pg_ref, wo_ref,
             dy_ref, do_ref, dr_ref, dgs5_ref, dggla_ref, dbg_ref, dgn_ref,
             z5b_ref, dgp_ref, ys5b_ref, dm5b_ref, yglab_ref, dmgb_ref, mergedb_ref, dh2b_ref):
        i = pl.program_id(0)
        yv, ov, rv = y_ref[...], o_ref[...], r_ref[...]
        wg, gn, ps5t, pglat = wg_ref[...], gn_ref[...], ps_ref[...], pg_ref[...]
        m = _post_math(yv, ov, rv, gs5_ref[...], ggla_ref[...], wg, bg_ref[...], gn, ps5t, pglat)
        dh2b = dh2_ref[...].astype(BF16)
        dmerged = _nt(dh2b, wo_ref[...])
        s5g, glag = m["s5g"], m["glag"]
        dgs5_ref[...] = (dmerged * m["m5"] * s5g * (1.0 - s5g)).astype(BF16)
        dggla_ref[...] = (dmerged * m["mg"] * glag * (1.0 - glag)).astype(BF16)
        dm5b = (dmerged * s5g).astype(BF16)
        dmgb = (dmerged * glag).astype(BF16)
        dys5 = _nn(dm5b, ps5t)
        dygla = _nn(dmgb, pglat)
        gate, z5, th = m["gate"], m["z5"], m["th"]
        dgpre = dys5 * z5 * gate * (1.0 - gate)
        dgpb = dgpre.astype(BF16)
        dz5 = dys5 * gate + _nt(dgpb, wg)
        dgelu = 0.5 * (1.0 + th) + 0.5 * yv * (1.0 - th * th) * GELU_C0 * (1.0 + 3.0 * GELU_C1 * m["y2"])
        dy_ref[...] = dz5 * dgelu
        on, sr, silu_r = m["on"], m["sr"], m["silu_r"]
        dr_ref[...] = (dygla * on * gn * sr * (1.0 + rv * (1.0 - sr))).astype(BF16)
        dgn = jnp.sum(dygla * on * silu_r, axis=0, keepdims=True)
        don = dygla * gn * silu_r
        for h in range(GLA_HEADS):
            cols = slice(h * GLA_DV, (h + 1) * GLA_DV)
            donh, onh = don[:, cols], on[:, cols]
            do_ref[:, cols] = (m["rs"][h] * (donh - onh * jnp.mean(donh * onh, axis=-1, keepdims=True))).astype(BF16)

        @pl.when(i == 0)
        def _():
            dbg_ref[...] = jnp.zeros_like(dbg_ref)
            dgn_ref[...] = jnp.zeros_like(dgn_ref)

        dbg_ref[...] += jnp.sum(dgpre, axis=0, keepdims=True)
        dgn_ref[...] += dgn
        z5b_ref[...] = m["z5b"]
        dgp_ref[...] = dgpb
        ys5b_ref[...] = m["ys5b"]
        dm5b_ref[...] = dm5b
        yglab_ref[...] = m["yglab"]
        dmgb_ref[...] = dmgb
        mergedb_ref[...] = m["mergedb"]
        dh2b_ref[...] = dh2b

    def f32(d):
        return jax.ShapeDtypeStruct((t, d), F32)

    def b16(d):
        return jax.ShapeDtypeStruct((t, d), BF16)

    widths = (512, 512, 512, 1024, 512, 1024, 1024, 1024)
    return _call(
        body, name="mix_post_bwd", grid=(t // tm,),
        in_specs=[_row_tile(tm, 512)] * 3 + [_row_tile(tm, D_MODEL)] * 3
        + [VMEM_FULL, _acc_row(512), _acc_row(512), VMEM_FULL, VMEM_FULL, VMEM_FULL],
        out_specs=[_row_tile(tm, 512)] * 3 + [_row_tile(tm, D_MODEL)] * 2
        + [_acc_row(512)] * 2 + [_row_tile(tm, w) for w in widths],
        out_shape=[f32(512), b16(512), b16(512), b16(D_MODEL), b16(D_MODEL)]
        + [jax.ShapeDtypeStruct((1, 512), F32)] * 2
        + [b16(w) for w in widths],
        args=(y, o, r, gs5, ggla, dh2, wg, bg, gn, ps5t, pglat, wout), carry=carry)


ADAM_TILE_ELEMS = 256 * 1024


def _adamw(w, g, m, v, name):
    rows, cols = w.shape
    tr = rows
    while tr * cols > ADAM_TILE_ELEMS and tr % 16 == 0:
        tr //= 2

    spec = pl.BlockSpec((tr, cols), lambda i: (i, 0))
    sh = jax.ShapeDtypeStruct((rows, cols), F32)
    return pl.pallas_call(functools.partial(_adamw_body), name=name, grid=(rows // tr,), in_specs=[spec] * 4,
                          out_specs=[spec] * 3, out_shape=[sh] * 3, compiler_params=_cparams(1))(w, g, m, v)


def _adamw_math(w, g, m, v):
    nm = ADAM_B1 * m + (1.0 - ADAM_B1) * g
    nv = ADAM_B2 * v + (1.0 - ADAM_B2) * (g * g)
    m_hat = nm / (1.0 - ADAM_B1 ** ADAM_STEP)
    v_hat = nv / (1.0 - ADAM_B2 ** ADAM_STEP)
    return -ADAM_LR * (m_hat / (jnp.sqrt(v_hat) + ADAM_EPS) + ADAM_WD * w), nm, nv


def _adamw_body(w_ref, g_ref, m_ref, v_ref, d_ref, nm_ref, nv_ref):
    d_ref[...], nm_ref[...], nv_ref[...] = _adamw_math(w_ref[...], g_ref[...], m_ref[...], v_ref[...])


SUM_ADAM_ROWS = 32


def _sum_adamw(landed, ws, ms, vs, name, carry=None):
    k = len(ws)
    n = landed[0].shape[0]
    r, c = ws[0].shape
    tr = SUM_ADAM_ROWS

    def body(*refs):
        lands, (w_refs, m_refs, v_refs), outs = refs[:k], (refs[k:2 * k], refs[2 * k:3 * k], refs[3 * k:4 * k]), refs[4 * k:]
        for i in range(k):
            g = lands[i][0].astype(F32)
            for s in range(1, n):
                g = g + lands[i][s].astype(F32)
            outs[i][...] = g
            outs[k + i][...], outs[2 * k + i][...], outs[3 * k + i][...] = _adamw_math(
                w_refs[i][...], g, m_refs[i][...], v_refs[i][...])

    row = pl.BlockSpec((tr, c), lambda i: (i, 0))
    return _call(
        body, name=name, grid=(r // tr,),
        in_specs=[pl.BlockSpec((n, tr, c), lambda i: (0, i, 0))] * k + [row] * (3 * k),
        out_specs=[row] * (4 * k), out_shape=[jax.ShapeDtypeStruct((r, c), F32)] * (4 * k),
        args=(*landed, *ws, *ms, *vs), carry=carry)


def _adamw_many(ws, gs, ms, vs, name):
    n = len(ws)

    def body(*refs):
        ins, outs = refs[:4 * n], refs[4 * n:]
        for i in range(n):
            _adamw_body(*(ins[j * n + i] for j in range(4)), *(outs[j * n + i] for j in range(3)))

    shapes = [jax.ShapeDtypeStruct(w.shape, F32) for w in ws]
    res = pl.pallas_call(body, name=name, in_specs=[VMEM_FULL] * (4 * n), out_specs=[VMEM_FULL] * (3 * n),
                         out_shape=shapes * 3)(*ws, *gs, *ms, *vs)
    return res[:n], res[n:2 * n], res[2 * n:]


def _exchange(carry, name):
    return _call(None, name=name, grid=(1,), in_specs=[], out_specs=[], out_shape=[], args=(), carry=carry)[1]


def _pair_add(slabs, from_pair, name):
    _, r, cols = slabs.shape

    def body(s_ref, p_ref, o_ref):
        c = lax.axis_index("c")
        mine = jnp.where(c == 0, s_ref[0, 0].astype(F32), s_ref[0, 1].astype(F32))
        o_ref[0] = (mine + p_ref[0].astype(F32)).astype(BF16)

    return pl.pallas_call(
        body, name=name, grid=(4,),
        in_specs=[pl.BlockSpec((1, 2, r, cols), lambda i: (i, 0, 0, 0)), pl.BlockSpec((1, r, cols), lambda i: (i, 0, 0))],
        out_specs=pl.BlockSpec((1, r, cols), lambda i: (i, 0, 0)),
        out_shape=jax.ShapeDtypeStruct((4, r, cols), BF16),
        compiler_params=_cparams(1),
    )(slabs.reshape(4, 2, r, cols), from_pair)


def _sum_slabs(slabs, name):
    n = slabs.shape[0]

    def body(s_ref, o_ref):
        acc = s_ref[0].astype(F32)
        for s in range(1, n):
            acc = acc + s_ref[s].astype(F32)
        o_ref[...] = acc

    return pl.pallas_call(
        body, name=name, in_specs=[VMEM_FULL], out_specs=VMEM_FULL,
        out_shape=jax.ShapeDtypeStruct(slabs.shape[1:], F32),
        compiler_params=pltpu.CompilerParams(vmem_limit_bytes=VMEM_LIMIT_BYTES),
    )(slabs)


BIG = ("ffn1_w1", "ffn1_w3", "ffn1_w2", "w_in", "s5_glu_w", "gla_a_up_w", "proj_s5", "proj_gla", "w_out",
       "ffn2_w1", "ffn2_w3", "ffn2_w2")
GROUPS = (("ffn1_w1", "ffn1_w3", "ffn1_w2"),
          ("w_in", "s5_glu_w", "gla_a_up_w", "proj_s5", "proj_gla", "w_out"),
          ("ffn2_w1", "ffn2_w3", "ffn2_w2"))
W_IN_ROWS = 514
W_IN_PAD = 528
UP_COLS = 32
ROW_ADAM = ("ffn1_w1", "ffn1_w3", "w_in", "ffn2_w1", "ffn2_w3")
COL_SHARDED = ("ffn1_w1", "ffn1_w3", "w_in", "proj_s5", "proj_gla", "ffn2_w1", "ffn2_w3")

SMALL = ("ffn1_norm", "mix_norm", "s5_lambda_re", "s5_lambda_im", "s5_log_dt", "s5_b_re", "s5_b_im", "s5_c_re",
         "s5_c_im", "s5_d", "s5_glu_b", "gla_a_up_b", "gla_out_norm", "ffn2_norm", "final_norm")
SMALL_SHAPES = dict(ffn1_norm=(1, 1024), mix_norm=(1, 1024), s5_lambda_re=(1, 32, 64), s5_lambda_im=(1, 32, 64),
                    s5_log_dt=(1, 32), s5_b_re=(1, 32, 64, 16), s5_b_im=(1, 32, 64, 16), s5_c_re=(1, 32, 16, 64),
                    s5_c_im=(1, 32, 16, 64), s5_d=(1, 32, 16), s5_glu_b=(1, 512), gla_a_up_b=(1, 256),
                    gla_out_norm=(1, 512), ffn2_norm=(1, 1024), final_norm=(1024,))
SMALL_N = sum(math.prod(s) for s in SMALL_SHAPES.values())
SMALL_R = -(-SMALL_N // (64 * 1024)) * 64


def _shard_rows(name, a):
    if name == "gla_a_up_w":
        return jnp.pad(a, ((0, 0), (0, 128 - UP_COLS)))
    if name in COL_SHARDED:
        a = a.T
    if name == "w_in":
        return jnp.pad(a, ((0, W_IN_PAD - W_IN_ROWS), (0, 0)))
    return a.reshape(-1, 1024)


def _unshard_rows(name, rows, shape):
    if name == "gla_a_up_w":
        return rows[:, :UP_COLS]
    if name == "w_in":
        rows = rows[:W_IN_ROWS]
    if name in COL_SHARDED:
        return rows.reshape(shape[1], shape[0]).T
    return rows.reshape(shape)


def _pack_small(vals, loss):
    flat = jnp.concatenate([vals[n].reshape(-1).astype(F32) for n in SMALL] + [loss.reshape(1)])
    return jnp.pad(flat, (0, SMALL_R * 1024 - SMALL_N - 1)).reshape(SMALL_R, 1024)


S5_B = ("s5_b_re", "s5_b_im")


def _working(name, a):
    return a[0].transpose(0, 2, 1) if name in S5_B else a


def _declared(name, a):
    return a.transpose(0, 2, 1)[None] if name in S5_B else a.reshape(SMALL_SHAPES[name])


def _unpack_small(slab):
    flat = slab.reshape(-1)
    out, off = {}, 0
    for n in SMALL:
        size = math.prod(SMALL_SHAPES[n])
        shape = (S5_GROUPS, S5_GROUP, S5_STATE) if n in S5_B else SMALL_SHAPES[n]
        out[n] = flat[off:off + size].reshape(shape)
        off += size
    return out


FULL_SHAPES = dict(w_in=(IN_COLS, D_MODEL), s5_glu_w=(S5_WIDTH, S5_WIDTH), gla_a_up_w=(GLA_RANK, GLA_KEY),
                   proj_s5=(D_MODEL, S5_WIDTH), proj_gla=(D_MODEL, GLA_VAL), w_out=(D_MODEL, D_MODEL))


def _full_weight(name, gathered):
    if name == "gla_a_up_w":
        return gathered[:, :, :UP_COLS].transpose(1, 0, 2).reshape(GLA_RANK, GLA_KEY)
    if name == "w_in":
        gathered = gathered[:, :W_IN_ROWS]
    return gathered.reshape(FULL_SHAPES.get(name, (D_FF, D_MODEL)))


def _grad_slabs(name, g):
    if name == "gla_a_up_w":
        g = g.reshape(GLA_RANK, N_DEV, UP_COLS).transpose(1, 0, 2)
        return jnp.pad(g, ((0, 0), (0, 0), (0, 128 - UP_COLS))).astype(BF16)
    if name == "w_in":
        return jnp.pad(g.reshape(N_DEV, W_IN_ROWS, D_MODEL), ((0, 0), (0, W_IN_PAD - W_IN_ROWS), (0, 0)))
    return g.reshape(N_DEV, -1, 1024)


def _s5_dense(re, im, sign_im):
    eye = jnp.eye(8, dtype=F32)

    def one(a):
        a = a.reshape(S5_BLOCKS, 8, S5_GROUP, S5_STATE)
        return jnp.einsum("cghp,gk->cghkp", a, eye).reshape(S5_BLOCKS, 128, S5_BSTATE)

    return jnp.concatenate([one(re), sign_im * one(im)], axis=-1)


def _s5_undense(d):
    eye = jnp.eye(8, dtype=F32)

    def one(a):
        a = a.reshape(S5_BLOCKS, 8, S5_GROUP, 8, S5_STATE)
        return jnp.einsum("cghkp,gk->cghp", a, eye).reshape(S5_GROUPS, S5_GROUP, S5_STATE)

    return one(d[..., :S5_BSTATE]), one(d[..., S5_BSTATE:])


def _local_step(x, target, p, w, rows=None, opt=None):
    w = dict(w or {})
    landed_grads = {}

    def gather(names):
        return None if rows is None else ([rows[n] for n in names], "gather")

    def gathered(names, landed):
        w.update({n: _full_weight(n, g) for n, g in zip(names, landed)})

    def scatter(names):
        return None if rows is None else ([_grad_slabs(n, big[n]) for n in names], "scatter")

    def scattered(names, landed):
        landed_grads.update(zip(names, landed))

    if rows is not None:
        gathered(GROUPS[0], _exchange(gather(GROUPS[0]), "gather_ffn1"))
    g1, gm, g2 = p["ffn1_norm"], p["mix_norm"], p["ffn2_norm"]
    gf = p["final_norm"].reshape(1, D_MODEL)
    lre, lim = p["s5_lambda_re"][0], p["s5_lambda_im"][0]
    ldt = p["s5_log_dt"][0].reshape(S5_GROUPS, 1)
    bre = p["s5_b_re"][0].transpose(0, 2, 1)
    bim = p["s5_b_im"][0].transpose(0, 2, 1)
    cre, cim = p["s5_c_re"][0], p["s5_c_im"][0]
    dskip = p["s5_d"][0].reshape(1, S5_WIDTH)
    bg, bup, gn = p["s5_glu_b"], p["gla_a_up_b"], p["gla_out_norm"]

    mix_first, mix_rest = ("w_in", "gla_a_up_w"), ("s5_glu_w", "proj_s5", "proj_gla", "w_out")
    h1, (a3_1, b3_1, n1), got = _ffn_fwd(x, g1, w["ffn1_w1"], w["ffn1_w3"], w["ffn1_w2"], "ffn1_fwd",
                                         gather(mix_first + mix_rest))
    gathered(mix_first + mix_rest, got)
    wup = w["gla_a_up_w"].astype(F32)
    (u, s5in, q, k, v, r, alow, gs5, ggla), _ = _mix_pre_fwd(h1, gm, w["w_in"])
    ar, ai, bbr, bbi = _s5_disc(lre, lim, ldt, bre, bim)
    bd = _s5_dense(bbr, bbi, 1.0)
    cd = _s5_dense(cre, cim, -1.0)
    bd16, cd16 = bd.astype(BF16), cd.astype(BF16)
    bdt16, ctd16 = bd16.transpose(0, 2, 1), cd16.transpose(0, 2, 1)
    ar4 = ar.reshape(S5_BLOCKS, 1, S5_BSTATE)
    ai4 = ai.reshape(S5_BLOCKS, 1, S5_BSTATE)
    (xs, y), got = _s5_fwd(s5in, bd16, ctd16, ar4, ai4, dskip, gather(GROUPS[2][:2]))
    gathered(GROUPS[2][:2], got)
    (o, ssave), _ = _gla_fwd(q, k, v, alow, wup, bup)
    post_w = (w["s5_glu_w"], bg, gn, w["proj_s5"], w["proj_gla"], w["w_out"])
    h2, got = _mix_post_fwd(y, o, r, gs5, ggla, h1, *post_w, carry=gather(GROUPS[2][2:]))
    gathered(GROUPS[2][2:], got)
    loss, dh3, dgf, (a3_2, b3_2, n2) = _ffn_fwd_head(h2, g2, w["ffn2_w1"], w["ffn2_w3"], w["ffn2_w2"], gf, target,
                                                     "ffn2_fwd")

    big, small = {}, {}
    small["final_norm"] = dgf.reshape(D_MODEL)
    (dh2, dg2, da3, db3, s3, dhh2), _ = _ffn_bwd(
        h2, dh3, g2, a3_2, b3_2, w["ffn2_w1"], w["ffn2_w3"], w["ffn2_w2"], "ffn2_bwd")
    small["ffn2_norm"] = dg2
    big["ffn2_w1"] = _mm_tn(da3, n2, "ffn2_dw1")
    big["ffn2_w3"] = _mm_tn(db3, n2, "ffn2_dw3")
    big["ffn2_w2"] = _mm_tn(s3, dhh2, "ffn2_dw2")
    (dy, do, dr, dgs5, dggla, dbg, dgn, z5b, dgpb, ys5b, dm5b, yglab, dmgb, mergedb, dh2b), got = _mix_post_bwd(
        y, o, r, gs5, ggla, dh2, *post_w, carry=scatter(GROUPS[2][:1]))
    scattered(GROUPS[2][:1], got)
    small["s5_glu_b"] = dbg
    small["gla_out_norm"] = dgn
    big["s5_glu_w"] = _mm_tn(z5b, dgpb, "glu_dw")
    big["proj_s5"] = _mm_tn(dm5b, ys5b, "proj_s5_dw")
    big["proj_gla"] = _mm_tn(dmgb, yglab, "proj_gla_dw")
    big["w_out"] = _mm_tn(mergedb, dh2b, "w_out_dw")
    (dq, dk, dv, dalow, dwup, dbup), got = _gla_bwd(q, k, v, alow, wup, bup, ssave, do, scatter(GROUPS[2][1:2]))
    scattered(GROUPS[2][1:2], got)
    big["gla_a_up_w"] = dwup
    small["gla_a_up_b"] = dbup
    (ds5in, dbd, dcd, dd, dar4, dai4), got = _s5_bwd(
        dy, s5in, xs, cd16, bdt16, ar4, ai4, dskip, scatter(GROUPS[2][2:]))
    scattered(GROUPS[2][2:], got)
    dbbr, dbbi = _s5_undense(dbd)
    dcre, dcim_neg = _s5_undense(dcd)
    glre, glim, gldt, gbre, gbim = _s5_disc_bwd(
        lre, lim, ldt, bre, bim, dar4.reshape(S5_GROUPS, S5_STATE), dai4.reshape(S5_GROUPS, S5_STATE),
        dbbr, dbbi)
    small["s5_lambda_re"] = glre[None]
    small["s5_lambda_im"] = glim[None]
    small["s5_log_dt"] = gldt.reshape(1, S5_GROUPS)
    small["s5_b_re"] = gbre
    small["s5_b_im"] = gbim
    small["s5_c_re"] = dcre[None]
    small["s5_c_im"] = -dcim_neg[None]
    small["s5_d"] = dd.reshape(1, S5_GROUPS, S5_GROUP)
    dz = (ds5in, dq, dk, dv, dr, dalow, dgs5, dggla)
    (dh1, dgm), got = _mix_pre_bwd(h1, gm, w["w_in"], dh2, dz, scatter(mix_rest))
    scattered(mix_rest, got)
    small["mix_norm"] = dgm
    wide = _mm_tn_many(dz[:5] + dz[6:], u, "w_in_dw")
    low_at = IN_OFFS[5]
    big["w_in"] = jnp.concatenate([wide[:low_at], _mm_tn(dalow, u, "w_in_dw_low"), wide[low_at:]], axis=0)
    (dx, dg1, da3, db3, s3, dhh1), got = _ffn_bwd(
        x, dh1, g1, a3_1, b3_1, w["ffn1_w1"], w["ffn1_w3"], w["ffn1_w2"], "ffn1_bwd",
        scatter(mix_first))
    scattered(mix_first, got)
    small["ffn1_norm"] = dg1
    if rows is None:
        big["ffn1_w1"] = _mm_tn(da3, n1, "ffn1_dw1")
        big["ffn1_w3"] = _mm_tn(db3, n1, "ffn1_dw3")
        big["ffn1_w2"] = _mm_tn(s3, dhh1, "ffn1_dw2")
        return loss[0, 0], dx, big, small
    part = _pack_small(small, loss).reshape(N_DEV, SMALL_R // N_DEV, 1024)
    big["ffn1_w1"], (small_landed,) = _mm_tn(da3, n1, "ffn1_dw1", ([part], "scatter"))
    small_mine = _sum_slabs(small_landed, "sum_small")
    slabs1 = _grad_slabs("ffn1_w1", big["ffn1_w1"])
    big["ffn1_w3"], (from_pair, small_all) = _mm_tn(db3, n1, "ffn1_dw3",
                                                    [([slabs1], "pair"), ([small_mine], "gather")])
    small = small_all.reshape(SMALL_R, 1024)
    sums1 = _pair_add(slabs1, from_pair, "ffn1_w1_pair")
    slabs3 = _grad_slabs("ffn1_w3", big["ffn1_w3"])
    big["ffn1_w2"], (landed1, from_pair) = _mm_tn(s3, dhh1, "ffn1_dw2", [([sums1], "chips"), ([slabs3], "pair")])
    sums3 = _pair_add(slabs3, from_pair, "ffn1_w3_pair")
    slabs2 = _grad_slabs("ffn1_w2", big["ffn1_w2"])

    def sum_adamw(names, lands, name, carry=None):
        outs, got = _sum_adamw(lands, *([opt[n][j] for n in names] for j in range(3)), name, carry)
        for i, n in enumerate(names):
            updated[n] = outs[i::len(names)]
        return got

    updated = {}
    landed3, from_pair = sum_adamw(GROUPS[2], [landed_grads.pop(n) for n in GROUPS[2]], "adamw_ffn2",
                                   [([sums3], "chips"), ([slabs2], "pair")])
    sums2 = _pair_add(slabs2, from_pair, "ffn1_w2_pair")
    (landed2,) = _exchange(([sums2], "chips"), "scatter_ffn1_b")
    sum_adamw(GROUPS[0], [landed1, landed3, landed2], "adamw_ffn1")
    return loss[0, 0], dx, landed_grads, small, updated


NAMES = ("ffn1_norm", "ffn1_w1", "ffn1_w3", "ffn1_w2", "mix_norm", "w_in", "s5_lambda_re", "s5_lambda_im",
         "s5_log_dt", "s5_b_re", "s5_b_im", "s5_c_re", "s5_c_im", "s5_d", "s5_glu_w", "s5_glu_b", "gla_a_up_w",
         "gla_a_up_b", "gla_out_norm", "proj_s5", "proj_gla", "w_out", "ffn2_norm", "ffn2_w1", "ffn2_w3", "ffn2_w2",
         "final_norm")


def kernel(*args):
    nw = len(NAMES)
    x = args[0][0]
    wts = dict(zip(NAMES, args[1:1 + nw]))
    target = args[1 + nw][0]
    mom = dict(zip(NAMES, args[2 + nw:2 + 2 * nw]))
    var = dict(zip(NAMES, args[2 + 2 * nw:2 + 3 * nw]))

    shards = {n: wts[n][0] for n in BIG}
    rows = {n: _shard_rows(n, shards[n]).astype(BF16) for n in BIG}
    def row_layout(n, a):
        return a.T if n in ROW_ADAM else a

    opt = {n: tuple(row_layout(n, d[n][0]) for d in (wts, mom, var)) for n in GROUPS[0] + GROUPS[2]}
    _, dx, landed, small_slab, updated = _local_step(x, target, {n: wts[n] for n in SMALL}, None, rows, opt)
    loss = small_slab.reshape(-1)[SMALL_N]
    g_small = _unpack_small(small_slab)

    grad, delta, new_m, new_v = {}, {}, {}, {}
    for n, arrays in updated.items():
        grad[n], delta[n], new_m[n], new_v[n] = (row_layout(n, a)[None] for a in arrays)
    for n in GROUPS[1]:
        g_rows = _sum_slabs(landed[n], "sum_" + n)
        if n in ROW_ADAM:
            g = g_rows[:W_IN_ROWS] if n == "w_in" else g_rows
            outs = _adamw(shards[n].T, g, mom[n][0].T, var[n][0].T, "adamw_" + n)
            grad[n], delta[n], new_m[n], new_v[n] = (a.T[None] for a in (g, *outs))
        else:
            g = _unshard_rows(n, g_rows, shards[n].shape)
            outs = _adamw(shards[n], g, mom[n][0], var[n][0], "adamw_" + n)
            grad[n], delta[n], new_m[n], new_v[n] = (a[None] for a in (g, *outs))

    def flat2d(a):
        return a.reshape(-1, a.shape[-1])

    operands = ([flat2d(_working(n, d[n])) for n in SMALL] for d in (wts, mom, var))
    w2d, m2d, v2d = operands
    outs = _adamw_many(w2d, [flat2d(g_small[n]) for n in SMALL], m2d, v2d, "adamw_small")
    for out, arrays in zip((grad, delta, new_m, new_v), ([g_small[n] for n in SMALL], *outs)):
        out.update({n: _declared(n, a.reshape(g_small[n].shape)) for n, a in zip(SMALL, arrays)})
    return (loss, dx[None], *(d[n] for d in (grad, delta, new_m, new_v) for n in NAMES))
```

```python
import functools
import math

import jax
import jax.numpy as jnp
from jax import lax
from jax.experimental import pallas as pl
from jax.experimental.pallas import tpu as pltpu

F32, BF16 = jnp.float32, jnp.bfloat16

D_MODEL = 1024
D_FF = 2816
N_DEV = 8
S5_WIDTH, S5_GROUPS, S5_GROUP, S5_STATE = 512, 32, 16, 64
S5_BLOCKS = 4
S5_BSTATE = 512
S5_SEGS = 8
GLA_HEADS, GLA_DK, GLA_DV = 4, 64, 128
GLA_KEY, GLA_VAL, GLA_RANK, GLA_CHUNK = 256, 512, 16, 64
GLA_TAU = 16.0
GLA_STEP_CHUNKS = 4
EPS = 1e-6
IN_SIZES = (512, 256, 256, 512, 512, 16, 1024, 1024)
IN_OFFS = tuple(sum(IN_SIZES[:i]) for i in range(len(IN_SIZES)))
IN_COLS = sum(IN_SIZES)
ADAM_LR, ADAM_B1, ADAM_B2, ADAM_EPS, ADAM_WD, ADAM_STEP = 0.001, 0.9, 0.999, 1e-08, 0.01, 10
GELU_C0 = math.sqrt(2.0 / math.pi)
GELU_C1 = 0.044715

FFN_FT = 256
VMEM_LIMIT_BYTES = 56 * 1024 * 1024

VMEM_FULL = pl.BlockSpec(memory_space=pltpu.VMEM)
ANY = pl.BlockSpec(memory_space=pl.ANY)


def _cparams(n_grid):
    return pltpu.CompilerParams(dimension_semantics=("arbitrary",) * n_grid, vmem_limit_bytes=VMEM_LIMIT_BYTES)


def _tile(t):
    return 512 if t >= 1024 else t // 2


def _nn(a, b):
    return jnp.dot(a, b, preferred_element_type=F32)


def _nt(a, b):
    return lax.dot_general(a, b, (((1,), (1,)), ((), ())), preferred_element_type=F32)


def _tn(a, b):
    return lax.dot_general(a, b, (((0,), (0,)), ((), ())), preferred_element_type=F32)


def _rms_parts(x):
    r = lax.rsqrt(jnp.mean(x * x, axis=-1, keepdims=True) + EPS)
    return x * r, r


def _rms_bwd(dn, g, xhat, r):
    dxh = dn * g
    dx = r * (dxh - xhat * jnp.mean(dxh * xhat, axis=-1, keepdims=True))
    return dx, jnp.sum(dn * xhat, axis=0, keepdims=True)


def _peers():
    x, y, c = lax.axis_index("x"), lax.axis_index("y"), lax.axis_index("c")
    out = []
    for k in range(1, N_DEV):
        px = 1 - x if k & 4 else x
        py = 1 - y if k & 2 else y
        pc = 1 - c if k & 1 else c
        out.append(((px, py, pc), 4 * px + 2 * py + pc))
    return 4 * x + 2 * y + c, out


def _exchange_copies(src_refs, out_refs, send_sems, recv_sems, local_sems, scatter, with_recvs):
    me, peers = _peers()
    locals_, sends, recvs = [], [], []
    for a, (src_ref, out_ref) in enumerate(zip(src_refs, out_refs)):
        def mine(idx, src_ref=src_ref):
            return src_ref.at[idx] if scatter else src_ref

        locals_.append(pltpu.make_async_copy(mine(me), out_ref.at[me], local_sems.at[a]))
        for k, (dev, idx) in enumerate(peers):
            sends.append(pltpu.make_async_remote_copy(
                src_ref=mine(idx), dst_ref=out_ref.at[me], send_sem=send_sems.at[a, k], recv_sem=recv_sems.at[a, k],
                device_id=dev, device_id_type=pl.DeviceIdType.MESH))
            if with_recvs:
                recvs.append(pltpu.make_async_remote_copy(
                    src_ref=mine(idx), dst_ref=out_ref.at[idx], send_sem=send_sems.at[a, k],
                    recv_sem=recv_sems.at[a, k], device_id=dev, device_id_type=pl.DeviceIdType.MESH))
    return locals_, sends, recvs


def _remote(src, dst, send_sems, recv_sems, a, k, dev):
    return pltpu.make_async_remote_copy(src_ref=src, dst_ref=dst, send_sem=send_sems.at[a, k],
                                        recv_sem=recv_sems.at[a, k], device_id=dev,
                                        device_id_type=pl.DeviceIdType.MESH)


def _gather_places():
    x, y, c = lax.axis_index("x"), lax.axis_index("y"), lax.axis_index("c")

    def at(chip, core):
        return 4 * chip[0] + 2 * chip[1] + core

    xn, yn, diag = (1 - x, y), (x, 1 - y), (1 - x, 1 - y)
    relay = (x * (1 - c) + (1 - x) * c, (1 - y) * (1 - c) + y * c)
    passed = ((1 - x) * (1 - c) + x * c, y * (1 - c) + (1 - y) * c)
    return dict(sibling=(x, y, 1 - c), me=at((x, y), c), sib=at((x, y), 1 - c), c=c, at=at,
                xn=xn, yn=yn, diag=diag, relay=relay, passed=passed)


def _gather_start(src_refs, out_refs, send_sems, recv_sems, local_sems):
    p = _gather_places()
    for a, (src, out) in enumerate(zip(src_refs, out_refs)):
        mine = out.at[p["me"]]
        pltpu.make_async_copy(src, mine, local_sems.at[a]).start()
        _remote(src, mine, send_sems, recv_sems, a, 0, p["sibling"]).start()
        _remote(src, mine, send_sems, recv_sems, a, 1, (*p["xn"], p["c"])).start()
        _remote(src, mine, send_sems, recv_sems, a, 2, (*p["yn"], p["c"])).start()


def _gather_forward(src_refs, out_refs, send_sems, recv_sems, local_sems):
    p = _gather_places()
    c, at = p["c"], p["at"]
    for a, (src, out) in enumerate(zip(src_refs, out_refs)):
        from_x, from_y = out.at[at(p["xn"], c)], out.at[at(p["yn"], c)]
        _remote(src, from_x, send_sems, recv_sems, a, 1, (*p["xn"], c)).wait_recv()
        _remote(src, from_y, send_sems, recv_sems, a, 2, (*p["yn"], c)).wait_recv()
        relayed = out.at[at(p["passed"], c)]
        _remote(relayed, relayed, send_sems, recv_sems, a, 3, (*p["relay"], c)).start()
        _remote(from_x, from_x, send_sems, recv_sems, a, 4, p["sibling"]).start()
        _remote(from_y, from_y, send_sems, recv_sems, a, 5, p["sibling"]).start()


def _gather_finish(src_refs, out_refs, send_sems, recv_sems, local_sems):
    p = _gather_places()
    c, at, sibling = p["c"], p["at"], p["sibling"]
    arrays = list(enumerate(zip(src_refs, out_refs)))
    for a, (src, out) in arrays:
        from_diag = out.at[at(p["diag"], c)]
        _remote(src, from_diag, send_sems, recv_sems, a, 3, (*p["relay"], c)).wait_recv()
        _remote(from_diag, from_diag, send_sems, recv_sems, a, 6, sibling).start()
    for a, (src, out) in arrays:
        _remote(src, out.at[p["sib"]], send_sems, recv_sems, a, 0, sibling).wait_recv()
        for k, chip in ((4, p["xn"]), (5, p["yn"]), (6, p["diag"])):
            _remote(src, out.at[at(chip, 1 - c)], send_sems, recv_sems, a, k, sibling).wait_recv()
        for k in range(N_DEV - 1):
            _remote(src, out.at[p["me"]], send_sems, recv_sems, a, k, sibling).wait_send()
        pltpu.make_async_copy(src, out.at[p["me"]], local_sems.at[a]).wait()


def _exchange_start(*refs, scatter):
    locals_, sends, _ = _exchange_copies(*refs, scatter=scatter, with_recvs=False)
    for cp in locals_ + sends:
        cp.start()


def _exchange_wait(*refs, scatter):
    locals_, sends, recvs = _exchange_copies(*refs, scatter=scatter, with_recvs=True)
    for cp in recvs:
        cp.wait_recv()
    for cp in sends:
        cp.wait_send()
    for cp in locals_:
        cp.wait()


def _halves_places():
    x, y, c = lax.axis_index("x"), lax.axis_index("y"), lax.axis_index("c")
    flips = [(1 - x, y), (x, 1 - y), (1 - x, 1 - y)]
    return (x, y, 1 - c), c, 2 * x + y, [((fx, fy, c), 2 * fx + fy) for fx, fy in flips]


def _pair_start(src_refs, out_refs, send_sems, recv_sems, local_sems):
    sibling, c, _, _ = _halves_places()
    for a, (src, out) in enumerate(zip(src_refs, out_refs)):
        for i in range(4):
            _remote(src.at[2 * i + 1 - c], out.at[i], send_sems, recv_sems, a, i, sibling).start()


def _pair_finish(src_refs, out_refs, send_sems, recv_sems, local_sems):
    sibling, c, _, _ = _halves_places()
    for a, (src, out) in enumerate(zip(src_refs, out_refs)):
        for i in range(4):
            _remote(src.at[2 * i + 1 - c], out.at[i], send_sems, recv_sems, a, i, sibling).wait()


def _chips_start(src_refs, out_refs, send_sems, recv_sems, local_sems):
    _, _, chip, others = _halves_places()
    for a, (src, out) in enumerate(zip(src_refs, out_refs)):
        pltpu.make_async_copy(src.at[chip], out.at[chip], local_sems.at[a]).start()
        for k, (dev, their_chip) in enumerate(others):
            _remote(src.at[their_chip], out.at[chip], send_sems, recv_sems, a, k, dev).start()


def _chips_finish(src_refs, out_refs, send_sems, recv_sems, local_sems):
    _, _, chip, others = _halves_places()
    for a, (src, out) in enumerate(zip(src_refs, out_refs)):
        for k, (dev, their_chip) in enumerate(others):
            _remote(src.at[their_chip], out.at[their_chip], send_sems, recv_sems, a, k, dev).wait_recv()
        for k, (dev, their_chip) in enumerate(others):
            _remote(src.at[their_chip], out.at[chip], send_sems, recv_sems, a, k, dev).wait_send()
        pltpu.make_async_copy(src.at[chip], out.at[chip], local_sems.at[a]).wait()


EXCHANGES = {
    "gather": (_gather_start, _gather_forward, _gather_finish, N_DEV, False),
    "scatter": (functools.partial(_exchange_start, scatter=True), None, functools.partial(_exchange_wait, scatter=True),
                N_DEV, True),
    "pair": (_pair_start, None, _pair_finish, 4, True),
    "chips": (_chips_start, None, _chips_finish, 4, True),
}


def _exchange_sems(n_arrays):
    return [pltpu.SemaphoreType.DMA((n_arrays, N_DEV - 1)), pltpu.SemaphoreType.DMA((n_arrays, N_DEV - 1)),
            pltpu.SemaphoreType.DMA((n_arrays,))]


def _exchange_shapes(srcs, kind):
    lead, slabbed = EXCHANGES[kind][3:]
    return [jax.ShapeDtypeStruct((lead,) + tuple(s.shape[1:] if slabbed else s.shape), s.dtype) for s in srcs]


def _carries(carry):
    if carry is None:
        return []
    return [carry] if isinstance(carry, tuple) else list(carry)


def _call(body, *, name, grid, in_specs, out_specs, out_shape, args, scratch_shapes=(), carry=None):
    n_in, n_out, n_scr = len(in_specs), len(out_specs), len(scratch_shapes)
    groups = _carries(carry)
    sizes = [len(arrays) for arrays, _ in groups]
    nc = sum(sizes)

    def wrapped(*refs):
        ins, refs = refs[:n_in], refs[n_in:]
        csrc, refs = refs[:nc], refs[nc:]
        outs, refs = refs[:n_out], refs[n_out:]
        cland, refs = refs[:nc], refs[nc:]
        scr, sems = refs[:n_scr], refs[n_scr:]

        def run(phase):
            at = 0
            for gi, ((_, kind), size) in enumerate(zip(groups, sizes)):
                if EXCHANGES[kind][phase] is not None:
                    EXCHANGES[kind][phase](csrc[at:at + size], cland[at:at + size], *sems[3 * gi:3 * gi + 3])
                at += size

        last = pl.program_id(0) == grid[0] - 1
        if nc:
            pl.when(pl.program_id(0) == 0)(functools.partial(run, 0))
            pl.when(last)(functools.partial(run, 1))
        if body is not None:
            body(*ins, *outs, *scr)
        if nc:
            pl.when(last)(functools.partial(run, 2))

    res = pl.pallas_call(
        wrapped, name=name, grid=grid,
        in_specs=list(in_specs) + [ANY] * nc, out_specs=list(out_specs) + [ANY] * nc,
        out_shape=list(out_shape) + [s for arrays, kind in groups for s in _exchange_shapes(arrays, kind)],
        scratch_shapes=list(scratch_shapes) + [s for size in sizes for s in _exchange_sems(size)],
        compiler_params=_cparams(1),
    )(*args, *[a for arrays, _ in groups for a in arrays])
    return res[:n_out], res[n_out:]


def _row_tile(tm, d):
    return pl.BlockSpec((tm, d), lambda i: (i, 0))


def _acc_row(d):
    return pl.BlockSpec((1, d), lambda i: (0, 0))


def _ffn_body(x_ref, g_ref, w1_ref, w3_ref, w2_ref, acc_ref, a_ref, b_ref, n_ref):
    xv = x_ref[...]
    xhat, _ = _rms_parts(xv)
    n = (xhat * g_ref[...]).astype(BF16)
    n_ref[...] = n
    acc_ref[...] = xv

    def fstep(f, c):
        rows = pl.ds(pl.multiple_of(f * FFN_FT, FFN_FT), FFN_FT)
        a = _nt(n, w1_ref[rows, :])
        b = _nt(n, w3_ref[rows, :])
        a_ref[f] = a.astype(BF16)
        b_ref[f] = b.astype(BF16)
        s = (a * jax.nn.sigmoid(a) * b).astype(BF16)
        acc_ref[...] += 0.5 * _nn(s, w2_ref[rows, :])
        return c

    lax.fori_loop(0, D_FF // FFN_FT, fstep, 0, unroll=True)


def _ffn_fwd(x, g, w1t, w3t, w2, name, carry=None):
    t = x.shape[0]
    tm = _tile(t)
    nf = D_FF // FFN_FT
    blk3 = pl.BlockSpec((nf, tm, FFN_FT), lambda i: (0, i, 0))
    sh3 = jax.ShapeDtypeStruct((nf, t, FFN_FT), BF16)
    (h, a3, b3, n), landed = _call(
        functools.partial(_ffn_body), name=name, grid=(t // tm,),
        in_specs=[_row_tile(tm, D_MODEL), _acc_row(D_MODEL), VMEM_FULL, VMEM_FULL, VMEM_FULL],
        out_specs=[_row_tile(tm, D_MODEL), blk3, blk3, _row_tile(tm, D_MODEL)],
        out_shape=[jax.ShapeDtypeStruct((t, D_MODEL), F32), sh3, sh3, jax.ShapeDtypeStruct((t, D_MODEL), BF16)],
        args=(x, g, w1t, w3t, w2), carry=carry)
    return h, (a3, b3, n), landed


def _ffn_fwd_head(x, g, w1t, w3t, w2, gf, target, name):
    t = x.shape[0]
    tm = _tile(t)
    nf = D_FF // FFN_FT

    def body(x_ref, g_ref, w1_ref, w3_ref, w2_ref, gf_ref, t_ref, loss_ref, dh_ref, dgf_ref, a_ref, b_ref, n_ref, acc):
        _ffn_body(x_ref, g_ref, w1_ref, w3_ref, w2_ref, acc, a_ref, b_ref, n_ref)
        _head_math(acc[...], gf_ref[...], t_ref[...], loss_ref, dh_ref, dgf_ref)

    blk3 = pl.BlockSpec((nf, tm, FFN_FT), lambda i: (0, i, 0))
    sh3 = jax.ShapeDtypeStruct((nf, t, FFN_FT), BF16)
    (loss, dh, dgf, a3, b3, n), _ = _call(
        body, name=name, grid=(t // tm,),
        in_specs=[_row_tile(tm, D_MODEL), _acc_row(D_MODEL), VMEM_FULL, VMEM_FULL, VMEM_FULL, _acc_row(D_MODEL),
                  _row_tile(tm, D_MODEL)],
        out_specs=[pl.BlockSpec((1, 1), lambda i: (0, 0)), _row_tile(tm, D_MODEL), _acc_row(D_MODEL), blk3, blk3,
                   _row_tile(tm, D_MODEL)],
        out_shape=[jax.ShapeDtypeStruct((1, 1), F32), jax.ShapeDtypeStruct((t, D_MODEL), F32),
                   jax.ShapeDtypeStruct((1, D_MODEL), F32), sh3, sh3, jax.ShapeDtypeStruct((t, D_MODEL), BF16)],
        scratch_shapes=[pltpu.VMEM((tm, D_MODEL), F32)],
        args=(x, g, w1t, w3t, w2, gf, target))
    return loss, dh, dgf, (a3, b3, n)


def _head_math(h, gv, target, loss_ref, dh_ref, dg_ref):
    i = pl.program_id(0)
    xhat, r = _rms_parts(h)
    err = xhat * gv - target
    dx, dg = _rms_bwd(err * (1.0 / D_MODEL), gv, xhat, r)
    dh_ref[...] = dx

    @pl.when(i == 0)
    def _():
        loss_ref[...] = jnp.zeros_like(loss_ref)
        dg_ref[...] = jnp.zeros_like(dg_ref)

    loss_ref[...] += (0.5 / D_MODEL) * jnp.sum(jnp.sum(err * err, axis=1, keepdims=True), axis=0, keepdims=True)
    dg_ref[...] += dg


def _ffn_bwd(x, dh, g, a3, b3, w1t, w3t, w2, name, carry=None):
    t = x.shape[0]
    tm = _tile(t) // 2
    nf = D_FF // FFN_FT

    def body(x_ref, dh_ref, g_ref, a_ref, b_ref, w1_ref, w3_ref, w2_ref,
             dx_ref, dg_ref, da_ref, db_ref, s_ref, dhh_ref, dn_acc):
        i = pl.program_id(0)
        xv = x_ref[...]
        gv = g_ref[...]
        xhat, r = _rms_parts(xv)
        dhv = dh_ref[...]
        dhh = (0.5 * dhv).astype(BF16)
        dhh_ref[...] = dhh
        dn_acc[...] = jnp.zeros_like(dn_acc)

        def fstep(f, c):
            rows = pl.ds(f * FFN_FT, FFN_FT)
            w1c, w3c, w2c = w1_ref[rows, :], w3_ref[rows, :], w2_ref[rows, :]
            a = a_ref[f].astype(F32)
            b = b_ref[f].astype(F32)
            sg = jax.nn.sigmoid(a)
            sl = a * sg
            ds = _nt(dhh, w2c)
            da = (ds * b * sg * (1.0 + a * (1.0 - sg))).astype(BF16)
            db = (ds * sl).astype(BF16)
            s_ref[f] = (sl * b).astype(BF16)
            da_ref[f] = da
            db_ref[f] = db
            return c

        def nstep(f, c):
            rows = pl.ds(f * FFN_FT, FFN_FT)
            dn_acc[...] += _nn(da_ref[f], w1_ref[rows, :]) + _nn(db_ref[f], w3_ref[rows, :])
            return c

        for f in range(nf + 1):
            if f < nf:
                fstep(f, 0)
            if f:
                nstep(f - 1, 0)
        dx, dg = _rms_bwd(dn_acc[...], gv, xhat, r)
        dx_ref[...] = dhv + dx

        @pl.when(i == 0)
        def _():
            dg_ref[...] = jnp.zeros_like(dg_ref)

        dg_ref[...] += dg

    blk3 = pl.BlockSpec((nf, tm, FFN_FT), lambda i: (0, i, 0))
    sh3 = jax.ShapeDtypeStruct((nf, t, FFN_FT), BF16)
    return _call(
        body, name=name, grid=(t // tm,),
        in_specs=[_row_tile(tm, D_MODEL), _row_tile(tm, D_MODEL), _acc_row(D_MODEL), blk3, blk3,
                  VMEM_FULL, VMEM_FULL, VMEM_FULL],
        out_specs=[_row_tile(tm, D_MODEL), _acc_row(D_MODEL), blk3, blk3, blk3, _row_tile(tm, D_MODEL)],
        out_shape=[jax.ShapeDtypeStruct((t, D_MODEL), F32), jax.ShapeDtypeStruct((1, D_MODEL), F32), sh3, sh3, sh3,
                   jax.ShapeDtypeStruct((t, D_MODEL), BF16)],
        scratch_shapes=[pltpu.VMEM((tm, D_MODEL), F32)],
        args=(x, dh, g, a3, b3, w1t, w3t, w2), carry=carry)


def _mm_tn(a, b, name, carry=None):
    t, n = b.shape
    kc = min(512, t)
    if a.ndim == 3:
        nb, _, tb = a.shape
        a_spec = pl.BlockSpec((1, t, tb), lambda i: (i, 0, 0))
    else:
        m = a.shape[1]
        tb = min(m, 256)
        nb = m // tb
        a_spec = pl.BlockSpec((t, tb), lambda i: (0, i))
    three_d = a.ndim == 3

    def body(a_ref, b_ref, o_ref, acc):
        acc[...] = jnp.zeros_like(acc)

        def kstep(k, c):
            rows = pl.ds(pl.multiple_of(k * kc, kc), kc)
            av = a_ref[0, rows, :] if three_d else a_ref[rows, :]
            acc[...] += _tn(av.astype(BF16), b_ref[rows, :])
            return c

        lax.fori_loop(0, t // kc, kstep, 0, unroll=True)
        o_ref[...] = acc[...].astype(BF16)

    (out,), landed = _call(
        body, name=name, grid=(nb,),
        in_specs=[a_spec, VMEM_FULL],
        out_specs=[pl.BlockSpec((tb, n), lambda i: (i, 0))],
        out_shape=[jax.ShapeDtypeStruct((nb * tb, n), BF16)],
        scratch_shapes=[pltpu.VMEM((tb, n), F32)],
        args=(a, b), carry=carry)
    return (out, landed) if carry is not None else out


MM_TB = 256


def _mm_tn_many(arrays, b, name):
    t, n = b.shape
    kc = min(512, t)
    counts = [a.shape[1] // MM_TB for a in arrays]
    starts = [sum(counts[:k]) for k in range(len(arrays))]

    def spec(start, count):
        return pl.BlockSpec((t, MM_TB), lambda i: (0, jnp.clip(i - start, 0, count - 1)))

    def body(*refs):
        a_refs, (b_ref, o_ref, acc) = refs[:len(arrays)], refs[len(arrays):]
        i = pl.program_id(0)
        for a_ref, start, count in zip(a_refs, starts, counts):
            @pl.when((i >= start) & (i < start + count))
            def _(a_ref=a_ref):
                acc[...] = jnp.zeros_like(acc)

                def kstep(k, c):
                    rows = pl.ds(pl.multiple_of(k * kc, kc), kc)
                    acc[...] += _tn(a_ref[rows, :].astype(BF16), b_ref[rows, :])
                    return c

                lax.fori_loop(0, t // kc, kstep, 0, unroll=True)
                o_ref[...] = acc[...].astype(BF16)

    return pl.pallas_call(
        body, name=name, grid=(sum(counts),),
        in_specs=[spec(s, c) for s, c in zip(starts, counts)] + [VMEM_FULL],
        out_specs=pl.BlockSpec((MM_TB, n), lambda i: (i, 0)),
        out_shape=jax.ShapeDtypeStruct((sum(counts) * MM_TB, n), BF16),
        scratch_shapes=[pltpu.VMEM((MM_TB, n), F32)],
        compiler_params=_cparams(1),
    )(*arrays, b)


def _mix_pre_fwd(h, g, wint, carry=None):
    t = h.shape[0]
    tm = _tile(t)

    def body(h_ref, g_ref, w_ref, u_ref, *outs):
        xhat, _ = _rms_parts(h_ref[...])
        u = (xhat * g_ref[...]).astype(BF16)
        u_ref[...] = u
        for o_ref, off, size in zip(outs, IN_OFFS, IN_SIZES):
            o_ref[...] = _nt(u, w_ref[off:off + size, :])

    return _call(
        body, name="mix_pre_fwd", grid=(t // tm,),
        in_specs=[_row_tile(tm, D_MODEL), _acc_row(D_MODEL), VMEM_FULL],
        out_specs=[_row_tile(tm, D_MODEL)] + [_row_tile(tm, s) for s in IN_SIZES],
        out_shape=[jax.ShapeDtypeStruct((t, D_MODEL), BF16)] + [jax.ShapeDtypeStruct((t, s), F32) for s in IN_SIZES],
        args=(h, g, wint), carry=carry)


def _mix_pre_bwd(h, g, wint, dh2, dz, carry=None):
    t = h.shape[0]
    tm = _tile(t)

    def body(h_ref, g_ref, w_ref, dh2_ref, *rest):
        dz_refs, (dh1_ref, dg_ref) = rest[:len(IN_SIZES)], rest[len(IN_SIZES):]
        i = pl.program_id(0)
        gv = g_ref[...]
        xhat, r = _rms_parts(h_ref[...])
        du = jnp.zeros((tm, D_MODEL), F32)
        for dz_ref, off, size in zip(dz_refs, IN_OFFS, IN_SIZES):
            du = du + _nn(dz_ref[...].astype(BF16), w_ref[off:off + size, :])
        dx, dg = _rms_bwd(du, gv, xhat, r)
        dh1_ref[...] = dh2_ref[...] + dx

        @pl.when(i == 0)
        def _():
            dg_ref[...] = jnp.zeros_like(dg_ref)

        dg_ref[...] += dg

    return _call(
        body, name="mix_pre_bwd", grid=(t // tm,),
        in_specs=[_row_tile(tm, D_MODEL), _acc_row(D_MODEL), VMEM_FULL, _row_tile(tm, D_MODEL)]
        + [_row_tile(tm, s) for s in IN_SIZES],
        out_specs=[_row_tile(tm, D_MODEL), _acc_row(D_MODEL)],
        out_shape=[jax.ShapeDtypeStruct((t, D_MODEL), F32), jax.ShapeDtypeStruct((1, D_MODEL), F32)],
        args=(h, g, wint, dh2, *dz), carry=carry)


def _disc_math(lre, lim, ldt, bre, bim):
    dt = jnp.exp(ldt)
    mag = jnp.exp(lre * dt)
    ar = mag * jnp.cos(lim * dt)
    ai = mag * jnp.sin(lim * dt)
    den = lre * lre + lim * lim
    nr = ar - 1.0
    fr = (nr * lre + ai * lim) / den
    fi = (ai * lre - nr * lim) / den
    fr, fi = fr[:, None, :], fi[:, None, :]
    return ar, ai, fr * bre - fi * bim, fr * bim + fi * bre


def _s5_disc(lre, lim, ldt, bre, bim):
    def body(lre_ref, lim_ref, ldt_ref, bre_ref, bim_ref, ar_ref, ai_ref, bbr_ref, bbi_ref):
        ar, ai, bbr, bbi = _disc_math(lre_ref[...], lim_ref[...], ldt_ref[...], bre_ref[...], bim_ref[...])
        ar_ref[...] = ar
        ai_ref[...] = ai
        bbr_ref[...] = bbr
        bbi_ref[...] = bbi

    small = jax.ShapeDtypeStruct(lre.shape, F32)
    big = jax.ShapeDtypeStruct(bre.shape, F32)
    return pl.pallas_call(body, name="s5_disc", out_shape=[small, small, big, big],
                          in_specs=[VMEM_FULL] * 5, out_specs=[VMEM_FULL] * 4)(lre, lim, ldt, bre, bim)


def _s5_disc_bwd(lre, lim, ldt, bre, bim, dar, dai, dbbr, dbbi):
    def body(lre_ref, lim_ref, ldt_ref, bre_ref, bim_ref, dar_ref, dai_ref, dbbr_ref, dbbi_ref,
             glre_ref, glim_ref, gldt_ref, gbre_ref, gbim_ref):
        _, vjp = jax.vjp(_disc_math, lre_ref[...], lim_ref[...], ldt_ref[...], bre_ref[...], bim_ref[...])
        glre, glim, gldt, gbre, gbim = vjp((dar_ref[...], dai_ref[...], dbbr_ref[...], dbbi_ref[...]))
        glre_ref[...] = glre
        glim_ref[...] = glim
        gldt_ref[...] = gldt
        gbre_ref[...] = gbre
        gbim_ref[...] = gbim

    small = jax.ShapeDtypeStruct(lre.shape, F32)
    big = jax.ShapeDtypeStruct(bre.shape, F32)
    return pl.pallas_call(body, name="s5_disc_bwd",
                          out_shape=[small, small, jax.ShapeDtypeStruct(ldt.shape, F32), big, big],
                          in_specs=[VMEM_FULL] * 9, out_specs=[VMEM_FULL] * 5,
                          )(lre, lim, ldt, bre, bim, dar, dai, dbbr, dbbi)


def _cmul(ar, ai, br, bi):
    return ar * br - ai * bi, ar * bi + ai * br


def _cpow(ar, ai, n):
    rr, ri = None, None
    pr, pi = ar, ai
    while n:
        if n & 1:
            rr, ri = (pr, pi) if rr is None else _cmul(rr, ri, pr, pi)
        n >>= 1
        if n:
            pr, pi = _cmul(pr, pi, pr, pi)
    return rr, ri


def _shift_rows(v, down):
    row = lax.broadcasted_iota(jnp.int32, v.shape, 0)
    if down:
        return jnp.where(row == 0, 0.0, pltpu.roll(v, 1, 0))
    return jnp.where(row == S5_SEGS - 1, 0.0, pltpu.roll(v, S5_SEGS - 1, 0))


def _chain_segments(er, ei, pr, pi, down):
    fr, fi = er, ei
    for _ in range(S5_SEGS - 1):
        sr, si = _shift_rows(fr, down), _shift_rows(fi, down)
        mr, mi = _cmul(pr, pi, sr, si)
        fr, fi = er + mr, ei + mi
    return _shift_rows(fr, down), _shift_rows(fi, down)


def _rows_to_scan_order(src_ref, dst_ref, t):
    ls = t // S5_SEGS

    def tile(j, c):
        dst_ref[pl.ds(pl.multiple_of(j * S5_SEGS, S5_SEGS), S5_SEGS), :] = src_ref[pl.ds(j, S5_SEGS, stride=ls), :]
        return c

    lax.fori_loop(0, ls, tile, 0, unroll=8)


def _rows_from_scan_order(src_ref, dst_ref, t):
    ls = t // S5_SEGS
    for s in range(S5_SEGS):
        def tile(jb, c, s=s):
            dst_ref[pl.ds(pl.multiple_of(s * ls + jb * 8, 8), 8), :] = (
                src_ref[pl.ds(jb * 8 * S5_SEGS + s, 8, stride=S5_SEGS), :])
            return c

        lax.fori_loop(0, ls // 8, tile, 0, unroll=8)


def _s5_fwd(ug, bd, ctd, ar4, ai4, dskip, carry=None):
    t = ug.shape[0]
    ls = t // S5_SEGS
    rc = min(512, t)
    ns = S5_BSTATE

    def body(ugn_ref, bd_ref, ct_ref, ar_ref, ai_ref, d_ref, xs_hbm, yn_ref, buf, ug_ref, y_ref, sem):
        cb = pl.program_id(0)
        bdv = bd_ref[0]
        _rows_to_scan_order(ugn_ref, ug_ref, t)

        def mm(i, c):
            rows = pl.ds(pl.multiple_of(i * rc, rc), rc)
            buf[rows, :] = _nn(ug_ref[rows, :].astype(BF16), bdv)
            return c

        lax.fori_loop(0, t // rc, mm, 0, unroll=True)
        arb = jnp.broadcast_to(ar_ref[0], (S5_SEGS, ns))
        aib = jnp.broadcast_to(ai_ref[0], (S5_SEGS, ns))

        def step(j, c, store):
            sr, si = c
            rows = pl.ds(pl.multiple_of(j * S5_SEGS, S5_SEGS), S5_SEGS)
            nr = arb * sr - aib * si + buf[rows, 0:ns]
            ni = arb * si + aib * sr + buf[rows, ns:2 * ns]
            if store:
                buf[rows, 0:ns] = nr
                buf[rows, ns:2 * ns] = ni
            return nr, ni

        zero = jnp.zeros((S5_SEGS, ns), F32)
        er, ei = lax.fori_loop(0, ls, functools.partial(step, store=False), (zero, zero))
        pr, pi = _cpow(arb, aib, ls)
        init = _chain_segments(er, ei, pr, pi, down=True)
        lax.fori_loop(0, ls, functools.partial(step, store=True), init)

        out = pltpu.make_async_copy(buf, xs_hbm.at[cb], sem)
        out.start()
        ctv = ct_ref[0]
        dv = d_ref[...]

        def ymm(i, c):
            rows = pl.ds(pl.multiple_of(i * rc, rc), rc)
            y_ref[rows, :] = _nn(buf[rows, :].astype(BF16), ctv) + dv * ug_ref[rows, :]
            return c

        lax.fori_loop(0, t // rc, ymm, 0, unroll=True)
        _rows_from_scan_order(y_ref, yn_ref, t)
        out.wait()

    return _call(
        body, name="s5_fwd", grid=(S5_BLOCKS,),
        in_specs=[pl.BlockSpec((t, 128), lambda i: (0, i)),
                  pl.BlockSpec((1, 128, 2 * ns), lambda i: (i, 0, 0)),
                  pl.BlockSpec((1, 2 * ns, 128), lambda i: (i, 0, 0)),
                  pl.BlockSpec((1, 1, ns), lambda i: (i, 0, 0)),
                  pl.BlockSpec((1, 1, ns), lambda i: (i, 0, 0)),
                  pl.BlockSpec((1, 128), lambda i: (0, i))],
        out_specs=[ANY, pl.BlockSpec((t, 128), lambda i: (0, i))],
        out_shape=[jax.ShapeDtypeStruct((S5_BLOCKS, t, 2 * ns), F32), jax.ShapeDtypeStruct((t, S5_WIDTH), F32)],
        scratch_shapes=[pltpu.VMEM((t, 2 * ns), F32), pltpu.VMEM((t, 128), F32), pltpu.VMEM((t, 128), F32),
                        pltpu.SemaphoreType.DMA(())],
        args=(ug, bd, ctd, ar4, ai4, dskip), carry=carry)


def _s5_bwd(dy, ug, xs, cd, bdt, ar4, ai4, dskip, carry=None):
    t = ug.shape[0]
    ls = t // S5_SEGS
    rc = min(512, t)
    ns = S5_BSTATE

    def body(dyn_ref, ugn_ref, xs_hbm, cd_ref, bdt_ref, ar_ref, ai_ref, d_ref,
             dugn_ref, dbd_ref, dcd_ref, dd_ref, dar_ref, dai_ref, xbuf, lam, dy_ref, ug_ref, dug_ref, sem):
        cb = pl.program_id(0)
        load = pltpu.make_async_copy(xs_hbm.at[cb], xbuf, sem)
        load.start()
        cdv = cd_ref[0]
        _rows_to_scan_order(dyn_ref, dy_ref, t)
        _rows_to_scan_order(ugn_ref, ug_ref, t)

        def mm(i, c):
            rows = pl.ds(pl.multiple_of(i * rc, rc), rc)
            lam[rows, :] = _nn(dy_ref[rows, :].astype(BF16), cdv)
            return c

        lax.fori_loop(0, t // rc, mm, 0, unroll=True)
        arb = jnp.broadcast_to(ar_ref[0], (S5_SEGS, ns))
        aib = jnp.broadcast_to(ai_ref[0], (S5_SEGS, ns))

        def lam_step(j, lr, li):
            rows = pl.ds(pl.multiple_of(j * S5_SEGS, S5_SEGS), S5_SEGS)
            nr = arb * lr + aib * li + lam[rows, 0:ns]
            ni = arb * li - aib * lr + lam[rows, ns:2 * ns]
            return rows, nr, ni

        def pass1(jj, c):
            _, nr, ni = lam_step(ls - 1 - jj, *c)
            return nr, ni

        zero = jnp.zeros((S5_SEGS, ns), F32)
        er, ei = lax.fori_loop(0, ls, pass1, (zero, zero))
        pr, pi = _cpow(arb, aib, ls)
        init = _chain_segments(er, ei, pr, -pi, down=False)
        load.wait()

        def accumulate(acc, nr, ni, xpr, xpi):
            return acc[0] + nr * xpr + ni * xpi, acc[1] + ni * xpr - nr * xpi

        def pass2(jj, c):
            lr, li, accr, acci = c
            j = ls - 1 - jj
            rows, nr, ni = lam_step(j, lr, li)
            lam[rows, 0:ns] = nr
            lam[rows, ns:2 * ns] = ni
            prev = pl.ds(pl.multiple_of((j - 1) * S5_SEGS, S5_SEGS), S5_SEGS)
            accr, acci = accumulate((accr, acci), nr, ni, xbuf[prev, 0:ns], xbuf[prev, ns:2 * ns])
            return nr, ni, accr, acci

        lr, li, accr, acci = lax.fori_loop(0, ls - 1, pass2, (init[0], init[1], zero, zero))
        rows, nr, ni = lam_step(0, lr, li)
        lam[rows, 0:ns] = nr
        lam[rows, ns:2 * ns] = ni
        last = pl.ds((ls - 1) * S5_SEGS, S5_SEGS)
        accr, acci = accumulate((accr, acci), nr, ni,
                                _shift_rows(xbuf[last, 0:ns], True), _shift_rows(xbuf[last, ns:2 * ns], True))
        dar_ref[0] = jnp.sum(accr, axis=0, keepdims=True)
        dai_ref[0] = jnp.sum(acci, axis=0, keepdims=True)

        bdtv = bdt_ref[0]
        dv = d_ref[...]
        dbd_ref[...] = jnp.zeros_like(dbd_ref)
        dcd_ref[...] = jnp.zeros_like(dcd_ref)
        dd_ref[...] = jnp.zeros_like(dd_ref)

        def tail(i, c):
            rows = pl.ds(pl.multiple_of(i * rc, rc), rc)
            dy = dy_ref[rows, :]
            ug = ug_ref[rows, :]
            lb = lam[rows, :].astype(BF16)
            dug_ref[rows, :] = _nn(lb, bdtv) + dv * dy
            dbd_ref[0] += _tn(ug.astype(BF16), lb)
            dcd_ref[0] += _tn(dy.astype(BF16), xbuf[rows, :].astype(BF16))
            dd_ref[...] += jnp.sum(dy * ug, axis=0, keepdims=True)
            return c

        lax.fori_loop(0, t // rc, tail, 0, unroll=True)
        _rows_from_scan_order(dug_ref, dugn_ref, t)

    chan = pl.BlockSpec((t, 128), lambda i: (0, i))
    dense = pl.BlockSpec((1, 128, 2 * ns), lambda i: (i, 0, 0))
    vec = pl.BlockSpec((1, 1, ns), lambda i: (i, 0, 0))
    return _call(
        body, name="s5_bwd", grid=(S5_BLOCKS,),
        in_specs=[chan, chan, ANY, dense, pl.BlockSpec((1, 2 * ns, 128), lambda i: (i, 0, 0)), vec, vec,
                  pl.BlockSpec((1, 128), lambda i: (0, i))],
        out_specs=[chan, dense, dense, pl.BlockSpec((1, 128), lambda i: (0, i)), vec, vec],
        out_shape=[jax.ShapeDtypeStruct((t, S5_WIDTH), F32),
                   jax.ShapeDtypeStruct((S5_BLOCKS, 128, 2 * ns), F32),
                   jax.ShapeDtypeStruct((S5_BLOCKS, 128, 2 * ns), F32),
                   jax.ShapeDtypeStruct((1, S5_WIDTH), F32),
                   jax.ShapeDtypeStruct((S5_BLOCKS, 1, ns), F32),
                   jax.ShapeDtypeStruct((S5_BLOCKS, 1, ns), F32)],
        scratch_shapes=[pltpu.VMEM((t, 2 * ns), F32), pltpu.VMEM((t, 2 * ns), F32)]
        + [pltpu.VMEM((t, 128), F32)] * 3 + [pltpu.SemaphoreType.DMA(())],
        args=(dy, ug, xs, cd, bdt, ar4, ai4, dskip), carry=carry)


def _cumsum_rows(x, reverse):
    c = x.shape[0]
    row = lax.broadcasted_iota(jnp.int32, x.shape, 0)
    d = 1
    while d < c:
        if reverse:
            x = x + jnp.where(row < c - d, pltpu.roll(x, c - d, 0), 0.0)
        else:
            x = x + jnp.where(row >= d, pltpu.roll(x, d, 0), 0.0)
        d *= 2
    return x


def _gla_common(q, k, alow, wup, bup):
    c = GLA_CHUNK
    pre = _nn(alow.astype(BF16), wup.astype(BF16)) + bup
    la = (jnp.minimum(pre, 0.0) - jnp.log(1.0 + jnp.exp(-jnp.abs(pre)))) * (1.0 / GLA_TAU)
    rr = lax.broadcasted_iota(jnp.int32, (c, c), 0)
    cc = lax.broadcasted_iota(jnp.int32, (c, c), 1)
    tril = (rr >= cc).astype(F32)
    bc = _cumsum_rows(la, reverse=False)
    bl = bc[c - 1:c, :]
    e_pos = jnp.exp(bc)
    e_neg = jnp.exp(-bc)
    e_end = jnp.exp(bl - bc)
    qt = q * (GLA_DK ** -0.5) * e_pos
    kt = k * e_neg
    ke = k * e_end
    lane = lax.broadcasted_iota(jnp.int32, (1, GLA_KEY), 1)
    masks = [((lane >= h * GLA_DK) & (lane < (h + 1) * GLA_DK)).astype(F32) for h in range(GLA_HEADS)]
    return dict(pre=pre, tril=tril, bc=bc, bl=bl, e_pos=e_pos, e_neg=e_neg, e_end=e_end,
                qt=qt, kt=kt, ke=ke, dec=jnp.exp(bl), masks=masks)


def _gla_fwd(q, k, v, alow, wup, bup, carry=None):
    t = q.shape[0]
    c = GLA_CHUNK
    n = t // c
    step = GLA_STEP_CHUNKS * c

    def body(q_ref, k_ref, v_ref, al_ref, wup_ref, bup_ref, o_ref, ss_ref, s_ref):
        i = pl.program_id(0)

        @pl.when(i == 0)
        def _():
            s_ref[...] = jnp.zeros_like(s_ref)

        wup_v, bup_v = wup_ref[...], bup_ref[...]
        s = s_ref[...]
        for j in range(GLA_STEP_CHUNKS):
            tok = slice(j * c, (j + 1) * c)
            m = _gla_common(q_ref[tok, :], k_ref[tok, :], al_ref[tok, :], wup_v, bup_v)
            ss_ref[j] = s
            sb = s.astype(BF16)
            ktb = m["kt"].astype(BF16)
            update = jnp.zeros_like(s)
            for h in range(GLA_HEADS):
                mask = m["masks"][h]
                qm = (m["qt"] * mask).astype(BF16)
                vh = v_ref[tok, h * GLA_DV:(h + 1) * GLA_DV].astype(BF16)
                p = (m["tril"] * _nt(qm, ktb)).astype(BF16)
                o_ref[tok, h * GLA_DV:(h + 1) * GLA_DV] = _nn(p, vh) + _nt(qm, sb)
                update = update + _tn(vh, (m["ke"] * mask).astype(BF16))
            s = m["dec"] * s + update
        s_ref[...] = s

    return _call(
        body, name="gla_fwd", grid=(t // step,),
        in_specs=[_row_tile(step, GLA_KEY), _row_tile(step, GLA_KEY), _row_tile(step, GLA_VAL),
                  _row_tile(step, GLA_RANK), VMEM_FULL, VMEM_FULL],
        out_specs=[_row_tile(step, GLA_VAL), pl.BlockSpec((GLA_STEP_CHUNKS, GLA_DV, GLA_KEY), lambda i: (i, 0, 0))],
        out_shape=[jax.ShapeDtypeStruct((t, GLA_VAL), F32), jax.ShapeDtypeStruct((n, GLA_DV, GLA_KEY), F32)],
        scratch_shapes=[pltpu.VMEM((GLA_DV, GLA_KEY), F32)],
        args=(q, k, v, alow, wup, bup), carry=carry)


def _gla_bwd(q, k, v, alow, wup, bup, ssave, do, carry=None):
    t = q.shape[0]
    c = GLA_CHUNK
    n = t // c

    def body(q_ref, k_ref, v_ref, al_ref, wup_ref, bup_ref, ss_ref, do_ref,
             dq_ref, dk_ref, dv_ref, dal_ref, dwup_ref, dbup_ref, ds_ref):
        i = pl.program_id(0)

        @pl.when(i == 0)
        def _():
            ds_ref[...] = jnp.zeros_like(ds_ref)
            dwup_ref[...] = jnp.zeros_like(dwup_ref)
            dbup_ref[...] = jnp.zeros_like(dbup_ref)

        wup_v, bup_v = wup_ref[...], bup_ref[...]
        ds_in = ds_ref[...]
        dwup = jnp.zeros((GLA_RANK, GLA_KEY), F32)
        dbup = jnp.zeros((1, GLA_KEY), F32)
        for j in reversed(range(GLA_STEP_CHUNKS)):
            tok = slice(j * c, (j + 1) * c)
            alow_v = al_ref[tok, :]
            m = _gla_common(q_ref[tok, :], k_ref[tok, :], alow_v, wup_v, bup_v)
            s = ss_ref[j]
            sb = s.astype(BF16)
            dsb = ds_in.astype(BF16)
            qt, kt, ke = m["qt"], m["kt"], m["ke"]
            ktb = kt.astype(BF16)
            dqt = jnp.zeros((c, GLA_KEY), F32)
            dkt = jnp.zeros((c, GLA_KEY), F32)
            dke = jnp.zeros((c, GLA_KEY), F32)
            update = jnp.zeros_like(ds_in)
            for h in range(GLA_HEADS):
                mask = m["masks"][h]
                qm = (qt * mask).astype(BF16)
                km = (kt * mask).astype(BF16)
                kem = (ke * mask).astype(BF16)
                cols = slice(h * GLA_DV, (h + 1) * GLA_DV)
                vh = v_ref[tok, cols].astype(BF16)
                doh = do_ref[tok, cols].astype(BF16)
                p = (m["tril"] * _nt(qm, ktb)).astype(BF16)
                dp = (m["tril"] * _nt(doh, vh)).astype(BF16)
                dv_ref[tok, cols] = (_tn(p, doh) + _nt(kem, dsb)).astype(BF16)
                dqt = dqt + _nn(dp, km) + _nn(doh, sb) * mask
                dkt = dkt + _tn(dp, qm)
                dke = dke + _nn(vh, dsb) * mask
                update = update + _tn(doh, qm)
            ddec = jnp.sum(ds_in * s, axis=0, keepdims=True)
            dq_ref[tok, :] = (dqt * m["e_pos"] * (GLA_DK ** -0.5)).astype(BF16)
            dk_ref[tok, :] = (dkt * m["e_neg"] + dke * m["e_end"]).astype(BF16)
            dkeke = dke * ke
            dbl = jnp.sum(dkeke, axis=0, keepdims=True) + ddec * m["dec"]
            last = (lax.broadcasted_iota(jnp.int32, (c, 1), 0) == c - 1).astype(F32)
            dla = _cumsum_rows(dqt * qt - dkt * kt - dkeke + last * dbl, reverse=True)
            dpre = dla * (1.0 / GLA_TAU) * jax.nn.sigmoid(-m["pre"])
            dpb = dpre.astype(BF16)
            dal_ref[tok, :] = _nt(dpb, wup_v.astype(BF16)).astype(BF16)
            dwup = dwup + _tn(alow_v.astype(BF16), dpb)
            dbup = dbup + jnp.sum(dpre, axis=0, keepdims=True)
            ds_in = m["dec"] * ds_in + update
        ds_ref[...] = ds_in
        dwup_ref[...] += dwup
        dbup_ref[...] += dbup

    step = GLA_STEP_CHUNKS * c
    nsteps = t // step

    def rev(d):
        return pl.BlockSpec((step, d), lambda i: (nsteps - 1 - i, 0))

    return _call(
        body, name="gla_bwd", grid=(nsteps,),
        in_specs=[rev(GLA_KEY), rev(GLA_KEY), rev(GLA_VAL), rev(GLA_RANK), VMEM_FULL, VMEM_FULL,
                  pl.BlockSpec((GLA_STEP_CHUNKS, GLA_DV, GLA_KEY), lambda i: (nsteps - 1 - i, 0, 0)), rev(GLA_VAL)],
        out_specs=[rev(GLA_KEY), rev(GLA_KEY), rev(GLA_VAL), rev(GLA_RANK),
                   pl.BlockSpec((GLA_RANK, GLA_KEY), lambda i: (0, 0)), _acc_row(GLA_KEY)],
        out_shape=[jax.ShapeDtypeStruct((t, GLA_KEY), BF16), jax.ShapeDtypeStruct((t, GLA_KEY), BF16),
                   jax.ShapeDtypeStruct((t, GLA_VAL), BF16), jax.ShapeDtypeStruct((t, GLA_RANK), BF16),
                   jax.ShapeDtypeStruct((GLA_RANK, GLA_KEY), F32), jax.ShapeDtypeStruct((1, GLA_KEY), F32)],
        scratch_shapes=[pltpu.VMEM((GLA_DV, GLA_KEY), F32)],
        args=(q, k, v, alow, wup, bup, ssave, do), carry=carry)


def _post_math(y, o, r, gs5, ggla, wg, bg, gn, ps5t, pglat):
    y2 = y * y
    th = jnp.tanh(GELU_C0 * (y + GELU_C1 * y * y2))
    z5 = 0.5 * y * (1.0 + th)
    z5b = z5.astype(BF16)
    gate = jax.nn.sigmoid(_nn(z5b, wg) + bg)
    ys5 = z5 * gate
    rs, on = [], []
    for h in range(GLA_HEADS):
        oh = o[:, h * GLA_DV:(h + 1) * GLA_DV]
        rh = lax.rsqrt(jnp.mean(oh * oh, axis=-1, keepdims=True) + EPS)
        rs.append(rh)
        on.append(oh * rh)
    on = jnp.concatenate(on, axis=-1)
    sr = jax.nn.sigmoid(r)
    silu_r = r * sr
    ygla = on * gn * silu_r
    ys5b, yglab = ys5.astype(BF16), ygla.astype(BF16)
    m5 = _nt(ys5b, ps5t)
    mg = _nt(yglab, pglat)
    s5g, glag = jax.nn.sigmoid(gs5), jax.nn.sigmoid(ggla)
    merged = s5g * m5 + glag * mg
    return dict(y2=y2, th=th, z5=z5, z5b=z5b, gate=gate, ys5b=ys5b, yglab=yglab, rs=rs, on=on, sr=sr,
                silu_r=silu_r, m5=m5, mg=mg, s5g=s5g, glag=glag, mergedb=merged.astype(BF16))


def _mix_post_fwd(y, o, r, gs5, ggla, h1, wg, bg, gn, ps5t, pglat, wout, carry=None):
    t = o.shape[0]
    tm = _tile(t)

    def body(y_ref, o_ref, r_ref, gs5_ref, ggla_ref, h1_ref, wg_ref, bg_ref, gn_ref, ps_ref, pg_ref, wo_ref, h2_ref):
        m = _post_math(y_ref[...], o_ref[...], r_ref[...], gs5_ref[...], ggla_ref[...],
                       wg_ref[...], bg_ref[...], gn_ref[...], ps_ref[...], pg_ref[...])
        h2_ref[...] = h1_ref[...] + _nn(m["mergedb"], wo_ref[...])

    (h2,), landed = _call(
        body, name="mix_post_fwd", grid=(t // tm,),
        in_specs=[_row_tile(tm, 512)] * 3 + [_row_tile(tm, D_MODEL)] * 3
        + [VMEM_FULL, _acc_row(512), _acc_row(512), VMEM_FULL, VMEM_FULL, VMEM_FULL],
        out_specs=[_row_tile(tm, D_MODEL)],
        out_shape=[jax.ShapeDtypeStruct((t, D_MODEL), F32)],
        args=(y, o, r, gs5, ggla, h1, wg, bg, gn, ps5t, pglat, wout), carry=carry)
    return h2, landed


def _mix_post_bwd(y, o, r, gs5, ggla, dh2, wg, bg, gn, ps5t, pglat, wout, carry=None):
    t = o.shape[0]
    tm = _tile(t) // 2

    def body(y_ref, o_ref, r_ref, gs5_ref, ggla_ref, dh2_ref, wg_ref, bg_ref, gn_ref, ps_ref, pg_ref, wo_ref,
             dy_ref, do_ref, dr_ref, dgs5_ref, dggla_ref, dbg_ref, dgn_ref,
             z5b_ref, dgp_ref, ys5b_ref, dm5b_ref, yglab_ref, dmgb_ref, mergedb_ref, dh2b_ref):
        i = pl.program_id(0)
        yv, ov, rv = y_ref[...], o_ref[...], r_ref[...]
        wg, gn, ps5t, pglat = wg_ref[...], gn_ref[...], ps_ref[...], pg_ref[...]
        m = _post_math(yv, ov, rv, gs5_ref[...], ggla_ref[...], wg, bg_ref[...], gn, ps5t, pglat)
        dh2b = dh2_ref[...].astype(BF16)
        dmerged = _nt(dh2b, wo_ref[...])
        s5g, glag = m["s5g"], m["glag"]
        dgs5_ref[...] = (dmerged * m["m5"] * s5g * (1.0 - s5g)).astype(BF16)
        dggla_ref[...] = (dmerged * m["mg"] * glag * (1.0 - glag)).astype(BF16)
        dm5b = (dmerged * s5g).astype(BF16)
        dmgb = (dmerged * glag).astype(BF16)
        dys5 = _nn(dm5b, ps5t)
        dygla = _nn(dmgb, pglat)
        gate, z5, th = m["gate"], m["z5"], m["th"]
        dgpre = dys5 * z5 * gate * (1.0 - gate)
        dgpb = dgpre.astype(BF16)
        dz5 = dys5 * gate + _nt(dgpb, wg)
        dgelu = 0.5 * (1.0 + th) + 0.5 * yv * (1.0 - th * th) * GELU_C0 * (1.0 + 3.0 * GELU_C1 * m["y2"])
        dy_ref[...] = dz5 * dgelu
        on, sr, silu_r = m["on"], m["sr"], m["silu_r"]
        dr_ref[...] = (dygla * on * gn * sr * (1.0 + rv * (1.0 - sr))).astype(BF16)
        dgn = jnp.sum(dygla * on * silu_r, axis=0, keepdims=True)
        don = dygla * gn * silu_r
        for h in range(GLA_HEADS):
            cols = slice(h * GLA_DV, (h + 1) * GLA_DV)
            donh, onh = don[:, cols], on[:, cols]
            do_ref[:, cols] = (m["rs"][h] * (donh - onh * jnp.mean(donh * onh, axis=-1, keepdims=True))).astype(BF16)

        @pl.when(i == 0)
        def _():
            dbg_ref[...] = jnp.zeros_like(dbg_ref)
            dgn_ref[...] = jnp.zeros_like(dgn_ref)

        dbg_ref[...] += jnp.sum(dgpre, axis=0, keepdims=True)
        dgn_ref[...] += dgn
        z5b_ref[...] = m["z5b"]
        dgp_ref[...] = dgpb
        ys5b_ref[...] = m["ys5b"]
        dm5b_ref[...] = dm5b
        yglab_ref[...] = m["yglab"]
        dmgb_ref[...] = dmgb
        mergedb_ref[...] = m["mergedb"]
        dh2b_ref[...] = dh2b

    def f32(d):
        return jax.ShapeDtypeStruct((t, d), F32)

    def b16(d):
        return jax.ShapeDtypeStruct((t, d), BF16)

    widths = (512, 512, 512, 1024, 512, 1024, 1024, 1024)
    return _call(
        body, name="mix_post_bwd", grid=(t // tm,),
        in_specs=[_row_tile(tm, 512)] * 3 + [_row_tile(tm, D_MODEL)] * 3
        + [VMEM_FULL, _acc_row(512), _acc_row(512), VMEM_FULL, VMEM_FULL, VMEM_FULL],
        out_specs=[_row_tile(tm, 512)] * 3 + [_row_tile(tm, D_MODEL)] * 2
        + [_acc_row(512)] * 2 + [_row_tile(tm, w) for w in widths],
        out_shape=[f32(512), b16(512), b16(512), b16(D_MODEL), b16(D_MODEL)]
        + [jax.ShapeDtypeStruct((1, 512), F32)] * 2
        + [b16(w) for w in widths],
        args=(y, o, r, gs5, ggla, dh2, wg, bg, gn, ps5t, pglat, wout), carry=carry)


ADAM_TILE_ELEMS = 256 * 1024


def _adamw(w, g, m, v, name):
    rows, cols = w.shape
    tr = rows
    while tr * cols > ADAM_TILE_ELEMS and tr % 16 == 0:
        tr //= 2

    spec = pl.BlockSpec((tr, cols), lambda i: (i, 0))
    sh = jax.ShapeDtypeStruct((rows, cols), F32)
    return pl.pallas_call(functools.partial(_adamw_body), name=name, grid=(rows // tr,), in_specs=[spec] * 4,
                          out_specs=[spec] * 3, out_shape=[sh] * 3, compiler_params=_cparams(1))(w, g, m, v)


def _adamw_math(w, g, m, v):
    nm = ADAM_B1 * m + (1.0 - ADAM_B1) * g
    nv = ADAM_B2 * v + (1.0 - ADAM_B2) * (g * g)
    m_hat = nm / (1.0 - ADAM_B1 ** ADAM_STEP)
    v_hat = nv / (1.0 - ADAM_B2 ** ADAM_STEP)
    return -ADAM_LR * (m_hat / (jnp.sqrt(v_hat) + ADAM_EPS) + ADAM_WD * w), nm, nv


def _adamw_body(w_ref, g_ref, m_ref, v_ref, d_ref, nm_ref, nv_ref):
    d_ref[...], nm_ref[...], nv_ref[...] = _adamw_math(w_ref[...], g_ref[...], m_ref[...], v_ref[...])


SUM_ADAM_ROWS = 32


def _sum_adamw(landed, ws, ms, vs, name, carry=None):
    k = len(ws)
    n = landed[0].shape[0]
    r, c = ws[0].shape
    tr = SUM_ADAM_ROWS

    def body(*refs):
        lands, (w_refs, m_refs, v_refs), outs = refs[:k], (refs[k:2 * k], refs[2 * k:3 * k], refs[3 * k:4 * k]), refs[4 * k:]
        for i in range(k):
            g = lands[i][0].astype(F32)
            for s in range(1, n):
                g = g + lands[i][s].astype(F32)
            outs[i][...] = g
            outs[k + i][...], outs[2 * k + i][...], outs[3 * k + i][...] = _adamw_math(
                w_refs[i][...], g, m_refs[i][...], v_refs[i][...])

    row = pl.BlockSpec((tr, c), lambda i: (i, 0))
    return _call(
        body, name=name, grid=(r // tr,),
        in_specs=[pl.BlockSpec((n, tr, c), lambda i: (0, i, 0))] * k + [row] * (3 * k),
        out_specs=[row] * (4 * k), out_shape=[jax.ShapeDtypeStruct((r, c), F32)] * (4 * k),
        args=(*landed, *ws, *ms, *vs), carry=carry)


def _adamw_many(ws, gs, ms, vs, name):
    n = len(ws)

    def body(*refs):
        ins, outs = refs[:4 * n], refs[4 * n:]
        for i in range(n):
            _adamw_body(*(ins[j * n + i] for j in range(4)), *(outs[j * n + i] for j in range(3)))

    shapes = [jax.ShapeDtypeStruct(w.shape, F32) for w in ws]
    res = pl.pallas_call(body, name=name, in_specs=[VMEM_FULL] * (4 * n), out_specs=[VMEM_FULL] * (3 * n),
                         out_shape=shapes * 3)(*ws, *gs, *ms, *vs)
    return res[:n], res[n:2 * n], res[2 * n:]


def _exchange(carry, name):
    return _call(None, name=name, grid=(1,), in_specs=[], out_specs=[], out_shape=[], args=(), carry=carry)[1]


def _pair_add(slabs, from_pair, name):
    _, r, cols = slabs.shape

    def body(s_ref, p_ref, o_ref):
        c = lax.axis_index("c")
        mine = jnp.where(c == 0, s_ref[0, 0].astype(F32), s_ref[0, 1].astype(F32))
        o_ref[0] = (mine + p_ref[0].astype(F32)).astype(BF16)

    return pl.pallas_call(
        body, name=name, grid=(4,),
        in_specs=[pl.BlockSpec((1, 2, r, cols), lambda i: (i, 0, 0, 0)), pl.BlockSpec((1, r, cols), lambda i: (i, 0, 0))],
        out_specs=pl.BlockSpec((1, r, cols), lambda i: (i, 0, 0)),
        out_shape=jax.ShapeDtypeStruct((4, r, cols), BF16),
        compiler_params=_cparams(1),
    )(slabs.reshape(4, 2, r, cols), from_pair)


def _sum_slabs(slabs, name):
    n = slabs.shape[0]

    def body(s_ref, o_ref):
        acc = s_ref[0].astype(F32)
        for s in range(1, n):
            acc = acc + s_ref[s].astype(F32)
        o_ref[...] = acc

    return pl.pallas_call(
        body, name=name, in_specs=[VMEM_FULL], out_specs=VMEM_FULL,
        out_shape=jax.ShapeDtypeStruct(slabs.shape[1:], F32),
        compiler_params=pltpu.CompilerParams(vmem_limit_bytes=VMEM_LIMIT_BYTES),
    )(slabs)


BIG = ("ffn1_w1", "ffn1_w3", "ffn1_w2", "w_in", "s5_glu_w", "gla_a_up_w", "proj_s5", "proj_gla", "w_out",
       "ffn2_w1", "ffn2_w3", "ffn2_w2")
GROUPS = (("ffn1_w1", "ffn1_w3", "ffn1_w2"),
          ("w_in", "s5_glu_w", "gla_a_up_w", "proj_s5", "proj_gla", "w_out"),
          ("ffn2_w1", "ffn2_w3", "ffn2_w2"))
W_IN_ROWS = 514
W_IN_PAD = 528
UP_COLS = 32
ROW_ADAM = ("ffn1_w1", "ffn1_w3", "w_in", "ffn2_w1", "ffn2_w3")
COL_SHARDED = ("ffn1_w1", "ffn1_w3", "w_in", "proj_s5", "proj_gla", "ffn2_w1", "ffn2_w3")

SMALL = ("ffn1_norm", "mix_norm", "s5_lambda_re", "s5_lambda_im", "s5_log_dt", "s5_b_re", "s5_b_im", "s5_c_re",
         "s5_c_im", "s5_d", "s5_glu_b", "gla_a_up_b", "gla_out_norm", "ffn2_norm", "final_norm")
SMALL_SHAPES = dict(ffn1_norm=(1, 1024), mix_norm=(1, 1024), s5_lambda_re=(1, 32, 64), s5_lambda_im=(1, 32, 64),
                    s5_log_dt=(1, 32), s5_b_re=(1, 32, 64, 16), s5_b_im=(1, 32, 64, 16), s5_c_re=(1, 32, 16, 64),
                    s5_c_im=(1, 32, 16, 64), s5_d=(1, 32, 16), s5_glu_b=(1, 512), gla_a_up_b=(1, 256),
                    gla_out_norm=(1, 512), ffn2_norm=(1, 1024), final_norm=(1024,))
SMALL_N = sum(math.prod(s) for s in SMALL_SHAPES.values())
SMALL_R = -(-SMALL_N // (64 * 1024)) * 64


def _shard_rows(name, a):
    if name == "gla_a_up_w":
        return jnp.pad(a, ((0, 0), (0, 128 - UP_COLS)))
    if name in COL_SHARDED:
        a = a.T
    if name == "w_in":
        return jnp.pad(a, ((0, W_IN_PAD - W_IN_ROWS), (0, 0)))
    return a.reshape(-1, 1024)


def _unshard_rows(name, rows, shape):
    if name == "gla_a_up_w":
        return rows[:, :UP_COLS]
    if name == "w_in":
        rows = rows[:W_IN_ROWS]
    if name in COL_SHARDED:
        return rows.reshape(shape[1], shape[0]).T
    return rows.reshape(shape)


def _pack_small(vals, loss):
    flat = jnp.concatenate([vals[n].reshape(-1).astype(F32) for n in SMALL] + [loss.reshape(1)])
    return jnp.pad(flat, (0, SMALL_R * 1024 - SMALL_N - 1)).reshape(SMALL_R, 1024)


S5_B = ("s5_b_re", "s5_b_im")


def _working(name, a):
    return a[0].transpose(0, 2, 1) if name in S5_B else a


def _declared(name, a):
    return a.transpose(0, 2, 1)[None] if name in S5_B else a.reshape(SMALL_SHAPES[name])


def _unpack_small(slab):
    flat = slab.reshape(-1)
    out, off = {}, 0
    for n in SMALL:
        size = math.prod(SMALL_SHAPES[n])
        shape = (S5_GROUPS, S5_GROUP, S5_STATE) if n in S5_B else SMALL_SHAPES[n]
        out[n] = flat[off:off + size].reshape(shape)
        off += size
    return out


FULL_SHAPES = dict(w_in=(IN_COLS, D_MODEL), s5_glu_w=(S5_WIDTH, S5_WIDTH), gla_a_up_w=(GLA_RANK, GLA_KEY),
                   proj_s5=(D_MODEL, S5_WIDTH), proj_gla=(D_MODEL, GLA_VAL), w_out=(D_MODEL, D_MODEL))


def _full_weight(name, gathered):
    if name == "gla_a_up_w":
        return gathered[:, :, :UP_COLS].transpose(1, 0, 2).reshape(GLA_RANK, GLA_KEY)
    if name == "w_in":
        gathered = gathered[:, :W_IN_ROWS]
    return gathered.reshape(FULL_SHAPES.get(name, (D_FF, D_MODEL)))


def _grad_slabs(name, g):
    if name == "gla_a_up_w":
        g = g.reshape(GLA_RANK, N_DEV, UP_COLS).transpose(1, 0, 2)
        return jnp.pad(g, ((0, 0), (0, 0), (0, 128 - UP_COLS))).astype(BF16)
    if name == "w_in":
        return jnp.pad(g.reshape(N_DEV, W_IN_ROWS, D_MODEL), ((0, 0), (0, W_IN_PAD - W_IN_ROWS), (0, 0)))
    return g.reshape(N_DEV, -1, 1024)


def _s5_dense(re, im, sign_im):
    eye = jnp.eye(8, dtype=F32)

    def one(a):
        a = a.reshape(S5_BLOCKS, 8, S5_GROUP, S5_STATE)
        return jnp.einsum("cghp,gk->cghkp", a, eye).reshape(S5_BLOCKS, 128, S5_BSTATE)

    return jnp.concatenate([one(re), sign_im * one(im)], axis=-1)


def _s5_undense(d):
    eye = jnp.eye(8, dtype=F32)

    def one(a):
        a = a.reshape(S5_BLOCKS, 8, S5_GROUP, 8, S5_STATE)
        return jnp.einsum("cghkp,gk->cghp", a, eye).reshape(S5_GROUPS, S5_GROUP, S5_STATE)

    return one(d[..., :S5_BSTATE]), one(d[..., S5_BSTATE:])


def _local_step(x, target, p, w, rows=None, opt=None):
    w = dict(w or {})
    landed_grads = {}

    def gather(names):
        return None if rows is None else ([rows[n] for n in names], "gather")

    def gathered(names, landed):
        w.update({n: _full_weight(n, g) for n, g in zip(names, landed)})

    def scatter(names):
        return None if rows is None else ([_grad_slabs(n, big[n]) for n in names], "scatter")

    def scattered(names, landed):
        landed_grads.update(zip(names, landed))

    if rows is not None:
        gathered(GROUPS[0], _exchange(gather(GROUPS[0]), "gather_ffn1"))
    g1, gm, g2 = p["ffn1_norm"], p["mix_norm"], p["ffn2_norm"]
    gf = p["final_norm"].reshape(1, D_MODEL)
    lre, lim = p["s5_lambda_re"][0], p["s5_lambda_im"][0]
    ldt = p["s5_log_dt"][0].reshape(S5_GROUPS, 1)
    bre = p["s5_b_re"][0].transpose(0, 2, 1)
    bim = p["s5_b_im"][0].transpose(0, 2, 1)
    cre, cim = p["s5_c_re"][0], p["s5_c_im"][0]
    dskip = p["s5_d"][0].reshape(1, S5_WIDTH)
    bg, bup, gn = p["s5_glu_b"], p["gla_a_up_b"], p["gla_out_norm"]

    mix_first, mix_rest = ("w_in", "gla_a_up_w"), ("s5_glu_w", "proj_s5", "proj_gla", "w_out")
    h1, (a3_1, b3_1, n1), got = _ffn_fwd(x, g1, w["ffn1_w1"], w["ffn1_w3"], w["ffn1_w2"], "ffn1_fwd",
                                         gather(mix_first + mix_rest))
    gathered(mix_first + mix_rest, got)
    wup = w["gla_a_up_w"].astype(F32)
    (u, s5in, q, k, v, r, alow, gs5, ggla), _ = _mix_pre_fwd(h1, gm, w["w_in"])
    ar, ai, bbr, bbi = _s5_disc(lre, lim, ldt, bre, bim)
    bd = _s5_dense(bbr, bbi, 1.0)
    cd = _s5_dense(cre, cim, -1.0)
    bd16, cd16 = bd.astype(BF16), cd.astype(BF16)
    bdt16, ctd16 = bd16.transpose(0, 2, 1), cd16.transpose(0, 2, 1)
    ar4 = ar.reshape(S5_BLOCKS, 1, S5_BSTATE)
    ai4 = ai.reshape(S5_BLOCKS, 1, S5_BSTATE)
    (xs, y), got = _s5_fwd(s5in, bd16, ctd16, ar4, ai4, dskip, gather(GROUPS[2][:2]))
    gathered(GROUPS[2][:2], got)
    (o, ssave), _ = _gla_fwd(q, k, v, alow, wup, bup)
    post_w = (w["s5_glu_w"], bg, gn, w["proj_s5"], w["proj_gla"], w["w_out"])
    h2, got = _mix_post_fwd(y, o, r, gs5, ggla, h1, *post_w, carry=gather(GROUPS[2][2:]))
    gathered(GROUPS[2][2:], got)
    loss, dh3, dgf, (a3_2, b3_2, n2) = _ffn_fwd_head(h2, g2, w["ffn2_w1"], w["ffn2_w3"], w["ffn2_w2"], gf, target,
                                                     "ffn2_fwd")

    big, small = {}, {}
    small["final_norm"] = dgf.reshape(D_MODEL)
    (dh2, dg2, da3, db3, s3, dhh2), _ = _ffn_bwd(
        h2, dh3, g2, a3_2, b3_2, w["ffn2_w1"], w["ffn2_w3"], w["ffn2_w2"], "ffn2_bwd")
    small["ffn2_norm"] = dg2
    big["ffn2_w1"] = _mm_tn(da3, n2, "ffn2_dw1")
    big["ffn2_w3"] = _mm_tn(db3, n2, "ffn2_dw3")
    big["ffn2_w2"] = _mm_tn(s3, dhh2, "ffn2_dw2")
    (dy, do, dr, dgs5, dggla, dbg, dgn, z5b, dgpb, ys5b, dm5b, yglab, dmgb, mergedb, dh2b), got = _mix_post_bwd(
        y, o, r, gs5, ggla, dh2, *post_w, carry=scatter(GROUPS[2][:1]))
    scattered(GROUPS[2][:1], got)
    small["s5_glu_b"] = dbg
    small["gla_out_norm"] = dgn
    big["s5_glu_w"] = _mm_tn(z5b, dgpb, "glu_dw")
    big["proj_s5"] = _mm_tn(dm5b, ys5b, "proj_s5_dw")
    big["proj_gla"] = _mm_tn(dmgb, yglab, "proj_gla_dw")
    big["w_out"] = _mm_tn(mergedb, dh2b, "w_out_dw")
    (dq, dk, dv, dalow, dwup, dbup), got = _gla_bwd(q, k, v, alow, wup, bup, ssave, do, scatter(GROUPS[2][1:2]))
    scattered(GROUPS[2][1:2], got)
    big["gla_a_up_w"] = dwup
    small["gla_a_up_b"] = dbup
    (ds5in, dbd, dcd, dd, dar4, dai4), got = _s5_bwd(
        dy, s5in, xs, cd16, bdt16, ar4, ai4, dskip, scatter(GROUPS[2][2:]))
    scattered(GROUPS[2][2:], got)
    dbbr, dbbi = _s5_undense(dbd)
    dcre, dcim_neg = _s5_undense(dcd)
    glre, glim, gldt, gbre, gbim = _s5_disc_bwd(
        lre, lim, ldt, bre, bim, dar4.reshape(S5_GROUPS, S5_STATE), dai4.reshape(S5_GROUPS, S5_STATE),
        dbbr, dbbi)
    small["s5_lambda_re"] = glre[None]
    small["s5_lambda_im"] = glim[None]
    small["s5_log_dt"] = gldt.reshape(1, S5_GROUPS)
    small["s5_b_re"] = gbre
    small["s5_b_im"] = gbim
    small["s5_c_re"] = dcre[None]
    small["s5_c_im"] = -dcim_neg[None]
    small["s5_d"] = dd.reshape(1, S5_GROUPS, S5_GROUP)
    dz = (ds5in, dq, dk, dv, dr, dalow, dgs5, dggla)
    (dh1, dgm), got = _mix_pre_bwd(h1, gm, w["w_in"], dh2, dz, scatter(mix_rest))
    scattered(mix_rest, got)
    small["mix_norm"] = dgm
    wide = _mm_tn_many(dz[:5] + dz[6:], u, "w_in_dw")
    low_at = IN_OFFS[5]
    big["w_in"] = jnp.concatenate([wide[:low_at], _mm_tn(dalow, u, "w_in_dw_low"), wide[low_at:]], axis=0)
    (dx, dg1, da3, db3, s3, dhh1), got = _ffn_bwd(
        x, dh1, g1, a3_1, b3_1, w["ffn1_w1"], w["ffn1_w3"], w["ffn1_w2"], "ffn1_bwd",
        scatter(mix_first))
    scattered(mix_first, got)
    small["ffn1_norm"] = dg1
    if rows is None:
        big["ffn1_w1"] = _mm_tn(da3, n1, "ffn1_dw1")
        big["ffn1_w3"] = _mm_tn(db3, n1, "ffn1_dw3")
        big["ffn1_w2"] = _mm_tn(s3, dhh1, "ffn1_dw2")
        return loss[0, 0], dx, big, small
    part = _pack_small(small, loss).reshape(N_DEV, SMALL_R // N_DEV, 1024)
    big["ffn1_w1"], (small_landed,) = _mm_tn(da3, n1, "ffn1_dw1", ([part], "scatter"))
    small_mine = _sum_slabs(small_landed, "sum_small")
    slabs1 = _grad_slabs("ffn1_w1", big["ffn1_w1"])
    big["ffn1_w3"], (from_pair, small_all) = _mm_tn(db3, n1, "ffn1_dw3",
                                                    [([slabs1], "pair"), ([small_mine], "gather")])
    small = small_all.reshape(SMALL_R, 1024)
    sums1 = _pair_add(slabs1, from_pair, "ffn1_w1_pair")
    slabs3 = _grad_slabs("ffn1_w3", big["ffn1_w3"])
    big["ffn1_w2"], (landed1, from_pair) = _mm_tn(s3, dhh1, "ffn1_dw2", [([sums1], "chips"), ([slabs3], "pair")])
    sums3 = _pair_add(slabs3, from_pair, "ffn1_w3_pair")
    slabs2 = _grad_slabs("ffn1_w2", big["ffn1_w2"])

    def sum_adamw(names, lands, name, carry=None):
        outs, got = _sum_adamw(lands, *([opt[n][j] for n in names] for j in range(3)), name, carry)
        for i, n in enumerate(names):
            updated[n] = outs[i::len(names)]
        return got

    updated = {}
    landed3, from_pair = sum_adamw(GROUPS[2], [landed_grads.pop(n) for n in GROUPS[2]], "adamw_ffn2",
                                   [([sums3], "chips"), ([slabs2], "pair")])
    sums2 = _pair_add(slabs2, from_pair, "ffn1_w2_pair")
    (landed2,) = _exchange(([sums2], "chips"), "scatter_ffn1_b")
    sum_adamw(GROUPS[0], [landed1, landed3, landed2], "adamw_ffn1")
    return loss[0, 0], dx, landed_grads, small, updated


NAMES = ("ffn1_norm", "ffn1_w1", "ffn1_w3", "ffn1_w2", "mix_norm", "w_in", "s5_lambda_re", "s5_lambda_im",
         "s5_log_dt", "s5_b_re", "s5_b_im", "s5_c_re", "s5_c_im", "s5_d", "s5_glu_w", "s5_glu_b", "gla_a_up_w",
         "gla_a_up_b", "gla_out_norm", "proj_s5", "proj_gla", "w_out", "ffn2_norm", "ffn2_w1", "ffn2_w3", "ffn2_w2",
         "final_norm")


def kernel(*args):
    nw = len(NAMES)
    x = args[0][0]
    wts = dict(zip(NAMES, args[1:1 + nw]))
    target = args[1 + nw][0]
    mom = dict(zip(NAMES, args[2 + nw:2 + 2 * nw]))
    var = dict(zip(NAMES, args[2 + 2 * nw:2 + 3 * nw]))

    shards = {n: wts[n][0] for n in BIG}
    rows = {n: _shard_rows(n, shards[n]).astype(BF16) for n in BIG}
    def row_layout(n, a):
        return a.T if n in ROW_ADAM else a

    opt = {n: tuple(row_layout(n, d[n][0]) for d in (wts, mom, var)) for n in GROUPS[0] + GROUPS[2]}
    _, dx, landed, small_slab, updated = _local_step(x, target, {n: wts[n] for n in SMALL}, None, rows, opt)
    loss = small_slab.reshape(-1)[SMALL_N]
    g_small = _unpack_small(small_slab)

    grad, delta, new_m, new_v = {}, {}, {}, {}
    for n, arrays in updated.items():
        grad[n], delta[n], new_m[n], new_v[n] = (row_layout(n, a)[None] for a in arrays)
    for n in GROUPS[1]:
        g_rows = _sum_slabs(landed[n], "sum_" + n)
        if n in ROW_ADAM:
            g = g_rows[:W_IN_ROWS] if n == "w_in" else g_rows
            outs = _adamw(shards[n].T, g, mom[n][0].T, var[n][0].T, "adamw_" + n)
            grad[n], delta[n], new_m[n], new_v[n] = (a.T[None] for a in (g, *outs))
        else:
            g = _unshard_rows(n, g_rows, shards[n].shape)
            outs = _adamw(shards[n], g, mom[n][0], var[n][0], "adamw_" + n)
            grad[n], delta[n], new_m[n], new_v[n] = (a[None] for a in (g, *outs))

    def flat2d(a):
        return a.reshape(-1, a.shape[-1])

    operands = ([flat2d(_working(n, d[n])) for n in SMALL] for d in (wts, mom, var))
    w2d, m2d, v2d = operands
    outs = _adamw_many(w2d, [flat2d(g_small[n]) for n in SMALL], m2d, v2d, "adamw_small")
    for out, arrays in zip((grad, delta, new_m, new_v), ([g_small[n] for n in SMALL], *outs)):
        out.update({n: _declared(n, a.reshape(g_small[n].shape)) for n, a in zip(SMALL, arrays)})
    return (loss, dx[None], *(d[n] for d in (grad, delta, new_m, new_v) for n in NAMES))
```

```python
import functools
import math

import jax
import jax.numpy as jnp
from jax import lax
from jax.experimental import pallas as pl
from jax.experimental.pallas import tpu as pltpu

F32, BF16 = jnp.float32, jnp.bfloat16

D_MODEL = 1024
D_FF = 2816
N_DEV = 8
S5_WIDTH, S5_GROUPS, S5_GROUP, S5_STATE = 512, 32, 16, 64
S5_BLOCKS = 4
S5_BSTATE = 512
S5_SEGS = 8
GLA_HEADS, GLA_DK, GLA_DV = 4, 64, 128
GLA_KEY, GLA_VAL, GLA_RANK, GLA_CHUNK = 256, 512, 16, 64
GLA_TAU = 16.0
GLA_STEP_CHUNKS = 4
EPS = 1e-6
IN_SIZES = (512, 256, 256, 512, 512, 16, 1024, 1024)
IN_OFFS = tuple(sum(IN_SIZES[:i]) for i in range(len(IN_SIZES)))
IN_COLS = sum(IN_SIZES)
ADAM_LR, ADAM_B1, ADAM_B2, ADAM_EPS, ADAM_WD, ADAM_STEP = 0.001, 0.9, 0.999, 1e-08, 0.01, 10
GELU_C0 = math.sqrt(2.0 / math.pi)
GELU_C1 = 0.044715

FFN_FT = 256
VMEM_LIMIT_BYTES = 56 * 1024 * 1024

VMEM_FULL = pl.BlockSpec(memory_space=pltpu.VMEM)
ANY = pl.BlockSpec(memory_space=pl.ANY)


def _cparams(n_grid):
    return pltpu.CompilerParams(dimension_semantics=("arbitrary",) * n_grid, vmem_limit_bytes=VMEM_LIMIT_BYTES)


def _tile(t):
    return 512 if t >= 1024 else t // 2


def _nn(a, b):
    return jnp.dot(a, b, preferred_element_type=F32)


def _nt(a, b):
    return lax.dot_general(a, b, (((1,), (1,)), ((), ())), preferred_element_type=F32)


def _tn(a, b):
    return lax.dot_general(a, b, (((0,), (0,)), ((), ())), preferred_element_type=F32)


def _rms_parts(x):
    r = lax.rsqrt(jnp.mean(x * x, axis=-1, keepdims=True) + EPS)
    return x * r, r


def _rms_bwd(dn, g, xhat, r):
    dxh = dn * g
    dx = r * (dxh - xhat * jnp.mean(dxh * xhat, axis=-1, keepdims=True))
    return dx, jnp.sum(dn * xhat, axis=0, keepdims=True)


def _peers():
    x, y, c = lax.axis_index("x"), lax.axis_index("y"), lax.axis_index("c")
    out = []
    for k in range(1, N_DEV):
        px = 1 - x if k & 4 else x
        py = 1 - y if k & 2 else y
        pc = 1 - c if k & 1 else c
        out.append(((px, py, pc), 4 * px + 2 * py + pc))
    return 4 * x + 2 * y + c, out


def _exchange_copies(src_refs, out_refs, send_sems, recv_sems, local_sems, scatter, with_recvs):
    me, peers = _peers()
    locals_, sends, recvs = [], [], []
    for a, (src_ref, out_ref) in enumerate(zip(src_refs, out_refs)):
        def mine(idx, src_ref=src_ref):
            return src_ref.at[idx] if scatter else src_ref

        locals_.append(pltpu.make_async_copy(mine(me), out_ref.at[me], local_sems.at[a]))
        for k, (dev, idx) in enumerate(peers):
            sends.append(pltpu.make_async_remote_copy(
                src_ref=mine(idx), dst_ref=out_ref.at[me], send_sem=send_sems.at[a, k], recv_sem=recv_sems.at[a, k],
                device_id=dev, device_id_type=pl.DeviceIdType.MESH))
            if with_recvs:
                recvs.append(pltpu.make_async_remote_copy(
                    src_ref=mine(idx), dst_ref=out_ref.at[idx], send_sem=send_sems.at[a, k],
                    recv_sem=recv_sems.at[a, k], device_id=dev, device_id_type=pl.DeviceIdType.MESH))
    return locals_, sends, recvs


def _remote(src, dst, send_sems, recv_sems, a, k, dev):
    return pltpu.make_async_remote_copy(src_ref=src, dst_ref=dst, send_sem=send_sems.at[a, k],
                                        recv_sem=recv_sems.at[a, k], device_id=dev,
                                        device_id_type=pl.DeviceIdType.MESH)


def _gather_places():
    x, y, c = lax.axis_index("x"), lax.axis_index("y"), lax.axis_index("c")

    def at(chip, core):
        return 4 * chip[0] + 2 * chip[1] + core

    xn, yn, diag = (1 - x, y), (x, 1 - y), (1 - x, 1 - y)
    relay = (x * (1 - c) + (1 - x) * c, (1 - y) * (1 - c) + y * c)
    passed = ((1 - x) * (1 - c) + x * c, y * (1 - c) + (1 - y) * c)
    return dict(sibling=(x, y, 1 - c), me=at((x, y), c), sib=at((x, y), 1 - c), c=c, at=at,
                xn=xn, yn=yn, diag=diag, relay=relay, passed=passed)


def _gather_start(src_refs, out_refs, send_sems, recv_sems, local_sems):
    p = _gather_places()
    for a, (src, out) in enumerate(zip(src_refs, out_refs)):
        mine = out.at[p["me"]]
        pltpu.make_async_copy(src, mine, local_sems.at[a]).start()
        _remote(src, mine, send_sems, recv_sems, a, 0, p["sibling"]).start()
        _remote(src, mine, send_sems, recv_sems, a, 1, (*p["xn"], p["c"])).start()
        _remote(src, mine, send_sems, recv_sems, a, 2, (*p["yn"], p["c"])).start()


def _gather_forward(src_refs, out_refs, send_sems, recv_sems, local_sems):
    p = _gather_places()
    c, at = p["c"], p["at"]
    for a, (src, out) in enumerate(zip(src_refs, out_refs)):
        from_x, from_y = out.at[at(p["xn"], c)], out.at[at(p["yn"], c)]
        _remote(src, from_x, send_sems, recv_sems, a, 1, (*p["xn"], c)).wait_recv()
        _remote(src, from_y, send_sems, recv_sems, a, 2, (*p["yn"], c)).wait_recv()
        relayed = out.at[at(p["passed"], c)]
        _remote(relayed, relayed, send_sems, recv_sems, a, 3, (*p["relay"], c)).start()
        _remote(from_x, from_x, send_sems, recv_sems, a, 4, p["sibling"]).start()
        _remote(from_y, from_y, send_sems, recv_sems, a, 5, p["sibling"]).start()


def _gather_diagonal(src_refs, out_refs, send_sems, recv_sems, local_sems):
    p = _gather_places()
    for a, (src, out) in enumerate(zip(src_refs, out_refs)):
        from_diag = out.at[p["at"](p["diag"], p["c"])]
        _remote(src, from_diag, send_sems, recv_sems, a, 3, (*p["relay"], p["c"])).wait_recv()
        _remote(from_diag, from_diag, send_sems, recv_sems, a, 6, p["sibling"]).start()


def _gather_finish(src_refs, out_refs, send_sems, recv_sems, local_sems):
    p = _gather_places()
    c, at, sibling = p["c"], p["at"], p["sibling"]
    arrays = list(enumerate(zip(src_refs, out_refs)))
    for a, (src, out) in arrays:
        _remote(src, out.at[p["sib"]], send_sems, recv_sems, a, 0, sibling).wait_recv()
        for k, chip in ((4, p["xn"]), (5, p["yn"]), (6, p["diag"])):
            _remote(src, out.at[at(chip, 1 - c)], send_sems, recv_sems, a, k, sibling).wait_recv()
        for k in range(N_DEV - 1):
            _remote(src, out.at[p["me"]], send_sems, recv_sems, a, k, sibling).wait_send()
        pltpu.make_async_copy(src, out.at[p["me"]], local_sems.at[a]).wait()


def _exchange_start(*refs, scatter):
    locals_, sends, _ = _exchange_copies(*refs, scatter=scatter, with_recvs=False)
    for cp in locals_ + sends:
        cp.start()


def _exchange_wait(*refs, scatter):
    locals_, sends, recvs = _exchange_copies(*refs, scatter=scatter, with_recvs=True)
    for cp in recvs:
        cp.wait_recv()
    for cp in sends:
        cp.wait_send()
    for cp in locals_:
        cp.wait()


def _halves_places():
    x, y, c = lax.axis_index("x"), lax.axis_index("y"), lax.axis_index("c")
    flips = [(1 - x, y), (x, 1 - y), (1 - x, 1 - y)]
    return (x, y, 1 - c), c, 2 * x + y, [((fx, fy, c), 2 * fx + fy) for fx, fy in flips]


def _pair_start(src_refs, out_refs, send_sems, recv_sems, local_sems):
    sibling, c, _, _ = _halves_places()
    for a, (src, out) in enumerate(zip(src_refs, out_refs)):
        for i in range(4):
            _remote(src.at[2 * i + 1 - c], out.at[i], send_sems, recv_sems, a, i, sibling).start()


def _pair_finish(src_refs, out_refs, send_sems, recv_sems, local_sems):
    sibling, c, _, _ = _halves_places()
    for a, (src, out) in enumerate(zip(src_refs, out_refs)):
        for i in range(4):
            _remote(src.at[2 * i + 1 - c], out.at[i], send_sems, recv_sems, a, i, sibling).wait()


def _chips_start(src_refs, out_refs, send_sems, recv_sems, local_sems):
    _, _, chip, others = _halves_places()
    for a, (src, out) in enumerate(zip(src_refs, out_refs)):
        pltpu.make_async_copy(src.at[chip], out.at[chip], local_sems.at[a]).start()
        for k, (dev, their_chip) in enumerate(others):
            _remote(src.at[their_chip], out.at[chip], send_sems, recv_sems, a, k, dev).start()


def _chips_finish(src_refs, out_refs, send_sems, recv_sems, local_sems):
    _, _, chip, others = _halves_places()
    for a, (src, out) in enumerate(zip(src_refs, out_refs)):
        for k, (dev, their_chip) in enumerate(others):
            _remote(src.at[their_chip], out.at[their_chip], send_sems, recv_sems, a, k, dev).wait_recv()
        for k, (dev, their_chip) in enumerate(others):
            _remote(src.at[their_chip], out.at[chip], send_sems, recv_sems, a, k, dev).wait_send()
        pltpu.make_async_copy(src.at[chip], out.at[chip], local_sems.at[a]).wait()


EXCHANGES = {
    "gather": (_gather_start, _gather_forward, _gather_diagonal, _gather_finish, N_DEV, False),
    "scatter": (functools.partial(_exchange_start, scatter=True), None, None,
                functools.partial(_exchange_wait, scatter=True), N_DEV, True),
    "pair": (_pair_start, None, None, _pair_finish, 4, True),
    "chips": (_chips_start, None, None, _chips_finish, 4, True),
}


def _exchange_sems(n_arrays):
    return [pltpu.SemaphoreType.DMA((n_arrays, N_DEV - 1)), pltpu.SemaphoreType.DMA((n_arrays, N_DEV - 1)),
            pltpu.SemaphoreType.DMA((n_arrays,))]


def _exchange_shapes(srcs, kind):
    lead, slabbed = EXCHANGES[kind][4:]
    return [jax.ShapeDtypeStruct((lead,) + tuple(s.shape[1:] if slabbed else s.shape), s.dtype) for s in srcs]


def _carries(carry):
    if carry is None:
        return []
    return [carry] if isinstance(carry, tuple) else list(carry)


def _call(body, *, name, grid, in_specs, out_specs, out_shape, args, scratch_shapes=(), carry=None):
    n_in, n_out, n_scr = len(in_specs), len(out_specs), len(scratch_shapes)
    groups = _carries(carry)
    sizes = [len(arrays) for arrays, _ in groups]
    nc = sum(sizes)

    def wrapped(*refs):
        ins, refs = refs[:n_in], refs[n_in:]
        csrc, refs = refs[:nc], refs[nc:]
        outs, refs = refs[:n_out], refs[n_out:]
        cland, refs = refs[:nc], refs[nc:]
        scr, sems = refs[:n_scr], refs[n_scr:]

        def run(phase):
            at = 0
            for gi, ((_, kind), size) in enumerate(zip(groups, sizes)):
                if EXCHANGES[kind][phase] is not None:
                    EXCHANGES[kind][phase](csrc[at:at + size], cland[at:at + size], *sems[3 * gi:3 * gi + 3])
                at += size

        last = pl.program_id(0) == grid[0] - 1
        if nc:
            pl.when(pl.program_id(0) == 0)(functools.partial(run, 0))
            pl.when(pl.program_id(0) == max(grid[0] - 2, 0))(functools.partial(run, 1))
            pl.when(last)(functools.partial(run, 2))
        if body is not None:
            body(*ins, *outs, *scr)
        if nc:
            pl.when(last)(functools.partial(run, 3))

    res = pl.pallas_call(
        wrapped, name=name, grid=grid,
        in_specs=list(in_specs) + [ANY] * nc, out_specs=list(out_specs) + [ANY] * nc,
        out_shape=list(out_shape) + [s for arrays, kind in groups for s in _exchange_shapes(arrays, kind)],
        scratch_shapes=list(scratch_shapes) + [s for size in sizes for s in _exchange_sems(size)],
        compiler_params=_cparams(1),
    )(*args, *[a for arrays, _ in groups for a in arrays])
    return res[:n_out], res[n_out:]


def _row_tile(tm, d):
    return pl.BlockSpec((tm, d), lambda i: (i, 0))


def _acc_row(d):
    return pl.BlockSpec((1, d), lambda i: (0, 0))


def _ffn_body(x_ref, g_ref, w1_ref, w3_ref, w2_ref, acc_ref, a_ref, b_ref, n_ref):
    xv = x_ref[...]
    xhat, _ = _rms_parts(xv)
    n = (xhat * g_ref[...]).astype(BF16)
    n_ref[...] = n
    acc_ref[...] = xv

    def fstep(f, c):
        rows = pl.ds(pl.multiple_of(f * FFN_FT, FFN_FT), FFN_FT)
        a = _nt(n, w1_ref[rows, :])
        b = _nt(n, w3_ref[rows, :])
        a_ref[f] = a.astype(BF16)
        b_ref[f] = b.astype(BF16)
        s = (a * jax.nn.sigmoid(a) * b).astype(BF16)
        acc_ref[...] += 0.5 * _nn(s, w2_ref[rows, :])
        return c

    lax.fori_loop(0, D_FF // FFN_FT, fstep, 0, unroll=True)


def _ffn_fwd(x, g, w1t, w3t, w2, name, carry=None):
    t = x.shape[0]
    tm = _tile(t)
    nf = D_FF // FFN_FT
    blk3 = pl.BlockSpec((nf, tm, FFN_FT), lambda i: (0, i, 0))
    sh3 = jax.ShapeDtypeStruct((nf, t, FFN_FT), BF16)
    (h, a3, b3, n), landed = _call(
        functools.partial(_ffn_body), name=name, grid=(t // tm,),
        in_specs=[_row_tile(tm, D_MODEL), _acc_row(D_MODEL), VMEM_FULL, VMEM_FULL, VMEM_FULL],
        out_specs=[_row_tile(tm, D_MODEL), blk3, blk3, _row_tile(tm, D_MODEL)],
        out_shape=[jax.ShapeDtypeStruct((t, D_MODEL), F32), sh3, sh3, jax.ShapeDtypeStruct((t, D_MODEL), BF16)],
        args=(x, g, w1t, w3t, w2), carry=carry)
    return h, (a3, b3, n), landed


def _ffn_fwd_head(x, g, w1t, w3t, w2, gf, target, name):
    t = x.shape[0]
    tm = _tile(t)
    nf = D_FF // FFN_FT

    def body(x_ref, g_ref, w1_ref, w3_ref, w2_ref, gf_ref, t_ref, loss_ref, dh_ref, dgf_ref, a_ref, b_ref, n_ref, acc):
        _ffn_body(x_ref, g_ref, w1_ref, w3_ref, w2_ref, acc, a_ref, b_ref, n_ref)
        _head_math(acc[...], gf_ref[...], t_ref[...], loss_ref, dh_ref, dgf_ref)

    blk3 = pl.BlockSpec((nf, tm, FFN_FT), lambda i: (0, i, 0))
    sh3 = jax.ShapeDtypeStruct((nf, t, FFN_FT), BF16)
    (loss, dh, dgf, a3, b3, n), _ = _call(
        body, name=name, grid=(t // tm,),
        in_specs=[_row_tile(tm, D_MODEL), _acc_row(D_MODEL), VMEM_FULL, VMEM_FULL, VMEM_FULL, _acc_row(D_MODEL),
                  _row_tile(tm, D_MODEL)],
        out_specs=[pl.BlockSpec((1, 1), lambda i: (0, 0)), _row_tile(tm, D_MODEL), _acc_row(D_MODEL), blk3, blk3,
                   _row_tile(tm, D_MODEL)],
        out_shape=[jax.ShapeDtypeStruct((1, 1), F32), jax.ShapeDtypeStruct((t, D_MODEL), F32),
                   jax.ShapeDtypeStruct((1, D_MODEL), F32), sh3, sh3, jax.ShapeDtypeStruct((t, D_MODEL), BF16)],
        scratch_shapes=[pltpu.VMEM((tm, D_MODEL), F32)],
        args=(x, g, w1t, w3t, w2, gf, target))
    return loss, dh, dgf, (a3, b3, n)


def _head_math(h, gv, target, loss_ref, dh_ref, dg_ref):
    i = pl.program_id(0)
    xhat, r = _rms_parts(h)
    err = xhat * gv - target
    dx, dg = _rms_bwd(err * (1.0 / D_MODEL), gv, xhat, r)
    dh_ref[...] = dx

    @pl.when(i == 0)
    def _():
        loss_ref[...] = jnp.zeros_like(loss_ref)
        dg_ref[...] = jnp.zeros_like(dg_ref)

    loss_ref[...] += (0.5 / D_MODEL) * jnp.sum(jnp.sum(err * err, axis=1, keepdims=True), axis=0, keepdims=True)
    dg_ref[...] += dg


def _ffn_bwd(x, dh, g, a3, b3, w1t, w3t, w2, name, carry=None):
    t = x.shape[0]
    tm = _tile(t) // 2
    nf = D_FF // FFN_FT

    def body(x_ref, dh_ref, g_ref, a_ref, b_ref, w1_ref, w3_ref, w2_ref,
             dx_ref, dg_ref, da_ref, db_ref, s_ref, dhh_ref, dn_acc):
        i = pl.program_id(0)
        xv = x_ref[...]
        gv = g_ref[...]
        xhat, r = _rms_parts(xv)
        dhv = dh_ref[...]
        dhh = (0.5 * dhv).astype(BF16)
        dhh_ref[...] = dhh
        dn_acc[...] = jnp.zeros_like(dn_acc)

        def fstep(f, c):
            rows = pl.ds(f * FFN_FT, FFN_FT)
            w1c, w3c, w2c = w1_ref[rows, :], w3_ref[rows, :], w2_ref[rows, :]
            a = a_ref[f].astype(F32)
            b = b_ref[f].astype(F32)
            sg = jax.nn.sigmoid(a)
            sl = a * sg
            ds = _nt(dhh, w2c)
            da = (ds * b * sg * (1.0 + a * (1.0 - sg))).astype(BF16)
            db = (ds * sl).astype(BF16)
            s_ref[f] = (sl * b).astype(BF16)
            da_ref[f] = da
            db_ref[f] = db
            return c

        def nstep(f, c):
            rows = pl.ds(f * FFN_FT, FFN_FT)
            dn_acc[...] += _nn(da_ref[f], w1_ref[rows, :]) + _nn(db_ref[f], w3_ref[rows, :])
            return c

        for f in range(nf + 1):
            if f < nf:
                fstep(f, 0)
            if f:
                nstep(f - 1, 0)
        dx, dg = _rms_bwd(dn_acc[...], gv, xhat, r)
        dx_ref[...] = dhv + dx

        @pl.when(i == 0)
        def _():
            dg_ref[...] = jnp.zeros_like(dg_ref)

        dg_ref[...] += dg

    blk3 = pl.BlockSpec((nf, tm, FFN_FT), lambda i: (0, i, 0))
    sh3 = jax.ShapeDtypeStruct((nf, t, FFN_FT), BF16)
    return _call(
        body, name=name, grid=(t // tm,),
        in_specs=[_row_tile(tm, D_MODEL), _row_tile(tm, D_MODEL), _acc_row(D_MODEL), blk3, blk3,
                  VMEM_FULL, VMEM_FULL, VMEM_FULL],
        out_specs=[_row_tile(tm, D_MODEL), _acc_row(D_MODEL), blk3, blk3, blk3, _row_tile(tm, D_MODEL)],
        out_shape=[jax.ShapeDtypeStruct((t, D_MODEL), F32), jax.ShapeDtypeStruct((1, D_MODEL), F32), sh3, sh3, sh3,
                   jax.ShapeDtypeStruct((t, D_MODEL), BF16)],
        scratch_shapes=[pltpu.VMEM((tm, D_MODEL), F32)],
        args=(x, dh, g, a3, b3, w1t, w3t, w2), carry=carry)


def _mm_tn(a, b, name, carry=None):
    t, n = b.shape
    kc = min(512, t)
    if a.ndim == 3:
        nb, _, tb = a.shape
        a_spec = pl.BlockSpec((1, t, tb), lambda i: (i, 0, 0))
    else:
        m = a.shape[1]
        tb = min(m, 256)
        nb = m // tb
        a_spec = pl.BlockSpec((t, tb), lambda i: (0, i))
    three_d = a.ndim == 3

    def body(a_ref, b_ref, o_ref, acc):
        acc[...] = jnp.zeros_like(acc)

        def kstep(k, c):
            rows = pl.ds(pl.multiple_of(k * kc, kc), kc)
            av = a_ref[0, rows, :] if three_d else a_ref[rows, :]
            acc[...] += _tn(av.astype(BF16), b_ref[rows, :])
            return c

        lax.fori_loop(0, t // kc, kstep, 0, unroll=True)
        o_ref[...] = acc[...].astype(BF16)

    (out,), landed = _call(
        body, name=name, grid=(nb,),
        in_specs=[a_spec, VMEM_FULL],
        out_specs=[pl.BlockSpec((tb, n), lambda i: (i, 0))],
        out_shape=[jax.ShapeDtypeStruct((nb * tb, n), BF16)],
        scratch_shapes=[pltpu.VMEM((tb, n), F32)],
        args=(a, b), carry=carry)
    return (out, landed) if carry is not None else out


MM_TB = 256


def _mm_tn_many(arrays, b, name):
    t, n = b.shape
    kc = min(512, t)
    counts = [a.shape[1] // MM_TB for a in arrays]
    starts = [sum(counts[:k]) for k in range(len(arrays))]

    def spec(start, count):
        return pl.BlockSpec((t, MM_TB), lambda i: (0, jnp.clip(i - start, 0, count - 1)))

    def body(*refs):
        a_refs, (b_ref, o_ref, acc) = refs[:len(arrays)], refs[len(arrays):]
        i = pl.program_id(0)
        for a_ref, start, count in zip(a_refs, starts, counts):
            @pl.when((i >= start) & (i < start + count))
            def _(a_ref=a_ref):
                acc[...] = jnp.zeros_like(acc)

                def kstep(k, c):
                    rows = pl.ds(pl.multiple_of(k * kc, kc), kc)
                    acc[...] += _tn(a_ref[rows, :].astype(BF16), b_ref[rows, :])
                    return c

                lax.fori_loop(0, t // kc, kstep, 0, unroll=True)
                o_ref[...] = acc[...].astype(BF16)

    return pl.pallas_call(
        body, name=name, grid=(sum(counts),),
        in_specs=[spec(s, c) for s, c in zip(starts, counts)] + [VMEM_FULL],
        out_specs=pl.BlockSpec((MM_TB, n), lambda i: (i, 0)),
        out_shape=jax.ShapeDtypeStruct((sum(counts) * MM_TB, n), BF16),
        scratch_shapes=[pltpu.VMEM((MM_TB, n), F32)],
        compiler_params=_cparams(1),
    )(*arrays, b)


def _mix_pre_fwd(h, g, wint, carry=None):
    t = h.shape[0]
    tm = _tile(t)

    def body(h_ref, g_ref, w_ref, u_ref, *outs):
        xhat, _ = _rms_parts(h_ref[...])
        u = (xhat * g_ref[...]).astype(BF16)
        u_ref[...] = u
        for o_ref, off, size in zip(outs, IN_OFFS, IN_SIZES):
            o_ref[...] = _nt(u, w_ref[off:off + size, :])

    return _call(
        body, name="mix_pre_fwd", grid=(t // tm,),
        in_specs=[_row_tile(tm, D_MODEL), _acc_row(D_MODEL), VMEM_FULL],
        out_specs=[_row_tile(tm, D_MODEL)] + [_row_tile(tm, s) for s in IN_SIZES],
        out_shape=[jax.ShapeDtypeStruct((t, D_MODEL), BF16)] + [jax.ShapeDtypeStruct((t, s), F32) for s in IN_SIZES],
        args=(h, g, wint), carry=carry)


def _mix_pre_bwd(h, g, wint, dh2, dz, carry=None):
    t = h.shape[0]
    tm = _tile(t)

    def body(h_ref, g_ref, w_ref, dh2_ref, *rest):
        dz_refs, (dh1_ref, dg_ref) = rest[:len(IN_SIZES)], rest[len(IN_SIZES):]
        i = pl.program_id(0)
        gv = g_ref[...]
        xhat, r = _rms_parts(h_ref[...])
        du = jnp.zeros((tm, D_MODEL), F32)
        for dz_ref, off, size in zip(dz_refs, IN_OFFS, IN_SIZES):
            du = du + _nn(dz_ref[...].astype(BF16), w_ref[off:off + size, :])
        dx, dg = _rms_bwd(du, gv, xhat, r)
        dh1_ref[...] = dh2_ref[...] + dx

        @pl.when(i == 0)
        def _():
            dg_ref[...] = jnp.zeros_like(dg_ref)

        dg_ref[...] += dg

    return _call(
        body, name="mix_pre_bwd", grid=(t // tm,),
        in_specs=[_row_tile(tm, D_MODEL), _acc_row(D_MODEL), VMEM_FULL, _row_tile(tm, D_MODEL)]
        + [_row_tile(tm, s) for s in IN_SIZES],
        out_specs=[_row_tile(tm, D_MODEL), _acc_row(D_MODEL)],
        out_shape=[jax.ShapeDtypeStruct((t, D_MODEL), F32), jax.ShapeDtypeStruct((1, D_MODEL), F32)],
        args=(h, g, wint, dh2, *dz), carry=carry)


def _disc_math(lre, lim, ldt, bre, bim):
    dt = jnp.exp(ldt)
    mag = jnp.exp(lre * dt)
    ar = mag * jnp.cos(lim * dt)
    ai = mag * jnp.sin(lim * dt)
    den = lre * lre + lim * lim
    nr = ar - 1.0
    fr = (nr * lre + ai * lim) / den
    fi = (ai * lre - nr * lim) / den
    fr, fi = fr[:, None, :], fi[:, None, :]
    return ar, ai, fr * bre - fi * bim, fr * bim + fi * bre


def _s5_disc(lre, lim, ldt, bre, bim):
    def body(lre_ref, lim_ref, ldt_ref, bre_ref, bim_ref, ar_ref, ai_ref, bbr_ref, bbi_ref):
        ar, ai, bbr, bbi = _disc_math(lre_ref[...], lim_ref[...], ldt_ref[...], bre_ref[...], bim_ref[...])
        ar_ref[...] = ar
        ai_ref[...] = ai
        bbr_ref[...] = bbr
        bbi_ref[...] = bbi

    small = jax.ShapeDtypeStruct(lre.shape, F32)
    big = jax.ShapeDtypeStruct(bre.shape, F32)
    return pl.pallas_call(body, name="s5_disc", out_shape=[small, small, big, big],
                          in_specs=[VMEM_FULL] * 5, out_specs=[VMEM_FULL] * 4)(lre, lim, ldt, bre, bim)


def _s5_disc_bwd(lre, lim, ldt, bre, bim, dar, dai, dbbr, dbbi):
    def body(lre_ref, lim_ref, ldt_ref, bre_ref, bim_ref, dar_ref, dai_ref, dbbr_ref, dbbi_ref,
             glre_ref, glim_ref, gldt_ref, gbre_ref, gbim_ref):
        _, vjp = jax.vjp(_disc_math, lre_ref[...], lim_ref[...], ldt_ref[...], bre_ref[...], bim_ref[...])
        glre, glim, gldt, gbre, gbim = vjp((dar_ref[...], dai_ref[...], dbbr_ref[...], dbbi_ref[...]))
        glre_ref[...] = glre
        glim_ref[...] = glim
        gldt_ref[...] = gldt
        gbre_ref[...] = gbre
        gbim_ref[...] = gbim

    small = jax.ShapeDtypeStruct(lre.shape, F32)
    big = jax.ShapeDtypeStruct(bre.shape, F32)
    return pl.pallas_call(body, name="s5_disc_bwd",
                          out_shape=[small, small, jax.ShapeDtypeStruct(ldt.shape, F32), big, big],
                          in_specs=[VMEM_FULL] * 9, out_specs=[VMEM_FULL] * 5,
                          )(lre, lim, ldt, bre, bim, dar, dai, dbbr, dbbi)


def _cmul(ar, ai, br, bi):
    return ar * br - ai * bi, ar * bi + ai * br


def _cpow(ar, ai, n):
    rr, ri = None, None
    pr, pi = ar, ai
    while n:
        if n & 1:
            rr, ri = (pr, pi) if rr is None else _cmul(rr, ri, pr, pi)
        n >>= 1
        if n:
            pr, pi = _cmul(pr, pi, pr, pi)
    return rr, ri


def _shift_rows(v, down):
    row = lax.broadcasted_iota(jnp.int32, v.shape, 0)
    if down:
        return jnp.where(row == 0, 0.0, pltpu.roll(v, 1, 0))
    return jnp.where(row == S5_SEGS - 1, 0.0, pltpu.roll(v, S5_SEGS - 1, 0))


def _chain_segments(er, ei, pr, pi, down):
    fr, fi = er, ei
    for _ in range(S5_SEGS - 1):
        sr, si = _shift_rows(fr, down), _shift_rows(fi, down)
        mr, mi = _cmul(pr, pi, sr, si)
        fr, fi = er + mr, ei + mi
    return _shift_rows(fr, down), _shift_rows(fi, down)


def _rows_to_scan_order(src_ref, dst_ref, t):
    ls = t // S5_SEGS

    def tile(j, c):
        dst_ref[pl.ds(pl.multiple_of(j * S5_SEGS, S5_SEGS), S5_SEGS), :] = src_ref[pl.ds(j, S5_SEGS, stride=ls), :]
        return c

    lax.fori_loop(0, ls, tile, 0, unroll=8)


def _rows_from_scan_order(src_ref, dst_ref, t):
    ls = t // S5_SEGS
    for s in range(S5_SEGS):
        def tile(jb, c, s=s):
            dst_ref[pl.ds(pl.multiple_of(s * ls + jb * 8, 8), 8), :] = (
                src_ref[pl.ds(jb * 8 * S5_SEGS + s, 8, stride=S5_SEGS), :])
            return c

        lax.fori_loop(0, ls // 8, tile, 0, unroll=8)


def _s5_fwd(ug, bd, ctd, ar4, ai4, dskip, carry=None):
    t = ug.shape[0]
    ls = t // S5_SEGS
    rc = min(512, t)
    ns = S5_BSTATE

    def body(ugn_ref, bd_ref, ct_ref, ar_ref, ai_ref, d_ref, xs_hbm, yn_ref, buf, ug_ref, y_ref, sem):
        cb = pl.program_id(0)
        bdv = bd_ref[0]
        _rows_to_scan_order(ugn_ref, ug_ref, t)

        def mm(i, c):
            rows = pl.ds(pl.multiple_of(i * rc, rc), rc)
            buf[rows, :] = _nn(ug_ref[rows, :].astype(BF16), bdv)
            return c

        lax.fori_loop(0, t // rc, mm, 0, unroll=True)
        arb = jnp.broadcast_to(ar_ref[0], (S5_SEGS, ns))
        aib = jnp.broadcast_to(ai_ref[0], (S5_SEGS, ns))

        def step(j, c, store):
            sr, si = c
            rows = pl.ds(pl.multiple_of(j * S5_SEGS, S5_SEGS), S5_SEGS)
            nr = arb * sr - aib * si + buf[rows, 0:ns]
            ni = arb * si + aib * sr + buf[rows, ns:2 * ns]
            if store:
                buf[rows, 0:ns] = nr
                buf[rows, ns:2 * ns] = ni
            return nr, ni

        zero = jnp.zeros((S5_SEGS, ns), F32)
        er, ei = lax.fori_loop(0, ls, functools.partial(step, store=False), (zero, zero))
        pr, pi = _cpow(arb, aib, ls)
        init = _chain_segments(er, ei, pr, pi, down=True)
        lax.fori_loop(0, ls, functools.partial(step, store=True), init)

        out = pltpu.make_async_copy(buf, xs_hbm.at[cb], sem)
        out.start()
        ctv = ct_ref[0]
        dv = d_ref[...]

        def ymm(i, c):
            rows = pl.ds(pl.multiple_of(i * rc, rc), rc)
            y_ref[rows, :] = _nn(buf[rows, :].astype(BF16), ctv) + dv * ug_ref[rows, :]
            return c

        lax.fori_loop(0, t // rc, ymm, 0, unroll=True)
        _rows_from_scan_order(y_ref, yn_ref, t)
        out.wait()

    return _call(
        body, name="s5_fwd", grid=(S5_BLOCKS,),
        in_specs=[pl.BlockSpec((t, 128), lambda i: (0, i)),
                  pl.BlockSpec((1, 128, 2 * ns), lambda i: (i, 0, 0)),
                  pl.BlockSpec((1, 2 * ns, 128), lambda i: (i, 0, 0)),
                  pl.BlockSpec((1, 1, ns), lambda i: (i, 0, 0)),
                  pl.BlockSpec((1, 1, ns), lambda i: (i, 0, 0)),
                  pl.BlockSpec((1, 128), lambda i: (0, i))],
        out_specs=[ANY, pl.BlockSpec((t, 128), lambda i: (0, i))],
        out_shape=[jax.ShapeDtypeStruct((S5_BLOCKS, t, 2 * ns), F32), jax.ShapeDtypeStruct((t, S5_WIDTH), F32)],
        scratch_shapes=[pltpu.VMEM((t, 2 * ns), F32), pltpu.VMEM((t, 128), F32), pltpu.VMEM((t, 128), F32),
                        pltpu.SemaphoreType.DMA(())],
        args=(ug, bd, ctd, ar4, ai4, dskip), carry=carry)


def _s5_bwd(dy, ug, xs, cd, bdt, ar4, ai4, dskip, carry=None):
    t = ug.shape[0]
    ls = t // S5_SEGS
    rc = min(512, t)
    ns = S5_BSTATE

    def body(dyn_ref, ugn_ref, xs_hbm, cd_ref, bdt_ref, ar_ref, ai_ref, d_ref,
             dugn_ref, dbd_ref, dcd_ref, dd_ref, dar_ref, dai_ref, xbuf, lam, dy_ref, ug_ref, dug_ref, sem):
        cb = pl.program_id(0)
        load = pltpu.make_async_copy(xs_hbm.at[cb], xbuf, sem)
        load.start()
        cdv = cd_ref[0]
        _rows_to_scan_order(dyn_ref, dy_ref, t)
        _rows_to_scan_order(ugn_ref, ug_ref, t)

        def mm(i, c):
            rows = pl.ds(pl.multiple_of(i * rc, rc), rc)
            lam[rows, :] = _nn(dy_ref[rows, :].astype(BF16), cdv)
            return c

        lax.fori_loop(0, t // rc, mm, 0, unroll=True)
        arb = jnp.broadcast_to(ar_ref[0], (S5_SEGS, ns))
        aib = jnp.broadcast_to(ai_ref[0], (S5_SEGS, ns))

        def lam_step(j, lr, li):
            rows = pl.ds(pl.multiple_of(j * S5_SEGS, S5_SEGS), S5_SEGS)
            nr = arb * lr + aib * li + lam[rows, 0:ns]
            ni = arb * li - aib * lr + lam[rows, ns:2 * ns]
            return rows, nr, ni

        def pass1(jj, c):
            _, nr, ni = lam_step(ls - 1 - jj, *c)
            return nr, ni

        zero = jnp.zeros((S5_SEGS, ns), F32)
        er, ei = lax.fori_loop(0, ls, pass1, (zero, zero))
        pr, pi = _cpow(arb, aib, ls)
        init = _chain_segments(er, ei, pr, -pi, down=False)
        load.wait()

        def accumulate(acc, nr, ni, xpr, xpi):
            return acc[0] + nr * xpr + ni * xpi, acc[1] + ni * xpr - nr * xpi

        def pass2(jj, c):
            lr, li, accr, acci = c
            j = ls - 1 - jj
            rows, nr, ni = lam_step(j, lr, li)
            lam[rows, 0:ns] = nr
            lam[rows, ns:2 * ns] = ni
            prev = pl.ds(pl.multiple_of((j - 1) * S5_SEGS, S5_SEGS), S5_SEGS)
            accr, acci = accumulate((accr, acci), nr, ni, xbuf[prev, 0:ns], xbuf[prev, ns:2 * ns])
            return nr, ni, accr, acci

        lr, li, accr, acci = lax.fori_loop(0, ls - 1, pass2, (init[0], init[1], zero, zero))
        rows, nr, ni = lam_step(0, lr, li)
        lam[rows, 0:ns] = nr
        lam[rows, ns:2 * ns] = ni
        last = pl.ds((ls - 1) * S5_SEGS, S5_SEGS)
        accr, acci = accumulate((accr, acci), nr, ni,
                                _shift_rows(xbuf[last, 0:ns], True), _shift_rows(xbuf[last, ns:2 * ns], True))
        dar_ref[0] = jnp.sum(accr, axis=0, keepdims=True)
        dai_ref[0] = jnp.sum(acci, axis=0, keepdims=True)

        bdtv = bdt_ref[0]
        dv = d_ref[...]
        dbd_ref[...] = jnp.zeros_like(dbd_ref)
        dcd_ref[...] = jnp.zeros_like(dcd_ref)
        dd_ref[...] = jnp.zeros_like(dd_ref)

        def tail(i, c):
            rows = pl.ds(pl.multiple_of(i * rc, rc), rc)
            dy = dy_ref[rows, :]
            ug = ug_ref[rows, :]
            lb = lam[rows, :].astype(BF16)
            dug_ref[rows, :] = _nn(lb, bdtv) + dv * dy
            dbd_ref[0] += _tn(ug.astype(BF16), lb)
            dcd_ref[0] += _tn(dy.astype(BF16), xbuf[rows, :].astype(BF16))
            dd_ref[...] += jnp.sum(dy * ug, axis=0, keepdims=True)
            return c

        lax.fori_loop(0, t // rc, tail, 0, unroll=True)
        _rows_from_scan_order(dug_ref, dugn_ref, t)

    chan = pl.BlockSpec((t, 128), lambda i: (0, i))
    dense = pl.BlockSpec((1, 128, 2 * ns), lambda i: (i, 0, 0))
    vec = pl.BlockSpec((1, 1, ns), lambda i: (i, 0, 0))
    return _call(
        body, name="s5_bwd", grid=(S5_BLOCKS,),
        in_specs=[chan, chan, ANY, dense, pl.BlockSpec((1, 2 * ns, 128), lambda i: (i, 0, 0)), vec, vec,
                  pl.BlockSpec((1, 128), lambda i: (0, i))],
        out_specs=[chan, dense, dense, pl.BlockSpec((1, 128), lambda i: (0, i)), vec, vec],
        out_shape=[jax.ShapeDtypeStruct((t, S5_WIDTH), F32),
                   jax.ShapeDtypeStruct((S5_BLOCKS, 128, 2 * ns), F32),
                   jax.ShapeDtypeStruct((S5_BLOCKS, 128, 2 * ns), F32),
                   jax.ShapeDtypeStruct((1, S5_WIDTH), F32),
                   jax.ShapeDtypeStruct((S5_BLOCKS, 1, ns), F32),
                   jax.ShapeDtypeStruct((S5_BLOCKS, 1, ns), F32)],
        scratch_shapes=[pltpu.VMEM((t, 2 * ns), F32), pltpu.VMEM((t, 2 * ns), F32)]
        + [pltpu.VMEM((t, 128), F32)] * 3 + [pltpu.SemaphoreType.DMA(())],
        args=(dy, ug, xs, cd, bdt, ar4, ai4, dskip), carry=carry)


def _cumsum_rows(x, reverse):
    c = x.shape[0]
    row = lax.broadcasted_iota(jnp.int32, x.shape, 0)
    d = 1
    while d < c:
        if reverse:
            x = x + jnp.where(row < c - d, pltpu.roll(x, c - d, 0), 0.0)
        else:
            x = x + jnp.where(row >= d, pltpu.roll(x, d, 0), 0.0)
        d *= 2
    return x


def _gla_common(q, k, alow, wup, bup):
    c = GLA_CHUNK
    pre = _nn(alow.astype(BF16), wup.astype(BF16)) + bup
    la = (jnp.minimum(pre, 0.0) - jnp.log(1.0 + jnp.exp(-jnp.abs(pre)))) * (1.0 / GLA_TAU)
    rr = lax.broadcasted_iota(jnp.int32, (c, c), 0)
    cc = lax.broadcasted_iota(jnp.int32, (c, c), 1)
    tril = (rr >= cc).astype(F32)
    bc = _cumsum_rows(la, reverse=False)
    bl = bc[c - 1:c, :]
    e_pos = jnp.exp(bc)
    e_neg = jnp.exp(-bc)
    e_end = jnp.exp(bl - bc)
    qt = q * (GLA_DK ** -0.5) * e_pos
    kt = k * e_neg
    ke = k * e_end
    lane = lax.broadcasted_iota(jnp.int32, (1, GLA_KEY), 1)
    masks = [((lane >= h * GLA_DK) & (lane < (h + 1) * GLA_DK)).astype(F32) for h in range(GLA_HEADS)]
    return dict(pre=pre, tril=tril, bc=bc, bl=bl, e_pos=e_pos, e_neg=e_neg, e_end=e_end,
                qt=qt, kt=kt, ke=ke, dec=jnp.exp(bl), masks=masks)


def _gla_fwd(q, k, v, alow, wup, bup, carry=None):
    t = q.shape[0]
    c = GLA_CHUNK
    n = t // c
    step = GLA_STEP_CHUNKS * c

    def body(q_ref, k_ref, v_ref, al_ref, wup_ref, bup_ref, o_ref, ss_ref, s_ref):
        i = pl.program_id(0)

        @pl.when(i == 0)
        def _():
            s_ref[...] = jnp.zeros_like(s_ref)

        wup_v, bup_v = wup_ref[...], bup_ref[...]
        s = s_ref[...]
        for j in range(GLA_STEP_CHUNKS):
            tok = slice(j * c, (j + 1) * c)
            m = _gla_common(q_ref[tok, :], k_ref[tok, :], al_ref[tok, :], wup_v, bup_v)
            ss_ref[j] = s
            sb = s.astype(BF16)
            ktb = m["kt"].astype(BF16)
            update = jnp.zeros_like(s)
            for h in range(GLA_HEADS):
                mask = m["masks"][h]
                qm = (m["qt"] * mask).astype(BF16)
                vh = v_ref[tok, h * GLA_DV:(h + 1) * GLA_DV].astype(BF16)
                p = (m["tril"] * _nt(qm, ktb)).astype(BF16)
                o_ref[tok, h * GLA_DV:(h + 1) * GLA_DV] = _nn(p, vh) + _nt(qm, sb)
                update = update + _tn(vh, (m["ke"] * mask).astype(BF16))
            s = m["dec"] * s + update
        s_ref[...] = s

    return _call(
        body, name="gla_fwd", grid=(t // step,),
        in_specs=[_row_tile(step, GLA_KEY), _row_tile(step, GLA_KEY), _row_tile(step, GLA_VAL),
                  _row_tile(step, GLA_RANK), VMEM_FULL, VMEM_FULL],
        out_specs=[_row_tile(step, GLA_VAL), pl.BlockSpec((GLA_STEP_CHUNKS, GLA_DV, GLA_KEY), lambda i: (i, 0, 0))],
        out_shape=[jax.ShapeDtypeStruct((t, GLA_VAL), F32), jax.ShapeDtypeStruct((n, GLA_DV, GLA_KEY), F32)],
        scratch_shapes=[pltpu.VMEM((GLA_DV, GLA_KEY), F32)],
        args=(q, k, v, alow, wup, bup), carry=carry)


def _gla_bwd(q, k, v, alow, wup, bup, ssave, do, carry=None):
    t = q.shape[0]
    c = GLA_CHUNK
    n = t // c

    def body(q_ref, k_ref, v_ref, al_ref, wup_ref, bup_ref, ss_ref, do_ref,
             dq_ref, dk_ref, dv_ref, dal_ref, dwup_ref, dbup_ref, ds_ref):
        i = pl.program_id(0)

        @pl.when(i == 0)
        def _():
            ds_ref[...] = jnp.zeros_like(ds_ref)
            dwup_ref[...] = jnp.zeros_like(dwup_ref)
            dbup_ref[...] = jnp.zeros_like(dbup_ref)

        wup_v, bup_v = wup_ref[...], bup_ref[...]
        ds_in = ds_ref[...]
        dwup = jnp.zeros((GLA_RANK, GLA_KEY), F32)
        dbup = jnp.zeros((1, GLA_KEY), F32)
        for j in reversed(range(GLA_STEP_CHUNKS)):
            tok = slice(j * c, (j + 1) * c)
            alow_v = al_ref[tok, :]
            m = _gla_common(q_ref[tok, :], k_ref[tok, :], alow_v, wup_v, bup_v)
            s = ss_ref[j]
            sb = s.astype(BF16)
            dsb = ds_in.astype(BF16)
            qt, kt, ke = m["qt"], m["kt"], m["ke"]
            ktb = kt.astype(BF16)
            dqt = jnp.zeros((c, GLA_KEY), F32)
            dkt = jnp.zeros((c, GLA_KEY), F32)
            dke = jnp.zeros((c, GLA_KEY), F32)
            update = jnp.zeros_like(ds_in)
            for h in range(GLA_HEADS):
                mask = m["masks"][h]
                qm = (qt * mask).astype(BF16)
                km = (kt * mask).astype(BF16)
                kem = (ke * mask).astype(BF16)
                cols = slice(h * GLA_DV, (h + 1) * GLA_DV)
                vh = v_ref[tok, cols].astype(BF16)
                doh = do_ref[tok, cols].astype(BF16)
                p = (m["tril"] * _nt(qm, ktb)).astype(BF16)
                dp = (m["tril"] * _nt(doh, vh)).astype(BF16)
                dv_ref[tok, cols] = (_tn(p, doh) + _nt(kem, dsb)).astype(BF16)
                dqt = dqt + _nn(dp, km) + _nn(doh, sb) * mask
                dkt = dkt + _tn(dp, qm)
                dke = dke + _nn(vh, dsb) * mask
                update = update + _tn(doh, qm)
            ddec = jnp.sum(ds_in * s, axis=0, keepdims=True)
            dq_ref[tok, :] = (dqt * m["e_pos"] * (GLA_DK ** -0.5)).astype(BF16)
            dk_ref[tok, :] = (dkt * m["e_neg"] + dke * m["e_end"]).astype(BF16)
            dkeke = dke * ke
            dbl = jnp.sum(dkeke, axis=0, keepdims=True) + ddec * m["dec"]
            last = (lax.broadcasted_iota(jnp.int32, (c, 1), 0) == c - 1).astype(F32)
            dla = _cumsum_rows(dqt * qt - dkt * kt - dkeke + last * dbl, reverse=True)
            dpre = dla * (1.0 / GLA_TAU) * jax.nn.sigmoid(-m["pre"])
            dpb = dpre.astype(BF16)
            dal_ref[tok, :] = _nt(dpb, wup_v.astype(BF16)).astype(BF16)
            dwup = dwup + _tn(alow_v.astype(BF16), dpb)
            dbup = dbup + jnp.sum(dpre, axis=0, keepdims=True)
            ds_in = m["dec"] * ds_in + update
        ds_ref[...] = ds_in
        dwup_ref[...] += dwup
        dbup_ref[...] += dbup

    step = GLA_STEP_CHUNKS * c
    nsteps = t // step

    def rev(d):
        return pl.BlockSpec((step, d), lambda i: (nsteps - 1 - i, 0))

    return _call(
        body, name="gla_bwd", grid=(nsteps,),
        in_specs=[rev(GLA_KEY), rev(GLA_KEY), rev(GLA_VAL), rev(GLA_RANK), VMEM_FULL, VMEM_FULL,
                  pl.BlockSpec((GLA_STEP_CHUNKS, GLA_DV, GLA_KEY), lambda i: (nsteps - 1 - i, 0, 0)), rev(GLA_VAL)],
        out_specs=[rev(GLA_KEY), rev(GLA_KEY), rev(GLA_VAL), rev(GLA_RANK),
                   pl.BlockSpec((GLA_RANK, GLA_KEY), lambda i: (0, 0)), _acc_row(GLA_KEY)],
        out_shape=[jax.ShapeDtypeStruct((t, GLA_KEY), BF16), jax.ShapeDtypeStruct((t, GLA_KEY), BF16),
                   jax.ShapeDtypeStruct((t, GLA_VAL), BF16), jax.ShapeDtypeStruct((t, GLA_RANK), BF16),
                   jax.ShapeDtypeStruct((GLA_RANK, GLA_KEY), F32), jax.ShapeDtypeStruct((1, GLA_KEY), F32)],
        scratch_shapes=[pltpu.VMEM((GLA_DV, GLA_KEY), F32)],
        args=(q, k, v, alow, wup, bup, ssave, do), carry=carry)


def _post_math(y, o, r, gs5, ggla, wg, bg, gn, ps5t, pglat):
    y2 = y * y
    th = jnp.tanh(GELU_C0 * (y + GELU_C1 * y * y2))
    z5 = 0.5 * y * (1.0 + th)
    z5b = z5.astype(BF16)
    gate = jax.nn.sigmoid(_nn(z5b, wg) + bg)
    ys5 = z5 * gate
    rs, on = [], []
    for h in range(GLA_HEADS):
        oh = o[:, h * GLA_DV:(h + 1) * GLA_DV]
        rh = lax.rsqrt(jnp.mean(oh * oh, axis=-1, keepdims=True) + EPS)
        rs.append(rh)
        on.append(oh * rh)
    on = jnp.concatenate(on, axis=-1)
    sr = jax.nn.sigmoid(r)
    silu_r = r * sr
    ygla = on * gn * silu_r
    ys5b, yglab = ys5.astype(BF16), ygla.astype(BF16)
    m5 = _nt(ys5b, ps5t)
    mg = _nt(yglab, pglat)
    s5g, glag = jax.nn.sigmoid(gs5), jax.nn.sigmoid(ggla)
    merged = s5g * m5 + glag * mg
    return dict(y2=y2, th=th, z5=z5, z5b=z5b, gate=gate, ys5b=ys5b, yglab=yglab, rs=rs, on=on, sr=sr,
                silu_r=silu_r, m5=m5, mg=mg, s5g=s5g, glag=glag, mergedb=merged.astype(BF16))


def _mix_post_fwd(y, o, r, gs5, ggla, h1, wg, bg, gn, ps5t, pglat, wout, carry=None):
    t = o.shape[0]
    tm = _tile(t)

    def body(y_ref, o_ref, r_ref, gs5_ref, ggla_ref, h1_ref, wg_ref, bg_ref, gn_ref, ps_ref, pg_ref, wo_ref, h2_ref):
        m = _post_math(y_ref[...], o_ref[...], r_ref[...], gs5_ref[...], ggla_ref[...],
                       wg_ref[...], bg_ref[...], gn_ref[...], ps_ref[...], pg_ref[...])
        h2_ref[...] = h1_ref[...] + _nn(m["mergedb"], wo_ref[...])

    (h2,), landed = _call(
        body, name="mix_post_fwd", grid=(t // tm,),
        in_specs=[_row_tile(tm, 512)] * 3 + [_row_tile(tm, D_MODEL)] * 3
        + [VMEM_FULL, _acc_row(512), _acc_row(512), VMEM_FULL, VMEM_FULL, VMEM_FULL],
        out_specs=[_row_tile(tm, D_MODEL)],
        out_shape=[jax.ShapeDtypeStruct((t, D_MODEL), F32)],
        args=(y, o, r, gs5, ggla, h1, wg, bg, gn, ps5t, pglat, wout), carry=carry)
    return h2, landed


def _mix_post_bwd(y, o, r, gs5, ggla, dh2, wg, bg, gn, ps5t, pglat, wout, carry=None):
    t = o.shape[0]
    tm = _tile(t) // 2

    def body(y_ref, o_ref, r_ref, gs5_ref, ggla_ref, dh2_ref, wg_ref, bg_ref, gn_ref, ps_ref, pg_ref, wo_ref,
             dy_ref, do_ref, dr_ref, dgs5_ref, dggla_ref, dbg_ref, dgn_ref,
             z5b_ref, dgp_ref, ys5b_ref, dm5b_ref, yglab_ref, dmgb_ref, mergedb_ref, dh2b_ref):
        i = pl.program_id(0)
        yv, ov, rv = y_ref[...], o_ref[...], r_ref[...]
        wg, gn, ps5t, pglat = wg_ref[...], gn_ref[...], ps_ref[...], pg_ref[...]
        m = _post_math(yv, ov, rv, gs5_ref[...], ggla_ref[...], wg, bg_ref[...], gn, ps5t, pglat)
        dh2b = dh2_ref[...].astype(BF16)
        dmerged = _nt(dh2b, wo_ref[...])
        s5g, glag = m["s5g"], m["glag"]
        dgs5_ref[...] = (dmerged * m["m5"] * s5g * (1.0 - s5g)).astype(BF16)
        dggla_ref[...] = (dmerged * m["mg"] * glag * (1.0 - glag)).astype(BF16)
        dm5b = (dmerged * s5g).astype(BF16)
        dmgb = (dmerged * glag).astype(BF16)
        dys5 = _nn(dm5b, ps5t)
        dygla = _nn(dmgb, pglat)
        gate, z5, th = m["gate"], m["z5"], m["th"]
        dgpre = dys5 * z5 * gate * (1.0 - gate)
        dgpb = dgpre.astype(BF16)
        dz5 = dys5 * gate + _nt(dgpb, wg)
        dgelu = 0.5 * (1.0 + th) + 0.5 * yv * (1.0 - th * th) * GELU_C0 * (1.0 + 3.0 * GELU_C1 * m["y2"])
        dy_ref[...] = dz5 * dgelu
        on, sr, silu_r = m["on"], m["sr"], m["silu_r"]
        dr_ref[...] = (dygla * on * gn * sr * (1.0 + rv * (1.0 - sr))).astype(BF16)
        dgn = jnp.sum(dygla * on * silu_r, axis=0, keepdims=True)
        don = dygla * gn * silu_r
        for h in range(GLA_HEADS):
            cols = slice(h * GLA_DV, (h + 1) * GLA_DV)
            donh, onh = don[:, cols], on[:, cols]
            do_ref[:, cols] = (m["rs"][h] * (donh - onh * jnp.mean(donh * onh, axis=-1, keepdims=True))).astype(BF16)

        @pl.when(i == 0)
        def _():
            dbg_ref[...] = jnp.zeros_like(dbg_ref)
            dgn_ref[...] = jnp.zeros_like(dgn_ref)

        dbg_ref[...] += jnp.sum(dgpre, axis=0, keepdims=True)
        dgn_ref[...] += dgn
        z5b_ref[...] = m["z5b"]
        dgp_ref[...] = dgpb
        ys5b_ref[...] = m["ys5b"]
        dm5b_ref[...] = dm5b
        yglab_ref[...] = m["yglab"]
        dmgb_ref[...] = dmgb
        mergedb_ref[...] = m["mergedb"]
        dh2b_ref[...] = dh2b

    def f32(d):
        return jax.ShapeDtypeStruct((t, d), F32)

    def b16(d):
        return jax.ShapeDtypeStruct((t, d), BF16)

    widths = (512, 512, 512, 1024, 512, 1024, 1024, 1024)
    return _call(
        body, name="mix_post_bwd", grid=(t // tm,),
        in_specs=[_row_tile(tm, 512)] * 3 + [_row_tile(tm, D_MODEL)] * 3
        + [VMEM_FULL, _acc_row(512), _acc_row(512), VMEM_FULL, VMEM_FULL, VMEM_FULL],
        out_specs=[_row_tile(tm, 512)] * 3 + [_row_tile(tm, D_MODEL)] * 2
        + [_acc_row(512)] * 2 + [_row_tile(tm, w) for w in widths],
        out_shape=[f32(512), b16(512), b16(512), b16(D_MODEL), b16(D_MODEL)]
        + [jax.ShapeDtypeStruct((1, 512), F32)] * 2
        + [b16(w) for w in widths],
        args=(y, o, r, gs5, ggla, dh2, wg, bg, gn, ps5t, pglat, wout), carry=carry)


ADAM_TILE_ELEMS = 256 * 1024


def _adamw(w, g, m, v, name):
    rows, cols = w.shape
    tr = rows
    while tr * cols > ADAM_TILE_ELEMS and tr % 16 == 0:
        tr //= 2

    spec = pl.BlockSpec((tr, cols), lambda i: (i, 0))
    sh = jax.ShapeDtypeStruct((rows, cols), F32)
    return pl.pallas_call(functools.partial(_adamw_body), name=name, grid=(rows // tr,), in_specs=[spec] * 4,
                          out_specs=[spec] * 3, out_shape=[sh] * 3, compiler_params=_cparams(1))(w, g, m, v)


def _adamw_math(w, g, m, v):
    nm = ADAM_B1 * m + (1.0 - ADAM_B1) * g
    nv = ADAM_B2 * v + (1.0 - ADAM_B2) * (g * g)
    m_hat = nm / (1.0 - ADAM_B1 ** ADAM_STEP)
    v_hat = nv / (1.0 - ADAM_B2 ** ADAM_STEP)
    return -ADAM_LR * (m_hat / (jnp.sqrt(v_hat) + ADAM_EPS) + ADAM_WD * w), nm, nv


def _adamw_body(w_ref, g_ref, m_ref, v_ref, d_ref, nm_ref, nv_ref):
    d_ref[...], nm_ref[...], nv_ref[...] = _adamw_math(w_ref[...], g_ref[...], m_ref[...], v_ref[...])


SUM_ADAM_ROWS = 32


def _sum_adamw(landed, ws, ms, vs, name, carry=None):
    k = len(ws)
    n = landed[0].shape[0]
    r, c = ws[0].shape
    tr = SUM_ADAM_ROWS

    def body(*refs):
        lands, (w_refs, m_refs, v_refs), outs = refs[:k], (refs[k:2 * k], refs[2 * k:3 * k], refs[3 * k:4 * k]), refs[4 * k:]
        for i in range(k):
            g = lands[i][0].astype(F32)
            for s in range(1, n):
                g = g + lands[i][s].astype(F32)
            outs[i][...] = g
            outs[k + i][...], outs[2 * k + i][...], outs[3 * k + i][...] = _adamw_math(
                w_refs[i][...], g, m_refs[i][...], v_refs[i][...])

    row = pl.BlockSpec((tr, c), lambda i: (i, 0))
    return _call(
        body, name=name, grid=(r // tr,),
        in_specs=[pl.BlockSpec((n, tr, c), lambda i: (0, i, 0))] * k + [row] * (3 * k),
        out_specs=[row] * (4 * k), out_shape=[jax.ShapeDtypeStruct((r, c), F32)] * (4 * k),
        args=(*landed, *ws, *ms, *vs), carry=carry)


def _adamw_many(ws, gs, ms, vs, name):
    n = len(ws)

    def body(*refs):
        ins, outs = refs[:4 * n], refs[4 * n:]
        for i in range(n):
            _adamw_body(*(ins[j * n + i] for j in range(4)), *(outs[j * n + i] for j in range(3)))

    shapes = [jax.ShapeDtypeStruct(w.shape, F32) for w in ws]
    res = pl.pallas_call(body, name=name, in_specs=[VMEM_FULL] * (4 * n), out_specs=[VMEM_FULL] * (3 * n),
                         out_shape=shapes * 3)(*ws, *gs, *ms, *vs)
    return res[:n], res[n:2 * n], res[2 * n:]


def _exchange(carry, name):
    return _call(None, name=name, grid=(1,), in_specs=[], out_specs=[], out_shape=[], args=(), carry=carry)[1]


def _pair_add(slabs, from_pair, name):
    _, r, cols = slabs.shape

    def body(s_ref, p_ref, o_ref):
        c = lax.axis_index("c")
        mine = jnp.where(c == 0, s_ref[0, 0].astype(F32), s_ref[0, 1].astype(F32))
        o_ref[0] = (mine + p_ref[0].astype(F32)).astype(BF16)

    return pl.pallas_call(
        body, name=name, grid=(4,),
        in_specs=[pl.BlockSpec((1, 2, r, cols), lambda i: (i, 0, 0, 0)), pl.BlockSpec((1, r, cols), lambda i: (i, 0, 0))],
        out_specs=pl.BlockSpec((1, r, cols), lambda i: (i, 0, 0)),
        out_shape=jax.ShapeDtypeStruct((4, r, cols), BF16),
        compiler_params=_cparams(1),
    )(slabs.reshape(4, 2, r, cols), from_pair)


def _sum_slabs(slabs, name):
    n = slabs.shape[0]

    def body(s_ref, o_ref):
        acc = s_ref[0].astype(F32)
        for s in range(1, n):
            acc = acc + s_ref[s].astype(F32)
        o_ref[...] = acc

    return pl.pallas_call(
        body, name=name, in_specs=[VMEM_FULL], out_specs=VMEM_FULL,
        out_shape=jax.ShapeDtypeStruct(slabs.shape[1:], F32),
        compiler_params=pltpu.CompilerParams(vmem_limit_bytes=VMEM_LIMIT_BYTES),
    )(slabs)


BIG = ("ffn1_w1", "ffn1_w3", "ffn1_w2", "w_in", "s5_glu_w", "gla_a_up_w", "proj_s5", "proj_gla", "w_out",
       "ffn2_w1", "ffn2_w3", "ffn2_w2")
GROUPS = (("ffn1_w1", "ffn1_w3", "ffn1_w2"),
          ("w_in", "s5_glu_w", "gla_a_up_w", "proj_s5", "proj_gla", "w_out"),
          ("ffn2_w1", "ffn2_w3", "ffn2_w2"))
W_IN_ROWS = 514
W_IN_PAD = 528
UP_COLS = 32
ROW_ADAM = ("ffn1_w1", "ffn1_w3", "w_in", "ffn2_w1", "ffn2_w3")
COL_SHARDED = ("ffn1_w1", "ffn1_w3", "w_in", "proj_s5", "proj_gla", "ffn2_w1", "ffn2_w3")

SMALL = ("ffn1_norm", "mix_norm", "s5_lambda_re", "s5_lambda_im", "s5_log_dt", "s5_b_re", "s5_b_im", "s5_c_re",
         "s5_c_im", "s5_d", "s5_glu_b", "gla_a_up_b", "gla_out_norm", "ffn2_norm", "final_norm")
SMALL_SHAPES = dict(ffn1_norm=(1, 1024), mix_norm=(1, 1024), s5_lambda_re=(1, 32, 64), s5_lambda_im=(1, 32, 64),
                    s5_log_dt=(1, 32), s5_b_re=(1, 32, 64, 16), s5_b_im=(1, 32, 64, 16), s5_c_re=(1, 32, 16, 64),
                    s5_c_im=(1, 32, 16, 64), s5_d=(1, 32, 16), s5_glu_b=(1, 512), gla_a_up_b=(1, 256),
                    gla_out_norm=(1, 512), ffn2_norm=(1, 1024), final_norm=(1024,))
SMALL_N = sum(math.prod(s) for s in SMALL_SHAPES.values())
SMALL_R = -(-SMALL_N // (64 * 1024)) * 64


def _shard_rows(name, a):
    if name == "gla_a_up_w":
        return jnp.pad(a, ((0, 0), (0, 128 - UP_COLS)))
    if name in COL_SHARDED:
        a = a.T
    if name == "w_in":
        return jnp.pad(a, ((0, W_IN_PAD - W_IN_ROWS), (0, 0)))
    return a.reshape(-1, 1024)


def _unshard_rows(name, rows, shape):
    if name == "gla_a_up_w":
        return rows[:, :UP_COLS]
    if name == "w_in":
        rows = rows[:W_IN_ROWS]
    if name in COL_SHARDED:
        return rows.reshape(shape[1], shape[0]).T
    return rows.reshape(shape)


def _pack_small(vals, loss):
    flat = jnp.concatenate([vals[n].reshape(-1).astype(F32) for n in SMALL] + [loss.reshape(1)])
    return jnp.pad(flat, (0, SMALL_R * 1024 - SMALL_N - 1)).reshape(SMALL_R, 1024)


S5_B = ("s5_b_re", "s5_b_im")


def _working(name, a):
    return a[0].transpose(0, 2, 1) if name in S5_B else a


def _declared(name, a):
    return a.transpose(0, 2, 1)[None] if name in S5_B else a.reshape(SMALL_SHAPES[name])


def _unpack_small(slab):
    flat = slab.reshape(-1)
    out, off = {}, 0
    for n in SMALL:
        size = math.prod(SMALL_SHAPES[n])
        shape = (S5_GROUPS, S5_GROUP, S5_STATE) if n in S5_B else SMALL_SHAPES[n]
        out[n] = flat[off:off + size].reshape(shape)
        off += size
    return out


FULL_SHAPES = dict(w_in=(IN_COLS, D_MODEL), s5_glu_w=(S5_WIDTH, S5_WIDTH), gla_a_up_w=(GLA_RANK, GLA_KEY),
                   proj_s5=(D_MODEL, S5_WIDTH), proj_gla=(D_MODEL, GLA_VAL), w_out=(D_MODEL, D_MODEL))


def _full_weight(name, gathered):
    if name == "gla_a_up_w":
        return gathered[:, :, :UP_COLS].transpose(1, 0, 2).reshape(GLA_RANK, GLA_KEY)
    if name == "w_in":
        gathered = gathered[:, :W_IN_ROWS]
    return gathered.reshape(FULL_SHAPES.get(name, (D_FF, D_MODEL)))


def _grad_slabs(name, g):
    if name == "gla_a_up_w":
        g = g.reshape(GLA_RANK, N_DEV, UP_COLS).transpose(1, 0, 2)
        return jnp.pad(g, ((0, 0), (0, 0), (0, 128 - UP_COLS))).astype(BF16)
    if name == "w_in":
        return jnp.pad(g.reshape(N_DEV, W_IN_ROWS, D_MODEL), ((0, 0), (0, W_IN_PAD - W_IN_ROWS), (0, 0)))
    return g.reshape(N_DEV, -1, 1024)


def _s5_dense(re, im, sign_im):
    eye = jnp.eye(8, dtype=F32)

    def one(a):
        a = a.reshape(S5_BLOCKS, 8, S5_GROUP, S5_STATE)
        return jnp.einsum("cghp,gk->cghkp", a, eye).reshape(S5_BLOCKS, 128, S5_BSTATE)

    return jnp.concatenate([one(re), sign_im * one(im)], axis=-1)


def _s5_undense(d):
    eye = jnp.eye(8, dtype=F32)

    def one(a):
        a = a.reshape(S5_BLOCKS, 8, S5_GROUP, 8, S5_STATE)
        return jnp.einsum("cghkp,gk->cghp", a, eye).reshape(S5_GROUPS, S5_GROUP, S5_STATE)

    return one(d[..., :S5_BSTATE]), one(d[..., S5_BSTATE:])


def _local_step(x, target, p, w, rows=None, opt=None):
    w = dict(w or {})
    landed_grads = {}

    def gather(names):
        return None if rows is None else ([rows[n] for n in names], "gather")

    def gathered(names, landed):
        w.update({n: _full_weight(n, g) for n, g in zip(names, landed)})

    def scatter(names):
        return None if rows is None else ([_grad_slabs(n, big[n]) for n in names], "scatter")

    def scattered(names, landed):
        landed_grads.update(zip(names, landed))

    if rows is not None:
        gathered(GROUPS[0], _exchange(gather(GROUPS[0]), "gather_ffn1"))
    g1, gm, g2 = p["ffn1_norm"], p["mix_norm"], p["ffn2_norm"]
    gf = p["final_norm"].reshape(1, D_MODEL)
    lre, lim = p["s5_lambda_re"][0], p["s5_lambda_im"][0]
    ldt = p["s5_log_dt"][0].reshape(S5_GROUPS, 1)
    bre = p["s5_b_re"][0].transpose(0, 2, 1)
    bim = p["s5_b_im"][0].transpose(0, 2, 1)
    cre, cim = p["s5_c_re"][0], p["s5_c_im"][0]
    dskip = p["s5_d"][0].reshape(1, S5_WIDTH)
    bg, bup, gn = p["s5_glu_b"], p["gla_a_up_b"], p["gla_out_norm"]

    mix_first, mix_rest = ("w_in", "gla_a_up_w"), ("s5_glu_w", "proj_s5", "proj_gla", "w_out")
    h1, (a3_1, b3_1, n1), got = _ffn_fwd(x, g1, w["ffn1_w1"], w["ffn1_w3"], w["ffn1_w2"], "ffn1_fwd",
                                         gather(mix_first + mix_rest))
    gathered(mix_first + mix_rest, got)
    wup = w["gla_a_up_w"].astype(F32)
    (u, s5in, q, k, v, r, alow, gs5, ggla), _ = _mix_pre_fwd(h1, gm, w["w_in"])
    ar, ai, bbr, bbi = _s5_disc(lre, lim, ldt, bre, bim)
    bd = _s5_dense(bbr, bbi, 1.0)
    cd = _s5_dense(cre, cim, -1.0)
    bd16, cd16 = bd.astype(BF16), cd.astype(BF16)
    bdt16, ctd16 = bd16.transpose(0, 2, 1), cd16.transpose(0, 2, 1)
    ar4 = ar.reshape(S5_BLOCKS, 1, S5_BSTATE)
    ai4 = ai.reshape(S5_BLOCKS, 1, S5_BSTATE)
    (xs, y), got = _s5_fwd(s5in, bd16, ctd16, ar4, ai4, dskip, gather(GROUPS[2][:2]))
    gathered(GROUPS[2][:2], got)
    (o, ssave), _ = _gla_fwd(q, k, v, alow, wup, bup)
    post_w = (w["s5_glu_w"], bg, gn, w["proj_s5"], w["proj_gla"], w["w_out"])
    h2, got = _mix_post_fwd(y, o, r, gs5, ggla, h1, *post_w, carry=gather(GROUPS[2][2:]))
    gathered(GROUPS[2][2:], got)
    loss, dh3, dgf, (a3_2, b3_2, n2) = _ffn_fwd_head(h2, g2, w["ffn2_w1"], w["ffn2_w3"], w["ffn2_w2"], gf, target,
                                                     "ffn2_fwd")

    big, small = {}, {}
    small["final_norm"] = dgf.reshape(D_MODEL)
    (dh2, dg2, da3, db3, s3, dhh2), _ = _ffn_bwd(
        h2, dh3, g2, a3_2, b3_2, w["ffn2_w1"], w["ffn2_w3"], w["ffn2_w2"], "ffn2_bwd")
    small["ffn2_norm"] = dg2
    big["ffn2_w1"] = _mm_tn(da3, n2, "ffn2_dw1")
    big["ffn2_w3"] = _mm_tn(db3, n2, "ffn2_dw3")
    big["ffn2_w2"] = _mm_tn(s3, dhh2, "ffn2_dw2")
    (dy, do, dr, dgs5, dggla, dbg, dgn, z5b, dgpb, ys5b, dm5b, yglab, dmgb, mergedb, dh2b), got = _mix_post_bwd(
        y, o, r, gs5, ggla, dh2, *post_w, carry=scatter(GROUPS[2][:1]))
    scattered(GROUPS[2][:1], got)
    small["s5_glu_b"] = dbg
    small["gla_out_norm"] = dgn
    big["s5_glu_w"] = _mm_tn(z5b, dgpb, "glu_dw")
    big["proj_s5"] = _mm_tn(dm5b, ys5b, "proj_s5_dw")
    big["proj_gla"] = _mm_tn(dmgb, yglab, "proj_gla_dw")
    big["w_out"] = _mm_tn(mergedb, dh2b, "w_out_dw")
    (dq, dk, dv, dalow, dwup, dbup), got = _gla_bwd(q, k, v, alow, wup, bup, ssave, do, scatter(GROUPS[2][1:2]))
    scattered(GROUPS[2][1:2], got)
    big["gla_a_up_w"] = dwup
    small["gla_a_up_b"] = dbup
    (ds5in, dbd, dcd, dd, dar4, dai4), got = _s5_bwd(
        dy, s5in, xs, cd16, bdt16, ar4, ai4, dskip, scatter(GROUPS[2][2:]))
    scattered(GROUPS[2][2:], got)
    dbbr, dbbi = _s5_undense(dbd)
    dcre, dcim_neg = _s5_undense(dcd)
    glre, glim, gldt, gbre, gbim = _s5_disc_bwd(
        lre, lim, ldt, bre, bim, dar4.reshape(S5_GROUPS, S5_STATE), dai4.reshape(S5_GROUPS, S5_STATE),
        dbbr, dbbi)
    small["s5_lambda_re"] = glre[None]
    small["s5_lambda_im"] = glim[None]
    small["s5_log_dt"] = gldt.reshape(1, S5_GROUPS)
    small["s5_b_re"] = gbre
    small["s5_b_im"] = gbim
    small["s5_c_re"] = dcre[None]
    small["s5_c_im"] = -dcim_neg[None]
    small["s5_d"] = dd.reshape(1, S5_GROUPS, S5_GROUP)
    dz = (ds5in, dq, dk, dv, dr, dalow, dgs5, dggla)
    (dh1, dgm), got = _mix_pre_bwd(h1, gm, w["w_in"], dh2, dz, scatter(mix_rest))
    scattered(mix_rest, got)
    small["mix_norm"] = dgm
    wide = _mm_tn_many(dz[:5] + dz[6:], u, "w_in_dw")
    low_at = IN_OFFS[5]
    big["w_in"] = jnp.concatenate([wide[:low_at], _mm_tn(dalow, u, "w_in_dw_low"), wide[low_at:]], axis=0)
    (dx, dg1, da3, db3, s3, dhh1), got = _ffn_bwd(
        x, dh1, g1, a3_1, b3_1, w["ffn1_w1"], w["ffn1_w3"], w["ffn1_w2"], "ffn1_bwd",
        scatter(mix_first))
    scattered(mix_first, got)
    small["ffn1_norm"] = dg1
    if rows is None:
        big["ffn1_w1"] = _mm_tn(da3, n1, "ffn1_dw1")
        big["ffn1_w3"] = _mm_tn(db3, n1, "ffn1_dw3")
        big["ffn1_w2"] = _mm_tn(s3, dhh1, "ffn1_dw2")
        return loss[0, 0], dx, big, small
    part = _pack_small(small, loss).reshape(N_DEV, SMALL_R // N_DEV, 1024)
    big["ffn1_w1"], (small_landed,) = _mm_tn(da3, n1, "ffn1_dw1", ([part], "scatter"))
    small_mine = _sum_slabs(small_landed, "sum_small")
    slabs1 = _grad_slabs("ffn1_w1", big["ffn1_w1"])
    big["ffn1_w3"], (from_pair, small_all) = _mm_tn(db3, n1, "ffn1_dw3",
                                                    [([slabs1], "pair"), ([small_mine], "gather")])
    small = small_all.reshape(SMALL_R, 1024)
    sums1 = _pair_add(slabs1, from_pair, "ffn1_w1_pair")
    slabs3 = _grad_slabs("ffn1_w3", big["ffn1_w3"])
    big["ffn1_w2"], (landed1, from_pair) = _mm_tn(s3, dhh1, "ffn1_dw2", [([sums1], "chips"), ([slabs3], "pair")])
    sums3 = _pair_add(slabs3, from_pair, "ffn1_w3_pair")
    slabs2 = _grad_slabs("ffn1_w2", big["ffn1_w2"])

    def sum_adamw(names, lands, name, carry=None):
        outs, got = _sum_adamw(lands, *([opt[n][j] for n in names] for j in range(3)), name, carry)
        for i, n in enumerate(names):
            updated[n] = outs[i::len(names)]
        return got

    updated = {}
    landed3, from_pair = sum_adamw(GROUPS[2], [landed_grads.pop(n) for n in GROUPS[2]], "adamw_ffn2",
                                   [([sums3], "chips"), ([slabs2], "pair")])
    sums2 = _pair_add(slabs2, from_pair, "ffn1_w2_pair")
    (landed2,) = _exchange(([sums2], "chips"), "scatter_ffn1_b")
    sum_adamw(GROUPS[0], [landed1, landed3, landed2], "adamw_ffn1")
    return loss[0, 0], dx, landed_grads, small, updated


NAMES = ("ffn1_norm", "ffn1_w1", "ffn1_w3", "ffn1_w2", "mix_norm", "w_in", "s5_lambda_re", "s5_lambda_im",
         "s5_log_dt", "s5_b_re", "s5_b_im", "s5_c_re", "s5_c_im", "s5_d", "s5_glu_w", "s5_glu_b", "gla_a_up_w",
         "gla_a_up_b", "gla_out_norm", "proj_s5", "proj_gla", "w_out", "ffn2_norm", "ffn2_w1", "ffn2_w3", "ffn2_w2",
         "final_norm")


def kernel(*args):
    nw = len(NAMES)
    x = args[0][0]
    wts = dict(zip(NAMES, args[1:1 + nw]))
    target = args[1 + nw][0]
    mom = dict(zip(NAMES, args[2 + nw:2 + 2 * nw]))
    var = dict(zip(NAMES, args[2 + 2 * nw:2 + 3 * nw]))

    shards = {n: wts[n][0] for n in BIG}
    rows = {n: _shard_rows(n, shards[n]).astype(BF16) for n in BIG}
    def row_layout(n, a):
        return a.T if n in ROW_ADAM else a

    opt = {n: tuple(row_layout(n, d[n][0]) for d in (wts, mom, var)) for n in GROUPS[0] + GROUPS[2]}
    _, dx, landed, small_slab, updated = _local_step(x, target, {n: wts[n] for n in SMALL}, None, rows, opt)
    loss = small_slab.reshape(-1)[SMALL_N]
    g_small = _unpack_small(small_slab)

    grad, delta, new_m, new_v = {}, {}, {}, {}
    for n, arrays in updated.items():
        grad[n], delta[n], new_m[n], new_v[n] = (row_layout(n, a)[None] for a in arrays)
    for n in GROUPS[1]:
        g_rows = _sum_slabs(landed[n], "sum_" + n)
        if n in ROW_ADAM:
            g = g_rows[:W_IN_ROWS] if n == "w_in" else g_rows
            outs = _adamw(shards[n].T, g, mom[n][0].T, var[n][0].T, "adamw_" + n)
            grad[n], delta[n], new_m[n], new_v[n] = (a.T[None] for a in (g, *outs))
        else:
            g = _unshard_rows(n, g_rows, shards[n].shape)
            outs = _adamw(shards[n], g, mom[n][0], var[n][0], "adamw_" + n)
            grad[n], delta[n], new_m[n], new_v[n] = (a[None] for a in (g, *outs))

    def flat2d(a):
        return a.reshape(-1, a.shape[-1])

    operands = ([flat2d(_working(n, d[n])) for n in SMALL] for d in (wts, mom, var))
    w2d, m2d, v2d = operands
    outs = _adamw_many(w2d, [flat2d(g_small[n]) for n in SMALL], m2d, v2d, "adamw_small")
    for out, arrays in zip((grad, delta, new_m, new_v), ([g_small[n] for n in SMALL], *outs)):
        out.update({n: _declared(n, a.reshape(g_small[n].shape)) for n, a in zip(SMALL, arrays)})
    return (loss, dx[None], *(d[n] for d in (grad, delta, new_m, new_v) for n in NAMES))
```

```python
import functools
import math

import jax
import jax.numpy as jnp
from jax import lax
from jax.experimental import pallas as pl
from jax.experimental.pallas import tpu as pltpu

F32, BF16 = jnp.float32, jnp.bfloat16

D_MODEL = 1024
D_FF = 2816
N_DEV = 8
S5_WIDTH, S5_GROUPS, S5_GROUP, S5_STATE = 512, 32, 16, 64
S5_BLOCKS = 4
S5_BSTATE = 512
S5_SEGS = 8
GLA_HEADS, GLA_DK, GLA_DV = 4, 64, 128
GLA_KEY, GLA_VAL, GLA_RANK, GLA_CHUNK = 256, 512, 16, 64
GLA_TAU = 16.0
GLA_STEP_CHUNKS = 4
EPS = 1e-6
IN_SIZES = (512, 256, 256, 512, 512, 16, 1024, 1024)
IN_OFFS = tuple(sum(IN_SIZES[:i]) for i in range(len(IN_SIZES)))
IN_COLS = sum(IN_SIZES)
ADAM_LR, ADAM_B1, ADAM_B2, ADAM_EPS, ADAM_WD, ADAM_STEP = 0.001, 0.9, 0.999, 1e-08, 0.01, 10
GELU_C0 = math.sqrt(2.0 / math.pi)
GELU_C1 = 0.044715

FFN_FT = 256
VMEM_LIMIT_BYTES = 56 * 1024 * 1024

VMEM_FULL = pl.BlockSpec(memory_space=pltpu.VMEM)
ANY = pl.BlockSpec(memory_space=pl.ANY)


def _cparams(n_grid):
    return pltpu.CompilerParams(dimension_semantics=("arbitrary",) * n_grid, vmem_limit_bytes=VMEM_LIMIT_BYTES)


def _tile(t):
    return 512 if t >= 1024 else t // 2


def _nn(a, b):
    return jnp.dot(a, b, preferred_element_type=F32)


def _nt(a, b):
    return lax.dot_general(a, b, (((1,), (1,)), ((), ())), preferred_element_type=F32)


def _tn(a, b):
    return lax.dot_general(a, b, (((0,), (0,)), ((), ())), preferred_element_type=F32)


def _rms_parts(x):
    r = lax.rsqrt(jnp.mean(x * x, axis=-1, keepdims=True) + EPS)
    return x * r, r


def _rms_bwd(dn, g, xhat, r):
    dxh = dn * g
    dx = r * (dxh - xhat * jnp.mean(dxh * xhat, axis=-1, keepdims=True))
    return dx, jnp.sum(dn * xhat, axis=0, keepdims=True)


def _peers():
    x, y, c = lax.axis_index("x"), lax.axis_index("y"), lax.axis_index("c")
    out = []
    for k in range(1, N_DEV):
        px = 1 - x if k & 4 else x
        py = 1 - y if k & 2 else y
        pc = 1 - c if k & 1 else c
        out.append(((px, py, pc), 4 * px + 2 * py + pc))
    return 4 * x + 2 * y + c, out


def _exchange_copies(src_refs, out_refs, send_sems, recv_sems, local_sems, scatter, with_recvs):
    me, peers = _peers()
    locals_, sends, recvs = [], [], []
    for a, (src_ref, out_ref) in enumerate(zip(src_refs, out_refs)):
        def mine(idx, src_ref=src_ref):
            return src_ref.at[idx] if scatter else src_ref

        locals_.append(pltpu.make_async_copy(mine(me), out_ref.at[me], local_sems.at[a]))
        for k, (dev, idx) in enumerate(peers):
            sends.append(pltpu.make_async_remote_copy(
                src_ref=mine(idx), dst_ref=out_ref.at[me], send_sem=send_sems.at[a, k], recv_sem=recv_sems.at[a, k],
                device_id=dev, device_id_type=pl.DeviceIdType.MESH))
            if with_recvs:
                recvs.append(pltpu.make_async_remote_copy(
                    src_ref=mine(idx), dst_ref=out_ref.at[idx], send_sem=send_sems.at[a, k],
                    recv_sem=recv_sems.at[a, k], device_id=dev, device_id_type=pl.DeviceIdType.MESH))
    return locals_, sends, recvs


def _remote(src, dst, send_sems, recv_sems, a, k, dev):
    return pltpu.make_async_remote_copy(src_ref=src, dst_ref=dst, send_sem=send_sems.at[a, k],
                                        recv_sem=recv_sems.at[a, k], device_id=dev,
                                        device_id_type=pl.DeviceIdType.MESH)


def _gather_places():
    x, y, c = lax.axis_index("x"), lax.axis_index("y"), lax.axis_index("c")

    def at(chip, core):
        return 4 * chip[0] + 2 * chip[1] + core

    xn, yn, diag = (1 - x, y), (x, 1 - y), (1 - x, 1 - y)
    relay = (x * (1 - c) + (1 - x) * c, (1 - y) * (1 - c) + y * c)
    passed = ((1 - x) * (1 - c) + x * c, y * (1 - c) + (1 - y) * c)
    return dict(sibling=(x, y, 1 - c), me=at((x, y), c), sib=at((x, y), 1 - c), c=c, at=at,
                xn=xn, yn=yn, diag=diag, relay=relay, passed=passed)


def _gather_start(src_refs, out_refs, send_sems, recv_sems, local_sems):
    p = _gather_places()
    for a, (src, out) in enumerate(zip(src_refs, out_refs)):
        mine = out.at[p["me"]]
        pltpu.make_async_copy(src, mine, local_sems.at[a]).start()
        _remote(src, mine, send_sems, recv_sems, a, 0, p["sibling"]).start()
        _remote(src, mine, send_sems, recv_sems, a, 1, (*p["xn"], p["c"])).start()
        _remote(src, mine, send_sems, recv_sems, a, 2, (*p["yn"], p["c"])).start()


def _gather_forward(src_refs, out_refs, send_sems, recv_sems, local_sems):
    p = _gather_places()
    c, at = p["c"], p["at"]
    for a, (src, out) in enumerate(zip(src_refs, out_refs)):
        from_x, from_y = out.at[at(p["xn"], c)], out.at[at(p["yn"], c)]
        _remote(src, from_x, send_sems, recv_sems, a, 1, (*p["xn"], c)).wait_recv()
        _remote(src, from_y, send_sems, recv_sems, a, 2, (*p["yn"], c)).wait_recv()
        relayed = out.at[at(p["passed"], c)]
        _remote(relayed, relayed, send_sems, recv_sems, a, 3, (*p["relay"], c)).start()
        _remote(from_x, from_x, send_sems, recv_sems, a, 4, p["sibling"]).start()
        _remote(from_y, from_y, send_sems, recv_sems, a, 5, p["sibling"]).start()


def _gather_diagonal(src_refs, out_refs, send_sems, recv_sems, local_sems):
    p = _gather_places()
    for a, (src, out) in enumerate(zip(src_refs, out_refs)):
        from_diag = out.at[p["at"](p["diag"], p["c"])]
        _remote(src, from_diag, send_sems, recv_sems, a, 3, (*p["relay"], p["c"])).wait_recv()
        _remote(from_diag, from_diag, send_sems, recv_sems, a, 6, p["sibling"]).start()


def _gather_finish(src_refs, out_refs, send_sems, recv_sems, local_sems):
    p = _gather_places()
    c, at, sibling = p["c"], p["at"], p["sibling"]
    arrays = list(enumerate(zip(src_refs, out_refs)))
    for a, (src, out) in arrays:
        _remote(src, out.at[p["sib"]], send_sems, recv_sems, a, 0, sibling).wait_recv()
        for k, chip in ((4, p["xn"]), (5, p["yn"]), (6, p["diag"])):
            _remote(src, out.at[at(chip, 1 - c)], send_sems, recv_sems, a, k, sibling).wait_recv()
        for k in range(N_DEV - 1):
            _remote(src, out.at[p["me"]], send_sems, recv_sems, a, k, sibling).wait_send()
        pltpu.make_async_copy(src, out.at[p["me"]], local_sems.at[a]).wait()


def _exchange_start(*refs, scatter):
    locals_, sends, _ = _exchange_copies(*refs, scatter=scatter, with_recvs=False)
    for cp in locals_ + sends:
        cp.start()


def _exchange_wait(*refs, scatter):
    locals_, sends, recvs = _exchange_copies(*refs, scatter=scatter, with_recvs=True)
    for cp in recvs:
        cp.wait_recv()
    for cp in sends:
        cp.wait_send()
    for cp in locals_:
        cp.wait()


def _halves_places():
    x, y, c = lax.axis_index("x"), lax.axis_index("y"), lax.axis_index("c")
    flips = [(1 - x, y), (x, 1 - y), (1 - x, 1 - y)]
    return (x, y, 1 - c), c, 2 * x + y, [((fx, fy, c), 2 * fx + fy) for fx, fy in flips]


def _pair_start(src_refs, out_refs, send_sems, recv_sems, local_sems):
    sibling, c, _, _ = _halves_places()
    for a, (src, out) in enumerate(zip(src_refs, out_refs)):
        for i in range(4):
            _remote(src.at[2 * i + 1 - c], out.at[i], send_sems, recv_sems, a, i, sibling).start()


def _pair_finish(src_refs, out_refs, send_sems, recv_sems, local_sems):
    sibling, c, _, _ = _halves_places()
    for a, (src, out) in enumerate(zip(src_refs, out_refs)):
        for i in range(4):
            _remote(src.at[2 * i + 1 - c], out.at[i], send_sems, recv_sems, a, i, sibling).wait()


def _chips_start(src_refs, out_refs, send_sems, recv_sems, local_sems):
    _, _, chip, others = _halves_places()
    for a, (src, out) in enumerate(zip(src_refs, out_refs)):
        pltpu.make_async_copy(src.at[chip], out.at[chip], local_sems.at[a]).start()
        for k, (dev, their_chip) in enumerate(others):
            _remote(src.at[their_chip], out.at[chip], send_sems, recv_sems, a, k, dev).start()


def _chips_finish(src_refs, out_refs, send_sems, recv_sems, local_sems):
    _, _, chip, others = _halves_places()
    for a, (src, out) in enumerate(zip(src_refs, out_refs)):
        for k, (dev, their_chip) in enumerate(others):
            _remote(src.at[their_chip], out.at[their_chip], send_sems, recv_sems, a, k, dev).wait_recv()
        for k, (dev, their_chip) in enumerate(others):
            _remote(src.at[their_chip], out.at[chip], send_sems, recv_sems, a, k, dev).wait_send()
        pltpu.make_async_copy(src.at[chip], out.at[chip], local_sems.at[a]).wait()


EXCHANGES = {
    "gather": (_gather_start, _gather_forward, _gather_diagonal, _gather_finish, N_DEV, False),
    "scatter": (functools.partial(_exchange_start, scatter=True), None, None,
                functools.partial(_exchange_wait, scatter=True), N_DEV, True),
    "pair": (_pair_start, None, None, _pair_finish, 4, True),
    "chips": (_chips_start, None, None, _chips_finish, 4, True),
}


def _exchange_sems(n_arrays):
    return [pltpu.SemaphoreType.DMA((n_arrays, N_DEV - 1)), pltpu.SemaphoreType.DMA((n_arrays, N_DEV - 1)),
            pltpu.SemaphoreType.DMA((n_arrays,))]


def _exchange_shapes(srcs, kind):
    lead, slabbed = EXCHANGES[kind][4:]
    return [jax.ShapeDtypeStruct((lead,) + tuple(s.shape[1:] if slabbed else s.shape), s.dtype) for s in srcs]


def _carries(carry):
    if carry is None:
        return []
    return [carry] if isinstance(carry, tuple) else list(carry)


def _call(body, *, name, grid, in_specs, out_specs, out_shape, args, scratch_shapes=(), carry=None):
    n_in, n_out, n_scr = len(in_specs), len(out_specs), len(scratch_shapes)
    groups = _carries(carry)
    sizes = [len(arrays) for arrays, _ in groups]
    nc = sum(sizes)

    def wrapped(*refs):
        ins, refs = refs[:n_in], refs[n_in:]
        csrc, refs = refs[:nc], refs[nc:]
        outs, refs = refs[:n_out], refs[n_out:]
        cland, refs = refs[:nc], refs[nc:]
        scr, sems = refs[:n_scr], refs[n_scr:]

        def run(phase):
            at = 0
            for gi, ((_, kind), size) in enumerate(zip(groups, sizes)):
                if EXCHANGES[kind][phase] is not None:
                    EXCHANGES[kind][phase](csrc[at:at + size], cland[at:at + size], *sems[3 * gi:3 * gi + 3])
                at += size

        last = pl.program_id(0) == grid[0] - 1
        if nc:
            pl.when(pl.program_id(0) == 0)(functools.partial(run, 0))
            pl.when(pl.program_id(0) == max(grid[0] - 2, 0))(functools.partial(run, 1))
            pl.when(last)(functools.partial(run, 2))
        if body is not None:
            body(*ins, *outs, *scr)
        if nc:
            pl.when(last)(functools.partial(run, 3))

    res = pl.pallas_call(
        wrapped, name=name, grid=grid,
        in_specs=list(in_specs) + [ANY] * nc, out_specs=list(out_specs) + [ANY] * nc,
        out_shape=list(out_shape) + [s for arrays, kind in groups for s in _exchange_shapes(arrays, kind)],
        scratch_shapes=list(scratch_shapes) + [s for size in sizes for s in _exchange_sems(size)],
        compiler_params=_cparams(1),
    )(*args, *[a for arrays, _ in groups for a in arrays])
    return res[:n_out], res[n_out:]


def _row_tile(tm, d):
    return pl.BlockSpec((tm, d), lambda i: (i, 0))


def _acc_row(d):
    return pl.BlockSpec((1, d), lambda i: (0, 0))


def _ffn_body(x_ref, g_ref, w1_ref, w3_ref, w2_ref, acc_ref, a_ref, b_ref, n_ref):
    xv = x_ref[...]
    xhat, _ = _rms_parts(xv)
    n = (xhat * g_ref[...]).astype(BF16)
    n_ref[...] = n
    acc_ref[...] = xv

    def fstep(f, c):
        rows = pl.ds(pl.multiple_of(f * FFN_FT, FFN_FT), FFN_FT)
        a = _nt(n, w1_ref[rows, :])
        b = _nt(n, w3_ref[rows, :])
        a_ref[f] = a.astype(BF16)
        b_ref[f] = b.astype(BF16)
        s = (a * jax.nn.sigmoid(a) * b).astype(BF16)
        acc_ref[...] += 0.5 * _nn(s, w2_ref[rows, :])
        return c

    lax.fori_loop(0, D_FF // FFN_FT, fstep, 0, unroll=True)


def _ffn_fwd(x, g, w1t, w3t, w2, name, carry=None):
    t = x.shape[0]
    tm = _tile(t)
    nf = D_FF // FFN_FT
    blk3 = pl.BlockSpec((nf, tm, FFN_FT), lambda i: (0, i, 0))
    sh3 = jax.ShapeDtypeStruct((nf, t, FFN_FT), BF16)
    (h, a3, b3, n), landed = _call(
        functools.partial(_ffn_body), name=name, grid=(t // tm,),
        in_specs=[_row_tile(tm, D_MODEL), _acc_row(D_MODEL), VMEM_FULL, VMEM_FULL, VMEM_FULL],
        out_specs=[_row_tile(tm, D_MODEL), blk3, blk3, _row_tile(tm, D_MODEL)],
        out_shape=[jax.ShapeDtypeStruct((t, D_MODEL), F32), sh3, sh3, jax.ShapeDtypeStruct((t, D_MODEL), BF16)],
        args=(x, g, w1t, w3t, w2), carry=carry)
    return h, (a3, b3, n), landed


def _ffn_fwd_head(x, g, w1t, w3t, w2, gf, target, name):
    t = x.shape[0]
    tm = _tile(t)
    nf = D_FF // FFN_FT

    def body(x_ref, g_ref, w1_ref, w3_ref, w2_ref, gf_ref, t_ref, loss_ref, dh_ref, dgf_ref, a_ref, b_ref, n_ref, acc):
        _ffn_body(x_ref, g_ref, w1_ref, w3_ref, w2_ref, acc, a_ref, b_ref, n_ref)
        _head_math(acc[...], gf_ref[...], t_ref[...], loss_ref, dh_ref, dgf_ref)

    blk3 = pl.BlockSpec((nf, tm, FFN_FT), lambda i: (0, i, 0))
    sh3 = jax.ShapeDtypeStruct((nf, t, FFN_FT), BF16)
    (loss, dh, dgf, a3, b3, n), _ = _call(
        body, name=name, grid=(t // tm,),
        in_specs=[_row_tile(tm, D_MODEL), _acc_row(D_MODEL), VMEM_FULL, VMEM_FULL, VMEM_FULL, _acc_row(D_MODEL),
                  _row_tile(tm, D_MODEL)],
        out_specs=[pl.BlockSpec((1, 1), lambda i: (0, 0)), _row_tile(tm, D_MODEL), _acc_row(D_MODEL), blk3, blk3,
                   _row_tile(tm, D_MODEL)],
        out_shape=[jax.ShapeDtypeStruct((1, 1), F32), jax.ShapeDtypeStruct((t, D_MODEL), F32),
                   jax.ShapeDtypeStruct((1, D_MODEL), F32), sh3, sh3, jax.ShapeDtypeStruct((t, D_MODEL), BF16)],
        scratch_shapes=[pltpu.VMEM((tm, D_MODEL), F32)],
        args=(x, g, w1t, w3t, w2, gf, target))
    return loss, dh, dgf, (a3, b3, n)


def _head_math(h, gv, target, loss_ref, dh_ref, dg_ref):
    i = pl.program_id(0)
    xhat, r = _rms_parts(h)
    err = xhat * gv - target
    dx, dg = _rms_bwd(err * (1.0 / D_MODEL), gv, xhat, r)
    dh_ref[...] = dx

    @pl.when(i == 0)
    def _():
        loss_ref[...] = jnp.zeros_like(loss_ref)
        dg_ref[...] = jnp.zeros_like(dg_ref)

    loss_ref[...] += (0.5 / D_MODEL) * jnp.sum(jnp.sum(err * err, axis=1, keepdims=True), axis=0, keepdims=True)
    dg_ref[...] += dg


def _ffn_bwd(x, dh, g, a3, b3, w1t, w3t, w2, name, carry=None):
    t = x.shape[0]
    tm = _tile(t) // 2
    nf = D_FF // FFN_FT

    def body(x_ref, dh_ref, g_ref, a_ref, b_ref, w1_ref, w3_ref, w2_ref,
             dx_ref, dg_ref, da_ref, db_ref, s_ref, dhh_ref, dn_acc):
        i = pl.program_id(0)
        xv = x_ref[...]
        gv = g_ref[...]
        xhat, r = _rms_parts(xv)
        dhv = dh_ref[...]
        dhh = (0.5 * dhv).astype(BF16)
        dhh_ref[...] = dhh
        dn_acc[...] = jnp.zeros_like(dn_acc)

        def fstep(f, c):
            rows = pl.ds(f * FFN_FT, FFN_FT)
            w1c, w3c, w2c = w1_ref[rows, :], w3_ref[rows, :], w2_ref[rows, :]
            a = a_ref[f].astype(F32)
            b = b_ref[f].astype(F32)
            sg = jax.nn.sigmoid(a)
            sl = a * sg
            ds = _nt(dhh, w2c)
            da = (ds * b * sg * (1.0 + a * (1.0 - sg))).astype(BF16)
            db = (ds * sl).astype(BF16)
            s_ref[f] = (sl * b).astype(BF16)
            da_ref[f] = da
            db_ref[f] = db
            return c

        def nstep(f, c):
            rows = pl.ds(f * FFN_FT, FFN_FT)
            dn_acc[...] += _nn(da_ref[f], w1_ref[rows, :]) + _nn(db_ref[f], w3_ref[rows, :])
            return c

        for f in range(nf + 1):
            if f < nf:
                fstep(f, 0)
            if f:
                nstep(f - 1, 0)
        dx, dg = _rms_bwd(dn_acc[...], gv, xhat, r)
        dx_ref[...] = dhv + dx

        @pl.when(i == 0)
        def _():
            dg_ref[...] = jnp.zeros_like(dg_ref)

        dg_ref[...] += dg

    blk3 = pl.BlockSpec((nf, tm, FFN_FT), lambda i: (0, i, 0))
    sh3 = jax.ShapeDtypeStruct((nf, t, FFN_FT), BF16)
    return _call(
        body, name=name, grid=(t // tm,),
        in_specs=[_row_tile(tm, D_MODEL), _row_tile(tm, D_MODEL), _acc_row(D_MODEL), blk3, blk3,
                  VMEM_FULL, VMEM_FULL, VMEM_FULL],
        out_specs=[_row_tile(tm, D_MODEL), _acc_row(D_MODEL), blk3, blk3, blk3, _row_tile(tm, D_MODEL)],
        out_shape=[jax.ShapeDtypeStruct((t, D_MODEL), F32), jax.ShapeDtypeStruct((1, D_MODEL), F32), sh3, sh3, sh3,
                   jax.ShapeDtypeStruct((t, D_MODEL), BF16)],
        scratch_shapes=[pltpu.VMEM((tm, D_MODEL), F32)],
        args=(x, dh, g, a3, b3, w1t, w3t, w2), carry=carry)


def _mm_tn(a, b, name, carry=None):
    t, n = b.shape
    kc = min(512, t)
    if a.ndim == 3:
        nb, _, tb = a.shape
        a_spec = pl.BlockSpec((1, t, tb), lambda i: (i, 0, 0))
    else:
        m = a.shape[1]
        tb = min(m, 256)
        nb = m // tb
        a_spec = pl.BlockSpec((t, tb), lambda i: (0, i))
    three_d = a.ndim == 3

    def body(a_ref, b_ref, o_ref, acc):
        acc[...] = jnp.zeros_like(acc)

        def kstep(k, c):
            rows = pl.ds(pl.multiple_of(k * kc, kc), kc)
            av = a_ref[0, rows, :] if three_d else a_ref[rows, :]
            acc[...] += _tn(av.astype(BF16), b_ref[rows, :])
            return c

        lax.fori_loop(0, t // kc, kstep, 0, unroll=True)
        o_ref[...] = acc[...].astype(BF16)

    (out,), landed = _call(
        body, name=name, grid=(nb,),
        in_specs=[a_spec, VMEM_FULL],
        out_specs=[pl.BlockSpec((tb, n), lambda i: (i, 0))],
        out_shape=[jax.ShapeDtypeStruct((nb * tb, n), BF16)],
        scratch_shapes=[pltpu.VMEM((tb, n), F32)],
        args=(a, b), carry=carry)
    return (out, landed) if carry is not None else out


MM_TB = 256


def _mm_tn_many(arrays, b, name):
    t, n = b.shape
    kc = min(512, t)
    counts = [a.shape[1] // MM_TB for a in arrays]
    starts = [sum(counts[:k]) for k in range(len(arrays))]

    def spec(start, count):
        return pl.BlockSpec((t, MM_TB), lambda i: (0, jnp.clip(i - start, 0, count - 1)))

    def body(*refs):
        a_refs, (b_ref, o_ref, acc) = refs[:len(arrays)], refs[len(arrays):]
        i = pl.program_id(0)
        for a_ref, start, count in zip(a_refs, starts, counts):
            @pl.when((i >= start) & (i < start + count))
            def _(a_ref=a_ref):
                acc[...] = jnp.zeros_like(acc)

                def kstep(k, c):
                    rows = pl.ds(pl.multiple_of(k * kc, kc), kc)
                    acc[...] += _tn(a_ref[rows, :].astype(BF16), b_ref[rows, :])
                    return c

                lax.fori_loop(0, t // kc, kstep, 0, unroll=True)
                o_ref[...] = acc[...].astype(BF16)

    return pl.pallas_call(
        body, name=name, grid=(sum(counts),),
        in_specs=[spec(s, c) for s, c in zip(starts, counts)] + [VMEM_FULL],
        out_specs=pl.BlockSpec((MM_TB, n), lambda i: (i, 0)),
        out_shape=jax.ShapeDtypeStruct((sum(counts) * MM_TB, n), BF16),
        scratch_shapes=[pltpu.VMEM((MM_TB, n), F32)],
        compiler_params=_cparams(1),
    )(*arrays, b)


def _mix_pre_fwd(h, g, wint, carry=None):
    t = h.shape[0]
    tm = _tile(t)

    def body(h_ref, g_ref, w_ref, u_ref, *outs):
        xhat, _ = _rms_parts(h_ref[...])
        u = (xhat * g_ref[...]).astype(BF16)
        u_ref[...] = u
        for o_ref, off, size in zip(outs, IN_OFFS, IN_SIZES):
            o_ref[...] = _nt(u, w_ref[off:off + size, :])

    return _call(
        body, name="mix_pre_fwd", grid=(t // tm,),
        in_specs=[_row_tile(tm, D_MODEL), _acc_row(D_MODEL), VMEM_FULL],
        out_specs=[_row_tile(tm, D_MODEL)] + [_row_tile(tm, s) for s in IN_SIZES],
        out_shape=[jax.ShapeDtypeStruct((t, D_MODEL), BF16)] + [jax.ShapeDtypeStruct((t, s), F32) for s in IN_SIZES],
        args=(h, g, wint), carry=carry)


def _mix_pre_bwd(h, g, wint, dh2, dz, carry=None):
    t = h.shape[0]
    tm = _tile(t)

    def body(h_ref, g_ref, w_ref, dh2_ref, *rest):
        dz_refs, (dh1_ref, dg_ref) = rest[:len(IN_SIZES)], rest[len(IN_SIZES):]
        i = pl.program_id(0)
        gv = g_ref[...]
        xhat, r = _rms_parts(h_ref[...])
        du = jnp.zeros((tm, D_MODEL), F32)
        for dz_ref, off, size in zip(dz_refs, IN_OFFS, IN_SIZES):
            du = du + _nn(dz_ref[...].astype(BF16), w_ref[off:off + size, :])
        dx, dg = _rms_bwd(du, gv, xhat, r)
        dh1_ref[...] = dh2_ref[...] + dx

        @pl.when(i == 0)
        def _():
            dg_ref[...] = jnp.zeros_like(dg_ref)

        dg_ref[...] += dg

    return _call(
        body, name="mix_pre_bwd", grid=(t // tm,),
        in_specs=[_row_tile(tm, D_MODEL), _acc_row(D_MODEL), VMEM_FULL, _row_tile(tm, D_MODEL)]
        + [_row_tile(tm, s) for s in IN_SIZES],
        out_specs=[_row_tile(tm, D_MODEL), _acc_row(D_MODEL)],
        out_shape=[jax.ShapeDtypeStruct((t, D_MODEL), F32), jax.ShapeDtypeStruct((1, D_MODEL), F32)],
        args=(h, g, wint, dh2, *dz), carry=carry)


def _disc_math(lre, lim, ldt, bre, bim):
    dt = jnp.exp(ldt)
    mag = jnp.exp(lre * dt)
    ar = mag * jnp.cos(lim * dt)
    ai = mag * jnp.sin(lim * dt)
    den = lre * lre + lim * lim
    nr = ar - 1.0
    fr = (nr * lre + ai * lim) / den
    fi = (ai * lre - nr * lim) / den
    fr, fi = fr[:, None, :], fi[:, None, :]
    return ar, ai, fr * bre - fi * bim, fr * bim + fi * bre


def _s5_disc(lre, lim, ldt, bre, bim):
    def body(lre_ref, lim_ref, ldt_ref, bre_ref, bim_ref, ar_ref, ai_ref, bbr_ref, bbi_ref):
        ar, ai, bbr, bbi = _disc_math(lre_ref[...], lim_ref[...], ldt_ref[...], bre_ref[...], bim_ref[...])
        ar_ref[...] = ar
        ai_ref[...] = ai
        bbr_ref[...] = bbr
        bbi_ref[...] = bbi

    small = jax.ShapeDtypeStruct(lre.shape, F32)
    big = jax.ShapeDtypeStruct(bre.shape, F32)
    return pl.pallas_call(body, name="s5_disc", out_shape=[small, small, big, big],
                          in_specs=[VMEM_FULL] * 5, out_specs=[VMEM_FULL] * 4)(lre, lim, ldt, bre, bim)


def _s5_disc_bwd(lre, lim, ldt, bre, bim, dar, dai, dbbr, dbbi):
    def body(lre_ref, lim_ref, ldt_ref, bre_ref, bim_ref, dar_ref, dai_ref, dbbr_ref, dbbi_ref,
             glre_ref, glim_ref, gldt_ref, gbre_ref, gbim_ref):
        _, vjp = jax.vjp(_disc_math, lre_ref[...], lim_ref[...], ldt_ref[...], bre_ref[...], bim_ref[...])
        glre, glim, gldt, gbre, gbim = vjp((dar_ref[...], dai_ref[...], dbbr_ref[...], dbbi_ref[...]))
        glre_ref[...] = glre
        glim_ref[...] = glim
        gldt_ref[...] = gldt
        gbre_ref[...] = gbre
        gbim_ref[...] = gbim

    small = jax.ShapeDtypeStruct(lre.shape, F32)
    big = jax.ShapeDtypeStruct(bre.shape, F32)
    return pl.pallas_call(body, name="s5_disc_bwd",
                          out_shape=[small, small, jax.ShapeDtypeStruct(ldt.shape, F32), big, big],
                          in_specs=[VMEM_FULL] * 9, out_specs=[VMEM_FULL] * 5,
                          )(lre, lim, ldt, bre, bim, dar, dai, dbbr, dbbi)


def _cmul(ar, ai, br, bi):
    return ar * br - ai * bi, ar * bi + ai * br


def _cpow(ar, ai, n):
    rr, ri = None, None
    pr, pi = ar, ai
    while n:
        if n & 1:
            rr, ri = (pr, pi) if rr is None else _cmul(rr, ri, pr, pi)
        n >>= 1
        if n:
            pr, pi = _cmul(pr, pi, pr, pi)
    return rr, ri


def _shift_rows(v, down):
    row = lax.broadcasted_iota(jnp.int32, v.shape, 0)
    if down:
        return jnp.where(row == 0, 0.0, pltpu.roll(v, 1, 0))
    return jnp.where(row == S5_SEGS - 1, 0.0, pltpu.roll(v, S5_SEGS - 1, 0))


def _chain_segments(er, ei, pr, pi, down):
    fr, fi = er, ei
    for _ in range(S5_SEGS - 1):
        sr, si = _shift_rows(fr, down), _shift_rows(fi, down)
        mr, mi = _cmul(pr, pi, sr, si)
        fr, fi = er + mr, ei + mi
    return _shift_rows(fr, down), _shift_rows(fi, down)


def _rows_to_scan_order(src_ref, dst_ref, t):
    ls = t // S5_SEGS

    def tile(j, c):
        dst_ref[pl.ds(pl.multiple_of(j * S5_SEGS, S5_SEGS), S5_SEGS), :] = src_ref[pl.ds(j, S5_SEGS, stride=ls), :]
        return c

    lax.fori_loop(0, ls, tile, 0, unroll=8)


def _rows_from_scan_order(src_ref, dst_ref, t):
    ls = t // S5_SEGS
    for s in range(S5_SEGS):
        def tile(jb, c, s=s):
            dst_ref[pl.ds(pl.multiple_of(s * ls + jb * 8, 8), 8), :] = (
                src_ref[pl.ds(jb * 8 * S5_SEGS + s, 8, stride=S5_SEGS), :])
            return c

        lax.fori_loop(0, ls // 8, tile, 0, unroll=8)


def _s5_fwd(ug, bd, ctd, ar4, ai4, dskip, carry=None):
    t = ug.shape[0]
    ls = t // S5_SEGS
    rc = min(512, t)
    ns = S5_BSTATE

    def body(ugn_ref, bd_ref, ct_ref, ar_ref, ai_ref, d_ref, xs_hbm, yn_ref, buf, ug_ref, y_ref, sem):
        cb = pl.program_id(0)
        bdv = bd_ref[0]
        _rows_to_scan_order(ugn_ref, ug_ref, t)

        def mm(i, c):
            rows = pl.ds(pl.multiple_of(i * rc, rc), rc)
            buf[rows, :] = _nn(ug_ref[rows, :].astype(BF16), bdv)
            return c

        lax.fori_loop(0, t // rc, mm, 0, unroll=True)
        arb = jnp.broadcast_to(ar_ref[0], (S5_SEGS, ns))
        aib = jnp.broadcast_to(ai_ref[0], (S5_SEGS, ns))

        def step(j, c, store):
            sr, si = c
            rows = pl.ds(pl.multiple_of(j * S5_SEGS, S5_SEGS), S5_SEGS)
            nr = arb * sr - aib * si + buf[rows, 0:ns]
            ni = arb * si + aib * sr + buf[rows, ns:2 * ns]
            if store:
                buf[rows, 0:ns] = nr
                buf[rows, ns:2 * ns] = ni
            return nr, ni

        zero = jnp.zeros((S5_SEGS, ns), F32)
        er, ei = lax.fori_loop(0, ls, functools.partial(step, store=False), (zero, zero))
        pr, pi = _cpow(arb, aib, ls)
        init = _chain_segments(er, ei, pr, pi, down=True)
        lax.fori_loop(0, ls, functools.partial(step, store=True), init)

        out = pltpu.make_async_copy(buf, xs_hbm.at[cb], sem)
        out.start()
        ctv = ct_ref[0]
        dv = d_ref[...]

        def ymm(i, c):
            rows = pl.ds(pl.multiple_of(i * rc, rc), rc)
            y_ref[rows, :] = _nn(buf[rows, :].astype(BF16), ctv) + dv * ug_ref[rows, :]
            return c

        lax.fori_loop(0, t // rc, ymm, 0, unroll=True)
        _rows_from_scan_order(y_ref, yn_ref, t)
        out.wait()

    return _call(
        body, name="s5_fwd", grid=(S5_BLOCKS,),
        in_specs=[pl.BlockSpec((t, 128), lambda i: (0, i)),
                  pl.BlockSpec((1, 128, 2 * ns), lambda i: (i, 0, 0)),
                  pl.BlockSpec((1, 2 * ns, 128), lambda i: (i, 0, 0)),
                  pl.BlockSpec((1, 1, ns), lambda i: (i, 0, 0)),
                  pl.BlockSpec((1, 1, ns), lambda i: (i, 0, 0)),
                  pl.BlockSpec((1, 128), lambda i: (0, i))],
        out_specs=[ANY, pl.BlockSpec((t, 128), lambda i: (0, i))],
        out_shape=[jax.ShapeDtypeStruct((S5_BLOCKS, t, 2 * ns), F32), jax.ShapeDtypeStruct((t, S5_WIDTH), F32)],
        scratch_shapes=[pltpu.VMEM((t, 2 * ns), F32), pltpu.VMEM((t, 128), F32), pltpu.VMEM((t, 128), F32),
                        pltpu.SemaphoreType.DMA(())],
        args=(ug, bd, ctd, ar4, ai4, dskip), carry=carry)


def _s5_bwd(dy, ug, xs, cd, bdt, ar4, ai4, dskip, carry=None):
    t = ug.shape[0]
    ls = t // S5_SEGS
    rc = min(512, t)
    ns = S5_BSTATE

    def body(dyn_ref, ugn_ref, xs_hbm, cd_ref, bdt_ref, ar_ref, ai_ref, d_ref,
             dugn_ref, dbd_ref, dcd_ref, dd_ref, dar_ref, dai_ref, xbuf, lam, dy_ref, ug_ref, dug_ref, sem):
        cb = pl.program_id(0)
        load = pltpu.make_async_copy(xs_hbm.at[cb], xbuf, sem)
        load.start()
        cdv = cd_ref[0]
        _rows_to_scan_order(dyn_ref, dy_ref, t)
        _rows_to_scan_order(ugn_ref, ug_ref, t)

        def mm(i, c):
            rows = pl.ds(pl.multiple_of(i * rc, rc), rc)
            lam[rows, :] = _nn(dy_ref[rows, :].astype(BF16), cdv)
            return c

        lax.fori_loop(0, t // rc, mm, 0, unroll=True)
        arb = jnp.broadcast_to(ar_ref[0], (S5_SEGS, ns))
        aib = jnp.broadcast_to(ai_ref[0], (S5_SEGS, ns))

        def lam_step(j, lr, li):
            rows = pl.ds(pl.multiple_of(j * S5_SEGS, S5_SEGS), S5_SEGS)
            nr = arb * lr + aib * li + lam[rows, 0:ns]
            ni = arb * li - aib * lr + lam[rows, ns:2 * ns]
            return rows, nr, ni

        def pass1(jj, c):
            _, nr, ni = lam_step(ls - 1 - jj, *c)
            return nr, ni

        zero = jnp.zeros((S5_SEGS, ns), F32)
        er, ei = lax.fori_loop(0, ls, pass1, (zero, zero))
        pr, pi = _cpow(arb, aib, ls)
        init = _chain_segments(er, ei, pr, -pi, down=False)
        load.wait()

        def accumulate(acc, nr, ni, xpr, xpi):
            return acc[0] + nr * xpr + ni * xpi, acc[1] + ni * xpr - nr * xpi

        def pass2(jj, c):
            lr, li, accr, acci = c
            j = ls - 1 - jj
            rows, nr, ni = lam_step(j, lr, li)
            lam[rows, 0:ns] = nr
            lam[rows, ns:2 * ns] = ni
            prev = pl.ds(pl.multiple_of((j - 1) * S5_SEGS, S5_SEGS), S5_SEGS)
            accr, acci = accumulate((accr, acci), nr, ni, xbuf[prev, 0:ns], xbuf[prev, ns:2 * ns])
            return nr, ni, accr, acci

        lr, li, accr, acci = lax.fori_loop(0, ls - 1, pass2, (init[0], init[1], zero, zero))
        rows, nr, ni = lam_step(0, lr, li)
        lam[rows, 0:ns] = nr
        lam[rows, ns:2 * ns] = ni
        last = pl.ds((ls - 1) * S5_SEGS, S5_SEGS)
        accr, acci = accumulate((accr, acci), nr, ni,
                                _shift_rows(xbuf[last, 0:ns], True), _shift_rows(xbuf[last, ns:2 * ns], True))
        dar_ref[0] = jnp.sum(accr, axis=0, keepdims=True)
        dai_ref[0] = jnp.sum(acci, axis=0, keepdims=True)

        bdtv = bdt_ref[0]
        dv = d_ref[...]
        dbd_ref[...] = jnp.zeros_like(dbd_ref)
        dcd_ref[...] = jnp.zeros_like(dcd_ref)
        dd_ref[...] = jnp.zeros_like(dd_ref)

        def tail(i, c):
            rows = pl.ds(pl.multiple_of(i * rc, rc), rc)
            dy = dy_ref[rows, :]
            ug = ug_ref[rows, :]
            lb = lam[rows, :].astype(BF16)
            dug_ref[rows, :] = _nn(lb, bdtv) + dv * dy
            dbd_ref[0] += _tn(ug.astype(BF16), lb)
            dcd_ref[0] += _tn(dy.astype(BF16), xbuf[rows, :].astype(BF16))
            dd_ref[...] += jnp.sum(dy * ug, axis=0, keepdims=True)
            return c

        lax.fori_loop(0, t // rc, tail, 0, unroll=True)
        _rows_from_scan_order(dug_ref, dugn_ref, t)

    chan = pl.BlockSpec((t, 128), lambda i: (0, i))
    dense = pl.BlockSpec((1, 128, 2 * ns), lambda i: (i, 0, 0))
    vec = pl.BlockSpec((1, 1, ns), lambda i: (i, 0, 0))
    return _call(
        body, name="s5_bwd", grid=(S5_BLOCKS,),
        in_specs=[chan, chan, ANY, dense, pl.BlockSpec((1, 2 * ns, 128), lambda i: (i, 0, 0)), vec, vec,
                  pl.BlockSpec((1, 128), lambda i: (0, i))],
        out_specs=[chan, dense, dense, pl.BlockSpec((1, 128), lambda i: (0, i)), vec, vec],
        out_shape=[jax.ShapeDtypeStruct((t, S5_WIDTH), F32),
                   jax.ShapeDtypeStruct((S5_BLOCKS, 128, 2 * ns), F32),
                   jax.ShapeDtypeStruct((S5_BLOCKS, 128, 2 * ns), F32),
                   jax.ShapeDtypeStruct((1, S5_WIDTH), F32),
                   jax.ShapeDtypeStruct((S5_BLOCKS, 1, ns), F32),
                   jax.ShapeDtypeStruct((S5_BLOCKS, 1, ns), F32)],
        scratch_shapes=[pltpu.VMEM((t, 2 * ns), F32), pltpu.VMEM((t, 2 * ns), F32)]
        + [pltpu.VMEM((t, 128), F32)] * 3 + [pltpu.SemaphoreType.DMA(())],
        args=(dy, ug, xs, cd, bdt, ar4, ai4, dskip), carry=carry)


def _cumsum_rows(x, reverse):
    c = x.shape[0]
    row = lax.broadcasted_iota(jnp.int32, x.shape, 0)
    d = 1
    while d < c:
        if reverse:
            x = x + jnp.where(row < c - d, pltpu.roll(x, c - d, 0), 0.0)
        else:
            x = x + jnp.where(row >= d, pltpu.roll(x, d, 0), 0.0)
        d *= 2
    return x


def _gla_common(q, k, alow, wup, bup):
    c = GLA_CHUNK
    pre = _nn(alow.astype(BF16), wup.astype(BF16)) + bup
    la = (jnp.minimum(pre, 0.0) - jnp.log(1.0 + jnp.exp(-jnp.abs(pre)))) * (1.0 / GLA_TAU)
    rr = lax.broadcasted_iota(jnp.int32, (c, c), 0)
    cc = lax.broadcasted_iota(jnp.int32, (c, c), 1)
    tril = (rr >= cc).astype(F32)
    bc = _cumsum_rows(la, reverse=False)
    bl = bc[c - 1:c, :]
    e_pos = jnp.exp(bc)
    e_neg = jnp.exp(-bc)
    e_end = jnp.exp(bl - bc)
    qt = q * (GLA_DK ** -0.5) * e_pos
    kt = k * e_neg
    ke = k * e_end
    lane = lax.broadcasted_iota(jnp.int32, (1, GLA_KEY), 1)
    masks = [((lane >= h * GLA_DK) & (lane < (h + 1) * GLA_DK)).astype(F32) for h in range(GLA_HEADS)]
    return dict(pre=pre, tril=tril, bc=bc, bl=bl, e_pos=e_pos, e_neg=e_neg, e_end=e_end,
                qt=qt, kt=kt, ke=ke, dec=jnp.exp(bl), masks=masks)


def _gla_fwd(q, k, v, alow, wup, bup, carry=None):
    t = q.shape[0]
    c = GLA_CHUNK
    n = t // c
    step = GLA_STEP_CHUNKS * c

    def body(q_ref, k_ref, v_ref, al_ref, wup_ref, bup_ref, o_ref, ss_ref, s_ref):
        i = pl.program_id(0)

        @pl.when(i == 0)
        def _():
            s_ref[...] = jnp.zeros_like(s_ref)

        wup_v, bup_v = wup_ref[...], bup_ref[...]
        s = s_ref[...]
        for j in range(GLA_STEP_CHUNKS):
            tok = slice(j * c, (j + 1) * c)
            m = _gla_common(q_ref[tok, :], k_ref[tok, :], al_ref[tok, :], wup_v, bup_v)
            ss_ref[j] = s
            sb = s.astype(BF16)
            ktb = m["kt"].astype(BF16)
            update = jnp.zeros_like(s)
            for h in range(GLA_HEADS):
                mask = m["masks"][h]
                qm = (m["qt"] * mask).astype(BF16)
                vh = v_ref[tok, h * GLA_DV:(h + 1) * GLA_DV].astype(BF16)
                p = (m["tril"] * _nt(qm, ktb)).astype(BF16)
                o_ref[tok, h * GLA_DV:(h + 1) * GLA_DV] = _nn(p, vh) + _nt(qm, sb)
                update = update + _tn(vh, (m["ke"] * mask).astype(BF16))
            s = m["dec"] * s + update
        s_ref[...] = s

    return _call(
        body, name="gla_fwd", grid=(t // step,),
        in_specs=[_row_tile(step, GLA_KEY), _row_tile(step, GLA_KEY), _row_tile(step, GLA_VAL),
                  _row_tile(step, GLA_RANK), VMEM_FULL, VMEM_FULL],
        out_specs=[_row_tile(step, GLA_VAL), pl.BlockSpec((GLA_STEP_CHUNKS, GLA_DV, GLA_KEY), lambda i: (i, 0, 0))],
        out_shape=[jax.ShapeDtypeStruct((t, GLA_VAL), F32), jax.ShapeDtypeStruct((n, GLA_DV, GLA_KEY), F32)],
        scratch_shapes=[pltpu.VMEM((GLA_DV, GLA_KEY), F32)],
        args=(q, k, v, alow, wup, bup), carry=carry)


def _gla_bwd(q, k, v, alow, wup, bup, ssave, do, carry=None):
    t = q.shape[0]
    c = GLA_CHUNK
    n = t // c

    def body(q_ref, k_ref, v_ref, al_ref, wup_ref, bup_ref, ss_ref, do_ref,
             dq_ref, dk_ref, dv_ref, dal_ref, dwup_ref, dbup_ref, ds_ref):
        i = pl.program_id(0)

        @pl.when(i == 0)
        def _():
            ds_ref[...] = jnp.zeros_like(ds_ref)
            dwup_ref[...] = jnp.zeros_like(dwup_ref)
            dbup_ref[...] = jnp.zeros_like(dbup_ref)

        wup_v, bup_v = wup_ref[...], bup_ref[...]
        ds_in = ds_ref[...]
        dwup = jnp.zeros((GLA_RANK, GLA_KEY), F32)
        dbup = jnp.zeros((1, GLA_KEY), F32)
        for j in reversed(range(GLA_STEP_CHUNKS)):
            tok = slice(j * c, (j + 1) * c)
            alow_v = al_ref[tok, :]
            m = _gla_common(q_ref[tok, :], k_ref[tok, :], alow_v, wup_v, bup_v)
            s = ss_ref[j]
            sb = s.astype(BF16)
            dsb = ds_in.astype(BF16)
            qt, kt, ke = m["qt"], m["kt"], m["ke"]
            ktb = kt.astype(BF16)
            dqt = jnp.zeros((c, GLA_KEY), F32)
            dkt = jnp.zeros((c, GLA_KEY), F32)
            dke = jnp.zeros((c, GLA_KEY), F32)
            update = jnp.zeros_like(ds_in)
            for h in range(GLA_HEADS):
                mask = m["masks"][h]
                qm = (qt * mask).astype(BF16)
                km = (kt * mask).astype(BF16)
                kem = (ke * mask).astype(BF16)
                cols = slice(h * GLA_DV, (h + 1) * GLA_DV)
                vh = v_ref[tok, cols].astype(BF16)
                doh = do_ref[tok, cols].astype(BF16)
                p = (m["tril"] * _nt(qm, ktb)).astype(BF16)
                dp = (m["tril"] * _nt(doh, vh)).astype(BF16)
                dv_ref[tok, cols] = (_tn(p, doh) + _nt(kem, dsb)).astype(BF16)
                dqt = dqt + _nn(dp, km) + _nn(doh, sb) * mask
                dkt = dkt + _tn(dp, qm)
                dke = dke + _nn(vh, dsb) * mask
                update = update + _tn(doh, qm)
            ddec = jnp.sum(ds_in * s, axis=0, keepdims=True)
            dq_ref[tok, :] = (dqt * m["e_pos"] * (GLA_DK ** -0.5)).astype(BF16)
            dk_ref[tok, :] = (dkt * m["e_neg"] + dke * m["e_end"]).astype(BF16)
            dkeke = dke * ke
            dbl = jnp.sum(dkeke, axis=0, keepdims=True) + ddec * m["dec"]
            last = (lax.broadcasted_iota(jnp.int32, (c, 1), 0) == c - 1).astype(F32)
            dla = _cumsum_rows(dqt * qt - dkt * kt - dkeke + last * dbl, reverse=True)
            dpre = dla * (1.0 / GLA_TAU) * jax.nn.sigmoid(-m["pre"])
            dpb = dpre.astype(BF16)
            dal_ref[tok, :] = _nt(dpb, wup_v.astype(BF16)).astype(BF16)
            dwup = dwup + _tn(alow_v.astype(BF16), dpb)
            dbup = dbup + jnp.sum(dpre, axis=0, keepdims=True)
            ds_in = m["dec"] * ds_in + update
        ds_ref[...] = ds_in
        dwup_ref[...] += dwup
        dbup_ref[...] += dbup

    step = GLA_STEP_CHUNKS * c
    nsteps = t // step

    def rev(d):
        return pl.BlockSpec((step, d), lambda i: (nsteps - 1 - i, 0))

    return _call(
        body, name="gla_bwd", grid=(nsteps,),
        in_specs=[rev(GLA_KEY), rev(GLA_KEY), rev(GLA_VAL), rev(GLA_RANK), VMEM_FULL, VMEM_FULL,
                  pl.BlockSpec((GLA_STEP_CHUNKS, GLA_DV, GLA_KEY), lambda i: (nsteps - 1 - i, 0, 0)), rev(GLA_VAL)],
        out_specs=[rev(GLA_KEY), rev(GLA_KEY), rev(GLA_VAL), rev(GLA_RANK),
                   pl.BlockSpec((GLA_RANK, GLA_KEY), lambda i: (0, 0)), _acc_row(GLA_KEY)],
        out_shape=[jax.ShapeDtypeStruct((t, GLA_KEY), BF16), jax.ShapeDtypeStruct((t, GLA_KEY), BF16),
                   jax.ShapeDtypeStruct((t, GLA_VAL), BF16), jax.ShapeDtypeStruct((t, GLA_RANK), BF16),
                   jax.ShapeDtypeStruct((GLA_RANK, GLA_KEY), F32), jax.ShapeDtypeStruct((1, GLA_KEY), F32)],
        scratch_shapes=[pltpu.VMEM((GLA_DV, GLA_KEY), F32)],
        args=(q, k, v, alow, wup, bup, ssave, do), carry=carry)


def _post_narrow(y, o, r, wg, bg, gn):
    y2 = y * y
    th = jnp.tanh(GELU_C0 * (y + GELU_C1 * y * y2))
    z5 = 0.5 * y * (1.0 + th)
    z5b = z5.astype(BF16)
    gate = jax.nn.sigmoid(_nn(z5b, wg) + bg)
    ys5 = z5 * gate
    rs, on = [], []
    for h in range(GLA_HEADS):
        oh = o[:, h * GLA_DV:(h + 1) * GLA_DV]
        rh = lax.rsqrt(jnp.mean(oh * oh, axis=-1, keepdims=True) + EPS)
        rs.append(rh)
        on.append(oh * rh)
    on = jnp.concatenate(on, axis=-1)
    sr = jax.nn.sigmoid(r)
    silu_r = r * sr
    ygla = on * gn * silu_r
    return dict(y2=y2, th=th, z5=z5, z5b=z5b, gate=gate, ys5b=ys5.astype(BF16), yglab=ygla.astype(BF16), rs=rs,
                on=on, sr=sr, silu_r=silu_r)


def _post_math(y, o, r, gs5, ggla, wg, bg, gn, ps5t, pglat):
    m = _post_narrow(y, o, r, wg, bg, gn)
    m5 = _nt(m["ys5b"], ps5t)
    mg = _nt(m["yglab"], pglat)
    s5g, glag = jax.nn.sigmoid(gs5), jax.nn.sigmoid(ggla)
    merged = s5g * m5 + glag * mg
    m.update(m5=m5, mg=mg, s5g=s5g, glag=glag, mergedb=merged.astype(BF16))
    return m


def _mix_post_fwd(y, o, r, gs5, ggla, h1, wg, bg, gn, ps5t, pglat, wout, carry=None):
    t = o.shape[0]
    tm = _tile(t)

    def body(y_ref, o_ref, r_ref, gs5_ref, ggla_ref, h1_ref, wg_ref, bg_ref, gn_ref, ps_ref, pg_ref, wo_ref, h2_ref):
        m = _post_math(y_ref[...], o_ref[...], r_ref[...], gs5_ref[...], ggla_ref[...],
                       wg_ref[...], bg_ref[...], gn_ref[...], ps_ref[...], pg_ref[...])
        h2_ref[...] = h1_ref[...] + _nn(m["mergedb"], wo_ref[...])

    (h2,), landed = _call(
        body, name="mix_post_fwd", grid=(t // tm,),
        in_specs=[_row_tile(tm, 512)] * 3 + [_row_tile(tm, D_MODEL)] * 3
        + [VMEM_FULL, _acc_row(512), _acc_row(512), VMEM_FULL, VMEM_FULL, VMEM_FULL],
        out_specs=[_row_tile(tm, D_MODEL)],
        out_shape=[jax.ShapeDtypeStruct((t, D_MODEL), F32)],
        args=(y, o, r, gs5, ggla, h1, wg, bg, gn, ps5t, pglat, wout), carry=carry)
    return h2, landed


def _mix_post_bwd(y, o, r, gs5, ggla, dh2, wg, bg, gn, ps5t, pglat, wout, carry=None):
    t = o.shape[0]
    tm = _tile(t) // 2

    def body(y_ref, o_ref, r_ref, gs5_ref, ggla_ref, dh2_ref, wg_ref, bg_ref, gn_ref, ps_ref, pg_ref, wo_ref,
             dy_ref, do_ref, dr_ref, dgs5_ref, dggla_ref, dbg_ref, dgn_ref,
             z5b_ref, dgp_ref, ys5b_ref, dm5b_ref, yglab_ref, dmgb_ref, mergedb_ref, dh2b_ref):
        i = pl.program_id(0)
        yv, ov, rv = y_ref[...], o_ref[...], r_ref[...]
        wg, gn = wg_ref[...], gn_ref[...]
        m = _post_narrow(yv, ov, rv, wg, bg_ref[...], gn)
        dh2b = dh2_ref[...].astype(BF16)
        dys5 = jnp.zeros((tm, S5_WIDTH), F32)
        dygla = jnp.zeros((tm, GLA_VAL), F32)
        for half in range(2):
            cols = slice(half * (D_MODEL // 2), (half + 1) * (D_MODEL // 2))
            ps_h, pg_h = ps_ref[cols, :], pg_ref[cols, :]
            dmerged = _nt(dh2b, wo_ref[cols, :])
            m5 = _nt(m["ys5b"], ps_h)
            mg = _nt(m["yglab"], pg_h)
            s5g, glag = jax.nn.sigmoid(gs5_ref[:, cols]), jax.nn.sigmoid(ggla_ref[:, cols])
            dgs5_ref[:, cols] = (dmerged * m5 * s5g * (1.0 - s5g)).astype(BF16)
            dggla_ref[:, cols] = (dmerged * mg * glag * (1.0 - glag)).astype(BF16)
            dm5b = (dmerged * s5g).astype(BF16)
            dmgb = (dmerged * glag).astype(BF16)
            dm5b_ref[:, cols] = dm5b
            dmgb_ref[:, cols] = dmgb
            mergedb_ref[:, cols] = (s5g * m5 + glag * mg).astype(BF16)
            dys5 = dys5 + _nn(dm5b, ps_h)
            dygla = dygla + _nn(dmgb, pg_h)
        gate, z5, th = m["gate"], m["z5"], m["th"]
        dgpre = dys5 * z5 * gate * (1.0 - gate)
        dgpb = dgpre.astype(BF16)
        dz5 = dys5 * gate + _nt(dgpb, wg)
        dgelu = 0.5 * (1.0 + th) + 0.5 * yv * (1.0 - th * th) * GELU_C0 * (1.0 + 3.0 * GELU_C1 * m["y2"])
        dy_ref[...] = dz5 * dgelu
        on, sr, silu_r = m["on"], m["sr"], m["silu_r"]
        dr_ref[...] = (dygla * on * gn * sr * (1.0 + rv * (1.0 - sr))).astype(BF16)
        dgn = jnp.sum(dygla * on * silu_r, axis=0, keepdims=True)
        don = dygla * gn * silu_r
        for h in range(GLA_HEADS):
            cols = slice(h * GLA_DV, (h + 1) * GLA_DV)
            donh, onh = don[:, cols], on[:, cols]
            do_ref[:, cols] = (m["rs"][h] * (donh - onh * jnp.mean(donh * onh, axis=-1, keepdims=True))).astype(BF16)

        @pl.when(i == 0)
        def _():
            dbg_ref[...] = jnp.zeros_like(dbg_ref)
            dgn_ref[...] = jnp.zeros_like(dgn_ref)

        dbg_ref[...] += jnp.sum(dgpre, axis=0, keepdims=True)
        dgn_ref[...] += dgn
        z5b_ref[...] = m["z5b"]
        dgp_ref[...] = dgpb
        ys5b_ref[...] = m["ys5b"]
        yglab_ref[...] = m["yglab"]
        dh2b_ref[...] = dh2b

    def f32(d):
        return jax.ShapeDtypeStruct((t, d), F32)

    def b16(d):
        return jax.ShapeDtypeStruct((t, d), BF16)

    widths = (512, 512, 512, 1024, 512, 1024, 1024, 1024)
    return _call(
        body, name="mix_post_bwd", grid=(t // tm,),
        in_specs=[_row_tile(tm, 512)] * 3 + [_row_tile(tm, D_MODEL)] * 3
        + [VMEM_FULL, _acc_row(512), _acc_row(512), VMEM_FULL, VMEM_FULL, VMEM_FULL],
        out_specs=[_row_tile(tm, 512)] * 3 + [_row_tile(tm, D_MODEL)] * 2
        + [_acc_row(512)] * 2 + [_row_tile(tm, w) for w in widths],
        out_shape=[f32(512), b16(512), b16(512), b16(D_MODEL), b16(D_MODEL)]
        + [jax.ShapeDtypeStruct((1, 512), F32)] * 2
        + [b16(w) for w in widths],
        args=(y, o, r, gs5, ggla, dh2, wg, bg, gn, ps5t, pglat, wout), carry=carry)


ADAM_TILE_ELEMS = 256 * 1024


def _adamw(w, g, m, v, name):
    rows, cols = w.shape
    tr = rows
    while tr * cols > ADAM_TILE_ELEMS and tr % 16 == 0:
        tr //= 2

    spec = pl.BlockSpec((tr, cols), lambda i: (i, 0))
    sh = jax.ShapeDtypeStruct((rows, cols), F32)
    return pl.pallas_call(functools.partial(_adamw_body), name=name, grid=(rows // tr,), in_specs=[spec] * 4,
                          out_specs=[spec] * 3, out_shape=[sh] * 3, compiler_params=_cparams(1))(w, g, m, v)


def _adamw_math(w, g, m, v):
    nm = ADAM_B1 * m + (1.0 - ADAM_B1) * g
    nv = ADAM_B2 * v + (1.0 - ADAM_B2) * (g * g)
    m_hat = nm / (1.0 - ADAM_B1 ** ADAM_STEP)
    v_hat = nv / (1.0 - ADAM_B2 ** ADAM_STEP)
    return -ADAM_LR * (m_hat / (jnp.sqrt(v_hat) + ADAM_EPS) + ADAM_WD * w), nm, nv


def _adamw_body(w_ref, g_ref, m_ref, v_ref, d_ref, nm_ref, nv_ref):
    d_ref[...], nm_ref[...], nv_ref[...] = _adamw_math(w_ref[...], g_ref[...], m_ref[...], v_ref[...])


SUM_ADAM_ROWS = 32


def _sum_adamw(landed, ws, ms, vs, name, carry=None):
    k = len(ws)
    n = landed[0].shape[0]
    r, c = ws[0].shape
    tr = SUM_ADAM_ROWS

    def body(*refs):
        lands, (w_refs, m_refs, v_refs), outs = refs[:k], (refs[k:2 * k], refs[2 * k:3 * k], refs[3 * k:4 * k]), refs[4 * k:]
        for i in range(k):
            g = lands[i][0].astype(F32)
            for s in range(1, n):
                g = g + lands[i][s].astype(F32)
            outs[i][...] = g
            outs[k + i][...], outs[2 * k + i][...], outs[3 * k + i][...] = _adamw_math(
                w_refs[i][...], g, m_refs[i][...], v_refs[i][...])

    row = pl.BlockSpec((tr, c), lambda i: (i, 0))
    return _call(
        body, name=name, grid=(r // tr,),
        in_specs=[pl.BlockSpec((n, tr, c), lambda i: (0, i, 0))] * k + [row] * (3 * k),
        out_specs=[row] * (4 * k), out_shape=[jax.ShapeDtypeStruct((r, c), F32)] * (4 * k),
        args=(*landed, *ws, *ms, *vs), carry=carry)


def _adamw_many(ws, gs, ms, vs, name):
    n = len(ws)

    def body(*refs):
        ins, outs = refs[:4 * n], refs[4 * n:]
        for i in range(n):
            _adamw_body(*(ins[j * n + i] for j in range(4)), *(outs[j * n + i] for j in range(3)))

    shapes = [jax.ShapeDtypeStruct(w.shape, F32) for w in ws]
    res = pl.pallas_call(body, name=name, in_specs=[VMEM_FULL] * (4 * n), out_specs=[VMEM_FULL] * (3 * n),
                         out_shape=shapes * 3)(*ws, *gs, *ms, *vs)
    return res[:n], res[n:2 * n], res[2 * n:]


def _exchange(carry, name):
    return _call(None, name=name, grid=(1,), in_specs=[], out_specs=[], out_shape=[], args=(), carry=carry)[1]


def _pair_add(slabs, from_pair, name):
    _, r, cols = slabs.shape

    def body(s_ref, p_ref, o_ref):
        c = lax.axis_index("c")
        mine = jnp.where(c == 0, s_ref[0, 0].astype(F32), s_ref[0, 1].astype(F32))
        o_ref[0] = (mine + p_ref[0].astype(F32)).astype(BF16)

    return pl.pallas_call(
        body, name=name, grid=(4,),
        in_specs=[pl.BlockSpec((1, 2, r, cols), lambda i: (i, 0, 0, 0)), pl.BlockSpec((1, r, cols), lambda i: (i, 0, 0))],
        out_specs=pl.BlockSpec((1, r, cols), lambda i: (i, 0, 0)),
        out_shape=jax.ShapeDtypeStruct((4, r, cols), BF16),
        compiler_params=_cparams(1),
    )(slabs.reshape(4, 2, r, cols), from_pair)


def _sum_slabs(slabs, name):
    n = slabs.shape[0]

    def body(s_ref, o_ref):
        acc = s_ref[0].astype(F32)
        for s in range(1, n):
            acc = acc + s_ref[s].astype(F32)
        o_ref[...] = acc

    return pl.pallas_call(
        body, name=name, in_specs=[VMEM_FULL], out_specs=VMEM_FULL,
        out_shape=jax.ShapeDtypeStruct(slabs.shape[1:], F32),
        compiler_params=pltpu.CompilerParams(vmem_limit_bytes=VMEM_LIMIT_BYTES),
    )(slabs)


BIG = ("ffn1_w1", "ffn1_w3", "ffn1_w2", "w_in", "s5_glu_w", "gla_a_up_w", "proj_s5", "proj_gla", "w_out",
       "ffn2_w1", "ffn2_w3", "ffn2_w2")
GROUPS = (("ffn1_w1", "ffn1_w3", "ffn1_w2"),
          ("w_in", "s5_glu_w", "gla_a_up_w", "proj_s5", "proj_gla", "w_out"),
          ("ffn2_w1", "ffn2_w3", "ffn2_w2"))
W_IN_ROWS = 514
W_IN_PAD = 528
UP_COLS = 32
ROW_ADAM = ("ffn1_w1", "ffn1_w3", "w_in", "ffn2_w1", "ffn2_w3")
COL_SHARDED = ("ffn1_w1", "ffn1_w3", "w_in", "proj_s5", "proj_gla", "ffn2_w1", "ffn2_w3")

SMALL = ("ffn1_norm", "mix_norm", "s5_lambda_re", "s5_lambda_im", "s5_log_dt", "s5_b_re", "s5_b_im", "s5_c_re",
         "s5_c_im", "s5_d", "s5_glu_b", "gla_a_up_b", "gla_out_norm", "ffn2_norm", "final_norm")
SMALL_SHAPES = dict(ffn1_norm=(1, 1024), mix_norm=(1, 1024), s5_lambda_re=(1, 32, 64), s5_lambda_im=(1, 32, 64),
                    s5_log_dt=(1, 32), s5_b_re=(1, 32, 64, 16), s5_b_im=(1, 32, 64, 16), s5_c_re=(1, 32, 16, 64),
                    s5_c_im=(1, 32, 16, 64), s5_d=(1, 32, 16), s5_glu_b=(1, 512), gla_a_up_b=(1, 256),
                    gla_out_norm=(1, 512), ffn2_norm=(1, 1024), final_norm=(1024,))
SMALL_N = sum(math.prod(s) for s in SMALL_SHAPES.values())
SMALL_R = -(-SMALL_N // (64 * 1024)) * 64


def _shard_rows(name, a):
    if name == "gla_a_up_w":
        return jnp.pad(a, ((0, 0), (0, 128 - UP_COLS)))
    if name in COL_SHARDED:
        a = a.T
    if name == "w_in":
        return jnp.pad(a, ((0, W_IN_PAD - W_IN_ROWS), (0, 0)))
    return a.reshape(-1, 1024)


def _unshard_rows(name, rows, shape):
    if name == "gla_a_up_w":
        return rows[:, :UP_COLS]
    if name == "w_in":
        rows = rows[:W_IN_ROWS]
    if name in COL_SHARDED:
        return rows.reshape(shape[1], shape[0]).T
    return rows.reshape(shape)


def _pack_small(vals, loss):
    flat = jnp.concatenate([vals[n].reshape(-1).astype(F32) for n in SMALL] + [loss.reshape(1)])
    return jnp.pad(flat, (0, SMALL_R * 1024 - SMALL_N - 1)).reshape(SMALL_R, 1024)


S5_B = ("s5_b_re", "s5_b_im")


def _working(name, a):
    return a[0].transpose(0, 2, 1) if name in S5_B else a


def _declared(name, a):
    return a.transpose(0, 2, 1)[None] if name in S5_B else a.reshape(SMALL_SHAPES[name])


def _unpack_small(slab):
    flat = slab.reshape(-1)
    out, off = {}, 0
    for n in SMALL:
        size = math.prod(SMALL_SHAPES[n])
        shape = (S5_GROUPS, S5_GROUP, S5_STATE) if n in S5_B else SMALL_SHAPES[n]
        out[n] = flat[off:off + size].reshape(shape)
        off += size
    return out


FULL_SHAPES = dict(w_in=(IN_COLS, D_MODEL), s5_glu_w=(S5_WIDTH, S5_WIDTH), gla_a_up_w=(GLA_RANK, GLA_KEY),
                   proj_s5=(D_MODEL, S5_WIDTH), proj_gla=(D_MODEL, GLA_VAL), w_out=(D_MODEL, D_MODEL))


def _full_weight(name, gathered):
    if name == "gla_a_up_w":
        return gathered[:, :, :UP_COLS].transpose(1, 0, 2).reshape(GLA_RANK, GLA_KEY)
    if name == "w_in":
        gathered = gathered[:, :W_IN_ROWS]
    return gathered.reshape(FULL_SHAPES.get(name, (D_FF, D_MODEL)))


def _grad_slabs(name, g):
    if name == "gla_a_up_w":
        g = g.reshape(GLA_RANK, N_DEV, UP_COLS).transpose(1, 0, 2)
        return jnp.pad(g, ((0, 0), (0, 0), (0, 128 - UP_COLS))).astype(BF16)
    if name == "w_in":
        return jnp.pad(g.reshape(N_DEV, W_IN_ROWS, D_MODEL), ((0, 0), (0, W_IN_PAD - W_IN_ROWS), (0, 0)))
    return g.reshape(N_DEV, -1, 1024)


def _s5_dense(re, im, sign_im):
    eye = jnp.eye(8, dtype=F32)

    def one(a):
        a = a.reshape(S5_BLOCKS, 8, S5_GROUP, S5_STATE)
        return jnp.einsum("cghp,gk->cghkp", a, eye).reshape(S5_BLOCKS, 128, S5_BSTATE)

    return jnp.concatenate([one(re), sign_im * one(im)], axis=-1)


def _s5_undense(d):
    eye = jnp.eye(8, dtype=F32)

    def one(a):
        a = a.reshape(S5_BLOCKS, 8, S5_GROUP, 8, S5_STATE)
        return jnp.einsum("cghkp,gk->cghp", a, eye).reshape(S5_GROUPS, S5_GROUP, S5_STATE)

    return one(d[..., :S5_BSTATE]), one(d[..., S5_BSTATE:])


def _local_step(x, target, p, w, rows=None, opt=None):
    w = dict(w or {})
    landed_grads = {}

    def gather(names):
        return None if rows is None else ([rows[n] for n in names], "gather")

    def gathered(names, landed):
        w.update({n: _full_weight(n, g) for n, g in zip(names, landed)})

    def scatter(names):
        return None if rows is None else ([_grad_slabs(n, big[n]) for n in names], "scatter")

    def scattered(names, landed):
        landed_grads.update(zip(names, landed))

    if rows is not None:
        gathered(GROUPS[0], _exchange(gather(GROUPS[0]), "gather_ffn1"))
    g1, gm, g2 = p["ffn1_norm"], p["mix_norm"], p["ffn2_norm"]
    gf = p["final_norm"].reshape(1, D_MODEL)
    lre, lim = p["s5_lambda_re"][0], p["s5_lambda_im"][0]
    ldt = p["s5_log_dt"][0].reshape(S5_GROUPS, 1)
    bre = p["s5_b_re"][0].transpose(0, 2, 1)
    bim = p["s5_b_im"][0].transpose(0, 2, 1)
    cre, cim = p["s5_c_re"][0], p["s5_c_im"][0]
    dskip = p["s5_d"][0].reshape(1, S5_WIDTH)
    bg, bup, gn = p["s5_glu_b"], p["gla_a_up_b"], p["gla_out_norm"]

    mix_first, mix_rest = ("w_in", "gla_a_up_w"), ("s5_glu_w", "proj_s5", "proj_gla", "w_out")
    h1, (a3_1, b3_1, n1), got = _ffn_fwd(x, g1, w["ffn1_w1"], w["ffn1_w3"], w["ffn1_w2"], "ffn1_fwd",
                                         gather(mix_first + mix_rest))
    gathered(mix_first + mix_rest, got)
    wup = w["gla_a_up_w"].astype(F32)
    (u, s5in, q, k, v, r, alow, gs5, ggla), _ = _mix_pre_fwd(h1, gm, w["w_in"])
    ar, ai, bbr, bbi = _s5_disc(lre, lim, ldt, bre, bim)
    bd = _s5_dense(bbr, bbi, 1.0)
    cd = _s5_dense(cre, cim, -1.0)
    bd16, cd16 = bd.astype(BF16), cd.astype(BF16)
    bdt16, ctd16 = bd16.transpose(0, 2, 1), cd16.transpose(0, 2, 1)
    ar4 = ar.reshape(S5_BLOCKS, 1, S5_BSTATE)
    ai4 = ai.reshape(S5_BLOCKS, 1, S5_BSTATE)
    (xs, y), got = _s5_fwd(s5in, bd16, ctd16, ar4, ai4, dskip, gather(GROUPS[2][:2]))
    gathered(GROUPS[2][:2], got)
    (o, ssave), _ = _gla_fwd(q, k, v, alow, wup, bup)
    post_w = (w["s5_glu_w"], bg, gn, w["proj_s5"], w["proj_gla"], w["w_out"])
    h2, got = _mix_post_fwd(y, o, r, gs5, ggla, h1, *post_w, carry=gather(GROUPS[2][2:]))
    gathered(GROUPS[2][2:], got)
    loss, dh3, dgf, (a3_2, b3_2, n2) = _ffn_fwd_head(h2, g2, w["ffn2_w1"], w["ffn2_w3"], w["ffn2_w2"], gf, target,
                                                     "ffn2_fwd")

    big, small = {}, {}
    small["final_norm"] = dgf.reshape(D_MODEL)
    (dh2, dg2, da3, db3, s3, dhh2), _ = _ffn_bwd(
        h2, dh3, g2, a3_2, b3_2, w["ffn2_w1"], w["ffn2_w3"], w["ffn2_w2"], "ffn2_bwd")
    small["ffn2_norm"] = dg2
    big["ffn2_w1"] = _mm_tn(da3, n2, "ffn2_dw1")
    big["ffn2_w3"] = _mm_tn(db3, n2, "ffn2_dw3")
    big["ffn2_w2"] = _mm_tn(s3, dhh2, "ffn2_dw2")
    (dy, do, dr, dgs5, dggla, dbg, dgn, z5b, dgpb, ys5b, dm5b, yglab, dmgb, mergedb, dh2b), got = _mix_post_bwd(
        y, o, r, gs5, ggla, dh2, *post_w, carry=scatter(GROUPS[2][:1]))
    scattered(GROUPS[2][:1], got)
    small["s5_glu_b"] = dbg
    small["gla_out_norm"] = dgn
    big["s5_glu_w"] = _mm_tn(z5b, dgpb, "glu_dw")
    big["proj_s5"] = _mm_tn(dm5b, ys5b, "proj_s5_dw")
    big["proj_gla"] = _mm_tn(dmgb, yglab, "proj_gla_dw")
    big["w_out"] = _mm_tn(mergedb, dh2b, "w_out_dw")
    (dq, dk, dv, dalow, dwup, dbup), got = _gla_bwd(q, k, v, alow, wup, bup, ssave, do, scatter(GROUPS[2][1:2]))
    scattered(GROUPS[2][1:2], got)
    big["gla_a_up_w"] = dwup
    small["gla_a_up_b"] = dbup
    (ds5in, dbd, dcd, dd, dar4, dai4), got = _s5_bwd(
        dy, s5in, xs, cd16, bdt16, ar4, ai4, dskip, scatter(GROUPS[2][2:]))
    scattered(GROUPS[2][2:], got)
    dbbr, dbbi = _s5_undense(dbd)
    dcre, dcim_neg = _s5_undense(dcd)
    glre, glim, gldt, gbre, gbim = _s5_disc_bwd(
        lre, lim, ldt, bre, bim, dar4.reshape(S5_GROUPS, S5_STATE), dai4.reshape(S5_GROUPS, S5_STATE),
        dbbr, dbbi)
    small["s5_lambda_re"] = glre[None]
    small["s5_lambda_im"] = glim[None]
    small["s5_log_dt"] = gldt.reshape(1, S5_GROUPS)
    small["s5_b_re"] = gbre
    small["s5_b_im"] = gbim
    small["s5_c_re"] = dcre[None]
    small["s5_c_im"] = -dcim_neg[None]
    small["s5_d"] = dd.reshape(1, S5_GROUPS, S5_GROUP)
    dz = (ds5in, dq, dk, dv, dr, dalow, dgs5, dggla)
    (dh1, dgm), got = _mix_pre_bwd(h1, gm, w["w_in"], dh2, dz, scatter(mix_rest))
    scattered(mix_rest, got)
    small["mix_norm"] = dgm
    wide = _mm_tn_many(dz[:5] + dz[6:], u, "w_in_dw")
    low_at = IN_OFFS[5]
    big["w_in"] = jnp.concatenate([wide[:low_at], _mm_tn(dalow, u, "w_in_dw_low"), wide[low_at:]], axis=0)
    (dx, dg1, da3, db3, s3, dhh1), got = _ffn_bwd(
        x, dh1, g1, a3_1, b3_1, w["ffn1_w1"], w["ffn1_w3"], w["ffn1_w2"], "ffn1_bwd",
        scatter(mix_first))
    scattered(mix_first, got)
    small["ffn1_norm"] = dg1
    if rows is None:
        big["ffn1_w1"] = _mm_tn(da3, n1, "ffn1_dw1")
        big["ffn1_w3"] = _mm_tn(db3, n1, "ffn1_dw3")
        big["ffn1_w2"] = _mm_tn(s3, dhh1, "ffn1_dw2")
        return loss[0, 0], dx, big, small
    part = _pack_small(small, loss).reshape(N_DEV, SMALL_R // N_DEV, 1024)
    big["ffn1_w1"], (small_landed,) = _mm_tn(da3, n1, "ffn1_dw1", ([part], "scatter"))
    small_mine = _sum_slabs(small_landed, "sum_small")
    slabs1 = _grad_slabs("ffn1_w1", big["ffn1_w1"])
    big["ffn1_w3"], (from_pair, small_all) = _mm_tn(db3, n1, "ffn1_dw3",
                                                    [([slabs1], "pair"), ([small_mine], "gather")])
    small = small_all.reshape(SMALL_R, 1024)
    sums1 = _pair_add(slabs1, from_pair, "ffn1_w1_pair")
    slabs3 = _grad_slabs("ffn1_w3", big["ffn1_w3"])
    big["ffn1_w2"], (landed1, from_pair) = _mm_tn(s3, dhh1, "ffn1_dw2", [([sums1], "chips"), ([slabs3], "pair")])
    sums3 = _pair_add(slabs3, from_pair, "ffn1_w3_pair")
    slabs2 = _grad_slabs("ffn1_w2", big["ffn1_w2"])

    def sum_adamw(names, lands, name, carry=None):
        outs, got = _sum_adamw(lands, *([opt[n][j] for n in names] for j in range(3)), name, carry)
        for i, n in enumerate(names):
            updated[n] = outs[i::len(names)]
        return got

    updated = {}
    landed3, from_pair = sum_adamw(GROUPS[2], [landed_grads.pop(n) for n in GROUPS[2]], "adamw_ffn2",
                                   [([sums3], "chips"), ([slabs2], "pair")])
    sums2 = _pair_add(slabs2, from_pair, "ffn1_w2_pair")
    (landed2,) = _exchange(([sums2], "chips"), "scatter_ffn1_b")
    sum_adamw(GROUPS[0], [landed1, landed3, landed2], "adamw_ffn1")
    return loss[0, 0], dx, landed_grads, small, updated


NAMES = ("ffn1_norm", "ffn1_w1", "ffn1_w3", "ffn1_w2", "mix_norm", "w_in", "s5_lambda_re", "s5_lambda_im",
         "s5_log_dt", "s5_b_re", "s5_b_im", "s5_c_re", "s5_c_im", "s5_d", "s5_glu_w", "s5_glu_b", "gla_a_up_w",
         "gla_a_up_b", "gla_out_norm", "proj_s5", "proj_gla", "w_out", "ffn2_norm", "ffn2_w1", "ffn2_w3", "ffn2_w2",
         "final_norm")


def kernel(*args):
    nw = len(NAMES)
    x = args[0][0]
    wts = dict(zip(NAMES, args[1:1 + nw]))
    target = args[1 + nw][0]
    mom = dict(zip(NAMES, args[2 + nw:2 + 2 * nw]))
    var = dict(zip(NAMES, args[2 + 2 * nw:2 + 3 * nw]))

    shards = {n: wts[n][0] for n in BIG}
    rows = {n: _shard_rows(n, shards[n]).astype(BF16) for n in BIG}
    def row_layout(n, a):
        return a.T if n in ROW_ADAM else a

    opt = {n: tuple(row_layout(n, d[n][0]) for d in (wts, mom, var)) for n in GROUPS[0] + GROUPS[2]}
    _, dx, landed, small_slab, updated = _local_step(x, target, {n: wts[n] for n in SMALL}, None, rows, opt)
    loss = small_slab.reshape(-1)[SMALL_N]
    g_small = _unpack_small(small_slab)

    grad, delta, new_m, new_v = {}, {}, {}, {}
    for n, arrays in updated.items():
        grad[n], delta[n], new_m[n], new_v[n] = (row_layout(n, a)[None] for a in arrays)
    for n in GROUPS[1]:
        g_rows = _sum_slabs(landed[n], "sum_" + n)
        if n in ROW_ADAM:
            g = g_rows[:W_IN_ROWS] if n == "w_in" else g_rows
            outs = _adamw(shards[n].T, g, mom[n][0].T, var[n][0].T, "adamw_" + n)
            grad[n], delta[n], new_m[n], new_v[n] = (a.T[None] for a in (g, *outs))
        else:
            g = _unshard_rows(n, g_rows, shards[n].shape)
            outs = _adamw(shards[n], g, mom[n][0], var[n][0], "adamw_" + n)
            grad[n], delta[n], new_m[n], new_v[n] = (a[None] for a in (g, *outs))

    def flat2d(a):
        return a.reshape(-1, a.shape[-1])

    operands = ([flat2d(_working(n, d[n])) for n in SMALL] for d in (wts, mom, var))
    w2d, m2d, v2d = operands
    outs = _adamw_many(w2d, [flat2d(g_small[n]) for n in SMALL], m2d, v2d, "adamw_small")
    for out, arrays in zip((grad, delta, new_m, new_v), ([g_small[n] for n in SMALL], *outs)):
        out.update({n: _declared(n, a.reshape(g_small[n].shape)) for n, a in zip(SMALL, arrays)})
    return (loss, dx[None], *(d[n] for d in (grad, delta, new_m, new_v) for n in NAMES))
```

```python
import functools
import math

import jax
import jax.numpy as jnp
from jax import lax
from jax.experimental import pallas as pl
from jax.experimental.pallas import tpu as pltpu

F32, BF16 = jnp.float32, jnp.bfloat16

D_MODEL = 1024
D_FF = 2816
N_DEV = 8
S5_WIDTH, S5_GROUPS, S5_GROUP, S5_STATE = 512, 32, 16, 64
S5_BLOCKS = 4
S5_BSTATE = 512
S5_SEGS = 8
GLA_HEADS, GLA_DK, GLA_DV = 4, 64, 128
GLA_KEY, GLA_VAL, GLA_RANK, GLA_CHUNK = 256, 512, 16, 64
GLA_TAU = 16.0
GLA_STEP_CHUNKS = 4
EPS = 1e-6
IN_SIZES = (512, 256, 256, 512, 512, 16, 1024, 1024)
IN_OFFS = tuple(sum(IN_SIZES[:i]) for i in range(len(IN_SIZES)))
IN_COLS = sum(IN_SIZES)
ADAM_LR, ADAM_B1, ADAM_B2, ADAM_EPS, ADAM_WD, ADAM_STEP = 0.001, 0.9, 0.999, 1e-08, 0.01, 10
GELU_C0 = math.sqrt(2.0 / math.pi)
GELU_C1 = 0.044715

FFN_FT = 256
VMEM_LIMIT_BYTES = 56 * 1024 * 1024

VMEM_FULL = pl.BlockSpec(memory_space=pltpu.VMEM)
ANY = pl.BlockSpec(memory_space=pl.ANY)


def _cparams(n_grid):
    return pltpu.CompilerParams(dimension_semantics=("arbitrary",) * n_grid, vmem_limit_bytes=VMEM_LIMIT_BYTES)


def _tile(t):
    return 512 if t >= 1024 else t // 2


def _nn(a, b):
    return jnp.dot(a, b, preferred_element_type=F32)


def _nt(a, b):
    return lax.dot_general(a, b, (((1,), (1,)), ((), ())), preferred_element_type=F32)


def _tn(a, b):
    return lax.dot_general(a, b, (((0,), (0,)), ((), ())), preferred_element_type=F32)


def _rms_parts(x):
    r = lax.rsqrt(jnp.mean(x * x, axis=-1, keepdims=True) + EPS)
    return x * r, r


def _rms_bwd(dn, g, xhat, r):
    dxh = dn * g
    dx = r * (dxh - xhat * jnp.mean(dxh * xhat, axis=-1, keepdims=True))
    return dx, jnp.sum(dn * xhat, axis=0, keepdims=True)


def _peers():
    x, y, c = lax.axis_index("x"), lax.axis_index("y"), lax.axis_index("c")
    out = []
    for k in range(1, N_DEV):
        px = 1 - x if k & 4 else x
        py = 1 - y if k & 2 else y
        pc = 1 - c if k & 1 else c
        out.append(((px, py, pc), 4 * px + 2 * py + pc))
    return 4 * x + 2 * y + c, out


def _exchange_copies(src_refs, out_refs, send_sems, recv_sems, local_sems, scatter, with_recvs):
    me, peers = _peers()
    locals_, sends, recvs = [], [], []
    for a, (src_ref, out_ref) in enumerate(zip(src_refs, out_refs)):
        def mine(idx, src_ref=src_ref):
            return src_ref.at[idx] if scatter else src_ref

        locals_.append(pltpu.make_async_copy(mine(me), out_ref.at[me], local_sems.at[a]))
        for k, (dev, idx) in enumerate(peers):
            sends.append(pltpu.make_async_remote_copy(
                src_ref=mine(idx), dst_ref=out_ref.at[me], send_sem=send_sems.at[a, k], recv_sem=recv_sems.at[a, k],
                device_id=dev, device_id_type=pl.DeviceIdType.MESH))
            if with_recvs:
                recvs.append(pltpu.make_async_remote_copy(
                    src_ref=mine(idx), dst_ref=out_ref.at[idx], send_sem=send_sems.at[a, k],
                    recv_sem=recv_sems.at[a, k], device_id=dev, device_id_type=pl.DeviceIdType.MESH))
    return locals_, sends, recvs


def _remote(src, dst, send_sems, recv_sems, a, k, dev):
    return pltpu.make_async_remote_copy(src_ref=src, dst_ref=dst, send_sem=send_sems.at[a, k],
                                        recv_sem=recv_sems.at[a, k], device_id=dev,
                                        device_id_type=pl.DeviceIdType.MESH)


def _gather_places():
    x, y, c = lax.axis_index("x"), lax.axis_index("y"), lax.axis_index("c")

    def at(chip, core):
        return 4 * chip[0] + 2 * chip[1] + core

    xn, yn, diag = (1 - x, y), (x, 1 - y), (1 - x, 1 - y)
    relay = (x * (1 - c) + (1 - x) * c, (1 - y) * (1 - c) + y * c)
    passed = ((1 - x) * (1 - c) + x * c, y * (1 - c) + (1 - y) * c)
    return dict(sibling=(x, y, 1 - c), me=at((x, y), c), sib=at((x, y), 1 - c), c=c, at=at,
                xn=xn, yn=yn, diag=diag, relay=relay, passed=passed)


def _gather_start(src_refs, out_refs, send_sems, recv_sems, local_sems):
    p = _gather_places()
    for a, (src, out) in enumerate(zip(src_refs, out_refs)):
        mine = out.at[p["me"]]
        pltpu.make_async_copy(src, mine, local_sems.at[a]).start()
        _remote(src, mine, send_sems, recv_sems, a, 0, p["sibling"]).start()
        _remote(src, mine, send_sems, recv_sems, a, 1, (*p["xn"], p["c"])).start()
        _remote(src, mine, send_sems, recv_sems, a, 2, (*p["yn"], p["c"])).start()


def _gather_forward(src_refs, out_refs, send_sems, recv_sems, local_sems):
    p = _gather_places()
    c, at = p["c"], p["at"]
    for a, (src, out) in enumerate(zip(src_refs, out_refs)):
        from_x, from_y = out.at[at(p["xn"], c)], out.at[at(p["yn"], c)]
        _remote(src, from_x, send_sems, recv_sems, a, 1, (*p["xn"], c)).wait_recv()
        _remote(src, from_y, send_sems, recv_sems, a, 2, (*p["yn"], c)).wait_recv()
        relayed = out.at[at(p["passed"], c)]
        _remote(relayed, relayed, send_sems, recv_sems, a, 3, (*p["relay"], c)).start()
        _remote(from_x, from_x, send_sems, recv_sems, a, 4, p["sibling"]).start()
        _remote(from_y, from_y, send_sems, recv_sems, a, 5, p["sibling"]).start()


def _gather_diagonal(src_refs, out_refs, send_sems, recv_sems, local_sems):
    p = _gather_places()
    for a, (src, out) in enumerate(zip(src_refs, out_refs)):
        from_diag = out.at[p["at"](p["diag"], p["c"])]
        _remote(src, from_diag, send_sems, recv_sems, a, 3, (*p["relay"], p["c"])).wait_recv()
        _remote(from_diag, from_diag, send_sems, recv_sems, a, 6, p["sibling"]).start()


def _gather_finish(src_refs, out_refs, send_sems, recv_sems, local_sems):
    p = _gather_places()
    c, at, sibling = p["c"], p["at"], p["sibling"]
    arrays = list(enumerate(zip(src_refs, out_refs)))
    for a, (src, out) in arrays:
        _remote(src, out.at[p["sib"]], send_sems, recv_sems, a, 0, sibling).wait_recv()
        for k, chip in ((4, p["xn"]), (5, p["yn"]), (6, p["diag"])):
            _remote(src, out.at[at(chip, 1 - c)], send_sems, recv_sems, a, k, sibling).wait_recv()
        for k in range(N_DEV - 1):
            _remote(src, out.at[p["me"]], send_sems, recv_sems, a, k, sibling).wait_send()
        pltpu.make_async_copy(src, out.at[p["me"]], local_sems.at[a]).wait()


def _exchange_start(*refs, scatter):
    locals_, sends, _ = _exchange_copies(*refs, scatter=scatter, with_recvs=False)
    for cp in locals_ + sends:
        cp.start()


def _exchange_wait(*refs, scatter):
    locals_, sends, recvs = _exchange_copies(*refs, scatter=scatter, with_recvs=True)
    for cp in recvs:
        cp.wait_recv()
    for cp in sends:
        cp.wait_send()
    for cp in locals_:
        cp.wait()


def _halves_places():
    x, y, c = lax.axis_index("x"), lax.axis_index("y"), lax.axis_index("c")
    flips = [(1 - x, y), (x, 1 - y), (1 - x, 1 - y)]
    return (x, y, 1 - c), c, 2 * x + y, [((fx, fy, c), 2 * fx + fy) for fx, fy in flips]


def _pair_start(src_refs, out_refs, send_sems, recv_sems, local_sems):
    sibling, c, _, _ = _halves_places()
    for a, (src, out) in enumerate(zip(src_refs, out_refs)):
        for i in range(4):
            _remote(src.at[2 * i + 1 - c], out.at[i], send_sems, recv_sems, a, i, sibling).start()


def _pair_finish(src_refs, out_refs, send_sems, recv_sems, local_sems):
    sibling, c, _, _ = _halves_places()
    for a, (src, out) in enumerate(zip(src_refs, out_refs)):
        for i in range(4):
            _remote(src.at[2 * i + 1 - c], out.at[i], send_sems, recv_sems, a, i, sibling).wait()


def _chips_start(src_refs, out_refs, send_sems, recv_sems, local_sems):
    _, _, chip, others = _halves_places()
    for a, (src, out) in enumerate(zip(src_refs, out_refs)):
        pltpu.make_async_copy(src.at[chip], out.at[chip], local_sems.at[a]).start()
        for k, (dev, their_chip) in enumerate(others):
            _remote(src.at[their_chip], out.at[chip], send_sems, recv_sems, a, k, dev).start()


def _chips_finish(src_refs, out_refs, send_sems, recv_sems, local_sems):
    _, _, chip, others = _halves_places()
    for a, (src, out) in enumerate(zip(src_refs, out_refs)):
        for k, (dev, their_chip) in enumerate(others):
            _remote(src.at[their_chip], out.at[their_chip], send_sems, recv_sems, a, k, dev).wait_recv()
        for k, (dev, their_chip) in enumerate(others):
            _remote(src.at[their_chip], out.at[chip], send_sems, recv_sems, a, k, dev).wait_send()
        pltpu.make_async_copy(src.at[chip], out.at[chip], local_sems.at[a]).wait()


EXCHANGES = {
    "gather": (_gather_start, _gather_forward, _gather_diagonal, _gather_finish, N_DEV, False),
    "scatter": (functools.partial(_exchange_start, scatter=True), None, None,
                functools.partial(_exchange_wait, scatter=True), N_DEV, True),
    "pair": (_pair_start, None, None, _pair_finish, 4, True),
    "chips": (_chips_start, None, None, _chips_finish, 4, True),
}


def _exchange_sems(n_arrays):
    return [pltpu.SemaphoreType.DMA((n_arrays, N_DEV - 1)), pltpu.SemaphoreType.DMA((n_arrays, N_DEV - 1)),
            pltpu.SemaphoreType.DMA((n_arrays,))]


def _exchange_shapes(srcs, kind):
    lead, slabbed = EXCHANGES[kind][4:]
    return [jax.ShapeDtypeStruct((lead,) + tuple(s.shape[1:] if slabbed else s.shape), s.dtype) for s in srcs]


def _carries(carry):
    if carry is None:
        return []
    return [carry] if isinstance(carry, tuple) else list(carry)


def _call(body, *, name, grid, in_specs, out_specs, out_shape, args, scratch_shapes=(), carry=None):
    n_in, n_out, n_scr = len(in_specs), len(out_specs), len(scratch_shapes)
    groups = _carries(carry)
    sizes = [len(arrays) for arrays, _ in groups]
    nc = sum(sizes)

    def wrapped(*refs):
        ins, refs = refs[:n_in], refs[n_in:]
        csrc, refs = refs[:nc], refs[nc:]
        outs, refs = refs[:n_out], refs[n_out:]
        cland, refs = refs[:nc], refs[nc:]
        scr, sems = refs[:n_scr], refs[n_scr:]

        def run(phase):
            at = 0
            for gi, ((_, kind), size) in enumerate(zip(groups, sizes)):
                if EXCHANGES[kind][phase] is not None:
                    EXCHANGES[kind][phase](csrc[at:at + size], cland[at:at + size], *sems[3 * gi:3 * gi + 3])
                at += size

        last = pl.program_id(0) == grid[0] - 1
        if nc:
            pl.when(pl.program_id(0) == 0)(functools.partial(run, 0))
            pl.when(pl.program_id(0) == max(grid[0] - 2, 0))(functools.partial(run, 1))
            pl.when(last)(functools.partial(run, 2))
        if body is not None:
            body(*ins, *outs, *scr)
        if nc:
            pl.when(last)(functools.partial(run, 3))

    res = pl.pallas_call(
        wrapped, name=name, grid=grid,
        in_specs=list(in_specs) + [ANY] * nc, out_specs=list(out_specs) + [ANY] * nc,
        out_shape=list(out_shape) + [s for arrays, kind in groups for s in _exchange_shapes(arrays, kind)],
        scratch_shapes=list(scratch_shapes) + [s for size in sizes for s in _exchange_sems(size)],
        compiler_params=_cparams(1),
    )(*args, *[a for arrays, _ in groups for a in arrays])
    return res[:n_out], res[n_out:]


def _row_tile(tm, d):
    return pl.BlockSpec((tm, d), lambda i: (i, 0))


def _acc_row(d):
    return pl.BlockSpec((1, d), lambda i: (0, 0))


def _ffn_body(x_ref, g_ref, w1_ref, w3_ref, w2_ref, acc_ref, a_ref, b_ref, n_ref):
    xv = x_ref[...]
    xhat, _ = _rms_parts(xv)
    n = (xhat * g_ref[...]).astype(BF16)
    n_ref[...] = n
    acc_ref[...] = xv

    def fstep(f, c):
        rows = pl.ds(pl.multiple_of(f * FFN_FT, FFN_FT), FFN_FT)
        a = _nt(n, w1_ref[rows, :])
        b = _nt(n, w3_ref[rows, :])
        a_ref[f] = a.astype(BF16)
        b_ref[f] = b.astype(BF16)
        s = (a * jax.nn.sigmoid(a) * b).astype(BF16)
        acc_ref[...] += 0.5 * _nn(s, w2_ref[rows, :])
        return c

    lax.fori_loop(0, D_FF // FFN_FT, fstep, 0, unroll=True)


def _ffn_fwd(x, g, w1t, w3t, w2, name, carry=None):
    t = x.shape[0]
    tm = _tile(t)
    nf = D_FF // FFN_FT
    blk3 = pl.BlockSpec((nf, tm, FFN_FT), lambda i: (0, i, 0))
    sh3 = jax.ShapeDtypeStruct((nf, t, FFN_FT), BF16)
    (h, a3, b3, n), landed = _call(
        functools.partial(_ffn_body), name=name, grid=(t // tm,),
        in_specs=[_row_tile(tm, D_MODEL), _acc_row(D_MODEL), VMEM_FULL, VMEM_FULL, VMEM_FULL],
        out_specs=[_row_tile(tm, D_MODEL), blk3, blk3, _row_tile(tm, D_MODEL)],
        out_shape=[jax.ShapeDtypeStruct((t, D_MODEL), F32), sh3, sh3, jax.ShapeDtypeStruct((t, D_MODEL), BF16)],
        args=(x, g, w1t, w3t, w2), carry=carry)
    return h, (a3, b3, n), landed


def _ffn_fwd_head(x, g, w1t, w3t, w2, gf, target, name):
    t = x.shape[0]
    tm = _tile(t)
    nf = D_FF // FFN_FT

    def body(x_ref, g_ref, w1_ref, w3_ref, w2_ref, gf_ref, t_ref, loss_ref, dh_ref, dgf_ref, a_ref, b_ref, n_ref, acc):
        _ffn_body(x_ref, g_ref, w1_ref, w3_ref, w2_ref, acc, a_ref, b_ref, n_ref)
        _head_math(acc[...], gf_ref[...], t_ref[...], loss_ref, dh_ref, dgf_ref)

    blk3 = pl.BlockSpec((nf, tm, FFN_FT), lambda i: (0, i, 0))
    sh3 = jax.ShapeDtypeStruct((nf, t, FFN_FT), BF16)
    (loss, dh, dgf, a3, b3, n), _ = _call(
        body, name=name, grid=(t // tm,),
        in_specs=[_row_tile(tm, D_MODEL), _acc_row(D_MODEL), VMEM_FULL, VMEM_FULL, VMEM_FULL, _acc_row(D_MODEL),
                  _row_tile(tm, D_MODEL)],
        out_specs=[pl.BlockSpec((1, 1), lambda i: (0, 0)), _row_tile(tm, D_MODEL), _acc_row(D_MODEL), blk3, blk3,
                   _row_tile(tm, D_MODEL)],
        out_shape=[jax.ShapeDtypeStruct((1, 1), F32), jax.ShapeDtypeStruct((t, D_MODEL), F32),
                   jax.ShapeDtypeStruct((1, D_MODEL), F32), sh3, sh3, jax.ShapeDtypeStruct((t, D_MODEL), BF16)],
        scratch_shapes=[pltpu.VMEM((tm, D_MODEL), F32)],
        args=(x, g, w1t, w3t, w2, gf, target))
    return loss, dh, dgf, (a3, b3, n)


def _head_math(h, gv, target, loss_ref, dh_ref, dg_ref):
    i = pl.program_id(0)
    xhat, r = _rms_parts(h)
    err = xhat * gv - target
    dx, dg = _rms_bwd(err * (1.0 / D_MODEL), gv, xhat, r)
    dh_ref[...] = dx

    @pl.when(i == 0)
    def _():
        loss_ref[...] = jnp.zeros_like(loss_ref)
        dg_ref[...] = jnp.zeros_like(dg_ref)

    loss_ref[...] += (0.5 / D_MODEL) * jnp.sum(jnp.sum(err * err, axis=1, keepdims=True), axis=0, keepdims=True)
    dg_ref[...] += dg


def _ffn_bwd(x, dh, g, a3, b3, w1t, w3t, w2, name, carry=None):
    t = x.shape[0]
    tm = _tile(t) // 2
    nf = D_FF // FFN_FT

    def body(x_ref, dh_ref, g_ref, a_ref, b_ref, w1_ref, w3_ref, w2_ref,
             dx_ref, dg_ref, da_ref, db_ref, s_ref, dhh_ref, dn_acc):
        i = pl.program_id(0)
        xv = x_ref[...]
        gv = g_ref[...]
        xhat, r = _rms_parts(xv)
        dhv = dh_ref[...]
        dhh = (0.5 * dhv).astype(BF16)
        dhh_ref[...] = dhh
        dn_acc[...] = jnp.zeros_like(dn_acc)

        def fstep(f, c):
            rows = pl.ds(f * FFN_FT, FFN_FT)
            w1c, w3c, w2c = w1_ref[rows, :], w3_ref[rows, :], w2_ref[rows, :]
            a = a_ref[f].astype(F32)
            b = b_ref[f].astype(F32)
            sg = jax.nn.sigmoid(a)
            sl = a * sg
            ds = _nt(dhh, w2c)
            da = (ds * b * sg * (1.0 + a * (1.0 - sg))).astype(BF16)
            db = (ds * sl).astype(BF16)
            s_ref[f] = (sl * b).astype(BF16)
            da_ref[f] = da
            db_ref[f] = db
            return c

        def nstep(f, c):
            rows = pl.ds(f * FFN_FT, FFN_FT)
            dn_acc[...] += _nn(da_ref[f], w1_ref[rows, :]) + _nn(db_ref[f], w3_ref[rows, :])
            return c

        for f in range(nf + 1):
            if f < nf:
                fstep(f, 0)
            if f:
                nstep(f - 1, 0)
        dx, dg = _rms_bwd(dn_acc[...], gv, xhat, r)
        dx_ref[...] = dhv + dx

        @pl.when(i == 0)
        def _():
            dg_ref[...] = jnp.zeros_like(dg_ref)

        dg_ref[...] += dg

    blk3 = pl.BlockSpec((nf, tm, FFN_FT), lambda i: (0, i, 0))
    sh3 = jax.ShapeDtypeStruct((nf, t, FFN_FT), BF16)
    return _call(
        body, name=name, grid=(t // tm,),
        in_specs=[_row_tile(tm, D_MODEL), _row_tile(tm, D_MODEL), _acc_row(D_MODEL), blk3, blk3,
                  VMEM_FULL, VMEM_FULL, VMEM_FULL],
        out_specs=[_row_tile(tm, D_MODEL), _acc_row(D_MODEL), blk3, blk3, blk3, _row_tile(tm, D_MODEL)],
        out_shape=[jax.ShapeDtypeStruct((t, D_MODEL), F32), jax.ShapeDtypeStruct((1, D_MODEL), F32), sh3, sh3, sh3,
                   jax.ShapeDtypeStruct((t, D_MODEL), BF16)],
        scratch_shapes=[pltpu.VMEM((tm, D_MODEL), F32)],
        args=(x, dh, g, a3, b3, w1t, w3t, w2), carry=carry)


def _mm_tn(a, b, name, carry=None):
    t, n = b.shape
    kc = min(512, t)
    if a.ndim == 3:
        nb, _, tb = a.shape
        a_spec = pl.BlockSpec((1, t, tb), lambda i: (i, 0, 0))
    else:
        m = a.shape[1]
        tb = min(m, 256)
        nb = m // tb
        a_spec = pl.BlockSpec((t, tb), lambda i: (0, i))
    three_d = a.ndim == 3

    def body(a_ref, b_ref, o_ref, acc):
        acc[...] = jnp.zeros_like(acc)

        def kstep(k, c):
            rows = pl.ds(pl.multiple_of(k * kc, kc), kc)
            av = a_ref[0, rows, :] if three_d else a_ref[rows, :]
            acc[...] += _tn(av.astype(BF16), b_ref[rows, :])
            return c

        lax.fori_loop(0, t // kc, kstep, 0, unroll=True)
        o_ref[...] = acc[...].astype(BF16)

    (out,), landed = _call(
        body, name=name, grid=(nb,),
        in_specs=[a_spec, VMEM_FULL],
        out_specs=[pl.BlockSpec((tb, n), lambda i: (i, 0))],
        out_shape=[jax.ShapeDtypeStruct((nb * tb, n), BF16)],
        scratch_shapes=[pltpu.VMEM((tb, n), F32)],
        args=(a, b), carry=carry)
    return (out, landed) if carry is not None else out


MM_TB = 256


def _mm_tn_many(arrays, b, name):
    t, n = b.shape
    kc = min(512, t)
    counts = [a.shape[1] // MM_TB for a in arrays]
    starts = [sum(counts[:k]) for k in range(len(arrays))]

    def spec(start, count):
        return pl.BlockSpec((t, MM_TB), lambda i: (0, jnp.clip(i - start, 0, count - 1)))

    def body(*refs):
        a_refs, (b_ref, o_ref, acc) = refs[:len(arrays)], refs[len(arrays):]
        i = pl.program_id(0)
        for a_ref, start, count in zip(a_refs, starts, counts):
            @pl.when((i >= start) & (i < start + count))
            def _(a_ref=a_ref):
                acc[...] = jnp.zeros_like(acc)

                def kstep(k, c):
                    rows = pl.ds(pl.multiple_of(k * kc, kc), kc)
                    acc[...] += _tn(a_ref[rows, :].astype(BF16), b_ref[rows, :])
                    return c

                lax.fori_loop(0, t // kc, kstep, 0, unroll=True)
                o_ref[...] = acc[...].astype(BF16)

    return pl.pallas_call(
        body, name=name, grid=(sum(counts),),
        in_specs=[spec(s, c) for s, c in zip(starts, counts)] + [VMEM_FULL],
        out_specs=pl.BlockSpec((MM_TB, n), lambda i: (i, 0)),
        out_shape=jax.ShapeDtypeStruct((sum(counts) * MM_TB, n), BF16),
        scratch_shapes=[pltpu.VMEM((MM_TB, n), F32)],
        compiler_params=_cparams(1),
    )(*arrays, b)


def _mix_pre_fwd(h, g, wint, carry=None):
    t = h.shape[0]
    tm = _tile(t)

    def body(h_ref, g_ref, w_ref, u_ref, *outs):
        xhat, _ = _rms_parts(h_ref[...])
        u = (xhat * g_ref[...]).astype(BF16)
        u_ref[...] = u
        for o_ref, off, size in zip(outs, IN_OFFS, IN_SIZES):
            o_ref[...] = _nt(u, w_ref[off:off + size, :])

    return _call(
        body, name="mix_pre_fwd", grid=(t // tm,),
        in_specs=[_row_tile(tm, D_MODEL), _acc_row(D_MODEL), VMEM_FULL],
        out_specs=[_row_tile(tm, D_MODEL)] + [_row_tile(tm, s) for s in IN_SIZES],
        out_shape=[jax.ShapeDtypeStruct((t, D_MODEL), BF16)] + [jax.ShapeDtypeStruct((t, s), F32) for s in IN_SIZES],
        args=(h, g, wint), carry=carry)


def _mix_pre_bwd(h, g, wint, dh2, dz, carry=None):
    t = h.shape[0]
    tm = _tile(t)

    def body(h_ref, g_ref, w_ref, dh2_ref, *rest):
        dz_refs, (dh1_ref, dg_ref) = rest[:len(IN_SIZES)], rest[len(IN_SIZES):]
        i = pl.program_id(0)
        gv = g_ref[...]
        xhat, r = _rms_parts(h_ref[...])
        du = jnp.zeros((tm, D_MODEL), F32)
        for dz_ref, off, size in zip(dz_refs, IN_OFFS, IN_SIZES):
            du = du + _nn(dz_ref[...].astype(BF16), w_ref[off:off + size, :])
        dx, dg = _rms_bwd(du, gv, xhat, r)
        dh1_ref[...] = dh2_ref[...] + dx

        @pl.when(i == 0)
        def _():
            dg_ref[...] = jnp.zeros_like(dg_ref)

        dg_ref[...] += dg

    return _call(
        body, name="mix_pre_bwd", grid=(t // tm,),
        in_specs=[_row_tile(tm, D_MODEL), _acc_row(D_MODEL), VMEM_FULL, _row_tile(tm, D_MODEL)]
        + [_row_tile(tm, s) for s in IN_SIZES],
        out_specs=[_row_tile(tm, D_MODEL), _acc_row(D_MODEL)],
        out_shape=[jax.ShapeDtypeStruct((t, D_MODEL), F32), jax.ShapeDtypeStruct((1, D_MODEL), F32)],
        args=(h, g, wint, dh2, *dz), carry=carry)


def _disc_math(lre, lim, ldt, bre, bim):
    dt = jnp.exp(ldt)
    mag = jnp.exp(lre * dt)
    ar = mag * jnp.cos(lim * dt)
    ai = mag * jnp.sin(lim * dt)
    den = lre * lre + lim * lim
    nr = ar - 1.0
    fr = (nr * lre + ai * lim) / den
    fi = (ai * lre - nr * lim) / den
    fr, fi = fr[:, None, :], fi[:, None, :]
    return ar, ai, fr * bre - fi * bim, fr * bim + fi * bre


def _s5_disc(lre, lim, ldt, bre, bim):
    def body(lre_ref, lim_ref, ldt_ref, bre_ref, bim_ref, ar_ref, ai_ref, bbr_ref, bbi_ref):
        ar, ai, bbr, bbi = _disc_math(lre_ref[...], lim_ref[...], ldt_ref[...], bre_ref[...], bim_ref[...])
        ar_ref[...] = ar
        ai_ref[...] = ai
        bbr_ref[...] = bbr
        bbi_ref[...] = bbi

    small = jax.ShapeDtypeStruct(lre.shape, F32)
    big = jax.ShapeDtypeStruct(bre.shape, F32)
    return pl.pallas_call(body, name="s5_disc", out_shape=[small, small, big, big],
                          in_specs=[VMEM_FULL] * 5, out_specs=[VMEM_FULL] * 4)(lre, lim, ldt, bre, bim)


def _s5_disc_bwd(lre, lim, ldt, bre, bim, dar, dai, dbbr, dbbi):
    def body(lre_ref, lim_ref, ldt_ref, bre_ref, bim_ref, dar_ref, dai_ref, dbbr_ref, dbbi_ref,
             glre_ref, glim_ref, gldt_ref, gbre_ref, gbim_ref):
        _, vjp = jax.vjp(_disc_math, lre_ref[...], lim_ref[...], ldt_ref[...], bre_ref[...], bim_ref[...])
        glre, glim, gldt, gbre, gbim = vjp((dar_ref[...], dai_ref[...], dbbr_ref[...], dbbi_ref[...]))
        glre_ref[...] = glre
        glim_ref[...] = glim
        gldt_ref[...] = gldt
        gbre_ref[...] = gbre
        gbim_ref[...] = gbim

    small = jax.ShapeDtypeStruct(lre.shape, F32)
    big = jax.ShapeDtypeStruct(bre.shape, F32)
    return pl.pallas_call(body, name="s5_disc_bwd",
                          out_shape=[small, small, jax.ShapeDtypeStruct(ldt.shape, F32), big, big],
                          in_specs=[VMEM_FULL] * 9, out_specs=[VMEM_FULL] * 5,
                          )(lre, lim, ldt, bre, bim, dar, dai, dbbr, dbbi)


def _cmul(ar, ai, br, bi):
    return ar * br - ai * bi, ar * bi + ai * br


def _cpow(ar, ai, n):
    rr, ri = None, None
    pr, pi = ar, ai
    while n:
        if n & 1:
            rr, ri = (pr, pi) if rr is None else _cmul(rr, ri, pr, pi)
        n >>= 1
        if n:
            pr, pi = _cmul(pr, pi, pr, pi)
    return rr, ri


def _shift_rows(v, down):
    row = lax.broadcasted_iota(jnp.int32, v.shape, 0)
    if down:
        return jnp.where(row == 0, 0.0, pltpu.roll(v, 1, 0))
    return jnp.where(row == S5_SEGS - 1, 0.0, pltpu.roll(v, S5_SEGS - 1, 0))


def _chain_segments(er, ei, pr, pi, down):
    fr, fi = er, ei
    for _ in range(S5_SEGS - 1):
        sr, si = _shift_rows(fr, down), _shift_rows(fi, down)
        mr, mi = _cmul(pr, pi, sr, si)
        fr, fi = er + mr, ei + mi
    return _shift_rows(fr, down), _shift_rows(fi, down)


def _rows_to_scan_order(src_ref, dst_ref, t):
    ls = t // S5_SEGS

    def tile(j, c):
        dst_ref[pl.ds(pl.multiple_of(j * S5_SEGS, S5_SEGS), S5_SEGS), :] = src_ref[pl.ds(j, S5_SEGS, stride=ls), :]
        return c

    lax.fori_loop(0, ls, tile, 0, unroll=8)


def _rows_from_scan_order(src_ref, dst_ref, t):
    ls = t // S5_SEGS
    for s in range(S5_SEGS):
        def tile(jb, c, s=s):
            dst_ref[pl.ds(pl.multiple_of(s * ls + jb * 8, 8), 8), :] = (
                src_ref[pl.ds(jb * 8 * S5_SEGS + s, 8, stride=S5_SEGS), :])
            return c

        lax.fori_loop(0, ls // 8, tile, 0, unroll=8)


def _s5_fwd(ug, bd, ctd, ar4, ai4, dskip, carry=None):
    t = ug.shape[0]
    ls = t // S5_SEGS
    rc = min(512, t)
    ns = S5_BSTATE

    def body(ugn_ref, bd_ref, ct_ref, ar_ref, ai_ref, d_ref, xs_hbm, yn_ref, buf, ug_ref, y_ref, sem):
        cb = pl.program_id(0)
        bdv = bd_ref[0]
        _rows_to_scan_order(ugn_ref, ug_ref, t)

        def mm(i, c):
            rows = pl.ds(pl.multiple_of(i * rc, rc), rc)
            buf[rows, :] = _nn(ug_ref[rows, :].astype(BF16), bdv)
            return c

        lax.fori_loop(0, t // rc, mm, 0, unroll=True)
        arb = jnp.broadcast_to(ar_ref[0], (S5_SEGS, ns))
        aib = jnp.broadcast_to(ai_ref[0], (S5_SEGS, ns))

        def step(j, c, store):
            sr, si = c
            rows = pl.ds(pl.multiple_of(j * S5_SEGS, S5_SEGS), S5_SEGS)
            nr = arb * sr - aib * si + buf[rows, 0:ns]
            ni = arb * si + aib * sr + buf[rows, ns:2 * ns]
            if store:
                buf[rows, 0:ns] = nr
                buf[rows, ns:2 * ns] = ni
            return nr, ni

        zero = jnp.zeros((S5_SEGS, ns), F32)
        er, ei = lax.fori_loop(0, ls, functools.partial(step, store=False), (zero, zero))
        pr, pi = _cpow(arb, aib, ls)
        init = _chain_segments(er, ei, pr, pi, down=True)
        lax.fori_loop(0, ls, functools.partial(step, store=True), init)

        out = pltpu.make_async_copy(buf, xs_hbm.at[cb], sem)
        out.start()
        ctv = ct_ref[0]
        dv = d_ref[...]

        def ymm(i, c):
            rows = pl.ds(pl.multiple_of(i * rc, rc), rc)
            y_ref[rows, :] = _nn(buf[rows, :].astype(BF16), ctv) + dv * ug_ref[rows, :]
            return c

        lax.fori_loop(0, t // rc, ymm, 0, unroll=True)
        _rows_from_scan_order(y_ref, yn_ref, t)
        out.wait()

    return _call(
        body, name="s5_fwd", grid=(S5_BLOCKS,),
        in_specs=[pl.BlockSpec((t, 128), lambda i: (0, i)),
                  pl.BlockSpec((1, 128, 2 * ns), lambda i: (i, 0, 0)),
                  pl.BlockSpec((1, 2 * ns, 128), lambda i: (i, 0, 0)),
                  pl.BlockSpec((1, 1, ns), lambda i: (i, 0, 0)),
                  pl.BlockSpec((1, 1, ns), lambda i: (i, 0, 0)),
                  pl.BlockSpec((1, 128), lambda i: (0, i))],
        out_specs=[ANY, pl.BlockSpec((t, 128), lambda i: (0, i))],
        out_shape=[jax.ShapeDtypeStruct((S5_BLOCKS, t, 2 * ns), F32), jax.ShapeDtypeStruct((t, S5_WIDTH), F32)],
        scratch_shapes=[pltpu.VMEM((t, 2 * ns), F32), pltpu.VMEM((t, 128), F32), pltpu.VMEM((t, 128), F32),
                        pltpu.SemaphoreType.DMA(())],
        args=(ug, bd, ctd, ar4, ai4, dskip), carry=carry)


def _s5_bwd(dy, ug, xs, cd, bdt, ar4, ai4, dskip, carry=None):
    t = ug.shape[0]
    ls = t // S5_SEGS
    rc = min(512, t)
    ns = S5_BSTATE

    def body(dyn_ref, ugn_ref, xs_hbm, cd_ref, bdt_ref, ar_ref, ai_ref, d_ref,
             dugn_ref, dbd_ref, dcd_ref, dd_ref, dar_ref, dai_ref, xbuf, lam, dy_ref, ug_ref, dug_ref, sem):
        cb = pl.program_id(0)
        load = pltpu.make_async_copy(xs_hbm.at[cb], xbuf, sem)
        load.start()
        cdv = cd_ref[0]
        _rows_to_scan_order(dyn_ref, dy_ref, t)
        _rows_to_scan_order(ugn_ref, ug_ref, t)

        def mm(i, c):
            rows = pl.ds(pl.multiple_of(i * rc, rc), rc)
            lam[rows, :] = _nn(dy_ref[rows, :].astype(BF16), cdv)
            return c

        lax.fori_loop(0, t // rc, mm, 0, unroll=True)
        arb = jnp.broadcast_to(ar_ref[0], (S5_SEGS, ns))
        aib = jnp.broadcast_to(ai_ref[0], (S5_SEGS, ns))

        def lam_step(j, lr, li):
            rows = pl.ds(pl.multiple_of(j * S5_SEGS, S5_SEGS), S5_SEGS)
            nr = arb * lr + aib * li + lam[rows, 0:ns]
            ni = arb * li - aib * lr + lam[rows, ns:2 * ns]
            return rows, nr, ni

        def pass1(jj, c):
            _, nr, ni = lam_step(ls - 1 - jj, *c)
            return nr, ni

        zero = jnp.zeros((S5_SEGS, ns), F32)
        er, ei = lax.fori_loop(0, ls, pass1, (zero, zero))
        pr, pi = _cpow(arb, aib, ls)
        init = _chain_segments(er, ei, pr, -pi, down=False)
        load.wait()

        def accumulate(acc, nr, ni, xpr, xpi):
            return acc[0] + nr * xpr + ni * xpi, acc[1] + ni * xpr - nr * xpi

        def pass2(jj, c):
            lr, li, accr, acci = c
            j = ls - 1 - jj
            rows, nr, ni = lam_step(j, lr, li)
            lam[rows, 0:ns] = nr
            lam[rows, ns:2 * ns] = ni
            prev = pl.ds(pl.multiple_of((j - 1) * S5_SEGS, S5_SEGS), S5_SEGS)
            accr, acci = accumulate((accr, acci), nr, ni, xbuf[prev, 0:ns], xbuf[prev, ns:2 * ns])
            return nr, ni, accr, acci

        lr, li, accr, acci = lax.fori_loop(0, ls - 1, pass2, (init[0], init[1], zero, zero))
        rows, nr, ni = lam_step(0, lr, li)
        lam[rows, 0:ns] = nr
        lam[rows, ns:2 * ns] = ni
        last = pl.ds((ls - 1) * S5_SEGS, S5_SEGS)
        accr, acci = accumulate((accr, acci), nr, ni,
                                _shift_rows(xbuf[last, 0:ns], True), _shift_rows(xbuf[last, ns:2 * ns], True))
        dar_ref[0] = jnp.sum(accr, axis=0, keepdims=True)
        dai_ref[0] = jnp.sum(acci, axis=0, keepdims=True)

        bdtv = bdt_ref[0]
        dv = d_ref[...]
        dbd_ref[...] = jnp.zeros_like(dbd_ref)
        dcd_ref[...] = jnp.zeros_like(dcd_ref)
        dd_ref[...] = jnp.zeros_like(dd_ref)

        def tail(i, c):
            rows = pl.ds(pl.multiple_of(i * rc, rc), rc)
            dy = dy_ref[rows, :]
            ug = ug_ref[rows, :]
            lb = lam[rows, :].astype(BF16)
            dug_ref[rows, :] = _nn(lb, bdtv) + dv * dy
            dbd_ref[0] += _tn(ug.astype(BF16), lb)
            dcd_ref[0] += _tn(dy.astype(BF16), xbuf[rows, :].astype(BF16))
            dd_ref[...] += jnp.sum(dy * ug, axis=0, keepdims=True)
            return c

        lax.fori_loop(0, t // rc, tail, 0, unroll=True)
        _rows_from_scan_order(dug_ref, dugn_ref, t)

    chan = pl.BlockSpec((t, 128), lambda i: (0, i))
    dense = pl.BlockSpec((1, 128, 2 * ns), lambda i: (i, 0, 0))
    vec = pl.BlockSpec((1, 1, ns), lambda i: (i, 0, 0))
    return _call(
        body, name="s5_bwd", grid=(S5_BLOCKS,),
        in_specs=[chan, chan, ANY, dense, pl.BlockSpec((1, 2 * ns, 128), lambda i: (i, 0, 0)), vec, vec,
                  pl.BlockSpec((1, 128), lambda i: (0, i))],
        out_specs=[chan, dense, dense, pl.BlockSpec((1, 128), lambda i: (0, i)), vec, vec],
        out_shape=[jax.ShapeDtypeStruct((t, S5_WIDTH), F32),
                   jax.ShapeDtypeStruct((S5_BLOCKS, 128, 2 * ns), F32),
                   jax.ShapeDtypeStruct((S5_BLOCKS, 128, 2 * ns), F32),
                   jax.ShapeDtypeStruct((1, S5_WIDTH), F32),
                   jax.ShapeDtypeStruct((S5_BLOCKS, 1, ns), F32),
                   jax.ShapeDtypeStruct((S5_BLOCKS, 1, ns), F32)],
        scratch_shapes=[pltpu.VMEM((t, 2 * ns), F32), pltpu.VMEM((t, 2 * ns), F32)]
        + [pltpu.VMEM((t, 128), F32)] * 3 + [pltpu.SemaphoreType.DMA(())],
        args=(dy, ug, xs, cd, bdt, ar4, ai4, dskip), carry=carry)


def _cumsum_rows(x, reverse):
    c = x.shape[0]
    row = lax.broadcasted_iota(jnp.int32, x.shape, 0)
    d = 1
    while d < c:
        if reverse:
            x = x + jnp.where(row < c - d, pltpu.roll(x, c - d, 0), 0.0)
        else:
            x = x + jnp.where(row >= d, pltpu.roll(x, d, 0), 0.0)
        d *= 2
    return x


def _gla_common(q, k, alow, wup, bup):
    c = GLA_CHUNK
    pre = _nn(alow.astype(BF16), wup.astype(BF16)) + bup
    la = (jnp.minimum(pre, 0.0) - jnp.log(1.0 + jnp.exp(-jnp.abs(pre)))) * (1.0 / GLA_TAU)
    rr = lax.broadcasted_iota(jnp.int32, (c, c), 0)
    cc = lax.broadcasted_iota(jnp.int32, (c, c), 1)
    tril = (rr >= cc).astype(F32)
    bc = _cumsum_rows(la, reverse=False)
    bl = bc[c - 1:c, :]
    e_pos = jnp.exp(bc)
    e_neg = jnp.exp(-bc)
    e_end = jnp.exp(bl - bc)
    qt = q * (GLA_DK ** -0.5) * e_pos
    kt = k * e_neg
    ke = k * e_end
    lane = lax.broadcasted_iota(jnp.int32, (1, GLA_KEY), 1)
    masks = [((lane >= h * GLA_DK) & (lane < (h + 1) * GLA_DK)).astype(F32) for h in range(GLA_HEADS)]
    return dict(pre=pre, tril=tril, bc=bc, bl=bl, e_pos=e_pos, e_neg=e_neg, e_end=e_end,
                qt=qt, kt=kt, ke=ke, dec=jnp.exp(bl), masks=masks)


def _gla_fwd(q, k, v, alow, wup, bup, carry=None):
    t = q.shape[0]
    c = GLA_CHUNK
    n = t // c
    step = GLA_STEP_CHUNKS * c

    def body(q_ref, k_ref, v_ref, al_ref, wup_ref, bup_ref, o_ref, ss_ref, s_ref):
        i = pl.program_id(0)

        @pl.when(i == 0)
        def _():
            s_ref[...] = jnp.zeros_like(s_ref)

        wup_v, bup_v = wup_ref[...], bup_ref[...]
        s = s_ref[...]
        for j in range(GLA_STEP_CHUNKS):
            tok = slice(j * c, (j + 1) * c)
            m = _gla_common(q_ref[tok, :], k_ref[tok, :], al_ref[tok, :], wup_v, bup_v)
            ss_ref[j] = s
            sb = s.astype(BF16)
            ktb = m["kt"].astype(BF16)
            update = jnp.zeros_like(s)
            for h in range(GLA_HEADS):
                mask = m["masks"][h]
                qm = (m["qt"] * mask).astype(BF16)
                vh = v_ref[tok, h * GLA_DV:(h + 1) * GLA_DV].astype(BF16)
                p = (m["tril"] * _nt(qm, ktb)).astype(BF16)
                o_ref[tok, h * GLA_DV:(h + 1) * GLA_DV] = _nn(p, vh) + _nt(qm, sb)
                update = update + _tn(vh, (m["ke"] * mask).astype(BF16))
            s = m["dec"] * s + update
        s_ref[...] = s

    return _call(
        body, name="gla_fwd", grid=(t // step,),
        in_specs=[_row_tile(step, GLA_KEY), _row_tile(step, GLA_KEY), _row_tile(step, GLA_VAL),
                  _row_tile(step, GLA_RANK), VMEM_FULL, VMEM_FULL],
        out_specs=[_row_tile(step, GLA_VAL), pl.BlockSpec((GLA_STEP_CHUNKS, GLA_DV, GLA_KEY), lambda i: (i, 0, 0))],
        out_shape=[jax.ShapeDtypeStruct((t, GLA_VAL), F32), jax.ShapeDtypeStruct((n, GLA_DV, GLA_KEY), F32)],
        scratch_shapes=[pltpu.VMEM((GLA_DV, GLA_KEY), F32)],
        args=(q, k, v, alow, wup, bup), carry=carry)


def _gla_bwd(q, k, v, alow, wup, bup, ssave, do, carry=None):
    t = q.shape[0]
    c = GLA_CHUNK
    n = t // c

    def body(q_ref, k_ref, v_ref, al_ref, wup_ref, bup_ref, ss_ref, do_ref,
             dq_ref, dk_ref, dv_ref, dal_ref, dwup_ref, dbup_ref, ds_ref):
        i = pl.program_id(0)

        @pl.when(i == 0)
        def _():
            ds_ref[...] = jnp.zeros_like(ds_ref)
            dwup_ref[...] = jnp.zeros_like(dwup_ref)
            dbup_ref[...] = jnp.zeros_like(dbup_ref)

        wup_v, bup_v = wup_ref[...], bup_ref[...]
        ds_in = ds_ref[...]
        dwup = jnp.zeros((GLA_RANK, GLA_KEY), F32)
        dbup = jnp.zeros((1, GLA_KEY), F32)
        for j in reversed(range(GLA_STEP_CHUNKS)):
            tok = slice(j * c, (j + 1) * c)
            alow_v = al_ref[tok, :]
            m = _gla_common(q_ref[tok, :], k_ref[tok, :], alow_v, wup_v, bup_v)
            s = ss_ref[j]
            sb = s.astype(BF16)
            dsb = ds_in.astype(BF16)
            qt, kt, ke = m["qt"], m["kt"], m["ke"]
            ktb = kt.astype(BF16)
            dqt = jnp.zeros((c, GLA_KEY), F32)
            dkt = jnp.zeros((c, GLA_KEY), F32)
            dke = jnp.zeros((c, GLA_KEY), F32)
            update = jnp.zeros_like(ds_in)
            for h in range(GLA_HEADS):
                mask = m["masks"][h]
                qm = (qt * mask).astype(BF16)
                km = (kt * mask).astype(BF16)
                kem = (ke * mask).astype(BF16)
                cols = slice(h * GLA_DV, (h + 1) * GLA_DV)
                vh = v_ref[tok, cols].astype(BF16)
                doh = do_ref[tok, cols].astype(BF16)
                p = (m["tril"] * _nt(qm, ktb)).astype(BF16)
                dp = (m["tril"] * _nt(doh, vh)).astype(BF16)
                dv_ref[tok, cols] = (_tn(p, doh) + _nt(kem, dsb)).astype(BF16)
                dqt = dqt + _nn(dp, km) + _nn(doh, sb) * mask
                dkt = dkt + _tn(dp, qm)
                dke = dke + _nn(vh, dsb) * mask
                update = update + _tn(doh, qm)
            ddec = jnp.sum(ds_in * s, axis=0, keepdims=True)
            dq_ref[tok, :] = (dqt * m["e_pos"] * (GLA_DK ** -0.5)).astype(BF16)
            dk_ref[tok, :] = (dkt * m["e_neg"] + dke * m["e_end"]).astype(BF16)
            dkeke = dke * ke
            dbl = jnp.sum(dkeke, axis=0, keepdims=True) + ddec * m["dec"]
            last = (lax.broadcasted_iota(jnp.int32, (c, 1), 0) == c - 1).astype(F32)
            dla = _cumsum_rows(dqt * qt - dkt * kt - dkeke + last * dbl, reverse=True)
            dpre = dla * (1.0 / GLA_TAU) * jax.nn.sigmoid(-m["pre"])
            dpb = dpre.astype(BF16)
            dal_ref[tok, :] = _nt(dpb, wup_v.astype(BF16)).astype(BF16)
            dwup = dwup + _tn(alow_v.astype(BF16), dpb)
            dbup = dbup + jnp.sum(dpre, axis=0, keepdims=True)
            ds_in = m["dec"] * ds_in + update
        ds_ref[...] = ds_in
        dwup_ref[...] += dwup
        dbup_ref[...] += dbup

    step = GLA_STEP_CHUNKS * c
    nsteps = t // step

    def rev(d):
        return pl.BlockSpec((step, d), lambda i: (nsteps - 1 - i, 0))

    return _call(
        body, name="gla_bwd", grid=(nsteps,),
        in_specs=[rev(GLA_KEY), rev(GLA_KEY), rev(GLA_VAL), rev(GLA_RANK), VMEM_FULL, VMEM_FULL,
                  pl.BlockSpec((GLA_STEP_CHUNKS, GLA_DV, GLA_KEY), lambda i: (nsteps - 1 - i, 0, 0)), rev(GLA_VAL)],
        out_specs=[rev(GLA_KEY), rev(GLA_KEY), rev(GLA_VAL), rev(GLA_RANK),
                   pl.BlockSpec((GLA_RANK, GLA_KEY), lambda i: (0, 0)), _acc_row(GLA_KEY)],
        out_shape=[jax.ShapeDtypeStruct((t, GLA_KEY), BF16), jax.ShapeDtypeStruct((t, GLA_KEY), BF16),
                   jax.ShapeDtypeStruct((t, GLA_VAL), BF16), jax.ShapeDtypeStruct((t, GLA_RANK), BF16),
                   jax.ShapeDtypeStruct((GLA_RANK, GLA_KEY), F32), jax.ShapeDtypeStruct((1, GLA_KEY), F32)],
        scratch_shapes=[pltpu.VMEM((GLA_DV, GLA_KEY), F32)],
        args=(q, k, v, alow, wup, bup, ssave, do), carry=carry)


def _post_narrow(y, o, r, wg, bg, gn):
    y2 = y * y
    th = jnp.tanh(GELU_C0 * (y + GELU_C1 * y * y2))
    z5 = 0.5 * y * (1.0 + th)
    z5b = z5.astype(BF16)
    gate = jax.nn.sigmoid(_nn(z5b, wg) + bg)
    ys5 = z5 * gate
    rs, on = [], []
    for h in range(GLA_HEADS):
        oh = o[:, h * GLA_DV:(h + 1) * GLA_DV]
        rh = lax.rsqrt(jnp.mean(oh * oh, axis=-1, keepdims=True) + EPS)
        rs.append(rh)
        on.append(oh * rh)
    on = jnp.concatenate(on, axis=-1)
    sr = jax.nn.sigmoid(r)
    silu_r = r * sr
    ygla = on * gn * silu_r
    return dict(y2=y2, th=th, z5=z5, z5b=z5b, gate=gate, ys5b=ys5.astype(BF16), yglab=ygla.astype(BF16), rs=rs,
                on=on, sr=sr, silu_r=silu_r)


def _post_math(y, o, r, gs5, ggla, wg, bg, gn, ps5t, pglat):
    m = _post_narrow(y, o, r, wg, bg, gn)
    m5 = _nt(m["ys5b"], ps5t)
    mg = _nt(m["yglab"], pglat)
    s5g, glag = jax.nn.sigmoid(gs5), jax.nn.sigmoid(ggla)
    merged = s5g * m5 + glag * mg
    m.update(m5=m5, mg=mg, s5g=s5g, glag=glag, mergedb=merged.astype(BF16))
    return m


def _mix_post_fwd(y, o, r, gs5, ggla, h1, wg, bg, gn, ps5t, pglat, wout, carry=None):
    t = o.shape[0]
    tm = _tile(t)

    def body(y_ref, o_ref, r_ref, gs5_ref, ggla_ref, h1_ref, wg_ref, bg_ref, gn_ref, ps_ref, pg_ref, wo_ref, h2_ref):
        m = _post_math(y_ref[...], o_ref[...], r_ref[...], gs5_ref[...], ggla_ref[...],
                       wg_ref[...], bg_ref[...], gn_ref[...], ps_ref[...], pg_ref[...])
        h2_ref[...] = h1_ref[...] + _nn(m["mergedb"], wo_ref[...])

    (h2,), landed = _call(
        body, name="mix_post_fwd", grid=(t // tm,),
        in_specs=[_row_tile(tm, 512)] * 3 + [_row_tile(tm, D_MODEL)] * 3
        + [VMEM_FULL, _acc_row(512), _acc_row(512), VMEM_FULL, VMEM_FULL, VMEM_FULL],
        out_specs=[_row_tile(tm, D_MODEL)],
        out_shape=[jax.ShapeDtypeStruct((t, D_MODEL), F32)],
        args=(y, o, r, gs5, ggla, h1, wg, bg, gn, ps5t, pglat, wout), carry=carry)
    return h2, landed


def _mix_post_bwd(y, o, r, gs5, ggla, dh2, wg, bg, gn, ps5t, pglat, wout, carry=None):
    t = o.shape[0]
    tm = _tile(t) // 2

    def body(y_ref, o_ref, r_ref, gs5_ref, ggla_ref, dh2_ref, wg_ref, bg_ref, gn_ref, ps_ref, pg_ref, wo_ref,
             dy_ref, do_ref, dr_ref, dgs5_ref, dggla_ref, dbg_ref, dgn_ref,
             z5b_ref, dgp_ref, ys5b_ref, dm5b_ref, yglab_ref, dmgb_ref, mergedb_ref, dh2b_ref):
        i = pl.program_id(0)
        yv, ov, rv = y_ref[...], o_ref[...], r_ref[...]
        wg, gn = wg_ref[...], gn_ref[...]
        m = _post_narrow(yv, ov, rv, wg, bg_ref[...], gn)
        dh2b = dh2_ref[...].astype(BF16)
        dys5 = jnp.zeros((tm, S5_WIDTH), F32)
        dygla = jnp.zeros((tm, GLA_VAL), F32)
        for half in range(2):
            cols = slice(half * (D_MODEL // 2), (half + 1) * (D_MODEL // 2))
            ps_h, pg_h = ps_ref[cols, :], pg_ref[cols, :]
            dmerged = _nt(dh2b, wo_ref[cols, :])
            m5 = _nt(m["ys5b"], ps_h)
            mg = _nt(m["yglab"], pg_h)
            s5g, glag = jax.nn.sigmoid(gs5_ref[:, cols]), jax.nn.sigmoid(ggla_ref[:, cols])
            dgs5_ref[:, cols] = (dmerged * m5 * s5g * (1.0 - s5g)).astype(BF16)
            dggla_ref[:, cols] = (dmerged * mg * glag * (1.0 - glag)).astype(BF16)
            dm5b = (dmerged * s5g).astype(BF16)
            dmgb = (dmerged * glag).astype(BF16)
            dm5b_ref[:, cols] = dm5b
            dmgb_ref[:, cols] = dmgb
            mergedb_ref[:, cols] = (s5g * m5 + glag * mg).astype(BF16)
            dys5 = dys5 + _nn(dm5b, ps_h)
            dygla = dygla + _nn(dmgb, pg_h)
        gate, z5, th = m["gate"], m["z5"], m["th"]
        dgpre = dys5 * z5 * gate * (1.0 - gate)
        dgpb = dgpre.astype(BF16)
        dz5 = dys5 * gate + _nt(dgpb, wg)
        dgelu = 0.5 * (1.0 + th) + 0.5 * yv * (1.0 - th * th) * GELU_C0 * (1.0 + 3.0 * GELU_C1 * m["y2"])
        dy_ref[...] = dz5 * dgelu
        on, sr, silu_r = m["on"], m["sr"], m["silu_r"]
        dr_ref[...] = (dygla * on * gn * sr * (1.0 + rv * (1.0 - sr))).astype(BF16)
        dgn = jnp.sum(dygla * on * silu_r, axis=0, keepdims=True)
        don = dygla * gn * silu_r
        for h in range(GLA_HEADS):
            cols = slice(h * GLA_DV, (h + 1) * GLA_DV)
            donh, onh = don[:, cols], on[:, cols]
            do_ref[:, cols] = (m["rs"][h] * (donh - onh * jnp.mean(donh * onh, axis=-1, keepdims=True))).astype(BF16)

        @pl.when(i == 0)
        def _():
            dbg_ref[...] = jnp.zeros_like(dbg_ref)
            dgn_ref[...] = jnp.zeros_like(dgn_ref)

        dbg_ref[...] += jnp.sum(dgpre, axis=0, keepdims=True)
        dgn_ref[...] += dgn
        z5b_ref[...] = m["z5b"]
        dgp_ref[...] = dgpb
        ys5b_ref[...] = m["ys5b"]
        yglab_ref[...] = m["yglab"]
        dh2b_ref[...] = dh2b

    def f32(d):
        return jax.ShapeDtypeStruct((t, d), F32)

    def b16(d):
        return jax.ShapeDtypeStruct((t, d), BF16)

    widths = (512, 512, 512, 1024, 512, 1024, 1024, 1024)
    return _call(
        body, name="mix_post_bwd", grid=(t // tm,),
        in_specs=[_row_tile(tm, 512)] * 3 + [_row_tile(tm, D_MODEL)] * 3
        + [VMEM_FULL, _acc_row(512), _acc_row(512), VMEM_FULL, VMEM_FULL, VMEM_FULL],
        out_specs=[_row_tile(tm, 512)] * 3 + [_row_tile(tm, D_MODEL)] * 2
        + [_acc_row(512)] * 2 + [_row_tile(tm, w) for w in widths],
        out_shape=[f32(512), b16(512), b16(512), b16(D_MODEL), b16(D_MODEL)]
        + [jax.ShapeDtypeStruct((1, 512), F32)] * 2
        + [b16(w) for w in widths],
        args=(y, o, r, gs5, ggla, dh2, wg, bg, gn, ps5t, pglat, wout), carry=carry)


ADAM_TILE_ELEMS = 256 * 1024


def _adamw(w, g, m, v, name):
    rows, cols = w.shape
    tr = rows
    while tr * cols > ADAM_TILE_ELEMS and tr % 16 == 0:
        tr //= 2

    spec = pl.BlockSpec((tr, cols), lambda i: (i, 0))
    sh = jax.ShapeDtypeStruct((rows, cols), F32)
    return pl.pallas_call(functools.partial(_adamw_body), name=name, grid=(rows // tr,), in_specs=[spec] * 4,
                          out_specs=[spec] * 3, out_shape=[sh] * 3, compiler_params=_cparams(1))(w, g, m, v)


def _adamw_math(w, g, m, v):
    nm = ADAM_B1 * m + (1.0 - ADAM_B1) * g
    nv = ADAM_B2 * v + (1.0 - ADAM_B2) * (g * g)
    m_hat = nm / (1.0 - ADAM_B1 ** ADAM_STEP)
    v_hat = nv / (1.0 - ADAM_B2 ** ADAM_STEP)
    return -ADAM_LR * (m_hat / (jnp.sqrt(v_hat) + ADAM_EPS) + ADAM_WD * w), nm, nv


def _adamw_body(w_ref, g_ref, m_ref, v_ref, d_ref, nm_ref, nv_ref):
    d_ref[...], nm_ref[...], nv_ref[...] = _adamw_math(w_ref[...], g_ref[...], m_ref[...], v_ref[...])


SUM_ADAM_ROWS = 32


def _sum_adamw(landed, ws, ms, vs, name, carry=None):
    k = len(ws)
    n = landed[0].shape[0]
    r, c = ws[0].shape
    tr = SUM_ADAM_ROWS

    def body(*refs):
        lands, (w_refs, m_refs, v_refs), outs = refs[:k], (refs[k:2 * k], refs[2 * k:3 * k], refs[3 * k:4 * k]), refs[4 * k:]
        for i in range(k):
            g = lands[i][0].astype(F32)
            for s in range(1, n):
                g = g + lands[i][s].astype(F32)
            outs[i][...] = g
            outs[k + i][...], outs[2 * k + i][...], outs[3 * k + i][...] = _adamw_math(
                w_refs[i][...], g, m_refs[i][...], v_refs[i][...])

    row = pl.BlockSpec((tr, c), lambda i: (i, 0))
    return _call(
        body, name=name, grid=(r // tr,),
        in_specs=[pl.BlockSpec((n, tr, c), lambda i: (0, i, 0))] * k + [row] * (3 * k),
        out_specs=[row] * (4 * k), out_shape=[jax.ShapeDtypeStruct((r, c), F32)] * (4 * k),
        args=(*landed, *ws, *ms, *vs), carry=carry)


def _adamw_many(ws, gs, ms, vs, name):
    n = len(ws)

    def body(*refs):
        ins, outs = refs[:4 * n], refs[4 * n:]
        for i in range(n):
            _adamw_body(*(ins[j * n + i] for j in range(4)), *(outs[j * n + i] for j in range(3)))

    shapes = [jax.ShapeDtypeStruct(w.shape, F32) for w in ws]
    res = pl.pallas_call(body, name=name, in_specs=[VMEM_FULL] * (4 * n), out_specs=[VMEM_FULL] * (3 * n),
                         out_shape=shapes * 3)(*ws, *gs, *ms, *vs)
    return res[:n], res[n:2 * n], res[2 * n:]


def _exchange(carry, name):
    return _call(None, name=name, grid=(1,), in_specs=[], out_specs=[], out_shape=[], args=(), carry=carry)[1]


def _pair_add(slabs, from_pair, name):
    _, r, cols = slabs.shape

    def body(s_ref, p_ref, o_ref):
        c = lax.axis_index("c")
        mine = jnp.where(c == 0, s_ref[0, 0].astype(F32), s_ref[0, 1].astype(F32))
        o_ref[0] = (mine + p_ref[0].astype(F32)).astype(BF16)

    return pl.pallas_call(
        body, name=name, grid=(4,),
        in_specs=[pl.BlockSpec((1, 2, r, cols), lambda i: (i, 0, 0, 0)), pl.BlockSpec((1, r, cols), lambda i: (i, 0, 0))],
        out_specs=pl.BlockSpec((1, r, cols), lambda i: (i, 0, 0)),
        out_shape=jax.ShapeDtypeStruct((4, r, cols), BF16),
        compiler_params=_cparams(1),
    )(slabs.reshape(4, 2, r, cols), from_pair)


def _sum_slabs(slabs, name):
    n = slabs.shape[0]

    def body(s_ref, o_ref):
        acc = s_ref[0].astype(F32)
        for s in range(1, n):
            acc = acc + s_ref[s].astype(F32)
        o_ref[...] = acc

    return pl.pallas_call(
        body, name=name, in_specs=[VMEM_FULL], out_specs=VMEM_FULL,
        out_shape=jax.ShapeDtypeStruct(slabs.shape[1:], F32),
        compiler_params=pltpu.CompilerParams(vmem_limit_bytes=VMEM_LIMIT_BYTES),
    )(slabs)


BIG = ("ffn1_w1", "ffn1_w3", "ffn1_w2", "w_in", "s5_glu_w", "gla_a_up_w", "proj_s5", "proj_gla", "w_out",
       "ffn2_w1", "ffn2_w3", "ffn2_w2")
GROUPS = (("ffn1_w1", "ffn1_w3", "ffn1_w2"),
          ("w_in", "s5_glu_w", "gla_a_up_w", "proj_s5", "proj_gla", "w_out"),
          ("ffn2_w1", "ffn2_w3", "ffn2_w2"))
W_IN_ROWS = 514
W_IN_PAD = 528
UP_COLS = 32
ROW_ADAM = ("ffn1_w1", "ffn1_w3", "w_in", "ffn2_w1", "ffn2_w3")
COL_SHARDED = ("ffn1_w1", "ffn1_w3", "w_in", "proj_s5", "proj_gla", "ffn2_w1", "ffn2_w3")

SMALL = ("ffn1_norm", "mix_norm", "s5_lambda_re", "s5_lambda_im", "s5_log_dt", "s5_b_re", "s5_b_im", "s5_c_re",
         "s5_c_im", "s5_d", "s5_glu_b", "gla_a_up_b", "gla_out_norm", "ffn2_norm", "final_norm")
SMALL_SHAPES = dict(ffn1_norm=(1, 1024), mix_norm=(1, 1024), s5_lambda_re=(1, 32, 64), s5_lambda_im=(1, 32, 64),
                    s5_log_dt=(1, 32), s5_b_re=(1, 32, 64, 16), s5_b_im=(1, 32, 64, 16), s5_c_re=(1, 32, 16, 64),
                    s5_c_im=(1, 32, 16, 64), s5_d=(1, 32, 16), s5_glu_b=(1, 512), gla_a_up_b=(1, 256),
                    gla_out_norm=(1, 512), ffn2_norm=(1, 1024), final_norm=(1024,))
SMALL_N = sum(math.prod(s) for s in SMALL_SHAPES.values())
SMALL_R = -(-SMALL_N // (64 * 1024)) * 64


def _shard_rows(name, a):
    if name == "gla_a_up_w":
        return jnp.pad(a, ((0, 0), (0, 128 - UP_COLS)))
    if name in COL_SHARDED:
        a = a.T
    if name == "w_in":
        return jnp.pad(a, ((0, W_IN_PAD - W_IN_ROWS), (0, 0)))
    return a.reshape(-1, 1024)


def _unshard_rows(name, rows, shape):
    if name == "gla_a_up_w":
        return rows[:, :UP_COLS]
    if name == "w_in":
        rows = rows[:W_IN_ROWS]
    if name in COL_SHARDED:
        return rows.reshape(shape[1], shape[0]).T
    return rows.reshape(shape)


def _pack_small(vals, loss):
    flat = jnp.concatenate([vals[n].reshape(-1).astype(F32) for n in SMALL] + [loss.reshape(1)])
    return jnp.pad(flat, (0, SMALL_R * 1024 - SMALL_N - 1)).reshape(SMALL_R, 1024)


S5_B = ("s5_b_re", "s5_b_im")


def _working(name, a):
    return a[0].transpose(0, 2, 1) if name in S5_B else a


def _declared(name, a):
    return a.transpose(0, 2, 1)[None] if name in S5_B else a.reshape(SMALL_SHAPES[name])


def _unpack_small(slab):
    flat = slab.reshape(-1)
    out, off = {}, 0
    for n in SMALL:
        size = math.prod(SMALL_SHAPES[n])
        shape = (S5_GROUPS, S5_GROUP, S5_STATE) if n in S5_B else SMALL_SHAPES[n]
        out[n] = flat[off:off + size].reshape(shape)
        off += size
    return out


FULL_SHAPES = dict(w_in=(IN_COLS, D_MODEL), s5_glu_w=(S5_WIDTH, S5_WIDTH), gla_a_up_w=(GLA_RANK, GLA_KEY),
                   proj_s5=(D_MODEL, S5_WIDTH), proj_gla=(D_MODEL, GLA_VAL), w_out=(D_MODEL, D_MODEL))


def _full_weight(name, gathered):
    if name == "gla_a_up_w":
        return gathered[:, :, :UP_COLS].transpose(1, 0, 2).reshape(GLA_RANK, GLA_KEY)
    if name == "w_in":
        gathered = gathered[:, :W_IN_ROWS]
    return gathered.reshape(FULL_SHAPES.get(name, (D_FF, D_MODEL)))


def _grad_slabs(name, g):
    if name == "gla_a_up_w":
        g = g.reshape(GLA_RANK, N_DEV, UP_COLS).transpose(1, 0, 2)
        return jnp.pad(g, ((0, 0), (0, 0), (0, 128 - UP_COLS))).astype(BF16)
    if name == "w_in":
        return jnp.pad(g.reshape(N_DEV, W_IN_ROWS, D_MODEL), ((0, 0), (0, W_IN_PAD - W_IN_ROWS), (0, 0)))
    return g.reshape(N_DEV, -1, 1024)


def _s5_dense(re, im, sign_im):
    eye = jnp.eye(8, dtype=F32)

    def one(a):
        a = a.reshape(S5_BLOCKS, 8, S5_GROUP, S5_STATE)
        return jnp.einsum("cghp,gk->cghkp", a, eye).reshape(S5_BLOCKS, 128, S5_BSTATE)

    return jnp.concatenate([one(re), sign_im * one(im)], axis=-1)


def _s5_undense(d):
    eye = jnp.eye(8, dtype=F32)

    def one(a):
        a = a.reshape(S5_BLOCKS, 8, S5_GROUP, 8, S5_STATE)
        return jnp.einsum("cghkp,gk->cghp", a, eye).reshape(S5_GROUPS, S5_GROUP, S5_STATE)

    return one(d[..., :S5_BSTATE]), one(d[..., S5_BSTATE:])


def _local_step(x, target, p, w, rows=None, opt=None):
    w = dict(w or {})
    landed_grads = {}

    def gather(names):
        return None if rows is None else ([rows[n] for n in names], "gather")

    def gathered(names, landed):
        w.update({n: _full_weight(n, g) for n, g in zip(names, landed)})

    def scatter(names):
        return None if rows is None else ([_grad_slabs(n, big[n]) for n in names], "scatter")

    def scattered(names, landed):
        landed_grads.update(zip(names, landed))

    if rows is not None:
        gathered(GROUPS[0], _exchange(gather(GROUPS[0]), "gather_ffn1"))
    g1, gm, g2 = p["ffn1_norm"], p["mix_norm"], p["ffn2_norm"]
    gf = p["final_norm"].reshape(1, D_MODEL)
    lre, lim = p["s5_lambda_re"][0], p["s5_lambda_im"][0]
    ldt = p["s5_log_dt"][0].reshape(S5_GROUPS, 1)
    bre = p["s5_b_re"][0].transpose(0, 2, 1)
    bim = p["s5_b_im"][0].transpose(0, 2, 1)
    cre, cim = p["s5_c_re"][0], p["s5_c_im"][0]
    dskip = p["s5_d"][0].reshape(1, S5_WIDTH)
    bg, bup, gn = p["s5_glu_b"], p["gla_a_up_b"], p["gla_out_norm"]

    mix_first, mix_rest = ("w_in", "gla_a_up_w"), ("s5_glu_w", "proj_s5", "proj_gla", "w_out")
    h1, (a3_1, b3_1, n1), got = _ffn_fwd(x, g1, w["ffn1_w1"], w["ffn1_w3"], w["ffn1_w2"], "ffn1_fwd",
                                         gather(mix_first + mix_rest))
    gathered(mix_first + mix_rest, got)
    wup = w["gla_a_up_w"].astype(F32)
    (u, s5in, q, k, v, r, alow, gs5, ggla), _ = _mix_pre_fwd(h1, gm, w["w_in"])
    ar, ai, bbr, bbi = _s5_disc(lre, lim, ldt, bre, bim)
    bd = _s5_dense(bbr, bbi, 1.0)
    cd = _s5_dense(cre, cim, -1.0)
    bd16, cd16 = bd.astype(BF16), cd.astype(BF16)
    bdt16, ctd16 = bd16.transpose(0, 2, 1), cd16.transpose(0, 2, 1)
    ar4 = ar.reshape(S5_BLOCKS, 1, S5_BSTATE)
    ai4 = ai.reshape(S5_BLOCKS, 1, S5_BSTATE)
    (xs, y), got = _s5_fwd(s5in, bd16, ctd16, ar4, ai4, dskip, gather(GROUPS[2][:2]))
    gathered(GROUPS[2][:2], got)
    (o, ssave), _ = _gla_fwd(q, k, v, alow, wup, bup)
    post_w = (w["s5_glu_w"], bg, gn, w["proj_s5"], w["proj_gla"], w["w_out"])
    h2, got = _mix_post_fwd(y, o, r, gs5, ggla, h1, *post_w, carry=gather(GROUPS[2][2:]))
    gathered(GROUPS[2][2:], got)
    loss, dh3, dgf, (a3_2, b3_2, n2) = _ffn_fwd_head(h2, g2, w["ffn2_w1"], w["ffn2_w3"], w["ffn2_w2"], gf, target,
                                                     "ffn2_fwd")

    big, small = {}, {}
    small["final_norm"] = dgf.reshape(D_MODEL)
    (dh2, dg2, da3, db3, s3, dhh2), _ = _ffn_bwd(
        h2, dh3, g2, a3_2, b3_2, w["ffn2_w1"], w["ffn2_w3"], w["ffn2_w2"], "ffn2_bwd")
    small["ffn2_norm"] = dg2
    big["ffn2_w1"] = _mm_tn(da3, n2, "ffn2_dw1")
    big["ffn2_w3"] = _mm_tn(db3, n2, "ffn2_dw3")
    big["ffn2_w2"] = _mm_tn(s3, dhh2, "ffn2_dw2")
    (dy, do, dr, dgs5, dggla, dbg, dgn, z5b, dgpb, ys5b, dm5b, yglab, dmgb, mergedb, dh2b), got = _mix_post_bwd(
        y, o, r, gs5, ggla, dh2, *post_w, carry=scatter(GROUPS[2][:1]))
    scattered(GROUPS[2][:1], got)
    small["s5_glu_b"] = dbg
    small["gla_out_norm"] = dgn
    big["s5_glu_w"] = _mm_tn(z5b, dgpb, "glu_dw")
    big["proj_s5"] = _mm_tn(dm5b, ys5b, "proj_s5_dw")
    big["proj_gla"] = _mm_tn(dmgb, yglab, "proj_gla_dw")
    big["w_out"] = _mm_tn(mergedb, dh2b, "w_out_dw")
    (dq, dk, dv, dalow, dwup, dbup), got = _gla_bwd(q, k, v, alow, wup, bup, ssave, do, scatter(GROUPS[2][1:2]))
    scattered(GROUPS[2][1:2], got)
    big["gla_a_up_w"] = dwup
    small["gla_a_up_b"] = dbup
    (ds5in, dbd, dcd, dd, dar4, dai4), got = _s5_bwd(
        dy, s5in, xs, cd16, bdt16, ar4, ai4, dskip, scatter(GROUPS[2][2:] + mix_rest[3:]))
    scattered(GROUPS[2][2:] + mix_rest[3:], got)
    dbbr, dbbi = _s5_undense(dbd)
    dcre, dcim_neg = _s5_undense(dcd)
    glre, glim, gldt, gbre, gbim = _s5_disc_bwd(
        lre, lim, ldt, bre, bim, dar4.reshape(S5_GROUPS, S5_STATE), dai4.reshape(S5_GROUPS, S5_STATE),
        dbbr, dbbi)
    small["s5_lambda_re"] = glre[None]
    small["s5_lambda_im"] = glim[None]
    small["s5_log_dt"] = gldt.reshape(1, S5_GROUPS)
    small["s5_b_re"] = gbre
    small["s5_b_im"] = gbim
    small["s5_c_re"] = dcre[None]
    small["s5_c_im"] = -dcim_neg[None]
    small["s5_d"] = dd.reshape(1, S5_GROUPS, S5_GROUP)
    dz = (ds5in, dq, dk, dv, dr, dalow, dgs5, dggla)
    (dh1, dgm), got = _mix_pre_bwd(h1, gm, w["w_in"], dh2, dz, scatter(mix_rest[:3]))
    scattered(mix_rest[:3], got)
    small["mix_norm"] = dgm
    wide = _mm_tn_many(dz[:5] + dz[6:], u, "w_in_dw")
    low_at = IN_OFFS[5]
    big["w_in"] = jnp.concatenate([wide[:low_at], _mm_tn(dalow, u, "w_in_dw_low"), wide[low_at:]], axis=0)
    (dx, dg1, da3, db3, s3, dhh1), got = _ffn_bwd(
        x, dh1, g1, a3_1, b3_1, w["ffn1_w1"], w["ffn1_w3"], w["ffn1_w2"], "ffn1_bwd",
        scatter(mix_first))
    scattered(mix_first, got)
    small["ffn1_norm"] = dg1
    if rows is None:
        big["ffn1_w1"] = _mm_tn(da3, n1, "ffn1_dw1")
        big["ffn1_w3"] = _mm_tn(db3, n1, "ffn1_dw3")
        big["ffn1_w2"] = _mm_tn(s3, dhh1, "ffn1_dw2")
        return loss[0, 0], dx, big, small
    part = _pack_small(small, loss).reshape(N_DEV, SMALL_R // N_DEV, 1024)
    big["ffn1_w1"], (small_landed,) = _mm_tn(da3, n1, "ffn1_dw1", ([part], "scatter"))
    small_mine = _sum_slabs(small_landed, "sum_small")
    slabs1 = _grad_slabs("ffn1_w1", big["ffn1_w1"])
    big["ffn1_w3"], (from_pair, small_all) = _mm_tn(db3, n1, "ffn1_dw3",
                                                    [([slabs1], "pair"), ([small_mine], "gather")])
    small = small_all.reshape(SMALL_R, 1024)
    sums1 = _pair_add(slabs1, from_pair, "ffn1_w1_pair")
    slabs3 = _grad_slabs("ffn1_w3", big["ffn1_w3"])
    big["ffn1_w2"], (landed1, from_pair) = _mm_tn(s3, dhh1, "ffn1_dw2", [([sums1], "chips"), ([slabs3], "pair")])
    sums3 = _pair_add(slabs3, from_pair, "ffn1_w3_pair")
    slabs2 = _grad_slabs("ffn1_w2", big["ffn1_w2"])

    def sum_adamw(names, lands, name, carry=None):
        outs, got = _sum_adamw(lands, *([opt[n][j] for n in names] for j in range(3)), name, carry)
        for i, n in enumerate(names):
            updated[n] = outs[i::len(names)]
        return got

    updated = {}
    landed3, from_pair = sum_adamw(GROUPS[2], [landed_grads.pop(n) for n in GROUPS[2]], "adamw_ffn2",
                                   [([sums3], "chips"), ([slabs2], "pair")])
    sums2 = _pair_add(slabs2, from_pair, "ffn1_w2_pair")
    (landed2,) = _exchange(([sums2], "chips"), "scatter_ffn1_b")
    sum_adamw(GROUPS[0], [landed1, landed3, landed2], "adamw_ffn1")
    return loss[0, 0], dx, landed_grads, small, updated


NAMES = ("ffn1_norm", "ffn1_w1", "ffn1_w3", "ffn1_w2", "mix_norm", "w_in", "s5_lambda_re", "s5_lambda_im",
         "s5_log_dt", "s5_b_re", "s5_b_im", "s5_c_re", "s5_c_im", "s5_d", "s5_glu_w", "s5_glu_b", "gla_a_up_w",
         "gla_a_up_b", "gla_out_norm", "proj_s5", "proj_gla", "w_out", "ffn2_norm", "ffn2_w1", "ffn2_w3", "ffn2_w2",
         "final_norm")


def kernel(*args):
    nw = len(NAMES)
    x = args[0][0]
    wts = dict(zip(NAMES, args[1:1 + nw]))
    target = args[1 + nw][0]
    mom = dict(zip(NAMES, args[2 + nw:2 + 2 * nw]))
    var = dict(zip(NAMES, args[2 + 2 * nw:2 + 3 * nw]))

    shards = {n: wts[n][0] for n in BIG}
    rows = {n: _shard_rows(n, shards[n]).astype(BF16) for n in BIG}
    def row_layout(n, a):
        return a.T if n in ROW_ADAM else a

    opt = {n: tuple(row_layout(n, d[n][0]) for d in (wts, mom, var)) for n in GROUPS[0] + GROUPS[2]}
    _, dx, landed, small_slab, updated = _local_step(x, target, {n: wts[n] for n in SMALL}, None, rows, opt)
    loss = small_slab.reshape(-1)[SMALL_N]
    g_small = _unpack_small(small_slab)

    grad, delta, new_m, new_v = {}, {}, {}, {}
    for n, arrays in updated.items():
        grad[n], delta[n], new_m[n], new_v[n] = (row_layout(n, a)[None] for a in arrays)
    for n in GROUPS[1]:
        g_rows = _sum_slabs(landed[n], "sum_" + n)
        if n in ROW_ADAM:
            g = g_rows[:W_IN_ROWS] if n == "w_in" else g_rows
            outs = _adamw(shards[n].T, g, mom[n][0].T, var[n][0].T, "adamw_" + n)
            grad[n], delta[n], new_m[n], new_v[n] = (a.T[None] for a in (g, *outs))
        else:
            g = _unshard_rows(n, g_rows, shards[n].shape)
            outs = _adamw(shards[n], g, mom[n][0], var[n][0], "adamw_" + n)
            grad[n], delta[n], new_m[n], new_v[n] = (a[None] for a in (g, *outs))

    def flat2d(a):
        return a.reshape(-1, a.shape[-1])

    operands = ([flat2d(_working(n, d[n])) for n in SMALL] for d in (wts, mom, var))
    w2d, m2d, v2d = operands
    outs = _adamw_many(w2d, [flat2d(g_small[n]) for n in SMALL], m2d, v2d, "adamw_small")
    for out, arrays in zip((grad, delta, new_m, new_v), ([g_small[n] for n in SMALL], *outs)):
        out.update({n: _declared(n, a.reshape(g_small[n].shape)) for n, a in zip(SMALL, arrays)})
    return (loss, dx[None], *(d[n] for d in (grad, delta, new_m, new_v) for n in NAMES))
```
